```python
import math
import jax
import jax.numpy as jnp
from jax import lax
import numpy as np


D_MODEL = 2048
BATCH = 8
SEQ = 2048
DEPTH = 2

HEAD_DIM = 64
N_HEADS_A = 12
DILATED_CONFIGS = ((128, 1), (512, 4), (2048, 16))
DILATED_BLOCK = 64
N_HEADS_B = 10
N_KV_B = 2
WINDOW_B = 128
BLOCK_B = 128
N_HEADS_C = 10
GRID_W = 64
NA_ROWS = 8
NA_COLS = 16
WIDTH_A = N_HEADS_A * HEAD_DIM
WIDTH_B = N_HEADS_B * HEAD_DIM
WIDTH_B_KV = N_KV_B * HEAD_DIM
WIDTH_C = N_HEADS_C * HEAD_DIM
MIX_WIDTH = WIDTH_A + WIDTH_B + WIDTH_C
IN_COLS = 3 * WIDTH_A + WIDTH_B + 2 * WIDTH_B_KV + 3 * WIDTH_C
D_FF = 5632
CONV_WIDTH = 3
ROPE_THETA = 10000.0
EPS = 1e-6
NEG_INF = -1e30

kernel_name = 'hybrid_parallel_local_attention_encoder'


def rms_normalize(x):
    x32 = x.astype(jnp.float32)
    return (x32 * lax.rsqrt(jnp.mean(x32 * x32, axis=-1, keepdims=True) + EPS)).astype(x.dtype)


def rmsnorm(x, gain):
    return rms_normalize(x) * gain


def heads(x, n):
    b, t, _ = x.shape
    return x.reshape(b, t, n, HEAD_DIM).transpose(0, 2, 1, 3)


def merge_heads(x):
    b, h, t, d = x.shape
    return x.transpose(0, 2, 1, 3).reshape(b, t, h * d)


def rotary(x):
    t, dh = x.shape[-2], x.shape[-1]
    inv_freq = ROPE_THETA ** (-jnp.arange(0, dh, 2, dtype=jnp.float32) / dh)
    ang = jnp.arange(t, dtype=jnp.float32)[:, None] * inv_freq[None, :]
    cos = jnp.cos(ang).astype(x.dtype)
    sin = jnp.sin(ang).astype(x.dtype)
    x1, x2 = x[..., : dh // 2], x[..., dh // 2:]
    return jnp.concatenate([x1 * cos - x2 * sin, x2 * cos + x1 * sin], axis=-1)


def banded_attention(q, k, v, half, block):
    n, g, length, dh = q.shape
    nb = length // block
    span = block + 2 * half
    kp = jnp.pad(k, ((0, 0), (half, half), (0, 0)))
    vp = jnp.pad(v, ((0, 0), (half, half), (0, 0)))
    idx = jnp.arange(nb)[:, None] * block + jnp.arange(span)[None, :]
    kb = kp[:, idx]
    vb = vp[:, idx]
    qb = q.reshape(n, g, nb, block, dh)
    s = jnp.einsum('ngbqd,nbkd->ngbqk', qb, kb).astype(jnp.float32) * (dh ** -0.5)
    qpos = jnp.arange(nb)[:, None] * block + jnp.arange(block)[None, :]
    kpos = idx - half
    valid = ((jnp.abs(qpos[:, :, None] - kpos[:, None, :]) <= half)
             & (kpos >= 0)[:, None, :] & (kpos < length)[:, None, :])
    s = jnp.where(valid, s, NEG_INF)
    lse = jax.nn.logsumexp(s, axis=-1)
    p = jnp.exp(s - lse[..., None]).astype(v.dtype)
    o = jnp.einsum('ngbqk,nbkd->ngbqd', p, vb)
    return o.reshape(n, g, length, dh), lse.reshape(n, g, length)


def to_strided(x, r):
    b, h, t, d = x.shape
    return x.reshape(b, h, t // r, r, d).transpose(0, 1, 3, 2, 4).reshape(b * h * r, t // r, d)


def from_strided(x, b, h, r):
    length = x.shape[1]
    rest = x.shape[2:]
    y = x.reshape((b, h, r, length) + rest)
    y = jnp.moveaxis(y, 2, 3)
    return y.reshape((b, h, length * r) + rest)


def dilated_attention(q, k, v):
    b, h, t, dh = q.shape
    outs, lses = [], []
    for window, r in DILATED_CONFIGS:
        length = t // r
        half = window // (2 * r)
        block = math.gcd(length, DILATED_BLOCK)
        o, lse = banded_attention(to_strided(q, r)[:, None], to_strided(k, r), to_strided(v, r), half, block)
        outs.append(from_strided(o[:, 0], b, h, r))
        lses.append(from_strided(lse[:, 0], b, h, r))
    weights = jax.nn.softmax(jnp.stack(lses), axis=0)
    return jnp.einsum('cbht,cbhtd->bhtd', weights.astype(q.dtype), jnp.stack(outs))


def sink_window_attention(q, k, v, sink):
    b, hq, t, dh = q.shape
    g = hq // N_KV_B
    o, lse = banded_attention(q.reshape(b * N_KV_B, g, t, dh), k.reshape(b * N_KV_B, t, dh),
                              v.reshape(b * N_KV_B, t, dh), WINDOW_B, BLOCK_B)
    sink_g = jnp.tile(sink.astype(jnp.float32).reshape(N_KV_B, g), (b, 1))[:, :, None]
    keep = jnp.exp(lse - jnp.logaddexp(lse, sink_g))
    return (o * keep[..., None].astype(o.dtype)).reshape(b, hq, t, dh)


def neighborhood_attention(q, k, v, rpb):
    b, h, t, dh = q.shape
    rows = t // GRID_W
    kr = min(NA_ROWS, rows)
    kc = NA_COLS
    r = jnp.arange(rows)
    c = jnp.arange(GRID_W)
    row_start = jnp.clip(r - kr // 2, 0, rows - kr)
    ridx = row_start[:, None] + jnp.arange(kr)[None, :]
    col_start = jnp.clip(c - kc // 2, 0, GRID_W - kc)
    col_mask = (c[None, :] >= col_start[:, None]) & (c[None, :] < col_start[:, None] + kc)
    qg = q.reshape(b, h, rows, GRID_W, dh)
    kg = k.reshape(b, h, rows, GRID_W, dh)[:, :, ridx]
    vg = v.reshape(b, h, rows, GRID_W, dh)[:, :, ridx]
    s = jnp.einsum('bhrcd,bhrkwd->bhrckw', qg, kg).astype(jnp.float32) * (dh ** -0.5)
    roff = (ridx - r[:, None]) + (NA_ROWS - 1)
    coff = jnp.clip(c[None, :] - c[:, None] + (kc - 1), 0, 2 * kc - 2)
    bias = rpb[:, roff[:, None, :, None], coff[None, :, None, :]]
    s = jnp.where(col_mask[:, None, :], s + bias.astype(jnp.float32)[None], NEG_INF)
    p = jax.nn.softmax(s.reshape(b, h, rows, GRID_W, kr * GRID_W), axis=-1)
    p = p.reshape(s.shape).astype(v.dtype)
    o = jnp.einsum('bhrckw,bhrkwd->bhrcd', p, vg)
    return o.reshape(b, h, t, dh)


def split_projection(proj):
    sizes = (WIDTH_A,) * 3 + (WIDTH_B, WIDTH_B_KV, WIDTH_B_KV) + (WIDTH_C,) * 3
    offsets = np.cumsum(sizes)[:-1].tolist()
    return jnp.split(proj, offsets, axis=-1)


def depthwise_conv(u, w, bias):
    t = u.shape[1]
    pad = CONV_WIDTH // 2
    up = jnp.pad(u, ((0, 0), (pad, pad), (0, 0)))
    return sum(up[:, j:j + t] * w[j] for j in range(CONV_WIDTH)) + bias


def _fwd_setup_inputs(seed: int = 0) -> dict:
    key = jax.random.key(seed)
    ks = jax.random.split(key, 13)
    nrm = jax.random.normal
    f32 = jnp.float32
    return {
        'x': nrm(ks[0], (BATCH, SEQ, D_MODEL), f32),
        'ln_attn': 1.0 + 0.02 * nrm(ks[1], (DEPTH, D_MODEL), f32),
        'w_in': nrm(ks[2], (DEPTH, D_MODEL, IN_COLS), f32) * D_MODEL ** -0.5,
        'sink_b': 0.5 * nrm(ks[3], (DEPTH, N_HEADS_B), f32),
        'rpb_c': 0.1 * nrm(ks[4], (DEPTH, N_HEADS_C, 2 * NA_ROWS - 1, 2 * NA_COLS - 1), f32),
        'mix_gain': 1.0 + 0.02 * nrm(ks[5], (DEPTH, MIX_WIDTH), f32),
        'w_out': nrm(ks[6], (DEPTH, MIX_WIDTH, D_MODEL), f32) * MIX_WIDTH ** -0.5,
        'ln_ffn': 1.0 + 0.02 * nrm(ks[7], (DEPTH, D_MODEL), f32),
        'w_up': nrm(ks[8], (DEPTH, D_MODEL, 2 * D_FF), f32) * D_MODEL ** -0.5,
        'conv_w': nrm(ks[9], (DEPTH, CONV_WIDTH, 2 * D_FF), f32) * CONV_WIDTH ** -0.5,
        'conv_b': 0.01 * nrm(ks[10], (DEPTH, 2 * D_FF), f32),
        'w_down': nrm(ks[11], (DEPTH, D_FF, D_MODEL), f32) * D_FF ** -0.5,
        'ln_final': 1.0 + 0.02 * nrm(ks[12], (D_MODEL,), f32),
    }


def _fwd_reference(x, ln_attn, w_in, sink_b, rpb_c, mix_gain, w_out, ln_ffn, w_up, conv_w, conv_b, w_down, ln_final):
    for l in range(DEPTH):
        h = rmsnorm(x, ln_attn[l])
        qa, ka, va, qb, kb, vb, qc, kc, vc = split_projection(h @ w_in[l])
        oa = dilated_attention(rotary(heads(qa, N_HEADS_A)), rotary(heads(ka, N_HEADS_A)), heads(va, N_HEADS_A))
        ob = sink_window_attention(rotary(heads(qb, N_HEADS_B)), rotary(heads(kb, N_KV_B)), heads(vb, N_KV_B), sink_b[l])
        oc = neighborhood_attention(heads(qc, N_HEADS_C), heads(kc, N_HEADS_C), heads(vc, N_HEADS_C), rpb_c[l])
        mixed = jnp.concatenate([rms_normalize(merge_heads(oa)), rms_normalize(merge_heads(ob)),
                                 rms_normalize(merge_heads(oc))], axis=-1) * mix_gain[l]
        x = x + mixed @ w_out[l]
        h = rmsnorm(x, ln_ffn[l])
        u = depthwise_conv(h @ w_up[l], conv_w[l], conv_b[l])
        gate, val = jnp.split(u, 2, axis=-1)
        x = x + (jax.nn.silu(gate) * val) @ w_down[l]
    return rmsnorm(x, ln_final)


import jax as _jax
import jax.numpy as _jnp

TWIN_FORMAT = 'train_step'
FWD_PARAMS = ['x', 'ln_attn', 'w_in', 'sink_b', 'rpb_c', 'mix_gain', 'w_out', 'ln_ffn', 'w_up', 'conv_w', 'conv_b', 'w_down', 'ln_final']
TWIN_WEIGHTS = ['ln_attn', 'w_in', 'sink_b', 'rpb_c', 'mix_gain', 'w_out', 'ln_ffn', 'w_up', 'conv_w', 'conv_b', 'w_down', 'ln_final']
TWIN_DIFF_INPUT = 'x'
TWIN_INPUTS = ['x', 'ln_attn', 'w_in', 'sink_b', 'rpb_c', 'mix_gain', 'w_out', 'ln_ffn', 'w_up', 'conv_w', 'conv_b', 'w_down', 'ln_final', 'loss_target', 'm_ln_attn', 'm_w_in', 'm_sink_b', 'm_rpb_c', 'm_mix_gain', 'm_w_out', 'm_ln_ffn', 'm_w_up', 'm_conv_w', 'm_conv_b', 'm_w_down', 'm_ln_final', 'v_ln_attn', 'v_w_in', 'v_sink_b', 'v_rpb_c', 'v_mix_gain', 'v_w_out', 'v_ln_ffn', 'v_w_up', 'v_conv_w', 'v_conv_b', 'v_w_down', 'v_ln_final']
TWIN_OUTPUTS = ['loss', 'grad_x', 'grad_ln_attn', 'grad_w_in', 'grad_sink_b', 'grad_rpb_c', 'grad_mix_gain', 'grad_w_out', 'grad_ln_ffn', 'grad_w_up', 'grad_conv_w', 'grad_conv_b', 'grad_w_down', 'grad_ln_final', 'delta_ln_attn', 'delta_w_in', 'delta_sink_b', 'delta_rpb_c', 'delta_mix_gain', 'delta_w_out', 'delta_ln_ffn', 'delta_w_up', 'delta_conv_w', 'delta_conv_b', 'delta_w_down', 'delta_ln_final', 'new_m_ln_attn', 'new_m_w_in', 'new_m_sink_b', 'new_m_rpb_c', 'new_m_mix_gain', 'new_m_w_out', 'new_m_ln_ffn', 'new_m_w_up', 'new_m_conv_w', 'new_m_conv_b', 'new_m_w_down', 'new_m_ln_final', 'new_v_ln_attn', 'new_v_w_in', 'new_v_sink_b', 'new_v_rpb_c', 'new_v_mix_gain', 'new_v_w_out', 'new_v_ln_ffn', 'new_v_w_up', 'new_v_conv_w', 'new_v_conv_b', 'new_v_w_down', 'new_v_ln_final']
TWIN_LEAF_KINDS = {'loss': 'loss', 'grad_x': 'grad_x', 'grad_ln_attn': 'grad_w', 'grad_w_in': 'grad_w', 'grad_sink_b': 'grad_w', 'grad_rpb_c': 'grad_w', 'grad_mix_gain': 'grad_w', 'grad_w_out': 'grad_w', 'grad_ln_ffn': 'grad_w', 'grad_w_up': 'grad_w', 'grad_conv_w': 'grad_w', 'grad_conv_b': 'grad_w', 'grad_w_down': 'grad_w', 'grad_ln_final': 'grad_w', 'delta_ln_attn': 'delta_w', 'delta_w_in': 'delta_w', 'delta_sink_b': 'delta_w', 'delta_rpb_c': 'delta_w', 'delta_mix_gain': 'delta_w', 'delta_w_out': 'delta_w', 'delta_ln_ffn': 'delta_w', 'delta_w_up': 'delta_w', 'delta_conv_w': 'delta_w', 'delta_conv_b': 'delta_w', 'delta_w_down': 'delta_w', 'delta_ln_final': 'delta_w', 'new_m_ln_attn': 'new_m', 'new_m_w_in': 'new_m', 'new_m_sink_b': 'new_m', 'new_m_rpb_c': 'new_m', 'new_m_mix_gain': 'new_m', 'new_m_w_out': 'new_m', 'new_m_ln_ffn': 'new_m', 'new_m_w_up': 'new_m', 'new_m_conv_w': 'new_m', 'new_m_conv_b': 'new_m', 'new_m_w_down': 'new_m', 'new_m_ln_final': 'new_m', 'new_v_ln_attn': 'new_v', 'new_v_w_in': 'new_v', 'new_v_sink_b': 'new_v', 'new_v_rpb_c': 'new_v', 'new_v_mix_gain': 'new_v', 'new_v_w_out': 'new_v', 'new_v_ln_ffn': 'new_v', 'new_v_w_up': 'new_v', 'new_v_conv_w': 'new_v', 'new_v_conv_b': 'new_v', 'new_v_w_down': 'new_v', 'new_v_ln_final': 'new_v'}


def _forward(args):
    return _fwd_reference(*[args[k] for k in FWD_PARAMS])


def _output_shape():
    out = _jax.eval_shape(lambda: _forward(_fwd_setup_inputs(0)))
    return out.shape, out.dtype

N_MICROBATCH = 1
ADAM_LR = 0.001
ADAM_B1 = 0.9
ADAM_B2 = 0.999
ADAM_EPS = 1e-08
ADAM_WD = 0.01
ADAM_STEP = 10
PER_EXAMPLE_BATCH_AXIS = {'x': 0, 'loss_target': 0}
SHARED_INPUTS = []
_WEIGHT_DTYPES = {'ln_attn': _jnp.float32, 'w_in': _jnp.float32, 'sink_b': _jnp.float32, 'rpb_c': _jnp.float32, 'mix_gain': _jnp.float32, 'w_out': _jnp.float32, 'ln_ffn': _jnp.float32, 'w_up': _jnp.float32, 'conv_w': _jnp.float32, 'conv_b': _jnp.float32, 'w_down': _jnp.float32, 'ln_final': _jnp.float32}
MOMENT_SCALE = {'ln_attn': 7.137175e-02, 'w_in': 4.449005e-02, 'sink_b': 2.611535e-03, 'rpb_c': 1.295753e-02, 'mix_gain': 4.971651e-02, 'w_out': 4.973040e-02, 'ln_ffn': 3.252174e-02, 'w_up': 1.385919e-02, 'conv_w': 1.399622e-02, 'conv_b': 1.655154e-02, 'w_down': 2.276245e-02, 'ln_final': 7.950437e+00}


def _to_microbatches(a, axis):
    t = _jnp.moveaxis(a, axis, 0)
    t = t.reshape((N_MICROBATCH, t.shape[0] // N_MICROBATCH) + t.shape[1:])
    return _jnp.moveaxis(t, 1, axis + 1)


def setup_inputs(seed: int = 0) -> dict:
    inp = _fwd_setup_inputs(seed)
    key = _jax.random.fold_in(_jax.random.key(seed), 7919)
    shape, _ = _output_shape()
    out = dict(inp)
    out["loss_target"] = _jax.random.normal(_jax.random.fold_in(key, 0), shape, _jnp.float32)
    for i, name in enumerate(TWIN_WEIGHTS):
        w = inp[name].astype(_jnp.float32)
        if MOMENT_SCALE is None:
            s = _jnp.sqrt(_jnp.mean(_jnp.square(w)) + 1e-30)
        else:
            s = MOMENT_SCALE[name]
        km, kv = _jax.random.split(_jax.random.fold_in(key, i + 1))
        out[name] = w
        out["m_" + name] = s * _jax.random.normal(km, w.shape, _jnp.float32)
        out["v_" + name] = (s * s) * _jax.random.uniform(kv, w.shape, _jnp.float32, 0.5, 1.5)
    if N_MICROBATCH > 1:
        for name, axis in PER_EXAMPLE_BATCH_AXIS.items():
            out[name] = _to_microbatches(out[name], axis)
    return {'x': out['x'], 'ln_attn': out['ln_attn'], 'w_in': out['w_in'], 'sink_b': out['sink_b'], 'rpb_c': out['rpb_c'], 'mix_gain': out['mix_gain'], 'w_out': out['w_out'], 'ln_ffn': out['ln_ffn'], 'w_up': out['w_up'], 'conv_w': out['conv_w'], 'conv_b': out['conv_b'], 'w_down': out['w_down'], 'ln_final': out['ln_final'], 'loss_target': out['loss_target'], 'm_ln_attn': out['m_ln_attn'], 'm_w_in': out['m_w_in'], 'm_sink_b': out['m_sink_b'], 'm_rpb_c': out['m_rpb_c'], 'm_mix_gain': out['m_mix_gain'], 'm_w_out': out['m_w_out'], 'm_ln_ffn': out['m_ln_ffn'], 'm_w_up': out['m_w_up'], 'm_conv_w': out['m_conv_w'], 'm_conv_b': out['m_conv_b'], 'm_w_down': out['m_w_down'], 'm_ln_final': out['m_ln_final'], 'v_ln_attn': out['v_ln_attn'], 'v_w_in': out['v_w_in'], 'v_sink_b': out['v_sink_b'], 'v_rpb_c': out['v_rpb_c'], 'v_mix_gain': out['v_mix_gain'], 'v_w_out': out['v_w_out'], 'v_ln_ffn': out['v_ln_ffn'], 'v_w_up': out['v_w_up'], 'v_conv_w': out['v_conv_w'], 'v_conv_b': out['v_conv_b'], 'v_w_down': out['v_w_down'], 'v_ln_final': out['v_ln_final']}


def _loss(weights, diff, rest, loss_target):
    with _jax.named_scope("forward"):
        args = {**rest, TWIN_DIFF_INPUT: diff, **{k: w.astype(_WEIGHT_DTYPES[k]) for k, w in weights.items()}}
        y = _forward(args)
    with _jax.named_scope("loss_head"):
        err = _jnp.square(y.astype(_jnp.float32) - loss_target)
        return 0.5 * _jnp.sum(_jnp.mean(err, axis=-1)) if err.ndim else 0.5 * err


def _adamw(w, g, m, v):
    m = ADAM_B1 * m + (1.0 - ADAM_B1) * g
    v = ADAM_B2 * v + (1.0 - ADAM_B2) * _jnp.square(g)
    m_hat = m / (1.0 - ADAM_B1 ** ADAM_STEP)
    v_hat = v / (1.0 - ADAM_B2 ** ADAM_STEP)
    delta = -ADAM_LR * (m_hat / (_jnp.sqrt(v_hat) + ADAM_EPS) + ADAM_WD * w)
    return delta, m, v


def reference(x, ln_attn, w_in, sink_b, rpb_c, mix_gain, w_out, ln_ffn, w_up, conv_w, conv_b, w_down, ln_final, loss_target, m_ln_attn, m_w_in, m_sink_b, m_rpb_c, m_mix_gain, m_w_out, m_ln_ffn, m_w_up, m_conv_w, m_conv_b, m_w_down, m_ln_final, v_ln_attn, v_w_in, v_sink_b, v_rpb_c, v_mix_gain, v_w_out, v_ln_ffn, v_w_up, v_conv_w, v_conv_b, v_w_down, v_ln_final):
    given = dict(x=x, ln_attn=ln_attn, w_in=w_in, sink_b=sink_b, rpb_c=rpb_c, mix_gain=mix_gain, w_out=w_out, ln_ffn=ln_ffn, w_up=w_up, conv_w=conv_w, conv_b=conv_b, w_down=w_down, ln_final=ln_final, loss_target=loss_target, m_ln_attn=m_ln_attn, m_w_in=m_w_in, m_sink_b=m_sink_b, m_rpb_c=m_rpb_c, m_mix_gain=m_mix_gain, m_w_out=m_w_out, m_ln_ffn=m_ln_ffn, m_w_up=m_w_up, m_conv_w=m_conv_w, m_conv_b=m_conv_b, m_w_down=m_w_down, m_ln_final=m_ln_final, v_ln_attn=v_ln_attn, v_w_in=v_w_in, v_sink_b=v_sink_b, v_rpb_c=v_rpb_c, v_mix_gain=v_mix_gain, v_w_out=v_w_out, v_ln_ffn=v_ln_ffn, v_w_up=v_w_up, v_conv_w=v_conv_w, v_conv_b=v_conv_b, v_w_down=v_w_down, v_ln_final=v_ln_final)
    weights = {n: given[n] for n in TWIN_WEIGHTS}
    shared = {n: given[n] for n in SHARED_INPUTS}
    per_example = {n: given[n] for n in ['x']}
    grad_fn = _jax.value_and_grad(_loss, argnums=(0, 1))

    def one_microbatch(ex, loss_target):
        ex = dict(ex)
        diff = ex.pop(TWIN_DIFF_INPUT)
        return grad_fn(weights, diff, {**shared, **ex}, loss_target)

    if N_MICROBATCH == 1:
        loss, (grad_w, grad_x) = one_microbatch(per_example, given["loss_target"])
    else:
        def body(carry, xs):
            loss_sum, grad_sum = carry
            l_k, (gw_k, gx_k) = one_microbatch(xs[0], xs[1])
            with _jax.named_scope("update"):
                return (loss_sum + l_k, _jax.tree.map(_jnp.add, grad_sum, gw_k)), gx_k

        init = (_jnp.zeros((), _jnp.float32), _jax.tree.map(_jnp.zeros_like, weights))
        (loss, grad_w), grad_x = _jax.lax.scan(body, init, (per_example, given["loss_target"]))
    with _jax.named_scope("update"):
        delta_w, new_m, new_v = {}, {}, {}
        for n in TWIN_WEIGHTS:
            delta_w[n], new_m[n], new_v[n] = _adamw(weights[n], grad_w[n], given["m_" + n], given["v_" + n])
    return (loss, grad_x, *[grad_w[n] for n in TWIN_WEIGHTS], *[delta_w[n] for n in TWIN_WEIGHTS],
            *[new_m[n] for n in TWIN_WEIGHTS], *[new_v[n] for n in TWIN_WEIGHTS])
```

```python
import functools

import jax
import jax.numpy as jnp
from jax import lax
from jax.experimental import pallas as pl
from jax.experimental.pallas import tpu as pltpu

F32 = jnp.float32
BF16 = jnp.bfloat16

N_DEV = 8
T = 2048
D = 2048
DEPTH = 2
HD = 64
HA, HB, HKV, HC = 12, 10, 2, 10
WA, WB, WKV, WC = HA * HD, HB * HD, HKV * HD, HC * HD
IN_COLS = 3 * WA + WB + 2 * WKV + 3 * WC
DFF = 5632
GRID_W = 64
ROWS = T // GRID_W
NA_ROWS, NA_COLS = 8, 16
WINDOW_B = 128
EPS = 1e-6
NEG = -1e30
ROPE_THETA = 10000.0
LANE = 128
VMEM_LIMIT = 56 * 1024 * 1024

ADAM_LR, ADAM_B1, ADAM_B2, ADAM_EPS, ADAM_WD, ADAM_STEP = 0.001, 0.9, 0.999, 1e-08, 0.01, 10

GROUPS = (("qa", WA, True, True), ("ka", WA, True, False), ("va", WA, False, False),
          ("qb", WB, True, True), ("kb", WKV, True, False), ("vb", WKV, False, False),
          ("qc", WC, False, True), ("kc", WC, False, False), ("vc", WC, False, False))


def _params(sem=None):
    return pltpu.CompilerParams(dimension_semantics=sem, vmem_limit_bytes=VMEM_LIMIT)


def _exchange(name, srcs, out_shapes, src_of, dst_of):
    n = len(srcs)

    def body(*refs):
        ins, outs = refs[:n], refs[n:2 * n]
        send_sems, recv_sems, local_sems = refs[2 * n:]
        x, y, c = lax.axis_index("x"), lax.axis_index("y"), lax.axis_index("c")
        me = 4 * x + 2 * y + c
        sends, recvs, locals_ = [], [], []
        for a in range(n):
            keep = pltpu.make_async_copy(src_of(a, ins[a], me), dst_of(a, outs[a], me), local_sems.at[a])
            keep.start()
            locals_.append(keep)
            for p in range(1, N_DEV):
                px, py, pc = (x if not p & 4 else 1 - x), (y if not p & 2 else 1 - y), (c if not p & 1 else 1 - c)
                peer = 4 * px + 2 * py + pc
                send = pltpu.make_async_remote_copy(
                    src_ref=src_of(a, ins[a], peer), dst_ref=dst_of(a, outs[a], me),
                    send_sem=send_sems.at[a, p - 1], recv_sem=recv_sems.at[a, p - 1],
                    device_id=(px, py, pc), device_id_type=pl.DeviceIdType.MESH)
                send.start()
                sends.append(send)
                recvs.append(pltpu.make_async_remote_copy(
                    src_ref=src_of(a, ins[a], peer), dst_ref=dst_of(a, outs[a], peer),
                    send_sem=send_sems.at[a, p - 1], recv_sem=recv_sems.at[a, p - 1],
                    device_id=(px, py, pc), device_id_type=pl.DeviceIdType.MESH))
        for r in recvs:
            r.wait_recv()
        for s in sends:
            s.wait_send()
        for k in locals_:
            k.wait()

    hbm = pl.BlockSpec(memory_space=pl.ANY)
    return pl.pallas_call(
        body, name=name,
        out_shape=[jax.ShapeDtypeStruct(s, d) for s, d in out_shapes],
        in_specs=[hbm] * n, out_specs=[hbm] * n,
        scratch_shapes=[pltpu.SemaphoreType.DMA((n, N_DEV - 1)), pltpu.SemaphoreType.DMA((n, N_DEV - 1)),
                        pltpu.SemaphoreType.DMA((n,))],
    )(*srcs)


def _all_gather(name, shards):
    return _exchange(name, shards, [((N_DEV,) + s.shape, s.dtype) for s in shards],
                     lambda a, ref, peer: ref, lambda a, ref, origin: ref.at[origin])


def _grad_exchange(name, layer_pairs):
    srcs = [g for pair in layer_pairs for g in pair]
    outs = _exchange(name, srcs, [(g.shape, g.dtype) for g in srcs], lambda a, ref, peer: ref.at[peer],
                     lambda a, ref, origin: ref.at[origin])
    return [(outs[2 * i], outs[2 * i + 1]) for i in range(len(layer_pairs))]


def _flat2(v):
    return v.reshape(-1, v.shape[-1])


def _matmul(name, kind, a, a_spec, b, b_spec, out_shape, out_spec, grid, res=None, res_spec=None, acc_shape=None):
    dims = {"nn": (((1,), (0,)), ((), ())), "nt": (((1,), (1,)), ((), ())), "tn": (((0,), (0,)), ((), ()))}[kind]
    nred = grid[-1]

    def body(*refs):
        if res is None:
            a_ref, b_ref, o_ref = refs[:3]
            r_ref = None
        else:
            a_ref, b_ref, r_ref, o_ref = refs[:4]
        part = lax.dot_general(_flat2(a_ref[...]), _flat2(b_ref[...]), dims, preferred_element_type=F32)

        def finish(total):
            if r_ref is not None:
                total = total + r_ref[...]
            o_ref[...] = total.reshape(o_ref.shape).astype(o_ref.dtype)

        if nred == 1:
            finish(part)
        else:
            acc_ref = refs[-1]
            k = pl.program_id(len(grid) - 1)

            @pl.when(k == 0)
            def _():
                acc_ref[...] = part

            @pl.when(k > 0)
            def _():
                acc_ref[...] += part

            @pl.when(k == nred - 1)
            def _():
                finish(acc_ref[...])

    ins, specs = [a, b], [a_spec, b_spec]
    if res is not None:
        ins.append(res)
        specs.append(res_spec)
    scratch = [] if nred == 1 else [pltpu.VMEM(acc_shape, F32)]
    return pl.pallas_call(
        body, name=name, grid=grid, in_specs=specs, out_specs=out_spec, out_shape=out_shape, scratch_shapes=scratch,
        compiler_params=_params(("parallel",) * (len(grid) - 1) + ("arbitrary",)),
    )(*ins)


TM = 512


def _nn_cols(name, a, wg, l, out_dtype=F32):
    _, _, k, nj = wg.shape
    return _matmul(
        name, "nn", a, pl.BlockSpec((TM, k), lambda j, i, r: (i, 0)),
        wg, pl.BlockSpec((None, None, k, nj), lambda j, i, r: (j, l, 0, 0)),
        jax.ShapeDtypeStruct((T, N_DEV * nj), out_dtype), pl.BlockSpec((TM, nj), lambda j, i, r: (i, j)),
        (N_DEV, T // TM, 1))


def _nn_rows(name, a, wg, l, res, s):
    _, _, kj, n = wg.shape
    tn = 512
    return _matmul(
        name, "nn", a, pl.BlockSpec((TM, s * kj), lambda j, i, r: (i, r)),
        wg, pl.BlockSpec((s, None, kj, tn), lambda j, i, r: (r, l, 0, j)),
        jax.ShapeDtypeStruct((T, n), F32), pl.BlockSpec((TM, tn), lambda j, i, r: (i, j)),
        (n // tn, T // TM, N_DEV // s), res=res, res_spec=pl.BlockSpec((TM, tn), lambda j, i, r: (i, j)),
        acc_shape=(TM, tn))


def _nt_cols(name, dc, dc_spec_of, wg, l):
    _, _, k, nj = wg.shape
    tk = 512
    return _matmul(
        name, "nt", dc, dc_spec_of(TM, nj),
        wg, pl.BlockSpec((None, None, tk, nj), lambda kt, i, j: (j, l, kt, 0)),
        jax.ShapeDtypeStruct((T, k), F32), pl.BlockSpec((TM, tk), lambda kt, i, j: (i, kt)),
        (k // tk, T // TM, N_DEV), acc_shape=(TM, tk))


def _nt_rows(name, dc, wg, l, s):
    _, _, kj, n = wg.shape
    return _matmul(
        name, "nt", dc, pl.BlockSpec((TM, n), lambda kt, i, r: (i, 0)),
        wg, pl.BlockSpec((s, None, kj, n), lambda kt, i, r: (kt, l, 0, 0)),
        jax.ShapeDtypeStruct((T, N_DEV * kj), F32), pl.BlockSpec((TM, s * kj), lambda kt, i, r: (i, kt)),
        (N_DEV // s, T // TM, 1))


def _tn_cols(name, a, dc, dc_spec_of, nj):
    k = a.shape[1]
    tk = 512
    return _matmul(
        name, "tn", a, pl.BlockSpec((T, tk), lambda j, kt, r: (0, kt)),
        dc, dc_spec_of(T, nj),
        jax.ShapeDtypeStruct((N_DEV, k, nj), BF16), pl.BlockSpec((None, tk, nj), lambda j, kt, r: (j, kt, 0)),
        (N_DEV, k // tk, 1))


def _tn_rows(name, a, dc, kj, s):
    n = dc.shape[1]
    tn = 512
    return _matmul(
        name, "tn", a, pl.BlockSpec((T, s * kj), lambda kt, j, r: (0, kt)),
        dc, pl.BlockSpec((T, tn), lambda kt, j, r: (0, j)),
        jax.ShapeDtypeStruct((N_DEV, kj, n), BF16), pl.BlockSpec((s, kj, tn), lambda kt, j, r: (kt, 0, j)),
        (N_DEV // s, n // tn, 1))


TR = 256


def _rows(width):
    return pl.BlockSpec((TR, width), lambda i: (i, 0))


def _whole(shape):
    return pl.BlockSpec(shape, lambda i: (0,) * len(shape))


def _rmsnorm_fwd(name, x, g):
    def body(x_ref, g_ref, o_ref):
        xv = x_ref[...]
        r = lax.rsqrt(jnp.mean(xv * xv, axis=-1, keepdims=True) + EPS)
        o_ref[...] = ((xv * r) * g_ref[...]).astype(BF16)

    return pl.pallas_call(
        body, name=name, grid=(T // TR,), in_specs=[_rows(D), _whole((1, D))], out_specs=_rows(D),
        out_shape=jax.ShapeDtypeStruct((T, D), BF16), compiler_params=_params(("parallel",)),
    )(x, g)


def _rms_bwd_math(dy, xv, g):
    r = lax.rsqrt(jnp.mean(xv * xv, axis=-1, keepdims=True) + EPS)
    xhat = xv * r
    dxhat = dy * g
    dx = r * (dxhat - xhat * jnp.mean(dxhat * xhat, axis=-1, keepdims=True))
    return dx, dy * xhat


def _accumulate(ref, val):
    @pl.when(pl.program_id(0) == 0)
    def _():
        ref[...] = val

    @pl.when(pl.program_id(0) > 0)
    def _():
        ref[...] += val


def _rmsnorm_bwd(name, dy, x, g, res):
    def body(dy_ref, x_ref, g_ref, res_ref, dx_ref, dxb_ref, dg_ref):
        dx, dgr = _rms_bwd_math(dy_ref[...], x_ref[...], g_ref[...])
        tot = res_ref[...] + dx
        dx_ref[...] = tot
        dxb_ref[...] = tot.astype(BF16)
        _accumulate(dg_ref, jnp.sum(dgr, axis=0, keepdims=True))

    return pl.pallas_call(
        body, name=name, grid=(T // TR,), in_specs=[_rows(D), _rows(D), _whole((1, D)), _rows(D)],
        out_specs=[_rows(D), _rows(D), _whole((1, D))],
        out_shape=[jax.ShapeDtypeStruct((T, D), F32), jax.ShapeDtypeStruct((T, D), BF16),
                   jax.ShapeDtypeStruct((1, D), F32)],
        compiler_params=_params(("arbitrary",)),
    )(dy, x, g, res)


def _loss_head(x, g, target):
    def body(x_ref, g_ref, t_ref, loss_ref, dx_ref, dxb_ref, dg_ref):
        xv, gv = x_ref[...], g_ref[...]
        r = lax.rsqrt(jnp.mean(xv * xv, axis=-1, keepdims=True) + EPS)
        err = (xv * r) * gv - t_ref[...]
        part = 0.5 * jnp.sum(jnp.mean(err * err, axis=-1, keepdims=True))
        dx, dgr = _rms_bwd_math(err * (1.0 / D), xv, gv)
        dx_ref[...] = dx
        dxb_ref[...] = dx.astype(BF16)
        _accumulate(dg_ref, jnp.sum(dgr, axis=0, keepdims=True))
        _accumulate(loss_ref, jnp.full((8, LANE), part, F32))

    return pl.pallas_call(
        body, name="loss_head", grid=(T // TR,), in_specs=[_rows(D), _whole((1, D)), _rows(D)],
        out_specs=[_whole((8, LANE)), _rows(D), _rows(D), _whole((1, D))],
        out_shape=[jax.ShapeDtypeStruct((8, LANE), F32), jax.ShapeDtypeStruct((T, D), F32),
                   jax.ShapeDtypeStruct((T, D), BF16), jax.ShapeDtypeStruct((1, D), F32)],
        compiler_params=_params(("arbitrary",)),
    )(x, g, target)


MIX_OFFS = ((0, WA), (WA, WB), (WA + WB, WC))


def _mix_fwd(name, oa, ob, oc, gain):
    def body(oa_ref, ob_ref, oc_ref, g_ref, o_ref):
        for ref, (off, w) in zip((oa_ref, ob_ref, oc_ref), MIX_OFFS):
            o = ref[...]
            r = lax.rsqrt(jnp.mean(o * o, axis=-1, keepdims=True) + EPS)
            o_ref[:, off:off + w] = ((o * r) * g_ref[:, off:off + w]).astype(BF16)

    return pl.pallas_call(
        body, name=name, grid=(T // TR,), in_specs=[_rows(WA), _rows(WB), _rows(WC), _whole((1, D))],
        out_specs=_rows(D), out_shape=jax.ShapeDtypeStruct((T, D), BF16), compiler_params=_params(("parallel",)),
    )(oa, ob, oc, gain)


def _mix_bwd(name, dmixed, oa, ob, oc, gain):
    def body(dm_ref, oa_ref, ob_ref, oc_ref, g_ref, doa_ref, dob_ref, doc_ref, dg_ref):
        dgs = []
        for ref, dref, (off, w) in zip((oa_ref, ob_ref, oc_ref), (doa_ref, dob_ref, doc_ref), MIX_OFFS):
            dx, dgr = _rms_bwd_math(dm_ref[:, off:off + w], ref[...], g_ref[:, off:off + w])
            dref[...] = dx
            dgs.append(jnp.sum(dgr, axis=0, keepdims=True))
        _accumulate(dg_ref, jnp.concatenate(dgs, axis=1))

    return pl.pallas_call(
        body, name=name, grid=(T // TR,),
        in_specs=[_rows(D), _rows(WA), _rows(WB), _rows(WC), _whole((1, D))],
        out_specs=[_rows(WA), _rows(WB), _rows(WC), _whole((1, D))],
        out_shape=[jax.ShapeDtypeStruct((T, WA), F32), jax.ShapeDtypeStruct((T, WB), F32),
                   jax.ShapeDtypeStruct((T, WC), F32), jax.ShapeDtypeStruct((1, D), F32)],
        compiler_params=_params(("arbitrary",)),
    )(dmixed, oa, ob, oc, gain)


def _rope_tables():
    inv_freq = ROPE_THETA ** (-jnp.arange(0, HD, 2, dtype=F32) / HD)
    ang = jnp.arange(T, dtype=F32)[:, None] * inv_freq[None, :]
    cos, sin = jnp.cos(ang), jnp.sin(ang)
    cos2 = jnp.tile(jnp.concatenate([cos, cos], axis=1), (1, LANE // HD))
    sin2 = jnp.tile(jnp.concatenate([-sin, sin], axis=1), (1, LANE // HD))
    return cos2, sin2


def _rot_half(v):
    lane = lax.broadcasted_iota(jnp.int32, v.shape, 1)
    return jnp.where(lane % HD < HD // 2, pltpu.roll(v, LANE - HD // 2, 1), pltpu.roll(v, HD // 2, 1))


def _rope_fwd(name, proj, cos2, sin2):
    def body(p_ref, c_ref, s_ref, *outs):
        cv, sv = c_ref[...], s_ref[...]
        off = 0
        for o_ref, (_, w, rot, is_q) in zip(outs, GROUPS):
            for b in range(w // LANE):
                v = p_ref[:, off + b * LANE:off + (b + 1) * LANE]
                if rot:
                    v = v * cv + _rot_half(v) * sv
                if is_q:
                    v = v * (HD ** -0.5)
                o_ref[:, b * LANE:(b + 1) * LANE] = v.astype(BF16)
            off += w

    return pl.pallas_call(
        body, name=name, grid=(T // TR,), in_specs=[_rows(IN_COLS), _rows(LANE), _rows(LANE)],
        out_specs=[_rows(w) for _, w, _, _ in GROUPS],
        out_shape=[jax.ShapeDtypeStruct((T, w), BF16) for _, w, _, _ in GROUPS],
        compiler_params=_params(("parallel",)),
    )(proj, cos2, sin2)


def _rope_bwd(name, grads, cos2, sin2):
    def body(*refs):
        ins, (c_ref, s_ref, o_ref) = refs[:9], refs[9:]
        cv, sv = c_ref[...], s_ref[...]
        off = 0
        for d_ref, (_, w, rot, is_q) in zip(ins, GROUPS):
            for b in range(w // LANE):
                v = d_ref[:, b * LANE:(b + 1) * LANE]
                if is_q:
                    v = v * (HD ** -0.5)
                if rot:
                    v = v * cv + _rot_half(v * sv)
                o_ref[:, off + b * LANE:off + (b + 1) * LANE] = v.astype(BF16)
            off += w

    return pl.pallas_call(
        body, name=name, grid=(T // TR,), in_specs=[_rows(w) for _, w, _, _ in GROUPS] + [_rows(LANE), _rows(LANE)],
        out_specs=_rows(IN_COLS), out_shape=jax.ShapeDtypeStruct((T, IN_COLS), BF16),
        compiler_params=_params(("parallel",)),
    )(*grads, cos2, sin2)


NT_DIMS = (((1,), (1,)), ((), ()))
TN_DIMS = (((0,), (0,)), ((), ()))


def _softmax_parts(q, k, valid, mult, bias, sink):
    s = lax.dot_general(q, k, NT_DIMS, preferred_element_type=F32)
    if bias is not None:
        s = s + bias
    s = jnp.where(valid, s, NEG)
    m = jnp.max(s, axis=1, keepdims=True)
    e = jnp.exp(s - m)
    if mult is not None:
        e = e * mult
    l = jnp.sum(e, axis=1, keepdims=True)
    e_sink = None
    if sink is not None:
        e_sink = jnp.exp(sink - m)
        l = l + e_sink
    return e, l, e_sink


def _head_fwd(q, k, v, valid, mult=None, bias=None, sink=None):
    e, l, _ = _softmax_parts(q, k, valid, mult, bias, sink)
    return jnp.dot(e.astype(BF16), v, preferred_element_type=F32) / l


def _head_bwd(q, k, v, o, do, valid, mult=None, bias=None, sink=None):
    e, l, e_sink = _softmax_parts(q, k, valid, mult, bias, sink)
    p = e / l
    dob = do.astype(BF16)
    dp = lax.dot_general(dob, v, NT_DIMS, preferred_element_type=F32)
    delta = jnp.sum(do * o, axis=1, keepdims=True)
    ds = p * (dp - delta)
    dsb = ds.astype(BF16)
    dq = jnp.dot(dsb, k, preferred_element_type=F32)
    dk = lax.dot_general(dsb, q, TN_DIMS, preferred_element_type=F32)
    dv = lax.dot_general(p.astype(BF16), dob, TN_DIMS, preferred_element_type=F32)
    dsink = None if sink is None else -(e_sink / l) * delta
    return dq, dk, dv, ds, dsink


def _dilation_mask():
    d = jnp.arange(T, dtype=jnp.int32)[:, None] - jnp.arange(T, dtype=jnp.int32)[None, :]
    ad = jnp.abs(d)
    count = jnp.zeros((T, T), jnp.int32)
    for window, r in ((128, 1), (512, 4), (2048, 16)):
        count += ((ad % r == 0) & (ad // r <= window // (2 * r))).astype(jnp.int32)
    return count.astype(BF16)


BQ_A = 256


def _attn_a_fwd(name, qa, ka, va, mask):
    def body(q_ref, k_ref, v_ref, m_ref, o_ref):
        mult = m_ref[...].astype(F32)
        valid = mult > 0.0
        outs = [_head_fwd(q_ref[:, h * HD:(h + 1) * HD], k_ref[:, h * HD:(h + 1) * HD], v_ref[:, h * HD:(h + 1) * HD],
                          valid, mult=mult) for h in range(2)]
        o_ref[...] = jnp.concatenate(outs, axis=1)

    qs = pl.BlockSpec((BQ_A, LANE), lambda p, i: (i, p))
    ks = pl.BlockSpec((T, LANE), lambda p, i: (0, p))
    return pl.pallas_call(
        body, name=name, grid=(HA // 2, T // BQ_A),
        in_specs=[qs, ks, ks, pl.BlockSpec((BQ_A, T), lambda p, i: (i, 0))], out_specs=qs,
        out_shape=jax.ShapeDtypeStruct((T, WA), F32), compiler_params=_params(("parallel", "parallel")),
    )(qa, ka, va, mask)


def _attn_a_bwd(name, qa, ka, va, oa, doa, mask):
    def body(q_ref, k_ref, v_ref, o_ref, do_ref, m_ref, dq_ref, dk_ref, dv_ref):
        mult = m_ref[...].astype(F32)
        valid = mult > 0.0
        dqs, dks, dvs = [], [], []
        for h in range(2):
            sl = slice(h * HD, (h + 1) * HD)
            dq, dk, dv, _, _ = _head_bwd(q_ref[:, sl], k_ref[:, sl], v_ref[:, sl], o_ref[:, sl], do_ref[:, sl],
                                         valid, mult=mult)
            dqs.append(dq)
            dks.append(dk)
            dvs.append(dv)
        dq_ref[...] = jnp.concatenate(dqs, axis=1)
        dk2, dv2 = jnp.concatenate(dks, axis=1), jnp.concatenate(dvs, axis=1)

        @pl.when(pl.program_id(1) == 0)
        def _():
            dk_ref[...] = dk2
            dv_ref[...] = dv2

        @pl.when(pl.program_id(1) > 0)
        def _():
            dk_ref[...] += dk2
            dv_ref[...] += dv2

    qs = pl.BlockSpec((BQ_A, LANE), lambda p, i: (i, p))
    ks = pl.BlockSpec((T, LANE), lambda p, i: (0, p))
    return pl.pallas_call(
        body, name=name, grid=(HA // 2, T // BQ_A),
        in_specs=[qs, ks, ks, qs, qs, pl.BlockSpec((BQ_A, T), lambda p, i: (i, 0))], out_specs=[qs, ks, ks],
        out_shape=[jax.ShapeDtypeStruct((T, WA), F32)] * 3, compiler_params=_params(("parallel", "arbitrary")),
    )(qa, ka, va, oa, doa, mask)


BQ_B = 128
SPAN_B = BQ_B + 2 * WINDOW_B


def _window_b(i):
    start = pl.multiple_of(jnp.clip(i * BQ_B - WINDOW_B, 0, T - SPAN_B), BQ_B)
    qpos = i * BQ_B + lax.broadcasted_iota(jnp.int32, (BQ_B, SPAN_B), 0)
    kpos = start + lax.broadcasted_iota(jnp.int32, (BQ_B, SPAN_B), 1)
    return start, jnp.abs(qpos - kpos) <= WINDOW_B


def _attn_b_fwd(name, qb, kb, vb, sink):
    def body(sink_ref, q_ref, k_ref, v_ref, o_ref):
        start, valid = _window_b(pl.program_id(0))
        kw, vw = k_ref[pl.ds(start, SPAN_B), :], v_ref[pl.ds(start, SPAN_B), :]
        outs = []
        for h in range(HB):
            kv = slice((h // (HB // HKV)) * HD, (h // (HB // HKV) + 1) * HD)
            outs.append(_head_fwd(q_ref[:, h * HD:(h + 1) * HD], kw[:, kv], vw[:, kv], valid, sink=sink_ref[h]))
        o_ref[...] = jnp.concatenate(outs, axis=1)

    return pl.pallas_call(
        body, name=name, grid=(T // BQ_B,),
        in_specs=[pl.BlockSpec(memory_space=pltpu.SMEM), pl.BlockSpec((BQ_B, WB), lambda i: (i, 0)),
                  _whole((T, WKV)), _whole((T, WKV))],
        out_specs=pl.BlockSpec((BQ_B, WB), lambda i: (i, 0)),
        out_shape=jax.ShapeDtypeStruct((T, WB), F32), compiler_params=_params(("parallel",)),
    )(sink, qb, kb, vb)


def _attn_b_bwd(name, qb, kb, vb, ob, dob, sink):
    def body(sink_ref, q_ref, k_ref, v_ref, o_ref, do_ref, dq_ref, dk_ref, dv_ref, dsink_ref):
        i = pl.program_id(0)
        start, valid = _window_b(i)
        kw, vw = k_ref[pl.ds(start, SPAN_B), :], v_ref[pl.ds(start, SPAN_B), :]
        lane = lax.broadcasted_iota(jnp.int32, (1, LANE), 1)
        dsink = jnp.zeros((1, LANE), F32)
        dqs, dks, dvs = [], [], []
        for g in range(HKV):
            kv = slice(g * HD, (g + 1) * HD)
            dk_g = jnp.zeros((SPAN_B, HD), F32)
            dv_g = jnp.zeros((SPAN_B, HD), F32)
            for h in range(g * (HB // HKV), (g + 1) * (HB // HKV)):
                sl = slice(h * HD, (h + 1) * HD)
                dq, dk, dv, _, dsr = _head_bwd(q_ref[:, sl], kw[:, kv], vw[:, kv], o_ref[:, sl], do_ref[:, sl],
                                               valid, sink=sink_ref[h])
                dqs.append(dq)
                dk_g += dk
                dv_g += dv
                dsink += jnp.where(lane == h, jnp.sum(dsr), 0.0)
            dks.append(dk_g)
            dvs.append(dv_g)
        dq_ref[...] = jnp.concatenate(dqs, axis=1)

        @pl.when(i == 0)
        def _():
            dk_ref[...] = jnp.zeros_like(dk_ref)
            dv_ref[...] = jnp.zeros_like(dv_ref)
            dsink_ref[...] = jnp.zeros_like(dsink_ref)

        dk_ref[pl.ds(start, SPAN_B), :] += jnp.concatenate(dks, axis=1)
        dv_ref[pl.ds(start, SPAN_B), :] += jnp.concatenate(dvs, axis=1)
        dsink_ref[...] += dsink

    qs = pl.BlockSpec((BQ_B, WB), lambda i: (i, 0))
    return pl.pallas_call(
        body, name=name, grid=(T // BQ_B,),
        in_specs=[pl.BlockSpec(memory_space=pltpu.SMEM), qs, _whole((T, WKV)), _whole((T, WKV)), qs, qs],
        out_specs=[qs, _whole((T, WKV)), _whole((T, WKV)), _whole((1, LANE))],
        out_shape=[jax.ShapeDtypeStruct((T, WB), F32), jax.ShapeDtypeStruct((T, WKV), F32),
                   jax.ShapeDtypeStruct((T, WKV), F32), jax.ShapeDtypeStruct((1, LANE), F32)],
        compiler_params=_params(("arbitrary",)),
    )(sink, qb, kb, vb, ob, dob)


SPAN_C = NA_ROWS * GRID_W
N_OFF = NA_ROWS


def _row_start(r):
    return jnp.clip(r - NA_ROWS // 2, 0, ROWS - NA_ROWS)


def _off_index(r):
    return _row_start(r) - r + (NA_ROWS - 1)


def _col_valid():
    c = lax.broadcasted_iota(jnp.int32, (GRID_W, SPAN_C), 0)
    kc = lax.broadcasted_iota(jnp.int32, (GRID_W, SPAN_C), 1) % GRID_W
    cs = jnp.clip(c - NA_COLS // 2, 0, GRID_W - NA_COLS)
    return (kc >= cs) & (kc < cs + NA_COLS)


def _expand_rpb(rpb):
    a = jnp.arange(N_OFF)[:, None] + jnp.arange(NA_ROWS)[None, :]
    c = jnp.arange(GRID_W)
    j = jnp.clip(c[None, :] - c[:, None] + (NA_COLS - 1), 0, 2 * NA_COLS - 2)
    e = rpb[:, a[:, None, :, None], j[None, :, None, :]]
    return e.reshape(HC, N_OFF, GRID_W, SPAN_C)


def _attn_c_fwd(name, qc, kc, vc, bias):
    def body(q_ref, k_ref, v_ref, b_ref, o_ref):
        start = pl.multiple_of(_row_start(pl.program_id(1)) * GRID_W, GRID_W)
        valid = _col_valid()
        kw, vw = k_ref[pl.ds(start, SPAN_C), :], v_ref[pl.ds(start, SPAN_C), :]
        outs = [_head_fwd(q_ref[:, h * HD:(h + 1) * HD], kw[:, h * HD:(h + 1) * HD], vw[:, h * HD:(h + 1) * HD],
                          valid, bias=b_ref[h]) for h in range(2)]
        o_ref[...] = jnp.concatenate(outs, axis=1)

    qs = pl.BlockSpec((GRID_W, LANE), lambda p, r: (r, p))
    ks = pl.BlockSpec((T, LANE), lambda p, r: (0, p))
    bs = pl.BlockSpec((2, None, GRID_W, SPAN_C), lambda p, r: (p, _off_index(r), 0, 0))
    return pl.pallas_call(
        body, name=name, grid=(HC // 2, ROWS), in_specs=[qs, ks, ks, bs], out_specs=qs,
        out_shape=jax.ShapeDtypeStruct((T, WC), F32), compiler_params=_params(("parallel", "parallel")),
    )(qc, kc, vc, bias)


def _attn_c_bwd(name, qc, kc, vc, oc, doc, bias):
    def body(q_ref, k_ref, v_ref, o_ref, do_ref, b_ref, dq_ref, dk_ref, dv_ref, db_ref):
        r = pl.program_id(1)
        start = pl.multiple_of(_row_start(r) * GRID_W, GRID_W)
        valid = _col_valid()
        kw, vw = k_ref[pl.ds(start, SPAN_C), :], v_ref[pl.ds(start, SPAN_C), :]
        dqs, dks, dvs, dss = [], [], [], []
        for h in range(2):
            sl = slice(h * HD, (h + 1) * HD)
            dq, dk, dv, ds, _ = _head_bwd(q_ref[:, sl], kw[:, sl], vw[:, sl], o_ref[:, sl], do_ref[:, sl],
                                          valid, bias=b_ref[h])
            dqs.append(dq)
            dks.append(dk)
            dvs.append(dv)
            dss.append(ds)
        dq_ref[...] = jnp.concatenate(dqs, axis=1)

        @pl.when(r == 0)
        def _():
            dk_ref[...] = jnp.zeros_like(dk_ref)
            dv_ref[...] = jnp.zeros_like(dv_ref)

        dk_ref[pl.ds(start, SPAN_C), :] += jnp.concatenate(dks, axis=1)
        dv_ref[pl.ds(start, SPAN_C), :] += jnp.concatenate(dvs, axis=1)

        first = jnp.logical_or(r == 0, _off_index(r) != _off_index(jnp.maximum(r - 1, 0)))

        @pl.when(first)
        def _():
            for h in range(2):
                db_ref[h] = dss[h]

        @pl.when(jnp.logical_not(first))
        def _():
            for h in range(2):
                db_ref[h] += dss[h]

    qs = pl.BlockSpec((GRID_W, LANE), lambda p, r: (r, p))
    ks = pl.BlockSpec((T, LANE), lambda p, r: (0, p))
    bs = pl.BlockSpec((2, None, GRID_W, SPAN_C), lambda p, r: (p, _off_index(r), 0, 0))
    return pl.pallas_call(
        body, name=name, grid=(HC // 2, ROWS), in_specs=[qs, ks, ks, qs, qs, bs], out_specs=[qs, ks, ks, bs],
        out_shape=[jax.ShapeDtypeStruct((T, WC), F32)] * 3 + [jax.ShapeDtypeStruct((HC, N_OFF, GRID_W, SPAN_C), F32)],
        compiler_params=_params(("parallel", "arbitrary")),
    )(qc, kc, vc, oc, doc, bias)


def _split3(v):
    hi = v.astype(BF16)
    r1 = v - hi.astype(F32)
    mid = r1.astype(BF16)
    lo = (r1 - mid.astype(F32)).astype(BF16)
    return hi, mid, lo


def _rpb_reduce(name, dbias):
    x = dbias.reshape(HC, N_OFF, GRID_W, NA_ROWS, GRID_W).transpose(0, 1, 3, 2, 4)
    x = x.reshape(HC, N_OFF * NA_ROWS, GRID_W * GRID_W)
    c = jnp.arange(GRID_W)
    j = jnp.clip(c[None, :] - c[:, None] + (NA_COLS - 1), 0, 2 * NA_COLS - 2).reshape(-1)
    col_onehot = (j[:, None] == jnp.arange(LANE)[None, :]).astype(BF16)
    a = (jnp.arange(N_OFF)[:, None] + jnp.arange(NA_ROWS)[None, :]).reshape(-1)
    row_onehot = (jnp.arange(16)[:, None] == a[None, :]).astype(BF16)

    def body(x_ref, e_ref, f_ref, o_ref):
        y = sum(jnp.dot(part, e_ref[...], preferred_element_type=F32) for part in _split3(x_ref[...]))
        o_ref[...] = sum(jnp.dot(f_ref[...], part, preferred_element_type=F32) for part in _split3(y))

    out = pl.pallas_call(
        body, name=name, grid=(HC,),
        in_specs=[pl.BlockSpec((None, N_OFF * NA_ROWS, GRID_W * GRID_W), lambda h: (h, 0, 0)),
                  _whole((GRID_W * GRID_W, LANE)), _whole((16, N_OFF * NA_ROWS))],
        out_specs=pl.BlockSpec((None, 16, LANE), lambda h: (h, 0, 0)),
        out_shape=jax.ShapeDtypeStruct((HC, 16, LANE), F32), compiler_params=_params(("parallel",)),
    )(x, col_onehot, row_onehot)
    return out[:, :2 * NA_ROWS - 1, :2 * NA_COLS - 1]


TC = 128
NCB = DFF // TC


def _shift_down(v, rows):
    return jnp.where(rows == 0, 0.0, pltpu.roll(v, 1, 0))


def _shift_up(v, rows):
    return jnp.where(rows == T - 1, 0.0, pltpu.roll(v, T - 1, 0))


def _conv(v, w, b, rows):
    return _shift_down(v, rows) * w[0:1] + v * w[1:2] + _shift_up(v, rows) * w[2:3] + b


def _ffn_specs():
    gate = lambda shape: pl.BlockSpec(shape, lambda j: (0, j))
    val = lambda shape: pl.BlockSpec(shape, lambda j: (0, j + NCB))
    return [gate((T, TC)), val((T, TC)), gate((3, TC)), val((3, TC)), gate((1, TC)), val((1, TC))]


def _ffn_mid_fwd(name, up, conv_w, conv_b):
    def body(xg_ref, xv_ref, wg_ref, wv_ref, bg_ref, bv_ref, o_ref):
        rows = lax.broadcasted_iota(jnp.int32, (T, TC), 0)
        ug = _conv(xg_ref[...], wg_ref[...], bg_ref[...], rows)
        uv = _conv(xv_ref[...], wv_ref[...], bv_ref[...], rows)
        o_ref[...] = (ug * jax.nn.sigmoid(ug) * uv).astype(BF16)

    return pl.pallas_call(
        body, name=name, grid=(NCB,), in_specs=_ffn_specs(), out_specs=pl.BlockSpec((T, TC), lambda j: (0, j)),
        out_shape=jax.ShapeDtypeStruct((T, DFF), BF16), compiler_params=_params(("parallel",)),
    )(up, up, conv_w, conv_w, conv_b, conv_b)


def _ffn_mid_bwd(name, dact, up, conv_w, conv_b):
    def body(da_ref, xg_ref, xv_ref, wg_ref, wv_ref, bg_ref, bv_ref, dx_ref, dw_ref, db_ref):
        rows = lax.broadcasted_iota(jnp.int32, (T, TC), 0)
        xg, xv, wg, wv = xg_ref[...], xv_ref[...], wg_ref[...], wv_ref[...]
        ug = _conv(xg, wg, bg_ref[...], rows)
        uv = _conv(xv, wv, bv_ref[...], rows)
        sg = jax.nn.sigmoid(ug)
        da = da_ref[...]
        dug = da * uv * (sg * (1.0 + ug * (1.0 - sg)))
        duv = da * (ug * sg)
        for half, (xin, w, du) in enumerate(((xg, wg, dug), (xv, wv, duv))):
            dx = _shift_up(du, rows) * w[0:1] + du * w[1:2] + _shift_down(du, rows) * w[2:3]
            dx_ref[half] = dx.astype(BF16)
            dw_ref[half] = jnp.concatenate(
                [jnp.sum(_shift_down(xin, rows) * du, axis=0, keepdims=True), jnp.sum(xin * du, axis=0, keepdims=True),
                 jnp.sum(_shift_up(xin, rows) * du, axis=0, keepdims=True)], axis=0)
            db_ref[half] = jnp.sum(du, axis=0, keepdims=True)

    return pl.pallas_call(
        body, name=name, grid=(NCB,), in_specs=[pl.BlockSpec((T, TC), lambda j: (0, j))] + _ffn_specs(),
        out_specs=[pl.BlockSpec((2, T, TC), lambda j: (0, 0, j)), pl.BlockSpec((2, 3, TC), lambda j: (0, 0, j)),
                   pl.BlockSpec((2, 1, TC), lambda j: (0, 0, j))],
        out_shape=[jax.ShapeDtypeStruct((2, T, DFF), BF16), jax.ShapeDtypeStruct((2, 3, DFF), F32),
                   jax.ShapeDtypeStruct((2, 1, DFF), F32)],
        compiler_params=_params(("parallel",)),
    )(dact, up, up, conv_w, conv_w, conv_b, conv_b)


def _dup_spec(tm, nj):
    per = DFF // nj
    return pl.BlockSpec((None, tm, nj), lambda a, b, j: (j // per, 0 if tm == T else b, j % per))


def _dup_spec_tn(tm, nj):
    per = DFF // nj
    return pl.BlockSpec((None, tm, nj), lambda j, kt, r: (j // per, 0, j % per))


def _adamw_math(w, g, m, v):
    m = ADAM_B1 * m + (1.0 - ADAM_B1) * g
    v = ADAM_B2 * v + (1.0 - ADAM_B2) * (g * g)
    m_hat = m / (1.0 - ADAM_B1 ** ADAM_STEP)
    v_hat = v / (1.0 - ADAM_B2 ** ADAM_STEP)
    delta = -ADAM_LR * (m_hat / (jnp.sqrt(v_hat) + ADAM_EPS) + ADAM_WD * w)
    return delta, m, v


def _adamw_sharded(name, w, m, v, parts):
    _, r, c = w.shape
    tr = 64

    def body(w_ref, m_ref, v_ref, p0_ref, p1_ref, g_ref, d_ref, nm_ref, nv_ref):
        def run(p_ref):
            g = p_ref[0].astype(F32)
            for k in range(1, N_DEV):
                g = g + p_ref[k].astype(F32)
            d, nm, nv = _adamw_math(w_ref[...], g, m_ref[...], v_ref[...])
            g_ref[...] = g
            d_ref[...] = d
            nm_ref[...] = nm
            nv_ref[...] = nv

        @pl.when(pl.program_id(0) == 0)
        def _():
            run(p0_ref)

        @pl.when(pl.program_id(0) == 1)
        def _():
            run(p1_ref)

    ws = pl.BlockSpec((None, tr, c), lambda l, i: (l, i, 0))
    p0 = pl.BlockSpec((N_DEV, tr, c), lambda l, i: (0, jnp.where(l == 0, i, r // tr - 1), 0))
    p1 = pl.BlockSpec((N_DEV, tr, c), lambda l, i: (0, jnp.where(l == 1, i, 0), 0))
    return pl.pallas_call(
        body, name=name, grid=(DEPTH, r // tr), in_specs=[ws, ws, ws, p0, p1], out_specs=[ws] * 4,
        out_shape=[jax.ShapeDtypeStruct(w.shape, F32)] * 4, compiler_params=_params(("arbitrary", "arbitrary")),
    )(w, m, v, *parts)


def _sum_devices(name, parts):
    r = parts.shape[1]

    def body(p_ref, o_ref):
        g = p_ref[0]
        for k in range(1, N_DEV):
            g = g + p_ref[k]
        o_ref[...] = g

    return pl.pallas_call(
        body, name=name, in_specs=[pl.BlockSpec((N_DEV, r, LANE), lambda: (0, 0, 0))],
        out_specs=pl.BlockSpec((r, LANE), lambda: (0, 0)), out_shape=jax.ShapeDtypeStruct((r, LANE), F32),
        compiler_params=_params(),
    )(parts)


def _adamw_small(name, w, g, m, v):
    spec = pl.BlockSpec(w.shape, lambda: (0, 0))

    def body(w_ref, g_ref, m_ref, v_ref, d_ref, nm_ref, nv_ref):
        d, nm, nv = _adamw_math(w_ref[...], g_ref[...], m_ref[...], v_ref[...])
        d_ref[...] = d
        nm_ref[...] = nm
        nv_ref[...] = nv

    return pl.pallas_call(
        body, name=name, in_specs=[spec] * 4, out_specs=[spec] * 3,
        out_shape=[jax.ShapeDtypeStruct(w.shape, F32)] * 3, compiler_params=_params(),
    )(w, g, m, v)


def _pack(arrays):
    flat = jnp.concatenate([a.reshape(-1) for a in arrays])
    pad = (-flat.shape[0]) % (8 * LANE)
    return jnp.pad(flat, (0, pad)).reshape(-1, LANE)


def _unpack(buf, shapes):
    flat, out, off = buf.reshape(-1), [], 0
    for s in shapes:
        n = 1
        for d in s:
            n *= d
        out.append(flat[off:off + n].reshape(s))
        off += n
    return out


def _local_step(x, target, small, wg_in, wg_out, wg_up, wg_down, conv_w_full):
    cos2, sin2 = _rope_tables()
    mask_a = _dilation_mask()
    saved = []
    for l in range(DEPTH):
        g1, g2 = small["ln_attn"][l][None], small["ln_ffn"][l][None]
        gain, sink, cb = small["mix_gain"][l][None], small["sink_b"][l], small["conv_b"][l][None]
        cw = conv_w_full[l]
        bias = _expand_rpb(small["rpb_c"][l])
        h1 = _rmsnorm_fwd(f"norm_attn_{l}", x, g1)
        proj = _nn_cols(f"proj_in_{l}", h1, wg_in, l)
        qa, ka, va, qb, kb, vb, qc, kc, vc = _rope_fwd(f"rope_{l}", proj, cos2, sin2)
        oa = _attn_a_fwd(f"attn_a_{l}", qa, ka, va, mask_a)
        ob = _attn_b_fwd(f"attn_b_{l}", qb, kb, vb, sink)
        oc = _attn_c_fwd(f"attn_c_{l}", qc, kc, vc, bias)
        mixed = _mix_fwd(f"mix_{l}", oa, ob, oc, gain)
        x_mid = _nn_rows(f"proj_out_{l}", mixed, wg_out, l, x, 2)
        h2 = _rmsnorm_fwd(f"norm_ffn_{l}", x_mid, g2)
        up = _nn_cols(f"ffn_up_{l}", h2, wg_up, l)
        act = _ffn_mid_fwd(f"ffn_mid_{l}", up, cw, cb)
        x_out = _nn_rows(f"ffn_down_{l}", act, wg_down, l, x_mid, 2)
        saved.append(dict(x=x, h1=h1, qkv=(qa, ka, va, qb, kb, vb, qc, kc, vc), o=(oa, ob, oc), mixed=mixed,
                          x_mid=x_mid, h2=h2, up=up, act=act, g1=g1, g2=g2, gain=gain, sink=sink, cb=cb, cw=cw, bias=bias))
        x = x_out

    loss8, dx, dxb, d_ln_final = _loss_head(x, small["ln_final"][None], target)
    sgrads = [None] * DEPTH
    wgrads = [None] * DEPTH
    for l in reversed(range(DEPTH)):
        s = saved[l]
        qa, ka, va, qb, kb, vb, qc, kc, vc = s["qkv"]
        oa, ob, oc = s["o"]
        g_down = _tn_rows(f"wgrad_down_{l}", s["act"], dxb, wg_down.shape[2], 2)
        dact = _nt_rows(f"dgrad_down_{l}", dxb, wg_down, l, 2)
        dup, d_cw, d_cb = _ffn_mid_bwd(f"ffn_mid_bwd_{l}", dact, s["up"], s["cw"], s["cb"])
        g_up = _tn_cols(f"wgrad_up_{l}", s["h2"], dup, _dup_spec_tn, wg_up.shape[3])
        dh2 = _nt_cols(f"dgrad_up_{l}", dup, _dup_spec, wg_up, l)
        dx, dxb, d_g2 = _rmsnorm_bwd(f"norm_ffn_bwd_{l}", dh2, s["x_mid"], s["g2"], dx)
        g_out = _tn_rows(f"wgrad_out_{l}", s["mixed"], dxb, wg_out.shape[2], 2)
        dmixed = _nt_rows(f"dgrad_out_{l}", dxb, wg_out, l, 2)
        doa, dob, doc, d_gain = _mix_bwd(f"mix_bwd_{l}", dmixed, oa, ob, oc, s["gain"])
        dqa, dka, dva = _attn_a_bwd(f"attn_a_bwd_{l}", qa, ka, va, oa, doa, mask_a)
        dqb, dkb, dvb, d_sink = _attn_b_bwd(f"attn_b_bwd_{l}", qb, kb, vb, ob, dob, s["sink"])
        dqc, dkc, dvc, d_bias = _attn_c_bwd(f"attn_c_bwd_{l}", qc, kc, vc, oc, doc, s["bias"])
        d_rpb = _rpb_reduce(f"rpb_reduce_{l}", d_bias)
        dproj = _rope_bwd(f"rope_bwd_{l}", (dqa, dka, dva, dqb, dkb, dvb, dqc, dkc, dvc), cos2, sin2)
        nj_in = wg_in.shape[3]
        g_in = _tn_cols(f"wgrad_in_{l}", s["h1"], dproj,
                        lambda tm, nj: pl.BlockSpec((tm, nj), lambda j, kt, r: (0, j)), nj_in)
        dh1 = _nt_cols(f"dgrad_in_{l}", dproj, lambda tm, nj: pl.BlockSpec((tm, nj), lambda kt, i, j: (i, j)), wg_in, l)
        dx, dxb, d_g1 = _rmsnorm_bwd(f"norm_attn_bwd_{l}", dh1, s["x"], s["g1"], dx)
        sgrads[l] = dict(ln_attn=d_g1[0], sink_b=d_sink[0, :HB], rpb_c=d_rpb, mix_gain=d_gain[0], ln_ffn=d_g2[0],
                         conv_w=d_cw.transpose(1, 0, 2).reshape(3, 2 * DFF), conv_b=d_cb.reshape(2 * DFF))
        wgrads[l] = dict(w_in=g_in, w_out=g_out, w_up=g_up, w_down=g_down)
    return loss8[0, 0], dx, d_ln_final[0], sgrads, wgrads


SMALL_NAMES = ("ln_attn", "sink_b", "rpb_c", "mix_gain", "ln_ffn", "conv_b")


def kernel(x, ln_attn, w_in, sink_b, rpb_c, mix_gain, w_out, ln_ffn, w_up, conv_w, conv_b, w_down, ln_final, loss_target, m_ln_attn, m_w_in, m_sink_b, m_rpb_c, m_mix_gain, m_w_out, m_ln_ffn, m_w_up, m_conv_w, m_conv_b, m_w_down, m_ln_final, v_ln_attn, v_w_in, v_sink_b, v_rpb_c, v_mix_gain, v_w_out, v_ln_ffn, v_w_up, v_conv_w, v_conv_b, v_w_down, v_ln_final):
    me = 4 * lax.axis_index("x") + 2 * lax.axis_index("y") + lax.axis_index("c")
    small = dict(ln_attn=ln_attn, sink_b=sink_b, rpb_c=rpb_c, mix_gain=mix_gain, ln_ffn=ln_ffn, conv_b=conv_b,
                 ln_final=ln_final)

    wg_in, wg_out, wg_up, wg_down, cw_all = _all_gather(
        "gather_weights", [w_in.astype(BF16), w_out.astype(BF16), w_up.astype(BF16), w_down.astype(BF16),
                           _pack([conv_w])])
    nup = w_up.shape[2]
    cw_shards = cw_all.reshape(N_DEV, -1)[:, :DEPTH * 3 * nup].reshape(N_DEV, DEPTH, 3, nup)
    conv_w_full = cw_shards.transpose(1, 2, 0, 3).reshape(DEPTH, 3, N_DEV * nup)

    loss_local, dx, d_ln_final, sgrads, wgrads = _local_step(
        x[0], loss_target[0], small, wg_in, wg_out, wg_up, wg_down, conv_w_full)

    names = ("w_in", "w_out", "w_up", "w_down")
    parts = _grad_exchange("exchange_grads", [(wgrads[0][n], wgrads[1][n]) for n in names])
    big = {}
    for n, p, (w, m, v) in zip(names, parts, ((w_in, m_w_in, v_w_in), (w_out, m_w_out, v_w_out),
                                              (w_up, m_w_up, v_w_up), (w_down, m_w_down, v_w_down))):
        big[n] = _adamw_sharded(f"adamw_{n}", w, m, v, p)

    stacked = [jnp.stack([sgrads[l][n] for l in range(DEPTH)]) for n in SMALL_NAMES + ("conv_w",)] + [d_ln_final]
    shapes = [a.shape for a in stacked]
    (gathered,) = _all_gather("gather_small_grads", [_pack(stacked)])
    g_small = _unpack(_sum_devices("sum_small_grads", gathered), shapes)
    g = dict(zip(SMALL_NAMES + ("conv_w", "ln_final"), g_small))
    g["conv_w"] = lax.dynamic_slice_in_dim(g["conv_w"], me * nup, nup, axis=2)

    snames = SMALL_NAMES + ("conv_w", "ln_final")
    sw = dict(small, conv_w=conv_w)
    sm = dict(ln_attn=m_ln_attn, sink_b=m_sink_b, rpb_c=m_rpb_c, mix_gain=m_mix_gain, ln_ffn=m_ln_ffn,
              conv_b=m_conv_b, conv_w=m_conv_w, ln_final=m_ln_final)
    sv = dict(ln_attn=v_ln_attn, sink_b=v_sink_b, rpb_c=v_rpb_c, mix_gain=v_mix_gain, ln_ffn=v_ln_ffn,
              conv_b=v_conv_b, conv_w=v_conv_w, ln_final=v_ln_final)
    sshapes = [sw[n].shape for n in snames]
    packed = _adamw_small("adamw_small", _pack([sw[n] for n in snames]), _pack([g[n] for n in snames]),
                          _pack([sm[n] for n in snames]), _pack([sv[n] for n in snames]))
    s_delta, s_m, s_v = (dict(zip(snames, _unpack(buf, sshapes))) for buf in packed)

    loss = lax.psum(loss_local, ("x", "y", "c"))
    order = ("ln_attn", "w_in", "sink_b", "rpb_c", "mix_gain", "w_out", "ln_ffn", "w_up", "conv_w", "conv_b",
             "w_down", "ln_final")
    grads = [big[n][0] if n in big else g[n] for n in order]
    deltas = [big[n][1] if n in big else s_delta[n] for n in order]
    new_m = [big[n][2] if n in big else s_m[n] for n in order]
    new_v = [big[n][3] if n in big else s_v[n] for n in order]
    return (loss, dx[None], *grads, *deltas, *new_m, *new_v)
```

```python
import functools

import jax
import jax.numpy as jnp
from jax import lax
from jax.experimental import pallas as pl
from jax.experimental.pallas import tpu as pltpu

F32 = jnp.float32
BF16 = jnp.bfloat16

N_DEV = 8
T = 2048
D = 2048
DEPTH = 2
HD = 64
HA, HB, HKV, HC = 12, 10, 2, 10
WA, WB, WKV, WC = HA * HD, HB * HD, HKV * HD, HC * HD
IN_COLS = 3 * WA + WB + 2 * WKV + 3 * WC
DFF = 5632
GRID_W = 64
ROWS = T // GRID_W
NA_ROWS, NA_COLS = 8, 16
WINDOW_B = 128
EPS = 1e-6
NEG = -1e30
ROPE_THETA = 10000.0
LANE = 128
VMEM_LIMIT = 56 * 1024 * 1024

ADAM_LR, ADAM_B1, ADAM_B2, ADAM_EPS, ADAM_WD, ADAM_STEP = 0.001, 0.9, 0.999, 1e-08, 0.01, 10

GROUPS = (("qa", WA, True, True), ("ka", WA, True, False), ("va", WA, False, False),
          ("qb", WB, True, True), ("kb", WKV, True, False), ("vb", WKV, False, False),
          ("qc", WC, False, True), ("kc", WC, False, False), ("vc", WC, False, False))


def _params(sem=None):
    return pltpu.CompilerParams(dimension_semantics=sem, vmem_limit_bytes=VMEM_LIMIT)


def _exchange(name, srcs, out_shapes, src_of, dst_of):
    n = len(srcs)

    def body(*refs):
        ins, outs = refs[:n], refs[n:2 * n]
        send_sems, recv_sems, local_sems = refs[2 * n:]
        x, y, c = lax.axis_index("x"), lax.axis_index("y"), lax.axis_index("c")
        me = 4 * x + 2 * y + c
        sends, recvs, locals_ = [], [], []
        for a in range(n):
            keep = pltpu.make_async_copy(src_of(a, ins[a], me), dst_of(a, outs[a], me), local_sems.at[a])
            keep.start()
            locals_.append(keep)
            for p in range(1, N_DEV):
                px, py, pc = (x if not p & 4 else 1 - x), (y if not p & 2 else 1 - y), (c if not p & 1 else 1 - c)
                peer = 4 * px + 2 * py + pc
                send = pltpu.make_async_remote_copy(
                    src_ref=src_of(a, ins[a], peer), dst_ref=dst_of(a, outs[a], me),
                    send_sem=send_sems.at[a, p - 1], recv_sem=recv_sems.at[a, p - 1],
                    device_id=(px, py, pc), device_id_type=pl.DeviceIdType.MESH)
                send.start()
                sends.append(send)
                recvs.append(pltpu.make_async_remote_copy(
                    src_ref=src_of(a, ins[a], peer), dst_ref=dst_of(a, outs[a], peer),
                    send_sem=send_sems.at[a, p - 1], recv_sem=recv_sems.at[a, p - 1],
                    device_id=(px, py, pc), device_id_type=pl.DeviceIdType.MESH))
        for r in recvs:
            r.wait_recv()
        for s in sends:
            s.wait_send()
        for k in locals_:
            k.wait()

    hbm = pl.BlockSpec(memory_space=pl.ANY)
    return pl.pallas_call(
        body, name=name,
        out_shape=[jax.ShapeDtypeStruct(s, d) for s, d in out_shapes],
        in_specs=[hbm] * n, out_specs=[hbm] * n,
        scratch_shapes=[pltpu.SemaphoreType.DMA((n, N_DEV - 1)), pltpu.SemaphoreType.DMA((n, N_DEV - 1)),
                        pltpu.SemaphoreType.DMA((n,))],
    )(*srcs)


def _all_gather(name, shards):
    return _exchange(name, shards, [((N_DEV,) + s.shape, s.dtype) for s in shards],
                     lambda a, ref, peer: ref, lambda a, ref, origin: ref.at[origin])


HBM_SPEC = pl.BlockSpec(memory_space=pltpu.HBM)
SEM_SPEC = pl.BlockSpec(memory_space=pltpu.SEMAPHORE)
DATAFLOW = pltpu.SideEffectType.DATAFLOW_SIDE_EFFECTING


def _peer_copies(src_refs, land_refs, send_sems, recv_sems, src_of, dst_of):
    x, y, c = lax.axis_index("x"), lax.axis_index("y"), lax.axis_index("c")
    me = 4 * x + 2 * y + c
    sends, recvs = [], []
    for a in range(len(src_refs)):
        for p in range(1, N_DEV):
            px, py, pc = (x if not p & 4 else 1 - x), (y if not p & 2 else 1 - y), (c if not p & 1 else 1 - c)
            peer = 4 * px + 2 * py + pc
            for dst_slot, out in ((me, sends), (peer, recvs)):
                out.append(pltpu.make_async_remote_copy(
                    src_ref=src_of(src_refs[a], peer), dst_ref=dst_of(land_refs[a], dst_slot),
                    send_sem=send_sems[a].at[p - 1], recv_sem=recv_sems[a].at[p - 1],
                    device_id=(px, py, pc), device_id_type=pl.DeviceIdType.MESH))
    return sends, recvs


def _exchange_start(name, srcs, land_shapes, src_of, dst_of):
    n = len(srcs)

    def body(*refs):
        sends, _ = _peer_copies(refs[:n], refs[n:2 * n], refs[2 * n:3 * n], refs[3 * n:4 * n], src_of, dst_of)
        for s in sends:
            s.start()
        refs[6 * n][...] = jnp.zeros((8, LANE), F32)

    sems = [pltpu.SemaphoreType.DMA((N_DEV - 1,))] * (2 * n)
    thru = [pltpu.HBM(s.shape, s.dtype) for s in srcs] + [pltpu.HBM(shp, s.dtype) for shp, s in zip(land_shapes, srcs)]
    outs = pl.pallas_call(
        body, name=name, out_shape=sems + thru + [jax.ShapeDtypeStruct((8, LANE), F32)],
        in_specs=[HBM_SPEC] * (2 * n),
        out_specs=[SEM_SPEC] * (2 * n) + [HBM_SPEC] * (2 * n) + [pl.BlockSpec(memory_space=pltpu.VMEM)],
        input_output_aliases={i: 2 * n + i for i in range(2 * n)},
        compiler_params=pltpu.CompilerParams(has_side_effects=DATAFLOW),
    )(*[pltpu.with_memory_space_constraint(s, pltpu.HBM) for s in srcs],
      *[pltpu.with_memory_space_constraint(lax.empty(shp, s.dtype), pltpu.HBM) for shp, s in zip(land_shapes, srcs)])
    return tuple(outs[:4 * n]), outs[4 * n]


def _exchange_wait(name, handle, after, src_of, dst_of):
    n = len(handle) // 4
    sems, thru = handle[:2 * n], handle[2 * n:]

    def body(*refs):
        sends, recvs = _peer_copies(refs[:n], refs[n:2 * n], refs[2 * n:3 * n], refs[3 * n:4 * n], src_of, dst_of)
        for s in sends:
            s.wait_send()
        for r in recvs:
            r.wait_recv()

    outs = pl.pallas_call(
        body, name=name, out_shape=[pltpu.HBM(t.shape, t.dtype) for t in thru],
        in_specs=[HBM_SPEC] * (2 * n) + [SEM_SPEC] * (2 * n) + [pl.BlockSpec(memory_space=pl.ANY)],
        out_specs=[HBM_SPEC] * (2 * n), input_output_aliases={i: i for i in range(2 * n)},
        compiler_params=pltpu.CompilerParams(has_side_effects=DATAFLOW),
    )(*thru, *sems, after)
    return outs[:n], outs[n:]


def _whole_src(ref, peer):
    return ref


def _slot(ref, k):
    return ref.at[k]


def _flat2(v):
    return v.reshape(-1, v.shape[-1])


def _matmul(name, kind, a, a_spec, b, b_spec, out_shape, out_spec, grid, res=None, res_spec=None, acc_shape=None):
    dims = {"nn": (((1,), (0,)), ((), ())), "nt": (((1,), (1,)), ((), ())), "tn": (((0,), (0,)), ((), ()))}[kind]
    nred = grid[-1]

    def body(*refs):
        if res is None:
            a_ref, b_ref, o_ref = refs[:3]
            r_ref = None
        else:
            a_ref, b_ref, r_ref, o_ref = refs[:4]
        part = lax.dot_general(_flat2(a_ref[...]), _flat2(b_ref[...]), dims, preferred_element_type=F32)

        def finish(total):
            if r_ref is not None:
                total = total + r_ref[...]
            o_ref[...] = total.reshape(o_ref.shape).astype(o_ref.dtype)

        if nred == 1:
            finish(part)
        else:
            acc_ref = refs[-1]
            k = pl.program_id(len(grid) - 1)

            @pl.when(k == 0)
            def _():
                acc_ref[...] = part

            @pl.when(k > 0)
            def _():
                acc_ref[...] += part

            @pl.when(k == nred - 1)
            def _():
                finish(acc_ref[...])

    ins, specs = [a, b], [a_spec, b_spec]
    if res is not None:
        ins.append(res)
        specs.append(res_spec)
    scratch = [] if nred == 1 else [pltpu.VMEM(acc_shape, F32)]
    return pl.pallas_call(
        body, name=name, grid=grid, in_specs=specs, out_specs=out_spec, out_shape=out_shape, scratch_shapes=scratch,
        compiler_params=_params(("parallel",) * (len(grid) - 1) + ("arbitrary",)),
    )(*ins)


TM = 512


def _nn_cols(name, a, wg, out_dtype=F32):
    _, k, nj = wg.shape
    tm = 1024
    return _matmul(
        name, "nn", a, pl.BlockSpec((tm, k), lambda j, i, r: (i, 0)),
        wg, pl.BlockSpec((None, k, nj), lambda j, i, r: (j, 0, 0)),
        jax.ShapeDtypeStruct((T, N_DEV * nj), out_dtype), pl.BlockSpec((tm, nj), lambda j, i, r: (i, j)),
        (N_DEV, T // tm, 1))


def _nn_rows(name, a, wg, res, s, tn):
    _, kj, n = wg.shape
    tm = 1024
    return _matmul(
        name, "nn", a, pl.BlockSpec((tm, s * kj), lambda j, i, r: (i, r)),
        wg, pl.BlockSpec((s, kj, tn), lambda j, i, r: (r, 0, j)),
        jax.ShapeDtypeStruct((T, n), F32), pl.BlockSpec((tm, tn), lambda j, i, r: (i, j)),
        (n // tn, T // tm, N_DEV // s), res=res, res_spec=pl.BlockSpec((tm, tn), lambda j, i, r: (i, j)),
        acc_shape=(tm, tn))


def _nt_cols(name, dc, dc_spec_of, wg):
    _, k, nj = wg.shape
    tm = tk = 1024
    return _matmul(
        name, "nt", dc, dc_spec_of(tm, nj),
        wg, pl.BlockSpec((None, tk, nj), lambda kt, i, j: (j, kt, 0)),
        jax.ShapeDtypeStruct((T, k), F32), pl.BlockSpec((tm, tk), lambda kt, i, j: (i, kt)),
        (k // tk, T // tm, N_DEV), acc_shape=(tm, tk))


def _nt_rows(name, dc, wg, s):
    _, kj, n = wg.shape
    return _matmul(
        name, "nt", dc, pl.BlockSpec((TM, n), lambda kt, i, r: (i, 0)),
        wg, pl.BlockSpec((s, kj, n), lambda kt, i, r: (kt, 0, 0)),
        jax.ShapeDtypeStruct((T, N_DEV * kj), F32), pl.BlockSpec((TM, s * kj), lambda kt, i, r: (i, kt)),
        (N_DEV // s, T // TM, 1))


def _tn_cols(name, a, dc, dc_spec_of, nj):
    k = a.shape[1]
    tk = 512
    return _matmul(
        name, "tn", a, pl.BlockSpec((T, tk), lambda j, kt, r: (0, kt)),
        dc, dc_spec_of(T, nj),
        jax.ShapeDtypeStruct((N_DEV, k, nj), BF16), pl.BlockSpec((None, tk, nj), lambda j, kt, r: (j, kt, 0)),
        (N_DEV, k // tk, 1))


def _tn_rows(name, a, dc, kj, s):
    n = dc.shape[1]
    tn = 512
    return _matmul(
        name, "tn", a, pl.BlockSpec((T, s * kj), lambda kt, j, r: (0, kt)),
        dc, pl.BlockSpec((T, tn), lambda kt, j, r: (0, j)),
        jax.ShapeDtypeStruct((N_DEV, kj, n), BF16), pl.BlockSpec((s, kj, tn), lambda kt, j, r: (kt, 0, j)),
        (N_DEV // s, n // tn, 1))


TR = 256


def _rows(width):
    return pl.BlockSpec((TR, width), lambda i: (i, 0))


def _whole(shape):
    return pl.BlockSpec(shape, lambda i: (0,) * len(shape))


def _rmsnorm_fwd(name, x, g):
    def body(x_ref, g_ref, o_ref):
        xv = x_ref[...]
        r = lax.rsqrt(jnp.mean(xv * xv, axis=-1, keepdims=True) + EPS)
        o_ref[...] = ((xv * r) * g_ref[...]).astype(BF16)

    return pl.pallas_call(
        body, name=name, grid=(T // TR,), in_specs=[_rows(D), _whole((1, D))], out_specs=_rows(D),
        out_shape=jax.ShapeDtypeStruct((T, D), BF16), compiler_params=_params(("parallel",)),
    )(x, g)


def _rms_bwd_math(dy, xv, g):
    r = lax.rsqrt(jnp.mean(xv * xv, axis=-1, keepdims=True) + EPS)
    xhat = xv * r
    dxhat = dy * g
    dx = r * (dxhat - xhat * jnp.mean(dxhat * xhat, axis=-1, keepdims=True))
    return dx, dy * xhat


def _accumulate(ref, val):
    @pl.when(pl.program_id(0) == 0)
    def _():
        ref[...] = val

    @pl.when(pl.program_id(0) > 0)
    def _():
        ref[...] += val


def _rmsnorm_bwd(name, dy, x, g, res):
    def body(dy_ref, x_ref, g_ref, res_ref, dx_ref, dxb_ref, dg_ref):
        dx, dgr = _rms_bwd_math(dy_ref[...], x_ref[...], g_ref[...])
        tot = res_ref[...] + dx
        dx_ref[...] = tot
        dxb_ref[...] = tot.astype(BF16)
        _accumulate(dg_ref, jnp.sum(dgr, axis=0, keepdims=True))

    return pl.pallas_call(
        body, name=name, grid=(T // TR,), in_specs=[_rows(D), _rows(D), _whole((1, D)), _rows(D)],
        out_specs=[_rows(D), _rows(D), _whole((1, D))],
        out_shape=[jax.ShapeDtypeStruct((T, D), F32), jax.ShapeDtypeStruct((T, D), BF16),
                   jax.ShapeDtypeStruct((1, D), F32)],
        compiler_params=_params(("arbitrary",)),
    )(dy, x, g, res)


def _loss_head(x, g, target):
    def body(x_ref, g_ref, t_ref, loss_ref, dx_ref, dxb_ref, dg_ref):
        xv, gv = x_ref[...], g_ref[...]
        r = lax.rsqrt(jnp.mean(xv * xv, axis=-1, keepdims=True) + EPS)
        err = (xv * r) * gv - t_ref[...]
        part = 0.5 * jnp.sum(jnp.mean(err * err, axis=-1, keepdims=True))
        dx, dgr = _rms_bwd_math(err * (1.0 / D), xv, gv)
        dx_ref[...] = dx
        dxb_ref[...] = dx.astype(BF16)
        _accumulate(dg_ref, jnp.sum(dgr, axis=0, keepdims=True))
        _accumulate(loss_ref, jnp.full((8, LANE), part, F32))

    return pl.pallas_call(
        body, name="loss_head", grid=(T // TR,), in_specs=[_rows(D), _whole((1, D)), _rows(D)],
        out_specs=[_whole((8, LANE)), _rows(D), _rows(D), _whole((1, D))],
        out_shape=[jax.ShapeDtypeStruct((8, LANE), F32), jax.ShapeDtypeStruct((T, D), F32),
                   jax.ShapeDtypeStruct((T, D), BF16), jax.ShapeDtypeStruct((1, D), F32)],
        compiler_params=_params(("arbitrary",)),
    )(x, g, target)


MIX_OFFS = ((0, WA), (WA, WB), (WA + WB, WC))


def _mix_fwd(name, oa, ob, oc, gain):
    def body(oa_ref, ob_ref, oc_ref, g_ref, o_ref):
        for ref, (off, w) in zip((oa_ref, ob_ref, oc_ref), MIX_OFFS):
            o = ref[...]
            r = lax.rsqrt(jnp.mean(o * o, axis=-1, keepdims=True) + EPS)
            o_ref[:, off:off + w] = ((o * r) * g_ref[:, off:off + w]).astype(BF16)

    return pl.pallas_call(
        body, name=name, grid=(T // TR,), in_specs=[_rows(WA), _rows(WB), _rows(WC), _whole((1, D))],
        out_specs=_rows(D), out_shape=jax.ShapeDtypeStruct((T, D), BF16), compiler_params=_params(("parallel",)),
    )(oa, ob, oc, gain)


def _mix_bwd(name, dmixed, oa, ob, oc, gain):
    def body(dm_ref, oa_ref, ob_ref, oc_ref, g_ref, doa_ref, dob_ref, doc_ref, dg_ref):
        dgs = []
        for ref, dref, (off, w) in zip((oa_ref, ob_ref, oc_ref), (doa_ref, dob_ref, doc_ref), MIX_OFFS):
            dx, dgr = _rms_bwd_math(dm_ref[:, off:off + w], ref[...], g_ref[:, off:off + w])
            dref[...] = dx
            dgs.append(jnp.sum(dgr, axis=0, keepdims=True))
        _accumulate(dg_ref, jnp.concatenate(dgs, axis=1))

    return pl.pallas_call(
        body, name=name, grid=(T // TR,),
        in_specs=[_rows(D), _rows(WA), _rows(WB), _rows(WC), _whole((1, D))],
        out_specs=[_rows(WA), _rows(WB), _rows(WC), _whole((1, D))],
        out_shape=[jax.ShapeDtypeStruct((T, WA), F32), jax.ShapeDtypeStruct((T, WB), F32),
                   jax.ShapeDtypeStruct((T, WC), F32), jax.ShapeDtypeStruct((1, D), F32)],
        compiler_params=_params(("arbitrary",)),
    )(dmixed, oa, ob, oc, gain)


def _rope_tables():
    inv_freq = ROPE_THETA ** (-jnp.arange(0, HD, 2, dtype=F32) / HD)
    ang = jnp.arange(T, dtype=F32)[:, None] * inv_freq[None, :]
    cos, sin = jnp.cos(ang), jnp.sin(ang)
    cos2 = jnp.tile(jnp.concatenate([cos, cos], axis=1), (1, LANE // HD))
    sin2 = jnp.tile(jnp.concatenate([-sin, sin], axis=1), (1, LANE // HD))
    return cos2, sin2


def _rot_half(v):
    lane = lax.broadcasted_iota(jnp.int32, v.shape, 1)
    return jnp.where(lane % HD < HD // 2, pltpu.roll(v, LANE - HD // 2, 1), pltpu.roll(v, HD // 2, 1))


def _rope_fwd(name, proj, cos2, sin2):
    def body(p_ref, c_ref, s_ref, *outs):
        cv, sv = c_ref[...], s_ref[...]
        off = 0
        for o_ref, (_, w, rot, is_q) in zip(outs, GROUPS):
            for b in range(w // LANE):
                v = p_ref[:, off + b * LANE:off + (b + 1) * LANE]
                if rot:
                    v = v * cv + _rot_half(v) * sv
                if is_q:
                    v = v * (HD ** -0.5)
                o_ref[:, b * LANE:(b + 1) * LANE] = v.astype(BF16)
            off += w

    return pl.pallas_call(
        body, name=name, grid=(T // TR,), in_specs=[_rows(IN_COLS), _rows(LANE), _rows(LANE)],
        out_specs=[_rows(w) for _, w, _, _ in GROUPS],
        out_shape=[jax.ShapeDtypeStruct((T, w), BF16) for _, w, _, _ in GROUPS],
        compiler_params=_params(("parallel",)),
    )(proj, cos2, sin2)


def _rope_bwd(name, grads, cos2, sin2):
    def body(*refs):
        ins, (c_ref, s_ref, o_ref) = refs[:9], refs[9:]
        cv, sv = c_ref[...], s_ref[...]
        off = 0
        for d_ref, (_, w, rot, is_q) in zip(ins, GROUPS):
            for b in range(w // LANE):
                v = d_ref[:, b * LANE:(b + 1) * LANE]
                if is_q:
                    v = v * (HD ** -0.5)
                if rot:
                    v = v * cv + _rot_half(v * sv)
                o_ref[:, off + b * LANE:off + (b + 1) * LANE] = v.astype(BF16)
            off += w

    return pl.pallas_call(
        body, name=name, grid=(T // TR,), in_specs=[_rows(w) for _, w, _, _ in GROUPS] + [_rows(LANE), _rows(LANE)],
        out_specs=_rows(IN_COLS), out_shape=jax.ShapeDtypeStruct((T, IN_COLS), BF16),
        compiler_params=_params(("parallel",)),
    )(*grads, cos2, sin2)


NT_DIMS = (((1,), (1,)), ((), ()))
TN_DIMS = (((0,), (0,)), ((), ()))


def _softmax_parts(q, k, valid, mult, bias, sink):
    s = lax.dot_general(q, k, NT_DIMS, preferred_element_type=F32)
    if bias is not None:
        s = s + bias
    s = jnp.where(valid, s, NEG)
    m = jnp.max(s, axis=1, keepdims=True)
    e = jnp.exp(s - m)
    if mult is not None:
        e = e * mult
    l = jnp.sum(e, axis=1, keepdims=True)
    e_sink = None
    if sink is not None:
        e_sink = jnp.exp(sink - m)
        l = l + e_sink
    return e, l, e_sink


def _head_fwd(q, k, v, valid, mult=None, bias=None, sink=None):
    e, l, _ = _softmax_parts(q, k, valid, mult, bias, sink)
    return jnp.dot(e.astype(BF16), v, preferred_element_type=F32) / l


def _head_bwd(q, k, v, o, do, valid, mult=None, bias=None, sink=None):
    e, l, e_sink = _softmax_parts(q, k, valid, mult, bias, sink)
    p = e / l
    dob = do.astype(BF16)
    dp = lax.dot_general(dob, v, NT_DIMS, preferred_element_type=F32)
    delta = jnp.sum(do * o, axis=1, keepdims=True)
    ds = p * (dp - delta)
    dsb = ds.astype(BF16)
    dq = jnp.dot(dsb, k, preferred_element_type=F32)
    dk = lax.dot_general(dsb, q, TN_DIMS, preferred_element_type=F32)
    dv = lax.dot_general(p.astype(BF16), dob, TN_DIMS, preferred_element_type=F32)
    dsink = None if sink is None else -(e_sink / l) * delta
    return dq, dk, dv, ds, dsink


def _dilation_mask():
    d = jnp.arange(T, dtype=jnp.int32)[:, None] - jnp.arange(T, dtype=jnp.int32)[None, :]
    ad = jnp.abs(d)
    count = jnp.zeros((T, T), jnp.int32)
    for window, r in ((128, 1), (512, 4), (2048, 16)):
        count += ((ad % r == 0) & (ad // r <= window // (2 * r))).astype(jnp.int32)
    return count.astype(BF16)


BQ_A = 256


def _attn_a_fwd(name, qa, ka, va, mask):
    def body(q_ref, k_ref, v_ref, m_ref, o_ref):
        mult = m_ref[...].astype(F32)
        valid = mult > 0.0
        outs = [_head_fwd(q_ref[:, h * HD:(h + 1) * HD], k_ref[:, h * HD:(h + 1) * HD], v_ref[:, h * HD:(h + 1) * HD],
                          valid, mult=mult) for h in range(2)]
        o_ref[...] = jnp.concatenate(outs, axis=1)

    qs = pl.BlockSpec((BQ_A, LANE), lambda p, i: (i, p))
    ks = pl.BlockSpec((T, LANE), lambda p, i: (0, p))
    return pl.pallas_call(
        body, name=name, grid=(HA // 2, T // BQ_A),
        in_specs=[qs, ks, ks, pl.BlockSpec((BQ_A, T), lambda p, i: (i, 0))], out_specs=qs,
        out_shape=jax.ShapeDtypeStruct((T, WA), F32), compiler_params=_params(("parallel", "parallel")),
    )(qa, ka, va, mask)


def _attn_a_bwd(name, qa, ka, va, oa, doa, mask):
    def body(q_ref, k_ref, v_ref, o_ref, do_ref, m_ref, dq_ref, dk_ref, dv_ref):
        mult = m_ref[...].astype(F32)
        valid = mult > 0.0
        dqs, dks, dvs = [], [], []
        for h in range(2):
            sl = slice(h * HD, (h + 1) * HD)
            dq, dk, dv, _, _ = _head_bwd(q_ref[:, sl], k_ref[:, sl], v_ref[:, sl], o_ref[:, sl], do_ref[:, sl],
                                         valid, mult=mult)
            dqs.append(dq)
            dks.append(dk)
            dvs.append(dv)
        dq_ref[...] = jnp.concatenate(dqs, axis=1)
        dk2, dv2 = jnp.concatenate(dks, axis=1), jnp.concatenate(dvs, axis=1)

        @pl.when(pl.program_id(1) == 0)
        def _():
            dk_ref[...] = dk2
            dv_ref[...] = dv2

        @pl.when(pl.program_id(1) > 0)
        def _():
            dk_ref[...] += dk2
            dv_ref[...] += dv2

    qs = pl.BlockSpec((BQ_A, LANE), lambda p, i: (i, p))
    ks = pl.BlockSpec((T, LANE), lambda p, i: (0, p))
    return pl.pallas_call(
        body, name=name, grid=(HA // 2, T // BQ_A),
        in_specs=[qs, ks, ks, qs, qs, pl.BlockSpec((BQ_A, T), lambda p, i: (i, 0))], out_specs=[qs, ks, ks],
        out_shape=[jax.ShapeDtypeStruct((T, WA), F32)] * 3, compiler_params=_params(("parallel", "arbitrary")),
    )(qa, ka, va, oa, doa, mask)


BQ_B = 128
SPAN_B = BQ_B + 2 * WINDOW_B


def _window_b(i):
    start = pl.multiple_of(jnp.clip(i * BQ_B - WINDOW_B, 0, T - SPAN_B), BQ_B)
    qpos = i * BQ_B + lax.broadcasted_iota(jnp.int32, (BQ_B, SPAN_B), 0)
    kpos = start + lax.broadcasted_iota(jnp.int32, (BQ_B, SPAN_B), 1)
    return start, jnp.abs(qpos - kpos) <= WINDOW_B


def _attn_b_fwd(name, qb, kb, vb, sink):
    def body(sink_ref, q_ref, k_ref, v_ref, o_ref):
        start, valid = _window_b(pl.program_id(0))
        kw, vw = k_ref[pl.ds(start, SPAN_B), :], v_ref[pl.ds(start, SPAN_B), :]
        outs = []
        for h in range(HB):
            kv = slice((h // (HB // HKV)) * HD, (h // (HB // HKV) + 1) * HD)
            outs.append(_head_fwd(q_ref[:, h * HD:(h + 1) * HD], kw[:, kv], vw[:, kv], valid, sink=sink_ref[h]))
        o_ref[...] = jnp.concatenate(outs, axis=1)

    return pl.pallas_call(
        body, name=name, grid=(T // BQ_B,),
        in_specs=[pl.BlockSpec(memory_space=pltpu.SMEM), pl.BlockSpec((BQ_B, WB), lambda i: (i, 0)),
                  _whole((T, WKV)), _whole((T, WKV))],
        out_specs=pl.BlockSpec((BQ_B, WB), lambda i: (i, 0)),
        out_shape=jax.ShapeDtypeStruct((T, WB), F32), compiler_params=_params(("parallel",)),
    )(sink, qb, kb, vb)


def _attn_b_bwd(name, qb, kb, vb, ob, dob, sink):
    def body(sink_ref, q_ref, k_ref, v_ref, o_ref, do_ref, dq_ref, dk_ref, dv_ref, dsink_ref):
        i = pl.program_id(0)
        start, valid = _window_b(i)
        kw, vw = k_ref[pl.ds(start, SPAN_B), :], v_ref[pl.ds(start, SPAN_B), :]
        lane = lax.broadcasted_iota(jnp.int32, (1, LANE), 1)
        dsink = jnp.zeros((1, LANE), F32)
        dqs, dks, dvs = [], [], []
        for g in range(HKV):
            kv = slice(g * HD, (g + 1) * HD)
            dk_g = jnp.zeros((SPAN_B, HD), F32)
            dv_g = jnp.zeros((SPAN_B, HD), F32)
            for h in range(g * (HB // HKV), (g + 1) * (HB // HKV)):
                sl = slice(h * HD, (h + 1) * HD)
                dq, dk, dv, _, dsr = _head_bwd(q_ref[:, sl], kw[:, kv], vw[:, kv], o_ref[:, sl], do_ref[:, sl],
                                               valid, sink=sink_ref[h])
                dqs.append(dq)
                dk_g += dk
                dv_g += dv
                dsink += jnp.where(lane == h, jnp.sum(dsr), 0.0)
            dks.append(dk_g)
            dvs.append(dv_g)
        dq_ref[...] = jnp.concatenate(dqs, axis=1)

        @pl.when(i == 0)
        def _():
            dk_ref[...] = jnp.zeros_like(dk_ref)
            dv_ref[...] = jnp.zeros_like(dv_ref)
            dsink_ref[...] = jnp.zeros_like(dsink_ref)

        dk_ref[pl.ds(start, SPAN_B), :] += jnp.concatenate(dks, axis=1)
        dv_ref[pl.ds(start, SPAN_B), :] += jnp.concatenate(dvs, axis=1)
        dsink_ref[...] += dsink

    qs = pl.BlockSpec((BQ_B, WB), lambda i: (i, 0))
    return pl.pallas_call(
        body, name=name, grid=(T // BQ_B,),
        in_specs=[pl.BlockSpec(memory_space=pltpu.SMEM), qs, _whole((T, WKV)), _whole((T, WKV)), qs, qs],
        out_specs=[qs, _whole((T, WKV)), _whole((T, WKV)), _whole((1, LANE))],
        out_shape=[jax.ShapeDtypeStruct((T, WB), F32), jax.ShapeDtypeStruct((T, WKV), F32),
                   jax.ShapeDtypeStruct((T, WKV), F32), jax.ShapeDtypeStruct((1, LANE), F32)],
        compiler_params=_params(("arbitrary",)),
    )(sink, qb, kb, vb, ob, dob)


SPAN_C = NA_ROWS * GRID_W


def _row_start(r):
    return jnp.clip(r - NA_ROWS // 2, 0, ROWS - NA_ROWS)


def _off_index(r):
    return _row_start(r) - r + (NA_ROWS - 1)


def _col_valid():
    c = lax.broadcasted_iota(jnp.int32, (GRID_W, SPAN_C), 0)
    kc = lax.broadcasted_iota(jnp.int32, (GRID_W, SPAN_C), 1) % GRID_W
    cs = jnp.clip(c - NA_COLS // 2, 0, GRID_W - NA_COLS)
    return (kc >= cs) & (kc < cs + NA_COLS)


N_TAB = 16


def _rpb_tables(rpb):
    w129 = jnp.concatenate([rpb[..., NA_COLS - 1:], jnp.zeros(rpb.shape[:2] + (129 - (2 * NA_COLS - 1),), F32),
                            rpb[..., :NA_COLS - 1]], axis=-1)
    toep = jnp.tile(w129, (1, 1, GRID_W))[..., :GRID_W * LANE].reshape(HC, 2 * NA_ROWS - 1, GRID_W, LANE)
    pairs = jnp.concatenate([toep[:, :-1, :, :GRID_W], toep[:, 1:, :, :GRID_W]], axis=-1)
    return jnp.pad(pairs, ((0, 0), (0, N_TAB - pairs.shape[1]), (0, 0), (0, 0)))


def _bias_c(t_ref, h, d):
    return jnp.concatenate([t_ref[h, d + k] for k in range(0, NA_ROWS, 2)], axis=1)


def _attn_c_fwd(name, qc, kc, vc, tables):
    def body(q_ref, k_ref, v_ref, t_ref, o_ref):
        r = pl.program_id(1)
        start = pl.multiple_of(_row_start(r) * GRID_W, GRID_W)
        valid = _col_valid()
        kw, vw = k_ref[pl.ds(start, SPAN_C), :], v_ref[pl.ds(start, SPAN_C), :]
        outs = [_head_fwd(q_ref[:, h * HD:(h + 1) * HD], kw[:, h * HD:(h + 1) * HD], vw[:, h * HD:(h + 1) * HD],
                          valid, bias=_bias_c(t_ref, h, _off_index(r))) for h in range(2)]
        o_ref[...] = jnp.concatenate(outs, axis=1)

    qs = pl.BlockSpec((GRID_W, LANE), lambda p, r: (r, p))
    ks = pl.BlockSpec((T, LANE), lambda p, r: (0, p))
    ts = pl.BlockSpec((2, N_TAB, GRID_W, LANE), lambda p, r: (p, 0, 0, 0))
    return pl.pallas_call(
        body, name=name, grid=(HC // 2, ROWS), in_specs=[qs, ks, ks, ts], out_specs=qs,
        out_shape=jax.ShapeDtypeStruct((T, WC), F32), compiler_params=_params(("parallel", "parallel")),
    )(qc, kc, vc, tables)


def _attn_c_bwd(name, qc, kc, vc, oc, doc, tables):
    def body(q_ref, k_ref, v_ref, o_ref, do_ref, t_ref, dq_ref, dk_ref, dv_ref, dt_ref):
        r = pl.program_id(1)
        d = _off_index(r)
        start = pl.multiple_of(_row_start(r) * GRID_W, GRID_W)
        valid = _col_valid()
        kw, vw = k_ref[pl.ds(start, SPAN_C), :], v_ref[pl.ds(start, SPAN_C), :]

        @pl.when(r == 0)
        def _():
            dk_ref[...] = jnp.zeros_like(dk_ref)
            dv_ref[...] = jnp.zeros_like(dv_ref)
            dt_ref[...] = jnp.zeros_like(dt_ref)

        dqs, dks, dvs = [], [], []
        for h in range(2):
            sl = slice(h * HD, (h + 1) * HD)
            dq, dk, dv, ds, _ = _head_bwd(q_ref[:, sl], kw[:, sl], vw[:, sl], o_ref[:, sl], do_ref[:, sl],
                                          valid, bias=_bias_c(t_ref, h, d))
            dqs.append(dq)
            dks.append(dk)
            dvs.append(dv)
            for k in range(0, NA_ROWS, 2):
                dt_ref[h, d + k] += ds[:, k * GRID_W:(k + 2) * GRID_W]
        dq_ref[...] = jnp.concatenate(dqs, axis=1)
        dk_ref[pl.ds(start, SPAN_C), :] += jnp.concatenate(dks, axis=1)
        dv_ref[pl.ds(start, SPAN_C), :] += jnp.concatenate(dvs, axis=1)

    qs = pl.BlockSpec((GRID_W, LANE), lambda p, r: (r, p))
    ks = pl.BlockSpec((T, LANE), lambda p, r: (0, p))
    ts = pl.BlockSpec((2, N_TAB, GRID_W, LANE), lambda p, r: (p, 0, 0, 0))
    return pl.pallas_call(
        body, name=name, grid=(HC // 2, ROWS), in_specs=[qs, ks, ks, qs, qs, ts], out_specs=[qs, ks, ks, ts],
        out_shape=[jax.ShapeDtypeStruct((T, WC), F32)] * 3 + [jax.ShapeDtypeStruct((HC, N_TAB, GRID_W, LANE), F32)],
        compiler_params=_params(("parallel", "arbitrary")),
    )(qc, kc, vc, oc, doc, tables)


def _split3(v):
    hi = v.astype(BF16)
    r1 = v - hi.astype(F32)
    mid = r1.astype(BF16)
    lo = (r1 - mid.astype(F32)).astype(BF16)
    return hi, mid, lo


def _rpb_reduce(name, dtables):
    x = dtables.reshape(HC, N_TAB, GRID_W * LANE)
    c = jnp.arange(GRID_W)[:, None]
    lane = jnp.arange(LANE)[None, :]
    col = (lane // GRID_W) * LANE + jnp.clip(lane % GRID_W - c + (NA_COLS - 1), 0, 2 * NA_COLS - 2)
    col_onehot = (col.reshape(-1)[:, None] == jnp.arange(2 * LANE)[None, :]).astype(BF16)
    a2 = jnp.arange(N_TAB)[None, :]
    row_onehot = jnp.concatenate([(jnp.arange(16)[:, None] == a2 + u) & (a2 < 2 * NA_ROWS - 2) for u in range(2)],
                                 axis=1).astype(BF16)

    def body(x_ref, e_ref, f_ref, o_ref):
        y = sum(jnp.dot(part, e_ref[...], preferred_element_type=F32) for part in _split3(x_ref[...]))
        z = jnp.concatenate([y[:, :LANE], y[:, LANE:]], axis=0)
        o_ref[...] = sum(jnp.dot(f_ref[...], part, preferred_element_type=F32) for part in _split3(z))

    out = pl.pallas_call(
        body, name=name, grid=(HC,),
        in_specs=[pl.BlockSpec((None, N_TAB, GRID_W * LANE), lambda h: (h, 0, 0)),
                  _whole((GRID_W * LANE, 2 * LANE)), _whole((16, 2 * N_TAB))],
        out_specs=pl.BlockSpec((None, 16, LANE), lambda h: (h, 0, 0)),
        out_shape=jax.ShapeDtypeStruct((HC, 16, LANE), F32), compiler_params=_params(("parallel",)),
    )(x, col_onehot, row_onehot)
    return out[:, :2 * NA_ROWS - 1, :2 * NA_COLS - 1]


TC = 128
NCB = DFF // TC


def _shift_down(v, rows):
    return jnp.where(rows == 0, 0.0, pltpu.roll(v, 1, 0))


def _shift_up(v, rows):
    return jnp.where(rows == T - 1, 0.0, pltpu.roll(v, T - 1, 0))


def _conv(v, w, b, rows):
    return _shift_down(v, rows) * w[0:1] + v * w[1:2] + _shift_up(v, rows) * w[2:3] + b


def _ffn_specs():
    gate = lambda shape: pl.BlockSpec(shape, lambda j: (0, j))
    val = lambda shape: pl.BlockSpec(shape, lambda j: (0, j + NCB))
    return [gate((T, TC)), val((T, TC)), gate((3, TC)), val((3, TC)), gate((1, TC)), val((1, TC))]


def _ffn_mid_fwd(name, up, conv_w, conv_b):
    def body(xg_ref, xv_ref, wg_ref, wv_ref, bg_ref, bv_ref, o_ref):
        rows = lax.broadcasted_iota(jnp.int32, (T, TC), 0)
        ug = _conv(xg_ref[...], wg_ref[...], bg_ref[...], rows)
        uv = _conv(xv_ref[...], wv_ref[...], bv_ref[...], rows)
        o_ref[...] = (ug * jax.nn.sigmoid(ug) * uv).astype(BF16)

    return pl.pallas_call(
        body, name=name, grid=(NCB,), in_specs=_ffn_specs(), out_specs=pl.BlockSpec((T, TC), lambda j: (0, j)),
        out_shape=jax.ShapeDtypeStruct((T, DFF), BF16), compiler_params=_params(("parallel",)),
    )(up, up, conv_w, conv_w, conv_b, conv_b)


def _ffn_mid_bwd(name, dact, up, conv_w, conv_b):
    def body(da_ref, xg_ref, xv_ref, wg_ref, wv_ref, bg_ref, bv_ref, dx_ref, dw_ref, db_ref):
        rows = lax.broadcasted_iota(jnp.int32, (T, TC), 0)
        xg, xv, wg, wv = xg_ref[...], xv_ref[...], wg_ref[...], wv_ref[...]
        ug = _conv(xg, wg, bg_ref[...], rows)
        uv = _conv(xv, wv, bv_ref[...], rows)
        sg = jax.nn.sigmoid(ug)
        da = da_ref[...]
        dug = da * uv * (sg * (1.0 + ug * (1.0 - sg)))
        duv = da * (ug * sg)
        for half, (xin, w, du) in enumerate(((xg, wg, dug), (xv, wv, duv))):
            dx = _shift_up(du, rows) * w[0:1] + du * w[1:2] + _shift_down(du, rows) * w[2:3]
            dx_ref[half] = dx.astype(BF16)
            dw_ref[half] = jnp.concatenate(
                [jnp.sum(_shift_down(xin, rows) * du, axis=0, keepdims=True), jnp.sum(xin * du, axis=0, keepdims=True),
                 jnp.sum(_shift_up(xin, rows) * du, axis=0, keepdims=True)], axis=0)
            db_ref[half] = jnp.sum(du, axis=0, keepdims=True)

    return pl.pallas_call(
        body, name=name, grid=(NCB,), in_specs=[pl.BlockSpec((T, TC), lambda j: (0, j))] + _ffn_specs(),
        out_specs=[pl.BlockSpec((2, T, TC), lambda j: (0, 0, j)), pl.BlockSpec((2, 3, TC), lambda j: (0, 0, j)),
                   pl.BlockSpec((2, 1, TC), lambda j: (0, 0, j))],
        out_shape=[jax.ShapeDtypeStruct((2, T, DFF), BF16), jax.ShapeDtypeStruct((2, 3, DFF), F32),
                   jax.ShapeDtypeStruct((2, 1, DFF), F32)],
        compiler_params=_params(("parallel",)),
    )(dact, up, up, conv_w, conv_w, conv_b, conv_b)


def _dup_spec(tm, nj):
    per = DFF // nj
    return pl.BlockSpec((None, tm, nj), lambda a, b, j: (j // per, 0 if tm == T else b, j % per))


def _dup_spec_tn(tm, nj):
    per = DFF // nj
    return pl.BlockSpec((None, tm, nj), lambda j, kt, r: (j // per, 0, j % per))


def _adamw_math(w, g, m, v):
    m = ADAM_B1 * m + (1.0 - ADAM_B1) * g
    v = ADAM_B2 * v + (1.0 - ADAM_B2) * (g * g)
    m_hat = m / (1.0 - ADAM_B1 ** ADAM_STEP)
    v_hat = v / (1.0 - ADAM_B2 ** ADAM_STEP)
    delta = -ADAM_LR * (m_hat / (jnp.sqrt(v_hat) + ADAM_EPS) + ADAM_WD * w)
    return delta, m, v


def _adamw_sharded(name, w, m, v, parts):
    _, r, c = w.shape
    tr = 64

    def body(w_ref, m_ref, v_ref, p0_ref, p1_ref, g_ref, d_ref, nm_ref, nv_ref):
        def run(p_ref):
            g = p_ref[0].astype(F32)
            for k in range(1, N_DEV):
                g = g + p_ref[k].astype(F32)
            d, nm, nv = _adamw_math(w_ref[...], g, m_ref[...], v_ref[...])
            g_ref[...] = g
            d_ref[...] = d
            nm_ref[...] = nm
            nv_ref[...] = nv

        @pl.when(pl.program_id(0) == 0)
        def _():
            run(p0_ref)

        @pl.when(pl.program_id(0) == 1)
        def _():
            run(p1_ref)

    ws = pl.BlockSpec((None, tr, c), lambda l, i: (l, i, 0))
    p0 = pl.BlockSpec((N_DEV, tr, c), lambda l, i: (0, jnp.where(l == 0, i, r // tr - 1), 0))
    p1 = pl.BlockSpec((N_DEV, tr, c), lambda l, i: (0, jnp.where(l == 1, i, 0), 0))
    return pl.pallas_call(
        body, name=name, grid=(DEPTH, r // tr), in_specs=[ws, ws, ws, p0, p1], out_specs=[ws] * 4,
        out_shape=[jax.ShapeDtypeStruct(w.shape, F32)] * 4, compiler_params=_params(("arbitrary", "arbitrary")),
    )(w, m, v, *parts)


def _sum_devices(name, parts):
    r = parts.shape[1]

    def body(p_ref, o_ref):
        g = p_ref[0]
        for k in range(1, N_DEV):
            g = g + p_ref[k]
        o_ref[...] = g

    return pl.pallas_call(
        body, name=name, in_specs=[pl.BlockSpec((N_DEV, r, LANE), lambda: (0, 0, 0))],
        out_specs=pl.BlockSpec((r, LANE), lambda: (0, 0)), out_shape=jax.ShapeDtypeStruct((r, LANE), F32),
        compiler_params=_params(),
    )(parts)


def _adamw_small(name, w, g, m, v):
    spec = pl.BlockSpec(w.shape, lambda: (0, 0))

    def body(w_ref, g_ref, m_ref, v_ref, d_ref, nm_ref, nv_ref):
        d, nm, nv = _adamw_math(w_ref[...], g_ref[...], m_ref[...], v_ref[...])
        d_ref[...] = d
        nm_ref[...] = nm
        nv_ref[...] = nv

    return pl.pallas_call(
        body, name=name, in_specs=[spec] * 4, out_specs=[spec] * 3,
        out_shape=[jax.ShapeDtypeStruct(w.shape, F32)] * 3, compiler_params=_params(),
    )(w, g, m, v)


def _pack(arrays):
    flat = jnp.concatenate([a.reshape(-1) for a in arrays])
    pad = (-flat.shape[0]) % (8 * LANE)
    return jnp.pad(flat, (0, pad)).reshape(-1, LANE)


def _unpack(buf, shapes):
    flat, out, off = buf.reshape(-1), [], 0
    for s in shapes:
        n = 1
        for d in s:
            n *= d
        out.append(flat[off:off + n].reshape(s))
        off += n
    return out


def _local_step(x, target, small, weights, conv_w_full, hand_over):
    cos2, sin2 = _rope_tables()
    mask_a = _dilation_mask()
    saved = []
    for l in range(DEPTH):
        g1, g2 = small["ln_attn"][l][None], small["ln_ffn"][l][None]
        gain, sink, cb = small["mix_gain"][l][None], small["sink_b"][l], small["conv_b"][l][None]
        cw = conv_w_full[l]
        bias = _rpb_tables(small["rpb_c"][l])
        h1 = _rmsnorm_fwd(f"norm_attn_{l}", x, g1)
        proj = _nn_cols(f"proj_in_{l}", h1, weights("w_in", l, h1))
        qa, ka, va, qb, kb, vb, qc, kc, vc = _rope_fwd(f"rope_{l}", proj, cos2, sin2)
        oa = _attn_a_fwd(f"attn_a_{l}", qa, ka, va, mask_a)
        ob = _attn_b_fwd(f"attn_b_{l}", qb, kb, vb, sink)
        oc = _attn_c_fwd(f"attn_c_{l}", qc, kc, vc, bias)
        mixed = _mix_fwd(f"mix_{l}", oa, ob, oc, gain)
        x_mid = _nn_rows(f"proj_out_{l}", mixed, weights("w_out", l, mixed), x, 8, 512)
        h2 = _rmsnorm_fwd(f"norm_ffn_{l}", x_mid, g2)
        up = _nn_cols(f"ffn_up_{l}", h2, weights("w_up", l, h2))
        act = _ffn_mid_fwd(f"ffn_mid_{l}", up, cw, cb)
        x_out = _nn_rows(f"ffn_down_{l}", act, weights("w_down", l, act), x_mid, 2, 1024)
        saved.append(dict(x=x, h1=h1, qkv=(qa, ka, va, qb, kb, vb, qc, kc, vc), o=(oa, ob, oc), mixed=mixed,
                          x_mid=x_mid, h2=h2, up=up, act=act, g1=g1, g2=g2, gain=gain, sink=sink, cb=cb, cw=cw, bias=bias))
        x = x_out

    loss8, dx, dxb, d_ln_final = _loss_head(x, small["ln_final"][None], target)
    sgrads = [None] * DEPTH
    for l in reversed(range(DEPTH)):
        s = saved[l]
        qa, ka, va, qb, kb, vb, qc, kc, vc = s["qkv"]
        oa, ob, oc = s["o"]
        wg_in, wg_out = weights("w_in", l, None), weights("w_out", l, None)
        wg_up, wg_down = weights("w_up", l, None), weights("w_down", l, None)
        g_down = _tn_rows(f"wgrad_down_{l}", s["act"], dxb, wg_down.shape[1], 2)
        zero = hand_over("w_down", l, g_down)
        dact = _nt_rows(f"dgrad_down_{l}", dxb, wg_down, 2)
        dup, d_cw, d_cb = _ffn_mid_bwd(f"ffn_mid_bwd_{l}", dact, s["up"], s["cw"], s["cb"] + zero)
        g_up = _tn_cols(f"wgrad_up_{l}", s["h2"], dup, _dup_spec_tn, wg_up.shape[2])
        zero = hand_over("w_up", l, g_up)
        dh2 = _nt_cols(f"dgrad_up_{l}", dup, _dup_spec, wg_up)
        dx, dxb, d_g2 = _rmsnorm_bwd(f"norm_ffn_bwd_{l}", dh2, s["x_mid"], s["g2"] + zero, dx)
        g_out = _tn_rows(f"wgrad_out_{l}", s["mixed"], dxb, wg_out.shape[1], 2)
        zero = hand_over("w_out", l, g_out)
        dmixed = _nt_rows(f"dgrad_out_{l}", dxb, wg_out, 2)
        doa, dob, doc, d_gain = _mix_bwd(f"mix_bwd_{l}", dmixed, oa, ob, oc, s["gain"] + zero)
        dqa, dka, dva = _attn_a_bwd(f"attn_a_bwd_{l}", qa, ka, va, oa, doa, mask_a)
        dqb, dkb, dvb, d_sink = _attn_b_bwd(f"attn_b_bwd_{l}", qb, kb, vb, ob, dob, s["sink"])
        dqc, dkc, dvc, d_bias = _attn_c_bwd(f"attn_c_bwd_{l}", qc, kc, vc, oc, doc, s["bias"])
        d_rpb = _rpb_reduce(f"rpb_reduce_{l}", d_bias)
        dproj = _rope_bwd(f"rope_bwd_{l}", (dqa, dka, dva, dqb, dkb, dvb, dqc, dkc, dvc), cos2, sin2)
        g_in = _tn_cols(f"wgrad_in_{l}", s["h1"], dproj,
                        lambda tm, nj: pl.BlockSpec((tm, nj), lambda j, kt, r: (0, j)), wg_in.shape[2])
        zero = hand_over("w_in", l, g_in)
        dh1 = _nt_cols(f"dgrad_in_{l}", dproj, lambda tm, nj: pl.BlockSpec((tm, nj), lambda kt, i, j: (i, j)), wg_in)
        dx, dxb, d_g1 = _rmsnorm_bwd(f"norm_attn_bwd_{l}", dh1, s["x"], s["g1"] + zero, dx)
        sgrads[l] = dict(ln_attn=d_g1[0], sink_b=d_sink[0, :HB], rpb_c=d_rpb, mix_gain=d_gain[0], ln_ffn=d_g2[0],
                         conv_w=d_cw.transpose(1, 0, 2).reshape(3, 2 * DFF), conv_b=d_cb.reshape(2 * DFF))
    return loss8[0, 0], dx, d_ln_final[0], sgrads


SMALL_NAMES = ("ln_attn", "sink_b", "rpb_c", "mix_gain", "ln_ffn", "conv_b")


def kernel(x, ln_attn, w_in, sink_b, rpb_c, mix_gain, w_out, ln_ffn, w_up, conv_w, conv_b, w_down, ln_final, loss_target, m_ln_attn, m_w_in, m_sink_b, m_rpb_c, m_mix_gain, m_w_out, m_ln_ffn, m_w_up, m_conv_w, m_conv_b, m_w_down, m_ln_final, v_ln_attn, v_w_in, v_sink_b, v_rpb_c, v_mix_gain, v_w_out, v_ln_ffn, v_w_up, v_conv_w, v_conv_b, v_w_down, v_ln_final):
    me = 4 * lax.axis_index("x") + 2 * lax.axis_index("y") + lax.axis_index("c")
    small = dict(ln_attn=ln_attn, sink_b=sink_b, rpb_c=rpb_c, mix_gain=mix_gain, ln_ffn=ln_ffn, conv_b=conv_b,
                 ln_final=ln_final)

    names = ("w_in", "w_out", "w_up", "w_down")
    shards = dict(w_in=w_in, w_out=w_out, w_up=w_up, w_down=w_down)
    keys = [(n, l) for l in range(DEPTH) for n in names]
    srcs = [shards[n][l].astype(BF16) for n, l in keys]
    nk = len(keys)
    started, token = _exchange_start("gather_weights_start", srcs, [(N_DEV,) + s.shape for s in srcs],
                                     _whole_src, _slot)
    gathered = {}

    def weights(n, l, after):
        if (n, l) not in gathered:
            k = keys.index((n, l))
            (src,), (land,) = _exchange_wait(f"gather_{n}_{l}_wait", tuple(started[k + i * nk] for i in range(4)),
                                             after, _whole_src, _slot)
            gathered[(n, l)] = lax.dynamic_update_slice_in_dim(land, src[None], me, axis=0)
        return gathered[(n, l)]

    pending = {}

    def hand_over(n, l, g):
        pending[(n, l)], tok = _exchange_start(f"send_grad_{n}_{l}", [g], [g.shape], _slot, _slot)
        return tok[0, 0]

    (cw_all,) = _all_gather("gather_conv_w", [_pack([conv_w])])
    nup = w_up.shape[2]
    cw_shards = cw_all.reshape(N_DEV, -1)[:, :DEPTH * 3 * nup].reshape(N_DEV, DEPTH, 3, nup)
    conv_w_full = cw_shards.transpose(1, 2, 0, 3).reshape(DEPTH, 3, N_DEV * nup)

    loss_local, dx, d_ln_final, sgrads = _local_step(
        x[0], loss_target[0], dict(small, ln_attn=ln_attn + token[0, 0]), weights, conv_w_full, hand_over)

    handles = [pending[k] for k in keys]
    sent, landed = _exchange_wait("recv_grads", tuple(h[i] for i in range(4) for h in handles), dx, _slot, _slot)
    parts = {k: lax.dynamic_update_slice_in_dim(land, lax.dynamic_slice_in_dim(src, me, 1, axis=0), me, axis=0)
             for k, src, land in zip(keys, sent, landed)}
    big = {}
    for n, (w, m, v) in zip(names, ((w_in, m_w_in, v_w_in), (w_out, m_w_out, v_w_out),
                                    (w_up, m_w_up, v_w_up), (w_down, m_w_down, v_w_down))):
        big[n] = _adamw_sharded(f"adamw_{n}", w, m, v, (parts[(n, 0)], parts[(n, 1)]))

    stacked = [jnp.stack([sgrads[l][n] for l in range(DEPTH)]) for n in SMALL_NAMES + ("conv_w",)] + [d_ln_final]
    shapes = [a.shape for a in stacked]
    (gathered,) = _all_gather("gather_small_grads", [_pack(stacked)])
    g_small = _unpack(_sum_devices("sum_small_grads", gathered), shapes)
    g = dict(zip(SMALL_NAMES + ("conv_w", "ln_final"), g_small))
    g["conv_w"] = lax.dynamic_slice_in_dim(g["conv_w"], me * nup, nup, axis=2)

    snames = SMALL_NAMES + ("conv_w", "ln_final")
    sw = dict(small, conv_w=conv_w)
    sm = dict(ln_attn=m_ln_attn, sink_b=m_sink_b, rpb_c=m_rpb_c, mix_gain=m_mix_gain, ln_ffn=m_ln_ffn,
              conv_b=m_conv_b, conv_w=m_conv_w, ln_final=m_ln_final)
    sv = dict(ln_attn=v_ln_attn, sink_b=v_sink_b, rpb_c=v_rpb_c, mix_gain=v_mix_gain, ln_ffn=v_ln_ffn,
              conv_b=v_conv_b, conv_w=v_conv_w, ln_final=v_ln_final)
    sshapes = [sw[n].shape for n in snames]
    packed = _adamw_small("adamw_small", _pack([sw[n] for n in snames]), _pack([g[n] for n in snames]),
                          _pack([sm[n] for n in snames]), _pack([sv[n] for n in snames]))
    s_delta, s_m, s_v = (dict(zip(snames, _unpack(buf, sshapes))) for buf in packed)

    loss = lax.psum(loss_local, ("x", "y", "c"))
    order = ("ln_attn", "w_in", "sink_b", "rpb_c", "mix_gain", "w_out", "ln_ffn", "w_up", "conv_w", "conv_b",
             "w_down", "ln_final")
    grads = [big[n][0] if n in big else g[n] for n in order]
    deltas = [big[n][1] if n in big else s_delta[n] for n in order]
    new_m = [big[n][2] if n in big else s_m[n] for n in order]
    new_v = [big[n][3] if n in big else s_v[n] for n in order]
    return (loss, dx[None], *grads, *deltas, *new_m, *new_v)
```

```python
import functools

import jax
import jax.numpy as jnp
from jax import lax
from jax.experimental import pallas as pl
from jax.experimental.pallas import tpu as pltpu

F32 = jnp.float32
BF16 = jnp.bfloat16

N_DEV = 8
T = 2048
D = 2048
DEPTH = 2
HD = 64
HA, HB, HKV, HC = 12, 10, 2, 10
WA, WB, WKV, WC = HA * HD, HB * HD, HKV * HD, HC * HD
IN_COLS = 3 * WA + WB + 2 * WKV + 3 * WC
DFF = 5632
GRID_W = 64
ROWS = T // GRID_W
NA_ROWS, NA_COLS = 8, 16
WINDOW_B = 128
EPS = 1e-6
NEG = -1e30
ROPE_THETA = 10000.0
LANE = 128
VMEM_LIMIT = 56 * 1024 * 1024

ADAM_LR, ADAM_B1, ADAM_B2, ADAM_EPS, ADAM_WD, ADAM_STEP = 0.001, 0.9, 0.999, 1e-08, 0.01, 10

GROUPS = (("qa", WA, True, True), ("ka", WA, True, False), ("va", WA, False, False),
          ("qb", WB, True, True), ("kb", WKV, True, False), ("vb", WKV, False, False),
          ("qc", WC, False, True), ("kc", WC, False, False), ("vc", WC, False, False))


def _params(sem=None):
    return pltpu.CompilerParams(dimension_semantics=sem, vmem_limit_bytes=VMEM_LIMIT)


def _exchange(name, srcs, out_shapes, src_of, dst_of):
    n = len(srcs)

    def body(*refs):
        ins, outs = refs[:n], refs[n:2 * n]
        send_sems, recv_sems, local_sems = refs[2 * n:]
        x, y, c = lax.axis_index("x"), lax.axis_index("y"), lax.axis_index("c")
        me = 4 * x + 2 * y + c
        sends, recvs, locals_ = [], [], []
        for a in range(n):
            keep = pltpu.make_async_copy(src_of(a, ins[a], me), dst_of(a, outs[a], me), local_sems.at[a])
            keep.start()
            locals_.append(keep)
            for p in range(1, N_DEV):
                px, py, pc = (x if not p & 4 else 1 - x), (y if not p & 2 else 1 - y), (c if not p & 1 else 1 - c)
                peer = 4 * px + 2 * py + pc
                send = pltpu.make_async_remote_copy(
                    src_ref=src_of(a, ins[a], peer), dst_ref=dst_of(a, outs[a], me),
                    send_sem=send_sems.at[a, p - 1], recv_sem=recv_sems.at[a, p - 1],
                    device_id=(px, py, pc), device_id_type=pl.DeviceIdType.MESH)
                send.start()
                sends.append(send)
                recvs.append(pltpu.make_async_remote_copy(
                    src_ref=src_of(a, ins[a], peer), dst_ref=dst_of(a, outs[a], peer),
                    send_sem=send_sems.at[a, p - 1], recv_sem=recv_sems.at[a, p - 1],
                    device_id=(px, py, pc), device_id_type=pl.DeviceIdType.MESH))
        for r in recvs:
            r.wait_recv()
        for s in sends:
            s.wait_send()
        for k in locals_:
            k.wait()

    hbm = pl.BlockSpec(memory_space=pl.ANY)
    return pl.pallas_call(
        body, name=name,
        out_shape=[jax.ShapeDtypeStruct(s, d) for s, d in out_shapes],
        in_specs=[hbm] * n, out_specs=[hbm] * n,
        scratch_shapes=[pltpu.SemaphoreType.DMA((n, N_DEV - 1)), pltpu.SemaphoreType.DMA((n, N_DEV - 1)),
                        pltpu.SemaphoreType.DMA((n,))],
    )(*srcs)


def _all_gather(name, shards):
    return _exchange(name, shards, [((N_DEV,) + s.shape, s.dtype) for s in shards],
                     lambda a, ref, peer: ref, lambda a, ref, origin: ref.at[origin])


HBM_SPEC = pl.BlockSpec(memory_space=pltpu.HBM)
SEM_SPEC = pl.BlockSpec(memory_space=pltpu.SEMAPHORE)
DATAFLOW = pltpu.SideEffectType.DATAFLOW_SIDE_EFFECTING


def _peer_copies(src_refs, land_refs, send_sems, recv_sems, src_of, dst_of):
    x, y, c = lax.axis_index("x"), lax.axis_index("y"), lax.axis_index("c")
    me = 4 * x + 2 * y + c
    sends, recvs = [], []
    for a in range(len(src_refs)):
        for p in range(1, N_DEV):
            px, py, pc = (x if not p & 4 else 1 - x), (y if not p & 2 else 1 - y), (c if not p & 1 else 1 - c)
            peer = 4 * px + 2 * py + pc
            for dst_slot, out in ((me, sends), (peer, recvs)):
                out.append(pltpu.make_async_remote_copy(
                    src_ref=src_of(src_refs[a], peer), dst_ref=dst_of(land_refs[a], dst_slot),
                    send_sem=send_sems[a].at[p - 1], recv_sem=recv_sems[a].at[p - 1],
                    device_id=(px, py, pc), device_id_type=pl.DeviceIdType.MESH))
    return sends, recvs


def _exchange_start(name, srcs, land_shapes, src_of, dst_of):
    n = len(srcs)

    def body(*refs):
        sends, _ = _peer_copies(refs[:n], refs[n:2 * n], refs[2 * n:3 * n], refs[3 * n:4 * n], src_of, dst_of)
        for s in sends:
            s.start()
        refs[6 * n][...] = jnp.zeros((8, LANE), F32)

    sems = [pltpu.SemaphoreType.DMA((N_DEV - 1,))] * (2 * n)
    thru = [pltpu.HBM(s.shape, s.dtype) for s in srcs] + [pltpu.HBM(shp, s.dtype) for shp, s in zip(land_shapes, srcs)]
    outs = pl.pallas_call(
        body, name=name, out_shape=sems + thru + [jax.ShapeDtypeStruct((8, LANE), F32)],
        in_specs=[HBM_SPEC] * (2 * n),
        out_specs=[SEM_SPEC] * (2 * n) + [HBM_SPEC] * (2 * n) + [pl.BlockSpec(memory_space=pltpu.VMEM)],
        input_output_aliases={i: 2 * n + i for i in range(2 * n)},
        compiler_params=pltpu.CompilerParams(has_side_effects=DATAFLOW),
    )(*[pltpu.with_memory_space_constraint(s, pltpu.HBM) for s in srcs],
      *[pltpu.with_memory_space_constraint(lax.empty(shp, s.dtype), pltpu.HBM) for shp, s in zip(land_shapes, srcs)])
    return tuple(outs[:4 * n]), outs[4 * n]


def _exchange_wait(name, handle, after, src_of, dst_of):
    n = len(handle) // 4
    sems, thru = handle[:2 * n], handle[2 * n:]

    def body(*refs):
        sends, recvs = _peer_copies(refs[:n], refs[n:2 * n], refs[2 * n:3 * n], refs[3 * n:4 * n], src_of, dst_of)
        for s in sends:
            s.wait_send()
        for r in recvs:
            r.wait_recv()

    outs = pl.pallas_call(
        body, name=name, out_shape=[pltpu.HBM(t.shape, t.dtype) for t in thru],
        in_specs=[HBM_SPEC] * (2 * n) + [SEM_SPEC] * (2 * n) + [pl.BlockSpec(memory_space=pl.ANY)],
        out_specs=[HBM_SPEC] * (2 * n), input_output_aliases={i: i for i in range(2 * n)},
        compiler_params=pltpu.CompilerParams(has_side_effects=DATAFLOW),
    )(*thru, *sems, after)
    return outs[:n], outs[n:]


def _whole_src(ref, peer):
    return ref


def _slot(ref, k):
    return ref.at[k]


def _flat2(v):
    return v.reshape(-1, v.shape[-1])


def _matmul(name, kind, a, a_spec, b, b_spec, out_shape, out_spec, grid, res=None, res_spec=None, acc_shape=None):
    dims = {"nn": (((1,), (0,)), ((), ())), "nt": (((1,), (1,)), ((), ())), "tn": (((0,), (0,)), ((), ()))}[kind]
    nred = grid[-1]

    def body(*refs):
        if res is None:
            a_ref, b_ref, o_ref = refs[:3]
            r_ref = None
        else:
            a_ref, b_ref, r_ref, o_ref = refs[:4]
        part = lax.dot_general(_flat2(a_ref[...]), _flat2(b_ref[...]), dims, preferred_element_type=F32)

        def finish(total):
            if r_ref is not None:
                total = total + r_ref[...]
            o_ref[...] = total.reshape(o_ref.shape).astype(o_ref.dtype)

        if nred == 1:
            finish(part)
        else:
            acc_ref = refs[-1]
            k = pl.program_id(len(grid) - 1)

            @pl.when(k == 0)
            def _():
                acc_ref[...] = part

            @pl.when(k > 0)
            def _():
                acc_ref[...] += part

            @pl.when(k == nred - 1)
            def _():
                finish(acc_ref[...])

    ins, specs = [a, b], [a_spec, b_spec]
    if res is not None:
        ins.append(res)
        specs.append(res_spec)
    scratch = [] if nred == 1 else [pltpu.VMEM(acc_shape, F32)]
    return pl.pallas_call(
        body, name=name, grid=grid, in_specs=specs, out_specs=out_spec, out_shape=out_shape, scratch_shapes=scratch,
        compiler_params=_params(("parallel",) * (len(grid) - 1) + ("arbitrary",)),
    )(*ins)


TM = 512


def _nn_cols(name, a, wg, out_dtype=F32):
    _, k, nj = wg.shape
    tm = 1024
    return _matmul(
        name, "nn", a, pl.BlockSpec((tm, k), lambda j, i, r: (i, 0)),
        wg, pl.BlockSpec((None, k, nj), lambda j, i, r: (j, 0, 0)),
        jax.ShapeDtypeStruct((T, N_DEV * nj), out_dtype), pl.BlockSpec((tm, nj), lambda j, i, r: (i, j)),
        (N_DEV, T // tm, 1))


def _nn_rows(name, a, wg, res, s, tn):
    _, kj, n = wg.shape
    tm = 1024
    return _matmul(
        name, "nn", a, pl.BlockSpec((tm, s * kj), lambda j, i, r: (i, r)),
        wg, pl.BlockSpec((s, kj, tn), lambda j, i, r: (r, 0, j)),
        jax.ShapeDtypeStruct((T, n), F32), pl.BlockSpec((tm, tn), lambda j, i, r: (i, j)),
        (n // tn, T // tm, N_DEV // s), res=res, res_spec=pl.BlockSpec((tm, tn), lambda j, i, r: (i, j)),
        acc_shape=(tm, tn))


def _nt_cols(name, dc, dc_spec_of, wg):
    _, k, nj = wg.shape
    tm = tk = 1024
    return _matmul(
        name, "nt", dc, dc_spec_of(tm, nj),
        wg, pl.BlockSpec((None, tk, nj), lambda kt, i, j: (j, kt, 0)),
        jax.ShapeDtypeStruct((T, k), F32), pl.BlockSpec((tm, tk), lambda kt, i, j: (i, kt)),
        (k // tk, T // tm, N_DEV), acc_shape=(tm, tk))


def _nt_rows(name, dc, wg, s):
    _, kj, n = wg.shape
    return _matmul(
        name, "nt", dc, pl.BlockSpec((TM, n), lambda kt, i, r: (i, 0)),
        wg, pl.BlockSpec((s, kj, n), lambda kt, i, r: (kt, 0, 0)),
        jax.ShapeDtypeStruct((T, N_DEV * kj), F32), pl.BlockSpec((TM, s * kj), lambda kt, i, r: (i, kt)),
        (N_DEV // s, T // TM, 1))


def _tn_cols(name, a, dc, dc_spec_of, nj):
    k = a.shape[1]
    tk = 512
    return _matmul(
        name, "tn", a, pl.BlockSpec((T, tk), lambda j, kt, r: (0, kt)),
        dc, dc_spec_of(T, nj),
        jax.ShapeDtypeStruct((N_DEV, k, nj), BF16), pl.BlockSpec((None, tk, nj), lambda j, kt, r: (j, kt, 0)),
        (N_DEV, k // tk, 1))


def _tn_rows(name, a, dc, kj, s):
    n = dc.shape[1]
    tn = 512
    return _matmul(
        name, "tn", a, pl.BlockSpec((T, s * kj), lambda kt, j, r: (0, kt)),
        dc, pl.BlockSpec((T, tn), lambda kt, j, r: (0, j)),
        jax.ShapeDtypeStruct((N_DEV, kj, n), BF16), pl.BlockSpec((s, kj, tn), lambda kt, j, r: (kt, 0, j)),
        (N_DEV // s, n // tn, 1))


TR = 256


def _rows(width):
    return pl.BlockSpec((TR, width), lambda i: (i, 0))


def _whole(shape):
    return pl.BlockSpec(shape, lambda i: (0,) * len(shape))


def _rmsnorm_fwd(name, x, g):
    def body(x_ref, g_ref, o_ref):
        xv = x_ref[...]
        r = lax.rsqrt(jnp.mean(xv * xv, axis=-1, keepdims=True) + EPS)
        o_ref[...] = ((xv * r) * g_ref[...]).astype(BF16)

    return pl.pallas_call(
        body, name=name, grid=(T // TR,), in_specs=[_rows(D), _whole((1, D))], out_specs=_rows(D),
        out_shape=jax.ShapeDtypeStruct((T, D), BF16), compiler_params=_params(("parallel",)),
    )(x, g)


def _rms_bwd_math(dy, xv, g):
    r = lax.rsqrt(jnp.mean(xv * xv, axis=-1, keepdims=True) + EPS)
    xhat = xv * r
    dxhat = dy * g
    dx = r * (dxhat - xhat * jnp.mean(dxhat * xhat, axis=-1, keepdims=True))
    return dx, dy * xhat


def _accumulate(ref, val):
    @pl.when(pl.program_id(0) == 0)
    def _():
        ref[...] = val

    @pl.when(pl.program_id(0) > 0)
    def _():
        ref[...] += val


def _rmsnorm_bwd(name, dy, x, g, res):
    def body(dy_ref, x_ref, g_ref, res_ref, dx_ref, dxb_ref, dg_ref):
        dx, dgr = _rms_bwd_math(dy_ref[...], x_ref[...], g_ref[...])
        tot = res_ref[...] + dx
        dx_ref[...] = tot
        dxb_ref[...] = tot.astype(BF16)
        _accumulate(dg_ref, jnp.sum(dgr, axis=0, keepdims=True))

    return pl.pallas_call(
        body, name=name, grid=(T // TR,), in_specs=[_rows(D), _rows(D), _whole((1, D)), _rows(D)],
        out_specs=[_rows(D), _rows(D), _whole((1, D))],
        out_shape=[jax.ShapeDtypeStruct((T, D), F32), jax.ShapeDtypeStruct((T, D), BF16),
                   jax.ShapeDtypeStruct((1, D), F32)],
        compiler_params=_params(("arbitrary",)),
    )(dy, x, g, res)


def _loss_head(x, g, target):
    def body(x_ref, g_ref, t_ref, loss_ref, dx_ref, dxb_ref, dg_ref):
        xv, gv = x_ref[...], g_ref[...]
        r = lax.rsqrt(jnp.mean(xv * xv, axis=-1, keepdims=True) + EPS)
        err = (xv * r) * gv - t_ref[...]
        part = 0.5 * jnp.sum(jnp.mean(err * err, axis=-1, keepdims=True))
        dx, dgr = _rms_bwd_math(err * (1.0 / D), xv, gv)
        dx_ref[...] = dx
        dxb_ref[...] = dx.astype(BF16)
        _accumulate(dg_ref, jnp.sum(dgr, axis=0, keepdims=True))
        _accumulate(loss_ref, jnp.full((8, LANE), part, F32))

    return pl.pallas_call(
        body, name="loss_head", grid=(T // TR,), in_specs=[_rows(D), _whole((1, D)), _rows(D)],
        out_specs=[_whole((8, LANE)), _rows(D), _rows(D), _whole((1, D))],
        out_shape=[jax.ShapeDtypeStruct((8, LANE), F32), jax.ShapeDtypeStruct((T, D), F32),
                   jax.ShapeDtypeStruct((T, D), BF16), jax.ShapeDtypeStruct((1, D), F32)],
        compiler_params=_params(("arbitrary",)),
    )(x, g, target)


MIX_OFFS = ((0, WA), (WA, WB), (WA + WB, WC))


def _mix_fwd(name, oa, ob, oc, gain):
    def body(oa_ref, ob_ref, oc_ref, g_ref, o_ref):
        for ref, (off, w) in zip((oa_ref, ob_ref, oc_ref), MIX_OFFS):
            o = ref[...]
            r = lax.rsqrt(jnp.mean(o * o, axis=-1, keepdims=True) + EPS)
            o_ref[:, off:off + w] = ((o * r) * g_ref[:, off:off + w]).astype(BF16)

    return pl.pallas_call(
        body, name=name, grid=(T // TR,), in_specs=[_rows(WA), _rows(WB), _rows(WC), _whole((1, D))],
        out_specs=_rows(D), out_shape=jax.ShapeDtypeStruct((T, D), BF16), compiler_params=_params(("parallel",)),
    )(oa, ob, oc, gain)


def _mix_bwd(name, dmixed, oa, ob, oc, gain):
    def body(dm_ref, oa_ref, ob_ref, oc_ref, g_ref, doa_ref, dob_ref, doc_ref, dg_ref):
        dgs = []
        for ref, dref, (off, w) in zip((oa_ref, ob_ref, oc_ref), (doa_ref, dob_ref, doc_ref), MIX_OFFS):
            dx, dgr = _rms_bwd_math(dm_ref[:, off:off + w], ref[...], g_ref[:, off:off + w])
            dref[...] = dx
            dgs.append(jnp.sum(dgr, axis=0, keepdims=True))
        _accumulate(dg_ref, jnp.concatenate(dgs, axis=1))

    return pl.pallas_call(
        body, name=name, grid=(T // TR,),
        in_specs=[_rows(D), _rows(WA), _rows(WB), _rows(WC), _whole((1, D))],
        out_specs=[_rows(WA), _rows(WB), _rows(WC), _whole((1, D))],
        out_shape=[jax.ShapeDtypeStruct((T, WA), F32), jax.ShapeDtypeStruct((T, WB), F32),
                   jax.ShapeDtypeStruct((T, WC), F32), jax.ShapeDtypeStruct((1, D), F32)],
        compiler_params=_params(("arbitrary",)),
    )(dmixed, oa, ob, oc, gain)


def _rope_tables():
    inv_freq = ROPE_THETA ** (-jnp.arange(0, HD, 2, dtype=F32) / HD)
    ang = jnp.arange(T, dtype=F32)[:, None] * inv_freq[None, :]
    cos, sin = jnp.cos(ang), jnp.sin(ang)
    cos2 = jnp.tile(jnp.concatenate([cos, cos], axis=1), (1, LANE // HD))
    sin2 = jnp.tile(jnp.concatenate([-sin, sin], axis=1), (1, LANE // HD))
    return cos2, sin2


def _rot_half(v):
    lane = lax.broadcasted_iota(jnp.int32, v.shape, 1)
    return jnp.where(lane % HD < HD // 2, pltpu.roll(v, LANE - HD // 2, 1), pltpu.roll(v, HD // 2, 1))


def _rope_fwd(name, proj, cos2, sin2):
    def body(p_ref, c_ref, s_ref, *outs):
        cv, sv = c_ref[...], s_ref[...]
        off = 0
        for o_ref, (_, w, rot, is_q) in zip(outs, GROUPS):
            for b in range(w // LANE):
                v = p_ref[:, off + b * LANE:off + (b + 1) * LANE]
                if rot:
                    v = v * cv + _rot_half(v) * sv
                if is_q:
                    v = v * (HD ** -0.5)
                o_ref[:, b * LANE:(b + 1) * LANE] = v.astype(BF16)
            off += w

    return pl.pallas_call(
        body, name=name, grid=(T // TR,), in_specs=[_rows(IN_COLS), _rows(LANE), _rows(LANE)],
        out_specs=[_rows(w) for _, w, _, _ in GROUPS],
        out_shape=[jax.ShapeDtypeStruct((T, w), BF16) for _, w, _, _ in GROUPS],
        compiler_params=_params(("parallel",)),
    )(proj, cos2, sin2)


def _rope_bwd(name, grads, cos2, sin2):
    def body(*refs):
        ins, (c_ref, s_ref, o_ref) = refs[:9], refs[9:]
        cv, sv = c_ref[...], s_ref[...]
        off = 0
        for d_ref, (_, w, rot, is_q) in zip(ins, GROUPS):
            for b in range(w // LANE):
                v = d_ref[:, b * LANE:(b + 1) * LANE]
                if is_q:
                    v = v * (HD ** -0.5)
                if rot:
                    v = v * cv + _rot_half(v * sv)
                o_ref[:, off + b * LANE:off + (b + 1) * LANE] = v.astype(BF16)
            off += w

    return pl.pallas_call(
        body, name=name, grid=(T // TR,), in_specs=[_rows(w) for _, w, _, _ in GROUPS] + [_rows(LANE), _rows(LANE)],
        out_specs=_rows(IN_COLS), out_shape=jax.ShapeDtypeStruct((T, IN_COLS), BF16),
        compiler_params=_params(("parallel",)),
    )(*grads, cos2, sin2)


NT_DIMS = (((1,), (1,)), ((), ()))
TN_DIMS = (((0,), (0,)), ((), ()))


def _scores(q, k, bias, valid):
    s = lax.dot_general(q, k, NT_DIMS, preferred_element_type=F32)
    if bias is not None:
        s = s + bias
    if valid is not None:
        s = jnp.where(valid, s, NEG)
    return s


def _head_fwd(q, k, v, bias=None, valid=None, sink=None):
    s = _scores(q, k, bias, valid)
    m = jnp.max(s, axis=1, keepdims=True)
    e = jnp.exp(s - m)
    l = jnp.sum(e, axis=1, keepdims=True)
    if sink is not None:
        l = l + jnp.exp(sink - m)
    return jnp.dot(e.astype(BF16), v, preferred_element_type=F32) / l, m + jnp.log(l)


def _head_bwd(q, k, v, o, do, lse, bias=None, valid=None, sink=None):
    p = jnp.exp(_scores(q, k, bias, valid) - lse)
    dob = do.astype(BF16)
    dp = lax.dot_general(dob, v, NT_DIMS, preferred_element_type=F32)
    delta = jnp.sum(do * o, axis=1, keepdims=True)
    ds = p * (dp - delta)
    dsb = ds.astype(BF16)
    dq = jnp.dot(dsb, k, preferred_element_type=F32)
    dk = lax.dot_general(dsb, q, TN_DIMS, preferred_element_type=F32)
    dv = lax.dot_general(p.astype(BF16), dob, TN_DIMS, preferred_element_type=F32)
    dsink = None if sink is None else -jnp.exp(sink - lse) * delta
    return dq, dk, dv, ds, dsink


def _per_head(cols):
    return jnp.concatenate([jnp.broadcast_to(c, (c.shape[0], HD)) for c in cols], axis=1)


def _dilation_bias():
    d = jnp.arange(T, dtype=jnp.int32)[:, None] - jnp.arange(T, dtype=jnp.int32)[None, :]
    ad = jnp.abs(d)
    count = jnp.zeros((T, T), jnp.int32)
    for window, r in ((128, 1), (512, 4), (2048, 16)):
        count += ((ad % r == 0) & (ad // r <= window // (2 * r))).astype(jnp.int32)
    return jnp.where(count > 0, jnp.log(jnp.maximum(count, 1).astype(F32)), NEG)


BQ_A = 256


def _attn_a_fwd(name, qa, ka, va, bias):
    def body(q_ref, k_ref, v_ref, b_ref, o_ref, lse_ref):
        b = b_ref[...]
        outs = [_head_fwd(q_ref[:, h * HD:(h + 1) * HD], k_ref[:, h * HD:(h + 1) * HD], v_ref[:, h * HD:(h + 1) * HD],
                          bias=b) for h in range(2)]
        o_ref[...] = jnp.concatenate([o for o, _ in outs], axis=1)
        lse_ref[...] = _per_head([lse for _, lse in outs])

    qs = pl.BlockSpec((BQ_A, LANE), lambda p, i: (i, p))
    ks = pl.BlockSpec((T, LANE), lambda p, i: (0, p))
    return pl.pallas_call(
        body, name=name, grid=(HA // 2, T // BQ_A),
        in_specs=[qs, ks, ks, pl.BlockSpec((BQ_A, T), lambda p, i: (i, 0))], out_specs=[qs, qs],
        out_shape=[jax.ShapeDtypeStruct((T, WA), F32)] * 2, compiler_params=_params(("parallel", "parallel")),
    )(qa, ka, va, bias)


def _attn_a_bwd(name, qa, ka, va, oa, lse, doa, bias):
    def body(q_ref, k_ref, v_ref, o_ref, lse_ref, do_ref, b_ref, dq_ref, dk_ref, dv_ref):
        b = b_ref[...]
        dqs, dks, dvs = [], [], []
        for h in range(2):
            sl = slice(h * HD, (h + 1) * HD)
            dq, dk, dv, _, _ = _head_bwd(q_ref[:, sl], k_ref[:, sl], v_ref[:, sl], o_ref[:, sl], do_ref[:, sl],
                                         lse_ref[:, h * HD:h * HD + 1], bias=b)
            dqs.append(dq)
            dks.append(dk)
            dvs.append(dv)
        dq_ref[...] = jnp.concatenate(dqs, axis=1)
        dk2, dv2 = jnp.concatenate(dks, axis=1), jnp.concatenate(dvs, axis=1)

        @pl.when(pl.program_id(1) == 0)
        def _():
            dk_ref[...] = dk2
            dv_ref[...] = dv2

        @pl.when(pl.program_id(1) > 0)
        def _():
            dk_ref[...] += dk2
            dv_ref[...] += dv2

    qs = pl.BlockSpec((BQ_A, LANE), lambda p, i: (i, p))
    ks = pl.BlockSpec((T, LANE), lambda p, i: (0, p))
    return pl.pallas_call(
        body, name=name, grid=(HA // 2, T // BQ_A),
        in_specs=[qs, ks, ks, qs, qs, qs, pl.BlockSpec((BQ_A, T), lambda p, i: (i, 0))], out_specs=[qs, ks, ks],
        out_shape=[jax.ShapeDtypeStruct((T, WA), F32)] * 3, compiler_params=_params(("parallel", "arbitrary")),
    )(qa, ka, va, oa, lse, doa, bias)


BQ_B = 128
SPAN_B = BQ_B + 2 * WINDOW_B


def _window_b(i):
    start = pl.multiple_of(jnp.clip(i * BQ_B - WINDOW_B, 0, T - SPAN_B), BQ_B)
    qpos = i * BQ_B + lax.broadcasted_iota(jnp.int32, (BQ_B, SPAN_B), 0)
    kpos = start + lax.broadcasted_iota(jnp.int32, (BQ_B, SPAN_B), 1)
    return start, jnp.abs(qpos - kpos) <= WINDOW_B


def _attn_b_fwd(name, qb, kb, vb, sink):
    def body(sink_ref, q_ref, k_ref, v_ref, o_ref, lse_ref):
        start, valid = _window_b(pl.program_id(0))
        kw, vw = k_ref[pl.ds(start, SPAN_B), :], v_ref[pl.ds(start, SPAN_B), :]
        outs = []
        for h in range(HB):
            kv = slice((h // (HB // HKV)) * HD, (h // (HB // HKV) + 1) * HD)
            outs.append(_head_fwd(q_ref[:, h * HD:(h + 1) * HD], kw[:, kv], vw[:, kv], valid=valid, sink=sink_ref[h]))
        o_ref[...] = jnp.concatenate([o for o, _ in outs], axis=1)
        lse_ref[...] = _per_head([lse for _, lse in outs])

    qs = pl.BlockSpec((BQ_B, WB), lambda i: (i, 0))
    return pl.pallas_call(
        body, name=name, grid=(T // BQ_B,),
        in_specs=[pl.BlockSpec(memory_space=pltpu.SMEM), qs, _whole((T, WKV)), _whole((T, WKV))],
        out_specs=[qs, qs],
        out_shape=[jax.ShapeDtypeStruct((T, WB), F32)] * 2, compiler_params=_params(("parallel",)),
    )(sink, qb, kb, vb)


def _attn_b_bwd(name, qb, kb, vb, ob, lse, dob, sink):
    def body(sink_ref, q_ref, k_ref, v_ref, o_ref, lse_ref, do_ref, dq_ref, dk_ref, dv_ref, dsink_ref):
        i = pl.program_id(0)
        start, valid = _window_b(i)
        kw, vw = k_ref[pl.ds(start, SPAN_B), :], v_ref[pl.ds(start, SPAN_B), :]
        lane = lax.broadcasted_iota(jnp.int32, (1, LANE), 1)
        dsink = jnp.zeros((1, LANE), F32)
        dqs, dks, dvs = [], [], []
        for g in range(HKV):
            kv = slice(g * HD, (g + 1) * HD)
            dk_g = jnp.zeros((SPAN_B, HD), F32)
            dv_g = jnp.zeros((SPAN_B, HD), F32)
            for h in range(g * (HB // HKV), (g + 1) * (HB // HKV)):
                sl = slice(h * HD, (h + 1) * HD)
                dq, dk, dv, _, dsr = _head_bwd(q_ref[:, sl], kw[:, kv], vw[:, kv], o_ref[:, sl], do_ref[:, sl],
                                               lse_ref[:, h * HD:h * HD + 1], valid=valid, sink=sink_ref[h])
                dqs.append(dq)
                dk_g += dk
                dv_g += dv
                dsink += jnp.where(lane == h, jnp.sum(dsr), 0.0)
            dks.append(dk_g)
            dvs.append(dv_g)
        dq_ref[...] = jnp.concatenate(dqs, axis=1)

        @pl.when(i == 0)
        def _():
            dk_ref[...] = jnp.zeros_like(dk_ref)
            dv_ref[...] = jnp.zeros_like(dv_ref)
            dsink_ref[...] = jnp.zeros_like(dsink_ref)

        dk_ref[pl.ds(start, SPAN_B), :] += jnp.concatenate(dks, axis=1)
        dv_ref[pl.ds(start, SPAN_B), :] += jnp.concatenate(dvs, axis=1)
        dsink_ref[...] += dsink

    qs = pl.BlockSpec((BQ_B, WB), lambda i: (i, 0))
    return pl.pallas_call(
        body, name=name, grid=(T // BQ_B,),
        in_specs=[pl.BlockSpec(memory_space=pltpu.SMEM), qs, _whole((T, WKV)), _whole((T, WKV)), qs, qs, qs],
        out_specs=[qs, _whole((T, WKV)), _whole((T, WKV)), _whole((1, LANE))],
        out_shape=[jax.ShapeDtypeStruct((T, WB), F32), jax.ShapeDtypeStruct((T, WKV), F32),
                   jax.ShapeDtypeStruct((T, WKV), F32), jax.ShapeDtypeStruct((1, LANE), F32)],
        compiler_params=_params(("arbitrary",)),
    )(sink, qb, kb, vb, ob, lse, dob)


SPAN_C = NA_ROWS * GRID_W


def _row_start(r):
    return jnp.clip(r - NA_ROWS // 2, 0, ROWS - NA_ROWS)


def _off_index(r):
    return _row_start(r) - r + (NA_ROWS - 1)


N_TAB = 16
RPS = 4


def _rpb_tables(rpb):
    w129 = jnp.concatenate([rpb[..., NA_COLS - 1:], jnp.zeros(rpb.shape[:2] + (129 - (2 * NA_COLS - 1),), F32),
                            rpb[..., :NA_COLS - 1]], axis=-1)
    toep = jnp.tile(w129, (1, 1, GRID_W))[..., :GRID_W * LANE].reshape(HC, 2 * NA_ROWS - 1, GRID_W, LANE)
    pairs = jnp.concatenate([toep[:, :-1, :, :GRID_W], toep[:, 1:, :, :GRID_W]], axis=-1)
    c = jnp.arange(GRID_W)[:, None]
    kc = jnp.arange(LANE)[None, :] % GRID_W
    cs = jnp.clip(c - NA_COLS // 2, 0, GRID_W - NA_COLS)
    pairs = jnp.where((kc >= cs) & (kc < cs + NA_COLS), pairs, NEG)
    return jnp.pad(pairs, ((0, 0), (0, N_TAB - pairs.shape[1]), (0, 0), (0, 0)))


def _bias_c(t_ref, h, d):
    return jnp.concatenate([t_ref[h, d + k] for k in range(0, NA_ROWS, 2)], axis=1)


def _attn_c_fwd(name, qc, kc, vc, tables):
    def body(q_ref, k_ref, v_ref, t_ref, o_ref, lse_ref):
        for rr in range(RPS):
            r = pl.program_id(1) * RPS + rr
            rows = slice(rr * GRID_W, (rr + 1) * GRID_W)
            start = pl.multiple_of(_row_start(r) * GRID_W, GRID_W)
            kw, vw = k_ref[pl.ds(start, SPAN_C), :], v_ref[pl.ds(start, SPAN_C), :]
            outs = [_head_fwd(q_ref[rows, h * HD:(h + 1) * HD], kw[:, h * HD:(h + 1) * HD], vw[:, h * HD:(h + 1) * HD],
                              bias=_bias_c(t_ref, h, _off_index(r))) for h in range(2)]
            o_ref[rows, :] = jnp.concatenate([o for o, _ in outs], axis=1)
            lse_ref[rows, :] = _per_head([lse for _, lse in outs])

    qs = pl.BlockSpec((RPS * GRID_W, LANE), lambda p, r: (r, p))
    ks = pl.BlockSpec((T, LANE), lambda p, r: (0, p))
    ts = pl.BlockSpec((2, N_TAB, GRID_W, LANE), lambda p, r: (p, 0, 0, 0))
    return pl.pallas_call(
        body, name=name, grid=(HC // 2, ROWS // RPS), in_specs=[qs, ks, ks, ts], out_specs=[qs, qs],
        out_shape=[jax.ShapeDtypeStruct((T, WC), F32)] * 2, compiler_params=_params(("parallel", "parallel")),
    )(qc, kc, vc, tables)


def _attn_c_bwd(name, qc, kc, vc, oc, lse, doc, tables):
    def body(q_ref, k_ref, v_ref, o_ref, lse_ref, do_ref, t_ref, dq_ref, dk_ref, dv_ref, dt_ref):
        @pl.when(pl.program_id(1) == 0)
        def _():
            dk_ref[...] = jnp.zeros_like(dk_ref)
            dv_ref[...] = jnp.zeros_like(dv_ref)
            dt_ref[...] = jnp.zeros_like(dt_ref)

        for rr in range(RPS):
            r = pl.program_id(1) * RPS + rr
            rows = slice(rr * GRID_W, (rr + 1) * GRID_W)
            d = _off_index(r)
            start = pl.multiple_of(_row_start(r) * GRID_W, GRID_W)
            kw, vw = k_ref[pl.ds(start, SPAN_C), :], v_ref[pl.ds(start, SPAN_C), :]
            dqs, dks, dvs = [], [], []
            for h in range(2):
                sl = slice(h * HD, (h + 1) * HD)
                dq, dk, dv, ds, _ = _head_bwd(q_ref[rows, sl], kw[:, sl], vw[:, sl], o_ref[rows, sl], do_ref[rows, sl],
                                              lse_ref[rows, h * HD:h * HD + 1], bias=_bias_c(t_ref, h, d))
                dqs.append(dq)
                dks.append(dk)
                dvs.append(dv)
                for k in range(0, NA_ROWS, 2):
                    dt_ref[h, d + k] += ds[:, k * GRID_W:(k + 2) * GRID_W]
            dq_ref[rows, :] = jnp.concatenate(dqs, axis=1)
            dk_ref[pl.ds(start, SPAN_C), :] += jnp.concatenate(dks, axis=1)
            dv_ref[pl.ds(start, SPAN_C), :] += jnp.concatenate(dvs, axis=1)

    qs = pl.BlockSpec((RPS * GRID_W, LANE), lambda p, r: (r, p))
    ks = pl.BlockSpec((T, LANE), lambda p, r: (0, p))
    ts = pl.BlockSpec((2, N_TAB, GRID_W, LANE), lambda p, r: (p, 0, 0, 0))
    return pl.pallas_call(
        body, name=name, grid=(HC // 2, ROWS // RPS), in_specs=[qs, ks, ks, qs, qs, qs, ts],
        out_specs=[qs, ks, ks, ts],
        out_shape=[jax.ShapeDtypeStruct((T, WC), F32)] * 3 + [jax.ShapeDtypeStruct((HC, N_TAB, GRID_W, LANE), F32)],
        compiler_params=_params(("parallel", "arbitrary")),
    )(qc, kc, vc, oc, lse, doc, tables)


def _split3(v):
    hi = v.astype(BF16)
    r1 = v - hi.astype(F32)
    mid = r1.astype(BF16)
    lo = (r1 - mid.astype(F32)).astype(BF16)
    return hi, mid, lo


def _rpb_reduce(name, dtables):
    x = dtables.reshape(HC, N_TAB, GRID_W * LANE)
    c = jnp.arange(GRID_W)[:, None]
    lane = jnp.arange(LANE)[None, :]
    col = (lane // GRID_W) * LANE + jnp.clip(lane % GRID_W - c + (NA_COLS - 1), 0, 2 * NA_COLS - 2)
    col_onehot = (col.reshape(-1)[:, None] == jnp.arange(2 * LANE)[None, :]).astype(BF16)
    a2 = jnp.arange(N_TAB)[None, :]
    row_onehot = jnp.concatenate([(jnp.arange(16)[:, None] == a2 + u) & (a2 < 2 * NA_ROWS - 2) for u in range(2)],
                                 axis=1).astype(BF16)

    def body(x_ref, e_ref, f_ref, o_ref):
        y = sum(jnp.dot(part, e_ref[...], preferred_element_type=F32) for part in _split3(x_ref[...]))
        z = jnp.concatenate([y[:, :LANE], y[:, LANE:]], axis=0)
        o_ref[...] = sum(jnp.dot(f_ref[...], part, preferred_element_type=F32) for part in _split3(z))

    out = pl.pallas_call(
        body, name=name, grid=(HC,),
        in_specs=[pl.BlockSpec((None, N_TAB, GRID_W * LANE), lambda h: (h, 0, 0)),
                  _whole((GRID_W * LANE, 2 * LANE)), _whole((16, 2 * N_TAB))],
        out_specs=pl.BlockSpec((None, 16, LANE), lambda h: (h, 0, 0)),
        out_shape=jax.ShapeDtypeStruct((HC, 16, LANE), F32), compiler_params=_params(("parallel",)),
    )(x, col_onehot, row_onehot)
    return out[:, :2 * NA_ROWS - 1, :2 * NA_COLS - 1]


TC = 128
NCB = DFF // TC


def _shift_down(v, rows):
    return jnp.where(rows == 0, 0.0, pltpu.roll(v, 1, 0))


def _shift_up(v, rows):
    return jnp.where(rows == T - 1, 0.0, pltpu.roll(v, T - 1, 0))


def _conv(v, w, b, rows):
    return _shift_down(v, rows) * w[0:1] + v * w[1:2] + _shift_up(v, rows) * w[2:3] + b


def _ffn_specs():
    gate = lambda shape: pl.BlockSpec(shape, lambda j: (0, j))
    val = lambda shape: pl.BlockSpec(shape, lambda j: (0, j + NCB))
    return [gate((T, TC)), val((T, TC)), gate((3, TC)), val((3, TC)), gate((1, TC)), val((1, TC))]


def _ffn_mid_fwd(name, up, conv_w, conv_b):
    def body(xg_ref, xv_ref, wg_ref, wv_ref, bg_ref, bv_ref, o_ref):
        rows = lax.broadcasted_iota(jnp.int32, (T, TC), 0)
        ug = _conv(xg_ref[...], wg_ref[...], bg_ref[...], rows)
        uv = _conv(xv_ref[...], wv_ref[...], bv_ref[...], rows)
        o_ref[...] = (ug * jax.nn.sigmoid(ug) * uv).astype(BF16)

    return pl.pallas_call(
        body, name=name, grid=(NCB,), in_specs=_ffn_specs(), out_specs=pl.BlockSpec((T, TC), lambda j: (0, j)),
        out_shape=jax.ShapeDtypeStruct((T, DFF), BF16), compiler_params=_params(("parallel",)),
    )(up, up, conv_w, conv_w, conv_b, conv_b)


def _ffn_mid_bwd(name, dact, up, conv_w, conv_b):
    def body(da_ref, xg_ref, xv_ref, wg_ref, wv_ref, bg_ref, bv_ref, dx_ref, dw_ref, db_ref):
        rows = lax.broadcasted_iota(jnp.int32, (T, TC), 0)
        xg, xv, wg, wv = xg_ref[...], xv_ref[...], wg_ref[...], wv_ref[...]
        ug = _conv(xg, wg, bg_ref[...], rows)
        uv = _conv(xv, wv, bv_ref[...], rows)
        sg = jax.nn.sigmoid(ug)
        da = da_ref[...]
        dug = da * uv * (sg * (1.0 + ug * (1.0 - sg)))
        duv = da * (ug * sg)
        for half, (xin, w, du) in enumerate(((xg, wg, dug), (xv, wv, duv))):
            dx = _shift_up(du, rows) * w[0:1] + du * w[1:2] + _shift_down(du, rows) * w[2:3]
            dx_ref[half] = dx.astype(BF16)
            dw_ref[half] = jnp.concatenate(
                [jnp.sum(_shift_down(xin, rows) * du, axis=0, keepdims=True), jnp.sum(xin * du, axis=0, keepdims=True),
                 jnp.sum(_shift_up(xin, rows) * du, axis=0, keepdims=True)], axis=0)
            db_ref[half] = jnp.sum(du, axis=0, keepdims=True)

    return pl.pallas_call(
        body, name=name, grid=(NCB,), in_specs=[pl.BlockSpec((T, TC), lambda j: (0, j))] + _ffn_specs(),
        out_specs=[pl.BlockSpec((2, T, TC), lambda j: (0, 0, j)), pl.BlockSpec((2, 3, TC), lambda j: (0, 0, j)),
                   pl.BlockSpec((2, 1, TC), lambda j: (0, 0, j))],
        out_shape=[jax.ShapeDtypeStruct((2, T, DFF), BF16), jax.ShapeDtypeStruct((2, 3, DFF), F32),
                   jax.ShapeDtypeStruct((2, 1, DFF), F32)],
        compiler_params=_params(("parallel",)),
    )(dact, up, up, conv_w, conv_w, conv_b, conv_b)


def _dup_spec(tm, nj):
    per = DFF // nj
    return pl.BlockSpec((None, tm, nj), lambda a, b, j: (j // per, 0 if tm == T else b, j % per))


def _dup_spec_tn(tm, nj):
    per = DFF // nj
    return pl.BlockSpec((None, tm, nj), lambda j, kt, r: (j // per, 0, j % per))


def _adamw_math(w, g, m, v):
    m = ADAM_B1 * m + (1.0 - ADAM_B1) * g
    v = ADAM_B2 * v + (1.0 - ADAM_B2) * (g * g)
    m_hat = m / (1.0 - ADAM_B1 ** ADAM_STEP)
    v_hat = v / (1.0 - ADAM_B2 ** ADAM_STEP)
    delta = -ADAM_LR * (m_hat / (jnp.sqrt(v_hat) + ADAM_EPS) + ADAM_WD * w)
    return delta, m, v


def _adamw_sharded(name, w, m, v, parts):
    _, r, c = w.shape
    tr = 64

    def body(w_ref, m_ref, v_ref, p0_ref, p1_ref, g_ref, d_ref, nm_ref, nv_ref):
        def run(p_ref):
            g = p_ref[0].astype(F32)
            for k in range(1, N_DEV):
                g = g + p_ref[k].astype(F32)
            d, nm, nv = _adamw_math(w_ref[...], g, m_ref[...], v_ref[...])
            g_ref[...] = g
            d_ref[...] = d
            nm_ref[...] = nm
            nv_ref[...] = nv

        @pl.when(pl.program_id(0) == 0)
        def _():
            run(p0_ref)

        @pl.when(pl.program_id(0) == 1)
        def _():
            run(p1_ref)

    ws = pl.BlockSpec((None, tr, c), lambda l, i: (l, i, 0))
    p0 = pl.BlockSpec((N_DEV, tr, c), lambda l, i: (0, jnp.where(l == 0, i, r // tr - 1), 0))
    p1 = pl.BlockSpec((N_DEV, tr, c), lambda l, i: (0, jnp.where(l == 1, i, 0), 0))
    return pl.pallas_call(
        body, name=name, grid=(DEPTH, r // tr), in_specs=[ws, ws, ws, p0, p1], out_specs=[ws] * 4,
        out_shape=[jax.ShapeDtypeStruct(w.shape, F32)] * 4, compiler_params=_params(("arbitrary", "arbitrary")),
    )(w, m, v, *parts)


def _sum_devices(name, parts):
    r = parts.shape[1]

    def body(p_ref, o_ref):
        g = p_ref[0]
        for k in range(1, N_DEV):
            g = g + p_ref[k]
        o_ref[...] = g

    return pl.pallas_call(
        body, name=name, in_specs=[pl.BlockSpec((N_DEV, r, LANE), lambda: (0, 0, 0))],
        out_specs=pl.BlockSpec((r, LANE), lambda: (0, 0)), out_shape=jax.ShapeDtypeStruct((r, LANE), F32),
        compiler_params=_params(),
    )(parts)


def _adamw_small(name, w, g, m, v):
    spec = pl.BlockSpec(w.shape, lambda: (0, 0))

    def body(w_ref, g_ref, m_ref, v_ref, d_ref, nm_ref, nv_ref):
        d, nm, nv = _adamw_math(w_ref[...], g_ref[...], m_ref[...], v_ref[...])
        d_ref[...] = d
        nm_ref[...] = nm
        nv_ref[...] = nv

    return pl.pallas_call(
        body, name=name, in_specs=[spec] * 4, out_specs=[spec] * 3,
        out_shape=[jax.ShapeDtypeStruct(w.shape, F32)] * 3, compiler_params=_params(),
    )(w, g, m, v)


def _pack(arrays):
    flat = jnp.concatenate([a.reshape(-1) for a in arrays])
    pad = (-flat.shape[0]) % (8 * LANE)
    return jnp.pad(flat, (0, pad)).reshape(-1, LANE)


def _unpack(buf, shapes):
    flat, out, off = buf.reshape(-1), [], 0
    for s in shapes:
        n = 1
        for d in s:
            n *= d
        out.append(flat[off:off + n].reshape(s))
        off += n
    return out


def _local_step(x, target, small, weights, conv_w_full, hand_over):
    cos2, sin2 = _rope_tables()
    bias_a = _dilation_bias()
    saved = []
    for l in range(DEPTH):
        g1, g2 = small["ln_attn"][l][None], small["ln_ffn"][l][None]
        gain, sink, cb = small["mix_gain"][l][None], small["sink_b"][l], small["conv_b"][l][None]
        cw = conv_w_full[l]
        bias = _rpb_tables(small["rpb_c"][l])
        h1 = _rmsnorm_fwd(f"norm_attn_{l}", x, g1)
        proj = _nn_cols(f"proj_in_{l}", h1, weights("w_in", l, h1))
        qa, ka, va, qb, kb, vb, qc, kc, vc = _rope_fwd(f"rope_{l}", proj, cos2, sin2)
        oa, lse_a = _attn_a_fwd(f"attn_a_{l}", qa, ka, va, bias_a)
        ob, lse_b = _attn_b_fwd(f"attn_b_{l}", qb, kb, vb, sink)
        oc, lse_c = _attn_c_fwd(f"attn_c_{l}", qc, kc, vc, bias)
        mixed = _mix_fwd(f"mix_{l}", oa, ob, oc, gain)
        x_mid = _nn_rows(f"proj_out_{l}", mixed, weights("w_out", l, mixed), x, 8, 512)
        h2 = _rmsnorm_fwd(f"norm_ffn_{l}", x_mid, g2)
        up = _nn_cols(f"ffn_up_{l}", h2, weights("w_up", l, h2))
        act = _ffn_mid_fwd(f"ffn_mid_{l}", up, cw, cb)
        x_out = _nn_rows(f"ffn_down_{l}", act, weights("w_down", l, act), x_mid, 2, 1024)
        saved.append(dict(x=x, h1=h1, qkv=(qa, ka, va, qb, kb, vb, qc, kc, vc), o=(oa, ob, oc), lse=(lse_a, lse_b, lse_c), mixed=mixed,
                          x_mid=x_mid, h2=h2, up=up, act=act, g1=g1, g2=g2, gain=gain, sink=sink, cb=cb, cw=cw, bias=bias))
        x = x_out

    loss8, dx, dxb, d_ln_final = _loss_head(x, small["ln_final"][None], target)
    sgrads = [None] * DEPTH
    for l in reversed(range(DEPTH)):
        s = saved[l]
        qa, ka, va, qb, kb, vb, qc, kc, vc = s["qkv"]
        oa, ob, oc = s["o"]
        wg_in, wg_out = weights("w_in", l, None), weights("w_out", l, None)
        wg_up, wg_down = weights("w_up", l, None), weights("w_down", l, None)
        g_down = _tn_rows(f"wgrad_down_{l}", s["act"], dxb, wg_down.shape[1], 2)
        zero = hand_over("w_down", l, g_down)
        dact = _nt_rows(f"dgrad_down_{l}", dxb, wg_down, 2)
        dup, d_cw, d_cb = _ffn_mid_bwd(f"ffn_mid_bwd_{l}", dact, s["up"], s["cw"], s["cb"] + zero)
        g_up = _tn_cols(f"wgrad_up_{l}", s["h2"], dup, _dup_spec_tn, wg_up.shape[2])
        zero = hand_over("w_up", l, g_up)
        dh2 = _nt_cols(f"dgrad_up_{l}", dup, _dup_spec, wg_up)
        dx, dxb, d_g2 = _rmsnorm_bwd(f"norm_ffn_bwd_{l}", dh2, s["x_mid"], s["g2"] + zero, dx)
        g_out = _tn_rows(f"wgrad_out_{l}", s["mixed"], dxb, wg_out.shape[1], 2)
        zero = hand_over("w_out", l, g_out)
        dmixed = _nt_rows(f"dgrad_out_{l}", dxb, wg_out, 2)
        doa, dob, doc, d_gain = _mix_bwd(f"mix_bwd_{l}", dmixed, oa, ob, oc, s["gain"] + zero)
        lse_a, lse_b, lse_c = s["lse"]
        dqa, dka, dva = _attn_a_bwd(f"attn_a_bwd_{l}", qa, ka, va, oa, lse_a, doa, bias_a)
        dqb, dkb, dvb, d_sink = _attn_b_bwd(f"attn_b_bwd_{l}", qb, kb, vb, ob, lse_b, dob, s["sink"])
        dqc, dkc, dvc, d_bias = _attn_c_bwd(f"attn_c_bwd_{l}", qc, kc, vc, oc, lse_c, doc, s["bias"])
        d_rpb = _rpb_reduce(f"rpb_reduce_{l}", d_bias)
        dproj = _rope_bwd(f"rope_bwd_{l}", (dqa, dka, dva, dqb, dkb, dvb, dqc, dkc, dvc), cos2, sin2)
        g_in = _tn_cols(f"wgrad_in_{l}", s["h1"], dproj,
                        lambda tm, nj: pl.BlockSpec((tm, nj), lambda j, kt, r: (0, j)), wg_in.shape[2])
        zero = hand_over("w_in", l, g_in)
        dh1 = _nt_cols(f"dgrad_in_{l}", dproj, lambda tm, nj: pl.BlockSpec((tm, nj), lambda kt, i, j: (i, j)), wg_in)
        dx, dxb, d_g1 = _rmsnorm_bwd(f"norm_attn_bwd_{l}", dh1, s["x"], s["g1"] + zero, dx)
        sgrads[l] = dict(ln_attn=d_g1[0], sink_b=d_sink[0, :HB], rpb_c=d_rpb, mix_gain=d_gain[0], ln_ffn=d_g2[0],
                         conv_w=d_cw.transpose(1, 0, 2).reshape(3, 2 * DFF), conv_b=d_cb.reshape(2 * DFF))
    return loss8[0, 0], dx, d_ln_final[0], sgrads


SMALL_NAMES = ("ln_attn", "sink_b", "rpb_c", "mix_gain", "ln_ffn", "conv_b")


def kernel(x, ln_attn, w_in, sink_b, rpb_c, mix_gain, w_out, ln_ffn, w_up, conv_w, conv_b, w_down, ln_final, loss_target, m_ln_attn, m_w_in, m_sink_b, m_rpb_c, m_mix_gain, m_w_out, m_ln_ffn, m_w_up, m_conv_w, m_conv_b, m_w_down, m_ln_final, v_ln_attn, v_w_in, v_sink_b, v_rpb_c, v_mix_gain, v_w_out, v_ln_ffn, v_w_up, v_conv_w, v_conv_b, v_w_down, v_ln_final):
    me = 4 * lax.axis_index("x") + 2 * lax.axis_index("y") + lax.axis_index("c")
    small = dict(ln_attn=ln_attn, sink_b=sink_b, rpb_c=rpb_c, mix_gain=mix_gain, ln_ffn=ln_ffn, conv_b=conv_b,
                 ln_final=ln_final)

    names = ("w_in", "w_out", "w_up", "w_down")
    shards = dict(w_in=w_in, w_out=w_out, w_up=w_up, w_down=w_down)
    keys = [("conv_w", 0)] + [(n, l) for l in range(DEPTH) for n in names]
    srcs = [_pack([conv_w])] + [shards[n][l].astype(BF16) for n, l in keys[1:]]
    nk = len(keys)
    started, token = _exchange_start("gather_weights_start", srcs, [(N_DEV,) + s.shape for s in srcs],
                                     _whole_src, _slot)
    gathered = {}

    def weights(n, l, after):
        if (n, l) not in gathered:
            k = keys.index((n, l))
            (src,), (land,) = _exchange_wait(f"gather_{n}_{l}_wait", tuple(started[k + i * nk] for i in range(4)),
                                             after, _whole_src, _slot)
            gathered[(n, l)] = lax.dynamic_update_slice_in_dim(land, src[None], me, axis=0)
        return gathered[(n, l)]

    pending = {}

    def hand_over(n, l, g):
        pending[(n, l)], tok = _exchange_start(f"send_grad_{n}_{l}", [g], [g.shape], _slot, _slot)
        return tok[0, 0]

    cw_all = weights("conv_w", 0, token)
    nup = w_up.shape[2]
    cw_shards = cw_all.reshape(N_DEV, -1)[:, :DEPTH * 3 * nup].reshape(N_DEV, DEPTH, 3, nup)
    conv_w_full = cw_shards.transpose(1, 2, 0, 3).reshape(DEPTH, 3, N_DEV * nup)

    loss_local, dx, d_ln_final, sgrads = _local_step(
        x[0], loss_target[0], dict(small, ln_attn=ln_attn + token[0, 0]), weights, conv_w_full, hand_over)

    handles = [pending[k] for k in keys[1:]]
    sent, landed = _exchange_wait("recv_grads", tuple(h[i] for i in range(4) for h in handles), dx, _slot, _slot)
    parts = {k: lax.dynamic_update_slice_in_dim(land, lax.dynamic_slice_in_dim(src, me, 1, axis=0), me, axis=0)
             for k, src, land in zip(keys[1:], sent, landed)}
    big = {}
    for n, (w, m, v) in zip(names, ((w_in, m_w_in, v_w_in), (w_out, m_w_out, v_w_out),
                                    (w_up, m_w_up, v_w_up), (w_down, m_w_down, v_w_down))):
        big[n] = _adamw_sharded(f"adamw_{n}", w, m, v, (parts[(n, 0)], parts[(n, 1)]))

    stacked = [jnp.stack([sgrads[l][n] for l in range(DEPTH)]) for n in SMALL_NAMES + ("conv_w",)] + [d_ln_final]
    shapes = [a.shape for a in stacked]
    (gathered,) = _all_gather("gather_small_grads", [_pack(stacked)])
    g_small = _unpack(_sum_devices("sum_small_grads", gathered), shapes)
    g = dict(zip(SMALL_NAMES + ("conv_w", "ln_final"), g_small))
    g["conv_w"] = lax.dynamic_slice_in_dim(g["conv_w"], me * nup, nup, axis=2)

    snames = SMALL_NAMES + ("conv_w", "ln_final")
    sw = dict(small, conv_w=conv_w)
    sm = dict(ln_attn=m_ln_attn, sink_b=m_sink_b, rpb_c=m_rpb_c, mix_gain=m_mix_gain, ln_ffn=m_ln_ffn,
              conv_b=m_conv_b, conv_w=m_conv_w, ln_final=m_ln_final)
    sv = dict(ln_attn=v_ln_attn, sink_b=v_sink_b, rpb_c=v_rpb_c, mix_gain=v_mix_gain, ln_ffn=v_ln_ffn,
              conv_b=v_conv_b, conv_w=v_conv_w, ln_final=v_ln_final)
    sshapes = [sw[n].shape for n in snames]
    packed = _adamw_small("adamw_small", _pack([sw[n] for n in snames]), _pack([g[n] for n in snames]),
                          _pack([sm[n] for n in snames]), _pack([sv[n] for n in snames]))
    s_delta, s_m, s_v = (dict(zip(snames, _unpack(buf, sshapes))) for buf in packed)

    loss = lax.psum(loss_local, ("x", "y", "c"))
    order = ("ln_attn", "w_in", "sink_b", "rpb_c", "mix_gain", "w_out", "ln_ffn", "w_up", "conv_w", "conv_b",
             "w_down", "ln_final")
    grads = [big[n][0] if n in big else g[n] for n in order]
    deltas = [big[n][1] if n in big else s_delta[n] for n in order]
    new_m = [big[n][2] if n in big else s_m[n] for n in order]
    new_v = [big[n][3] if n in big else s_v[n] for n in order]
    return (loss, dx[None], *grads, *deltas, *new_m, *new_v)
```

```python
import functools

import jax
import jax.numpy as jnp
from jax import lax
from jax.experimental import pallas as pl
from jax.experimental.pallas import tpu as pltpu

F32 = jnp.float32
BF16 = jnp.bfloat16

N_DEV = 8
T = 2048
D = 2048
DEPTH = 2
HD = 64
HA, HB, HKV, HC = 12, 10, 2, 10
WA, WB, WKV, WC = HA * HD, HB * HD, HKV * HD, HC * HD
IN_COLS = 3 * WA + WB + 2 * WKV + 3 * WC
DFF = 5632
GRID_W = 64
ROWS = T // GRID_W
NA_ROWS, NA_COLS = 8, 16
WINDOW_B = 128
EPS = 1e-6
NEG = -1e30
ROPE_THETA = 10000.0
LANE = 128
VMEM_LIMIT = 56 * 1024 * 1024

ADAM_LR, ADAM_B1, ADAM_B2, ADAM_EPS, ADAM_WD, ADAM_STEP = 0.001, 0.9, 0.999, 1e-08, 0.01, 10

GROUPS = (("qa", WA, True, True), ("ka", WA, True, False), ("va", WA, False, False),
          ("qb", WB, True, True), ("kb", WKV, True, False), ("vb", WKV, False, False),
          ("qc", WC, False, True), ("kc", WC, False, False), ("vc", WC, False, False))


def _params(sem=None):
    return pltpu.CompilerParams(dimension_semantics=sem, vmem_limit_bytes=VMEM_LIMIT)


HBM_SPEC = pl.BlockSpec(memory_space=pltpu.HBM)
SEM_SPEC = pl.BlockSpec(memory_space=pltpu.SEMAPHORE)
DATAFLOW = pltpu.SideEffectType.DATAFLOW_SIDE_EFFECTING


ALL_PEERS = tuple((p >> 2 & 1, p >> 1 & 1, p & 1) for p in range(1, N_DEV))
OTHER_CHIPS = ((1, 0, 0), (0, 1, 0), (1, 1, 0))
NEAR_PEERS = ((0, 0, 1),) + OTHER_CHIPS


def _flip(x, y, c, f):
    return (1 - x if f[0] else x, 1 - y if f[1] else y, 1 - c if f[2] else c)


def _index(pos):
    return 4 * pos[0] + 2 * pos[1] + pos[2]


def _descriptors(plan, bufs, send_sems, recv_sems):
    x, y, c = lax.axis_index("x"), lax.axis_index("y"), lax.axis_index("c")
    return [pltpu.make_async_remote_copy(src_ref=src, dst_ref=dst, send_sem=send_sems[g].at[i],
                                         recv_sem=recv_sems[g].at[i], device_id=partner,
                                         device_id_type=pl.DeviceIdType.MESH)
            for g, copies in enumerate(plan(bufs, x, y, c)) for i, (src, dst, partner) in enumerate(copies)]


def _copy_start(name, bufs, plan, sizes):
    nb, ng = len(bufs), len(sizes)

    def body(*refs):
        for d in _descriptors(plan, refs[:nb], refs[nb:nb + ng], refs[nb + ng:nb + 2 * ng]):
            d.start()
        refs[2 * nb + 2 * ng][...] = jnp.zeros((8, LANE), F32)

    outs = pl.pallas_call(
        body, name=name,
        out_shape=[pltpu.SemaphoreType.DMA((s,)) for s in sizes] * 2 + [pltpu.HBM(b.shape, b.dtype) for b in bufs]
        + [jax.ShapeDtypeStruct((8, LANE), F32)],
        in_specs=[HBM_SPEC] * nb,
        out_specs=[SEM_SPEC] * (2 * ng) + [HBM_SPEC] * nb + [pl.BlockSpec(memory_space=pltpu.VMEM)],
        input_output_aliases={i: 2 * ng + i for i in range(nb)},
        compiler_params=pltpu.CompilerParams(has_side_effects=DATAFLOW),
    )(*[pltpu.with_memory_space_constraint(b, pltpu.HBM) for b in bufs])
    return outs[:ng], outs[ng:2 * ng], outs[2 * ng:2 * ng + nb], outs[2 * ng + nb]


def _copy_wait(name, bufs, send_sems, recv_sems, plan, after):
    nb, ng = len(bufs), len(send_sems)

    def body(*refs):
        for d in _descriptors(plan, refs[:nb], refs[nb:nb + ng], refs[nb + ng:nb + 2 * ng]):
            d.wait_send()
            d.wait_recv()

    return pl.pallas_call(
        body, name=name, out_shape=[pltpu.HBM(b.shape, b.dtype) for b in bufs],
        in_specs=[HBM_SPEC] * nb + [SEM_SPEC] * (2 * ng) + [pl.BlockSpec(memory_space=pl.ANY)],
        out_specs=[HBM_SPEC] * nb, input_output_aliases={i: i for i in range(nb)},
        compiler_params=pltpu.CompilerParams(has_side_effects=DATAFLOW),
    )(*bufs, *send_sems, *recv_sems, after)


def _gather_plan(peer_sets):
    def plan(bufs, x, y, c):
        n = len(peer_sets)
        return [[(bufs[i], bufs[n + i].at[_index((x, y, c))], _flip(x, y, c, f)) for f in peers]
                for i, peers in enumerate(peer_sets)]
    return plan


def _forward_plan(bufs, x, y, c):
    slots = [_index(_flip(x, y, c, f)) for f in OTHER_CHIPS]
    return [[(bufs[0].at[s], bufs[0].at[s], _flip(x, y, c, (0, 0, 1))) for s in slots]]


def _scatter_plan(bufs, x, y, c):
    peers = [_flip(x, y, c, f) for f in ALL_PEERS]
    return [[(bufs[0].at[_index(p)], bufs[1].at[_index((x, y, c))], p) for p in peers]]


def _flat2(v):
    return v.reshape(-1, v.shape[-1])


def _matmul(name, kind, a, a_spec, b, b_spec, out_shape, out_spec, grid, res=None, res_spec=None, acc_shape=None):
    dims = {"nn": (((1,), (0,)), ((), ())), "nt": (((1,), (1,)), ((), ())), "tn": (((0,), (0,)), ((), ()))}[kind]
    nred = grid[-1]

    def body(*refs):
        if res is None:
            a_ref, b_ref, o_ref = refs[:3]
            r_ref = None
        else:
            a_ref, b_ref, r_ref, o_ref = refs[:4]
        part = lax.dot_general(_flat2(a_ref[...]), _flat2(b_ref[...]), dims, preferred_element_type=F32)

        def finish(total):
            if r_ref is not None:
                total = total + r_ref[...]
            o_ref[...] = total.reshape(o_ref.shape).astype(o_ref.dtype)

        if nred == 1:
            finish(part)
        else:
            acc_ref = refs[-1]
            k = pl.program_id(len(grid) - 1)

            @pl.when(k == 0)
            def _():
                acc_ref[...] = part

            @pl.when(k > 0)
            def _():
                acc_ref[...] += part

            @pl.when(k == nred - 1)
            def _():
                finish(acc_ref[...])

    ins, specs = [a, b], [a_spec, b_spec]
    if res is not None:
        ins.append(res)
        specs.append(res_spec)
    scratch = [] if nred == 1 else [pltpu.VMEM(acc_shape, F32)]
    return pl.pallas_call(
        body, name=name, grid=grid, in_specs=specs, out_specs=out_spec, out_shape=out_shape, scratch_shapes=scratch,
        compiler_params=_params(("parallel",) * (len(grid) - 1) + ("arbitrary",)),
    )(*ins)


TM = 512


def _nn_cols(name, a, wg, out_dtype=F32):
    _, k, nj = wg.shape
    tm = 1024
    return _matmul(
        name, "nn", a, pl.BlockSpec((tm, k), lambda j, i, r: (i, 0)),
        wg, pl.BlockSpec((None, k, nj), lambda j, i, r: (j, 0, 0)),
        jax.ShapeDtypeStruct((T, N_DEV * nj), out_dtype), pl.BlockSpec((tm, nj), lambda j, i, r: (i, j)),
        (N_DEV, T // tm, 1))


def _nn_rows(name, a, wg, res, s, tn):
    _, kj, n = wg.shape
    tm = 1024
    return _matmul(
        name, "nn", a, pl.BlockSpec((tm, s * kj), lambda j, i, r: (i, r)),
        wg, pl.BlockSpec((s, kj, tn), lambda j, i, r: (r, 0, j)),
        jax.ShapeDtypeStruct((T, n), F32), pl.BlockSpec((tm, tn), lambda j, i, r: (i, j)),
        (n // tn, T // tm, N_DEV // s), res=res, res_spec=pl.BlockSpec((tm, tn), lambda j, i, r: (i, j)),
        acc_shape=(tm, tn))


def _nt_cols(name, dc, dc_spec_of, wg):
    _, k, nj = wg.shape
    tm = tk = 1024
    return _matmul(
        name, "nt", dc, dc_spec_of(tm, nj),
        wg, pl.BlockSpec((None, tk, nj), lambda kt, i, j: (j, kt, 0)),
        jax.ShapeDtypeStruct((T, k), F32), pl.BlockSpec((tm, tk), lambda kt, i, j: (i, kt)),
        (k // tk, T // tm, N_DEV), acc_shape=(tm, tk))


def _nt_rows(name, dc, wg, s):
    _, kj, n = wg.shape
    return _matmul(
        name, "nt", dc, pl.BlockSpec((TM, n), lambda kt, i, r: (i, 0)),
        wg, pl.BlockSpec((s, kj, n), lambda kt, i, r: (kt, 0, 0)),
        jax.ShapeDtypeStruct((T, N_DEV * kj), F32), pl.BlockSpec((TM, s * kj), lambda kt, i, r: (i, kt)),
        (N_DEV // s, T // TM, 1))


def _tn_cols(name, a, dc, dc_spec_of, nj):
    k = a.shape[1]
    tk = 512
    return _matmul(
        name, "tn", a, pl.BlockSpec((T, tk), lambda j, kt, r: (0, kt)),
        dc, dc_spec_of(T, nj),
        jax.ShapeDtypeStruct((N_DEV, k, nj), BF16), pl.BlockSpec((None, tk, nj), lambda j, kt, r: (j, kt, 0)),
        (N_DEV, k // tk, 1))


def _tn_rows(name, a, dc, kj, s):
    n = dc.shape[1]
    tn = 512
    return _matmul(
        name, "tn", a, pl.BlockSpec((T, s * kj), lambda kt, j, r: (0, kt)),
        dc, pl.BlockSpec((T, tn), lambda kt, j, r: (0, j)),
        jax.ShapeDtypeStruct((N_DEV, kj, n), BF16), pl.BlockSpec((s, kj, tn), lambda kt, j, r: (kt, 0, j)),
        (N_DEV // s, n // tn, 1))


TR = 256


def _rows(width):
    return pl.BlockSpec((TR, width), lambda i: (i, 0))


def _whole(shape):
    return pl.BlockSpec(shape, lambda i: (0,) * len(shape))


def _rmsnorm_fwd(name, x, g):
    def body(x_ref, g_ref, o_ref):
        xv = x_ref[...]
        r = lax.rsqrt(jnp.mean(xv * xv, axis=-1, keepdims=True) + EPS)
        o_ref[...] = ((xv * r) * g_ref[...]).astype(BF16)

    return pl.pallas_call(
        body, name=name, grid=(T // TR,), in_specs=[_rows(D), _whole((1, D))], out_specs=_rows(D),
        out_shape=jax.ShapeDtypeStruct((T, D), BF16), compiler_params=_params(("parallel",)),
    )(x, g)


def _rms_bwd_math(dy, xv, g):
    r = lax.rsqrt(jnp.mean(xv * xv, axis=-1, keepdims=True) + EPS)
    xhat = xv * r
    dxhat = dy * g
    dx = r * (dxhat - xhat * jnp.mean(dxhat * xhat, axis=-1, keepdims=True))
    return dx, dy * xhat


def _accumulate(ref, val):
    @pl.when(pl.program_id(0) == 0)
    def _():
        ref[...] = val

    @pl.when(pl.program_id(0) > 0)
    def _():
        ref[...] += val


def _rmsnorm_bwd(name, dy, x, g, res):
    def body(dy_ref, x_ref, g_ref, res_ref, dx_ref, dxb_ref, dg_ref):
        dx, dgr = _rms_bwd_math(dy_ref[...], x_ref[...], g_ref[...])
        tot = res_ref[...] + dx
        dx_ref[...] = tot
        dxb_ref[...] = tot.astype(BF16)
        _accumulate(dg_ref, jnp.sum(dgr, axis=0, keepdims=True))

    return pl.pallas_call(
        body, name=name, grid=(T // TR,), in_specs=[_rows(D), _rows(D), _whole((1, D)), _rows(D)],
        out_specs=[_rows(D), _rows(D), _whole((1, D))],
        out_shape=[jax.ShapeDtypeStruct((T, D), F32), jax.ShapeDtypeStruct((T, D), BF16),
                   jax.ShapeDtypeStruct((1, D), F32)],
        compiler_params=_params(("arbitrary",)),
    )(dy, x, g, res)


def _loss_head(x, g, target):
    def body(x_ref, g_ref, t_ref, loss_ref, dx_ref, dxb_ref, dg_ref):
        xv, gv = x_ref[...], g_ref[...]
        r = lax.rsqrt(jnp.mean(xv * xv, axis=-1, keepdims=True) + EPS)
        err = (xv * r) * gv - t_ref[...]
        part = 0.5 * jnp.sum(jnp.mean(err * err, axis=-1, keepdims=True))
        dx, dgr = _rms_bwd_math(err * (1.0 / D), xv, gv)
        dx_ref[...] = dx
        dxb_ref[...] = dx.astype(BF16)
        _accumulate(dg_ref, jnp.sum(dgr, axis=0, keepdims=True))
        _accumulate(loss_ref, jnp.full((8, LANE), part, F32))

    return pl.pallas_call(
        body, name="loss_head", grid=(T // TR,), in_specs=[_rows(D), _whole((1, D)), _rows(D)],
        out_specs=[_whole((8, LANE)), _rows(D), _rows(D), _whole((1, D))],
        out_shape=[jax.ShapeDtypeStruct((8, LANE), F32), jax.ShapeDtypeStruct((T, D), F32),
                   jax.ShapeDtypeStruct((T, D), BF16), jax.ShapeDtypeStruct((1, D), F32)],
        compiler_params=_params(("arbitrary",)),
    )(x, g, target)


MIX_OFFS = ((0, WA), (WA, WB), (WA + WB, WC))


def _mix_fwd(name, oa, ob, oc, gain):
    def body(oa_ref, ob_ref, oc_ref, g_ref, o_ref):
        for ref, (off, w) in zip((oa_ref, ob_ref, oc_ref), MIX_OFFS):
            o = ref[...]
            r = lax.rsqrt(jnp.mean(o * o, axis=-1, keepdims=True) + EPS)
            o_ref[:, off:off + w] = ((o * r) * g_ref[:, off:off + w]).astype(BF16)

    return pl.pallas_call(
        body, name=name, grid=(T // TR,), in_specs=[_rows(WA), _rows(WB), _rows(WC), _whole((1, D))],
        out_specs=_rows(D), out_shape=jax.ShapeDtypeStruct((T, D), BF16), compiler_params=_params(("parallel",)),
    )(oa, ob, oc, gain)


def _mix_bwd(name, dmixed, oa, ob, oc, gain):
    def body(dm_ref, oa_ref, ob_ref, oc_ref, g_ref, doa_ref, dob_ref, doc_ref, dg_ref):
        dgs = []
        for ref, dref, (off, w) in zip((oa_ref, ob_ref, oc_ref), (doa_ref, dob_ref, doc_ref), MIX_OFFS):
            dx, dgr = _rms_bwd_math(dm_ref[:, off:off + w], ref[...], g_ref[:, off:off + w])
            dref[...] = dx
            dgs.append(jnp.sum(dgr, axis=0, keepdims=True))
        _accumulate(dg_ref, jnp.concatenate(dgs, axis=1))

    return pl.pallas_call(
        body, name=name, grid=(T // TR,),
        in_specs=[_rows(D), _rows(WA), _rows(WB), _rows(WC), _whole((1, D))],
        out_specs=[_rows(WA), _rows(WB), _rows(WC), _whole((1, D))],
        out_shape=[jax.ShapeDtypeStruct((T, WA), F32), jax.ShapeDtypeStruct((T, WB), F32),
                   jax.ShapeDtypeStruct((T, WC), F32), jax.ShapeDtypeStruct((1, D), F32)],
        compiler_params=_params(("arbitrary",)),
    )(dmixed, oa, ob, oc, gain)


def _rope_tables():
    inv_freq = ROPE_THETA ** (-jnp.arange(0, HD, 2, dtype=F32) / HD)
    ang = jnp.arange(T, dtype=F32)[:, None] * inv_freq[None, :]
    cos, sin = jnp.cos(ang), jnp.sin(ang)
    cos2 = jnp.tile(jnp.concatenate([cos, cos], axis=1), (1, LANE // HD))
    sin2 = jnp.tile(jnp.concatenate([-sin, sin], axis=1), (1, LANE // HD))
    return cos2, sin2


def _rot_half(v):
    lane = lax.broadcasted_iota(jnp.int32, v.shape, 1)
    return jnp.where(lane % HD < HD // 2, pltpu.roll(v, LANE - HD // 2, 1), pltpu.roll(v, HD // 2, 1))


def _rope_fwd(name, proj, cos2, sin2):
    def body(p_ref, c_ref, s_ref, *outs):
        cv, sv = c_ref[...], s_ref[...]
        off = 0
        for o_ref, (_, w, rot, is_q) in zip(outs, GROUPS):
            for b in range(w // LANE):
                v = p_ref[:, off + b * LANE:off + (b + 1) * LANE]
                if rot:
                    v = v * cv + _rot_half(v) * sv
                if is_q:
                    v = v * (HD ** -0.5)
                o_ref[:, b * LANE:(b + 1) * LANE] = v.astype(BF16)
            off += w

    return pl.pallas_call(
        body, name=name, grid=(T // TR,), in_specs=[_rows(IN_COLS), _rows(LANE), _rows(LANE)],
        out_specs=[_rows(w) for _, w, _, _ in GROUPS],
        out_shape=[jax.ShapeDtypeStruct((T, w), BF16) for _, w, _, _ in GROUPS],
        compiler_params=_params(("parallel",)),
    )(proj, cos2, sin2)


def _rope_bwd(name, grads, cos2, sin2):
    def body(*refs):
        ins, (c_ref, s_ref, o_ref) = refs[:9], refs[9:]
        cv, sv = c_ref[...], s_ref[...]
        off = 0
        for d_ref, (_, w, rot, is_q) in zip(ins, GROUPS):
            for b in range(w // LANE):
                v = d_ref[:, b * LANE:(b + 1) * LANE]
                if is_q:
                    v = v * (HD ** -0.5)
                if rot:
                    v = v * cv + _rot_half(v * sv)
                o_ref[:, off + b * LANE:off + (b + 1) * LANE] = v.astype(BF16)
            off += w

    return pl.pallas_call(
        body, name=name, grid=(T // TR,), in_specs=[_rows(w) for _, w, _, _ in GROUPS] + [_rows(LANE), _rows(LANE)],
        out_specs=_rows(IN_COLS), out_shape=jax.ShapeDtypeStruct((T, IN_COLS), BF16),
        compiler_params=_params(("parallel",)),
    )(*grads, cos2, sin2)


NT_DIMS = (((1,), (1,)), ((), ()))
TN_DIMS = (((0,), (0,)), ((), ()))


def _scores(q, k, bias, valid):
    s = lax.dot_general(q, k, NT_DIMS, preferred_element_type=F32)
    if bias is not None:
        s = s + bias
    if valid is not None:
        s = jnp.where(valid, s, NEG)
    return s


def _head_fwd(q, k, v, bias=None, valid=None, sink=None):
    s = _scores(q, k, bias, valid)
    m = jnp.max(s, axis=1, keepdims=True)
    e = jnp.exp(s - m)
    l = jnp.sum(e, axis=1, keepdims=True)
    if sink is not None:
        l = l + jnp.exp(sink - m)
    return jnp.dot(e.astype(BF16), v, preferred_element_type=F32) / l, m + jnp.log(l)


def _head_bwd(q, k, v, o, do, lse, bias=None, valid=None, sink=None):
    p = jnp.exp(_scores(q, k, bias, valid) - lse)
    dob = do.astype(BF16)
    dp = lax.dot_general(dob, v, NT_DIMS, preferred_element_type=F32)
    delta = jnp.sum(do * o, axis=1, keepdims=True)
    ds = p * (dp - delta)
    dsb = ds.astype(BF16)
    dq = jnp.dot(dsb, k, preferred_element_type=F32)
    dk = lax.dot_general(dsb, q, TN_DIMS, preferred_element_type=F32)
    dv = lax.dot_general(p.astype(BF16), dob, TN_DIMS, preferred_element_type=F32)
    dsink = None if sink is None else -jnp.exp(sink - lse) * delta
    return dq, dk, dv, ds, dsink


def _per_head(cols):
    return jnp.concatenate([jnp.broadcast_to(c, (c.shape[0], HD)) for c in cols], axis=1)


def _dilation_bias():
    d = jnp.arange(T, dtype=jnp.int32)[:, None] - jnp.arange(T, dtype=jnp.int32)[None, :]
    ad = jnp.abs(d)
    count = jnp.zeros((T, T), jnp.int32)
    for window, r in ((128, 1), (512, 4), (2048, 16)):
        count += ((ad % r == 0) & (ad // r <= window // (2 * r))).astype(jnp.int32)
    return jnp.where(count > 0, jnp.log(jnp.maximum(count, 1).astype(F32)), NEG)


BQ_A = 256


def _attn_a_fwd(name, qa, ka, va, bias):
    def body(q_ref, k_ref, v_ref, b_ref, o_ref, lse_ref):
        b = b_ref[...]
        outs = [_head_fwd(q_ref[:, h * HD:(h + 1) * HD], k_ref[:, h * HD:(h + 1) * HD], v_ref[:, h * HD:(h + 1) * HD],
                          bias=b) for h in range(2)]
        o_ref[...] = jnp.concatenate([o for o, _ in outs], axis=1)
        lse_ref[...] = _per_head([lse for _, lse in outs])

    qs = pl.BlockSpec((BQ_A, LANE), lambda p, i: (i, p))
    ks = pl.BlockSpec((T, LANE), lambda p, i: (0, p))
    return pl.pallas_call(
        body, name=name, grid=(HA // 2, T // BQ_A),
        in_specs=[qs, ks, ks, pl.BlockSpec((BQ_A, T), lambda p, i: (i, 0))], out_specs=[qs, qs],
        out_shape=[jax.ShapeDtypeStruct((T, WA), F32)] * 2, compiler_params=_params(("parallel", "parallel")),
    )(qa, ka, va, bias)


def _attn_a_bwd(name, qa, ka, va, oa, lse, doa, bias):
    def body(q_ref, k_ref, v_ref, o_ref, lse_ref, do_ref, b_ref, dq_ref, dk_ref, dv_ref):
        b = b_ref[...]
        dqs, dks, dvs = [], [], []
        for h in range(2):
            sl = slice(h * HD, (h + 1) * HD)
            dq, dk, dv, _, _ = _head_bwd(q_ref[:, sl], k_ref[:, sl], v_ref[:, sl], o_ref[:, sl], do_ref[:, sl],
                                         lse_ref[:, h * HD:h * HD + 1], bias=b)
            dqs.append(dq)
            dks.append(dk)
            dvs.append(dv)
        dq_ref[...] = jnp.concatenate(dqs, axis=1)
        dk2, dv2 = jnp.concatenate(dks, axis=1), jnp.concatenate(dvs, axis=1)

        @pl.when(pl.program_id(1) == 0)
        def _():
            dk_ref[...] = dk2
            dv_ref[...] = dv2

        @pl.when(pl.program_id(1) > 0)
        def _():
            dk_ref[...] += dk2
            dv_ref[...] += dv2

    qs = pl.BlockSpec((BQ_A, LANE), lambda p, i: (i, p))
    ks = pl.BlockSpec((T, LANE), lambda p, i: (0, p))
    return pl.pallas_call(
        body, name=name, grid=(HA // 2, T // BQ_A),
        in_specs=[qs, ks, ks, qs, qs, qs, pl.BlockSpec((BQ_A, T), lambda p, i: (i, 0))], out_specs=[qs, ks, ks],
        out_shape=[jax.ShapeDtypeStruct((T, WA), F32)] * 3, compiler_params=_params(("parallel", "arbitrary")),
    )(qa, ka, va, oa, lse, doa, bias)


BQ_B = 128
SPAN_B = BQ_B + 2 * WINDOW_B


def _window_b(i):
    start = pl.multiple_of(jnp.clip(i * BQ_B - WINDOW_B, 0, T - SPAN_B), BQ_B)
    qpos = i * BQ_B + lax.broadcasted_iota(jnp.int32, (BQ_B, SPAN_B), 0)
    kpos = start + lax.broadcasted_iota(jnp.int32, (BQ_B, SPAN_B), 1)
    return start, jnp.abs(qpos - kpos) <= WINDOW_B


def _attn_b_fwd(name, qb, kb, vb, sink):
    def body(sink_ref, q_ref, k_ref, v_ref, o_ref, lse_ref):
        start, valid = _window_b(pl.program_id(0))
        kw, vw = k_ref[pl.ds(start, SPAN_B), :], v_ref[pl.ds(start, SPAN_B), :]
        outs = []
        for h in range(HB):
            kv = slice((h // (HB // HKV)) * HD, (h // (HB // HKV) + 1) * HD)
            outs.append(_head_fwd(q_ref[:, h * HD:(h + 1) * HD], kw[:, kv], vw[:, kv], valid=valid, sink=sink_ref[h]))
        o_ref[...] = jnp.concatenate([o for o, _ in outs], axis=1)
        lse_ref[...] = _per_head([lse for _, lse in outs])

    qs = pl.BlockSpec((BQ_B, WB), lambda i: (i, 0))
    return pl.pallas_call(
        body, name=name, grid=(T // BQ_B,),
        in_specs=[pl.BlockSpec(memory_space=pltpu.SMEM), qs, _whole((T, WKV)), _whole((T, WKV))],
        out_specs=[qs, qs],
        out_shape=[jax.ShapeDtypeStruct((T, WB), F32)] * 2, compiler_params=_params(("parallel",)),
    )(sink, qb, kb, vb)


def _attn_b_bwd(name, qb, kb, vb, ob, lse, dob, sink):
    def body(sink_ref, q_ref, k_ref, v_ref, o_ref, lse_ref, do_ref, dq_ref, dk_ref, dv_ref, dsink_ref):
        i = pl.program_id(0)
        start, valid = _window_b(i)
        kw, vw = k_ref[pl.ds(start, SPAN_B), :], v_ref[pl.ds(start, SPAN_B), :]
        lane = lax.broadcasted_iota(jnp.int32, (1, LANE), 1)
        dsink = jnp.zeros((1, LANE), F32)
        dqs, dks, dvs = [], [], []
        for g in range(HKV):
            kv = slice(g * HD, (g + 1) * HD)
            dk_g = jnp.zeros((SPAN_B, HD), F32)
            dv_g = jnp.zeros((SPAN_B, HD), F32)
            for h in range(g * (HB // HKV), (g + 1) * (HB // HKV)):
                sl = slice(h * HD, (h + 1) * HD)
                dq, dk, dv, _, dsr = _head_bwd(q_ref[:, sl], kw[:, kv], vw[:, kv], o_ref[:, sl], do_ref[:, sl],
                                               lse_ref[:, h * HD:h * HD + 1], valid=valid, sink=sink_ref[h])
                dqs.append(dq)
                dk_g += dk
                dv_g += dv
                dsink += jnp.where(lane == h, jnp.sum(dsr), 0.0)
            dks.append(dk_g)
            dvs.append(dv_g)
        dq_ref[...] = jnp.concatenate(dqs, axis=1)

        @pl.when(i == 0)
        def _():
            dk_ref[...] = jnp.zeros_like(dk_ref)
            dv_ref[...] = jnp.zeros_like(dv_ref)
            dsink_ref[...] = jnp.zeros_like(dsink_ref)

        dk_ref[pl.ds(start, SPAN_B), :] += jnp.concatenate(dks, axis=1)
        dv_ref[pl.ds(start, SPAN_B), :] += jnp.concatenate(dvs, axis=1)
        dsink_ref[...] += dsink

    qs = pl.BlockSpec((BQ_B, WB), lambda i: (i, 0))
    return pl.pallas_call(
        body, name=name, grid=(T // BQ_B,),
        in_specs=[pl.BlockSpec(memory_space=pltpu.SMEM), qs, _whole((T, WKV)), _whole((T, WKV)), qs, qs, qs],
        out_specs=[qs, _whole((T, WKV)), _whole((T, WKV)), _whole((1, LANE))],
        out_shape=[jax.ShapeDtypeStruct((T, WB), F32), jax.ShapeDtypeStruct((T, WKV), F32),
                   jax.ShapeDtypeStruct((T, WKV), F32), jax.ShapeDtypeStruct((1, LANE), F32)],
        compiler_params=_params(("arbitrary",)),
    )(sink, qb, kb, vb, ob, lse, dob)


SPAN_C = NA_ROWS * GRID_W


def _row_start(r):
    return jnp.clip(r - NA_ROWS // 2, 0, ROWS - NA_ROWS)


def _off_index(r):
    return _row_start(r) - r + (NA_ROWS - 1)


N_TAB = 16
RPS = 4


def _rpb_tables(rpb):
    w129 = jnp.concatenate([rpb[..., NA_COLS - 1:], jnp.zeros(rpb.shape[:2] + (129 - (2 * NA_COLS - 1),), F32),
                            rpb[..., :NA_COLS - 1]], axis=-1)
    toep = jnp.tile(w129, (1, 1, GRID_W))[..., :GRID_W * LANE].reshape(HC, 2 * NA_ROWS - 1, GRID_W, LANE)
    pairs = jnp.concatenate([toep[:, :-1, :, :GRID_W], toep[:, 1:, :, :GRID_W]], axis=-1)
    c = jnp.arange(GRID_W)[:, None]
    kc = jnp.arange(LANE)[None, :] % GRID_W
    cs = jnp.clip(c - NA_COLS // 2, 0, GRID_W - NA_COLS)
    pairs = jnp.where((kc >= cs) & (kc < cs + NA_COLS), pairs, NEG)
    return jnp.pad(pairs, ((0, 0), (0, N_TAB - pairs.shape[1]), (0, 0), (0, 0)))


def _bias_c(t_ref, h, d):
    return jnp.concatenate([t_ref[h, d + k] for k in range(0, NA_ROWS, 2)], axis=1)


def _attn_c_fwd(name, qc, kc, vc, tables):
    def body(q_ref, k_ref, v_ref, t_ref, o_ref, lse_ref):
        for rr in range(RPS):
            r = pl.program_id(1) * RPS + rr
            rows = slice(rr * GRID_W, (rr + 1) * GRID_W)
            start = pl.multiple_of(_row_start(r) * GRID_W, GRID_W)
            kw, vw = k_ref[pl.ds(start, SPAN_C), :], v_ref[pl.ds(start, SPAN_C), :]
            outs = [_head_fwd(q_ref[rows, h * HD:(h + 1) * HD], kw[:, h * HD:(h + 1) * HD], vw[:, h * HD:(h + 1) * HD],
                              bias=_bias_c(t_ref, h, _off_index(r))) for h in range(2)]
            o_ref[rows, :] = jnp.concatenate([o for o, _ in outs], axis=1)
            lse_ref[rows, :] = _per_head([lse for _, lse in outs])

    qs = pl.BlockSpec((RPS * GRID_W, LANE), lambda p, r: (r, p))
    ks = pl.BlockSpec((T, LANE), lambda p, r: (0, p))
    ts = pl.BlockSpec((2, N_TAB, GRID_W, LANE), lambda p, r: (p, 0, 0, 0))
    return pl.pallas_call(
        body, name=name, grid=(HC // 2, ROWS // RPS), in_specs=[qs, ks, ks, ts], out_specs=[qs, qs],
        out_shape=[jax.ShapeDtypeStruct((T, WC), F32)] * 2, compiler_params=_params(("parallel", "parallel")),
    )(qc, kc, vc, tables)


def _attn_c_bwd(name, qc, kc, vc, oc, lse, doc, tables):
    def body(q_ref, k_ref, v_ref, o_ref, lse_ref, do_ref, t_ref, dq_ref, dk_ref, dv_ref, dt_ref):
        @pl.when(pl.program_id(1) == 0)
        def _():
            dk_ref[...] = jnp.zeros_like(dk_ref)
            dv_ref[...] = jnp.zeros_like(dv_ref)
            dt_ref[...] = jnp.zeros_like(dt_ref)

        for rr in range(RPS):
            r = pl.program_id(1) * RPS + rr
            rows = slice(rr * GRID_W, (rr + 1) * GRID_W)
            d = _off_index(r)
            start = pl.multiple_of(_row_start(r) * GRID_W, GRID_W)
            kw, vw = k_ref[pl.ds(start, SPAN_C), :], v_ref[pl.ds(start, SPAN_C), :]
            dqs, dks, dvs = [], [], []
            for h in range(2):
                sl = slice(h * HD, (h + 1) * HD)
                dq, dk, dv, ds, _ = _head_bwd(q_ref[rows, sl], kw[:, sl], vw[:, sl], o_ref[rows, sl], do_ref[rows, sl],
                                              lse_ref[rows, h * HD:h * HD + 1], bias=_bias_c(t_ref, h, d))
                dqs.append(dq)
                dks.append(dk)
                dvs.append(dv)
                for k in range(0, NA_ROWS, 2):
                    dt_ref[h, d + k] += ds[:, k * GRID_W:(k + 2) * GRID_W]
            dq_ref[rows, :] = jnp.concatenate(dqs, axis=1)
            dk_ref[pl.ds(start, SPAN_C), :] += jnp.concatenate(dks, axis=1)
            dv_ref[pl.ds(start, SPAN_C), :] += jnp.concatenate(dvs, axis=1)

    qs = pl.BlockSpec((RPS * GRID_W, LANE), lambda p, r: (r, p))
    ks = pl.BlockSpec((T, LANE), lambda p, r: (0, p))
    ts = pl.BlockSpec((2, N_TAB, GRID_W, LANE), lambda p, r: (p, 0, 0, 0))
    return pl.pallas_call(
        body, name=name, grid=(HC // 2, ROWS // RPS), in_specs=[qs, ks, ks, qs, qs, qs, ts],
        out_specs=[qs, ks, ks, ts],
        out_shape=[jax.ShapeDtypeStruct((T, WC), F32)] * 3 + [jax.ShapeDtypeStruct((HC, N_TAB, GRID_W, LANE), F32)],
        compiler_params=_params(("parallel", "arbitrary")),
    )(qc, kc, vc, oc, lse, doc, tables)


def _split3(v):
    hi = v.astype(BF16)
    r1 = v - hi.astype(F32)
    mid = r1.astype(BF16)
    lo = (r1 - mid.astype(F32)).astype(BF16)
    return hi, mid, lo


def _rpb_reduce(name, dtables):
    x = dtables.reshape(HC, N_TAB, GRID_W * LANE)
    c = jnp.arange(GRID_W)[:, None]
    lane = jnp.arange(LANE)[None, :]
    col = (lane // GRID_W) * LANE + jnp.clip(lane % GRID_W - c + (NA_COLS - 1), 0, 2 * NA_COLS - 2)
    col_onehot = (col.reshape(-1)[:, None] == jnp.arange(2 * LANE)[None, :]).astype(BF16)
    a2 = jnp.arange(N_TAB)[None, :]
    row_onehot = jnp.concatenate([(jnp.arange(16)[:, None] == a2 + u) & (a2 < 2 * NA_ROWS - 2) for u in range(2)],
                                 axis=1).astype(BF16)

    def body(x_ref, e_ref, f_ref, o_ref):
        y = sum(jnp.dot(part, e_ref[...], preferred_element_type=F32) for part in _split3(x_ref[...]))
        z = jnp.concatenate([y[:, :LANE], y[:, LANE:]], axis=0)
        o_ref[...] = sum(jnp.dot(f_ref[...], part, preferred_element_type=F32) for part in _split3(z))

    out = pl.pallas_call(
        body, name=name, grid=(HC,),
        in_specs=[pl.BlockSpec((None, N_TAB, GRID_W * LANE), lambda h: (h, 0, 0)),
                  _whole((GRID_W * LANE, 2 * LANE)), _whole((16, 2 * N_TAB))],
        out_specs=pl.BlockSpec((None, 16, LANE), lambda h: (h, 0, 0)),
        out_shape=jax.ShapeDtypeStruct((HC, 16, LANE), F32), compiler_params=_params(("parallel",)),
    )(x, col_onehot, row_onehot)
    return out[:, :2 * NA_ROWS - 1, :2 * NA_COLS - 1]


TC = 128
NCB = DFF // TC


def _shift_down(v, rows):
    return jnp.where(rows == 0, 0.0, pltpu.roll(v, 1, 0))


def _shift_up(v, rows):
    return jnp.where(rows == T - 1, 0.0, pltpu.roll(v, T - 1, 0))


def _conv(v, w, b, rows):
    return _shift_down(v, rows) * w[0:1] + v * w[1:2] + _shift_up(v, rows) * w[2:3] + b


def _ffn_specs():
    gate = lambda shape: pl.BlockSpec(shape, lambda j: (0, j))
    val = lambda shape: pl.BlockSpec(shape, lambda j: (0, j + NCB))
    return [gate((T, TC)), val((T, TC)), gate((3, TC)), val((3, TC)), gate((1, TC)), val((1, TC))]


def _ffn_mid_fwd(name, up, conv_w, conv_b):
    def body(xg_ref, xv_ref, wg_ref, wv_ref, bg_ref, bv_ref, o_ref):
        rows = lax.broadcasted_iota(jnp.int32, (T, TC), 0)
        ug = _conv(xg_ref[...], wg_ref[...], bg_ref[...], rows)
        uv = _conv(xv_ref[...], wv_ref[...], bv_ref[...], rows)
        o_ref[...] = (ug * jax.nn.sigmoid(ug) * uv).astype(BF16)

    return pl.pallas_call(
        body, name=name, grid=(NCB,), in_specs=_ffn_specs(), out_specs=pl.BlockSpec((T, TC), lambda j: (0, j)),
        out_shape=jax.ShapeDtypeStruct((T, DFF), BF16), compiler_params=_params(("parallel",)),
    )(up, up, conv_w, conv_w, conv_b, conv_b)


def _ffn_mid_bwd(name, dact, up, conv_w, conv_b):
    def body(da_ref, xg_ref, xv_ref, wg_ref, wv_ref, bg_ref, bv_ref, dx_ref, dw_ref, db_ref):
        rows = lax.broadcasted_iota(jnp.int32, (T, TC), 0)
        xg, xv, wg, wv = xg_ref[...], xv_ref[...], wg_ref[...], wv_ref[...]
        ug = _conv(xg, wg, bg_ref[...], rows)
        uv = _conv(xv, wv, bv_ref[...], rows)
        sg = jax.nn.sigmoid(ug)
        da = da_ref[...]
        dug = da * uv * (sg * (1.0 + ug * (1.0 - sg)))
        duv = da * (ug * sg)
        for half, (xin, w, du) in enumerate(((xg, wg, dug), (xv, wv, duv))):
            dx = _shift_up(du, rows) * w[0:1] + du * w[1:2] + _shift_down(du, rows) * w[2:3]
            dx_ref[half] = dx.astype(BF16)
            dw_ref[half] = jnp.concatenate(
                [jnp.sum(_shift_down(xin, rows) * du, axis=0, keepdims=True), jnp.sum(xin * du, axis=0, keepdims=True),
                 jnp.sum(_shift_up(xin, rows) * du, axis=0, keepdims=True)], axis=0)
            db_ref[half] = jnp.sum(du, axis=0, keepdims=True)

    return pl.pallas_call(
        body, name=name, grid=(NCB,), in_specs=[pl.BlockSpec((T, TC), lambda j: (0, j))] + _ffn_specs(),
        out_specs=[pl.BlockSpec((2, T, TC), lambda j: (0, 0, j)), pl.BlockSpec((2, 3, TC), lambda j: (0, 0, j)),
                   pl.BlockSpec((2, 1, TC), lambda j: (0, 0, j))],
        out_shape=[jax.ShapeDtypeStruct((2, T, DFF), BF16), jax.ShapeDtypeStruct((2, 3, DFF), F32),
                   jax.ShapeDtypeStruct((2, 1, DFF), F32)],
        compiler_params=_params(("parallel",)),
    )(dact, up, up, conv_w, conv_w, conv_b, conv_b)


def _dup_spec(tm, nj):
    per = DFF // nj
    return pl.BlockSpec((None, tm, nj), lambda a, b, j: (j // per, 0 if tm == T else b, j % per))


def _dup_spec_tn(tm, nj):
    per = DFF // nj
    return pl.BlockSpec((None, tm, nj), lambda j, kt, r: (j // per, 0, j % per))


def _adamw_math(w, g, m, v):
    m = ADAM_B1 * m + (1.0 - ADAM_B1) * g
    v = ADAM_B2 * v + (1.0 - ADAM_B2) * (g * g)
    m_hat = m / (1.0 - ADAM_B1 ** ADAM_STEP)
    v_hat = v / (1.0 - ADAM_B2 ** ADAM_STEP)
    delta = -ADAM_LR * (m_hat / (jnp.sqrt(v_hat) + ADAM_EPS) + ADAM_WD * w)
    return delta, m, v


def _adamw_sharded(name, w, m, v, parts):
    _, r, c = w.shape
    tr = 64

    def body(w_ref, m_ref, v_ref, p0_ref, p1_ref, g_ref, d_ref, nm_ref, nv_ref):
        def run(p_ref):
            g = p_ref[0].astype(F32)
            for k in range(1, N_DEV):
                g = g + p_ref[k].astype(F32)
            d, nm, nv = _adamw_math(w_ref[...], g, m_ref[...], v_ref[...])
            g_ref[...] = g
            d_ref[...] = d
            nm_ref[...] = nm
            nv_ref[...] = nv

        @pl.when(pl.program_id(0) == 0)
        def _():
            run(p0_ref)

        @pl.when(pl.program_id(0) == 1)
        def _():
            run(p1_ref)

    ws = pl.BlockSpec((None, tr, c), lambda l, i: (l, i, 0))
    p0 = pl.BlockSpec((N_DEV, tr, c), lambda l, i: (0, jnp.where(l == 0, i, r // tr - 1), 0))
    p1 = pl.BlockSpec((N_DEV, tr, c), lambda l, i: (0, jnp.where(l == 1, i, 0), 0))
    return pl.pallas_call(
        body, name=name, grid=(DEPTH, r // tr), in_specs=[ws, ws, ws, p0, p1], out_specs=[ws] * 4,
        out_shape=[jax.ShapeDtypeStruct(w.shape, F32)] * 4, compiler_params=_params(("arbitrary", "arbitrary")),
    )(w, m, v, *parts)


def _sum_devices(name, parts):
    r = parts.shape[1]

    def body(p_ref, o_ref):
        g = p_ref[0]
        for k in range(1, N_DEV):
            g = g + p_ref[k]
        o_ref[...] = g

    return pl.pallas_call(
        body, name=name, in_specs=[pl.BlockSpec((N_DEV, r, LANE), lambda: (0, 0, 0))],
        out_specs=pl.BlockSpec((r, LANE), lambda: (0, 0)), out_shape=jax.ShapeDtypeStruct((r, LANE), F32),
        compiler_params=_params(),
    )(parts)


def _adamw_small(name, w, g, m, v):
    spec = pl.BlockSpec(w.shape, lambda: (0, 0))

    def body(w_ref, g_ref, m_ref, v_ref, d_ref, nm_ref, nv_ref):
        d, nm, nv = _adamw_math(w_ref[...], g_ref[...], m_ref[...], v_ref[...])
        d_ref[...] = d
        nm_ref[...] = nm
        nv_ref[...] = nv

    return pl.pallas_call(
        body, name=name, in_specs=[spec] * 4, out_specs=[spec] * 3,
        out_shape=[jax.ShapeDtypeStruct(w.shape, F32)] * 3, compiler_params=_params(),
    )(w, g, m, v)


def _pack(arrays):
    flat = jnp.concatenate([a.reshape(-1) for a in arrays])
    pad = (-flat.shape[0]) % (8 * LANE)
    return jnp.pad(flat, (0, pad)).reshape(-1, LANE)


def _unpack(buf, shapes):
    flat, out, off = buf.reshape(-1), [], 0
    for s in shapes:
        n = 1
        for d in s:
            n *= d
        out.append(flat[off:off + n].reshape(s))
        off += n
    return out


def _local_step(x, target, small, weights, conv_w_full, hand_over):
    cos2, sin2 = _rope_tables()
    bias_a = _dilation_bias()
    saved = []
    for l in range(DEPTH):
        g1, g2 = small["ln_attn"][l][None], small["ln_ffn"][l][None]
        gain, sink, cb = small["mix_gain"][l][None], small["sink_b"][l], small["conv_b"][l][None]
        cw = conv_w_full[l]
        bias = _rpb_tables(small["rpb_c"][l])
        h1 = _rmsnorm_fwd(f"norm_attn_{l}", x, g1)
        proj = _nn_cols(f"proj_in_{l}", h1, weights("w_in", l, h1))
        qa, ka, va, qb, kb, vb, qc, kc, vc = _rope_fwd(f"rope_{l}", proj, cos2, sin2)
        oa, lse_a = _attn_a_fwd(f"attn_a_{l}", qa, ka, va, bias_a)
        ob, lse_b = _attn_b_fwd(f"attn_b_{l}", qb, kb, vb, sink)
        oc, lse_c = _attn_c_fwd(f"attn_c_{l}", qc, kc, vc, bias)
        mixed = _mix_fwd(f"mix_{l}", oa, ob, oc, gain)
        x_mid = _nn_rows(f"proj_out_{l}", mixed, weights("w_out", l, mixed), x, 8, 512)
        h2 = _rmsnorm_fwd(f"norm_ffn_{l}", x_mid, g2)
        up = _nn_cols(f"ffn_up_{l}", h2, weights("w_up", l, h2))
        act = _ffn_mid_fwd(f"ffn_mid_{l}", up, cw, cb)
        x_out = _nn_rows(f"ffn_down_{l}", act, weights("w_down", l, act), x_mid, 2, 1024)
        saved.append(dict(x=x, h1=h1, qkv=(qa, ka, va, qb, kb, vb, qc, kc, vc), o=(oa, ob, oc), lse=(lse_a, lse_b, lse_c), mixed=mixed,
                          x_mid=x_mid, h2=h2, up=up, act=act, g1=g1, g2=g2, gain=gain, sink=sink, cb=cb, cw=cw, bias=bias))
        x = x_out

    loss8, dx, dxb, d_ln_final = _loss_head(x, small["ln_final"][None], target)
    sgrads = [None] * DEPTH
    for l in reversed(range(DEPTH)):
        s = saved[l]
        qa, ka, va, qb, kb, vb, qc, kc, vc = s["qkv"]
        oa, ob, oc = s["o"]
        wg_in, wg_out = weights("w_in", l, None), weights("w_out", l, None)
        wg_up, wg_down = weights("w_up", l, None), weights("w_down", l, None)
        g_down = _tn_rows(f"wgrad_down_{l}", s["act"], dxb, wg_down.shape[1], 2)
        zero = hand_over("w_down", l, g_down)
        dact = _nt_rows(f"dgrad_down_{l}", dxb, wg_down, 2)
        dup, d_cw, d_cb = _ffn_mid_bwd(f"ffn_mid_bwd_{l}", dact, s["up"], s["cw"], s["cb"] + zero)
        g_up = _tn_cols(f"wgrad_up_{l}", s["h2"], dup, _dup_spec_tn, wg_up.shape[2])
        zero = hand_over("w_up", l, g_up)
        dh2 = _nt_cols(f"dgrad_up_{l}", dup, _dup_spec, wg_up)
        dx, dxb, d_g2 = _rmsnorm_bwd(f"norm_ffn_bwd_{l}", dh2, s["x_mid"], s["g2"] + zero, dx)
        g_out = _tn_rows(f"wgrad_out_{l}", s["mixed"], dxb, wg_out.shape[1], 2)
        zero = hand_over("w_out", l, g_out)
        dmixed = _nt_rows(f"dgrad_out_{l}", dxb, wg_out, 2)
        doa, dob, doc, d_gain = _mix_bwd(f"mix_bwd_{l}", dmixed, oa, ob, oc, s["gain"] + zero)
        lse_a, lse_b, lse_c = s["lse"]
        dqa, dka, dva = _attn_a_bwd(f"attn_a_bwd_{l}", qa, ka, va, oa, lse_a, doa, bias_a)
        dqb, dkb, dvb, d_sink = _attn_b_bwd(f"attn_b_bwd_{l}", qb, kb, vb, ob, lse_b, dob, s["sink"])
        dqc, dkc, dvc, d_bias = _attn_c_bwd(f"attn_c_bwd_{l}", qc, kc, vc, oc, lse_c, doc, s["bias"])
        d_rpb = _rpb_reduce(f"rpb_reduce_{l}", d_bias)
        dproj = _rope_bwd(f"rope_bwd_{l}", (dqa, dka, dva, dqb, dkb, dvb, dqc, dkc, dvc), cos2, sin2)
        g_in = _tn_cols(f"wgrad_in_{l}", s["h1"], dproj,
                        lambda tm, nj: pl.BlockSpec((tm, nj), lambda j, kt, r: (0, j)), wg_in.shape[2])
        zero = hand_over("w_in", l, g_in)
        dh1 = _nt_cols(f"dgrad_in_{l}", dproj, lambda tm, nj: pl.BlockSpec((tm, nj), lambda kt, i, j: (i, j)), wg_in)
        dx, dxb, d_g1 = _rmsnorm_bwd(f"norm_attn_bwd_{l}", dh1, s["x"], s["g1"] + zero, dx)
        sgrads[l] = dict(ln_attn=d_g1[0], sink_b=d_sink[0, :HB], rpb_c=d_rpb, mix_gain=d_gain[0], ln_ffn=d_g2[0],
                         conv_w=d_cw.transpose(1, 0, 2).reshape(3, 2 * DFF), conv_b=d_cb.reshape(2 * DFF))
    return loss8[0, 0], dx, d_ln_final[0], sgrads


SMALL_NAMES = ("ln_attn", "sink_b", "rpb_c", "mix_gain", "ln_ffn", "conv_b")


def kernel(x, ln_attn, w_in, sink_b, rpb_c, mix_gain, w_out, ln_ffn, w_up, conv_w, conv_b, w_down, ln_final, loss_target, m_ln_attn, m_w_in, m_sink_b, m_rpb_c, m_mix_gain, m_w_out, m_ln_ffn, m_w_up, m_conv_w, m_conv_b, m_w_down, m_ln_final, v_ln_attn, v_w_in, v_sink_b, v_rpb_c, v_mix_gain, v_w_out, v_ln_ffn, v_w_up, v_conv_w, v_conv_b, v_w_down, v_ln_final):
    me = 4 * lax.axis_index("x") + 2 * lax.axis_index("y") + lax.axis_index("c")
    small = dict(ln_attn=ln_attn, sink_b=sink_b, rpb_c=rpb_c, mix_gain=mix_gain, ln_ffn=ln_ffn, conv_b=conv_b,
                 ln_final=ln_final)

    names = ("w_in", "w_out", "w_up", "w_down")
    shards = dict(w_in=w_in, w_out=w_out, w_up=w_up, w_down=w_down)
    order = [(n, l) for l in range(DEPTH) for n in names]
    keys = [("conv_w", 0)] + order
    srcs = [_pack([conv_w])] + [shards[n][l].astype(BF16) for n, l in order]
    peer_sets = [ALL_PEERS] + [NEAR_PEERS] * len(order)
    nk = len(keys)
    send_a, recv_a, bufs_a, token = _copy_start(
        "gather_start", srcs + [lax.empty((N_DEV,) + s.shape, s.dtype) for s in srcs], _gather_plan(peer_sets),
        [len(p) for p in peer_sets])
    arrived, forwarded, gathered = {}, {}, {}

    def arrive(k, after):
        i = keys.index(k)
        arrived[k] = _copy_wait(f"gather_{k[0]}_{k[1]}_arrived", [bufs_a[i], bufs_a[nk + i]], [send_a[i]],
                                [recv_a[i]], _gather_plan([peer_sets[i]]), after)

    def forward(k, after):
        arrive(k, after)
        forwarded[k] = _copy_start(f"gather_{k[0]}_{k[1]}_forward", [arrived[k][1]], _forward_plan, [len(OTHER_CHIPS)])
        return forwarded[k][3]

    def complete(k, land):
        return lax.dynamic_update_slice_in_dim(land, arrived[k][0][None], me, axis=0)

    def weights(n, l, after):
        k = (n, l)
        if k not in gathered:
            i = order.index(k)
            if k not in forwarded:
                forward(k, after)
            token_next = forward(order[i + 1], after) if i + 1 < len(order) else after
            send_b, recv_b, (land,), _ = forwarded[k]
            (land,) = _copy_wait(f"gather_{n}_{l}_done", [land], send_b, recv_b, _forward_plan, token_next)
            gathered[k] = complete(k, land)
        return gathered[k]

    pending = {}

    def hand_over(n, l, g):
        send, recv, bufs, tok = _copy_start(f"send_grad_{n}_{l}", [g, lax.empty(g.shape, g.dtype)], _scatter_plan,
                                            [len(ALL_PEERS)])
        pending[(n, l)] = (send, recv, bufs)
        return tok[0, 0]

    def received(k, after):
        send, recv, bufs = pending[k]
        src, land = _copy_wait(f"recv_grad_{k[0]}_{k[1]}", bufs, send, recv, _scatter_plan, after)
        return lax.dynamic_update_slice_in_dim(land, lax.dynamic_slice_in_dim(src, me, 1, axis=0), me, axis=0)

    arrive(keys[0], token)
    cw_all = complete(keys[0], arrived[keys[0]][1])
    nup = w_up.shape[2]
    cw_shards = cw_all.reshape(N_DEV, -1)[:, :DEPTH * 3 * nup].reshape(N_DEV, DEPTH, 3, nup)
    conv_w_full = cw_shards.transpose(1, 2, 0, 3).reshape(DEPTH, 3, N_DEV * nup)

    loss_local, dx, d_ln_final, sgrads = _local_step(
        x[0], loss_target[0], dict(small, ln_attn=ln_attn + token[0, 0]), weights, conv_w_full, hand_over)

    stacked = [jnp.stack([sgrads[l][n] for l in range(DEPTH)]) for n in SMALL_NAMES + ("conv_w",)] + [d_ln_final]
    shapes = [a.shape for a in stacked]
    mine = _pack(stacked)
    send_s, recv_s, bufs_s, _ = _copy_start("gather_small_grads_start", [mine, lax.empty((N_DEV,) + mine.shape, F32)],
                                            _gather_plan([ALL_PEERS]), [len(ALL_PEERS)])

    big, after = {}, dx
    moments = dict(w_in=(m_w_in, v_w_in), w_out=(m_w_out, v_w_out), w_up=(m_w_up, v_w_up), w_down=(m_w_down, v_w_down))
    for n in reversed(names):
        parts = (received((n, 0), after), received((n, 1), after))
        big[n] = _adamw_sharded(f"adamw_{n}", shards[n], *moments[n], parts)
        after = big[n][1]

    mine, land = _copy_wait("gather_small_grads_done", bufs_s, send_s, recv_s, _gather_plan([ALL_PEERS]), after)
    everyone = lax.dynamic_update_slice_in_dim(land, mine[None], me, axis=0)
    g_small = _unpack(_sum_devices("sum_small_grads", everyone), shapes)
    g = dict(zip(SMALL_NAMES + ("conv_w", "ln_final"), g_small))
    g["conv_w"] = lax.dynamic_slice_in_dim(g["conv_w"], me * nup, nup, axis=2)

    snames = SMALL_NAMES + ("conv_w", "ln_final")
    sw = dict(small, conv_w=conv_w)
    sm = dict(ln_attn=m_ln_attn, sink_b=m_sink_b, rpb_c=m_rpb_c, mix_gain=m_mix_gain, ln_ffn=m_ln_ffn,
              conv_b=m_conv_b, conv_w=m_conv_w, ln_final=m_ln_final)
    sv = dict(ln_attn=v_ln_attn, sink_b=v_sink_b, rpb_c=v_rpb_c, mix_gain=v_mix_gain, ln_ffn=v_ln_ffn,
              conv_b=v_conv_b, conv_w=v_conv_w, ln_final=v_ln_final)
    sshapes = [sw[n].shape for n in snames]
    packed = _adamw_small("adamw_small", _pack([sw[n] for n in snames]), _pack([g[n] for n in snames]),
                          _pack([sm[n] for n in snames]), _pack([sv[n] for n in snames]))
    s_delta, s_m, s_v = (dict(zip(snames, _unpack(buf, sshapes))) for buf in packed)

    loss = lax.psum(loss_local, ("x", "y", "c"))
    outputs = ("ln_attn", "w_in", "sink_b", "rpb_c", "mix_gain", "w_out", "ln_ffn", "w_up", "conv_w", "conv_b",
               "w_down", "ln_final")
    grads = [big[n][0] if n in big else g[n] for n in outputs]
    deltas = [big[n][1] if n in big else s_delta[n] for n in outputs]
    new_m = [big[n][2] if n in big else s_m[n] for n in outputs]
    new_v = [big[n][3] if n in big else s_v[n] for n in outputs]
    return (loss, dx[None], *grads, *deltas, *new_m, *new_v)
```

```python
import functools

import jax
import jax.numpy as jnp
from jax import lax
from jax.experimental import pallas as pl
from jax.experimental.pallas import tpu as pltpu

F32 = jnp.float32
BF16 = jnp.bfloat16

N_DEV = 8
T = 2048
D = 2048
DEPTH = 2
HD = 64
HA, HB, HKV, HC = 12, 10, 2, 10
WA, WB, WKV, WC = HA * HD, HB * HD, HKV * HD, HC * HD
IN_COLS = 3 * WA + WB + 2 * WKV + 3 * WC
DFF = 5632
GRID_W = 64
ROWS = T // GRID_W
NA_ROWS, NA_COLS = 8, 16
WINDOW_B = 128
EPS = 1e-6
NEG = -1e30
ROPE_THETA = 10000.0
LANE = 128
VMEM_LIMIT = 56 * 1024 * 1024

ADAM_LR, ADAM_B1, ADAM_B2, ADAM_EPS, ADAM_WD, ADAM_STEP = 0.001, 0.9, 0.999, 1e-08, 0.01, 10

GROUPS = (("qa", WA, True, True), ("ka", WA, True, False), ("va", WA, False, False),
          ("qb", WB, True, True), ("kb", WKV, True, False), ("vb", WKV, False, False),
          ("qc", WC, False, True), ("kc", WC, False, False), ("vc", WC, False, False))


def _params(sem=None):
    return pltpu.CompilerParams(dimension_semantics=sem, vmem_limit_bytes=VMEM_LIMIT)


HBM_SPEC = pl.BlockSpec(memory_space=pltpu.HBM)
SEM_SPEC = pl.BlockSpec(memory_space=pltpu.SEMAPHORE)
DATAFLOW = pltpu.SideEffectType.DATAFLOW_SIDE_EFFECTING


ALL_PEERS = tuple((p >> 2 & 1, p >> 1 & 1, p & 1) for p in range(1, N_DEV))
OTHER_CHIPS = ((1, 0, 0), (0, 1, 0), (1, 1, 0))
NEAR_PEERS = ((0, 0, 1),) + OTHER_CHIPS


def _flip(x, y, c, f):
    return (1 - x if f[0] else x, 1 - y if f[1] else y, 1 - c if f[2] else c)


def _index(pos):
    return 4 * pos[0] + 2 * pos[1] + pos[2]


def _descriptors(plan, bufs, send_sems, recv_sems):
    x, y, c = lax.axis_index("x"), lax.axis_index("y"), lax.axis_index("c")
    return [pltpu.make_async_remote_copy(src_ref=src, dst_ref=dst, send_sem=send_sems[g].at[i],
                                         recv_sem=recv_sems[g].at[i], device_id=partner,
                                         device_id_type=pl.DeviceIdType.MESH)
            for g, copies in enumerate(plan(bufs, x, y, c)) for i, (src, dst, partner) in enumerate(copies)]


def _copy_start(name, bufs, plan, sizes):
    nb, ng = len(bufs), len(sizes)

    def body(*refs):
        for d in _descriptors(plan, refs[:nb], refs[nb:nb + ng], refs[nb + ng:nb + 2 * ng]):
            d.start()
        refs[2 * nb + 2 * ng][...] = jnp.zeros((8, LANE), F32)

    outs = pl.pallas_call(
        body, name=name,
        out_shape=[pltpu.SemaphoreType.DMA((s,)) for s in sizes] * 2 + [pltpu.HBM(b.shape, b.dtype) for b in bufs]
        + [jax.ShapeDtypeStruct((8, LANE), F32)],
        in_specs=[HBM_SPEC] * nb,
        out_specs=[SEM_SPEC] * (2 * ng) + [HBM_SPEC] * nb + [pl.BlockSpec(memory_space=pltpu.VMEM)],
        input_output_aliases={i: 2 * ng + i for i in range(nb)},
        compiler_params=pltpu.CompilerParams(has_side_effects=DATAFLOW),
    )(*[pltpu.with_memory_space_constraint(b, pltpu.HBM) for b in bufs])
    return outs[:ng], outs[ng:2 * ng], outs[2 * ng:2 * ng + nb], outs[2 * ng + nb]


def _copy_wait(name, bufs, send_sems, recv_sems, plan, after):
    nb, ng = len(bufs), len(send_sems)

    def body(*refs):
        for d in _descriptors(plan, refs[:nb], refs[nb:nb + ng], refs[nb + ng:nb + 2 * ng]):
            d.wait_send()
            d.wait_recv()

    return pl.pallas_call(
        body, name=name, out_shape=[pltpu.HBM(b.shape, b.dtype) for b in bufs],
        in_specs=[HBM_SPEC] * nb + [SEM_SPEC] * (2 * ng) + [pl.BlockSpec(memory_space=pl.ANY)],
        out_specs=[HBM_SPEC] * nb, input_output_aliases={i: i for i in range(nb)},
        compiler_params=pltpu.CompilerParams(has_side_effects=DATAFLOW),
    )(*bufs, *send_sems, *recv_sems, after)


def _gather_plan(peer_sets):
    def plan(bufs, x, y, c):
        n = len(peer_sets)
        return [[(bufs[i], bufs[n + i].at[_index((x, y, c))], _flip(x, y, c, f)) for f in peers]
                for i, peers in enumerate(peer_sets)]
    return plan


def _forward_plan(bufs, x, y, c):
    slots = [_index(_flip(x, y, c, f)) for f in OTHER_CHIPS]
    return [[(bufs[0].at[s], bufs[0].at[s], _flip(x, y, c, (0, 0, 1))) for s in slots]]


def _scatter_plan(bufs, x, y, c):
    peers = [_flip(x, y, c, f) for f in ALL_PEERS]
    return [[(bufs[0].at[_index(p)], bufs[1].at[_index((x, y, c))], p) for p in peers]]


def _flat2(v):
    return v.reshape(-1, v.shape[-1])


def _matmul(name, kind, a, a_spec, b, b_spec, out_shape, out_spec, grid, res=None, res_spec=None, acc_shape=None):
    dims = {"nn": (((1,), (0,)), ((), ())), "nt": NT_DIMS, "nts": NT_DIMS, "tn": (((0,), (0,)), ((), ()))}[kind]
    nred = grid[-1]

    def body(*refs):
        if res is None:
            a_ref, b_ref, o_ref = refs[:3]
            r_ref = None
        else:
            a_ref, b_ref, r_ref, o_ref = refs[:4]
        if kind == "nts":
            n = b_ref.shape[-1]
            part = sum(lax.dot_general(a_ref[:, blk * n:(blk + 1) * n], b_ref[blk], dims, preferred_element_type=F32)
                       for blk in range(b_ref.shape[0]))
        else:
            part = lax.dot_general(_flat2(a_ref[...]), _flat2(b_ref[...]), dims, preferred_element_type=F32)

        def finish(total):
            if r_ref is not None:
                total = total + r_ref[...]
            o_ref[...] = total.reshape(o_ref.shape).astype(o_ref.dtype)

        if nred == 1:
            finish(part)
        else:
            acc_ref = refs[-1]
            k = pl.program_id(len(grid) - 1)

            @pl.when(k == 0)
            def _():
                acc_ref[...] = part

            @pl.when(jnp.logical_and(k > 0, k < nred - 1))
            def _():
                acc_ref[...] += part

            @pl.when(k == nred - 1)
            def _():
                finish(acc_ref[...] + part)

    ins, specs = [a, b], [a_spec, b_spec]
    if res is not None:
        ins.append(res)
        specs.append(res_spec)
    scratch = [] if nred == 1 else [pltpu.VMEM(acc_shape, F32)]
    return pl.pallas_call(
        body, name=name, grid=grid, in_specs=specs, out_specs=out_spec, out_shape=out_shape, scratch_shapes=scratch,
        compiler_params=_params(("parallel",) * (len(grid) - 1) + ("arbitrary",)),
    )(*ins)


TM = 512


def _nn_cols(name, a, wg, out_dtype=F32):
    _, k, nj = wg.shape
    tm = 1024
    return _matmul(
        name, "nn", a, pl.BlockSpec((tm, k), lambda j, i, r: (i, 0)),
        wg, pl.BlockSpec((None, k, nj), lambda j, i, r: (j, 0, 0)),
        jax.ShapeDtypeStruct((T, N_DEV * nj), out_dtype), pl.BlockSpec((tm, nj), lambda j, i, r: (i, j)),
        (N_DEV, T // tm, 1))


def _nn_rows(name, a, wg, res, s, tn):
    _, kj, n = wg.shape
    tm = 1024
    return _matmul(
        name, "nn", a, pl.BlockSpec((tm, s * kj), lambda j, i, r: (i, r)),
        wg, pl.BlockSpec((s, kj, tn), lambda j, i, r: (r, 0, j)),
        jax.ShapeDtypeStruct((T, n), F32), pl.BlockSpec((tm, tn), lambda j, i, r: (i, j)),
        (n // tn, T // tm, N_DEV // s), res=res, res_spec=pl.BlockSpec((tm, tn), lambda j, i, r: (i, j)),
        acc_shape=(tm, tn))


def _nt_cols(name, dc, dc_spec_of, wg, s):
    _, k, nj = wg.shape
    tm = tk = 1024
    return _matmul(
        name, "nts", dc, dc_spec_of(tm, s * nj),
        wg, pl.BlockSpec((s, tk, nj), lambda kt, i, j: (j, kt, 0)),
        jax.ShapeDtypeStruct((T, k), F32), pl.BlockSpec((tm, tk), lambda kt, i, j: (i, kt)),
        (k // tk, T // tm, N_DEV // s), acc_shape=(tm, tk))


def _nt_rows(name, dc, wg, s):
    _, kj, n = wg.shape
    return _matmul(
        name, "nt", dc, pl.BlockSpec((TM, n), lambda kt, i, r: (i, 0)),
        wg, pl.BlockSpec((s, kj, n), lambda kt, i, r: (kt, 0, 0)),
        jax.ShapeDtypeStruct((T, N_DEV * kj), F32), pl.BlockSpec((TM, s * kj), lambda kt, i, r: (i, kt)),
        (N_DEV // s, T // TM, 1))


def _tn_cols(name, a, dc, dc_spec_of, nj):
    k = a.shape[1]
    tk = 512
    return _matmul(
        name, "tn", a, pl.BlockSpec((T, tk), lambda j, kt, r: (0, kt)),
        dc, dc_spec_of(T, nj),
        jax.ShapeDtypeStruct((N_DEV, k, nj), BF16), pl.BlockSpec((None, tk, nj), lambda j, kt, r: (j, kt, 0)),
        (N_DEV, k // tk, 1))


def _tn_rows(name, a, dc, kj, s):
    n = dc.shape[1]
    tn = 512
    return _matmul(
        name, "tn", a, pl.BlockSpec((T, s * kj), lambda kt, j, r: (0, kt)),
        dc, pl.BlockSpec((T, tn), lambda kt, j, r: (0, j)),
        jax.ShapeDtypeStruct((N_DEV, kj, n), BF16), pl.BlockSpec((s, kj, tn), lambda kt, j, r: (kt, 0, j)),
        (N_DEV // s, n // tn, 1))


TR = 256


def _rows(width):
    return pl.BlockSpec((TR, width), lambda i: (i, 0))


def _whole(shape):
    return pl.BlockSpec(shape, lambda i: (0,) * len(shape))


def _rmsnorm_fwd(name, x, g):
    def body(x_ref, g_ref, o_ref):
        xv = x_ref[...]
        r = lax.rsqrt(jnp.mean(xv * xv, axis=-1, keepdims=True) + EPS)
        o_ref[...] = ((xv * r) * g_ref[...]).astype(BF16)

    return pl.pallas_call(
        body, name=name, grid=(T // TR,), in_specs=[_rows(D), _whole((1, D))], out_specs=_rows(D),
        out_shape=jax.ShapeDtypeStruct((T, D), BF16), compiler_params=_params(("parallel",)),
    )(x, g)


def _rms_bwd_math(dy, xv, g):
    r = lax.rsqrt(jnp.mean(xv * xv, axis=-1, keepdims=True) + EPS)
    xhat = xv * r
    dxhat = dy * g
    dx = r * (dxhat - xhat * jnp.mean(dxhat * xhat, axis=-1, keepdims=True))
    return dx, dy * xhat


def _accumulate(ref, val):
    @pl.when(pl.program_id(0) == 0)
    def _():
        ref[...] = val

    @pl.when(pl.program_id(0) > 0)
    def _():
        ref[...] += val


def _rmsnorm_bwd(name, dy, x, g, res):
    def body(dy_ref, x_ref, g_ref, res_ref, dx_ref, dxb_ref, dg_ref):
        dx, dgr = _rms_bwd_math(dy_ref[...], x_ref[...], g_ref[...])
        tot = res_ref[...] + dx
        dx_ref[...] = tot
        dxb_ref[...] = tot.astype(BF16)
        _accumulate(dg_ref, jnp.sum(dgr, axis=0, keepdims=True))

    return pl.pallas_call(
        body, name=name, grid=(T // TR,), in_specs=[_rows(D), _rows(D), _whole((1, D)), _rows(D)],
        out_specs=[_rows(D), _rows(D), _whole((1, D))],
        out_shape=[jax.ShapeDtypeStruct((T, D), F32), jax.ShapeDtypeStruct((T, D), BF16),
                   jax.ShapeDtypeStruct((1, D), F32)],
        compiler_params=_params(("arbitrary",)),
    )(dy, x, g, res)


def _loss_head(x, g, target):
    def body(x_ref, g_ref, t_ref, loss_ref, dx_ref, dxb_ref, dg_ref):
        xv, gv = x_ref[...], g_ref[...]
        r = lax.rsqrt(jnp.mean(xv * xv, axis=-1, keepdims=True) + EPS)
        err = (xv * r) * gv - t_ref[...]
        part = 0.5 * jnp.sum(jnp.mean(err * err, axis=-1, keepdims=True))
        dx, dgr = _rms_bwd_math(err * (1.0 / D), xv, gv)
        dx_ref[...] = dx
        dxb_ref[...] = dx.astype(BF16)
        _accumulate(dg_ref, jnp.sum(dgr, axis=0, keepdims=True))
        _accumulate(loss_ref, jnp.full((8, LANE), part, F32))

    return pl.pallas_call(
        body, name="loss_head", grid=(T // TR,), in_specs=[_rows(D), _whole((1, D)), _rows(D)],
        out_specs=[_whole((8, LANE)), _rows(D), _rows(D), _whole((1, D))],
        out_shape=[jax.ShapeDtypeStruct((8, LANE), F32), jax.ShapeDtypeStruct((T, D), F32),
                   jax.ShapeDtypeStruct((T, D), BF16), jax.ShapeDtypeStruct((1, D), F32)],
        compiler_params=_params(("arbitrary",)),
    )(x, g, target)


MIX_OFFS = ((0, WA), (WA, WB), (WA + WB, WC))


def _mix_fwd(name, oa, ob, oc, gain):
    def body(oa_ref, ob_ref, oc_ref, g_ref, o_ref):
        for ref, (off, w) in zip((oa_ref, ob_ref, oc_ref), MIX_OFFS):
            o = ref[...]
            r = lax.rsqrt(jnp.mean(o * o, axis=-1, keepdims=True) + EPS)
            o_ref[:, off:off + w] = ((o * r) * g_ref[:, off:off + w]).astype(BF16)

    return pl.pallas_call(
        body, name=name, grid=(T // TR,), in_specs=[_rows(WA), _rows(WB), _rows(WC), _whole((1, D))],
        out_specs=_rows(D), out_shape=jax.ShapeDtypeStruct((T, D), BF16), compiler_params=_params(("parallel",)),
    )(oa, ob, oc, gain)


def _mix_bwd(name, dmixed, oa, ob, oc, gain):
    def body(dm_ref, oa_ref, ob_ref, oc_ref, g_ref, doa_ref, dob_ref, doc_ref, dg_ref):
        dgs = []
        for ref, dref, (off, w) in zip((oa_ref, ob_ref, oc_ref), (doa_ref, dob_ref, doc_ref), MIX_OFFS):
            dx, dgr = _rms_bwd_math(dm_ref[:, off:off + w], ref[...], g_ref[:, off:off + w])
            dref[...] = dx
            dgs.append(jnp.sum(dgr, axis=0, keepdims=True))
        _accumulate(dg_ref, jnp.concatenate(dgs, axis=1))

    return pl.pallas_call(
        body, name=name, grid=(T // TR,),
        in_specs=[_rows(D), _rows(WA), _rows(WB), _rows(WC), _whole((1, D))],
        out_specs=[_rows(WA), _rows(WB), _rows(WC), _whole((1, D))],
        out_shape=[jax.ShapeDtypeStruct((T, WA), F32), jax.ShapeDtypeStruct((T, WB), F32),
                   jax.ShapeDtypeStruct((T, WC), F32), jax.ShapeDtypeStruct((1, D), F32)],
        compiler_params=_params(("arbitrary",)),
    )(dmixed, oa, ob, oc, gain)


def _rope_tables():
    inv_freq = ROPE_THETA ** (-jnp.arange(0, HD, 2, dtype=F32) / HD)
    ang = jnp.arange(T, dtype=F32)[:, None] * inv_freq[None, :]
    cos, sin = jnp.cos(ang), jnp.sin(ang)
    cos2 = jnp.tile(jnp.concatenate([cos, cos], axis=1), (1, LANE // HD))
    sin2 = jnp.tile(jnp.concatenate([-sin, sin], axis=1), (1, LANE // HD))
    return cos2, sin2


def _rot_half(v):
    lane = lax.broadcasted_iota(jnp.int32, v.shape, 1)
    return jnp.where(lane % HD < HD // 2, pltpu.roll(v, LANE - HD // 2, 1), pltpu.roll(v, HD // 2, 1))


def _rope_fwd(name, proj, cos2, sin2):
    def body(p_ref, c_ref, s_ref, *outs):
        cv, sv = c_ref[...], s_ref[...]
        off = 0
        for o_ref, (_, w, rot, is_q) in zip(outs, GROUPS):
            for b in range(w // LANE):
                v = p_ref[:, off + b * LANE:off + (b + 1) * LANE]
                if rot:
                    v = v * cv + _rot_half(v) * sv
                if is_q:
                    v = v * (HD ** -0.5)
                o_ref[:, b * LANE:(b + 1) * LANE] = v.astype(BF16)
            off += w

    return pl.pallas_call(
        body, name=name, grid=(T // TR,), in_specs=[_rows(IN_COLS), _rows(LANE), _rows(LANE)],
        out_specs=[_rows(w) for _, w, _, _ in GROUPS],
        out_shape=[jax.ShapeDtypeStruct((T, w), BF16) for _, w, _, _ in GROUPS],
        compiler_params=_params(("parallel",)),
    )(proj, cos2, sin2)


def _rope_bwd(name, grads, cos2, sin2):
    def body(*refs):
        ins, (c_ref, s_ref, o_ref) = refs[:9], refs[9:]
        cv, sv = c_ref[...], s_ref[...]
        off = 0
        for d_ref, (_, w, rot, is_q) in zip(ins, GROUPS):
            for b in range(w // LANE):
                v = d_ref[:, b * LANE:(b + 1) * LANE]
                if is_q:
                    v = v * (HD ** -0.5)
                if rot:
                    v = v * cv + _rot_half(v * sv)
                o_ref[:, off + b * LANE:off + (b + 1) * LANE] = v.astype(BF16)
            off += w

    return pl.pallas_call(
        body, name=name, grid=(T // TR,), in_specs=[_rows(w) for _, w, _, _ in GROUPS] + [_rows(LANE), _rows(LANE)],
        out_specs=_rows(IN_COLS), out_shape=jax.ShapeDtypeStruct((T, IN_COLS), BF16),
        compiler_params=_params(("parallel",)),
    )(*grads, cos2, sin2)


NT_DIMS = (((1,), (1,)), ((), ()))
TN_DIMS = (((0,), (0,)), ((), ()))


def _scores(q, k, bias, valid):
    s = lax.dot_general(q, k, NT_DIMS, preferred_element_type=F32)
    if bias is not None:
        s = s + bias
    if valid is not None:
        s = jnp.where(valid, s, NEG)
    return s


def _heads_fwd(heads):
    scores = [_scores(h["q"], h["k"], h.get("bias"), h.get("valid")) for h in heads]
    soft = []
    for s, h in zip(scores, heads):
        m = jnp.max(s, axis=1, keepdims=True)
        e = jnp.exp(s - m)
        l = jnp.sum(e, axis=1, keepdims=True)
        if h.get("sink") is not None:
            l = l + jnp.exp(h["sink"] - m)
        soft.append((e.astype(BF16), l, m + jnp.log(l)))
    return [(jnp.dot(e, h["v"], preferred_element_type=F32) / l, lse) for (e, l, lse), h in zip(soft, heads)]


def _heads_bwd(heads):
    dobs = [h["do"].astype(BF16) for h in heads]
    scores = [_scores(h["q"], h["k"], h.get("bias"), h.get("valid")) for h in heads]
    dps = [lax.dot_general(dob, h["v"], NT_DIMS, preferred_element_type=F32) for dob, h in zip(dobs, heads)]
    mid = []
    for s, dp, h in zip(scores, dps, heads):
        p = jnp.exp(s - h["lse"])
        delta = jnp.sum(h["do"] * h["o"], axis=1, keepdims=True)
        ds = p * (dp - delta)
        dsink = None if h.get("sink") is None else -jnp.exp(h["sink"] - h["lse"]) * delta
        mid.append((p.astype(BF16), ds, dsink))
    out = []
    for (pb, ds, dsink), dob, h in zip(mid, dobs, heads):
        dsb = ds.astype(BF16)
        out.append((jnp.dot(dsb, h["k"], preferred_element_type=F32),
                    lax.dot_general(dsb, h["q"], TN_DIMS, preferred_element_type=F32),
                    lax.dot_general(pb, dob, TN_DIMS, preferred_element_type=F32), ds, dsink))
    return out


def _per_head(cols):
    return jnp.concatenate([jnp.broadcast_to(c, (c.shape[0], HD)) for c in cols], axis=1)


def _dilation_bias():
    d = jnp.arange(T, dtype=jnp.int32)[:, None] - jnp.arange(T, dtype=jnp.int32)[None, :]
    ad = jnp.abs(d)
    count = jnp.zeros((T, T), jnp.int32)
    for window, r in ((128, 1), (512, 4), (2048, 16)):
        count += ((ad % r == 0) & (ad // r <= window // (2 * r))).astype(jnp.int32)
    return jnp.where(count > 0, jnp.log(jnp.maximum(count, 1).astype(F32)), NEG)


BQ_A = 256


def _attn_a_fwd(name, qa, ka, va, bias):
    def body(q_ref, k_ref, v_ref, b_ref, o_ref, lse_ref):
        b = b_ref[...]
        outs = _heads_fwd([dict(q=q_ref[:, h * HD:(h + 1) * HD], k=k_ref[:, h * HD:(h + 1) * HD],
                                v=v_ref[:, h * HD:(h + 1) * HD], bias=b) for h in range(2)])
        o_ref[...] = jnp.concatenate([o for o, _ in outs], axis=1)
        lse_ref[...] = _per_head([lse for _, lse in outs])

    qs = pl.BlockSpec((BQ_A, LANE), lambda p, i: (i, p))
    ks = pl.BlockSpec((T, LANE), lambda p, i: (0, p))
    return pl.pallas_call(
        body, name=name, grid=(HA // 2, T // BQ_A),
        in_specs=[qs, ks, ks, pl.BlockSpec((BQ_A, T), lambda p, i: (i, 0))], out_specs=[qs, qs],
        out_shape=[jax.ShapeDtypeStruct((T, WA), F32)] * 2, compiler_params=_params(("parallel", "parallel")),
    )(qa, ka, va, bias)


def _attn_a_bwd(name, qa, ka, va, oa, lse, doa, bias):
    def body(q_ref, k_ref, v_ref, o_ref, lse_ref, do_ref, b_ref, dq_ref, dk_ref, dv_ref):
        b = b_ref[...]
        sls = [slice(h * HD, (h + 1) * HD) for h in range(2)]
        res = _heads_bwd([dict(q=q_ref[:, sl], k=k_ref[:, sl], v=v_ref[:, sl], o=o_ref[:, sl], do=do_ref[:, sl],
                               lse=lse_ref[:, sl.start:sl.start + 1], bias=b) for sl in sls])
        dq_ref[...] = jnp.concatenate([r[0] for r in res], axis=1)
        dk2, dv2 = jnp.concatenate([r[1] for r in res], axis=1), jnp.concatenate([r[2] for r in res], axis=1)

        @pl.when(pl.program_id(1) == 0)
        def _():
            dk_ref[...] = dk2
            dv_ref[...] = dv2

        @pl.when(pl.program_id(1) > 0)
        def _():
            dk_ref[...] += dk2
            dv_ref[...] += dv2

    qs = pl.BlockSpec((BQ_A, LANE), lambda p, i: (i, p))
    ks = pl.BlockSpec((T, LANE), lambda p, i: (0, p))
    return pl.pallas_call(
        body, name=name, grid=(HA // 2, T // BQ_A),
        in_specs=[qs, ks, ks, qs, qs, qs, pl.BlockSpec((BQ_A, T), lambda p, i: (i, 0))], out_specs=[qs, ks, ks],
        out_shape=[jax.ShapeDtypeStruct((T, WA), F32)] * 3, compiler_params=_params(("parallel", "arbitrary")),
    )(qa, ka, va, oa, lse, doa, bias)


BQ_B = 128
SPAN_B = BQ_B + 2 * WINDOW_B


def _window_b(i):
    start = pl.multiple_of(jnp.clip(i * BQ_B - WINDOW_B, 0, T - SPAN_B), BQ_B)
    qpos = i * BQ_B + lax.broadcasted_iota(jnp.int32, (BQ_B, SPAN_B), 0)
    kpos = start + lax.broadcasted_iota(jnp.int32, (BQ_B, SPAN_B), 1)
    return start, jnp.abs(qpos - kpos) <= WINDOW_B


GROUP_B = HB // HKV


def _stack_group(ref, g):
    return jnp.concatenate([ref[:, h * HD:(h + 1) * HD] for h in range(g * GROUP_B, (g + 1) * GROUP_B)], axis=0)


def _sink_column(sink_ref, g):
    return jnp.concatenate([jnp.full((BQ_B, 1), sink_ref[h], F32) for h in range(g * GROUP_B, (g + 1) * GROUP_B)],
                           axis=0)


def _unstack(stacked):
    return [s[j * BQ_B:(j + 1) * BQ_B] for s in stacked for j in range(GROUP_B)]


def _attn_b_fwd(name, qb, kb, vb, sink):
    def body(sink_ref, q_ref, k_ref, v_ref, o_ref, lse_ref):
        start, valid = _window_b(pl.program_id(0))
        valid = jnp.concatenate([valid] * GROUP_B, axis=0)
        kw, vw = k_ref[pl.ds(start, SPAN_B), :], v_ref[pl.ds(start, SPAN_B), :]
        outs = _heads_fwd([dict(q=_stack_group(q_ref, g), k=kw[:, g * HD:(g + 1) * HD], v=vw[:, g * HD:(g + 1) * HD],
                                valid=valid, sink=_sink_column(sink_ref, g)) for g in range(HKV)])
        o_ref[...] = jnp.concatenate(_unstack([o for o, _ in outs]), axis=1)
        lse_ref[...] = _per_head(_unstack([lse for _, lse in outs]))

    qs = pl.BlockSpec((BQ_B, WB), lambda i: (i, 0))
    return pl.pallas_call(
        body, name=name, grid=(T // BQ_B,),
        in_specs=[pl.BlockSpec(memory_space=pltpu.SMEM), qs, _whole((T, WKV)), _whole((T, WKV))],
        out_specs=[qs, qs],
        out_shape=[jax.ShapeDtypeStruct((T, WB), F32)] * 2, compiler_params=_params(("parallel",)),
    )(sink, qb, kb, vb)


def _attn_b_bwd(name, qb, kb, vb, ob, lse, dob, sink):
    def body(sink_ref, q_ref, k_ref, v_ref, o_ref, lse_ref, do_ref, dq_ref, dk_ref, dv_ref, dsink_ref):
        i = pl.program_id(0)
        start, valid = _window_b(i)
        valid = jnp.concatenate([valid] * GROUP_B, axis=0)
        kw, vw = k_ref[pl.ds(start, SPAN_B), :], v_ref[pl.ds(start, SPAN_B), :]
        res = _heads_bwd([dict(q=_stack_group(q_ref, g), k=kw[:, g * HD:(g + 1) * HD], v=vw[:, g * HD:(g + 1) * HD],
                               o=_stack_group(o_ref, g), do=_stack_group(do_ref, g),
                               lse=jnp.concatenate([lse_ref[:, h * HD:h * HD + 1]
                                                    for h in range(g * GROUP_B, (g + 1) * GROUP_B)], axis=0),
                               valid=valid, sink=_sink_column(sink_ref, g)) for g in range(HKV)])
        dks, dvs = [r[1] for r in res], [r[2] for r in res]
        lane = lax.broadcasted_iota(jnp.int32, (1, LANE), 1)
        dsink = jnp.zeros((1, LANE), F32)
        for h, rows in enumerate(_unstack([r[4] for r in res])):
            dsink += jnp.where(lane == h, jnp.sum(rows), 0.0)
        dq_ref[...] = jnp.concatenate(_unstack([r[0] for r in res]), axis=1)

        @pl.when(i == 0)
        def _():
            dk_ref[...] = jnp.zeros_like(dk_ref)
            dv_ref[...] = jnp.zeros_like(dv_ref)
            dsink_ref[...] = jnp.zeros_like(dsink_ref)

        dk_ref[pl.ds(start, SPAN_B), :] += jnp.concatenate(dks, axis=1)
        dv_ref[pl.ds(start, SPAN_B), :] += jnp.concatenate(dvs, axis=1)
        dsink_ref[...] += dsink

    qs = pl.BlockSpec((BQ_B, WB), lambda i: (i, 0))
    return pl.pallas_call(
        body, name=name, grid=(T // BQ_B,),
        in_specs=[pl.BlockSpec(memory_space=pltpu.SMEM), qs, _whole((T, WKV)), _whole((T, WKV)), qs, qs, qs],
        out_specs=[qs, _whole((T, WKV)), _whole((T, WKV)), _whole((1, LANE))],
        out_shape=[jax.ShapeDtypeStruct((T, WB), F32), jax.ShapeDtypeStruct((T, WKV), F32),
                   jax.ShapeDtypeStruct((T, WKV), F32), jax.ShapeDtypeStruct((1, LANE), F32)],
        compiler_params=_params(("arbitrary",)),
    )(sink, qb, kb, vb, ob, lse, dob)


SPAN_C = NA_ROWS * GRID_W


def _row_start(r):
    return jnp.clip(r - NA_ROWS // 2, 0, ROWS - NA_ROWS)


def _off_index(r):
    return _row_start(r) - r + (NA_ROWS - 1)


N_TAB = 16
RPS = 4


def _rpb_tables(rpb):
    w129 = jnp.concatenate([rpb[..., NA_COLS - 1:], jnp.zeros(rpb.shape[:2] + (129 - (2 * NA_COLS - 1),), F32),
                            rpb[..., :NA_COLS - 1]], axis=-1)
    toep = jnp.tile(w129, (1, 1, GRID_W))[..., :GRID_W * LANE].reshape(HC, 2 * NA_ROWS - 1, GRID_W, LANE)
    pairs = jnp.concatenate([toep[:, :-1, :, :GRID_W], toep[:, 1:, :, :GRID_W]], axis=-1)
    c = jnp.arange(GRID_W)[:, None]
    kc = jnp.arange(LANE)[None, :] % GRID_W
    cs = jnp.clip(c - NA_COLS // 2, 0, GRID_W - NA_COLS)
    pairs = jnp.where((kc >= cs) & (kc < cs + NA_COLS), pairs, NEG)
    return jnp.pad(pairs, ((0, 0), (0, N_TAB - pairs.shape[1]), (0, 0), (0, 0)))


def _bias_c(t_ref, h, d):
    return jnp.concatenate([t_ref[h, d + k] for k in range(0, NA_ROWS, 2)], axis=1)


def _attn_c_fwd(name, qc, kc, vc, tables):
    def body(q_ref, k_ref, v_ref, t_ref, o_ref, lse_ref):
        heads = []
        for rr in range(RPS):
            r = pl.program_id(1) * RPS + rr
            rows = slice(rr * GRID_W, (rr + 1) * GRID_W)
            start = pl.multiple_of(_row_start(r) * GRID_W, GRID_W)
            kw, vw = k_ref[pl.ds(start, SPAN_C), :], v_ref[pl.ds(start, SPAN_C), :]
            heads += [dict(q=q_ref[rows, h * HD:(h + 1) * HD], k=kw[:, h * HD:(h + 1) * HD], v=vw[:, h * HD:(h + 1) * HD],
                           bias=_bias_c(t_ref, h, _off_index(r))) for h in range(2)]
        outs = _heads_fwd(heads)
        for rr in range(RPS):
            rows = slice(rr * GRID_W, (rr + 1) * GRID_W)
            o_ref[rows, :] = jnp.concatenate([o for o, _ in outs[2 * rr:2 * rr + 2]], axis=1)
            lse_ref[rows, :] = _per_head([lse for _, lse in outs[2 * rr:2 * rr + 2]])

    qs = pl.BlockSpec((RPS * GRID_W, LANE), lambda p, r: (r, p))
    ks = pl.BlockSpec((T, LANE), lambda p, r: (0, p))
    ts = pl.BlockSpec((2, N_TAB, GRID_W, LANE), lambda p, r: (p, 0, 0, 0))
    return pl.pallas_call(
        body, name=name, grid=(HC // 2, ROWS // RPS), in_specs=[qs, ks, ks, ts], out_specs=[qs, qs],
        out_shape=[jax.ShapeDtypeStruct((T, WC), F32)] * 2, compiler_params=_params(("parallel", "parallel")),
    )(qc, kc, vc, tables)


def _attn_c_bwd(name, qc, kc, vc, oc, lse, doc, tables):
    def body(q_ref, k_ref, v_ref, o_ref, lse_ref, do_ref, t_ref, dq_ref, dk_ref, dv_ref, dt_ref):
        @pl.when(pl.program_id(1) == 0)
        def _():
            dk_ref[...] = jnp.zeros_like(dk_ref)
            dv_ref[...] = jnp.zeros_like(dv_ref)
            dt_ref[...] = jnp.zeros_like(dt_ref)

        heads, where = [], []
        for rr in range(RPS):
            r = pl.program_id(1) * RPS + rr
            rows = slice(rr * GRID_W, (rr + 1) * GRID_W)
            d = _off_index(r)
            start = pl.multiple_of(_row_start(r) * GRID_W, GRID_W)
            kw, vw = k_ref[pl.ds(start, SPAN_C), :], v_ref[pl.ds(start, SPAN_C), :]
            where.append((rows, d, start))
            for h in range(2):
                sl = slice(h * HD, (h + 1) * HD)
                heads.append(dict(q=q_ref[rows, sl], k=kw[:, sl], v=vw[:, sl], o=o_ref[rows, sl], do=do_ref[rows, sl],
                                  lse=lse_ref[rows, h * HD:h * HD + 1], bias=_bias_c(t_ref, h, d)))
        res = _heads_bwd(heads)
        for rr, (rows, d, start) in enumerate(where):
            pair = res[2 * rr:2 * rr + 2]
            for h in range(2):
                for k in range(0, NA_ROWS, 2):
                    dt_ref[h, d + k] += pair[h][3][:, k * GRID_W:(k + 2) * GRID_W]
            dq_ref[rows, :] = jnp.concatenate([p[0] for p in pair], axis=1)
            dk_ref[pl.ds(start, SPAN_C), :] += jnp.concatenate([p[1] for p in pair], axis=1)
            dv_ref[pl.ds(start, SPAN_C), :] += jnp.concatenate([p[2] for p in pair], axis=1)

    qs = pl.BlockSpec((RPS * GRID_W, LANE), lambda p, r: (r, p))
    ks = pl.BlockSpec((T, LANE), lambda p, r: (0, p))
    ts = pl.BlockSpec((2, N_TAB, GRID_W, LANE), lambda p, r: (p, 0, 0, 0))
    return pl.pallas_call(
        body, name=name, grid=(HC // 2, ROWS // RPS), in_specs=[qs, ks, ks, qs, qs, qs, ts],
        out_specs=[qs, ks, ks, ts],
        out_shape=[jax.ShapeDtypeStruct((T, WC), F32)] * 3 + [jax.ShapeDtypeStruct((HC, N_TAB, GRID_W, LANE), F32)],
        compiler_params=_params(("parallel", "arbitrary")),
    )(qc, kc, vc, oc, lse, doc, tables)


def _split3(v):
    hi = v.astype(BF16)
    r1 = v - hi.astype(F32)
    mid = r1.astype(BF16)
    lo = (r1 - mid.astype(F32)).astype(BF16)
    return hi, mid, lo


def _rpb_reduce(name, dtables):
    x = dtables.reshape(HC, N_TAB, GRID_W * LANE)
    c = jnp.arange(GRID_W)[:, None]
    lane = jnp.arange(LANE)[None, :]
    col = (lane // GRID_W) * LANE + jnp.clip(lane % GRID_W - c + (NA_COLS - 1), 0, 2 * NA_COLS - 2)
    col_onehot = (col.reshape(-1)[:, None] == jnp.arange(2 * LANE)[None, :]).astype(BF16)
    a2 = jnp.arange(N_TAB)[None, :]
    row_onehot = jnp.concatenate([(jnp.arange(16)[:, None] == a2 + u) & (a2 < 2 * NA_ROWS - 2) for u in range(2)],
                                 axis=1).astype(BF16)

    def body(x_ref, e_ref, f_ref, o_ref):
        y = sum(jnp.dot(part, e_ref[...], preferred_element_type=F32) for part in _split3(x_ref[...]))
        z = jnp.concatenate([y[:, :LANE], y[:, LANE:]], axis=0)
        o_ref[...] = sum(jnp.dot(f_ref[...], part, preferred_element_type=F32) for part in _split3(z))

    out = pl.pallas_call(
        body, name=name, grid=(HC,),
        in_specs=[pl.BlockSpec((None, N_TAB, GRID_W * LANE), lambda h: (h, 0, 0)),
                  _whole((GRID_W * LANE, 2 * LANE)), _whole((16, 2 * N_TAB))],
        out_specs=pl.BlockSpec((None, 16, LANE), lambda h: (h, 0, 0)),
        out_shape=jax.ShapeDtypeStruct((HC, 16, LANE), F32), compiler_params=_params(("parallel",)),
    )(x, col_onehot, row_onehot)
    return out[:, :2 * NA_ROWS - 1, :2 * NA_COLS - 1]


TC = 128
NCB = DFF // TC


def _shift_down(v, rows):
    return jnp.where(rows == 0, 0.0, pltpu.roll(v, 1, 0))


def _shift_up(v, rows):
    return jnp.where(rows == T - 1, 0.0, pltpu.roll(v, T - 1, 0))


def _conv(v, w, b, rows):
    return _shift_down(v, rows) * w[0:1] + v * w[1:2] + _shift_up(v, rows) * w[2:3] + b


def _ffn_specs():
    gate = lambda shape: pl.BlockSpec(shape, lambda j: (0, j))
    val = lambda shape: pl.BlockSpec(shape, lambda j: (0, j + NCB))
    return [gate((T, TC)), val((T, TC)), gate((3, TC)), val((3, TC)), gate((1, TC)), val((1, TC))]


def _ffn_mid_fwd(name, up, conv_w, conv_b):
    def body(xg_ref, xv_ref, wg_ref, wv_ref, bg_ref, bv_ref, o_ref):
        rows = lax.broadcasted_iota(jnp.int32, (T, TC), 0)
        ug = _conv(xg_ref[...], wg_ref[...], bg_ref[...], rows)
        uv = _conv(xv_ref[...], wv_ref[...], bv_ref[...], rows)
        o_ref[...] = (ug * jax.nn.sigmoid(ug) * uv).astype(BF16)

    return pl.pallas_call(
        body, name=name, grid=(NCB,), in_specs=_ffn_specs(), out_specs=pl.BlockSpec((T, TC), lambda j: (0, j)),
        out_shape=jax.ShapeDtypeStruct((T, DFF), BF16), compiler_params=_params(("parallel",)),
    )(up, up, conv_w, conv_w, conv_b, conv_b)


def _ffn_mid_bwd(name, dact, up, conv_w, conv_b):
    def body(da_ref, xg_ref, xv_ref, wg_ref, wv_ref, bg_ref, bv_ref, dx_ref, dw_ref, db_ref):
        rows = lax.broadcasted_iota(jnp.int32, (T, TC), 0)
        xg, xv, wg, wv = xg_ref[...], xv_ref[...], wg_ref[...], wv_ref[...]
        ug = _conv(xg, wg, bg_ref[...], rows)
        uv = _conv(xv, wv, bv_ref[...], rows)
        sg = jax.nn.sigmoid(ug)
        da = da_ref[...]
        dug = da * uv * (sg * (1.0 + ug * (1.0 - sg)))
        duv = da * (ug * sg)
        for half, (xin, w, du) in enumerate(((xg, wg, dug), (xv, wv, duv))):
            dx = _shift_up(du, rows) * w[0:1] + du * w[1:2] + _shift_down(du, rows) * w[2:3]
            dx_ref[half] = dx.astype(BF16)
            dw_ref[half] = jnp.concatenate(
                [jnp.sum(_shift_down(xin, rows) * du, axis=0, keepdims=True), jnp.sum(xin * du, axis=0, keepdims=True),
                 jnp.sum(_shift_up(xin, rows) * du, axis=0, keepdims=True)], axis=0)
            db_ref[half] = jnp.sum(du, axis=0, keepdims=True)

    return pl.pallas_call(
        body, name=name, grid=(NCB,), in_specs=[pl.BlockSpec((T, TC), lambda j: (0, j))] + _ffn_specs(),
        out_specs=[pl.BlockSpec((2, T, TC), lambda j: (0, 0, j)), pl.BlockSpec((2, 3, TC), lambda j: (0, 0, j)),
                   pl.BlockSpec((2, 1, TC), lambda j: (0, 0, j))],
        out_shape=[jax.ShapeDtypeStruct((2, T, DFF), BF16), jax.ShapeDtypeStruct((2, 3, DFF), F32),
                   jax.ShapeDtypeStruct((2, 1, DFF), F32)],
        compiler_params=_params(("parallel",)),
    )(dact, up, up, conv_w, conv_w, conv_b, conv_b)


def _dup_spec(tm, nj):
    per = DFF // nj
    return pl.BlockSpec((None, tm, nj), lambda a, b, j: (j // per, 0 if tm == T else b, j % per))


def _dup_spec_tn(tm, nj):
    per = DFF // nj
    return pl.BlockSpec((None, tm, nj), lambda j, kt, r: (j // per, 0, j % per))


def _adamw_math(w, g, m, v):
    m = ADAM_B1 * m + (1.0 - ADAM_B1) * g
    v = ADAM_B2 * v + (1.0 - ADAM_B2) * (g * g)
    m_hat = m / (1.0 - ADAM_B1 ** ADAM_STEP)
    v_hat = v / (1.0 - ADAM_B2 ** ADAM_STEP)
    delta = -ADAM_LR * (m_hat / (jnp.sqrt(v_hat) + ADAM_EPS) + ADAM_WD * w)
    return delta, m, v


ADAM_BLOCK = 256 * 1408


def _adamw_sharded(name, w, m, v, parts):
    _, r, c = w.shape
    tr = max(t for t in range(16, r + 1, 16) if r % t == 0 and t * c <= ADAM_BLOCK)

    def body(w_ref, m_ref, v_ref, p0_ref, p1_ref, g_ref, d_ref, nm_ref, nv_ref):
        def run(p_ref):
            g = p_ref[0].astype(F32)
            for k in range(1, N_DEV):
                g = g + p_ref[k].astype(F32)
            d, nm, nv = _adamw_math(w_ref[...], g, m_ref[...], v_ref[...])
            g_ref[...] = g
            d_ref[...] = d
            nm_ref[...] = nm
            nv_ref[...] = nv

        @pl.when(pl.program_id(0) == 0)
        def _():
            run(p0_ref)

        @pl.when(pl.program_id(0) == 1)
        def _():
            run(p1_ref)

    ws = pl.BlockSpec((None, tr, c), lambda l, i: (l, i, 0))
    p0 = pl.BlockSpec((N_DEV, tr, c), lambda l, i: (0, jnp.where(l == 0, i, r // tr - 1), 0))
    p1 = pl.BlockSpec((N_DEV, tr, c), lambda l, i: (0, jnp.where(l == 1, i, 0), 0))
    return pl.pallas_call(
        body, name=name, grid=(DEPTH, r // tr), in_specs=[ws, ws, ws, p0, p1], out_specs=[ws] * 4,
        out_shape=[jax.ShapeDtypeStruct(w.shape, F32)] * 4, compiler_params=_params(("arbitrary", "arbitrary")),
    )(w, m, v, *parts)


def _sum_devices(name, parts):
    r = parts.shape[1]

    def body(p_ref, o_ref):
        g = p_ref[0]
        for k in range(1, N_DEV):
            g = g + p_ref[k]
        o_ref[...] = g

    return pl.pallas_call(
        body, name=name, in_specs=[pl.BlockSpec((N_DEV, r, LANE), lambda: (0, 0, 0))],
        out_specs=pl.BlockSpec((r, LANE), lambda: (0, 0)), out_shape=jax.ShapeDtypeStruct((r, LANE), F32),
        compiler_params=_params(),
    )(parts)


def _adamw_small(name, w, g, m, v):
    spec = pl.BlockSpec(w.shape, lambda: (0, 0))

    def body(w_ref, g_ref, m_ref, v_ref, d_ref, nm_ref, nv_ref):
        d, nm, nv = _adamw_math(w_ref[...], g_ref[...], m_ref[...], v_ref[...])
        d_ref[...] = d
        nm_ref[...] = nm
        nv_ref[...] = nv

    return pl.pallas_call(
        body, name=name, in_specs=[spec] * 4, out_specs=[spec] * 3,
        out_shape=[jax.ShapeDtypeStruct(w.shape, F32)] * 3, compiler_params=_params(),
    )(w, g, m, v)


def _pack(arrays):
    flat = jnp.concatenate([a.reshape(-1) for a in arrays])
    pad = (-flat.shape[0]) % (8 * LANE)
    return jnp.pad(flat, (0, pad)).reshape(-1, LANE)


def _unpack(buf, shapes):
    flat, out, off = buf.reshape(-1), [], 0
    for s in shapes:
        n = 1
        for d in s:
            n *= d
        out.append(flat[off:off + n].reshape(s))
        off += n
    return out


def _local_step(x, target, small, weights, conv_w_full, hand_over):
    cos2, sin2 = _rope_tables()
    bias_a = _dilation_bias()
    saved = []
    for l in range(DEPTH):
        g1, g2 = small["ln_attn"][l][None], small["ln_ffn"][l][None]
        gain, sink, cb = small["mix_gain"][l][None], small["sink_b"][l], small["conv_b"][l][None]
        cw = conv_w_full[l]
        bias = _rpb_tables(small["rpb_c"][l])
        h1 = _rmsnorm_fwd(f"norm_attn_{l}", x, g1)
        proj = _nn_cols(f"proj_in_{l}", h1, weights("w_in", l, h1))
        qa, ka, va, qb, kb, vb, qc, kc, vc = _rope_fwd(f"rope_{l}", proj, cos2, sin2)
        oa, lse_a = _attn_a_fwd(f"attn_a_{l}", qa, ka, va, bias_a)
        ob, lse_b = _attn_b_fwd(f"attn_b_{l}", qb, kb, vb, sink)
        oc, lse_c = _attn_c_fwd(f"attn_c_{l}", qc, kc, vc, bias)
        mixed = _mix_fwd(f"mix_{l}", oa, ob, oc, gain)
        x_mid = _nn_rows(f"proj_out_{l}", mixed, weights("w_out", l, mixed), x, 8, 512)
        h2 = _rmsnorm_fwd(f"norm_ffn_{l}", x_mid, g2)
        up = _nn_cols(f"ffn_up_{l}", h2, weights("w_up", l, h2))
        act = _ffn_mid_fwd(f"ffn_mid_{l}", up, cw, cb)
        x_out = _nn_rows(f"ffn_down_{l}", act, weights("w_down", l, act), x_mid, 4, 1024)
        saved.append(dict(x=x, h1=h1, qkv=(qa, ka, va, qb, kb, vb, qc, kc, vc), o=(oa, ob, oc), lse=(lse_a, lse_b, lse_c), mixed=mixed,
                          x_mid=x_mid, h2=h2, up=up, act=act, g1=g1, g2=g2, gain=gain, sink=sink, cb=cb, cw=cw, bias=bias))
        x = x_out

    loss8, dx, dxb, d_ln_final = _loss_head(x, small["ln_final"][None], target)
    sgrads = [None] * DEPTH
    for l in reversed(range(DEPTH)):
        s = saved[l]
        qa, ka, va, qb, kb, vb, qc, kc, vc = s["qkv"]
        oa, ob, oc = s["o"]
        wg_in, wg_out = weights("w_in", l, None), weights("w_out", l, None)
        wg_up, wg_down = weights("w_up", l, None), weights("w_down", l, None)
        g_down = _tn_rows(f"wgrad_down_{l}", s["act"], dxb, wg_down.shape[1], 2)
        zero = hand_over("w_down", l, g_down)
        dact = _nt_rows(f"dgrad_down_{l}", dxb, wg_down, 2)
        dup, d_cw, d_cb = _ffn_mid_bwd(f"ffn_mid_bwd_{l}", dact, s["up"], s["cw"], s["cb"] + zero)
        g_up = _tn_cols(f"wgrad_up_{l}", s["h2"], dup, _dup_spec_tn, wg_up.shape[2])
        zero = hand_over("w_up", l, g_up)
        dh2 = _nt_cols(f"dgrad_up_{l}", dup, _dup_spec, wg_up, 2)
        dx, dxb, d_g2 = _rmsnorm_bwd(f"norm_ffn_bwd_{l}", dh2, s["x_mid"], s["g2"] + zero, dx)
        g_out = _tn_rows(f"wgrad_out_{l}", s["mixed"], dxb, wg_out.shape[1], 2)
        zero = hand_over("w_out", l, g_out)
        dmixed = _nt_rows(f"dgrad_out_{l}", dxb, wg_out, 2)
        doa, dob, doc, d_gain = _mix_bwd(f"mix_bwd_{l}", dmixed, oa, ob, oc, s["gain"] + zero)
        lse_a, lse_b, lse_c = s["lse"]
        dqa, dka, dva = _attn_a_bwd(f"attn_a_bwd_{l}", qa, ka, va, oa, lse_a, doa, bias_a)
        dqb, dkb, dvb, d_sink = _attn_b_bwd(f"attn_b_bwd_{l}", qb, kb, vb, ob, lse_b, dob, s["sink"])
        dqc, dkc, dvc, d_bias = _attn_c_bwd(f"attn_c_bwd_{l}", qc, kc, vc, oc, lse_c, doc, s["bias"])
        d_rpb = _rpb_reduce(f"rpb_reduce_{l}", d_bias)
        dproj = _rope_bwd(f"rope_bwd_{l}", (dqa, dka, dva, dqb, dkb, dvb, dqc, dkc, dvc), cos2, sin2)
        g_in = _tn_cols(f"wgrad_in_{l}", s["h1"], dproj,
                        lambda tm, nj: pl.BlockSpec((tm, nj), lambda j, kt, r: (0, j)), wg_in.shape[2])
        zero = hand_over("w_in", l, g_in)
        dh1 = _nt_cols(f"dgrad_in_{l}", dproj, lambda tm, nj: pl.BlockSpec((tm, nj), lambda kt, i, j: (i, j)), wg_in, 4)
        dx, dxb, d_g1 = _rmsnorm_bwd(f"norm_attn_bwd_{l}", dh1, s["x"], s["g1"] + zero, dx)
        sgrads[l] = dict(ln_attn=d_g1[0], sink_b=d_sink[0, :HB], rpb_c=d_rpb, mix_gain=d_gain[0], ln_ffn=d_g2[0],
                         conv_w=d_cw.transpose(1, 0, 2).reshape(3, 2 * DFF), conv_b=d_cb.reshape(2 * DFF))
    return loss8[0, 0], dx, d_ln_final[0], sgrads


SMALL_NAMES = ("ln_attn", "sink_b", "rpb_c", "mix_gain", "ln_ffn", "conv_b")


def kernel(x, ln_attn, w_in, sink_b, rpb_c, mix_gain, w_out, ln_ffn, w_up, conv_w, conv_b, w_down, ln_final, loss_target, m_ln_attn, m_w_in, m_sink_b, m_rpb_c, m_mix_gain, m_w_out, m_ln_ffn, m_w_up, m_conv_w, m_conv_b, m_w_down, m_ln_final, v_ln_attn, v_w_in, v_sink_b, v_rpb_c, v_mix_gain, v_w_out, v_ln_ffn, v_w_up, v_conv_w, v_conv_b, v_w_down, v_ln_final):
    me = 4 * lax.axis_index("x") + 2 * lax.axis_index("y") + lax.axis_index("c")
    small = dict(ln_attn=ln_attn, sink_b=sink_b, rpb_c=rpb_c, mix_gain=mix_gain, ln_ffn=ln_ffn, conv_b=conv_b,
                 ln_final=ln_final)

    names = ("w_in", "w_out", "w_up", "w_down")
    shards = dict(w_in=w_in, w_out=w_out, w_up=w_up, w_down=w_down)
    order = [(n, l) for l in range(DEPTH) for n in names]
    keys = [("conv_w", 0)] + order
    srcs = [_pack([conv_w])] + [shards[n][l].astype(BF16) for n, l in order]
    peer_sets = [ALL_PEERS] + [NEAR_PEERS] * len(order)
    nk = len(keys)
    send_a, recv_a, bufs_a, token = _copy_start(
        "gather_start", srcs + [lax.empty((N_DEV,) + s.shape, s.dtype) for s in srcs], _gather_plan(peer_sets),
        [len(p) for p in peer_sets])
    arrived, forwarded, gathered = {}, {}, {}

    def arrive(k, after):
        i = keys.index(k)
        arrived[k] = _copy_wait(f"gather_{k[0]}_{k[1]}_arrived", [bufs_a[i], bufs_a[nk + i]], [send_a[i]],
                                [recv_a[i]], _gather_plan([peer_sets[i]]), after)

    def forward(k, after):
        arrive(k, after)
        forwarded[k] = _copy_start(f"gather_{k[0]}_{k[1]}_forward", [arrived[k][1]], _forward_plan, [len(OTHER_CHIPS)])
        return forwarded[k][3]

    def complete(k, land):
        return lax.dynamic_update_slice_in_dim(land, arrived[k][0][None], me, axis=0)

    def weights(n, l, after):
        k = (n, l)
        if k not in gathered:
            i = order.index(k)
            if k not in forwarded:
                forward(k, after)
            token_next = forward(order[i + 1], after) if i + 1 < len(order) else after
            send_b, recv_b, (land,), _ = forwarded[k]
            (land,) = _copy_wait(f"gather_{n}_{l}_done", [land], send_b, recv_b, _forward_plan, token_next)
            gathered[k] = complete(k, land)
        return gathered[k]

    pending = {}

    def hand_over(n, l, g):
        send, recv, bufs, tok = _copy_start(f"send_grad_{n}_{l}", [g, lax.empty(g.shape, g.dtype)], _scatter_plan,
                                            [len(ALL_PEERS)])
        pending[(n, l)] = (send, recv, bufs)
        return tok[0, 0]

    def received(k, after):
        send, recv, bufs = pending[k]
        src, land = _copy_wait(f"recv_grad_{k[0]}_{k[1]}", bufs, send, recv, _scatter_plan, after)
        return lax.dynamic_update_slice_in_dim(land, lax.dynamic_slice_in_dim(src, me, 1, axis=0), me, axis=0)

    arrive(keys[0], token)
    cw_all = complete(keys[0], arrived[keys[0]][1])
    nup = w_up.shape[2]
    cw_shards = cw_all.reshape(N_DEV, -1)[:, :DEPTH * 3 * nup].reshape(N_DEV, DEPTH, 3, nup)
    conv_w_full = cw_shards.transpose(1, 2, 0, 3).reshape(DEPTH, 3, N_DEV * nup)

    loss_local, dx, d_ln_final, sgrads = _local_step(
        x[0], loss_target[0], dict(small, ln_attn=ln_attn + token[0, 0]), weights, conv_w_full, hand_over)

    stacked = [jnp.stack([sgrads[l][n] for l in range(DEPTH)]) for n in SMALL_NAMES + ("conv_w",)] + [d_ln_final]
    shapes = [a.shape for a in stacked]
    mine = _pack(stacked)
    send_s, recv_s, bufs_s, _ = _copy_start("gather_small_grads_start", [mine, lax.empty((N_DEV,) + mine.shape, F32)],
                                            _gather_plan([ALL_PEERS]), [len(ALL_PEERS)])

    big, after = {}, dx
    moments = dict(w_in=(m_w_in, v_w_in), w_out=(m_w_out, v_w_out), w_up=(m_w_up, v_w_up), w_down=(m_w_down, v_w_down))
    for n in reversed(names):
        parts = (received((n, 0), after), received((n, 1), after))
        big[n] = _adamw_sharded(f"adamw_{n}", shards[n], *moments[n], parts)
        after = big[n][1]

    mine, land = _copy_wait("gather_small_grads_done", bufs_s, send_s, recv_s, _gather_plan([ALL_PEERS]), after)
    everyone = lax.dynamic_update_slice_in_dim(land, mine[None], me, axis=0)
    g_small = _unpack(_sum_devices("sum_small_grads", everyone), shapes)
    g = dict(zip(SMALL_NAMES + ("conv_w", "ln_final"), g_small))
    g["conv_w"] = lax.dynamic_slice_in_dim(g["conv_w"], me * nup, nup, axis=2)

    snames = SMALL_NAMES + ("conv_w", "ln_final")
    sw = dict(small, conv_w=conv_w)
    sm = dict(ln_attn=m_ln_attn, sink_b=m_sink_b, rpb_c=m_rpb_c, mix_gain=m_mix_gain, ln_ffn=m_ln_ffn,
              conv_b=m_conv_b, conv_w=m_conv_w, ln_final=m_ln_final)
    sv = dict(ln_attn=v_ln_attn, sink_b=v_sink_b, rpb_c=v_rpb_c, mix_gain=v_mix_gain, ln_ffn=v_ln_ffn,
              conv_b=v_conv_b, conv_w=v_conv_w, ln_final=v_ln_final)
    sshapes = [sw[n].shape for n in snames]
    packed = _adamw_small("adamw_small", _pack([sw[n] for n in snames]), _pack([g[n] for n in snames]),
                          _pack([sm[n] for n in snames]), _pack([sv[n] for n in snames]))
    s_delta, s_m, s_v = (dict(zip(snames, _unpack(buf, sshapes))) for buf in packed)

    loss = lax.psum(loss_local, ("x", "y", "c"))
    outputs = ("ln_attn", "w_in", "sink_b", "rpb_c", "mix_gain", "w_out", "ln_ffn", "w_up", "conv_w", "conv_b",
               "w_down", "ln_final")
    grads = [big[n][0] if n in big else g[n] for n in outputs]
    deltas = [big[n][1] if n in big else s_delta[n] for n in outputs]
    new_m = [big[n][2] if n in big else s_m[n] for n in outputs]
    new_v = [big[n][3] if n in big else s_v[n] for n in outputs]
    return (loss, dx[None], *grads, *deltas, *new_m, *new_v)
```

```python
import functools

import jax
import jax.numpy as jnp
from jax import lax
from jax.experimental import pallas as pl
from jax.experimental.pallas import tpu as pltpu

F32 = jnp.float32
BF16 = jnp.bfloat16

N_DEV = 8
T = 2048
D = 2048
DEPTH = 2
HD = 64
HA, HB, HKV, HC = 12, 10, 2, 10
WA, WB, WKV, WC = HA * HD, HB * HD, HKV * HD, HC * HD
IN_COLS = 3 * WA + WB + 2 * WKV + 3 * WC
DFF = 5632
GRID_W = 64
ROWS = T // GRID_W
NA_ROWS, NA_COLS = 8, 16
WINDOW_B = 128
EPS = 1e-6
NEG = -1e30
ROPE_THETA = 10000.0
LANE = 128
VMEM_LIMIT = 56 * 1024 * 1024

ADAM_LR, ADAM_B1, ADAM_B2, ADAM_EPS, ADAM_WD, ADAM_STEP = 0.001, 0.9, 0.999, 1e-08, 0.01, 10

GROUPS = (("qa", WA, True, True), ("ka", WA, True, False), ("va", WA, False, False),
          ("qb", WB, True, True), ("kb", WKV, True, False), ("vb", WKV, False, False),
          ("qc", WC, False, True), ("kc", WC, False, False), ("vc", WC, False, False))


def _params(sem=None):
    return pltpu.CompilerParams(dimension_semantics=sem, vmem_limit_bytes=VMEM_LIMIT)


HBM_SPEC = pl.BlockSpec(memory_space=pltpu.HBM)
SEM_SPEC = pl.BlockSpec(memory_space=pltpu.SEMAPHORE)
DATAFLOW = pltpu.SideEffectType.DATAFLOW_SIDE_EFFECTING


ALL_PEERS = tuple((p >> 2 & 1, p >> 1 & 1, p & 1) for p in range(1, N_DEV))
OTHER_CHIPS = ((1, 0, 0), (0, 1, 0), (1, 1, 0))
NEAR_PEERS = ((0, 0, 1),) + OTHER_CHIPS


def _flip(x, y, c, f):
    return (1 - x if f[0] else x, 1 - y if f[1] else y, 1 - c if f[2] else c)


def _index(pos):
    return 4 * pos[0] + 2 * pos[1] + pos[2]


def _descriptors(plan, bufs, send_sems, recv_sems):
    x, y, c = lax.axis_index("x"), lax.axis_index("y"), lax.axis_index("c")
    return [pltpu.make_async_remote_copy(src_ref=src, dst_ref=dst, send_sem=send_sems[g].at[i],
                                         recv_sem=recv_sems[g].at[i], device_id=partner,
                                         device_id_type=pl.DeviceIdType.MESH)
            for g, copies in enumerate(plan(bufs, x, y, c)) for i, (src, dst, partner) in enumerate(copies)]


def _copy_start(name, bufs, plan, sizes):
    nb, ng = len(bufs), len(sizes)

    def body(*refs):
        for d in _descriptors(plan, refs[:nb], refs[nb:nb + ng], refs[nb + ng:nb + 2 * ng]):
            d.start()
        refs[2 * nb + 2 * ng][...] = jnp.zeros((8, LANE), F32)

    outs = pl.pallas_call(
        body, name=name,
        out_shape=[pltpu.SemaphoreType.DMA((s,)) for s in sizes] * 2 + [pltpu.HBM(b.shape, b.dtype) for b in bufs]
        + [jax.ShapeDtypeStruct((8, LANE), F32)],
        in_specs=[HBM_SPEC] * nb,
        out_specs=[SEM_SPEC] * (2 * ng) + [HBM_SPEC] * nb + [pl.BlockSpec(memory_space=pltpu.VMEM)],
        input_output_aliases={i: 2 * ng + i for i in range(nb)},
        compiler_params=pltpu.CompilerParams(has_side_effects=DATAFLOW),
    )(*[pltpu.with_memory_space_constraint(b, pltpu.HBM) for b in bufs])
    return outs[:ng], outs[ng:2 * ng], outs[2 * ng:2 * ng + nb], outs[2 * ng + nb]


def _copy_wait(name, bufs, send_sems, recv_sems, plan, after):
    nb, ng = len(bufs), len(send_sems)
    after = list(after) if isinstance(after, (list, tuple)) else [after]

    def body(*refs):
        for d in _descriptors(plan, refs[:nb], refs[nb:nb + ng], refs[nb + ng:nb + 2 * ng]):
            d.wait_send()
            d.wait_recv()

    return pl.pallas_call(
        body, name=name, out_shape=[pltpu.HBM(b.shape, b.dtype) for b in bufs],
        in_specs=[HBM_SPEC] * nb + [SEM_SPEC] * (2 * ng) + [pl.BlockSpec(memory_space=pl.ANY)] * len(after),
        out_specs=[HBM_SPEC] * nb, input_output_aliases={i: i for i in range(nb)},
        compiler_params=pltpu.CompilerParams(has_side_effects=DATAFLOW),
    )(*bufs, *send_sems, *recv_sems, *after)


def _gather_plan(peer_sets):
    def plan(bufs, x, y, c):
        n = len(peer_sets)
        return [[(bufs[i], bufs[n + i].at[_index((x, y, c))], _flip(x, y, c, f)) for f in peers]
                for i, peers in enumerate(peer_sets)]
    return plan


def _forward_plan(bufs, x, y, c):
    slots = [_index(_flip(x, y, c, f)) for f in OTHER_CHIPS]
    return [[(bufs[0].at[s], bufs[0].at[s], _flip(x, y, c, (0, 0, 1))) for s in slots]]


def _scatter_plan(bufs, x, y, c):
    peers = [_flip(x, y, c, f) for f in ALL_PEERS]
    return [[(bufs[0].at[_index(p)], bufs[1].at[_index((x, y, c))], p) for p in peers]]


def _flat2(v):
    return v.reshape(-1, v.shape[-1])


def _matmul(name, kind, a, a_spec, b, b_spec, out_shape, out_spec, grid, res=None, res_spec=None, acc_shape=None):
    dims = {"nn": (((1,), (0,)), ((), ())), "nt": NT_DIMS, "nts": NT_DIMS, "tn": (((0,), (0,)), ((), ()))}[kind]
    nred = grid[-1]

    def body(*refs):
        if res is None:
            a_ref, b_ref, o_ref = refs[:3]
            r_ref = None
        else:
            a_ref, b_ref, r_ref, o_ref = refs[:4]
        if kind == "nts":
            n = b_ref.shape[-1]
            part = sum(lax.dot_general(a_ref[:, blk * n:(blk + 1) * n], b_ref[blk], dims, preferred_element_type=F32)
                       for blk in range(b_ref.shape[0]))
        else:
            part = lax.dot_general(_flat2(a_ref[...]), _flat2(b_ref[...]), dims, preferred_element_type=F32)

        def finish(total):
            if r_ref is not None:
                total = total + r_ref[...]
            o_ref[...] = total.reshape(o_ref.shape).astype(o_ref.dtype)

        if nred == 1:
            finish(part)
        else:
            acc_ref = refs[-1]
            k = pl.program_id(len(grid) - 1)

            @pl.when(k == 0)
            def _():
                acc_ref[...] = part

            @pl.when(jnp.logical_and(k > 0, k < nred - 1))
            def _():
                acc_ref[...] += part

            @pl.when(k == nred - 1)
            def _():
                finish(acc_ref[...] + part)

    ins, specs = [a, b], [a_spec, b_spec]
    if res is not None:
        ins.append(res)
        specs.append(res_spec)
    scratch = [] if nred == 1 else [pltpu.VMEM(acc_shape, F32)]
    return pl.pallas_call(
        body, name=name, grid=grid, in_specs=specs, out_specs=out_spec, out_shape=out_shape, scratch_shapes=scratch,
        compiler_params=_params(("parallel",) * (len(grid) - 1) + ("arbitrary",)),
    )(*ins)


TM = 512


def _nn_cols(name, a, wg, out_dtype=F32):
    _, k, nj = wg.shape
    tm = 1024
    return _matmul(
        name, "nn", a, pl.BlockSpec((tm, k), lambda j, i, r: (i, 0)),
        wg, pl.BlockSpec((None, k, nj), lambda j, i, r: (j, 0, 0)),
        jax.ShapeDtypeStruct((T, N_DEV * nj), out_dtype), pl.BlockSpec((tm, nj), lambda j, i, r: (i, j)),
        (N_DEV, T // tm, 1))


def _nn_rows(name, a, wg, res, s, tn):
    _, kj, n = wg.shape
    tm = 1024
    return _matmul(
        name, "nn", a, pl.BlockSpec((tm, s * kj), lambda j, i, r: (i, r)),
        wg, pl.BlockSpec((s, kj, tn), lambda j, i, r: (r, 0, j)),
        jax.ShapeDtypeStruct((T, n), F32), pl.BlockSpec((tm, tn), lambda j, i, r: (i, j)),
        (n // tn, T // tm, N_DEV // s), res=res, res_spec=pl.BlockSpec((tm, tn), lambda j, i, r: (i, j)),
        acc_shape=(tm, tn))


def _nt_cols(name, dc, dc_spec_of, wg, s):
    _, k, nj = wg.shape
    tm = tk = 1024
    return _matmul(
        name, "nts", dc, dc_spec_of(tm, s * nj),
        wg, pl.BlockSpec((s, tk, nj), lambda kt, i, j: (j, kt, 0)),
        jax.ShapeDtypeStruct((T, k), F32), pl.BlockSpec((tm, tk), lambda kt, i, j: (i, kt)),
        (k // tk, T // tm, N_DEV // s), acc_shape=(tm, tk))


def _nt_rows(name, dc, wg, s):
    _, kj, n = wg.shape
    return _matmul(
        name, "nt", dc, pl.BlockSpec((TM, n), lambda kt, i, r: (i, 0)),
        wg, pl.BlockSpec((s, kj, n), lambda kt, i, r: (kt, 0, 0)),
        jax.ShapeDtypeStruct((T, N_DEV * kj), F32), pl.BlockSpec((TM, s * kj), lambda kt, i, r: (i, kt)),
        (N_DEV // s, T // TM, 1))


def _tn_cols(name, a, dc, dc_spec_of, nj):
    k = a.shape[1]
    tk = 512
    return _matmul(
        name, "tn", a, pl.BlockSpec((T, tk), lambda j, kt, r: (0, kt)),
        dc, dc_spec_of(T, nj),
        jax.ShapeDtypeStruct((N_DEV, k, nj), BF16), pl.BlockSpec((None, tk, nj), lambda j, kt, r: (j, kt, 0)),
        (N_DEV, k // tk, 1))


def _tn_rows(name, a, dc, kj, s):
    n = dc.shape[1]
    tn = 512
    return _matmul(
        name, "tn", a, pl.BlockSpec((T, s * kj), lambda kt, j, r: (0, kt)),
        dc, pl.BlockSpec((T, tn), lambda kt, j, r: (0, j)),
        jax.ShapeDtypeStruct((N_DEV, kj, n), BF16), pl.BlockSpec((s, kj, tn), lambda kt, j, r: (kt, 0, j)),
        (N_DEV // s, n // tn, 1))


TR = 256


def _rows(width):
    return pl.BlockSpec((TR, width), lambda i: (i, 0))


def _whole(shape):
    return pl.BlockSpec(shape, lambda i: (0,) * len(shape))


def _rmsnorm_fwd(name, x, g):
    def body(x_ref, g_ref, o_ref):
        xv = x_ref[...]
        r = lax.rsqrt(jnp.mean(xv * xv, axis=-1, keepdims=True) + EPS)
        o_ref[...] = ((xv * r) * g_ref[...]).astype(BF16)

    return pl.pallas_call(
        body, name=name, grid=(T // TR,), in_specs=[_rows(D), _whole((1, D))], out_specs=_rows(D),
        out_shape=jax.ShapeDtypeStruct((T, D), BF16), compiler_params=_params(("parallel",)),
    )(x, g)


def _rms_bwd_math(dy, xv, g):
    r = lax.rsqrt(jnp.mean(xv * xv, axis=-1, keepdims=True) + EPS)
    xhat = xv * r
    dxhat = dy * g
    dx = r * (dxhat - xhat * jnp.mean(dxhat * xhat, axis=-1, keepdims=True))
    return dx, dy * xhat


def _accumulate(ref, val):
    @pl.when(pl.program_id(0) == 0)
    def _():
        ref[...] = val

    @pl.when(pl.program_id(0) > 0)
    def _():
        ref[...] += val


def _rmsnorm_bwd(name, dy, x, g, res):
    def body(dy_ref, x_ref, g_ref, res_ref, dx_ref, dxb_ref, dg_ref):
        dx, dgr = _rms_bwd_math(dy_ref[...], x_ref[...], g_ref[...])
        tot = res_ref[...] + dx
        dx_ref[...] = tot
        dxb_ref[...] = tot.astype(BF16)
        _accumulate(dg_ref, jnp.sum(dgr, axis=0, keepdims=True))

    return pl.pallas_call(
        body, name=name, grid=(T // TR,), in_specs=[_rows(D), _rows(D), _whole((1, D)), _rows(D)],
        out_specs=[_rows(D), _rows(D), _whole((1, D))],
        out_shape=[jax.ShapeDtypeStruct((T, D), F32), jax.ShapeDtypeStruct((T, D), BF16),
                   jax.ShapeDtypeStruct((1, D), F32)],
        compiler_params=_params(("arbitrary",)),
    )(dy, x, g, res)


def _loss_head(x, g, target):
    def body(x_ref, g_ref, t_ref, loss_ref, dx_ref, dxb_ref, dg_ref):
        xv, gv = x_ref[...], g_ref[...]
        r = lax.rsqrt(jnp.mean(xv * xv, axis=-1, keepdims=True) + EPS)
        err = (xv * r) * gv - t_ref[...]
        part = 0.5 * jnp.sum(jnp.mean(err * err, axis=-1, keepdims=True))
        dx, dgr = _rms_bwd_math(err * (1.0 / D), xv, gv)
        dx_ref[...] = dx
        dxb_ref[...] = dx.astype(BF16)
        _accumulate(dg_ref, jnp.sum(dgr, axis=0, keepdims=True))
        _accumulate(loss_ref, jnp.full((8, LANE), part, F32))

    return pl.pallas_call(
        body, name="loss_head", grid=(T // TR,), in_specs=[_rows(D), _whole((1, D)), _rows(D)],
        out_specs=[_whole((8, LANE)), _rows(D), _rows(D), _whole((1, D))],
        out_shape=[jax.ShapeDtypeStruct((8, LANE), F32), jax.ShapeDtypeStruct((T, D), F32),
                   jax.ShapeDtypeStruct((T, D), BF16), jax.ShapeDtypeStruct((1, D), F32)],
        compiler_params=_params(("arbitrary",)),
    )(x, g, target)


MIX_OFFS = ((0, WA), (WA, WB), (WA + WB, WC))


def _mix_fwd(name, oa, ob, oc, gain):
    def body(oa_ref, ob_ref, oc_ref, g_ref, o_ref):
        for ref, (off, w) in zip((oa_ref, ob_ref, oc_ref), MIX_OFFS):
            o = ref[...]
            r = lax.rsqrt(jnp.mean(o * o, axis=-1, keepdims=True) + EPS)
            o_ref[:, off:off + w] = ((o * r) * g_ref[:, off:off + w]).astype(BF16)

    return pl.pallas_call(
        body, name=name, grid=(T // TR,), in_specs=[_rows(WA), _rows(WB), _rows(WC), _whole((1, D))],
        out_specs=_rows(D), out_shape=jax.ShapeDtypeStruct((T, D), BF16), compiler_params=_params(("parallel",)),
    )(oa, ob, oc, gain)


def _mix_bwd(name, dmixed, oa, ob, oc, gain):
    def body(dm_ref, oa_ref, ob_ref, oc_ref, g_ref, doa_ref, dob_ref, doc_ref, dg_ref):
        dgs = []
        for ref, dref, (off, w) in zip((oa_ref, ob_ref, oc_ref), (doa_ref, dob_ref, doc_ref), MIX_OFFS):
            dx, dgr = _rms_bwd_math(dm_ref[:, off:off + w], ref[...], g_ref[:, off:off + w])
            dref[...] = dx
            dgs.append(jnp.sum(dgr, axis=0, keepdims=True))
        _accumulate(dg_ref, jnp.concatenate(dgs, axis=1))

    return pl.pallas_call(
        body, name=name, grid=(T // TR,),
        in_specs=[_rows(D), _rows(WA), _rows(WB), _rows(WC), _whole((1, D))],
        out_specs=[_rows(WA), _rows(WB), _rows(WC), _whole((1, D))],
        out_shape=[jax.ShapeDtypeStruct((T, WA), F32), jax.ShapeDtypeStruct((T, WB), F32),
                   jax.ShapeDtypeStruct((T, WC), F32), jax.ShapeDtypeStruct((1, D), F32)],
        compiler_params=_params(("arbitrary",)),
    )(dmixed, oa, ob, oc, gain)


def _rope_tables():
    inv_freq = ROPE_THETA ** (-jnp.arange(0, HD, 2, dtype=F32) / HD)
    ang = jnp.arange(T, dtype=F32)[:, None] * inv_freq[None, :]
    cos, sin = jnp.cos(ang), jnp.sin(ang)
    cos2 = jnp.tile(jnp.concatenate([cos, cos], axis=1), (1, LANE // HD))
    sin2 = jnp.tile(jnp.concatenate([-sin, sin], axis=1), (1, LANE // HD))
    return cos2, sin2


def _rot_half(v):
    lane = lax.broadcasted_iota(jnp.int32, v.shape, 1)
    return jnp.where(lane % HD < HD // 2, pltpu.roll(v, LANE - HD // 2, 1), pltpu.roll(v, HD // 2, 1))


def _rope_fwd(name, proj, cos2, sin2):
    def body(p_ref, c_ref, s_ref, *outs):
        cv, sv = c_ref[...], s_ref[...]
        off = 0
        for o_ref, (_, w, rot, is_q) in zip(outs, GROUPS):
            for b in range(w // LANE):
                v = p_ref[:, off + b * LANE:off + (b + 1) * LANE]
                if rot:
                    v = v * cv + _rot_half(v) * sv
                if is_q:
                    v = v * (HD ** -0.5)
                o_ref[:, b * LANE:(b + 1) * LANE] = v.astype(BF16)
            off += w

    return pl.pallas_call(
        body, name=name, grid=(T // TR,), in_specs=[_rows(IN_COLS), _rows(LANE), _rows(LANE)],
        out_specs=[_rows(w) for _, w, _, _ in GROUPS],
        out_shape=[jax.ShapeDtypeStruct((T, w), BF16) for _, w, _, _ in GROUPS],
        compiler_params=_params(("parallel",)),
    )(proj, cos2, sin2)


def _rope_bwd(name, grads, cos2, sin2):
    def body(*refs):
        ins, (c_ref, s_ref, o_ref) = refs[:9], refs[9:]
        cv, sv = c_ref[...], s_ref[...]
        off = 0
        for d_ref, (_, w, rot, is_q) in zip(ins, GROUPS):
            for b in range(w // LANE):
                v = d_ref[:, b * LANE:(b + 1) * LANE]
                if is_q:
                    v = v * (HD ** -0.5)
                if rot:
                    v = v * cv + _rot_half(v * sv)
                o_ref[:, off + b * LANE:off + (b + 1) * LANE] = v.astype(BF16)
            off += w

    return pl.pallas_call(
        body, name=name, grid=(T // TR,), in_specs=[_rows(w) for _, w, _, _ in GROUPS] + [_rows(LANE), _rows(LANE)],
        out_specs=_rows(IN_COLS), out_shape=jax.ShapeDtypeStruct((T, IN_COLS), BF16),
        compiler_params=_params(("parallel",)),
    )(*grads, cos2, sin2)


NT_DIMS = (((1,), (1,)), ((), ()))
TN_DIMS = (((0,), (0,)), ((), ()))


def _scores(q, k, bias, valid):
    s = lax.dot_general(q, k, NT_DIMS, preferred_element_type=F32)
    if bias is not None:
        s = s + bias
    if valid is not None:
        s = jnp.where(valid, s, NEG)
    return s


def _heads_fwd(heads):
    scores = [_scores(h["q"], h["k"], h.get("bias"), h.get("valid")) for h in heads]
    soft = []
    for s, h in zip(scores, heads):
        m = jnp.max(s, axis=1, keepdims=True)
        e = jnp.exp(s - m)
        l = jnp.sum(e, axis=1, keepdims=True)
        if h.get("sink") is not None:
            l = l + jnp.exp(h["sink"] - m)
        soft.append((e.astype(BF16), l, m + jnp.log(l)))
    return [(jnp.dot(e, h["v"], preferred_element_type=F32) / l, lse) for (e, l, lse), h in zip(soft, heads)]


def _heads_bwd(heads):
    dobs = [h["do"].astype(BF16) for h in heads]
    scores = [_scores(h["q"], h["k"], h.get("bias"), h.get("valid")) for h in heads]
    dps = [lax.dot_general(dob, h["v"], NT_DIMS, preferred_element_type=F32) for dob, h in zip(dobs, heads)]
    mid = []
    for s, dp, h in zip(scores, dps, heads):
        p = jnp.exp(s - h["lse"])
        delta = jnp.sum(h["do"] * h["o"], axis=1, keepdims=True)
        ds = p * (dp - delta)
        dsink = None if h.get("sink") is None else -jnp.exp(h["sink"] - h["lse"]) * delta
        mid.append((p.astype(BF16), ds, dsink))
    out = []
    for (pb, ds, dsink), dob, h in zip(mid, dobs, heads):
        dsb = ds.astype(BF16)
        out.append((jnp.dot(dsb, h["k"], preferred_element_type=F32),
                    lax.dot_general(dsb, h["q"], TN_DIMS, preferred_element_type=F32),
                    lax.dot_general(pb, dob, TN_DIMS, preferred_element_type=F32), ds, dsink))
    return out


def _per_head(cols):
    return jnp.concatenate([jnp.broadcast_to(c, (c.shape[0], HD)) for c in cols], axis=1)


DILATIONS = ((128, 1), (512, 4), (2048, 16))


def _dilation_bias():
    def body(o_ref):
        t = pl.program_id(0) * TR + lax.broadcasted_iota(jnp.int32, (TR, T), 0)
        ad = jnp.abs(t - lax.broadcasted_iota(jnp.int32, (TR, T), 1))
        count = jnp.zeros((TR, T), jnp.int32)
        for window, r in DILATIONS:
            count += jnp.where(((ad & (r - 1)) == 0) & (ad <= window // 2), 1, 0)
        logs = jnp.where(count == 2, jnp.log(2.0), jnp.where(count == 3, jnp.log(3.0), 0.0)).astype(F32)
        o_ref[...] = jnp.where(count == 0, NEG, logs)

    return pl.pallas_call(
        body, name="dilation_bias", grid=(T // TR,), out_specs=_rows(T),
        out_shape=jax.ShapeDtypeStruct((T, T), F32), compiler_params=_params(("parallel",)),
    )()


BQ_A = 256


def _attn_a_fwd(name, qa, ka, va, bias):
    def body(q_ref, k_ref, v_ref, b_ref, o_ref, lse_ref):
        b = b_ref[...]
        outs = _heads_fwd([dict(q=q_ref[:, h * HD:(h + 1) * HD], k=k_ref[:, h * HD:(h + 1) * HD],
                                v=v_ref[:, h * HD:(h + 1) * HD], bias=b) for h in range(2)])
        o_ref[...] = jnp.concatenate([o for o, _ in outs], axis=1)
        lse_ref[...] = _per_head([lse for _, lse in outs])

    qs = pl.BlockSpec((BQ_A, LANE), lambda p, i: (i, p))
    ks = pl.BlockSpec((T, LANE), lambda p, i: (0, p))
    return pl.pallas_call(
        body, name=name, grid=(HA // 2, T // BQ_A),
        in_specs=[qs, ks, ks, pl.BlockSpec((BQ_A, T), lambda p, i: (i, 0))], out_specs=[qs, qs],
        out_shape=[jax.ShapeDtypeStruct((T, WA), F32)] * 2, compiler_params=_params(("parallel", "parallel")),
    )(qa, ka, va, bias)


def _attn_a_bwd(name, qa, ka, va, oa, lse, doa, bias):
    def body(q_ref, k_ref, v_ref, o_ref, lse_ref, do_ref, b_ref, dq_ref, dk_ref, dv_ref):
        b = b_ref[...]
        sls = [slice(h * HD, (h + 1) * HD) for h in range(2)]
        res = _heads_bwd([dict(q=q_ref[:, sl], k=k_ref[:, sl], v=v_ref[:, sl], o=o_ref[:, sl], do=do_ref[:, sl],
                               lse=lse_ref[:, sl.start:sl.start + 1], bias=b) for sl in sls])
        dq_ref[...] = jnp.concatenate([r[0] for r in res], axis=1)
        dk2, dv2 = jnp.concatenate([r[1] for r in res], axis=1), jnp.concatenate([r[2] for r in res], axis=1)

        @pl.when(pl.program_id(1) == 0)
        def _():
            dk_ref[...] = dk2
            dv_ref[...] = dv2

        @pl.when(pl.program_id(1) > 0)
        def _():
            dk_ref[...] += dk2
            dv_ref[...] += dv2

    qs = pl.BlockSpec((BQ_A, LANE), lambda p, i: (i, p))
    ks = pl.BlockSpec((T, LANE), lambda p, i: (0, p))
    return pl.pallas_call(
        body, name=name, grid=(HA // 2, T // BQ_A),
        in_specs=[qs, ks, ks, qs, qs, qs, pl.BlockSpec((BQ_A, T), lambda p, i: (i, 0))], out_specs=[qs, ks, ks],
        out_shape=[jax.ShapeDtypeStruct((T, WA), F32)] * 3, compiler_params=_params(("parallel", "arbitrary")),
    )(qa, ka, va, oa, lse, doa, bias)


BQ_B = 128
SPAN_B = BQ_B + 2 * WINDOW_B


def _window_b(i):
    start = pl.multiple_of(jnp.clip(i * BQ_B - WINDOW_B, 0, T - SPAN_B), BQ_B)
    qpos = i * BQ_B + lax.broadcasted_iota(jnp.int32, (BQ_B, SPAN_B), 0)
    kpos = start + lax.broadcasted_iota(jnp.int32, (BQ_B, SPAN_B), 1)
    return start, jnp.abs(qpos - kpos) <= WINDOW_B


GROUP_B = HB // HKV


def _stack_group(ref, g):
    return jnp.concatenate([ref[:, h * HD:(h + 1) * HD] for h in range(g * GROUP_B, (g + 1) * GROUP_B)], axis=0)


def _sink_column(sink_ref, g):
    return jnp.concatenate([jnp.full((BQ_B, 1), sink_ref[h], F32) for h in range(g * GROUP_B, (g + 1) * GROUP_B)],
                           axis=0)


def _unstack(stacked):
    return [s[j * BQ_B:(j + 1) * BQ_B] for s in stacked for j in range(GROUP_B)]


def _attn_b_fwd(name, qb, kb, vb, sink):
    def body(sink_ref, q_ref, k_ref, v_ref, o_ref, lse_ref):
        start, valid = _window_b(pl.program_id(0))
        valid = jnp.concatenate([valid] * GROUP_B, axis=0)
        kw, vw = k_ref[pl.ds(start, SPAN_B), :], v_ref[pl.ds(start, SPAN_B), :]
        outs = _heads_fwd([dict(q=_stack_group(q_ref, g), k=kw[:, g * HD:(g + 1) * HD], v=vw[:, g * HD:(g + 1) * HD],
                                valid=valid, sink=_sink_column(sink_ref, g)) for g in range(HKV)])
        o_ref[...] = jnp.concatenate(_unstack([o for o, _ in outs]), axis=1)
        lse_ref[...] = _per_head(_unstack([lse for _, lse in outs]))

    qs = pl.BlockSpec((BQ_B, WB), lambda i: (i, 0))
    return pl.pallas_call(
        body, name=name, grid=(T // BQ_B,),
        in_specs=[pl.BlockSpec(memory_space=pltpu.SMEM), qs, _whole((T, WKV)), _whole((T, WKV))],
        out_specs=[qs, qs],
        out_shape=[jax.ShapeDtypeStruct((T, WB), F32)] * 2, compiler_params=_params(("parallel",)),
    )(sink, qb, kb, vb)


def _attn_b_bwd(name, qb, kb, vb, ob, lse, dob, sink):
    def body(sink_ref, q_ref, k_ref, v_ref, o_ref, lse_ref, do_ref, dq_ref, dk_ref, dv_ref, dsink_ref):
        i = pl.program_id(0)
        start, valid = _window_b(i)
        valid = jnp.concatenate([valid] * GROUP_B, axis=0)
        kw, vw = k_ref[pl.ds(start, SPAN_B), :], v_ref[pl.ds(start, SPAN_B), :]
        res = _heads_bwd([dict(q=_stack_group(q_ref, g), k=kw[:, g * HD:(g + 1) * HD], v=vw[:, g * HD:(g + 1) * HD],
                               o=_stack_group(o_ref, g), do=_stack_group(do_ref, g),
                               lse=jnp.concatenate([lse_ref[:, h * HD:h * HD + 1]
                                                    for h in range(g * GROUP_B, (g + 1) * GROUP_B)], axis=0),
                               valid=valid, sink=_sink_column(sink_ref, g)) for g in range(HKV)])
        dks, dvs = [r[1] for r in res], [r[2] for r in res]
        lane = lax.broadcasted_iota(jnp.int32, (1, LANE), 1)
        dsink = jnp.zeros((1, LANE), F32)
        for h, rows in enumerate(_unstack([r[4] for r in res])):
            dsink += jnp.where(lane == h, jnp.sum(rows), 0.0)
        dq_ref[...] = jnp.concatenate(_unstack([r[0] for r in res]), axis=1)

        @pl.when(i == 0)
        def _():
            dk_ref[...] = jnp.zeros_like(dk_ref)
            dv_ref[...] = jnp.zeros_like(dv_ref)
            dsink_ref[...] = jnp.zeros_like(dsink_ref)

        dk_ref[pl.ds(start, SPAN_B), :] += jnp.concatenate(dks, axis=1)
        dv_ref[pl.ds(start, SPAN_B), :] += jnp.concatenate(dvs, axis=1)
        dsink_ref[...] += dsink

    qs = pl.BlockSpec((BQ_B, WB), lambda i: (i, 0))
    return pl.pallas_call(
        body, name=name, grid=(T // BQ_B,),
        in_specs=[pl.BlockSpec(memory_space=pltpu.SMEM), qs, _whole((T, WKV)), _whole((T, WKV)), qs, qs, qs],
        out_specs=[qs, _whole((T, WKV)), _whole((T, WKV)), _whole((1, LANE))],
        out_shape=[jax.ShapeDtypeStruct((T, WB), F32), jax.ShapeDtypeStruct((T, WKV), F32),
                   jax.ShapeDtypeStruct((T, WKV), F32), jax.ShapeDtypeStruct((1, LANE), F32)],
        compiler_params=_params(("arbitrary",)),
    )(sink, qb, kb, vb, ob, lse, dob)


SPAN_C = NA_ROWS * GRID_W


def _row_start(r):
    return jnp.clip(r - NA_ROWS // 2, 0, ROWS - NA_ROWS)


def _off_index(r):
    return _row_start(r) - r + (NA_ROWS - 1)


N_TAB = 16
RPS = 4


def _rpb_tables(name, rpb):
    circ = jnp.concatenate([rpb[..., NA_COLS - 1:], jnp.zeros(rpb.shape[:2] + (LANE - (2 * NA_COLS - 1),), F32),
                            rpb[..., :NA_COLS - 1]], axis=-1)
    circ = jnp.pad(circ, ((0, 0), (0, N_TAB + 1 - circ.shape[1]), (0, 0)))

    def body(w_ref, o_ref):
        c = lax.broadcasted_iota(jnp.int32, (GRID_W, LANE), 0)
        lane = lax.broadcasted_iota(jnp.int32, (GRID_W, LANE), 1)
        cs = jnp.clip(c - NA_COLS // 2, 0, GRID_W - NA_COLS)
        valid = (lane % GRID_W >= cs) & (lane % GRID_W < cs + NA_COLS)
        toep = [pltpu.roll(jnp.broadcast_to(w_ref[a:a + 1, :], (GRID_W, LANE)), 0, 1, stride=1, stride_axis=0)
                for a in range(N_TAB + 1)]
        for a in range(N_TAB):
            pair = jnp.where(lane < GRID_W, toep[a], pltpu.roll(toep[a + 1], GRID_W, 1))
            o_ref[a] = jnp.where(valid, pair, NEG)

    return pl.pallas_call(
        body, name=name, grid=(HC,),
        in_specs=[pl.BlockSpec((None, N_TAB + 1, LANE), lambda h: (h, 0, 0))],
        out_specs=pl.BlockSpec((None, N_TAB, GRID_W, LANE), lambda h: (h, 0, 0, 0)),
        out_shape=jax.ShapeDtypeStruct((HC, N_TAB, GRID_W, LANE), F32), compiler_params=_params(("parallel",)),
    )(circ)


def _bias_c(t_ref, h, d):
    return jnp.concatenate([t_ref[h, d + k] for k in range(0, NA_ROWS, 2)], axis=1)


def _attn_c_fwd(name, qc, kc, vc, tables):
    def body(q_ref, k_ref, v_ref, t_ref, o_ref, lse_ref):
        heads = []
        for rr in range(RPS):
            r = pl.program_id(1) * RPS + rr
            rows = slice(rr * GRID_W, (rr + 1) * GRID_W)
            start = pl.multiple_of(_row_start(r) * GRID_W, GRID_W)
            kw, vw = k_ref[pl.ds(start, SPAN_C), :], v_ref[pl.ds(start, SPAN_C), :]
            heads += [dict(q=q_ref[rows, h * HD:(h + 1) * HD], k=kw[:, h * HD:(h + 1) * HD], v=vw[:, h * HD:(h + 1) * HD],
                           bias=_bias_c(t_ref, h, _off_index(r))) for h in range(2)]
        outs = _heads_fwd(heads)
        for rr in range(RPS):
            rows = slice(rr * GRID_W, (rr + 1) * GRID_W)
            o_ref[rows, :] = jnp.concatenate([o for o, _ in outs[2 * rr:2 * rr + 2]], axis=1)
            lse_ref[rows, :] = _per_head([lse for _, lse in outs[2 * rr:2 * rr + 2]])

    qs = pl.BlockSpec((RPS * GRID_W, LANE), lambda p, r: (r, p))
    ks = pl.BlockSpec((T, LANE), lambda p, r: (0, p))
    ts = pl.BlockSpec((2, N_TAB, GRID_W, LANE), lambda p, r: (p, 0, 0, 0))
    return pl.pallas_call(
        body, name=name, grid=(HC // 2, ROWS // RPS), in_specs=[qs, ks, ks, ts], out_specs=[qs, qs],
        out_shape=[jax.ShapeDtypeStruct((T, WC), F32)] * 2, compiler_params=_params(("parallel", "parallel")),
    )(qc, kc, vc, tables)


def _attn_c_bwd(name, qc, kc, vc, oc, lse, doc, tables):
    def body(q_ref, k_ref, v_ref, o_ref, lse_ref, do_ref, t_ref, dq_ref, dk_ref, dv_ref, dt_ref):
        @pl.when(pl.program_id(1) == 0)
        def _():
            dk_ref[...] = jnp.zeros_like(dk_ref)
            dv_ref[...] = jnp.zeros_like(dv_ref)
            dt_ref[...] = jnp.zeros_like(dt_ref)

        heads, where = [], []
        for rr in range(RPS):
            r = pl.program_id(1) * RPS + rr
            rows = slice(rr * GRID_W, (rr + 1) * GRID_W)
            d = _off_index(r)
            start = pl.multiple_of(_row_start(r) * GRID_W, GRID_W)
            kw, vw = k_ref[pl.ds(start, SPAN_C), :], v_ref[pl.ds(start, SPAN_C), :]
            where.append((rows, d, start))
            for h in range(2):
                sl = slice(h * HD, (h + 1) * HD)
                heads.append(dict(q=q_ref[rows, sl], k=kw[:, sl], v=vw[:, sl], o=o_ref[rows, sl], do=do_ref[rows, sl],
                                  lse=lse_ref[rows, h * HD:h * HD + 1], bias=_bias_c(t_ref, h, d)))
        res = _heads_bwd(heads)
        for rr, (rows, d, start) in enumerate(where):
            pair = res[2 * rr:2 * rr + 2]
            for h in range(2):
                for k in range(0, NA_ROWS, 2):
                    dt_ref[h, d + k] += pair[h][3][:, k * GRID_W:(k + 2) * GRID_W]
            dq_ref[rows, :] = jnp.concatenate([p[0] for p in pair], axis=1)
            dk_ref[pl.ds(start, SPAN_C), :] += jnp.concatenate([p[1] for p in pair], axis=1)
            dv_ref[pl.ds(start, SPAN_C), :] += jnp.concatenate([p[2] for p in pair], axis=1)

    qs = pl.BlockSpec((RPS * GRID_W, LANE), lambda p, r: (r, p))
    ks = pl.BlockSpec((T, LANE), lambda p, r: (0, p))
    ts = pl.BlockSpec((2, N_TAB, GRID_W, LANE), lambda p, r: (p, 0, 0, 0))
    return pl.pallas_call(
        body, name=name, grid=(HC // 2, ROWS // RPS), in_specs=[qs, ks, ks, qs, qs, qs, ts],
        out_specs=[qs, ks, ks, ts],
        out_shape=[jax.ShapeDtypeStruct((T, WC), F32)] * 3 + [jax.ShapeDtypeStruct((HC, N_TAB, GRID_W, LANE), F32)],
        compiler_params=_params(("parallel", "arbitrary")),
    )(qc, kc, vc, oc, lse, doc, tables)


def _split3(v):
    hi = v.astype(BF16)
    r1 = v - hi.astype(F32)
    mid = r1.astype(BF16)
    lo = (r1 - mid.astype(F32)).astype(BF16)
    return hi, mid, lo


def _rpb_reduce(name, dtables):
    x = dtables.reshape(HC, N_TAB, GRID_W * LANE)
    c = jnp.arange(GRID_W)[:, None]
    lane = jnp.arange(LANE)[None, :]
    col = (lane // GRID_W) * LANE + jnp.clip(lane % GRID_W - c + (NA_COLS - 1), 0, 2 * NA_COLS - 2)
    col_onehot = (col.reshape(-1)[:, None] == jnp.arange(2 * LANE)[None, :]).astype(BF16)
    a2 = jnp.arange(N_TAB)[None, :]
    row_onehot = jnp.concatenate([(jnp.arange(16)[:, None] == a2 + u) & (a2 < 2 * NA_ROWS - 2) for u in range(2)],
                                 axis=1).astype(BF16)

    def body(x_ref, e_ref, f_ref, o_ref):
        y = sum(jnp.dot(part, e_ref[...], preferred_element_type=F32) for part in _split3(x_ref[...]))
        z = jnp.concatenate([y[:, :LANE], y[:, LANE:]], axis=0)
        o_ref[...] = sum(jnp.dot(f_ref[...], part, preferred_element_type=F32) for part in _split3(z))

    out = pl.pallas_call(
        body, name=name, grid=(HC,),
        in_specs=[pl.BlockSpec((None, N_TAB, GRID_W * LANE), lambda h: (h, 0, 0)),
                  _whole((GRID_W * LANE, 2 * LANE)), _whole((16, 2 * N_TAB))],
        out_specs=pl.BlockSpec((None, 16, LANE), lambda h: (h, 0, 0)),
        out_shape=jax.ShapeDtypeStruct((HC, 16, LANE), F32), compiler_params=_params(("parallel",)),
    )(x, col_onehot, row_onehot)
    return out[:, :2 * NA_ROWS - 1, :2 * NA_COLS - 1]


TC = 128
NCB = DFF // TC


def _shift_down(v, rows):
    return jnp.where(rows == 0, 0.0, pltpu.roll(v, 1, 0))


def _shift_up(v, rows):
    return jnp.where(rows == T - 1, 0.0, pltpu.roll(v, T - 1, 0))


def _conv(v, w, b, rows):
    return _shift_down(v, rows) * w[0:1] + v * w[1:2] + _shift_up(v, rows) * w[2:3] + b


def _ffn_specs():
    gate = lambda shape: pl.BlockSpec(shape, lambda j: (0, j))
    val = lambda shape: pl.BlockSpec(shape, lambda j: (0, j + NCB))
    return [gate((T, TC)), val((T, TC)), gate((3, TC)), val((3, TC)), gate((1, TC)), val((1, TC))]


def _ffn_mid_fwd(name, up, conv_w, conv_b):
    def body(xg_ref, xv_ref, wg_ref, wv_ref, bg_ref, bv_ref, o_ref):
        rows = lax.broadcasted_iota(jnp.int32, (T, TC), 0)
        ug = _conv(xg_ref[...], wg_ref[...], bg_ref[...], rows)
        uv = _conv(xv_ref[...], wv_ref[...], bv_ref[...], rows)
        o_ref[...] = (ug * jax.nn.sigmoid(ug) * uv).astype(BF16)

    return pl.pallas_call(
        body, name=name, grid=(NCB,), in_specs=_ffn_specs(), out_specs=pl.BlockSpec((T, TC), lambda j: (0, j)),
        out_shape=jax.ShapeDtypeStruct((T, DFF), BF16), compiler_params=_params(("parallel",)),
    )(up, up, conv_w, conv_w, conv_b, conv_b)


def _ffn_mid_bwd(name, dact, up, conv_w, conv_b):
    def body(da_ref, xg_ref, xv_ref, wg_ref, wv_ref, bg_ref, bv_ref, dx_ref, dw_ref, db_ref):
        rows = lax.broadcasted_iota(jnp.int32, (T, TC), 0)
        xg, xv, wg, wv = xg_ref[...], xv_ref[...], wg_ref[...], wv_ref[...]
        ug = _conv(xg, wg, bg_ref[...], rows)
        uv = _conv(xv, wv, bv_ref[...], rows)
        sg = jax.nn.sigmoid(ug)
        da = da_ref[...]
        dug = da * uv * (sg * (1.0 + ug * (1.0 - sg)))
        duv = da * (ug * sg)
        for half, (xin, w, du) in enumerate(((xg, wg, dug), (xv, wv, duv))):
            dx = _shift_up(du, rows) * w[0:1] + du * w[1:2] + _shift_down(du, rows) * w[2:3]
            dx_ref[half] = dx.astype(BF16)
            dw_ref[half] = jnp.concatenate(
                [jnp.sum(_shift_down(xin, rows) * du, axis=0, keepdims=True), jnp.sum(xin * du, axis=0, keepdims=True),
                 jnp.sum(_shift_up(xin, rows) * du, axis=0, keepdims=True)], axis=0)
            db_ref[half] = jnp.sum(du, axis=0, keepdims=True)

    return pl.pallas_call(
        body, name=name, grid=(NCB,), in_specs=[pl.BlockSpec((T, TC), lambda j: (0, j))] + _ffn_specs(),
        out_specs=[pl.BlockSpec((2, T, TC), lambda j: (0, 0, j)), pl.BlockSpec((2, 3, TC), lambda j: (0, 0, j)),
                   pl.BlockSpec((2, 1, TC), lambda j: (0, 0, j))],
        out_shape=[jax.ShapeDtypeStruct((2, T, DFF), BF16), jax.ShapeDtypeStruct((2, 3, DFF), F32),
                   jax.ShapeDtypeStruct((2, 1, DFF), F32)],
        compiler_params=_params(("parallel",)),
    )(dact, up, up, conv_w, conv_w, conv_b, conv_b)


def _dup_spec(tm, nj):
    per = DFF // nj
    return pl.BlockSpec((None, tm, nj), lambda a, b, j: (j // per, 0 if tm == T else b, j % per))


def _dup_spec_tn(tm, nj):
    per = DFF // nj
    return pl.BlockSpec((None, tm, nj), lambda j, kt, r: (j // per, 0, j % per))


def _adamw_math(w, g, m, v):
    m = ADAM_B1 * m + (1.0 - ADAM_B1) * g
    v = ADAM_B2 * v + (1.0 - ADAM_B2) * (g * g)
    m_hat = m / (1.0 - ADAM_B1 ** ADAM_STEP)
    v_hat = v / (1.0 - ADAM_B2 ** ADAM_STEP)
    delta = -ADAM_LR * (m_hat / (jnp.sqrt(v_hat) + ADAM_EPS) + ADAM_WD * w)
    return delta, m, v


ADAM_BLOCK = 256 * 1408


def _adamw_sharded(name, w, m, v, parts):
    _, r, c = w.shape
    tr = max(t for t in range(16, r + 1, 16) if r % t == 0 and t * c <= ADAM_BLOCK)

    def body(w_ref, m_ref, v_ref, p0_ref, p1_ref, g_ref, d_ref, nm_ref, nv_ref):
        def run(p_ref):
            g = p_ref[0].astype(F32)
            for k in range(1, N_DEV):
                g = g + p_ref[k].astype(F32)
            d, nm, nv = _adamw_math(w_ref[...], g, m_ref[...], v_ref[...])
            g_ref[...] = g
            d_ref[...] = d
            nm_ref[...] = nm
            nv_ref[...] = nv

        @pl.when(pl.program_id(0) == 0)
        def _():
            run(p0_ref)

        @pl.when(pl.program_id(0) == 1)
        def _():
            run(p1_ref)

    ws = pl.BlockSpec((None, tr, c), lambda l, i: (l, i, 0))
    p0 = pl.BlockSpec((N_DEV, tr, c), lambda l, i: (0, jnp.where(l == 0, i, r // tr - 1), 0))
    p1 = pl.BlockSpec((N_DEV, tr, c), lambda l, i: (0, jnp.where(l == 1, i, 0), 0))
    return pl.pallas_call(
        body, name=name, grid=(DEPTH, r // tr), in_specs=[ws, ws, ws, p0, p1], out_specs=[ws] * 4,
        out_shape=[jax.ShapeDtypeStruct(w.shape, F32)] * 4, compiler_params=_params(("arbitrary", "arbitrary")),
    )(w, m, v, *parts)


def _sum_devices(name, parts):
    r = parts.shape[1]

    def body(p_ref, o_ref):
        g = p_ref[0]
        for k in range(1, N_DEV):
            g = g + p_ref[k]
        o_ref[...] = g

    return pl.pallas_call(
        body, name=name, in_specs=[pl.BlockSpec((N_DEV, r, LANE), lambda: (0, 0, 0))],
        out_specs=pl.BlockSpec((r, LANE), lambda: (0, 0)), out_shape=jax.ShapeDtypeStruct((r, LANE), F32),
        compiler_params=_params(),
    )(parts)


def _adamw_small(name, w, g, m, v):
    spec = pl.BlockSpec(w.shape, lambda: (0, 0))

    def body(w_ref, g_ref, m_ref, v_ref, d_ref, nm_ref, nv_ref):
        d, nm, nv = _adamw_math(w_ref[...], g_ref[...], m_ref[...], v_ref[...])
        d_ref[...] = d
        nm_ref[...] = nm
        nv_ref[...] = nv

    return pl.pallas_call(
        body, name=name, in_specs=[spec] * 4, out_specs=[spec] * 3,
        out_shape=[jax.ShapeDtypeStruct(w.shape, F32)] * 3, compiler_params=_params(),
    )(w, g, m, v)


def _pack(arrays):
    flat = jnp.concatenate([a.reshape(-1) for a in arrays])
    pad = (-flat.shape[0]) % (8 * LANE)
    return jnp.pad(flat, (0, pad)).reshape(-1, LANE)


def _unpack(buf, shapes):
    flat, out, off = buf.reshape(-1), [], 0
    for s in shapes:
        n = 1
        for d in s:
            n *= d
        out.append(flat[off:off + n].reshape(s))
        off += n
    return out


def _local_step(x, target, small, weights, conv_w_full, hand_over, used):
    cos2, sin2 = _rope_tables()
    bias_a = _dilation_bias()
    tables = [_rpb_tables(f"rpb_tables_{l}", small["rpb_c"][l]) for l in range(DEPTH)]
    saved, carry = [], 0.0
    for l in range(DEPTH):
        g1, g2 = small["ln_attn"][l][None] + carry, small["ln_ffn"][l][None]
        gain, sink, cb = small["mix_gain"][l][None], small["sink_b"][l], small["conv_b"][l][None]
        cw = conv_w_full[l]
        bias = tables[l]
        h1 = _rmsnorm_fwd(f"norm_attn_{l}", x, g1)
        proj = _nn_cols(f"proj_in_{l}", h1, weights("w_in", l, [h1, cos2, sin2, bias_a] + tables if l == 0 else h1))
        zero = used(proj)
        qa, ka, va, qb, kb, vb, qc, kc, vc = _rope_fwd(f"rope_{l}", proj, cos2, sin2)
        oa, lse_a = _attn_a_fwd(f"attn_a_{l}", qa, ka, va, bias_a)
        ob, lse_b = _attn_b_fwd(f"attn_b_{l}", qb, kb, vb, sink + zero)
        oc, lse_c = _attn_c_fwd(f"attn_c_{l}", qc, kc, vc, bias)
        mixed = _mix_fwd(f"mix_{l}", oa, ob, oc, gain)
        x_mid = _nn_rows(f"proj_out_{l}", mixed, weights("w_out", l, mixed), x, 8, 512)
        h2 = _rmsnorm_fwd(f"norm_ffn_{l}", x_mid, g2 + used(x_mid))
        up = _nn_cols(f"ffn_up_{l}", h2, weights("w_up", l, h2))
        act = _ffn_mid_fwd(f"ffn_mid_{l}", up, cw, cb + used(up))
        x_out = _nn_rows(f"ffn_down_{l}", act, weights("w_down", l, act), x_mid, 4, 1024)
        carry = used(x_out)
        saved.append(dict(x=x, h1=h1, qkv=(qa, ka, va, qb, kb, vb, qc, kc, vc), o=(oa, ob, oc), lse=(lse_a, lse_b, lse_c), mixed=mixed,
                          x_mid=x_mid, h2=h2, up=up, act=act, g1=g1, g2=g2, gain=gain, sink=sink, cb=cb, cw=cw, bias=bias))
        x = x_out

    loss8, dx, dxb, d_ln_final = _loss_head(x, small["ln_final"][None], target)
    sgrads = [None] * DEPTH
    for l in reversed(range(DEPTH)):
        s = saved[l]
        qa, ka, va, qb, kb, vb, qc, kc, vc = s["qkv"]
        oa, ob, oc = s["o"]
        wg_in, wg_out = weights("w_in", l, None), weights("w_out", l, None)
        wg_up, wg_down = weights("w_up", l, None), weights("w_down", l, None)
        g_down = _tn_rows(f"wgrad_down_{l}", s["act"], dxb, wg_down.shape[1], 2)
        zero = hand_over("w_down", l, g_down)
        dact = _nt_rows(f"dgrad_down_{l}", dxb, wg_down, 2)
        dup, d_cw, d_cb = _ffn_mid_bwd(f"ffn_mid_bwd_{l}", dact, s["up"], s["cw"], s["cb"] + zero)
        g_up = _tn_cols(f"wgrad_up_{l}", s["h2"], dup, _dup_spec_tn, wg_up.shape[2])
        zero = hand_over("w_up", l, g_up)
        dh2 = _nt_cols(f"dgrad_up_{l}", dup, _dup_spec, wg_up, 2)
        dx, dxb, d_g2 = _rmsnorm_bwd(f"norm_ffn_bwd_{l}", dh2, s["x_mid"], s["g2"] + zero, dx)
        g_out = _tn_rows(f"wgrad_out_{l}", s["mixed"], dxb, wg_out.shape[1], 2)
        zero = hand_over("w_out", l, g_out)
        dmixed = _nt_rows(f"dgrad_out_{l}", dxb, wg_out, 2)
        doa, dob, doc, d_gain = _mix_bwd(f"mix_bwd_{l}", dmixed, oa, ob, oc, s["gain"] + zero)
        lse_a, lse_b, lse_c = s["lse"]
        dqa, dka, dva = _attn_a_bwd(f"attn_a_bwd_{l}", qa, ka, va, oa, lse_a, doa, bias_a)
        dqb, dkb, dvb, d_sink = _attn_b_bwd(f"attn_b_bwd_{l}", qb, kb, vb, ob, lse_b, dob, s["sink"])
        dqc, dkc, dvc, d_bias = _attn_c_bwd(f"attn_c_bwd_{l}", qc, kc, vc, oc, lse_c, doc, s["bias"])
        d_rpb = _rpb_reduce(f"rpb_reduce_{l}", d_bias)
        dproj = _rope_bwd(f"rope_bwd_{l}", (dqa, dka, dva, dqb, dkb, dvb, dqc, dkc, dvc), cos2, sin2)
        g_in = _tn_cols(f"wgrad_in_{l}", s["h1"], dproj,
                        lambda tm, nj: pl.BlockSpec((tm, nj), lambda j, kt, r: (0, j)), wg_in.shape[2])
        zero = hand_over("w_in", l, g_in)
        dh1 = _nt_cols(f"dgrad_in_{l}", dproj, lambda tm, nj: pl.BlockSpec((tm, nj), lambda kt, i, j: (i, j)), wg_in, 4)
        dx, dxb, d_g1 = _rmsnorm_bwd(f"norm_attn_bwd_{l}", dh1, s["x"], s["g1"] + zero, dx)
        sgrads[l] = dict(ln_attn=d_g1[0], sink_b=d_sink[0, :HB], rpb_c=d_rpb, mix_gain=d_gain[0], ln_ffn=d_g2[0],
                         conv_w=d_cw.transpose(1, 0, 2).reshape(3, 2 * DFF), conv_b=d_cb.reshape(2 * DFF))
    return loss8[0, 0], dx, d_ln_final[0], sgrads


SMALL_NAMES = ("ln_attn", "sink_b", "rpb_c", "mix_gain", "ln_ffn", "conv_b")


def kernel(x, ln_attn, w_in, sink_b, rpb_c, mix_gain, w_out, ln_ffn, w_up, conv_w, conv_b, w_down, ln_final, loss_target, m_ln_attn, m_w_in, m_sink_b, m_rpb_c, m_mix_gain, m_w_out, m_ln_ffn, m_w_up, m_conv_w, m_conv_b, m_w_down, m_ln_final, v_ln_attn, v_w_in, v_sink_b, v_rpb_c, v_mix_gain, v_w_out, v_ln_ffn, v_w_up, v_conv_w, v_conv_b, v_w_down, v_ln_final):
    me = 4 * lax.axis_index("x") + 2 * lax.axis_index("y") + lax.axis_index("c")
    small = dict(ln_attn=ln_attn, sink_b=sink_b, rpb_c=rpb_c, mix_gain=mix_gain, ln_ffn=ln_ffn, conv_b=conv_b,
                 ln_final=ln_final)

    names = ("w_in", "w_out", "w_up", "w_down")
    shards = dict(w_in=w_in, w_out=w_out, w_up=w_up, w_down=w_down)
    order = [(n, l) for l in range(DEPTH) for n in names]
    conv_key = ("conv_w", 0)
    started, arrived, forwarded, gathered = {}, {}, {}, {}

    def begin(name, ks, zero):
        srcs = [_pack([conv_w]) + zero if k == conv_key else (shards[k[0]][k[1]] + zero).astype(BF16) for k in ks]
        peers = [ALL_PEERS if k == conv_key else NEAR_PEERS for k in ks]
        send, recv, bufs, tok = _copy_start(name, srcs + [lax.empty((N_DEV,) + s.shape, s.dtype) for s in srcs],
                                            _gather_plan(peers), [len(p) for p in peers])
        for i, k in enumerate(ks):
            started[k] = (send[i], recv[i], bufs[i], bufs[len(ks) + i], peers[i])
        return tok

    token = begin("gather_start_first", order[:1], 0.0)
    token = begin("gather_start_rest", [conv_key] + order[1:], token[0, 0])

    def arrive(k, after):
        send, recv, src, land, peers = started[k]
        arrived[k] = _copy_wait(f"gather_{k[0]}_{k[1]}_arrived", [src, land], [send], [recv], _gather_plan([peers]),
                                after)

    queue = list(order)

    def advance(after):
        if not queue:
            return 0.0
        k = queue.pop(0)
        arrive(k, after)
        forwarded[k] = _copy_start(f"gather_{k[0]}_{k[1]}_forward", [arrived[k][1]], _forward_plan, [len(OTHER_CHIPS)])
        return forwarded[k][3][0, 0]

    def complete(k, land):
        return lax.dynamic_update_slice_in_dim(land, arrived[k][0][None], me, axis=0)

    def weights(n, l, after):
        k = (n, l)
        if k not in gathered:
            if k not in forwarded:
                advance(after)
            send_b, recv_b, (land,), _ = forwarded[k]
            (land,) = _copy_wait(f"gather_{n}_{l}_done", [land], send_b, recv_b, _forward_plan, after)
            gathered[k] = complete(k, land)
        return gathered[k]

    pending = {}

    def hand_over(n, l, g):
        send, recv, bufs, tok = _copy_start(f"send_grad_{n}_{l}", [g, lax.empty(g.shape, g.dtype)], _scatter_plan,
                                            [len(ALL_PEERS)])
        pending[(n, l)] = (send, recv, bufs)
        return tok[0, 0]

    def received(k, after):
        send, recv, bufs = pending[k]
        src, land = _copy_wait(f"recv_grad_{k[0]}_{k[1]}", bufs, send, recv, _scatter_plan, after)
        return lax.dynamic_update_slice_in_dim(land, lax.dynamic_slice_in_dim(src, me, 1, axis=0), me, axis=0)

    arrive(conv_key, token)
    cw_all = complete(conv_key, arrived[conv_key][1])
    nup = w_up.shape[2]
    cw_shards = cw_all.reshape(N_DEV, -1)[:, :DEPTH * 3 * nup].reshape(N_DEV, DEPTH, 3, nup)
    conv_w_full = cw_shards.transpose(1, 2, 0, 3).reshape(DEPTH, 3, N_DEV * nup)

    loss_local, dx, d_ln_final, sgrads = _local_step(
        x[0], loss_target[0], dict(small, ln_attn=ln_attn + token[0, 0]), weights, conv_w_full, hand_over, advance)

    stacked = [jnp.stack([sgrads[l][n] for l in range(DEPTH)]) for n in SMALL_NAMES + ("conv_w",)] + [d_ln_final]
    shapes = [a.shape for a in stacked]
    mine = _pack(stacked)
    send_s, recv_s, bufs_s, _ = _copy_start("gather_small_grads_start", [mine, lax.empty((N_DEV,) + mine.shape, F32)],
                                            _gather_plan([ALL_PEERS]), [len(ALL_PEERS)])

    big, after = {}, dx
    moments = dict(w_in=(m_w_in, v_w_in), w_out=(m_w_out, v_w_out), w_up=(m_w_up, v_w_up), w_down=(m_w_down, v_w_down))
    for n in reversed(names):
        parts = (received((n, 0), after), received((n, 1), after))
        big[n] = _adamw_sharded(f"adamw_{n}", shards[n], *moments[n], parts)
        after = big[n][1]

    mine, land = _copy_wait("gather_small_grads_done", bufs_s, send_s, recv_s, _gather_plan([ALL_PEERS]), after)
    everyone = lax.dynamic_update_slice_in_dim(land, mine[None], me, axis=0)
    g_small = _unpack(_sum_devices("sum_small_grads", everyone), shapes)
    g = dict(zip(SMALL_NAMES + ("conv_w", "ln_final"), g_small))
    g["conv_w"] = lax.dynamic_slice_in_dim(g["conv_w"], me * nup, nup, axis=2)

    snames = SMALL_NAMES + ("conv_w", "ln_final")
    sw = dict(small, conv_w=conv_w)
    sm = dict(ln_attn=m_ln_attn, sink_b=m_sink_b, rpb_c=m_rpb_c, mix_gain=m_mix_gain, ln_ffn=m_ln_ffn,
              conv_b=m_conv_b, conv_w=m_conv_w, ln_final=m_ln_final)
    sv = dict(ln_attn=v_ln_attn, sink_b=v_sink_b, rpb_c=v_rpb_c, mix_gain=v_mix_gain, ln_ffn=v_ln_ffn,
              conv_b=v_conv_b, conv_w=v_conv_w, ln_final=v_ln_final)
    sshapes = [sw[n].shape for n in snames]
    packed = _adamw_small("adamw_small", _pack([sw[n] for n in snames]), _pack([g[n] for n in snames]),
                          _pack([sm[n] for n in snames]), _pack([sv[n] for n in snames]))
    s_delta, s_m, s_v = (dict(zip(snames, _unpack(buf, sshapes))) for buf in packed)

    loss = lax.psum(loss_local, ("x", "y", "c"))
    outputs = ("ln_attn", "w_in", "sink_b", "rpb_c", "mix_gain", "w_out", "ln_ffn", "w_up", "conv_w", "conv_b",
               "w_down", "ln_final")
    grads = [big[n][0] if n in big else g[n] for n in outputs]
    deltas = [big[n][1] if n in big else s_delta[n] for n in outputs]
    new_m = [big[n][2] if n in big else s_m[n] for n in outputs]
    new_v = [big[n][3] if n in big else s_v[n] for n in outputs]
    return (loss, dx[None], *grads, *deltas, *new_m, *new_v)
```

```python
import functools

import jax
import jax.numpy as jnp
from jax import lax
from jax.experimental import pallas as pl
from jax.experimental.pallas import tpu as pltpu

F32 = jnp.float32
BF16 = jnp.bfloat16

N_DEV = 8
T = 2048
D = 2048
DEPTH = 2
HD = 64
HA, HB, HKV, HC = 12, 10, 2, 10
WA, WB, WKV, WC = HA * HD, HB * HD, HKV * HD, HC * HD
IN_COLS = 3 * WA + WB + 2 * WKV + 3 * WC
DFF = 5632
GRID_W = 64
ROWS = T // GRID_W
NA_ROWS, NA_COLS = 8, 16
WINDOW_B = 128
EPS = 1e-6
NEG = -1e30
ROPE_THETA = 10000.0
LANE = 128
VMEM_LIMIT = 56 * 1024 * 1024

ADAM_LR, ADAM_B1, ADAM_B2, ADAM_EPS, ADAM_WD, ADAM_STEP = 0.001, 0.9, 0.999, 1e-08, 0.01, 10

GROUPS = (("qa", WA, True, True), ("ka", WA, True, False), ("va", WA, False, False),
          ("qb", WB, True, True), ("kb", WKV, True, False), ("vb", WKV, False, False),
          ("qc", WC, False, True), ("kc", WC, False, False), ("vc", WC, False, False))


def _params(sem=None):
    return pltpu.CompilerParams(dimension_semantics=sem, vmem_limit_bytes=VMEM_LIMIT)


HBM_SPEC = pl.BlockSpec(memory_space=pltpu.HBM)
SEM_SPEC = pl.BlockSpec(memory_space=pltpu.SEMAPHORE)
DATAFLOW = pltpu.SideEffectType.DATAFLOW_SIDE_EFFECTING


ALL_PEERS = tuple((p >> 2 & 1, p >> 1 & 1, p & 1) for p in range(1, N_DEV))
OTHER_CHIPS = ((1, 0, 0), (0, 1, 0), (1, 1, 0))
NEAR_PEERS = ((0, 0, 1),) + OTHER_CHIPS


def _flip(x, y, c, f):
    return (1 - x if f[0] else x, 1 - y if f[1] else y, 1 - c if f[2] else c)


def _index(pos):
    return 4 * pos[0] + 2 * pos[1] + pos[2]


def _descriptors(plan, bufs, send_sems, recv_sems):
    x, y, c = lax.axis_index("x"), lax.axis_index("y"), lax.axis_index("c")
    return [pltpu.make_async_remote_copy(src_ref=src, dst_ref=dst, send_sem=send_sems[g].at[i],
                                         recv_sem=recv_sems[g].at[i], device_id=partner,
                                         device_id_type=pl.DeviceIdType.MESH)
            for g, copies in enumerate(plan(bufs, x, y, c)) for i, (src, dst, partner) in enumerate(copies)]


def _copy_start(name, bufs, plan, sizes):
    nb, ng = len(bufs), len(sizes)

    def body(*refs):
        for d in _descriptors(plan, refs[:nb], refs[nb:nb + ng], refs[nb + ng:nb + 2 * ng]):
            d.start()
        refs[2 * nb + 2 * ng][...] = jnp.zeros((8, LANE), F32)

    outs = pl.pallas_call(
        body, name=name,
        out_shape=[pltpu.SemaphoreType.DMA((s,)) for s in sizes] * 2 + [pltpu.HBM(b.shape, b.dtype) for b in bufs]
        + [jax.ShapeDtypeStruct((8, LANE), F32)],
        in_specs=[HBM_SPEC] * nb,
        out_specs=[SEM_SPEC] * (2 * ng) + [HBM_SPEC] * nb + [pl.BlockSpec(memory_space=pltpu.VMEM)],
        input_output_aliases={i: 2 * ng + i for i in range(nb)},
        compiler_params=pltpu.CompilerParams(has_side_effects=DATAFLOW),
    )(*[pltpu.with_memory_space_constraint(b, pltpu.HBM) for b in bufs])
    return outs[:ng], outs[ng:2 * ng], outs[2 * ng:2 * ng + nb], outs[2 * ng + nb]


def _copy_wait(name, bufs, send_sems, recv_sems, plan, after):
    nb, ng = len(bufs), len(send_sems)
    after = list(after) if isinstance(after, (list, tuple)) else [after]

    def body(*refs):
        for d in _descriptors(plan, refs[:nb], refs[nb:nb + ng], refs[nb + ng:nb + 2 * ng]):
            d.wait_send()
            d.wait_recv()

    return pl.pallas_call(
        body, name=name, out_shape=[pltpu.HBM(b.shape, b.dtype) for b in bufs],
        in_specs=[HBM_SPEC] * nb + [SEM_SPEC] * (2 * ng) + [pl.BlockSpec(memory_space=pl.ANY)] * len(after),
        out_specs=[HBM_SPEC] * nb, input_output_aliases={i: i for i in range(nb)},
        compiler_params=pltpu.CompilerParams(has_side_effects=DATAFLOW),
    )(*bufs, *send_sems, *recv_sems, *after)


def _lead_slot(ref, k):
    return ref.at[k]


def _col_slot(width):
    return lambda ref, k: ref.at[:, pl.ds(pl.multiple_of(k * width, LANE), width)]


def _gather_plan(peer_sets, slots):
    def plan(bufs, x, y, c):
        n = len(peer_sets)
        return [[(bufs[i], slots[i](bufs[n + i], _index((x, y, c))), _flip(x, y, c, f)) for f in peers]
                for i, peers in enumerate(peer_sets)]
    return plan


def _forward_plan(slot):
    def plan(bufs, x, y, c):
        pieces = [slot(bufs[0], _index(_flip(x, y, c, f))) for f in OTHER_CHIPS]
        return [[(p, p, _flip(x, y, c, (0, 0, 1))) for p in pieces]]
    return plan


def _scatter_plan(slot):
    def plan(bufs, x, y, c):
        peers = [_flip(x, y, c, f) for f in ALL_PEERS]
        return [[(slot(bufs[0], _index(p)), bufs[1].at[_index((x, y, c))], p) for p in peers]]
    return plan


def _flat2(v):
    return v.reshape(-1, v.shape[-1])


def _matmul(name, kind, a, a_spec, b, b_spec, out_shape, out_spec, grid, res=None, res_spec=None, acc_shape=None):
    dims = {"nn": (((1,), (0,)), ((), ())), "nt": NT_DIMS, "nts": NT_DIMS, "tn": (((0,), (0,)), ((), ()))}[kind]
    nred = grid[-1]

    def body(*refs):
        if res is None:
            a_ref, b_ref, o_ref = refs[:3]
            r_ref = None
        else:
            a_ref, b_ref, r_ref, o_ref = refs[:4]
        if kind == "nts":
            n = b_ref.shape[-1]
            part = sum(lax.dot_general(a_ref[:, blk * n:(blk + 1) * n], b_ref[blk], dims, preferred_element_type=F32)
                       for blk in range(b_ref.shape[0]))
        else:
            part = lax.dot_general(_flat2(a_ref[...]), _flat2(b_ref[...]), dims, preferred_element_type=F32)

        def finish(total):
            if r_ref is not None:
                total = total + r_ref[...]
            o_ref[...] = total.reshape(o_ref.shape).astype(o_ref.dtype)

        if nred == 1:
            finish(part)
        else:
            acc_ref = refs[-1]
            k = pl.program_id(len(grid) - 1)

            @pl.when(k == 0)
            def _():
                acc_ref[...] = part

            @pl.when(jnp.logical_and(k > 0, k < nred - 1))
            def _():
                acc_ref[...] += part

            @pl.when(k == nred - 1)
            def _():
                finish(acc_ref[...] + part)

    ins, specs = [a, b], [a_spec, b_spec]
    if res is not None:
        ins.append(res)
        specs.append(res_spec)
    scratch = [] if nred == 1 else [pltpu.VMEM(acc_shape, F32)]
    return pl.pallas_call(
        body, name=name, grid=grid, in_specs=specs, out_specs=out_spec, out_shape=out_shape, scratch_shapes=scratch,
        compiler_params=_params(("parallel",) * (len(grid) - 1) + ("arbitrary",)),
    )(*ins)


TM = 512


def _nn_cols(name, a, w, tn):
    k, n = w.shape
    tm = 1024
    return _matmul(
        name, "nn", a, pl.BlockSpec((tm, k), lambda j, i, r: (i, 0)),
        w, pl.BlockSpec((k, tn), lambda j, i, r: (0, j)),
        jax.ShapeDtypeStruct((T, n), F32), pl.BlockSpec((tm, tn), lambda j, i, r: (i, j)),
        (n // tn, T // tm, 1))


def _nn_rows(name, a, wg, res, s, tn):
    _, kj, n = wg.shape
    tm = 1024
    return _matmul(
        name, "nn", a, pl.BlockSpec((tm, s * kj), lambda j, i, r: (i, r)),
        wg, pl.BlockSpec((s, kj, tn), lambda j, i, r: (r, 0, j)),
        jax.ShapeDtypeStruct((T, n), F32), pl.BlockSpec((tm, tn), lambda j, i, r: (i, j)),
        (n // tn, T // tm, N_DEV // s), res=res, res_spec=pl.BlockSpec((tm, tn), lambda j, i, r: (i, j)),
        acc_shape=(tm, tn))


def _nt_cols(name, dc, dc_spec_of, w, nc):
    k, n = w.shape
    tm = tk = 1024
    return _matmul(
        name, "nt", dc, dc_spec_of(tm, nc),
        w, pl.BlockSpec((tk, nc), lambda kt, i, j: (kt, j)),
        jax.ShapeDtypeStruct((T, k), F32), pl.BlockSpec((tm, tk), lambda kt, i, j: (i, kt)),
        (k // tk, T // tm, n // nc), acc_shape=(tm, tk))


def _nt_rows(name, dc, wg, s):
    _, kj, n = wg.shape
    return _matmul(
        name, "nt", dc, pl.BlockSpec((TM, n), lambda kt, i, r: (i, 0)),
        wg, pl.BlockSpec((s, kj, n), lambda kt, i, r: (kt, 0, 0)),
        jax.ShapeDtypeStruct((T, N_DEV * kj), F32), pl.BlockSpec((TM, s * kj), lambda kt, i, r: (i, kt)),
        (N_DEV // s, T // TM, 1))


def _tn_cols(name, a, dc, dc_spec_of, n, tn):
    k = a.shape[1]
    tk = 512
    return _matmul(
        name, "tn", a, pl.BlockSpec((T, tk), lambda j, kt, r: (0, kt)),
        dc, dc_spec_of(T, tn),
        jax.ShapeDtypeStruct((k, n), BF16), pl.BlockSpec((tk, tn), lambda j, kt, r: (kt, j)),
        (n // tn, k // tk, 1))


def _tn_rows(name, a, dc, kj, s):
    n = dc.shape[1]
    tn = 512
    return _matmul(
        name, "tn", a, pl.BlockSpec((T, s * kj), lambda kt, j, r: (0, kt)),
        dc, pl.BlockSpec((T, tn), lambda kt, j, r: (0, j)),
        jax.ShapeDtypeStruct((N_DEV, kj, n), BF16), pl.BlockSpec((s, kj, tn), lambda kt, j, r: (kt, 0, j)),
        (N_DEV // s, n // tn, 1))


TR = 256


def _rows(width):
    return pl.BlockSpec((TR, width), lambda i: (i, 0))


def _whole(shape):
    return pl.BlockSpec(shape, lambda i: (0,) * len(shape))


def _rmsnorm_fwd(name, x, g):
    def body(x_ref, g_ref, o_ref):
        xv = x_ref[...]
        r = lax.rsqrt(jnp.mean(xv * xv, axis=-1, keepdims=True) + EPS)
        o_ref[...] = ((xv * r) * g_ref[...]).astype(BF16)

    return pl.pallas_call(
        body, name=name, grid=(T // TR,), in_specs=[_rows(D), _whole((1, D))], out_specs=_rows(D),
        out_shape=jax.ShapeDtypeStruct((T, D), BF16), compiler_params=_params(("parallel",)),
    )(x, g)


def _rms_bwd_math(dy, xv, g):
    r = lax.rsqrt(jnp.mean(xv * xv, axis=-1, keepdims=True) + EPS)
    xhat = xv * r
    dxhat = dy * g
    dx = r * (dxhat - xhat * jnp.mean(dxhat * xhat, axis=-1, keepdims=True))
    return dx, dy * xhat


def _accumulate(ref, val):
    @pl.when(pl.program_id(0) == 0)
    def _():
        ref[...] = val

    @pl.when(pl.program_id(0) > 0)
    def _():
        ref[...] += val


def _rmsnorm_bwd(name, dy, x, g, res):
    def body(dy_ref, x_ref, g_ref, res_ref, dx_ref, dxb_ref, dg_ref):
        dx, dgr = _rms_bwd_math(dy_ref[...], x_ref[...], g_ref[...])
        tot = res_ref[...] + dx
        dx_ref[...] = tot
        dxb_ref[...] = tot.astype(BF16)
        _accumulate(dg_ref, jnp.sum(dgr, axis=0, keepdims=True))

    return pl.pallas_call(
        body, name=name, grid=(T // TR,), in_specs=[_rows(D), _rows(D), _whole((1, D)), _rows(D)],
        out_specs=[_rows(D), _rows(D), _whole((1, D))],
        out_shape=[jax.ShapeDtypeStruct((T, D), F32), jax.ShapeDtypeStruct((T, D), BF16),
                   jax.ShapeDtypeStruct((1, D), F32)],
        compiler_params=_params(("arbitrary",)),
    )(dy, x, g, res)


def _loss_head(x, g, target):
    def body(x_ref, g_ref, t_ref, loss_ref, dx_ref, dxb_ref, dg_ref):
        xv, gv = x_ref[...], g_ref[...]
        r = lax.rsqrt(jnp.mean(xv * xv, axis=-1, keepdims=True) + EPS)
        err = (xv * r) * gv - t_ref[...]
        part = 0.5 * jnp.sum(jnp.mean(err * err, axis=-1, keepdims=True))
        dx, dgr = _rms_bwd_math(err * (1.0 / D), xv, gv)
        dx_ref[...] = dx
        dxb_ref[...] = dx.astype(BF16)
        _accumulate(dg_ref, jnp.sum(dgr, axis=0, keepdims=True))
        _accumulate(loss_ref, jnp.full((8, LANE), part, F32))

    return pl.pallas_call(
        body, name="loss_head", grid=(T // TR,), in_specs=[_rows(D), _whole((1, D)), _rows(D)],
        out_specs=[_whole((8, LANE)), _rows(D), _rows(D), _whole((1, D))],
        out_shape=[jax.ShapeDtypeStruct((8, LANE), F32), jax.ShapeDtypeStruct((T, D), F32),
                   jax.ShapeDtypeStruct((T, D), BF16), jax.ShapeDtypeStruct((1, D), F32)],
        compiler_params=_params(("arbitrary",)),
    )(x, g, target)


MIX_OFFS = ((0, WA), (WA, WB), (WA + WB, WC))


def _mix_fwd(name, oa, ob, oc, gain):
    def body(oa_ref, ob_ref, oc_ref, g_ref, o_ref):
        for ref, (off, w) in zip((oa_ref, ob_ref, oc_ref), MIX_OFFS):
            o = ref[...]
            r = lax.rsqrt(jnp.mean(o * o, axis=-1, keepdims=True) + EPS)
            o_ref[:, off:off + w] = ((o * r) * g_ref[:, off:off + w]).astype(BF16)

    return pl.pallas_call(
        body, name=name, grid=(T // TR,), in_specs=[_rows(WA), _rows(WB), _rows(WC), _whole((1, D))],
        out_specs=_rows(D), out_shape=jax.ShapeDtypeStruct((T, D), BF16), compiler_params=_params(("parallel",)),
    )(oa, ob, oc, gain)


def _mix_bwd(name, dmixed, oa, ob, oc, gain):
    def body(dm_ref, oa_ref, ob_ref, oc_ref, g_ref, doa_ref, dob_ref, doc_ref, dg_ref):
        dgs = []
        for ref, dref, (off, w) in zip((oa_ref, ob_ref, oc_ref), (doa_ref, dob_ref, doc_ref), MIX_OFFS):
            dx, dgr = _rms_bwd_math(dm_ref[:, off:off + w], ref[...], g_ref[:, off:off + w])
            dref[...] = dx
            dgs.append(jnp.sum(dgr, axis=0, keepdims=True))
        _accumulate(dg_ref, jnp.concatenate(dgs, axis=1))

    return pl.pallas_call(
        body, name=name, grid=(T // TR,),
        in_specs=[_rows(D), _rows(WA), _rows(WB), _rows(WC), _whole((1, D))],
        out_specs=[_rows(WA), _rows(WB), _rows(WC), _whole((1, D))],
        out_shape=[jax.ShapeDtypeStruct((T, WA), F32), jax.ShapeDtypeStruct((T, WB), F32),
                   jax.ShapeDtypeStruct((T, WC), F32), jax.ShapeDtypeStruct((1, D), F32)],
        compiler_params=_params(("arbitrary",)),
    )(dmixed, oa, ob, oc, gain)


def _rope_tables():
    inv_freq = ROPE_THETA ** (-jnp.arange(0, HD, 2, dtype=F32) / HD)
    ang = jnp.arange(T, dtype=F32)[:, None] * inv_freq[None, :]
    cos, sin = jnp.cos(ang), jnp.sin(ang)
    cos2 = jnp.tile(jnp.concatenate([cos, cos], axis=1), (1, LANE // HD))
    sin2 = jnp.tile(jnp.concatenate([-sin, sin], axis=1), (1, LANE // HD))
    return cos2, sin2


def _rot_half(v):
    lane = lax.broadcasted_iota(jnp.int32, v.shape, 1)
    return jnp.where(lane % HD < HD // 2, pltpu.roll(v, LANE - HD // 2, 1), pltpu.roll(v, HD // 2, 1))


def _rope_fwd(name, proj, cos2, sin2):
    def body(p_ref, c_ref, s_ref, *outs):
        cv, sv = c_ref[...], s_ref[...]
        off = 0
        for o_ref, (_, w, rot, is_q) in zip(outs, GROUPS):
            for b in range(w // LANE):
                v = p_ref[:, off + b * LANE:off + (b + 1) * LANE]
                if rot:
                    v = v * cv + _rot_half(v) * sv
                if is_q:
                    v = v * (HD ** -0.5)
                o_ref[:, b * LANE:(b + 1) * LANE] = v.astype(BF16)
            off += w

    return pl.pallas_call(
        body, name=name, grid=(T // TR,), in_specs=[_rows(IN_COLS), _rows(LANE), _rows(LANE)],
        out_specs=[_rows(w) for _, w, _, _ in GROUPS],
        out_shape=[jax.ShapeDtypeStruct((T, w), BF16) for _, w, _, _ in GROUPS],
        compiler_params=_params(("parallel",)),
    )(proj, cos2, sin2)


def _rope_bwd(name, grads, cos2, sin2):
    def body(*refs):
        ins, (c_ref, s_ref, o_ref) = refs[:9], refs[9:]
        cv, sv = c_ref[...], s_ref[...]
        off = 0
        for d_ref, (_, w, rot, is_q) in zip(ins, GROUPS):
            for b in range(w // LANE):
                v = d_ref[:, b * LANE:(b + 1) * LANE]
                if is_q:
                    v = v * (HD ** -0.5)
                if rot:
                    v = v * cv + _rot_half(v * sv)
                o_ref[:, off + b * LANE:off + (b + 1) * LANE] = v.astype(BF16)
            off += w

    return pl.pallas_call(
        body, name=name, grid=(T // TR,), in_specs=[_rows(w) for _, w, _, _ in GROUPS] + [_rows(LANE), _rows(LANE)],
        out_specs=_rows(IN_COLS), out_shape=jax.ShapeDtypeStruct((T, IN_COLS), BF16),
        compiler_params=_params(("parallel",)),
    )(*grads, cos2, sin2)


NT_DIMS = (((1,), (1,)), ((), ()))
TN_DIMS = (((0,), (0,)), ((), ()))


def _scores(q, k, bias, valid):
    s = lax.dot_general(q, k, NT_DIMS, preferred_element_type=F32)
    if bias is not None:
        s = s + bias
    if valid is not None:
        s = jnp.where(valid, s, NEG)
    return s


def _heads_fwd(heads):
    scores = [_scores(h["q"], h["k"], h.get("bias"), h.get("valid")) for h in heads]
    soft = []
    for s, h in zip(scores, heads):
        m = jnp.max(s, axis=1, keepdims=True)
        e = jnp.exp(s - m)
        l = jnp.sum(e, axis=1, keepdims=True)
        if h.get("sink") is not None:
            l = l + jnp.exp(h["sink"] - m)
        soft.append((e.astype(BF16), l, m + jnp.log(l)))
    return [(jnp.dot(e, h["v"], preferred_element_type=F32) / l, lse) for (e, l, lse), h in zip(soft, heads)]


def _heads_bwd(heads):
    dobs = [h["do"].astype(BF16) for h in heads]
    scores = [_scores(h["q"], h["k"], h.get("bias"), h.get("valid")) for h in heads]
    dps = [lax.dot_general(dob, h["v"], NT_DIMS, preferred_element_type=F32) for dob, h in zip(dobs, heads)]
    mid = []
    for s, dp, h in zip(scores, dps, heads):
        p = jnp.exp(s - h["lse"])
        delta = jnp.sum(h["do"] * h["o"], axis=1, keepdims=True)
        ds = p * (dp - delta)
        dsink = None if h.get("sink") is None else -jnp.exp(h["sink"] - h["lse"]) * delta
        mid.append((p.astype(BF16), ds, dsink))
    out = []
    for (pb, ds, dsink), dob, h in zip(mid, dobs, heads):
        dsb = ds.astype(BF16)
        out.append((jnp.dot(dsb, h["k"], preferred_element_type=F32),
                    lax.dot_general(dsb, h["q"], TN_DIMS, preferred_element_type=F32),
                    lax.dot_general(pb, dob, TN_DIMS, preferred_element_type=F32), ds, dsink))
    return out


def _per_head(cols):
    return jnp.concatenate([jnp.broadcast_to(c, (c.shape[0], HD)) for c in cols], axis=1)


DILATIONS = ((128, 1), (512, 4), (2048, 16))


def _dilation_bias():
    def body(o_ref):
        t = pl.program_id(0) * TR + lax.broadcasted_iota(jnp.int32, (TR, T), 0)
        ad = jnp.abs(t - lax.broadcasted_iota(jnp.int32, (TR, T), 1))
        count = jnp.zeros((TR, T), jnp.int32)
        for window, r in DILATIONS:
            count += jnp.where(((ad & (r - 1)) == 0) & (ad <= window // 2), 1, 0)
        logs = jnp.where(count == 2, jnp.log(2.0), jnp.where(count == 3, jnp.log(3.0), 0.0)).astype(F32)
        o_ref[...] = jnp.where(count == 0, NEG, logs)

    return pl.pallas_call(
        body, name="dilation_bias", grid=(T // TR,), out_specs=_rows(T),
        out_shape=jax.ShapeDtypeStruct((T, T), F32), compiler_params=_params(("parallel",)),
    )()


BQ_A = 256


def _attn_a_fwd(name, qa, ka, va, bias):
    def body(q_ref, k_ref, v_ref, b_ref, o_ref, lse_ref):
        b = b_ref[...]
        outs = _heads_fwd([dict(q=q_ref[:, h * HD:(h + 1) * HD], k=k_ref[:, h * HD:(h + 1) * HD],
                                v=v_ref[:, h * HD:(h + 1) * HD], bias=b) for h in range(2)])
        o_ref[...] = jnp.concatenate([o for o, _ in outs], axis=1)
        lse_ref[...] = _per_head([lse for _, lse in outs])

    qs = pl.BlockSpec((BQ_A, LANE), lambda p, i: (i, p))
    ks = pl.BlockSpec((T, LANE), lambda p, i: (0, p))
    return pl.pallas_call(
        body, name=name, grid=(HA // 2, T // BQ_A),
        in_specs=[qs, ks, ks, pl.BlockSpec((BQ_A, T), lambda p, i: (i, 0))], out_specs=[qs, qs],
        out_shape=[jax.ShapeDtypeStruct((T, WA), F32)] * 2, compiler_params=_params(("parallel", "parallel")),
    )(qa, ka, va, bias)


def _attn_a_bwd(name, qa, ka, va, oa, lse, doa, bias):
    def body(q_ref, k_ref, v_ref, o_ref, lse_ref, do_ref, b_ref, dq_ref, dk_ref, dv_ref):
        b = b_ref[...]
        sls = [slice(h * HD, (h + 1) * HD) for h in range(2)]
        res = _heads_bwd([dict(q=q_ref[:, sl], k=k_ref[:, sl], v=v_ref[:, sl], o=o_ref[:, sl], do=do_ref[:, sl],
                               lse=lse_ref[:, sl.start:sl.start + 1], bias=b) for sl in sls])
        dq_ref[...] = jnp.concatenate([r[0] for r in res], axis=1)
        dk2, dv2 = jnp.concatenate([r[1] for r in res], axis=1), jnp.concatenate([r[2] for r in res], axis=1)

        @pl.when(pl.program_id(1) == 0)
        def _():
            dk_ref[...] = dk2
            dv_ref[...] = dv2

        @pl.when(pl.program_id(1) > 0)
        def _():
            dk_ref[...] += dk2
            dv_ref[...] += dv2

    qs = pl.BlockSpec((BQ_A, LANE), lambda p, i: (i, p))
    ks = pl.BlockSpec((T, LANE), lambda p, i: (0, p))
    return pl.pallas_call(
        body, name=name, grid=(HA // 2, T // BQ_A),
        in_specs=[qs, ks, ks, qs, qs, qs, pl.BlockSpec((BQ_A, T), lambda p, i: (i, 0))], out_specs=[qs, ks, ks],
        out_shape=[jax.ShapeDtypeStruct((T, WA), F32)] * 3, compiler_params=_params(("parallel", "arbitrary")),
    )(qa, ka, va, oa, lse, doa, bias)


BQ_B = 128
SPAN_B = BQ_B + 2 * WINDOW_B


def _window_b(i):
    start = pl.multiple_of(jnp.clip(i * BQ_B - WINDOW_B, 0, T - SPAN_B), BQ_B)
    qpos = i * BQ_B + lax.broadcasted_iota(jnp.int32, (BQ_B, SPAN_B), 0)
    kpos = start + lax.broadcasted_iota(jnp.int32, (BQ_B, SPAN_B), 1)
    return start, jnp.abs(qpos - kpos) <= WINDOW_B


GROUP_B = HB // HKV


def _stack_group(ref, g):
    return jnp.concatenate([ref[:, h * HD:(h + 1) * HD] for h in range(g * GROUP_B, (g + 1) * GROUP_B)], axis=0)


def _sink_column(sink_ref, g):
    return jnp.concatenate([jnp.full((BQ_B, 1), sink_ref[h], F32) for h in range(g * GROUP_B, (g + 1) * GROUP_B)],
                           axis=0)


def _unstack(stacked):
    return [s[j * BQ_B:(j + 1) * BQ_B] for s in stacked for j in range(GROUP_B)]


def _attn_b_fwd(name, qb, kb, vb, sink):
    def body(sink_ref, q_ref, k_ref, v_ref, o_ref, lse_ref):
        start, valid = _window_b(pl.program_id(0))
        valid = jnp.concatenate([valid] * GROUP_B, axis=0)
        kw, vw = k_ref[pl.ds(start, SPAN_B), :], v_ref[pl.ds(start, SPAN_B), :]
        outs = _heads_fwd([dict(q=_stack_group(q_ref, g), k=kw[:, g * HD:(g + 1) * HD], v=vw[:, g * HD:(g + 1) * HD],
                                valid=valid, sink=_sink_column(sink_ref, g)) for g in range(HKV)])
        o_ref[...] = jnp.concatenate(_unstack([o for o, _ in outs]), axis=1)
        lse_ref[...] = _per_head(_unstack([lse for _, lse in outs]))

    qs = pl.BlockSpec((BQ_B, WB), lambda i: (i, 0))
    return pl.pallas_call(
        body, name=name, grid=(T // BQ_B,),
        in_specs=[pl.BlockSpec(memory_space=pltpu.SMEM), qs, _whole((T, WKV)), _whole((T, WKV))],
        out_specs=[qs, qs],
        out_shape=[jax.ShapeDtypeStruct((T, WB), F32)] * 2, compiler_params=_params(("parallel",)),
    )(sink, qb, kb, vb)


def _attn_b_bwd(name, qb, kb, vb, ob, lse, dob, sink):
    def body(sink_ref, q_ref, k_ref, v_ref, o_ref, lse_ref, do_ref, dq_ref, dk_ref, dv_ref, dsink_ref):
        i = pl.program_id(0)
        start, valid = _window_b(i)
        valid = jnp.concatenate([valid] * GROUP_B, axis=0)
        kw, vw = k_ref[pl.ds(start, SPAN_B), :], v_ref[pl.ds(start, SPAN_B), :]
        res = _heads_bwd([dict(q=_stack_group(q_ref, g), k=kw[:, g * HD:(g + 1) * HD], v=vw[:, g * HD:(g + 1) * HD],
                               o=_stack_group(o_ref, g), do=_stack_group(do_ref, g),
                               lse=jnp.concatenate([lse_ref[:, h * HD:h * HD + 1]
                                                    for h in range(g * GROUP_B, (g + 1) * GROUP_B)], axis=0),
                               valid=valid, sink=_sink_column(sink_ref, g)) for g in range(HKV)])
        dks, dvs = [r[1] for r in res], [r[2] for r in res]
        lane = lax.broadcasted_iota(jnp.int32, (1, LANE), 1)
        dsink = jnp.zeros((1, LANE), F32)
        for h, rows in enumerate(_unstack([r[4] for r in res])):
            dsink += jnp.where(lane == h, jnp.sum(rows), 0.0)
        dq_ref[...] = jnp.concatenate(_unstack([r[0] for r in res]), axis=1)

        @pl.when(i == 0)
        def _():
            dk_ref[...] = jnp.zeros_like(dk_ref)
            dv_ref[...] = jnp.zeros_like(dv_ref)
            dsink_ref[...] = jnp.zeros_like(dsink_ref)

        dk_ref[pl.ds(start, SPAN_B), :] += jnp.concatenate(dks, axis=1)
        dv_ref[pl.ds(start, SPAN_B), :] += jnp.concatenate(dvs, axis=1)
        dsink_ref[...] += dsink

    qs = pl.BlockSpec((BQ_B, WB), lambda i: (i, 0))
    return pl.pallas_call(
        body, name=name, grid=(T // BQ_B,),
        in_specs=[pl.BlockSpec(memory_space=pltpu.SMEM), qs, _whole((T, WKV)), _whole((T, WKV)), qs, qs, qs],
        out_specs=[qs, _whole((T, WKV)), _whole((T, WKV)), _whole((1, LANE))],
        out_shape=[jax.ShapeDtypeStruct((T, WB), F32), jax.ShapeDtypeStruct((T, WKV), F32),
                   jax.ShapeDtypeStruct((T, WKV), F32), jax.ShapeDtypeStruct((1, LANE), F32)],
        compiler_params=_params(("arbitrary",)),
    )(sink, qb, kb, vb, ob, lse, dob)


SPAN_C = NA_ROWS * GRID_W


def _row_start(r):
    return jnp.clip(r - NA_ROWS // 2, 0, ROWS - NA_ROWS)


def _off_index(r):
    return _row_start(r) - r + (NA_ROWS - 1)


N_TAB = 16
RPS = 4


def _rpb_tables(name, rpb):
    circ = jnp.concatenate([rpb[..., NA_COLS - 1:], jnp.zeros(rpb.shape[:2] + (LANE - (2 * NA_COLS - 1),), F32),
                            rpb[..., :NA_COLS - 1]], axis=-1)
    circ = jnp.pad(circ, ((0, 0), (0, N_TAB + 1 - circ.shape[1]), (0, 0)))

    def body(w_ref, o_ref):
        c = lax.broadcasted_iota(jnp.int32, (GRID_W, LANE), 0)
        lane = lax.broadcasted_iota(jnp.int32, (GRID_W, LANE), 1)
        cs = jnp.clip(c - NA_COLS // 2, 0, GRID_W - NA_COLS)
        valid = (lane % GRID_W >= cs) & (lane % GRID_W < cs + NA_COLS)
        toep = [pltpu.roll(jnp.broadcast_to(w_ref[a:a + 1, :], (GRID_W, LANE)), 0, 1, stride=1, stride_axis=0)
                for a in range(N_TAB + 1)]
        for a in range(N_TAB):
            pair = jnp.where(lane < GRID_W, toep[a], pltpu.roll(toep[a + 1], GRID_W, 1))
            o_ref[a] = jnp.where(valid, pair, NEG)

    return pl.pallas_call(
        body, name=name, grid=(HC,),
        in_specs=[pl.BlockSpec((None, N_TAB + 1, LANE), lambda h: (h, 0, 0))],
        out_specs=pl.BlockSpec((None, N_TAB, GRID_W, LANE), lambda h: (h, 0, 0, 0)),
        out_shape=jax.ShapeDtypeStruct((HC, N_TAB, GRID_W, LANE), F32), compiler_params=_params(("parallel",)),
    )(circ)


def _bias_c(t_ref, h, d):
    return jnp.concatenate([t_ref[h, d + k] for k in range(0, NA_ROWS, 2)], axis=1)


def _attn_c_fwd(name, qc, kc, vc, tables):
    def body(q_ref, k_ref, v_ref, t_ref, o_ref, lse_ref):
        heads = []
        for rr in range(RPS):
            r = pl.program_id(1) * RPS + rr
            rows = slice(rr * GRID_W, (rr + 1) * GRID_W)
            start = pl.multiple_of(_row_start(r) * GRID_W, GRID_W)
            kw, vw = k_ref[pl.ds(start, SPAN_C), :], v_ref[pl.ds(start, SPAN_C), :]
            heads += [dict(q=q_ref[rows, h * HD:(h + 1) * HD], k=kw[:, h * HD:(h + 1) * HD], v=vw[:, h * HD:(h + 1) * HD],
                           bias=_bias_c(t_ref, h, _off_index(r))) for h in range(2)]
        outs = _heads_fwd(heads)
        for rr in range(RPS):
            rows = slice(rr * GRID_W, (rr + 1) * GRID_W)
            o_ref[rows, :] = jnp.concatenate([o for o, _ in outs[2 * rr:2 * rr + 2]], axis=1)
            lse_ref[rows, :] = _per_head([lse for _, lse in outs[2 * rr:2 * rr + 2]])

    qs = pl.BlockSpec((RPS * GRID_W, LANE), lambda p, r: (r, p))
    ks = pl.BlockSpec((T, LANE), lambda p, r: (0, p))
    ts = pl.BlockSpec((2, N_TAB, GRID_W, LANE), lambda p, r: (p, 0, 0, 0))
    return pl.pallas_call(
        body, name=name, grid=(HC // 2, ROWS // RPS), in_specs=[qs, ks, ks, ts], out_specs=[qs, qs],
        out_shape=[jax.ShapeDtypeStruct((T, WC), F32)] * 2, compiler_params=_params(("parallel", "parallel")),
    )(qc, kc, vc, tables)


def _attn_c_bwd(name, qc, kc, vc, oc, lse, doc, tables):
    def body(q_ref, k_ref, v_ref, o_ref, lse_ref, do_ref, t_ref, dq_ref, dk_ref, dv_ref, dt_ref):
        @pl.when(pl.program_id(1) == 0)
        def _():
            dk_ref[...] = jnp.zeros_like(dk_ref)
            dv_ref[...] = jnp.zeros_like(dv_ref)
            dt_ref[...] = jnp.zeros_like(dt_ref)

        heads, where = [], []
        for rr in range(RPS):
            r = pl.program_id(1) * RPS + rr
            rows = slice(rr * GRID_W, (rr + 1) * GRID_W)
            d = _off_index(r)
            start = pl.multiple_of(_row_start(r) * GRID_W, GRID_W)
            kw, vw = k_ref[pl.ds(start, SPAN_C), :], v_ref[pl.ds(start, SPAN_C), :]
            where.append((rows, d, start))
            for h in range(2):
                sl = slice(h * HD, (h + 1) * HD)
                heads.append(dict(q=q_ref[rows, sl], k=kw[:, sl], v=vw[:, sl], o=o_ref[rows, sl], do=do_ref[rows, sl],
                                  lse=lse_ref[rows, h * HD:h * HD + 1], bias=_bias_c(t_ref, h, d)))
        res = _heads_bwd(heads)
        for rr, (rows, d, start) in enumerate(where):
            pair = res[2 * rr:2 * rr + 2]
            for h in range(2):
                for k in range(0, NA_ROWS, 2):
                    dt_ref[h, d + k] += pair[h][3][:, k * GRID_W:(k + 2) * GRID_W]
            dq_ref[rows, :] = jnp.concatenate([p[0] for p in pair], axis=1)
            dk_ref[pl.ds(start, SPAN_C), :] += jnp.concatenate([p[1] for p in pair], axis=1)
            dv_ref[pl.ds(start, SPAN_C), :] += jnp.concatenate([p[2] for p in pair], axis=1)

    qs = pl.BlockSpec((RPS * GRID_W, LANE), lambda p, r: (r, p))
    ks = pl.BlockSpec((T, LANE), lambda p, r: (0, p))
    ts = pl.BlockSpec((2, N_TAB, GRID_W, LANE), lambda p, r: (p, 0, 0, 0))
    return pl.pallas_call(
        body, name=name, grid=(HC // 2, ROWS // RPS), in_specs=[qs, ks, ks, qs, qs, qs, ts],
        out_specs=[qs, ks, ks, ts],
        out_shape=[jax.ShapeDtypeStruct((T, WC), F32)] * 3 + [jax.ShapeDtypeStruct((HC, N_TAB, GRID_W, LANE), F32)],
        compiler_params=_params(("parallel", "arbitrary")),
    )(qc, kc, vc, oc, lse, doc, tables)


def _split3(v):
    hi = v.astype(BF16)
    r1 = v - hi.astype(F32)
    mid = r1.astype(BF16)
    lo = (r1 - mid.astype(F32)).astype(BF16)
    return hi, mid, lo


def _rpb_reduce(name, dtables):
    x = dtables.reshape(HC, N_TAB, GRID_W * LANE)
    c = jnp.arange(GRID_W)[:, None]
    lane = jnp.arange(LANE)[None, :]
    col = (lane // GRID_W) * LANE + jnp.clip(lane % GRID_W - c + (NA_COLS - 1), 0, 2 * NA_COLS - 2)
    col_onehot = (col.reshape(-1)[:, None] == jnp.arange(2 * LANE)[None, :]).astype(BF16)
    a2 = jnp.arange(N_TAB)[None, :]
    row_onehot = jnp.concatenate([(jnp.arange(16)[:, None] == a2 + u) & (a2 < 2 * NA_ROWS - 2) for u in range(2)],
                                 axis=1).astype(BF16)

    def body(x_ref, e_ref, f_ref, o_ref):
        y = sum(jnp.dot(part, e_ref[...], preferred_element_type=F32) for part in _split3(x_ref[...]))
        z = jnp.concatenate([y[:, :LANE], y[:, LANE:]], axis=0)
        o_ref[...] = sum(jnp.dot(f_ref[...], part, preferred_element_type=F32) for part in _split3(z))

    out = pl.pallas_call(
        body, name=name, grid=(HC,),
        in_specs=[pl.BlockSpec((None, N_TAB, GRID_W * LANE), lambda h: (h, 0, 0)),
                  _whole((GRID_W * LANE, 2 * LANE)), _whole((16, 2 * N_TAB))],
        out_specs=pl.BlockSpec((None, 16, LANE), lambda h: (h, 0, 0)),
        out_shape=jax.ShapeDtypeStruct((HC, 16, LANE), F32), compiler_params=_params(("parallel",)),
    )(x, col_onehot, row_onehot)
    return out[:, :2 * NA_ROWS - 1, :2 * NA_COLS - 1]


TC = 128
NCB = DFF // TC


def _shift_down(v, rows):
    return jnp.where(rows == 0, 0.0, pltpu.roll(v, 1, 0))


def _shift_up(v, rows):
    return jnp.where(rows == T - 1, 0.0, pltpu.roll(v, T - 1, 0))


def _conv(v, w, b, rows):
    return _shift_down(v, rows) * w[0:1] + v * w[1:2] + _shift_up(v, rows) * w[2:3] + b


def _ffn_specs():
    gate = lambda shape: pl.BlockSpec(shape, lambda j: (0, j))
    val = lambda shape: pl.BlockSpec(shape, lambda j: (0, j + NCB))
    return [gate((T, TC)), val((T, TC)), gate((3, TC)), val((3, TC)), gate((1, TC)), val((1, TC))]


def _ffn_mid_fwd(name, up, conv_w, conv_b):
    def body(xg_ref, xv_ref, wg_ref, wv_ref, bg_ref, bv_ref, o_ref):
        rows = lax.broadcasted_iota(jnp.int32, (T, TC), 0)
        ug = _conv(xg_ref[...], wg_ref[...], bg_ref[...], rows)
        uv = _conv(xv_ref[...], wv_ref[...], bv_ref[...], rows)
        o_ref[...] = (ug * jax.nn.sigmoid(ug) * uv).astype(BF16)

    return pl.pallas_call(
        body, name=name, grid=(NCB,), in_specs=_ffn_specs(), out_specs=pl.BlockSpec((T, TC), lambda j: (0, j)),
        out_shape=jax.ShapeDtypeStruct((T, DFF), BF16), compiler_params=_params(("parallel",)),
    )(up, up, conv_w, conv_w, conv_b, conv_b)


def _ffn_mid_bwd(name, dact, up, conv_w, conv_b):
    def body(da_ref, xg_ref, xv_ref, wg_ref, wv_ref, bg_ref, bv_ref, dx_ref, dw_ref, db_ref):
        rows = lax.broadcasted_iota(jnp.int32, (T, TC), 0)
        xg, xv, wg, wv = xg_ref[...], xv_ref[...], wg_ref[...], wv_ref[...]
        ug = _conv(xg, wg, bg_ref[...], rows)
        uv = _conv(xv, wv, bv_ref[...], rows)
        sg = jax.nn.sigmoid(ug)
        da = da_ref[...]
        dug = da * uv * (sg * (1.0 + ug * (1.0 - sg)))
        duv = da * (ug * sg)
        for half, (xin, w, du) in enumerate(((xg, wg, dug), (xv, wv, duv))):
            dx = _shift_up(du, rows) * w[0:1] + du * w[1:2] + _shift_down(du, rows) * w[2:3]
            dx_ref[half] = dx.astype(BF16)
            dw_ref[half] = jnp.concatenate(
                [jnp.sum(_shift_down(xin, rows) * du, axis=0, keepdims=True), jnp.sum(xin * du, axis=0, keepdims=True),
                 jnp.sum(_shift_up(xin, rows) * du, axis=0, keepdims=True)], axis=0)
            db_ref[half] = jnp.sum(du, axis=0, keepdims=True)

    return pl.pallas_call(
        body, name=name, grid=(NCB,), in_specs=[pl.BlockSpec((T, TC), lambda j: (0, j))] + _ffn_specs(),
        out_specs=[pl.BlockSpec((2, T, TC), lambda j: (0, 0, j)), pl.BlockSpec((2, 3, TC), lambda j: (0, 0, j)),
                   pl.BlockSpec((2, 1, TC), lambda j: (0, 0, j))],
        out_shape=[jax.ShapeDtypeStruct((2, T, DFF), BF16), jax.ShapeDtypeStruct((2, 3, DFF), F32),
                   jax.ShapeDtypeStruct((2, 1, DFF), F32)],
        compiler_params=_params(("parallel",)),
    )(dact, up, up, conv_w, conv_w, conv_b, conv_b)


def _dup_spec(tm, nj):
    per = DFF // nj
    return pl.BlockSpec((None, tm, nj), lambda a, b, j: (j // per, 0 if tm == T else b, j % per))


def _dup_spec_tn(tm, nj):
    per = DFF // nj
    return pl.BlockSpec((None, tm, nj), lambda j, kt, r: (j // per, 0, j % per))


def _adamw_math(w, g, m, v):
    m = ADAM_B1 * m + (1.0 - ADAM_B1) * g
    v = ADAM_B2 * v + (1.0 - ADAM_B2) * (g * g)
    m_hat = m / (1.0 - ADAM_B1 ** ADAM_STEP)
    v_hat = v / (1.0 - ADAM_B2 ** ADAM_STEP)
    delta = -ADAM_LR * (m_hat / (jnp.sqrt(v_hat) + ADAM_EPS) + ADAM_WD * w)
    return delta, m, v


ADAM_BLOCK = 256 * 1408


def _adamw_sharded(name, w, m, v, parts):
    _, r, c = w.shape
    tr = max(t for t in range(16, r + 1, 16) if r % t == 0 and t * c <= ADAM_BLOCK)

    def body(w_ref, m_ref, v_ref, p0_ref, p1_ref, g_ref, d_ref, nm_ref, nv_ref):
        def run(p_ref):
            g = p_ref[0].astype(F32)
            for k in range(1, N_DEV):
                g = g + p_ref[k].astype(F32)
            d, nm, nv = _adamw_math(w_ref[...], g, m_ref[...], v_ref[...])
            g_ref[...] = g
            d_ref[...] = d
            nm_ref[...] = nm
            nv_ref[...] = nv

        @pl.when(pl.program_id(0) == 0)
        def _():
            run(p0_ref)

        @pl.when(pl.program_id(0) == 1)
        def _():
            run(p1_ref)

    ws = pl.BlockSpec((None, tr, c), lambda l, i: (l, i, 0))
    p0 = pl.BlockSpec((N_DEV, tr, c), lambda l, i: (0, jnp.where(l == 0, i, r // tr - 1), 0))
    p1 = pl.BlockSpec((N_DEV, tr, c), lambda l, i: (0, jnp.where(l == 1, i, 0), 0))
    return pl.pallas_call(
        body, name=name, grid=(DEPTH, r // tr), in_specs=[ws, ws, ws, p0, p1], out_specs=[ws] * 4,
        out_shape=[jax.ShapeDtypeStruct(w.shape, F32)] * 4, compiler_params=_params(("arbitrary", "arbitrary")),
    )(w, m, v, *parts)


def _sum_devices(name, parts):
    r = parts.shape[1]

    def body(p_ref, o_ref):
        g = p_ref[0]
        for k in range(1, N_DEV):
            g = g + p_ref[k]
        o_ref[...] = g

    return pl.pallas_call(
        body, name=name, in_specs=[pl.BlockSpec((N_DEV, r, LANE), lambda: (0, 0, 0))],
        out_specs=pl.BlockSpec((r, LANE), lambda: (0, 0)), out_shape=jax.ShapeDtypeStruct((r, LANE), F32),
        compiler_params=_params(),
    )(parts)


def _adamw_small(name, w, g, m, v):
    spec = pl.BlockSpec(w.shape, lambda: (0, 0))

    def body(w_ref, g_ref, m_ref, v_ref, d_ref, nm_ref, nv_ref):
        d, nm, nv = _adamw_math(w_ref[...], g_ref[...], m_ref[...], v_ref[...])
        d_ref[...] = d
        nm_ref[...] = nm
        nv_ref[...] = nv

    return pl.pallas_call(
        body, name=name, in_specs=[spec] * 4, out_specs=[spec] * 3,
        out_shape=[jax.ShapeDtypeStruct(w.shape, F32)] * 3, compiler_params=_params(),
    )(w, g, m, v)


def _pack(arrays):
    flat = jnp.concatenate([a.reshape(-1) for a in arrays])
    pad = (-flat.shape[0]) % (8 * LANE)
    return jnp.pad(flat, (0, pad)).reshape(-1, LANE)


def _unpack(buf, shapes):
    flat, out, off = buf.reshape(-1), [], 0
    for s in shapes:
        n = 1
        for d in s:
            n *= d
        out.append(flat[off:off + n].reshape(s))
        off += n
    return out


def _local_step(x, target, small, weights, conv_w_full, hand_over, used):
    cos2, sin2 = _rope_tables()
    bias_a = _dilation_bias()
    tables = [_rpb_tables(f"rpb_tables_{l}", small["rpb_c"][l]) for l in range(DEPTH)]
    saved, carry = [], 0.0
    for l in range(DEPTH):
        g1, g2 = small["ln_attn"][l][None] + carry, small["ln_ffn"][l][None]
        gain, sink, cb = small["mix_gain"][l][None], small["sink_b"][l], small["conv_b"][l][None]
        cw = conv_w_full[l]
        bias = tables[l]
        h1 = _rmsnorm_fwd(f"norm_attn_{l}", x, g1)
        proj = _nn_cols(f"proj_in_{l}", h1, weights("w_in", l, [h1, cos2, sin2, bias_a] + tables if l == 0 else h1), 1024)
        zero = used(proj)
        qa, ka, va, qb, kb, vb, qc, kc, vc = _rope_fwd(f"rope_{l}", proj, cos2, sin2)
        oa, lse_a = _attn_a_fwd(f"attn_a_{l}", qa, ka, va, bias_a)
        ob, lse_b = _attn_b_fwd(f"attn_b_{l}", qb, kb, vb, sink + zero)
        oc, lse_c = _attn_c_fwd(f"attn_c_{l}", qc, kc, vc, bias)
        mixed = _mix_fwd(f"mix_{l}", oa, ob, oc, gain)
        x_mid = _nn_rows(f"proj_out_{l}", mixed, weights("w_out", l, mixed), x, 8, 512)
        h2 = _rmsnorm_fwd(f"norm_ffn_{l}", x_mid, g2 + used(x_mid))
        up = _nn_cols(f"ffn_up_{l}", h2, weights("w_up", l, h2), 1024)
        act = _ffn_mid_fwd(f"ffn_mid_{l}", up, cw, cb + used(up))
        x_out = _nn_rows(f"ffn_down_{l}", act, weights("w_down", l, act), x_mid, 4, 1024)
        carry = used(x_out)
        saved.append(dict(x=x, h1=h1, qkv=(qa, ka, va, qb, kb, vb, qc, kc, vc), o=(oa, ob, oc), lse=(lse_a, lse_b, lse_c), mixed=mixed,
                          x_mid=x_mid, h2=h2, up=up, act=act, g1=g1, g2=g2, gain=gain, sink=sink, cb=cb, cw=cw, bias=bias))
        x = x_out

    loss8, dx, dxb, d_ln_final = _loss_head(x, small["ln_final"][None], target)
    sgrads = [None] * DEPTH
    for l in reversed(range(DEPTH)):
        s = saved[l]
        qa, ka, va, qb, kb, vb, qc, kc, vc = s["qkv"]
        oa, ob, oc = s["o"]
        wg_in, wg_out = weights("w_in", l, None), weights("w_out", l, None)
        wg_up, wg_down = weights("w_up", l, None), weights("w_down", l, None)
        g_down = _tn_rows(f"wgrad_down_{l}", s["act"], dxb, wg_down.shape[1], 2)
        zero = hand_over("w_down", l, g_down)
        dact = _nt_rows(f"dgrad_down_{l}", dxb, wg_down, 2)
        dup, d_cw, d_cb = _ffn_mid_bwd(f"ffn_mid_bwd_{l}", dact, s["up"], s["cw"], s["cb"] + zero)
        g_up = _tn_cols(f"wgrad_up_{l}", s["h2"], dup, _dup_spec_tn, 2 * DFF, DFF // 2)
        zero = hand_over("w_up", l, g_up)
        dh2 = _nt_cols(f"dgrad_up_{l}", dup, _dup_spec, wg_up, DFF // 2)
        dx, dxb, d_g2 = _rmsnorm_bwd(f"norm_ffn_bwd_{l}", dh2, s["x_mid"], s["g2"] + zero, dx)
        g_out = _tn_rows(f"wgrad_out_{l}", s["mixed"], dxb, wg_out.shape[1], 2)
        zero = hand_over("w_out", l, g_out)
        dmixed = _nt_rows(f"dgrad_out_{l}", dxb, wg_out, 2)
        doa, dob, doc, d_gain = _mix_bwd(f"mix_bwd_{l}", dmixed, oa, ob, oc, s["gain"] + zero)
        lse_a, lse_b, lse_c = s["lse"]
        dqa, dka, dva = _attn_a_bwd(f"attn_a_bwd_{l}", qa, ka, va, oa, lse_a, doa, bias_a)
        dqb, dkb, dvb, d_sink = _attn_b_bwd(f"attn_b_bwd_{l}", qb, kb, vb, ob, lse_b, dob, s["sink"])
        dqc, dkc, dvc, d_bias = _attn_c_bwd(f"attn_c_bwd_{l}", qc, kc, vc, oc, lse_c, doc, s["bias"])
        d_rpb = _rpb_reduce(f"rpb_reduce_{l}", d_bias)
        dproj = _rope_bwd(f"rope_bwd_{l}", (dqa, dka, dva, dqb, dkb, dvb, dqc, dkc, dvc), cos2, sin2)
        g_in = _tn_cols(f"wgrad_in_{l}", s["h1"], dproj,
                        lambda tm, tn: pl.BlockSpec((tm, tn), lambda j, kt, r: (0, j)), IN_COLS, 1024)
        zero = hand_over("w_in", l, g_in)
        dh1 = _nt_cols(f"dgrad_in_{l}", dproj, lambda tm, nc: pl.BlockSpec((tm, nc), lambda kt, i, j: (i, j)), wg_in,
                       IN_COLS // 2)
        dx, dxb, d_g1 = _rmsnorm_bwd(f"norm_attn_bwd_{l}", dh1, s["x"], s["g1"] + zero, dx)
        sgrads[l] = dict(ln_attn=d_g1[0], sink_b=d_sink[0, :HB], rpb_c=d_rpb, mix_gain=d_gain[0], ln_ffn=d_g2[0],
                         conv_w=d_cw.transpose(1, 0, 2).reshape(3, 2 * DFF), conv_b=d_cb.reshape(2 * DFF))
    return loss8[0, 0], dx, d_ln_final[0], sgrads


SMALL_NAMES = ("ln_attn", "sink_b", "rpb_c", "mix_gain", "ln_ffn", "conv_b")


def kernel(x, ln_attn, w_in, sink_b, rpb_c, mix_gain, w_out, ln_ffn, w_up, conv_w, conv_b, w_down, ln_final, loss_target, m_ln_attn, m_w_in, m_sink_b, m_rpb_c, m_mix_gain, m_w_out, m_ln_ffn, m_w_up, m_conv_w, m_conv_b, m_w_down, m_ln_final, v_ln_attn, v_w_in, v_sink_b, v_rpb_c, v_mix_gain, v_w_out, v_ln_ffn, v_w_up, v_conv_w, v_conv_b, v_w_down, v_ln_final):
    me = 4 * lax.axis_index("x") + 2 * lax.axis_index("y") + lax.axis_index("c")
    small = dict(ln_attn=ln_attn, sink_b=sink_b, rpb_c=rpb_c, mix_gain=mix_gain, ln_ffn=ln_ffn, conv_b=conv_b,
                 ln_final=ln_final)

    names = ("w_in", "w_out", "w_up", "w_down")
    shards = dict(w_in=w_in, w_out=w_out, w_up=w_up, w_down=w_down)
    order = [(n, l) for l in range(DEPTH) for n in names]
    conv_key = ("conv_w", 0)
    started, arrived, forwarded, gathered = {}, {}, {}, {}

    def side_by_side(k):
        return k[0] in ("w_in", "w_up")

    def slot_of(k):
        return _col_slot(shards[k[0]].shape[2]) if side_by_side(k) else _lead_slot

    def begin(name, ks, zero):
        srcs = [_pack([conv_w]) + zero if k == conv_key else (shards[k[0]][k[1]] + zero).astype(BF16) for k in ks]
        lands = [lax.empty((s.shape[0], N_DEV * s.shape[1]) if side_by_side(k) else (N_DEV,) + s.shape, s.dtype)
                 for k, s in zip(ks, srcs)]
        peers = [ALL_PEERS if k == conv_key else NEAR_PEERS for k in ks]
        send, recv, bufs, tok = _copy_start(name, srcs + lands, _gather_plan(peers, [slot_of(k) for k in ks]),
                                            [len(p) for p in peers])
        for i, k in enumerate(ks):
            started[k] = (send[i], recv[i], bufs[i], bufs[len(ks) + i], peers[i])
        return tok

    token = begin("gather_start_first", order[:1], 0.0)
    token = begin("gather_start_rest", [conv_key] + order[1:], token[0, 0])

    def arrive(k, after):
        send, recv, src, land, peers = started[k]
        arrived[k] = _copy_wait(f"gather_{k[0]}_{k[1]}_arrived", [src, land], [send], [recv],
                                _gather_plan([peers], [slot_of(k)]), after)

    queue = list(order)

    def advance(after):
        if not queue:
            return 0.0
        k = queue.pop(0)
        arrive(k, after)
        forwarded[k] = _copy_start(f"gather_{k[0]}_{k[1]}_forward", [arrived[k][1]], _forward_plan(slot_of(k)),
                                   [len(OTHER_CHIPS)])
        return forwarded[k][3][0, 0]

    def complete(k, land):
        shard = arrived[k][0]
        if side_by_side(k):
            return lax.dynamic_update_slice_in_dim(land, shard, me * shard.shape[1], axis=1)
        return lax.dynamic_update_slice_in_dim(land, shard[None], me, axis=0)

    def weights(n, l, after):
        k = (n, l)
        if k not in gathered:
            if k not in forwarded:
                advance(after)
            send_b, recv_b, (land,), _ = forwarded[k]
            (land,) = _copy_wait(f"gather_{n}_{l}_done", [land], send_b, recv_b, _forward_plan(slot_of(k)), after)
            gathered[k] = complete(k, land)
        return gathered[k]

    pending = {}

    def hand_over(n, l, g):
        shard = shards[n].shape[1:]
        send, recv, bufs, tok = _copy_start(f"send_grad_{n}_{l}", [g, lax.empty((N_DEV,) + shard, g.dtype)],
                                            _scatter_plan(slot_of((n, l))), [len(ALL_PEERS)])
        pending[(n, l)] = (send, recv, bufs)
        return tok[0, 0]

    def received(k, after):
        send, recv, bufs = pending[k]
        src, land = _copy_wait(f"recv_grad_{k[0]}_{k[1]}", bufs, send, recv, _scatter_plan(slot_of(k)), after)
        if side_by_side(k):
            own = lax.dynamic_slice_in_dim(src, me * land.shape[2], land.shape[2], axis=1)[None]
        else:
            own = lax.dynamic_slice_in_dim(src, me, 1, axis=0)
        return lax.dynamic_update_slice_in_dim(land, own, me, axis=0)

    arrive(conv_key, token)
    cw_all = complete(conv_key, arrived[conv_key][1])
    nup = w_up.shape[2]
    cw_shards = cw_all.reshape(N_DEV, -1)[:, :DEPTH * 3 * nup].reshape(N_DEV, DEPTH, 3, nup)
    conv_w_full = cw_shards.transpose(1, 2, 0, 3).reshape(DEPTH, 3, N_DEV * nup)

    loss_local, dx, d_ln_final, sgrads = _local_step(
        x[0], loss_target[0], dict(small, ln_attn=ln_attn + token[0, 0]), weights, conv_w_full, hand_over, advance)

    stacked = [jnp.stack([sgrads[l][n] for l in range(DEPTH)]) for n in SMALL_NAMES + ("conv_w",)] + [d_ln_final]
    shapes = [a.shape for a in stacked]
    mine = _pack(stacked)
    send_s, recv_s, bufs_s, _ = _copy_start("gather_small_grads_start", [mine, lax.empty((N_DEV,) + mine.shape, F32)],
                                            _gather_plan([ALL_PEERS], [_lead_slot]), [len(ALL_PEERS)])

    big, after = {}, dx
    moments = dict(w_in=(m_w_in, v_w_in), w_out=(m_w_out, v_w_out), w_up=(m_w_up, v_w_up), w_down=(m_w_down, v_w_down))
    for n in reversed(names):
        parts = (received((n, 0), after), received((n, 1), after))
        big[n] = _adamw_sharded(f"adamw_{n}", shards[n], *moments[n], parts)
        after = big[n][1]

    mine, land = _copy_wait("gather_small_grads_done", bufs_s, send_s, recv_s, _gather_plan([ALL_PEERS], [_lead_slot]), after)
    everyone = lax.dynamic_update_slice_in_dim(land, mine[None], me, axis=0)
    g_small = _unpack(_sum_devices("sum_small_grads", everyone), shapes)
    g = dict(zip(SMALL_NAMES + ("conv_w", "ln_final"), g_small))
    g["conv_w"] = lax.dynamic_slice_in_dim(g["conv_w"], me * nup, nup, axis=2)

    snames = SMALL_NAMES + ("conv_w", "ln_final")
    sw = dict(small, conv_w=conv_w)
    sm = dict(ln_attn=m_ln_attn, sink_b=m_sink_b, rpb_c=m_rpb_c, mix_gain=m_mix_gain, ln_ffn=m_ln_ffn,
              conv_b=m_conv_b, conv_w=m_conv_w, ln_final=m_ln_final)
    sv = dict(ln_attn=v_ln_attn, sink_b=v_sink_b, rpb_c=v_rpb_c, mix_gain=v_mix_gain, ln_ffn=v_ln_ffn,
              conv_b=v_conv_b, conv_w=v_conv_w, ln_final=v_ln_final)
    sshapes = [sw[n].shape for n in snames]
    packed = _adamw_small("adamw_small", _pack([sw[n] for n in snames]), _pack([g[n] for n in snames]),
                          _pack([sm[n] for n in snames]), _pack([sv[n] for n in snames]))
    s_delta, s_m, s_v = (dict(zip(snames, _unpack(buf, sshapes))) for buf in packed)

    loss = lax.psum(loss_local, ("x", "y", "c"))
    outputs = ("ln_attn", "w_in", "sink_b", "rpb_c", "mix_gain", "w_out", "ln_ffn", "w_up", "conv_w", "conv_b",
               "w_down", "ln_final")
    grads = [big[n][0] if n in big else g[n] for n in outputs]
    deltas = [big[n][1] if n in big else s_delta[n] for n in outputs]
    new_m = [big[n][2] if n in big else s_m[n] for n in outputs]
    new_v = [big[n][3] if n in big else s_v[n] for n in outputs]
    return (loss, dx[None], *grads, *deltas, *new_m, *new_v)
```

```python
import functools

import jax
import jax.numpy as jnp
from jax import lax
from jax.experimental import pallas as pl
from jax.experimental.pallas import tpu as pltpu

F32 = jnp.float32
BF16 = jnp.bfloat16

N_DEV = 8
T = 2048
D = 2048
DEPTH = 2
HD = 64
HA, HB, HKV, HC = 12, 10, 2, 10
WA, WB, WKV, WC = HA * HD, HB * HD, HKV * HD, HC * HD
IN_COLS = 3 * WA + WB + 2 * WKV + 3 * WC
DFF = 5632
GRID_W = 64
ROWS = T // GRID_W
NA_ROWS, NA_COLS = 8, 16
WINDOW_B = 128
EPS = 1e-6
NEG = -1e30
ROPE_THETA = 10000.0
LANE = 128
VMEM_LIMIT = 56 * 1024 * 1024

ADAM_LR, ADAM_B1, ADAM_B2, ADAM_EPS, ADAM_WD, ADAM_STEP = 0.001, 0.9, 0.999, 1e-08, 0.01, 10

GROUPS = (("qa", WA, True, True), ("ka", WA, True, False), ("va", WA, False, False),
          ("qb", WB, True, True), ("kb", WKV, True, False), ("vb", WKV, False, False),
          ("qc", WC, False, True), ("kc", WC, False, False), ("vc", WC, False, False))


def _params(sem=None):
    return pltpu.CompilerParams(dimension_semantics=sem, vmem_limit_bytes=VMEM_LIMIT)


HBM_SPEC = pl.BlockSpec(memory_space=pltpu.HBM)
SEM_SPEC = pl.BlockSpec(memory_space=pltpu.SEMAPHORE)
DATAFLOW = pltpu.SideEffectType.DATAFLOW_SIDE_EFFECTING


ALL_PEERS = tuple((p >> 2 & 1, p >> 1 & 1, p & 1) for p in range(1, N_DEV))
OTHER_CHIPS = ((1, 0, 0), (0, 1, 0), (1, 1, 0))
NEAR_PEERS = ((0, 0, 1),) + OTHER_CHIPS


def _flip(x, y, c, f):
    return (1 - x if f[0] else x, 1 - y if f[1] else y, 1 - c if f[2] else c)


def _index(pos):
    return 4 * pos[0] + 2 * pos[1] + pos[2]


class _LocalCopy:
    def __init__(self, src, dst, sem):
        self.copy = pltpu.make_async_copy(src, dst, sem)

    def start(self):
        self.copy.start()

    def wait_send(self):
        self.copy.wait()

    def wait_recv(self):
        pass


def _descriptors(plan, bufs, send_sems, recv_sems):
    x, y, c = lax.axis_index("x"), lax.axis_index("y"), lax.axis_index("c")
    return [_LocalCopy(src, dst, send_sems[g].at[i]) if partner is None else
            pltpu.make_async_remote_copy(src_ref=src, dst_ref=dst, send_sem=send_sems[g].at[i],
                                         recv_sem=recv_sems[g].at[i], device_id=partner,
                                         device_id_type=pl.DeviceIdType.MESH)
            for g, copies in enumerate(plan(bufs, x, y, c)) for i, (src, dst, partner) in enumerate(copies)]


def _copy_start(name, bufs, plan, sizes):
    nb, ng = len(bufs), len(sizes)

    def body(*refs):
        for d in _descriptors(plan, refs[:nb], refs[nb:nb + ng], refs[nb + ng:nb + 2 * ng]):
            d.start()
        refs[2 * nb + 2 * ng][...] = jnp.zeros((8, LANE), F32)

    outs = pl.pallas_call(
        body, name=name,
        out_shape=[pltpu.SemaphoreType.DMA((s,)) for s in sizes] * 2 + [pltpu.HBM(b.shape, b.dtype) for b in bufs]
        + [jax.ShapeDtypeStruct((8, LANE), F32)],
        in_specs=[HBM_SPEC] * nb,
        out_specs=[SEM_SPEC] * (2 * ng) + [HBM_SPEC] * nb + [pl.BlockSpec(memory_space=pltpu.VMEM)],
        input_output_aliases={i: 2 * ng + i for i in range(nb)},
        compiler_params=pltpu.CompilerParams(has_side_effects=DATAFLOW),
    )(*[pltpu.with_memory_space_constraint(b, pltpu.HBM) for b in bufs])
    return outs[:ng], outs[ng:2 * ng], outs[2 * ng:2 * ng + nb], outs[2 * ng + nb]


def _copy_wait(name, bufs, send_sems, recv_sems, plan, after):
    nb, ng = len(bufs), len(send_sems)
    after = list(after) if isinstance(after, (list, tuple)) else [after]

    def body(*refs):
        for d in _descriptors(plan, refs[:nb], refs[nb:nb + ng], refs[nb + ng:nb + 2 * ng]):
            d.wait_send()
            d.wait_recv()

    return pl.pallas_call(
        body, name=name, out_shape=[pltpu.HBM(b.shape, b.dtype) for b in bufs],
        in_specs=[HBM_SPEC] * nb + [SEM_SPEC] * (2 * ng) + [pl.BlockSpec(memory_space=pl.ANY)] * len(after),
        out_specs=[HBM_SPEC] * nb, input_output_aliases={i: i for i in range(nb)},
        compiler_params=pltpu.CompilerParams(has_side_effects=DATAFLOW),
    )(*bufs, *send_sems, *recv_sems, *after)


def _lead_slot(ref, k):
    return ref.at[k]


def _col_slot(width):
    return lambda ref, k: ref.at[:, pl.ds(pl.multiple_of(k * width, LANE), width)]


def _gather_plan(peer_sets, slots):
    def plan(bufs, x, y, c):
        n = len(peer_sets)
        return [[(bufs[i], slots[i](bufs[n + i], _index((x, y, c))), _flip(x, y, c, f)) for f in peers]
                + [(bufs[i], slots[i](bufs[n + i], _index((x, y, c))), None)] for i, peers in enumerate(peer_sets)]
    return plan


def _forward_plan(slot):
    def plan(bufs, x, y, c):
        pieces = [slot(bufs[0], _index(_flip(x, y, c, f))) for f in OTHER_CHIPS]
        return [[(p, p, _flip(x, y, c, (0, 0, 1))) for p in pieces]]
    return plan


def _scatter_plan(slot):
    def plan(bufs, x, y, c):
        me = _index((x, y, c))
        peers = [_flip(x, y, c, f) for f in ALL_PEERS]
        return [[(slot(bufs[0], _index(p)), bufs[1].at[me], p) for p in peers]
                + [(slot(bufs[0], me), bufs[1].at[me], None)]]
    return plan


def _flat2(v):
    return v.reshape(-1, v.shape[-1])


def _matmul(name, kind, a, a_spec, b, b_spec, out_shape, out_spec, grid, res=None, res_spec=None, acc_shape=None):
    dims = {"nn": (((1,), (0,)), ((), ())), "nt": NT_DIMS, "nts": NT_DIMS, "tn": (((0,), (0,)), ((), ()))}[kind]
    nred = grid[-1]

    def body(*refs):
        if res is None:
            a_ref, b_ref, o_ref = refs[:3]
            r_ref = None
        else:
            a_ref, b_ref, r_ref, o_ref = refs[:4]
        if kind == "nts":
            n = b_ref.shape[-1]
            part = sum(lax.dot_general(a_ref[:, blk * n:(blk + 1) * n], b_ref[blk], dims, preferred_element_type=F32)
                       for blk in range(b_ref.shape[0]))
        else:
            part = lax.dot_general(_flat2(a_ref[...]), _flat2(b_ref[...]), dims, preferred_element_type=F32)

        def finish(total):
            if r_ref is not None:
                total = total + r_ref[...]
            o_ref[...] = total.reshape(o_ref.shape).astype(o_ref.dtype)

        if nred == 1:
            finish(part)
        else:
            acc_ref = refs[-1]
            k = pl.program_id(len(grid) - 1)

            @pl.when(k == 0)
            def _():
                acc_ref[...] = part

            @pl.when(jnp.logical_and(k > 0, k < nred - 1))
            def _():
                acc_ref[...] += part

            @pl.when(k == nred - 1)
            def _():
                finish(acc_ref[...] + part)

    ins, specs = [a, b], [a_spec, b_spec]
    if res is not None:
        ins.append(res)
        specs.append(res_spec)
    scratch = [] if nred == 1 else [pltpu.VMEM(acc_shape, F32)]
    return pl.pallas_call(
        body, name=name, grid=grid, in_specs=specs, out_specs=out_spec, out_shape=out_shape, scratch_shapes=scratch,
        compiler_params=_params(("parallel",) * (len(grid) - 1) + ("arbitrary",)),
    )(*ins)


TM = 512


def _nn_cols(name, a, w, tn):
    k, n = w.shape
    tm = 1024
    return _matmul(
        name, "nn", a, pl.BlockSpec((tm, k), lambda j, i, r: (i, 0)),
        w, pl.BlockSpec((k, tn), lambda j, i, r: (0, j)),
        jax.ShapeDtypeStruct((T, n), F32), pl.BlockSpec((tm, tn), lambda j, i, r: (i, j)),
        (n // tn, T // tm, 1))


def _nn_rows(name, a, wg, res, s, tn):
    _, kj, n = wg.shape
    tm = 1024
    return _matmul(
        name, "nn", a, pl.BlockSpec((tm, s * kj), lambda j, i, r: (i, r)),
        wg, pl.BlockSpec((s, kj, tn), lambda j, i, r: (r, 0, j)),
        jax.ShapeDtypeStruct((T, n), F32), pl.BlockSpec((tm, tn), lambda j, i, r: (i, j)),
        (n // tn, T // tm, N_DEV // s), res=res, res_spec=pl.BlockSpec((tm, tn), lambda j, i, r: (i, j)),
        acc_shape=(tm, tn))


def _nt_cols(name, dc, dc_spec_of, w, nc):
    k, n = w.shape
    tm = tk = 1024
    return _matmul(
        name, "nt", dc, dc_spec_of(tm, nc),
        w, pl.BlockSpec((tk, nc), lambda kt, i, j: (kt, j)),
        jax.ShapeDtypeStruct((T, k), F32), pl.BlockSpec((tm, tk), lambda kt, i, j: (i, kt)),
        (k // tk, T // tm, n // nc), acc_shape=(tm, tk))


def _nt_rows(name, dc, wg, s):
    _, kj, n = wg.shape
    return _matmul(
        name, "nt", dc, pl.BlockSpec((TM, n), lambda kt, i, r: (i, 0)),
        wg, pl.BlockSpec((s, kj, n), lambda kt, i, r: (kt, 0, 0)),
        jax.ShapeDtypeStruct((T, N_DEV * kj), F32), pl.BlockSpec((TM, s * kj), lambda kt, i, r: (i, kt)),
        (N_DEV // s, T // TM, 1))


def _tn_cols(name, a, dc, dc_spec_of, n, tn):
    k = a.shape[1]
    tk = 512
    return _matmul(
        name, "tn", a, pl.BlockSpec((T, tk), lambda j, kt, r: (0, kt)),
        dc, dc_spec_of(T, tn),
        jax.ShapeDtypeStruct((k, n), BF16), pl.BlockSpec((tk, tn), lambda j, kt, r: (kt, j)),
        (n // tn, k // tk, 1))


def _tn_rows(name, a, dc, kj, s):
    n = dc.shape[1]
    tn = 512
    return _matmul(
        name, "tn", a, pl.BlockSpec((T, s * kj), lambda kt, j, r: (0, kt)),
        dc, pl.BlockSpec((T, tn), lambda kt, j, r: (0, j)),
        jax.ShapeDtypeStruct((N_DEV, kj, n), BF16), pl.BlockSpec((s, kj, tn), lambda kt, j, r: (kt, 0, j)),
        (N_DEV // s, n // tn, 1))


TR = 256


def _rows(width):
    return pl.BlockSpec((TR, width), lambda i: (i, 0))


def _whole(shape):
    return pl.BlockSpec(shape, lambda i: (0,) * len(shape))


def _rmsnorm_fwd(name, x, g):
    def body(x_ref, g_ref, o_ref):
        xv = x_ref[...]
        r = lax.rsqrt(jnp.mean(xv * xv, axis=-1, keepdims=True) + EPS)
        o_ref[...] = ((xv * r) * g_ref[...]).astype(BF16)

    return pl.pallas_call(
        body, name=name, grid=(T // TR,), in_specs=[_rows(D), _whole((1, D))], out_specs=_rows(D),
        out_shape=jax.ShapeDtypeStruct((T, D), BF16), compiler_params=_params(("parallel",)),
    )(x, g)


def _rms_bwd_math(dy, xv, g):
    r = lax.rsqrt(jnp.mean(xv * xv, axis=-1, keepdims=True) + EPS)
    xhat = xv * r
    dxhat = dy * g
    dx = r * (dxhat - xhat * jnp.mean(dxhat * xhat, axis=-1, keepdims=True))
    return dx, dy * xhat


def _accumulate(ref, val):
    @pl.when(pl.program_id(0) == 0)
    def _():
        ref[...] = val

    @pl.when(pl.program_id(0) > 0)
    def _():
        ref[...] += val


def _rmsnorm_bwd(name, dy, x, g, res):
    def body(dy_ref, x_ref, g_ref, res_ref, dx_ref, dxb_ref, dg_ref):
        dx, dgr = _rms_bwd_math(dy_ref[...], x_ref[...], g_ref[...])
        tot = res_ref[...] + dx
        dx_ref[...] = tot
        dxb_ref[...] = tot.astype(BF16)
        _accumulate(dg_ref, jnp.sum(dgr, axis=0, keepdims=True))

    return pl.pallas_call(
        body, name=name, grid=(T // TR,), in_specs=[_rows(D), _rows(D), _whole((1, D)), _rows(D)],
        out_specs=[_rows(D), _rows(D), _whole((1, D))],
        out_shape=[jax.ShapeDtypeStruct((T, D), F32), jax.ShapeDtypeStruct((T, D), BF16),
                   jax.ShapeDtypeStruct((1, D), F32)],
        compiler_params=_params(("arbitrary",)),
    )(dy, x, g, res)


def _loss_head(x, g, target):
    def body(x_ref, g_ref, t_ref, loss_ref, dx_ref, dxb_ref, dg_ref):
        xv, gv = x_ref[...], g_ref[...]
        r = lax.rsqrt(jnp.mean(xv * xv, axis=-1, keepdims=True) + EPS)
        err = (xv * r) * gv - t_ref[...]
        part = 0.5 * jnp.sum(jnp.mean(err * err, axis=-1, keepdims=True))
        dx, dgr = _rms_bwd_math(err * (1.0 / D), xv, gv)
        dx_ref[...] = dx
        dxb_ref[...] = dx.astype(BF16)
        _accumulate(dg_ref, jnp.sum(dgr, axis=0, keepdims=True))
        _accumulate(loss_ref, jnp.full((8, LANE), part, F32))

    return pl.pallas_call(
        body, name="loss_head", grid=(T // TR,), in_specs=[_rows(D), _whole((1, D)), _rows(D)],
        out_specs=[_whole((8, LANE)), _rows(D), _rows(D), _whole((1, D))],
        out_shape=[jax.ShapeDtypeStruct((8, LANE), F32), jax.ShapeDtypeStruct((T, D), F32),
                   jax.ShapeDtypeStruct((T, D), BF16), jax.ShapeDtypeStruct((1, D), F32)],
        compiler_params=_params(("arbitrary",)),
    )(x, g, target)


MIX_OFFS = ((0, WA), (WA, WB), (WA + WB, WC))


def _mix_fwd(name, oa, ob, oc, gain):
    def body(oa_ref, ob_ref, oc_ref, g_ref, o_ref):
        for ref, (off, w) in zip((oa_ref, ob_ref, oc_ref), MIX_OFFS):
            o = ref[...]
            r = lax.rsqrt(jnp.mean(o * o, axis=-1, keepdims=True) + EPS)
            o_ref[:, off:off + w] = ((o * r) * g_ref[:, off:off + w]).astype(BF16)

    return pl.pallas_call(
        body, name=name, grid=(T // TR,), in_specs=[_rows(WA), _rows(WB), _rows(WC), _whole((1, D))],
        out_specs=_rows(D), out_shape=jax.ShapeDtypeStruct((T, D), BF16), compiler_params=_params(("parallel",)),
    )(oa, ob, oc, gain)


def _mix_bwd(name, dmixed, oa, ob, oc, gain):
    def body(dm_ref, oa_ref, ob_ref, oc_ref, g_ref, doa_ref, dob_ref, doc_ref, dg_ref):
        dgs = []
        for ref, dref, (off, w) in zip((oa_ref, ob_ref, oc_ref), (doa_ref, dob_ref, doc_ref), MIX_OFFS):
            dx, dgr = _rms_bwd_math(dm_ref[:, off:off + w], ref[...], g_ref[:, off:off + w])
            dref[...] = dx
            dgs.append(jnp.sum(dgr, axis=0, keepdims=True))
        _accumulate(dg_ref, jnp.concatenate(dgs, axis=1))

    return pl.pallas_call(
        body, name=name, grid=(T // TR,),
        in_specs=[_rows(D), _rows(WA), _rows(WB), _rows(WC), _whole((1, D))],
        out_specs=[_rows(WA), _rows(WB), _rows(WC), _whole((1, D))],
        out_shape=[jax.ShapeDtypeStruct((T, WA), F32), jax.ShapeDtypeStruct((T, WB), F32),
                   jax.ShapeDtypeStruct((T, WC), F32), jax.ShapeDtypeStruct((1, D), F32)],
        compiler_params=_params(("arbitrary",)),
    )(dmixed, oa, ob, oc, gain)


def _rope_tables():
    inv_freq = ROPE_THETA ** (-jnp.arange(0, HD, 2, dtype=F32) / HD)
    ang = jnp.arange(T, dtype=F32)[:, None] * inv_freq[None, :]
    cos, sin = jnp.cos(ang), jnp.sin(ang)
    cos2 = jnp.tile(jnp.concatenate([cos, cos], axis=1), (1, LANE // HD))
    sin2 = jnp.tile(jnp.concatenate([-sin, sin], axis=1), (1, LANE // HD))
    return cos2, sin2


def _rot_half(v):
    lane = lax.broadcasted_iota(jnp.int32, v.shape, 1)
    return jnp.where(lane % HD < HD // 2, pltpu.roll(v, LANE - HD // 2, 1), pltpu.roll(v, HD // 2, 1))


def _rope_fwd(name, proj, cos2, sin2):
    def body(p_ref, c_ref, s_ref, *outs):
        cv, sv = c_ref[...], s_ref[...]
        off = 0
        for o_ref, (_, w, rot, is_q) in zip(outs, GROUPS):
            for b in range(w // LANE):
                v = p_ref[:, off + b * LANE:off + (b + 1) * LANE]
                if rot:
                    v = v * cv + _rot_half(v) * sv
                if is_q:
                    v = v * (HD ** -0.5)
                o_ref[:, b * LANE:(b + 1) * LANE] = v.astype(BF16)
            off += w

    return pl.pallas_call(
        body, name=name, grid=(T // TR,), in_specs=[_rows(IN_COLS), _rows(LANE), _rows(LANE)],
        out_specs=[_rows(w) for _, w, _, _ in GROUPS],
        out_shape=[jax.ShapeDtypeStruct((T, w), BF16) for _, w, _, _ in GROUPS],
        compiler_params=_params(("parallel",)),
    )(proj, cos2, sin2)


def _rope_bwd(name, grads, cos2, sin2):
    def body(*refs):
        ins, (c_ref, s_ref, o_ref) = refs[:9], refs[9:]
        cv, sv = c_ref[...], s_ref[...]
        off = 0
        for d_ref, (_, w, rot, is_q) in zip(ins, GROUPS):
            for b in range(w // LANE):
                v = d_ref[:, b * LANE:(b + 1) * LANE]
                if is_q:
                    v = v * (HD ** -0.5)
                if rot:
                    v = v * cv + _rot_half(v * sv)
                o_ref[:, off + b * LANE:off + (b + 1) * LANE] = v.astype(BF16)
            off += w

    return pl.pallas_call(
        body, name=name, grid=(T // TR,), in_specs=[_rows(w) for _, w, _, _ in GROUPS] + [_rows(LANE), _rows(LANE)],
        out_specs=_rows(IN_COLS), out_shape=jax.ShapeDtypeStruct((T, IN_COLS), BF16),
        compiler_params=_params(("parallel",)),
    )(*grads, cos2, sin2)


NT_DIMS = (((1,), (1,)), ((), ()))
TN_DIMS = (((0,), (0,)), ((), ()))


def _scores(q, k, bias, valid):
    s = lax.dot_general(q, k, NT_DIMS, preferred_element_type=F32)
    if bias is not None:
        s = s + bias
    if valid is not None:
        s = jnp.where(valid, s, NEG)
    return s


def _heads_fwd(heads):
    scores = [_scores(h["q"], h["k"], h.get("bias"), h.get("valid")) for h in heads]
    soft = []
    for s, h in zip(scores, heads):
        m = jnp.max(s, axis=1, keepdims=True)
        e = jnp.exp(s - m)
        l = jnp.sum(e, axis=1, keepdims=True)
        if h.get("sink") is not None:
            l = l + jnp.exp(h["sink"] - m)
        soft.append((e.astype(BF16), l, m + jnp.log(l)))
    return [(jnp.dot(e, h["v"], preferred_element_type=F32) / l, lse) for (e, l, lse), h in zip(soft, heads)]


def _heads_bwd(heads):
    dobs = [h["do"].astype(BF16) for h in heads]
    scores = [_scores(h["q"], h["k"], h.get("bias"), h.get("valid")) for h in heads]
    dps = [lax.dot_general(dob, h["v"], NT_DIMS, preferred_element_type=F32) for dob, h in zip(dobs, heads)]
    mid = []
    for s, dp, h in zip(scores, dps, heads):
        p = jnp.exp(s - h["lse"])
        delta = jnp.sum(h["do"] * h["o"], axis=1, keepdims=True)
        ds = p * (dp - delta)
        dsink = None if h.get("sink") is None else -jnp.exp(h["sink"] - h["lse"]) * delta
        mid.append((p.astype(BF16), ds, dsink))
    out = []
    for (pb, ds, dsink), dob, h in zip(mid, dobs, heads):
        dsb = ds.astype(BF16)
        out.append((jnp.dot(dsb, h["k"], preferred_element_type=F32),
                    lax.dot_general(dsb, h["q"], TN_DIMS, preferred_element_type=F32),
                    lax.dot_general(pb, dob, TN_DIMS, preferred_element_type=F32), ds, dsink))
    return out


def _per_head(cols):
    return jnp.concatenate([jnp.broadcast_to(c, (c.shape[0], HD)) for c in cols], axis=1)


DILATIONS = ((128, 1), (512, 4), (2048, 16))


def _dilation_bias():
    def body(o_ref):
        t = pl.program_id(0) * TR + lax.broadcasted_iota(jnp.int32, (TR, T), 0)
        ad = jnp.abs(t - lax.broadcasted_iota(jnp.int32, (TR, T), 1))
        count = jnp.zeros((TR, T), jnp.int32)
        for window, r in DILATIONS:
            count += jnp.where(((ad & (r - 1)) == 0) & (ad <= window // 2), 1, 0)
        logs = jnp.where(count == 2, jnp.log(2.0), jnp.where(count == 3, jnp.log(3.0), 0.0)).astype(F32)
        o_ref[...] = jnp.where(count == 0, NEG, logs)

    return pl.pallas_call(
        body, name="dilation_bias", grid=(T // TR,), out_specs=_rows(T),
        out_shape=jax.ShapeDtypeStruct((T, T), F32), compiler_params=_params(("parallel",)),
    )()


BQ_A = 256


def _attn_a_fwd(name, qa, ka, va, bias):
    def body(q_ref, k_ref, v_ref, b_ref, o_ref, lse_ref):
        b = b_ref[...]
        outs = _heads_fwd([dict(q=q_ref[:, h * HD:(h + 1) * HD], k=k_ref[:, h * HD:(h + 1) * HD],
                                v=v_ref[:, h * HD:(h + 1) * HD], bias=b) for h in range(2)])
        o_ref[...] = jnp.concatenate([o for o, _ in outs], axis=1)
        lse_ref[...] = _per_head([lse for _, lse in outs])

    qs = pl.BlockSpec((BQ_A, LANE), lambda p, i: (i, p))
    ks = pl.BlockSpec((T, LANE), lambda p, i: (0, p))
    return pl.pallas_call(
        body, name=name, grid=(HA // 2, T // BQ_A),
        in_specs=[qs, ks, ks, pl.BlockSpec((BQ_A, T), lambda p, i: (i, 0))], out_specs=[qs, qs],
        out_shape=[jax.ShapeDtypeStruct((T, WA), F32)] * 2, compiler_params=_params(("parallel", "parallel")),
    )(qa, ka, va, bias)


def _attn_a_bwd(name, qa, ka, va, oa, lse, doa, bias):
    def body(q_ref, k_ref, v_ref, o_ref, lse_ref, do_ref, b_ref, dq_ref, dk_ref, dv_ref):
        b = b_ref[...]
        sls = [slice(h * HD, (h + 1) * HD) for h in range(2)]
        res = _heads_bwd([dict(q=q_ref[:, sl], k=k_ref[:, sl], v=v_ref[:, sl], o=o_ref[:, sl], do=do_ref[:, sl],
                               lse=lse_ref[:, sl.start:sl.start + 1], bias=b) for sl in sls])
        dq_ref[...] = jnp.concatenate([r[0] for r in res], axis=1)
        dk2, dv2 = jnp.concatenate([r[1] for r in res], axis=1), jnp.concatenate([r[2] for r in res], axis=1)

        @pl.when(pl.program_id(1) == 0)
        def _():
            dk_ref[...] = dk2
            dv_ref[...] = dv2

        @pl.when(pl.program_id(1) > 0)
        def _():
            dk_ref[...] += dk2
            dv_ref[...] += dv2

    qs = pl.BlockSpec((BQ_A, LANE), lambda p, i: (i, p))
    ks = pl.BlockSpec((T, LANE), lambda p, i: (0, p))
    return pl.pallas_call(
        body, name=name, grid=(HA // 2, T // BQ_A),
        in_specs=[qs, ks, ks, qs, qs, qs, pl.BlockSpec((BQ_A, T), lambda p, i: (i, 0))], out_specs=[qs, ks, ks],
        out_shape=[jax.ShapeDtypeStruct((T, WA), F32)] * 3, compiler_params=_params(("parallel", "arbitrary")),
    )(qa, ka, va, oa, lse, doa, bias)


BQ_B = 128
SPAN_B = BQ_B + 2 * WINDOW_B


def _window_b(i):
    start = pl.multiple_of(jnp.clip(i * BQ_B - WINDOW_B, 0, T - SPAN_B), BQ_B)
    qpos = i * BQ_B + lax.broadcasted_iota(jnp.int32, (BQ_B, SPAN_B), 0)
    kpos = start + lax.broadcasted_iota(jnp.int32, (BQ_B, SPAN_B), 1)
    return start, jnp.abs(qpos - kpos) <= WINDOW_B


GROUP_B = HB // HKV


def _stack_group(ref, g):
    return jnp.concatenate([ref[:, h * HD:(h + 1) * HD] for h in range(g * GROUP_B, (g + 1) * GROUP_B)], axis=0)


def _sink_column(sink_ref, g):
    return jnp.concatenate([jnp.full((BQ_B, 1), sink_ref[h], F32) for h in range(g * GROUP_B, (g + 1) * GROUP_B)],
                           axis=0)


def _unstack(stacked):
    return [s[j * BQ_B:(j + 1) * BQ_B] for s in stacked for j in range(GROUP_B)]


def _attn_b_fwd(name, qb, kb, vb, sink):
    def body(sink_ref, q_ref, k_ref, v_ref, o_ref, lse_ref):
        start, valid = _window_b(pl.program_id(0))
        valid = jnp.concatenate([valid] * GROUP_B, axis=0)
        kw, vw = k_ref[pl.ds(start, SPAN_B), :], v_ref[pl.ds(start, SPAN_B), :]
        outs = _heads_fwd([dict(q=_stack_group(q_ref, g), k=kw[:, g * HD:(g + 1) * HD], v=vw[:, g * HD:(g + 1) * HD],
                                valid=valid, sink=_sink_column(sink_ref, g)) for g in range(HKV)])
        o_ref[...] = jnp.concatenate(_unstack([o for o, _ in outs]), axis=1)
        lse_ref[...] = _per_head(_unstack([lse for _, lse in outs]))

    qs = pl.BlockSpec((BQ_B, WB), lambda i: (i, 0))
    return pl.pallas_call(
        body, name=name, grid=(T // BQ_B,),
        in_specs=[pl.BlockSpec(memory_space=pltpu.SMEM), qs, _whole((T, WKV)), _whole((T, WKV))],
        out_specs=[qs, qs],
        out_shape=[jax.ShapeDtypeStruct((T, WB), F32)] * 2, compiler_params=_params(("parallel",)),
    )(sink, qb, kb, vb)


def _attn_b_bwd(name, qb, kb, vb, ob, lse, dob, sink):
    def body(sink_ref, q_ref, k_ref, v_ref, o_ref, lse_ref, do_ref, dq_ref, dk_ref, dv_ref, dsink_ref):
        i = pl.program_id(0)
        start, valid = _window_b(i)
        valid = jnp.concatenate([valid] * GROUP_B, axis=0)
        kw, vw = k_ref[pl.ds(start, SPAN_B), :], v_ref[pl.ds(start, SPAN_B), :]
        res = _heads_bwd([dict(q=_stack_group(q_ref, g), k=kw[:, g * HD:(g + 1) * HD], v=vw[:, g * HD:(g + 1) * HD],
                               o=_stack_group(o_ref, g), do=_stack_group(do_ref, g),
                               lse=jnp.concatenate([lse_ref[:, h * HD:h * HD + 1]
                                                    for h in range(g * GROUP_B, (g + 1) * GROUP_B)], axis=0),
                               valid=valid, sink=_sink_column(sink_ref, g)) for g in range(HKV)])
        dks, dvs = [r[1] for r in res], [r[2] for r in res]
        lane = lax.broadcasted_iota(jnp.int32, (1, LANE), 1)
        dsink = jnp.zeros((1, LANE), F32)
        for h, rows in enumerate(_unstack([r[4] for r in res])):
            dsink += jnp.where(lane == h, jnp.sum(rows), 0.0)
        dq_ref[...] = jnp.concatenate(_unstack([r[0] for r in res]), axis=1)

        @pl.when(i == 0)
        def _():
            dk_ref[...] = jnp.zeros_like(dk_ref)
            dv_ref[...] = jnp.zeros_like(dv_ref)
            dsink_ref[...] = jnp.zeros_like(dsink_ref)

        dk_ref[pl.ds(start, SPAN_B), :] += jnp.concatenate(dks, axis=1)
        dv_ref[pl.ds(start, SPAN_B), :] += jnp.concatenate(dvs, axis=1)
        dsink_ref[...] += dsink

    qs = pl.BlockSpec((BQ_B, WB), lambda i: (i, 0))
    return pl.pallas_call(
        body, name=name, grid=(T // BQ_B,),
        in_specs=[pl.BlockSpec(memory_space=pltpu.SMEM), qs, _whole((T, WKV)), _whole((T, WKV)), qs, qs, qs],
        out_specs=[qs, _whole((T, WKV)), _whole((T, WKV)), _whole((1, LANE))],
        out_shape=[jax.ShapeDtypeStruct((T, WB), F32), jax.ShapeDtypeStruct((T, WKV), F32),
                   jax.ShapeDtypeStruct((T, WKV), F32), jax.ShapeDtypeStruct((1, LANE), F32)],
        compiler_params=_params(("arbitrary",)),
    )(sink, qb, kb, vb, ob, lse, dob)


SPAN_C = NA_ROWS * GRID_W


def _row_start(r):
    return jnp.clip(r - NA_ROWS // 2, 0, ROWS - NA_ROWS)


def _off_index(r):
    return _row_start(r) - r + (NA_ROWS - 1)


N_TAB = 16
RPS = 4


def _rpb_tables(name, rpb):
    circ = jnp.concatenate([rpb[..., NA_COLS - 1:], jnp.zeros(rpb.shape[:2] + (LANE - (2 * NA_COLS - 1),), F32),
                            rpb[..., :NA_COLS - 1]], axis=-1)
    circ = jnp.pad(circ, ((0, 0), (0, N_TAB + 1 - circ.shape[1]), (0, 0)))

    def body(w_ref, o_ref):
        c = lax.broadcasted_iota(jnp.int32, (GRID_W, LANE), 0)
        lane = lax.broadcasted_iota(jnp.int32, (GRID_W, LANE), 1)
        cs = jnp.clip(c - NA_COLS // 2, 0, GRID_W - NA_COLS)
        valid = (lane % GRID_W >= cs) & (lane % GRID_W < cs + NA_COLS)
        toep = [pltpu.roll(jnp.broadcast_to(w_ref[a:a + 1, :], (GRID_W, LANE)), 0, 1, stride=1, stride_axis=0)
                for a in range(N_TAB + 1)]
        for a in range(N_TAB):
            pair = jnp.where(lane < GRID_W, toep[a], pltpu.roll(toep[a + 1], GRID_W, 1))
            o_ref[a] = jnp.where(valid, pair, NEG)

    return pl.pallas_call(
        body, name=name, grid=(HC,),
        in_specs=[pl.BlockSpec((None, N_TAB + 1, LANE), lambda h: (h, 0, 0))],
        out_specs=pl.BlockSpec((None, N_TAB, GRID_W, LANE), lambda h: (h, 0, 0, 0)),
        out_shape=jax.ShapeDtypeStruct((HC, N_TAB, GRID_W, LANE), F32), compiler_params=_params(("parallel",)),
    )(circ)


def _bias_c(t_ref, h, d):
    return jnp.concatenate([t_ref[h, d + k] for k in range(0, NA_ROWS, 2)], axis=1)


def _attn_c_fwd(name, qc, kc, vc, tables):
    def body(q_ref, k_ref, v_ref, t_ref, o_ref, lse_ref):
        heads = []
        for rr in range(RPS):
            r = pl.program_id(1) * RPS + rr
            rows = slice(rr * GRID_W, (rr + 1) * GRID_W)
            start = pl.multiple_of(_row_start(r) * GRID_W, GRID_W)
            kw, vw = k_ref[pl.ds(start, SPAN_C), :], v_ref[pl.ds(start, SPAN_C), :]
            heads += [dict(q=q_ref[rows, h * HD:(h + 1) * HD], k=kw[:, h * HD:(h + 1) * HD], v=vw[:, h * HD:(h + 1) * HD],
                           bias=_bias_c(t_ref, h, _off_index(r))) for h in range(2)]
        outs = _heads_fwd(heads)
        for rr in range(RPS):
            rows = slice(rr * GRID_W, (rr + 1) * GRID_W)
            o_ref[rows, :] = jnp.concatenate([o for o, _ in outs[2 * rr:2 * rr + 2]], axis=1)
            lse_ref[rows, :] = _per_head([lse for _, lse in outs[2 * rr:2 * rr + 2]])

    qs = pl.BlockSpec((RPS * GRID_W, LANE), lambda p, r: (r, p))
    ks = pl.BlockSpec((T, LANE), lambda p, r: (0, p))
    ts = pl.BlockSpec((2, N_TAB, GRID_W, LANE), lambda p, r: (p, 0, 0, 0))
    return pl.pallas_call(
        body, name=name, grid=(HC // 2, ROWS // RPS), in_specs=[qs, ks, ks, ts], out_specs=[qs, qs],
        out_shape=[jax.ShapeDtypeStruct((T, WC), F32)] * 2, compiler_params=_params(("parallel", "parallel")),
    )(qc, kc, vc, tables)


def _attn_c_bwd(name, qc, kc, vc, oc, lse, doc, tables):
    def body(q_ref, k_ref, v_ref, o_ref, lse_ref, do_ref, t_ref, dq_ref, dk_ref, dv_ref, dt_ref):
        @pl.when(pl.program_id(1) == 0)
        def _():
            dk_ref[...] = jnp.zeros_like(dk_ref)
            dv_ref[...] = jnp.zeros_like(dv_ref)
            dt_ref[...] = jnp.zeros_like(dt_ref)

        heads, where = [], []
        for rr in range(RPS):
            r = pl.program_id(1) * RPS + rr
            rows = slice(rr * GRID_W, (rr + 1) * GRID_W)
            d = _off_index(r)
            start = pl.multiple_of(_row_start(r) * GRID_W, GRID_W)
            kw, vw = k_ref[pl.ds(start, SPAN_C), :], v_ref[pl.ds(start, SPAN_C), :]
            where.append((rows, d, start))
            for h in range(2):
                sl = slice(h * HD, (h + 1) * HD)
                heads.append(dict(q=q_ref[rows, sl], k=kw[:, sl], v=vw[:, sl], o=o_ref[rows, sl], do=do_ref[rows, sl],
                                  lse=lse_ref[rows, h * HD:h * HD + 1], bias=_bias_c(t_ref, h, d)))
        res = _heads_bwd(heads)
        for rr, (rows, d, start) in enumerate(where):
            pair = res[2 * rr:2 * rr + 2]
            for h in range(2):
                for k in range(0, NA_ROWS, 2):
                    dt_ref[h, d + k] += pair[h][3][:, k * GRID_W:(k + 2) * GRID_W]
            dq_ref[rows, :] = jnp.concatenate([p[0] for p in pair], axis=1)
            dk_ref[pl.ds(start, SPAN_C), :] += jnp.concatenate([p[1] for p in pair], axis=1)
            dv_ref[pl.ds(start, SPAN_C), :] += jnp.concatenate([p[2] for p in pair], axis=1)

    qs = pl.BlockSpec((RPS * GRID_W, LANE), lambda p, r: (r, p))
    ks = pl.BlockSpec((T, LANE), lambda p, r: (0, p))
    ts = pl.BlockSpec((2, N_TAB, GRID_W, LANE), lambda p, r: (p, 0, 0, 0))
    return pl.pallas_call(
        body, name=name, grid=(HC // 2, ROWS // RPS), in_specs=[qs, ks, ks, qs, qs, qs, ts],
        out_specs=[qs, ks, ks, ts],
        out_shape=[jax.ShapeDtypeStruct((T, WC), F32)] * 3 + [jax.ShapeDtypeStruct((HC, N_TAB, GRID_W, LANE), F32)],
        compiler_params=_params(("parallel", "arbitrary")),
    )(qc, kc, vc, oc, lse, doc, tables)


def _split3(v):
    hi = v.astype(BF16)
    r1 = v - hi.astype(F32)
    mid = r1.astype(BF16)
    lo = (r1 - mid.astype(F32)).astype(BF16)
    return hi, mid, lo


def _rpb_reduce(name, dtables):
    x = dtables.reshape(HC, N_TAB, GRID_W * LANE)
    c = jnp.arange(GRID_W)[:, None]
    lane = jnp.arange(LANE)[None, :]
    col = (lane // GRID_W) * LANE + jnp.clip(lane % GRID_W - c + (NA_COLS - 1), 0, 2 * NA_COLS - 2)
    col_onehot = (col.reshape(-1)[:, None] == jnp.arange(2 * LANE)[None, :]).astype(BF16)
    a2 = jnp.arange(N_TAB)[None, :]
    row_onehot = jnp.concatenate([(jnp.arange(16)[:, None] == a2 + u) & (a2 < 2 * NA_ROWS - 2) for u in range(2)],
                                 axis=1).astype(BF16)

    def body(x_ref, e_ref, f_ref, o_ref):
        y = sum(jnp.dot(part, e_ref[...], preferred_element_type=F32) for part in _split3(x_ref[...]))
        z = jnp.concatenate([y[:, :LANE], y[:, LANE:]], axis=0)
        o_ref[...] = sum(jnp.dot(f_ref[...], part, preferred_element_type=F32) for part in _split3(z))

    out = pl.pallas_call(
        body, name=name, grid=(HC,),
        in_specs=[pl.BlockSpec((None, N_TAB, GRID_W * LANE), lambda h: (h, 0, 0)),
                  _whole((GRID_W * LANE, 2 * LANE)), _whole((16, 2 * N_TAB))],
        out_specs=pl.BlockSpec((None, 16, LANE), lambda h: (h, 0, 0)),
        out_shape=jax.ShapeDtypeStruct((HC, 16, LANE), F32), compiler_params=_params(("parallel",)),
    )(x, col_onehot, row_onehot)
    return out[:, :2 * NA_ROWS - 1, :2 * NA_COLS - 1]


TC = 128
NCB = DFF // TC


def _shift_down(v, rows):
    return jnp.where(rows == 0, 0.0, pltpu.roll(v, 1, 0))


def _shift_up(v, rows):
    return jnp.where(rows == T - 1, 0.0, pltpu.roll(v, T - 1, 0))


def _conv(v, w, b, rows):
    return _shift_down(v, rows) * w[0:1] + v * w[1:2] + _shift_up(v, rows) * w[2:3] + b


def _ffn_specs():
    gate = lambda shape: pl.BlockSpec(shape, lambda j: (0, j))
    val = lambda shape: pl.BlockSpec(shape, lambda j: (0, j + NCB))
    return [gate((T, TC)), val((T, TC)), gate((3, TC)), val((3, TC)), gate((1, TC)), val((1, TC))]


def _ffn_mid_fwd(name, up, conv_w, conv_b):
    def body(xg_ref, xv_ref, wg_ref, wv_ref, bg_ref, bv_ref, o_ref):
        rows = lax.broadcasted_iota(jnp.int32, (T, TC), 0)
        ug = _conv(xg_ref[...], wg_ref[...], bg_ref[...], rows)
        uv = _conv(xv_ref[...], wv_ref[...], bv_ref[...], rows)
        o_ref[...] = (ug * jax.nn.sigmoid(ug) * uv).astype(BF16)

    return pl.pallas_call(
        body, name=name, grid=(NCB,), in_specs=_ffn_specs(), out_specs=pl.BlockSpec((T, TC), lambda j: (0, j)),
        out_shape=jax.ShapeDtypeStruct((T, DFF), BF16), compiler_params=_params(("parallel",)),
    )(up, up, conv_w, conv_w, conv_b, conv_b)


def _ffn_mid_bwd(name, dact, up, conv_w, conv_b):
    def body(da_ref, xg_ref, xv_ref, wg_ref, wv_ref, bg_ref, bv_ref, dx_ref, dw_ref, db_ref):
        rows = lax.broadcasted_iota(jnp.int32, (T, TC), 0)
        xg, xv, wg, wv = xg_ref[...], xv_ref[...], wg_ref[...], wv_ref[...]
        ug = _conv(xg, wg, bg_ref[...], rows)
        uv = _conv(xv, wv, bv_ref[...], rows)
        sg = jax.nn.sigmoid(ug)
        da = da_ref[...]
        dug = da * uv * (sg * (1.0 + ug * (1.0 - sg)))
        duv = da * (ug * sg)
        for half, (xin, w, du) in enumerate(((xg, wg, dug), (xv, wv, duv))):
            dx = _shift_up(du, rows) * w[0:1] + du * w[1:2] + _shift_down(du, rows) * w[2:3]
            dx_ref[half] = dx.astype(BF16)
            dw_ref[half] = jnp.concatenate(
                [jnp.sum(_shift_down(xin, rows) * du, axis=0, keepdims=True), jnp.sum(xin * du, axis=0, keepdims=True),
                 jnp.sum(_shift_up(xin, rows) * du, axis=0, keepdims=True)], axis=0)
            db_ref[half] = jnp.sum(du, axis=0, keepdims=True)

    return pl.pallas_call(
        body, name=name, grid=(NCB,), in_specs=[pl.BlockSpec((T, TC), lambda j: (0, j))] + _ffn_specs(),
        out_specs=[pl.BlockSpec((2, T, TC), lambda j: (0, 0, j)), pl.BlockSpec((2, 3, TC), lambda j: (0, 0, j)),
                   pl.BlockSpec((2, 1, TC), lambda j: (0, 0, j))],
        out_shape=[jax.ShapeDtypeStruct((2, T, DFF), BF16), jax.ShapeDtypeStruct((2, 3, DFF), F32),
                   jax.ShapeDtypeStruct((2, 1, DFF), F32)],
        compiler_params=_params(("parallel",)),
    )(dact, up, up, conv_w, conv_w, conv_b, conv_b)


def _dup_spec(tm, nj):
    per = DFF // nj
    return pl.BlockSpec((None, tm, nj), lambda a, b, j: (j // per, 0 if tm == T else b, j % per))


def _dup_spec_tn(tm, nj):
    per = DFF // nj
    return pl.BlockSpec((None, tm, nj), lambda j, kt, r: (j // per, 0, j % per))


def _adamw_math(w, g, m, v):
    m = ADAM_B1 * m + (1.0 - ADAM_B1) * g
    v = ADAM_B2 * v + (1.0 - ADAM_B2) * (g * g)
    m_hat = m / (1.0 - ADAM_B1 ** ADAM_STEP)
    v_hat = v / (1.0 - ADAM_B2 ** ADAM_STEP)
    delta = -ADAM_LR * (m_hat / (jnp.sqrt(v_hat) + ADAM_EPS) + ADAM_WD * w)
    return delta, m, v


ADAM_BLOCK = 256 * 1408


def _adamw_sharded(name, w, m, v, parts):
    _, r, c = w.shape
    tr = max(t for t in range(16, r + 1, 16) if r % t == 0 and t * c <= ADAM_BLOCK)

    def body(w_ref, m_ref, v_ref, p0_ref, p1_ref, g_ref, d_ref, nm_ref, nv_ref):
        def run(p_ref):
            g = p_ref[0].astype(F32)
            for k in range(1, N_DEV):
                g = g + p_ref[k].astype(F32)
            d, nm, nv = _adamw_math(w_ref[...], g, m_ref[...], v_ref[...])
            g_ref[...] = g
            d_ref[...] = d
            nm_ref[...] = nm
            nv_ref[...] = nv

        @pl.when(pl.program_id(0) == 0)
        def _():
            run(p0_ref)

        @pl.when(pl.program_id(0) == 1)
        def _():
            run(p1_ref)

    ws = pl.BlockSpec((None, tr, c), lambda l, i: (l, i, 0))
    p0 = pl.BlockSpec((N_DEV, tr, c), lambda l, i: (0, jnp.where(l == 0, i, r // tr - 1), 0))
    p1 = pl.BlockSpec((N_DEV, tr, c), lambda l, i: (0, jnp.where(l == 1, i, 0), 0))
    return pl.pallas_call(
        body, name=name, grid=(DEPTH, r // tr), in_specs=[ws, ws, ws, p0, p1], out_specs=[ws] * 4,
        out_shape=[jax.ShapeDtypeStruct(w.shape, F32)] * 4, compiler_params=_params(("arbitrary", "arbitrary")),
    )(w, m, v, *parts)


def _sum_devices(name, parts):
    r = parts.shape[1]

    def body(p_ref, o_ref):
        g = p_ref[0]
        for k in range(1, N_DEV):
            g = g + p_ref[k]
        o_ref[...] = g

    return pl.pallas_call(
        body, name=name, in_specs=[pl.BlockSpec((N_DEV, r, LANE), lambda: (0, 0, 0))],
        out_specs=pl.BlockSpec((r, LANE), lambda: (0, 0)), out_shape=jax.ShapeDtypeStruct((r, LANE), F32),
        compiler_params=_params(),
    )(parts)


def _adamw_small(name, w, g, m, v):
    spec = pl.BlockSpec(w.shape, lambda: (0, 0))

    def body(w_ref, g_ref, m_ref, v_ref, d_ref, nm_ref, nv_ref):
        d, nm, nv = _adamw_math(w_ref[...], g_ref[...], m_ref[...], v_ref[...])
        d_ref[...] = d
        nm_ref[...] = nm
        nv_ref[...] = nv

    return pl.pallas_call(
        body, name=name, in_specs=[spec] * 4, out_specs=[spec] * 3,
        out_shape=[jax.ShapeDtypeStruct(w.shape, F32)] * 3, compiler_params=_params(),
    )(w, g, m, v)


def _pack(arrays):
    flat = jnp.concatenate([a.reshape(-1) for a in arrays])
    pad = (-flat.shape[0]) % (8 * LANE)
    return jnp.pad(flat, (0, pad)).reshape(-1, LANE)


def _unpack(buf, shapes):
    flat, out, off = buf.reshape(-1), [], 0
    for s in shapes:
        n = 1
        for d in s:
            n *= d
        out.append(flat[off:off + n].reshape(s))
        off += n
    return out


def _local_step(x, target, small, weights, conv_w_full, hand_over, used):
    cos2, sin2 = _rope_tables()
    bias_a = _dilation_bias()
    tables = [_rpb_tables(f"rpb_tables_{l}", small["rpb_c"][l]) for l in range(DEPTH)]
    saved, carry = [], 0.0
    for l in range(DEPTH):
        g1, g2 = small["ln_attn"][l][None] + carry, small["ln_ffn"][l][None]
        gain, sink, cb = small["mix_gain"][l][None], small["sink_b"][l], small["conv_b"][l][None]
        cw = conv_w_full[l]
        bias = tables[l]
        h1 = _rmsnorm_fwd(f"norm_attn_{l}", x, g1)
        proj = _nn_cols(f"proj_in_{l}", h1, weights("w_in", l, [h1, cos2, sin2, bias_a] + tables if l == 0 else h1), 1024)
        zero = used(proj)
        qa, ka, va, qb, kb, vb, qc, kc, vc = _rope_fwd(f"rope_{l}", proj, cos2, sin2)
        oa, lse_a = _attn_a_fwd(f"attn_a_{l}", qa, ka, va, bias_a)
        ob, lse_b = _attn_b_fwd(f"attn_b_{l}", qb, kb, vb, sink + zero)
        oc, lse_c = _attn_c_fwd(f"attn_c_{l}", qc, kc, vc, bias)
        mixed = _mix_fwd(f"mix_{l}", oa, ob, oc, gain)
        x_mid = _nn_rows(f"proj_out_{l}", mixed, weights("w_out", l, mixed), x, 8, 512)
        h2 = _rmsnorm_fwd(f"norm_ffn_{l}", x_mid, g2 + used(x_mid))
        up = _nn_cols(f"ffn_up_{l}", h2, weights("w_up", l, h2), 1024)
        act = _ffn_mid_fwd(f"ffn_mid_{l}", up, cw, cb + used(up))
        x_out = _nn_rows(f"ffn_down_{l}", act, weights("w_down", l, act), x_mid, 4, 1024)
        carry = used(x_out)
        saved.append(dict(x=x, h1=h1, qkv=(qa, ka, va, qb, kb, vb, qc, kc, vc), o=(oa, ob, oc), lse=(lse_a, lse_b, lse_c), mixed=mixed,
                          x_mid=x_mid, h2=h2, up=up, act=act, g1=g1, g2=g2, gain=gain, sink=sink, cb=cb, cw=cw, bias=bias))
        x = x_out

    loss8, dx, dxb, d_ln_final = _loss_head(x, small["ln_final"][None], target)
    sgrads = [None] * DEPTH
    for l in reversed(range(DEPTH)):
        s = saved[l]
        qa, ka, va, qb, kb, vb, qc, kc, vc = s["qkv"]
        oa, ob, oc = s["o"]
        wg_in, wg_out = weights("w_in", l, None), weights("w_out", l, None)
        wg_up, wg_down = weights("w_up", l, None), weights("w_down", l, None)
        g_down = _tn_rows(f"wgrad_down_{l}", s["act"], dxb, wg_down.shape[1], 2)
        zero = hand_over("w_down", l, g_down)
        dact = _nt_rows(f"dgrad_down_{l}", dxb, wg_down, 2)
        dup, d_cw, d_cb = _ffn_mid_bwd(f"ffn_mid_bwd_{l}", dact, s["up"], s["cw"], s["cb"] + zero)
        g_up = _tn_cols(f"wgrad_up_{l}", s["h2"], dup, _dup_spec_tn, 2 * DFF, DFF // 2)
        zero = hand_over("w_up", l, g_up)
        dh2 = _nt_cols(f"dgrad_up_{l}", dup, _dup_spec, wg_up, DFF // 2)
        dx, dxb, d_g2 = _rmsnorm_bwd(f"norm_ffn_bwd_{l}", dh2, s["x_mid"], s["g2"] + zero, dx)
        g_out = _tn_rows(f"wgrad_out_{l}", s["mixed"], dxb, wg_out.shape[1], 2)
        zero = hand_over("w_out", l, g_out)
        dmixed = _nt_rows(f"dgrad_out_{l}", dxb, wg_out, 2)
        doa, dob, doc, d_gain = _mix_bwd(f"mix_bwd_{l}", dmixed, oa, ob, oc, s["gain"] + zero)
        lse_a, lse_b, lse_c = s["lse"]
        dqa, dka, dva = _attn_a_bwd(f"attn_a_bwd_{l}", qa, ka, va, oa, lse_a, doa, bias_a)
        dqb, dkb, dvb, d_sink = _attn_b_bwd(f"attn_b_bwd_{l}", qb, kb, vb, ob, lse_b, dob, s["sink"])
        dqc, dkc, dvc, d_bias = _attn_c_bwd(f"attn_c_bwd_{l}", qc, kc, vc, oc, lse_c, doc, s["bias"])
        d_rpb = _rpb_reduce(f"rpb_reduce_{l}", d_bias)
        dproj = _rope_bwd(f"rope_bwd_{l}", (dqa, dka, dva, dqb, dkb, dvb, dqc, dkc, dvc), cos2, sin2)
        g_in = _tn_cols(f"wgrad_in_{l}", s["h1"], dproj,
                        lambda tm, tn: pl.BlockSpec((tm, tn), lambda j, kt, r: (0, j)), IN_COLS, 1024)
        zero = hand_over("w_in", l, g_in)
        dh1 = _nt_cols(f"dgrad_in_{l}", dproj, lambda tm, nc: pl.BlockSpec((tm, nc), lambda kt, i, j: (i, j)), wg_in,
                       IN_COLS // 2)
        dx, dxb, d_g1 = _rmsnorm_bwd(f"norm_attn_bwd_{l}", dh1, s["x"], s["g1"] + zero, dx)
        sgrads[l] = dict(ln_attn=d_g1[0], sink_b=d_sink[0, :HB], rpb_c=d_rpb, mix_gain=d_gain[0], ln_ffn=d_g2[0],
                         conv_w=d_cw.transpose(1, 0, 2).reshape(3, 2 * DFF), conv_b=d_cb.reshape(2 * DFF))
    return loss8[0, 0], dx, d_ln_final[0], sgrads


SMALL_NAMES = ("ln_attn", "sink_b", "rpb_c", "mix_gain", "ln_ffn", "conv_b")


def kernel(x, ln_attn, w_in, sink_b, rpb_c, mix_gain, w_out, ln_ffn, w_up, conv_w, conv_b, w_down, ln_final, loss_target, m_ln_attn, m_w_in, m_sink_b, m_rpb_c, m_mix_gain, m_w_out, m_ln_ffn, m_w_up, m_conv_w, m_conv_b, m_w_down, m_ln_final, v_ln_attn, v_w_in, v_sink_b, v_rpb_c, v_mix_gain, v_w_out, v_ln_ffn, v_w_up, v_conv_w, v_conv_b, v_w_down, v_ln_final):
    me = 4 * lax.axis_index("x") + 2 * lax.axis_index("y") + lax.axis_index("c")
    small = dict(ln_attn=ln_attn, sink_b=sink_b, rpb_c=rpb_c, mix_gain=mix_gain, ln_ffn=ln_ffn, conv_b=conv_b,
                 ln_final=ln_final)

    names = ("w_in", "w_out", "w_up", "w_down")
    shards = dict(w_in=w_in, w_out=w_out, w_up=w_up, w_down=w_down)
    order = [(n, l) for l in range(DEPTH) for n in names]
    conv_key = ("conv_w", 0)
    started, arrived, forwarded, gathered = {}, {}, {}, {}

    def side_by_side(k):
        return k[0] in ("w_in", "w_up")

    def slot_of(k):
        return _col_slot(shards[k[0]].shape[2]) if side_by_side(k) else _lead_slot

    def begin(name, ks, zero):
        srcs = [_pack([conv_w]) + zero if k == conv_key else (shards[k[0]][k[1]] + zero).astype(BF16) for k in ks]
        lands = [lax.empty((s.shape[0], N_DEV * s.shape[1]) if side_by_side(k) else (N_DEV,) + s.shape, s.dtype)
                 for k, s in zip(ks, srcs)]
        peers = [ALL_PEERS if k == conv_key else NEAR_PEERS for k in ks]
        send, recv, bufs, tok = _copy_start(name, srcs + lands, _gather_plan(peers, [slot_of(k) for k in ks]),
                                            [len(p) + 1 for p in peers])
        for i, k in enumerate(ks):
            started[k] = (send[i], recv[i], bufs[i], bufs[len(ks) + i], peers[i])
        return tok

    token = begin("gather_start_first", order[:1], 0.0)
    token = begin("gather_start_rest", [conv_key] + order[1:], token[0, 0])

    def arrive(k, after):
        send, recv, src, land, peers = started[k]
        arrived[k] = _copy_wait(f"gather_{k[0]}_{k[1]}_arrived", [src, land], [send], [recv],
                                _gather_plan([peers], [slot_of(k)]), after)

    queue = list(order)

    def advance(after):
        if not queue:
            return 0.0
        k = queue.pop(0)
        arrive(k, after)
        forwarded[k] = _copy_start(f"gather_{k[0]}_{k[1]}_forward", [arrived[k][1]], _forward_plan(slot_of(k)),
                                   [len(OTHER_CHIPS)])
        return forwarded[k][3][0, 0]

    def weights(n, l, after):
        k = (n, l)
        if k not in gathered:
            if k not in forwarded:
                advance(after)
            send_b, recv_b, (land,), _ = forwarded[k]
            (gathered[k],) = _copy_wait(f"gather_{n}_{l}_done", [land], send_b, recv_b, _forward_plan(slot_of(k)),
                                        after)
        return gathered[k]

    pending = {}

    def hand_over(n, l, g):
        shard = shards[n].shape[1:]
        send, recv, bufs, tok = _copy_start(f"send_grad_{n}_{l}", [g, lax.empty((N_DEV,) + shard, g.dtype)],
                                            _scatter_plan(slot_of((n, l))), [len(ALL_PEERS) + 1])
        pending[(n, l)] = (send, recv, bufs)
        return tok[0, 0]

    def received(k, after):
        send, recv, bufs = pending[k]
        return _copy_wait(f"recv_grad_{k[0]}_{k[1]}", bufs, send, recv, _scatter_plan(slot_of(k)), after)[1]

    arrive(conv_key, token)
    cw_all = arrived[conv_key][1]
    nup = w_up.shape[2]
    cw_shards = cw_all.reshape(N_DEV, -1)[:, :DEPTH * 3 * nup].reshape(N_DEV, DEPTH, 3, nup)
    conv_w_full = cw_shards.transpose(1, 2, 0, 3).reshape(DEPTH, 3, N_DEV * nup)

    loss_local, dx, d_ln_final, sgrads = _local_step(
        x[0], loss_target[0], dict(small, ln_attn=ln_attn + token[0, 0]), weights, conv_w_full, hand_over, advance)

    stacked = [jnp.stack([sgrads[l][n] for l in range(DEPTH)]) for n in SMALL_NAMES + ("conv_w",)] + [d_ln_final]
    shapes = [a.shape for a in stacked]
    mine = _pack(stacked)
    send_s, recv_s, bufs_s, _ = _copy_start("gather_small_grads_start", [mine, lax.empty((N_DEV,) + mine.shape, F32)],
                                            _gather_plan([ALL_PEERS], [_lead_slot]), [len(ALL_PEERS) + 1])

    big, after = {}, dx
    moments = dict(w_in=(m_w_in, v_w_in), w_out=(m_w_out, v_w_out), w_up=(m_w_up, v_w_up), w_down=(m_w_down, v_w_down))
    for n in reversed(names):
        parts = (received((n, 0), after), received((n, 1), after))
        big[n] = _adamw_sharded(f"adamw_{n}", shards[n], *moments[n], parts)
        after = big[n][1]

    _, everyone = _copy_wait("gather_small_grads_done", bufs_s, send_s, recv_s,
                             _gather_plan([ALL_PEERS], [_lead_slot]), after)
    g_small = _unpack(_sum_devices("sum_small_grads", everyone), shapes)
    g = dict(zip(SMALL_NAMES + ("conv_w", "ln_final"), g_small))
    g["conv_w"] = lax.dynamic_slice_in_dim(g["conv_w"], me * nup, nup, axis=2)

    snames = SMALL_NAMES + ("conv_w", "ln_final")
    sw = dict(small, conv_w=conv_w)
    sm = dict(ln_attn=m_ln_attn, sink_b=m_sink_b, rpb_c=m_rpb_c, mix_gain=m_mix_gain, ln_ffn=m_ln_ffn,
              conv_b=m_conv_b, conv_w=m_conv_w, ln_final=m_ln_final)
    sv = dict(ln_attn=v_ln_attn, sink_b=v_sink_b, rpb_c=v_rpb_c, mix_gain=v_mix_gain, ln_ffn=v_ln_ffn,
              conv_b=v_conv_b, conv_w=v_conv_w, ln_final=v_ln_final)
    sshapes = [sw[n].shape for n in snames]
    packed = _adamw_small("adamw_small", _pack([sw[n] for n in snames]), _pack([g[n] for n in snames]),
                          _pack([sm[n] for n in snames]), _pack([sv[n] for n in snames]))
    s_delta, s_m, s_v = (dict(zip(snames, _unpack(buf, sshapes))) for buf in packed)

    loss = lax.psum(loss_local, ("x", "y", "c"))
    outputs = ("ln_attn", "w_in", "sink_b", "rpb_c", "mix_gain", "w_out", "ln_ffn", "w_up", "conv_w", "conv_b",
               "w_down", "ln_final")
    grads = [big[n][0] if n in big else g[n] for n in outputs]
    deltas = [big[n][1] if n in big else s_delta[n] for n in outputs]
    new_m = [big[n][2] if n in big else s_m[n] for n in outputs]
    new_v = [big[n][3] if n in big else s_v[n] for n in outputs]
    return (loss, dx[None], *grads, *deltas, *new_m, *new_v)
```

```python
import functools

import jax
import jax.numpy as jnp
from jax import lax
from jax.experimental import pallas as pl
from jax.experimental.pallas import tpu as pltpu

F32 = jnp.float32
BF16 = jnp.bfloat16

N_DEV = 8
T = 2048
D = 2048
DEPTH = 2
HD = 64
HA, HB, HKV, HC = 12, 10, 2, 10
WA, WB, WKV, WC = HA * HD, HB * HD, HKV * HD, HC * HD
IN_COLS = 3 * WA + WB + 2 * WKV + 3 * WC
DFF = 5632
GRID_W = 64
ROWS = T // GRID_W
NA_ROWS, NA_COLS = 8, 16
WINDOW_B = 128
EPS = 1e-6
NEG = -1e30
ROPE_THETA = 10000.0
LANE = 128
VMEM_LIMIT = 56 * 1024 * 1024

ADAM_LR, ADAM_B1, ADAM_B2, ADAM_EPS, ADAM_WD, ADAM_STEP = 0.001, 0.9, 0.999, 1e-08, 0.01, 10

GROUPS = (("qa", WA, True, True), ("ka", WA, True, False), ("va", WA, False, False),
          ("qb", WB, True, True), ("kb", WKV, True, False), ("vb", WKV, False, False),
          ("qc", WC, False, True), ("kc", WC, False, False), ("vc", WC, False, False))


def _params(sem=None):
    return pltpu.CompilerParams(dimension_semantics=sem, vmem_limit_bytes=VMEM_LIMIT)


HBM_SPEC = pl.BlockSpec(memory_space=pltpu.HBM)
SEM_SPEC = pl.BlockSpec(memory_space=pltpu.SEMAPHORE)
DATAFLOW = pltpu.SideEffectType.DATAFLOW_SIDE_EFFECTING


ALL_PEERS = tuple((p >> 2 & 1, p >> 1 & 1, p & 1) for p in range(1, N_DEV))
OTHER_CHIPS = ((1, 0, 0), (0, 1, 0), (1, 1, 0))
NEAR_PEERS = ((0, 0, 1),) + OTHER_CHIPS


def _flip(x, y, c, f):
    return (1 - x if f[0] else x, 1 - y if f[1] else y, 1 - c if f[2] else c)


def _index(pos):
    return 4 * pos[0] + 2 * pos[1] + pos[2]


class _LocalCopy:
    def __init__(self, src, dst, sem):
        self.copy = pltpu.make_async_copy(src, dst, sem)

    def start(self):
        self.copy.start()

    def wait_send(self):
        self.copy.wait()

    def wait_recv(self):
        pass


def _descriptors(plan, bufs, send_sems, recv_sems):
    x, y, c = lax.axis_index("x"), lax.axis_index("y"), lax.axis_index("c")
    return [_LocalCopy(src, dst, send_sems[g].at[i]) if partner is None else
            pltpu.make_async_remote_copy(src_ref=src, dst_ref=dst, send_sem=send_sems[g].at[i],
                                         recv_sem=recv_sems[g].at[i], device_id=partner,
                                         device_id_type=pl.DeviceIdType.MESH)
            for g, copies in enumerate(plan(bufs, x, y, c)) for i, (src, dst, partner) in enumerate(copies)]


def _copy_start(name, bufs, plan, sizes):
    nb, ng = len(bufs), len(sizes)

    def body(*refs):
        for d in _descriptors(plan, refs[:nb], refs[nb:nb + ng], refs[nb + ng:nb + 2 * ng]):
            d.start()
        refs[2 * nb + 2 * ng][...] = jnp.zeros((8, LANE), F32)

    outs = pl.pallas_call(
        body, name=name,
        out_shape=[pltpu.SemaphoreType.DMA((s,)) for s in sizes] * 2 + [pltpu.HBM(b.shape, b.dtype) for b in bufs]
        + [jax.ShapeDtypeStruct((8, LANE), F32)],
        in_specs=[HBM_SPEC] * nb,
        out_specs=[SEM_SPEC] * (2 * ng) + [HBM_SPEC] * nb + [pl.BlockSpec(memory_space=pltpu.VMEM)],
        input_output_aliases={i: 2 * ng + i for i in range(nb)},
        compiler_params=pltpu.CompilerParams(has_side_effects=DATAFLOW),
    )(*[pltpu.with_memory_space_constraint(b, pltpu.HBM) for b in bufs])
    return outs[:ng], outs[ng:2 * ng], outs[2 * ng:2 * ng + nb], outs[2 * ng + nb]


def _copy_wait(name, bufs, send_sems, recv_sems, plan, after):
    nb, ng = len(bufs), len(send_sems)
    after = list(after) if isinstance(after, (list, tuple)) else [after]

    def body(*refs):
        for d in _descriptors(plan, refs[:nb], refs[nb:nb + ng], refs[nb + ng:nb + 2 * ng]):
            d.wait_send()
            d.wait_recv()

    return pl.pallas_call(
        body, name=name, out_shape=[pltpu.HBM(b.shape, b.dtype) for b in bufs],
        in_specs=[HBM_SPEC] * nb + [SEM_SPEC] * (2 * ng) + [pl.BlockSpec(memory_space=pl.ANY)] * len(after),
        out_specs=[HBM_SPEC] * nb, input_output_aliases={i: i for i in range(nb)},
        compiler_params=pltpu.CompilerParams(has_side_effects=DATAFLOW),
    )(*bufs, *send_sems, *recv_sems, *after)


def _lead_slot(ref, k):
    return ref.at[k]


def _col_slot(width):
    return lambda ref, k: ref.at[:, pl.ds(pl.multiple_of(k * width, LANE), width)]


def _gather_plan(peer_sets, slots):
    def plan(bufs, x, y, c):
        n = len(peer_sets)
        return [[(bufs[i], slots[i](bufs[n + i], _index((x, y, c))), _flip(x, y, c, f)) for f in peers]
                + [(bufs[i], slots[i](bufs[n + i], _index((x, y, c))), None)] for i, peers in enumerate(peer_sets)]
    return plan


def _forward_plan(slot):
    def plan(bufs, x, y, c):
        pieces = [slot(bufs[0], _index(_flip(x, y, c, f))) for f in OTHER_CHIPS]
        return [[(p, p, _flip(x, y, c, (0, 0, 1))) for p in pieces]]
    return plan


def _scatter_plan(slot):
    def plan(bufs, x, y, c):
        me = _index((x, y, c))
        peers = [_flip(x, y, c, f) for f in ALL_PEERS]
        return [[(slot(bufs[0], _index(p)), bufs[1].at[me], p) for p in peers]
                + [(slot(bufs[0], me), bufs[1].at[me], None)]]
    return plan


def _flat2(v):
    return v.reshape(-1, v.shape[-1])


def _matmul(name, kind, a, a_spec, b, b_spec, out_shape, out_spec, grid, res=None, res_spec=None, acc_shape=None):
    dims = {"nn": (((1,), (0,)), ((), ())), "nt": NT_DIMS, "nts": NT_DIMS, "tn": (((0,), (0,)), ((), ()))}[kind]
    nred = grid[-1]

    def body(*refs):
        if res is None:
            a_ref, b_ref, o_ref = refs[:3]
            r_ref = None
        else:
            a_ref, b_ref, r_ref, o_ref = refs[:4]
        if kind == "nts":
            n = b_ref.shape[-1]
            part = sum(lax.dot_general(a_ref[:, blk * n:(blk + 1) * n], b_ref[blk], dims, preferred_element_type=F32)
                       for blk in range(b_ref.shape[0]))
        else:
            part = lax.dot_general(_flat2(a_ref[...]), _flat2(b_ref[...]), dims, preferred_element_type=F32)

        def finish(total):
            if r_ref is not None:
                total = total + r_ref[...]
            o_ref[...] = total.reshape(o_ref.shape).astype(o_ref.dtype)

        if nred == 1:
            finish(part)
        else:
            acc_ref = refs[-1]
            k = pl.program_id(len(grid) - 1)

            @pl.when(k == 0)
            def _():
                acc_ref[...] = part

            @pl.when(jnp.logical_and(k > 0, k < nred - 1))
            def _():
                acc_ref[...] += part

            @pl.when(k == nred - 1)
            def _():
                finish(acc_ref[...] + part)

    ins, specs = [a, b], [a_spec, b_spec]
    if res is not None:
        ins.append(res)
        specs.append(res_spec)
    scratch = [] if nred == 1 else [pltpu.VMEM(acc_shape, F32)]
    return pl.pallas_call(
        body, name=name, grid=grid, in_specs=specs, out_specs=out_spec, out_shape=out_shape, scratch_shapes=scratch,
        compiler_params=_params(("parallel",) * (len(grid) - 1) + ("arbitrary",)),
    )(*ins)


TM = 512


def _nn_cols(name, a, w, tn):
    k, n = w.shape
    tm = 1024
    return _matmul(
        name, "nn", a, pl.BlockSpec((tm, k), lambda j, i, r: (i, 0)),
        w, pl.BlockSpec((k, tn), lambda j, i, r: (0, j)),
        jax.ShapeDtypeStruct((T, n), F32), pl.BlockSpec((tm, tn), lambda j, i, r: (i, j)),
        (n // tn, T // tm, 1))


def _nn_rows(name, a, wg, res, s, tn):
    _, kj, n = wg.shape
    tm = 1024
    return _matmul(
        name, "nn", a, pl.BlockSpec((tm, s * kj), lambda j, i, r: (i, r)),
        wg, pl.BlockSpec((s, kj, tn), lambda j, i, r: (r, 0, j)),
        jax.ShapeDtypeStruct((T, n), F32), pl.BlockSpec((tm, tn), lambda j, i, r: (i, j)),
        (n // tn, T // tm, N_DEV // s), res=res, res_spec=pl.BlockSpec((tm, tn), lambda j, i, r: (i, j)),
        acc_shape=(tm, tn))


def _nt_cols(name, dc, dc_spec_of, w, nc):
    k, n = w.shape
    tm = tk = 1024
    return _matmul(
        name, "nt", dc, dc_spec_of(tm, nc),
        w, pl.BlockSpec((tk, nc), lambda kt, i, j: (kt, j)),
        jax.ShapeDtypeStruct((T, k), F32), pl.BlockSpec((tm, tk), lambda kt, i, j: (i, kt)),
        (k // tk, T // tm, n // nc), acc_shape=(tm, tk))


def _nt_rows(name, dc, wg, s):
    _, kj, n = wg.shape
    return _matmul(
        name, "nt", dc, pl.BlockSpec((TM, n), lambda kt, i, r: (i, 0)),
        wg, pl.BlockSpec((s, kj, n), lambda kt, i, r: (kt, 0, 0)),
        jax.ShapeDtypeStruct((T, N_DEV * kj), F32), pl.BlockSpec((TM, s * kj), lambda kt, i, r: (i, kt)),
        (N_DEV // s, T // TM, 1))


def _tn_cols(name, a, dc, dc_spec_of, n, tn):
    k = a.shape[1]
    tk = 512
    return _matmul(
        name, "tn", a, pl.BlockSpec((T, tk), lambda j, kt, r: (0, kt)),
        dc, dc_spec_of(T, tn),
        jax.ShapeDtypeStruct((k, n), BF16), pl.BlockSpec((tk, tn), lambda j, kt, r: (kt, j)),
        (n // tn, k // tk, 1))


def _tn_rows(name, a, dc, kj, s):
    n = dc.shape[1]
    tn = 512
    return _matmul(
        name, "tn", a, pl.BlockSpec((T, s * kj), lambda kt, j, r: (0, kt)),
        dc, pl.BlockSpec((T, tn), lambda kt, j, r: (0, j)),
        jax.ShapeDtypeStruct((N_DEV, kj, n), BF16), pl.BlockSpec((s, kj, tn), lambda kt, j, r: (kt, 0, j)),
        (N_DEV // s, n // tn, 1))


TR = 256


def _rows(width):
    return pl.BlockSpec((TR, width), lambda i: (i, 0))


def _whole(shape):
    return pl.BlockSpec(shape, lambda i: (0,) * len(shape))


def _rmsnorm_fwd(name, x, g):
    def body(x_ref, g_ref, o_ref):
        xv = x_ref[...]
        r = lax.rsqrt(jnp.mean(xv * xv, axis=-1, keepdims=True) + EPS)
        o_ref[...] = ((xv * r) * g_ref[...]).astype(BF16)

    return pl.pallas_call(
        body, name=name, grid=(T // TR,), in_specs=[_rows(D), _whole((1, D))], out_specs=_rows(D),
        out_shape=jax.ShapeDtypeStruct((T, D), BF16), compiler_params=_params(("parallel",)),
    )(x, g)


def _rms_bwd_math(dy, xv, g):
    r = lax.rsqrt(jnp.mean(xv * xv, axis=-1, keepdims=True) + EPS)
    xhat = xv * r
    dxhat = dy * g
    dx = r * (dxhat - xhat * jnp.mean(dxhat * xhat, axis=-1, keepdims=True))
    return dx, dy * xhat


def _accumulate(ref, val):
    @pl.when(pl.program_id(0) == 0)
    def _():
        ref[...] = val

    @pl.when(pl.program_id(0) > 0)
    def _():
        ref[...] += val


def _rmsnorm_bwd(name, dy, x, g, res):
    def body(dy_ref, x_ref, g_ref, res_ref, dx_ref, dxb_ref, dg_ref):
        dx, dgr = _rms_bwd_math(dy_ref[...], x_ref[...], g_ref[...])
        tot = res_ref[...] + dx
        dx_ref[...] = tot
        dxb_ref[...] = tot.astype(BF16)
        _accumulate(dg_ref, jnp.sum(dgr, axis=0, keepdims=True))

    return pl.pallas_call(
        body, name=name, grid=(T // TR,), in_specs=[_rows(D), _rows(D), _whole((1, D)), _rows(D)],
        out_specs=[_rows(D), _rows(D), _whole((1, D))],
        out_shape=[jax.ShapeDtypeStruct((T, D), F32), jax.ShapeDtypeStruct((T, D), BF16),
                   jax.ShapeDtypeStruct((1, D), F32)],
        compiler_params=_params(("arbitrary",)),
    )(dy, x, g, res)


def _loss_head(x, g, target):
    def body(x_ref, g_ref, t_ref, loss_ref, dx_ref, dxb_ref, dg_ref):
        xv, gv = x_ref[...], g_ref[...]
        r = lax.rsqrt(jnp.mean(xv * xv, axis=-1, keepdims=True) + EPS)
        err = (xv * r) * gv - t_ref[...]
        part = 0.5 * jnp.sum(jnp.mean(err * err, axis=-1, keepdims=True))
        dx, dgr = _rms_bwd_math(err * (1.0 / D), xv, gv)
        dx_ref[...] = dx
        dxb_ref[...] = dx.astype(BF16)
        _accumulate(dg_ref, jnp.sum(dgr, axis=0, keepdims=True))
        _accumulate(loss_ref, jnp.full((8, LANE), part, F32))

    return pl.pallas_call(
        body, name="loss_head", grid=(T // TR,), in_specs=[_rows(D), _whole((1, D)), _rows(D)],
        out_specs=[_whole((8, LANE)), _rows(D), _rows(D), _whole((1, D))],
        out_shape=[jax.ShapeDtypeStruct((8, LANE), F32), jax.ShapeDtypeStruct((T, D), F32),
                   jax.ShapeDtypeStruct((T, D), BF16), jax.ShapeDtypeStruct((1, D), F32)],
        compiler_params=_params(("arbitrary",)),
    )(x, g, target)


MIX_OFFS = ((0, WA), (WA, WB), (WA + WB, WC))


def _mix_fwd(name, oa, ob, oc, gain):
    def body(oa_ref, ob_ref, oc_ref, g_ref, o_ref):
        for ref, (off, w) in zip((oa_ref, ob_ref, oc_ref), MIX_OFFS):
            o = ref[...]
            r = lax.rsqrt(jnp.mean(o * o, axis=-1, keepdims=True) + EPS)
            o_ref[:, off:off + w] = ((o * r) * g_ref[:, off:off + w]).astype(BF16)

    return pl.pallas_call(
        body, name=name, grid=(T // TR,), in_specs=[_rows(WA), _rows(WB), _rows(WC), _whole((1, D))],
        out_specs=_rows(D), out_shape=jax.ShapeDtypeStruct((T, D), BF16), compiler_params=_params(("parallel",)),
    )(oa, ob, oc, gain)


def _mix_bwd(name, dmixed, oa, ob, oc, gain):
    def body(dm_ref, oa_ref, ob_ref, oc_ref, g_ref, doa_ref, dob_ref, doc_ref, dg_ref):
        dgs = []
        for ref, dref, (off, w) in zip((oa_ref, ob_ref, oc_ref), (doa_ref, dob_ref, doc_ref), MIX_OFFS):
            dx, dgr = _rms_bwd_math(dm_ref[:, off:off + w], ref[...], g_ref[:, off:off + w])
            dref[...] = dx
            dgs.append(jnp.sum(dgr, axis=0, keepdims=True))
        _accumulate(dg_ref, jnp.concatenate(dgs, axis=1))

    return pl.pallas_call(
        body, name=name, grid=(T // TR,),
        in_specs=[_rows(D), _rows(WA), _rows(WB), _rows(WC), _whole((1, D))],
        out_specs=[_rows(WA), _rows(WB), _rows(WC), _whole((1, D))],
        out_shape=[jax.ShapeDtypeStruct((T, WA), F32), jax.ShapeDtypeStruct((T, WB), F32),
                   jax.ShapeDtypeStruct((T, WC), F32), jax.ShapeDtypeStruct((1, D), F32)],
        compiler_params=_params(("arbitrary",)),
    )(dmixed, oa, ob, oc, gain)


def _rope_tables():
    inv_freq = ROPE_THETA ** (-jnp.arange(0, HD, 2, dtype=F32) / HD)
    ang = jnp.arange(T, dtype=F32)[:, None] * inv_freq[None, :]
    cos, sin = jnp.cos(ang), jnp.sin(ang)
    cos2 = jnp.tile(jnp.concatenate([cos, cos], axis=1), (1, LANE // HD))
    sin2 = jnp.tile(jnp.concatenate([-sin, sin], axis=1), (1, LANE // HD))
    return cos2, sin2


def _rot_half(v):
    lane = lax.broadcasted_iota(jnp.int32, v.shape, 1)
    return jnp.where(lane % HD < HD // 2, pltpu.roll(v, LANE - HD // 2, 1), pltpu.roll(v, HD // 2, 1))


def _rope_fwd(name, proj, cos2, sin2):
    def body(p_ref, c_ref, s_ref, *outs):
        cv, sv = c_ref[...], s_ref[...]
        off = 0
        for o_ref, (_, w, rot, is_q) in zip(outs, GROUPS):
            for b in range(w // LANE):
                v = p_ref[:, off + b * LANE:off + (b + 1) * LANE]
                if rot:
                    v = v * cv + _rot_half(v) * sv
                if is_q:
                    v = v * (HD ** -0.5)
                o_ref[:, b * LANE:(b + 1) * LANE] = v.astype(BF16)
            off += w

    return pl.pallas_call(
        body, name=name, grid=(T // TR,), in_specs=[_rows(IN_COLS), _rows(LANE), _rows(LANE)],
        out_specs=[_rows(w) for _, w, _, _ in GROUPS],
        out_shape=[jax.ShapeDtypeStruct((T, w), BF16) for _, w, _, _ in GROUPS],
        compiler_params=_params(("parallel",)),
    )(proj, cos2, sin2)


def _rope_bwd(name, grads, cos2, sin2):
    def body(*refs):
        ins, (c_ref, s_ref, o_ref) = refs[:9], refs[9:]
        cv, sv = c_ref[...], s_ref[...]
        off = 0
        for d_ref, (_, w, rot, is_q) in zip(ins, GROUPS):
            for b in range(w // LANE):
                v = d_ref[:, b * LANE:(b + 1) * LANE]
                if is_q:
                    v = v * (HD ** -0.5)
                if rot:
                    v = v * cv + _rot_half(v * sv)
                o_ref[:, off + b * LANE:off + (b + 1) * LANE] = v.astype(BF16)
            off += w

    return pl.pallas_call(
        body, name=name, grid=(T // TR,), in_specs=[_rows(w) for _, w, _, _ in GROUPS] + [_rows(LANE), _rows(LANE)],
        out_specs=_rows(IN_COLS), out_shape=jax.ShapeDtypeStruct((T, IN_COLS), BF16),
        compiler_params=_params(("parallel",)),
    )(*grads, cos2, sin2)


NT_DIMS = (((1,), (1,)), ((), ()))
TN_DIMS = (((0,), (0,)), ((), ()))


def _scores(q, k, bias, valid):
    s = lax.dot_general(q, k, NT_DIMS, preferred_element_type=F32)
    if bias is not None:
        s = s + bias
    if valid is not None:
        s = jnp.where(valid, s, NEG)
    return s


def _heads_fwd(heads):
    scores = [_scores(h["q"], h["k"], h.get("bias"), h.get("valid")) for h in heads]
    soft = []
    for s, h in zip(scores, heads):
        m = jnp.max(s, axis=1, keepdims=True)
        e = jnp.exp(s - m)
        l = jnp.sum(e, axis=1, keepdims=True)
        if h.get("sink") is not None:
            l = l + jnp.exp(h["sink"] - m)
        soft.append((e.astype(BF16), l, m + jnp.log(l)))
    return [(jnp.dot(e, h["v"], preferred_element_type=F32) / l, lse) for (e, l, lse), h in zip(soft, heads)]


def _heads_bwd(heads):
    dobs = [h["do"].astype(BF16) for h in heads]
    scores = [_scores(h["q"], h["k"], h.get("bias"), h.get("valid")) for h in heads]
    dps = [lax.dot_general(dob, h["v"], NT_DIMS, preferred_element_type=F32) for dob, h in zip(dobs, heads)]
    mid = []
    for s, dp, h in zip(scores, dps, heads):
        p = jnp.exp(s - h["lse"])
        delta = jnp.sum(h["do"] * h["o"], axis=1, keepdims=True)
        ds = p * (dp - delta)
        dsink = None if h.get("sink") is None else -jnp.exp(h["sink"] - h["lse"]) * delta
        mid.append((p.astype(BF16), ds, dsink))
    out = []
    for (pb, ds, dsink), dob, h in zip(mid, dobs, heads):
        dsb = ds.astype(BF16)
        out.append((jnp.dot(dsb, h["k"], preferred_element_type=F32),
                    lax.dot_general(dsb, h["q"], TN_DIMS, preferred_element_type=F32),
                    lax.dot_general(pb, dob, TN_DIMS, preferred_element_type=F32), ds, dsink))
    return out


def _per_head(cols):
    return jnp.concatenate([jnp.broadcast_to(c, (c.shape[0], HD)) for c in cols], axis=1)


DILATIONS = ((128, 1), (512, 4), (2048, 16))


def _dilation_bias():
    def body(o_ref):
        t = pl.program_id(0) * TR + lax.broadcasted_iota(jnp.int32, (TR, T), 0)
        ad = jnp.abs(t - lax.broadcasted_iota(jnp.int32, (TR, T), 1))
        count = jnp.zeros((TR, T), jnp.int32)
        for window, r in DILATIONS:
            count += jnp.where(((ad & (r - 1)) == 0) & (ad <= window // 2), 1, 0)
        logs = jnp.where(count == 2, jnp.log(2.0), jnp.where(count == 3, jnp.log(3.0), 0.0)).astype(F32)
        o_ref[...] = jnp.where(count == 0, NEG, logs)

    return pl.pallas_call(
        body, name="dilation_bias", grid=(T // TR,), out_specs=_rows(T),
        out_shape=jax.ShapeDtypeStruct((T, T), F32), compiler_params=_params(("parallel",)),
    )()


BQ_A = 256


def _attn_a_fwd(name, qa, ka, va, bias):
    def body(q_ref, k_ref, v_ref, b_ref, o_ref, lse_ref):
        b = b_ref[...]
        outs = _heads_fwd([dict(q=q_ref[:, h * HD:(h + 1) * HD], k=k_ref[:, h * HD:(h + 1) * HD],
                                v=v_ref[:, h * HD:(h + 1) * HD], bias=b) for h in range(2)])
        o_ref[...] = jnp.concatenate([o for o, _ in outs], axis=1)
        lse_ref[...] = _per_head([lse for _, lse in outs])

    qs = pl.BlockSpec((BQ_A, LANE), lambda p, i: (i, p))
    ks = pl.BlockSpec((T, LANE), lambda p, i: (0, p))
    return pl.pallas_call(
        body, name=name, grid=(HA // 2, T // BQ_A),
        in_specs=[qs, ks, ks, pl.BlockSpec((BQ_A, T), lambda p, i: (i, 0))], out_specs=[qs, qs],
        out_shape=[jax.ShapeDtypeStruct((T, WA), F32)] * 2, compiler_params=_params(("parallel", "parallel")),
    )(qa, ka, va, bias)


def _attn_a_bwd(name, qa, ka, va, oa, lse, doa, bias):
    def body(q_ref, k_ref, v_ref, o_ref, lse_ref, do_ref, b_ref, dq_ref, dk_ref, dv_ref):
        b = b_ref[...]
        sls = [slice(h * HD, (h + 1) * HD) for h in range(2)]
        res = _heads_bwd([dict(q=q_ref[:, sl], k=k_ref[:, sl], v=v_ref[:, sl], o=o_ref[:, sl], do=do_ref[:, sl],
                               lse=lse_ref[:, sl.start:sl.start + 1], bias=b) for sl in sls])
        dq_ref[...] = jnp.concatenate([r[0] for r in res], axis=1)
        dk2, dv2 = jnp.concatenate([r[1] for r in res], axis=1), jnp.concatenate([r[2] for r in res], axis=1)

        @pl.when(pl.program_id(1) == 0)
        def _():
            dk_ref[...] = dk2
            dv_ref[...] = dv2

        @pl.when(pl.program_id(1) > 0)
        def _():
            dk_ref[...] += dk2
            dv_ref[...] += dv2

    qs = pl.BlockSpec((BQ_A, LANE), lambda p, i: (i, p))
    ks = pl.BlockSpec((T, LANE), lambda p, i: (0, p))
    return pl.pallas_call(
        body, name=name, grid=(HA // 2, T // BQ_A),
        in_specs=[qs, ks, ks, qs, qs, qs, pl.BlockSpec((BQ_A, T), lambda p, i: (i, 0))], out_specs=[qs, ks, ks],
        out_shape=[jax.ShapeDtypeStruct((T, WA), F32)] * 3, compiler_params=_params(("parallel", "arbitrary")),
    )(qa, ka, va, oa, lse, doa, bias)


BQ_B = 128
SPAN_B = BQ_B + 2 * WINDOW_B


def _window_b(i):
    start = pl.multiple_of(jnp.clip(i * BQ_B - WINDOW_B, 0, T - SPAN_B), BQ_B)
    qpos = i * BQ_B + lax.broadcasted_iota(jnp.int32, (BQ_B, SPAN_B), 0)
    kpos = start + lax.broadcasted_iota(jnp.int32, (BQ_B, SPAN_B), 1)
    return start, jnp.abs(qpos - kpos) <= WINDOW_B


GROUP_B = HB // HKV


def _stack_group(ref, g):
    return jnp.concatenate([ref[:, h * HD:(h + 1) * HD] for h in range(g * GROUP_B, (g + 1) * GROUP_B)], axis=0)


def _sink_column(sink_ref, g):
    return jnp.concatenate([jnp.full((BQ_B, 1), sink_ref[h], F32) for h in range(g * GROUP_B, (g + 1) * GROUP_B)],
                           axis=0)


def _unstack(stacked):
    return [s[j * BQ_B:(j + 1) * BQ_B] for s in stacked for j in range(GROUP_B)]


def _attn_b_fwd(name, qb, kb, vb, sink):
    def body(sink_ref, q_ref, k_ref, v_ref, o_ref, lse_ref):
        start, valid = _window_b(pl.program_id(0))
        valid = jnp.concatenate([valid] * GROUP_B, axis=0)
        kw, vw = k_ref[pl.ds(start, SPAN_B), :], v_ref[pl.ds(start, SPAN_B), :]
        outs = _heads_fwd([dict(q=_stack_group(q_ref, g), k=kw[:, g * HD:(g + 1) * HD], v=vw[:, g * HD:(g + 1) * HD],
                                valid=valid, sink=_sink_column(sink_ref, g)) for g in range(HKV)])
        o_ref[...] = jnp.concatenate(_unstack([o for o, _ in outs]), axis=1)
        lse_ref[...] = _per_head(_unstack([lse for _, lse in outs]))

    qs = pl.BlockSpec((BQ_B, WB), lambda i: (i, 0))
    return pl.pallas_call(
        body, name=name, grid=(T // BQ_B,),
        in_specs=[pl.BlockSpec(memory_space=pltpu.SMEM), qs, _whole((T, WKV)), _whole((T, WKV))],
        out_specs=[qs, qs],
        out_shape=[jax.ShapeDtypeStruct((T, WB), F32)] * 2, compiler_params=_params(("parallel",)),
    )(sink, qb, kb, vb)


def _attn_b_bwd(name, qb, kb, vb, ob, lse, dob, sink):
    def body(sink_ref, q_ref, k_ref, v_ref, o_ref, lse_ref, do_ref, dq_ref, dk_ref, dv_ref, dsink_ref):
        i = pl.program_id(0)
        start, valid = _window_b(i)
        valid = jnp.concatenate([valid] * GROUP_B, axis=0)
        kw, vw = k_ref[pl.ds(start, SPAN_B), :], v_ref[pl.ds(start, SPAN_B), :]
        res = _heads_bwd([dict(q=_stack_group(q_ref, g), k=kw[:, g * HD:(g + 1) * HD], v=vw[:, g * HD:(g + 1) * HD],
                               o=_stack_group(o_ref, g), do=_stack_group(do_ref, g),
                               lse=jnp.concatenate([lse_ref[:, h * HD:h * HD + 1]
                                                    for h in range(g * GROUP_B, (g + 1) * GROUP_B)], axis=0),
                               valid=valid, sink=_sink_column(sink_ref, g)) for g in range(HKV)])
        dks, dvs = [r[1] for r in res], [r[2] for r in res]
        lane = lax.broadcasted_iota(jnp.int32, (1, LANE), 1)
        dsink = jnp.zeros((1, LANE), F32)
        for h, rows in enumerate(_unstack([r[4] for r in res])):
            dsink += jnp.where(lane == h, jnp.sum(rows), 0.0)
        dq_ref[...] = jnp.concatenate(_unstack([r[0] for r in res]), axis=1)

        @pl.when(i == 0)
        def _():
            dk_ref[...] = jnp.zeros_like(dk_ref)
            dv_ref[...] = jnp.zeros_like(dv_ref)
            dsink_ref[...] = jnp.zeros_like(dsink_ref)

        dk_ref[pl.ds(start, SPAN_B), :] += jnp.concatenate(dks, axis=1)
        dv_ref[pl.ds(start, SPAN_B), :] += jnp.concatenate(dvs, axis=1)
        dsink_ref[...] += dsink

    qs = pl.BlockSpec((BQ_B, WB), lambda i: (i, 0))
    return pl.pallas_call(
        body, name=name, grid=(T // BQ_B,),
        in_specs=[pl.BlockSpec(memory_space=pltpu.SMEM), qs, _whole((T, WKV)), _whole((T, WKV)), qs, qs, qs],
        out_specs=[qs, _whole((T, WKV)), _whole((T, WKV)), _whole((1, LANE))],
        out_shape=[jax.ShapeDtypeStruct((T, WB), F32), jax.ShapeDtypeStruct((T, WKV), F32),
                   jax.ShapeDtypeStruct((T, WKV), F32), jax.ShapeDtypeStruct((1, LANE), F32)],
        compiler_params=_params(("arbitrary",)),
    )(sink, qb, kb, vb, ob, lse, dob)


SPAN_C = NA_ROWS * GRID_W


def _row_start(r):
    return jnp.clip(r - NA_ROWS // 2, 0, ROWS - NA_ROWS)


def _off_index(r):
    return _row_start(r) - r + (NA_ROWS - 1)


N_TAB = 16
RPS = 4


def _rpb_tables(name, rpb):
    circ = jnp.concatenate([rpb[..., NA_COLS - 1:], jnp.zeros(rpb.shape[:2] + (LANE - (2 * NA_COLS - 1),), F32),
                            rpb[..., :NA_COLS - 1]], axis=-1)
    circ = jnp.pad(circ, ((0, 0), (0, N_TAB + 1 - circ.shape[1]), (0, 0)))

    def body(w_ref, o_ref):
        c = lax.broadcasted_iota(jnp.int32, (GRID_W, LANE), 0)
        lane = lax.broadcasted_iota(jnp.int32, (GRID_W, LANE), 1)
        cs = jnp.clip(c - NA_COLS // 2, 0, GRID_W - NA_COLS)
        valid = (lane % GRID_W >= cs) & (lane % GRID_W < cs + NA_COLS)
        toep = [pltpu.roll(jnp.broadcast_to(w_ref[a:a + 1, :], (GRID_W, LANE)), 0, 1, stride=1, stride_axis=0)
                for a in range(N_TAB + 1)]
        for a in range(N_TAB):
            pair = jnp.where(lane < GRID_W, toep[a], pltpu.roll(toep[a + 1], GRID_W, 1))
            o_ref[a] = jnp.where(valid, pair, NEG)

    return pl.pallas_call(
        body, name=name, grid=(HC,),
        in_specs=[pl.BlockSpec((None, N_TAB + 1, LANE), lambda h: (h, 0, 0))],
        out_specs=pl.BlockSpec((None, N_TAB, GRID_W, LANE), lambda h: (h, 0, 0, 0)),
        out_shape=jax.ShapeDtypeStruct((HC, N_TAB, GRID_W, LANE), F32), compiler_params=_params(("parallel",)),
    )(circ)


def _bias_c(t_ref, h, d):
    return jnp.concatenate([t_ref[h, d + k] for k in range(0, NA_ROWS, 2)], axis=1)


def _attn_c_fwd(name, qc, kc, vc, tables):
    def body(q_ref, k_ref, v_ref, t_ref, o_ref, lse_ref):
        heads = []
        for rr in range(RPS):
            r = pl.program_id(1) * RPS + rr
            rows = slice(rr * GRID_W, (rr + 1) * GRID_W)
            start = pl.multiple_of(_row_start(r) * GRID_W, GRID_W)
            kw, vw = k_ref[pl.ds(start, SPAN_C), :], v_ref[pl.ds(start, SPAN_C), :]
            heads += [dict(q=q_ref[rows, h * HD:(h + 1) * HD], k=kw[:, h * HD:(h + 1) * HD], v=vw[:, h * HD:(h + 1) * HD],
                           bias=_bias_c(t_ref, h, _off_index(r))) for h in range(2)]
        outs = _heads_fwd(heads)
        for rr in range(RPS):
            rows = slice(rr * GRID_W, (rr + 1) * GRID_W)
            o_ref[rows, :] = jnp.concatenate([o for o, _ in outs[2 * rr:2 * rr + 2]], axis=1)
            lse_ref[rows, :] = _per_head([lse for _, lse in outs[2 * rr:2 * rr + 2]])

    qs = pl.BlockSpec((RPS * GRID_W, LANE), lambda p, r: (r, p))
    ks = pl.BlockSpec((T, LANE), lambda p, r: (0, p))
    ts = pl.BlockSpec((2, N_TAB, GRID_W, LANE), lambda p, r: (p, 0, 0, 0))
    return pl.pallas_call(
        body, name=name, grid=(HC // 2, ROWS // RPS), in_specs=[qs, ks, ks, ts], out_specs=[qs, qs],
        out_shape=[jax.ShapeDtypeStruct((T, WC), F32)] * 2, compiler_params=_params(("parallel", "parallel")),
    )(qc, kc, vc, tables)


def _attn_c_bwd(name, qc, kc, vc, oc, lse, doc, tables):
    def body(q_ref, k_ref, v_ref, o_ref, lse_ref, do_ref, t_ref, dq_ref, dk_ref, dv_ref, dt_ref):
        @pl.when(pl.program_id(1) == 0)
        def _():
            dk_ref[...] = jnp.zeros_like(dk_ref)
            dv_ref[...] = jnp.zeros_like(dv_ref)
            dt_ref[...] = jnp.zeros_like(dt_ref)

        heads, where = [], []
        for rr in range(RPS):
            r = pl.program_id(1) * RPS + rr
            rows = slice(rr * GRID_W, (rr + 1) * GRID_W)
            d = _off_index(r)
            start = pl.multiple_of(_row_start(r) * GRID_W, GRID_W)
            kw, vw = k_ref[pl.ds(start, SPAN_C), :], v_ref[pl.ds(start, SPAN_C), :]
            where.append((rows, d, start))
            for h in range(2):
                sl = slice(h * HD, (h + 1) * HD)
                heads.append(dict(q=q_ref[rows, sl], k=kw[:, sl], v=vw[:, sl], o=o_ref[rows, sl], do=do_ref[rows, sl],
                                  lse=lse_ref[rows, h * HD:h * HD + 1], bias=_bias_c(t_ref, h, d)))
        res = _heads_bwd(heads)
        for rr, (rows, d, start) in enumerate(where):
            pair = res[2 * rr:2 * rr + 2]
            for h in range(2):
                for k in range(0, NA_ROWS, 2):
                    dt_ref[h, d + k] += pair[h][3][:, k * GRID_W:(k + 2) * GRID_W]
            dq_ref[rows, :] = jnp.concatenate([p[0] for p in pair], axis=1)
            dk_ref[pl.ds(start, SPAN_C), :] += jnp.concatenate([p[1] for p in pair], axis=1)
            dv_ref[pl.ds(start, SPAN_C), :] += jnp.concatenate([p[2] for p in pair], axis=1)

    qs = pl.BlockSpec((RPS * GRID_W, LANE), lambda p, r: (r, p))
    ks = pl.BlockSpec((T, LANE), lambda p, r: (0, p))
    ts = pl.BlockSpec((2, N_TAB, GRID_W, LANE), lambda p, r: (p, 0, 0, 0))
    return pl.pallas_call(
        body, name=name, grid=(HC // 2, ROWS // RPS), in_specs=[qs, ks, ks, qs, qs, qs, ts],
        out_specs=[qs, ks, ks, ts],
        out_shape=[jax.ShapeDtypeStruct((T, WC), F32)] * 3 + [jax.ShapeDtypeStruct((HC, N_TAB, GRID_W, LANE), F32)],
        compiler_params=_params(("parallel", "arbitrary")),
    )(qc, kc, vc, oc, lse, doc, tables)


def _split3(v):
    hi = v.astype(BF16)
    r1 = v - hi.astype(F32)
    mid = r1.astype(BF16)
    lo = (r1 - mid.astype(F32)).astype(BF16)
    return hi, mid, lo


def _rpb_reduce(name, dtables):
    x = dtables.reshape(HC, N_TAB, GRID_W * LANE)
    c = jnp.arange(GRID_W)[:, None]
    lane = jnp.arange(LANE)[None, :]
    col = (lane // GRID_W) * LANE + jnp.clip(lane % GRID_W - c + (NA_COLS - 1), 0, 2 * NA_COLS - 2)
    col_onehot = (col.reshape(-1)[:, None] == jnp.arange(2 * LANE)[None, :]).astype(BF16)
    a2 = jnp.arange(N_TAB)[None, :]
    row_onehot = jnp.concatenate([(jnp.arange(16)[:, None] == a2 + u) & (a2 < 2 * NA_ROWS - 2) for u in range(2)],
                                 axis=1).astype(BF16)

    def body(x_ref, e_ref, f_ref, o_ref):
        y = sum(jnp.dot(part, e_ref[...], preferred_element_type=F32) for part in _split3(x_ref[...]))
        z = jnp.concatenate([y[:, :LANE], y[:, LANE:]], axis=0)
        o_ref[...] = sum(jnp.dot(f_ref[...], part, preferred_element_type=F32) for part in _split3(z))

    out = pl.pallas_call(
        body, name=name, grid=(HC,),
        in_specs=[pl.BlockSpec((None, N_TAB, GRID_W * LANE), lambda h: (h, 0, 0)),
                  _whole((GRID_W * LANE, 2 * LANE)), _whole((16, 2 * N_TAB))],
        out_specs=pl.BlockSpec((None, 16, LANE), lambda h: (h, 0, 0)),
        out_shape=jax.ShapeDtypeStruct((HC, 16, LANE), F32), compiler_params=_params(("parallel",)),
    )(x, col_onehot, row_onehot)
    return out[:, :2 * NA_ROWS - 1, :2 * NA_COLS - 1]


TC = 128
NCB = DFF // TC
CHUNK = 128
MARGIN = 8


def _shift_down(v, rows):
    return jnp.where(rows == 0, 0.0, pltpu.roll(v, 1, 0))


def _shift_up(v, rows):
    return jnp.where(rows == T - 1, 0.0, pltpu.roll(v, T - 1, 0))


def _conv(v, w, b, rows):
    return _shift_down(v, rows) * w[0:1] + v * w[1:2] + _shift_up(v, rows) * w[2:3] + b


def _ffn_specs():
    gate = lambda shape: pl.BlockSpec(shape, lambda j: (0, j))
    val = lambda shape: pl.BlockSpec(shape, lambda j: (0, j + NCB))
    return [gate((T, TC)), val((T, TC)), gate((3, TC)), val((3, TC)), gate((1, TC)), val((1, TC))]


def _ffn_mid_fwd(name, up, conv_w, conv_b):
    def body(xg_ref, xv_ref, wg_ref, wv_ref, bg_ref, bv_ref, o_ref):
        rows = lax.broadcasted_iota(jnp.int32, (T, TC), 0)
        ug = _conv(xg_ref[...], wg_ref[...], bg_ref[...], rows)
        uv = _conv(xv_ref[...], wv_ref[...], bv_ref[...], rows)
        o_ref[...] = (ug * jax.nn.sigmoid(ug) * uv).astype(BF16)

    return pl.pallas_call(
        body, name=name, grid=(NCB,), in_specs=_ffn_specs(), out_specs=pl.BlockSpec((T, TC), lambda j: (0, j)),
        out_shape=jax.ShapeDtypeStruct((T, DFF), BF16), compiler_params=_params(("parallel",)),
    )(up, up, conv_w, conv_w, conv_b, conv_b)


def _ffn_mid_bwd(name, dact, up, conv_w, conv_b):
    window = CHUNK + 2 * MARGIN
    centre = slice(MARGIN, MARGIN + CHUNK)

    def shifted(v):
        return pltpu.roll(v, 1, 0), pltpu.roll(v, window - 1, 0)

    def fold(v):
        return jnp.sum(v[centre].reshape(CHUNK // 8, 8, TC), axis=0)

    def body(da_ref, xg_ref, xv_ref, wg_ref, wv_ref, bg_ref, bv_ref, dx_ref, dw_ref, db_ref, dap, xgp, xvp):
        for src, pad in ((da_ref, dap), (xg_ref, xgp), (xv_ref, xvp)):
            pad[0:MARGIN, :] = jnp.zeros((MARGIN, TC), F32)
            pad[MARGIN:MARGIN + T, :] = src[...]
            pad[MARGIN + T:, :] = jnp.zeros((MARGIN, TC), F32)
        wg, wv, bg, bv = wg_ref[...], wv_ref[...], bg_ref[...], bv_ref[...]

        def chunk(c, sums):
            r0 = pl.multiple_of(c * CHUNK, CHUNK)
            da, xg, xv = dap[pl.ds(r0, window), :], xgp[pl.ds(r0, window), :], xvp[pl.ds(r0, window), :]
            xg_prev, xg_next = shifted(xg)
            xv_prev, xv_next = shifted(xv)
            ug = xg_prev * wg[0:1] + xg * wg[1:2] + xg_next * wg[2:3] + bg
            uv = xv_prev * wv[0:1] + xv * wv[1:2] + xv_next * wv[2:3] + bv
            sg = jax.nn.sigmoid(ug)
            dug = da * uv * (sg * (1.0 + ug * (1.0 - sg)))
            duv = da * (ug * sg)
            out = []
            for half, (x_prev, x, x_next, w, du) in enumerate(((xg_prev, xg, xg_next, wg, dug),
                                                               (xv_prev, xv, xv_next, wv, duv))):
                du_prev, du_next = shifted(du)
                dx = du_next * w[0:1] + du * w[1:2] + du_prev * w[2:3]
                dx_ref[half, pl.ds(r0, CHUNK), :] = dx[centre].astype(BF16)
                out += [fold(x_prev * du), fold(x * du), fold(x_next * du), fold(du)]
            return tuple(s + o for s, o in zip(sums, out))

        sums = lax.fori_loop(0, T // CHUNK, chunk, tuple(jnp.zeros((8, TC), F32) for _ in range(8)))
        rows = [jnp.sum(s, axis=0, keepdims=True) for s in sums]
        for half in range(2):
            dw_ref[half] = jnp.concatenate(rows[4 * half:4 * half + 3], axis=0)
            db_ref[half] = rows[4 * half + 3]

    return pl.pallas_call(
        body, name=name, grid=(NCB,), in_specs=[pl.BlockSpec((T, TC), lambda j: (0, j))] + _ffn_specs(),
        out_specs=[pl.BlockSpec((2, T, TC), lambda j: (0, 0, j)), pl.BlockSpec((2, 3, TC), lambda j: (0, 0, j)),
                   pl.BlockSpec((2, 1, TC), lambda j: (0, 0, j))],
        out_shape=[jax.ShapeDtypeStruct((2, T, DFF), BF16), jax.ShapeDtypeStruct((2, 3, DFF), F32),
                   jax.ShapeDtypeStruct((2, 1, DFF), F32)],
        scratch_shapes=[pltpu.VMEM((T + 2 * MARGIN, TC), F32)] * 3,
        compiler_params=_params(("parallel",)),
    )(dact, up, up, conv_w, conv_w, conv_b, conv_b)


def _dup_spec(tm, nj):
    per = DFF // nj
    return pl.BlockSpec((None, tm, nj), lambda a, b, j: (j // per, 0 if tm == T else b, j % per))


def _dup_spec_tn(tm, nj):
    per = DFF // nj
    return pl.BlockSpec((None, tm, nj), lambda j, kt, r: (j // per, 0, j % per))


def _adamw_math(w, g, m, v):
    m = ADAM_B1 * m + (1.0 - ADAM_B1) * g
    v = ADAM_B2 * v + (1.0 - ADAM_B2) * (g * g)
    m_hat = m / (1.0 - ADAM_B1 ** ADAM_STEP)
    v_hat = v / (1.0 - ADAM_B2 ** ADAM_STEP)
    delta = -ADAM_LR * (m_hat / (jnp.sqrt(v_hat) + ADAM_EPS) + ADAM_WD * w)
    return delta, m, v


ADAM_BLOCK = 256 * 1408


def _adamw_sharded(name, w, m, v, parts):
    _, r, c = w.shape
    tr = max(t for t in range(16, r + 1, 16) if r % t == 0 and t * c <= ADAM_BLOCK)

    def body(w_ref, m_ref, v_ref, p0_ref, p1_ref, g_ref, d_ref, nm_ref, nv_ref):
        def run(p_ref):
            g = p_ref[0].astype(F32)
            for k in range(1, N_DEV):
                g = g + p_ref[k].astype(F32)
            d, nm, nv = _adamw_math(w_ref[...], g, m_ref[...], v_ref[...])
            g_ref[...] = g
            d_ref[...] = d
            nm_ref[...] = nm
            nv_ref[...] = nv

        @pl.when(pl.program_id(0) == 0)
        def _():
            run(p0_ref)

        @pl.when(pl.program_id(0) == 1)
        def _():
            run(p1_ref)

    ws = pl.BlockSpec((None, tr, c), lambda l, i: (l, i, 0))
    p0 = pl.BlockSpec((N_DEV, tr, c), lambda l, i: (0, jnp.where(l == 0, i, r // tr - 1), 0))
    p1 = pl.BlockSpec((N_DEV, tr, c), lambda l, i: (0, jnp.where(l == 1, i, 0), 0))
    return pl.pallas_call(
        body, name=name, grid=(DEPTH, r // tr), in_specs=[ws, ws, ws, p0, p1], out_specs=[ws] * 4,
        out_shape=[jax.ShapeDtypeStruct(w.shape, F32)] * 4, compiler_params=_params(("arbitrary", "arbitrary")),
    )(w, m, v, *parts)


def _sum_devices(name, parts):
    r = parts.shape[1]

    def body(p_ref, o_ref):
        g = p_ref[0]
        for k in range(1, N_DEV):
            g = g + p_ref[k]
        o_ref[...] = g

    return pl.pallas_call(
        body, name=name, in_specs=[pl.BlockSpec((N_DEV, r, LANE), lambda: (0, 0, 0))],
        out_specs=pl.BlockSpec((r, LANE), lambda: (0, 0)), out_shape=jax.ShapeDtypeStruct((r, LANE), F32),
        compiler_params=_params(),
    )(parts)


def _adamw_small(name, ws, gs, ms, vs):
    n = len(ws)
    shapes = [w.shape for w in ws]
    ws, gs, ms, vs = ([a.reshape(1, -1) if a.ndim == 1 else a for a in arrs] for arrs in (ws, gs, ms, vs))
    specs = [pl.BlockSpec(memory_space=pltpu.VMEM)] * n

    def body(*refs):
        for i in range(n):
            w_ref, g_ref, m_ref, v_ref = (refs[k * n + i] for k in range(4))
            d, nm, nv = _adamw_math(w_ref[...], g_ref[...], m_ref[...], v_ref[...])
            refs[4 * n + i][...] = d
            refs[5 * n + i][...] = nm
            refs[6 * n + i][...] = nv

    outs = pl.pallas_call(
        body, name=name, in_specs=specs * 4, out_specs=specs * 3,
        out_shape=[jax.ShapeDtypeStruct(w.shape, F32) for w in ws] * 3, compiler_params=_params(),
    )(*ws, *gs, *ms, *vs)
    outs = [o.reshape(shapes[i % n]) for i, o in enumerate(outs)]
    return outs[:n], outs[n:2 * n], outs[2 * n:]


def _pack(arrays):
    flat = jnp.concatenate([a.reshape(-1) for a in arrays])
    pad = (-flat.shape[0]) % (8 * LANE)
    return jnp.pad(flat, (0, pad)).reshape(-1, LANE)


def _unpack(buf, shapes):
    flat, out, off = buf.reshape(-1), [], 0
    for s in shapes:
        n = 1
        for d in s:
            n *= d
        out.append(flat[off:off + n].reshape(s))
        off += n
    return out


def _local_step(x, target, small, weights, conv_w_full, hand_over, used):
    cos2, sin2 = _rope_tables()
    bias_a = _dilation_bias()
    tables = [_rpb_tables(f"rpb_tables_{l}", small["rpb_c"][l]) for l in range(DEPTH)]
    saved, carry = [], 0.0
    for l in range(DEPTH):
        g1, g2 = small["ln_attn"][l][None] + carry, small["ln_ffn"][l][None]
        gain, sink, cb = small["mix_gain"][l][None], small["sink_b"][l], small["conv_b"][l][None]
        cw = conv_w_full[l]
        bias = tables[l]
        h1 = _rmsnorm_fwd(f"norm_attn_{l}", x, g1)
        proj = _nn_cols(f"proj_in_{l}", h1, weights("w_in", l, [h1, cos2, sin2, bias_a] + tables if l == 0 else h1), 1024)
        zero = used(proj)
        qa, ka, va, qb, kb, vb, qc, kc, vc = _rope_fwd(f"rope_{l}", proj, cos2, sin2)
        oa, lse_a = _attn_a_fwd(f"attn_a_{l}", qa, ka, va, bias_a)
        ob, lse_b = _attn_b_fwd(f"attn_b_{l}", qb, kb, vb, sink + zero)
        oc, lse_c = _attn_c_fwd(f"attn_c_{l}", qc, kc, vc, bias)
        mixed = _mix_fwd(f"mix_{l}", oa, ob, oc, gain)
        x_mid = _nn_rows(f"proj_out_{l}", mixed, weights("w_out", l, mixed), x, 8, 512)
        h2 = _rmsnorm_fwd(f"norm_ffn_{l}", x_mid, g2 + used(x_mid))
        up = _nn_cols(f"ffn_up_{l}", h2, weights("w_up", l, h2), 1024)
        act = _ffn_mid_fwd(f"ffn_mid_{l}", up, cw, cb + used(up))
        x_out = _nn_rows(f"ffn_down_{l}", act, weights("w_down", l, act), x_mid, 4, 1024)
        carry = used(x_out)
        saved.append(dict(x=x, h1=h1, qkv=(qa, ka, va, qb, kb, vb, qc, kc, vc), o=(oa, ob, oc), lse=(lse_a, lse_b, lse_c), mixed=mixed,
                          x_mid=x_mid, h2=h2, up=up, act=act, g1=g1, g2=g2, gain=gain, sink=sink, cb=cb, cw=cw, bias=bias))
        x = x_out

    loss8, dx, dxb, d_ln_final = _loss_head(x, small["ln_final"][None], target)
    sgrads = [None] * DEPTH
    for l in reversed(range(DEPTH)):
        s = saved[l]
        qa, ka, va, qb, kb, vb, qc, kc, vc = s["qkv"]
        oa, ob, oc = s["o"]
        wg_in, wg_out = weights("w_in", l, None), weights("w_out", l, None)
        wg_up, wg_down = weights("w_up", l, None), weights("w_down", l, None)
        g_down = _tn_rows(f"wgrad_down_{l}", s["act"], dxb, wg_down.shape[1], 2)
        zero = hand_over("w_down", l, g_down)
        dact = _nt_rows(f"dgrad_down_{l}", dxb, wg_down, 2)
        dup, d_cw, d_cb = _ffn_mid_bwd(f"ffn_mid_bwd_{l}", dact, s["up"], s["cw"], s["cb"] + zero)
        g_up = _tn_cols(f"wgrad_up_{l}", s["h2"], dup, _dup_spec_tn, 2 * DFF, DFF // 2)
        zero = hand_over("w_up", l, g_up)
        dh2 = _nt_cols(f"dgrad_up_{l}", dup, _dup_spec, wg_up, DFF // 2)
        dx, dxb, d_g2 = _rmsnorm_bwd(f"norm_ffn_bwd_{l}", dh2, s["x_mid"], s["g2"] + zero, dx)
        g_out = _tn_rows(f"wgrad_out_{l}", s["mixed"], dxb, wg_out.shape[1], 2)
        zero = hand_over("w_out", l, g_out)
        dmixed = _nt_rows(f"dgrad_out_{l}", dxb, wg_out, 2)
        doa, dob, doc, d_gain = _mix_bwd(f"mix_bwd_{l}", dmixed, oa, ob, oc, s["gain"] + zero)
        lse_a, lse_b, lse_c = s["lse"]
        dqa, dka, dva = _attn_a_bwd(f"attn_a_bwd_{l}", qa, ka, va, oa, lse_a, doa, bias_a)
        dqb, dkb, dvb, d_sink = _attn_b_bwd(f"attn_b_bwd_{l}", qb, kb, vb, ob, lse_b, dob, s["sink"])
        dqc, dkc, dvc, d_bias = _attn_c_bwd(f"attn_c_bwd_{l}", qc, kc, vc, oc, lse_c, doc, s["bias"])
        d_rpb = _rpb_reduce(f"rpb_reduce_{l}", d_bias)
        dproj = _rope_bwd(f"rope_bwd_{l}", (dqa, dka, dva, dqb, dkb, dvb, dqc, dkc, dvc), cos2, sin2)
        g_in = _tn_cols(f"wgrad_in_{l}", s["h1"], dproj,
                        lambda tm, tn: pl.BlockSpec((tm, tn), lambda j, kt, r: (0, j)), IN_COLS, 1024)
        zero = hand_over("w_in", l, g_in)
        dh1 = _nt_cols(f"dgrad_in_{l}", dproj, lambda tm, nc: pl.BlockSpec((tm, nc), lambda kt, i, j: (i, j)), wg_in,
                       IN_COLS // 2)
        dx, dxb, d_g1 = _rmsnorm_bwd(f"norm_attn_bwd_{l}", dh1, s["x"], s["g1"] + zero, dx)
        sgrads[l] = dict(ln_attn=d_g1[0], sink_b=d_sink[0, :HB], rpb_c=d_rpb, mix_gain=d_gain[0], ln_ffn=d_g2[0],
                         conv_w=d_cw.transpose(1, 0, 2).reshape(3, 2 * DFF), conv_b=d_cb.reshape(2 * DFF))
    return loss8[0, 0], dx, d_ln_final[0], sgrads


SMALL_NAMES = ("ln_attn", "sink_b", "rpb_c", "mix_gain", "ln_ffn", "conv_b")


def kernel(x, ln_attn, w_in, sink_b, rpb_c, mix_gain, w_out, ln_ffn, w_up, conv_w, conv_b, w_down, ln_final, loss_target, m_ln_attn, m_w_in, m_sink_b, m_rpb_c, m_mix_gain, m_w_out, m_ln_ffn, m_w_up, m_conv_w, m_conv_b, m_w_down, m_ln_final, v_ln_attn, v_w_in, v_sink_b, v_rpb_c, v_mix_gain, v_w_out, v_ln_ffn, v_w_up, v_conv_w, v_conv_b, v_w_down, v_ln_final):
    me = 4 * lax.axis_index("x") + 2 * lax.axis_index("y") + lax.axis_index("c")
    small = dict(ln_attn=ln_attn, sink_b=sink_b, rpb_c=rpb_c, mix_gain=mix_gain, ln_ffn=ln_ffn, conv_b=conv_b,
                 ln_final=ln_final)

    names = ("w_in", "w_out", "w_up", "w_down")
    shards = dict(w_in=w_in, w_out=w_out, w_up=w_up, w_down=w_down)
    order = [(n, l) for l in range(DEPTH) for n in names]
    conv_key = ("conv_w", 0)
    started, arrived, forwarded, gathered = {}, {}, {}, {}

    def side_by_side(k):
        return k[0] in ("w_in", "w_up")

    def slot_of(k):
        return _col_slot(shards[k[0]].shape[2]) if side_by_side(k) else _lead_slot

    def begin(name, ks, zero):
        srcs = [_pack([conv_w]) + zero if k == conv_key else (shards[k[0]][k[1]] + zero).astype(BF16) for k in ks]
        lands = [lax.empty((s.shape[0], N_DEV * s.shape[1]) if side_by_side(k) else (N_DEV,) + s.shape, s.dtype)
                 for k, s in zip(ks, srcs)]
        peers = [ALL_PEERS if k == conv_key else NEAR_PEERS for k in ks]
        send, recv, bufs, tok = _copy_start(name, srcs + lands, _gather_plan(peers, [slot_of(k) for k in ks]),
                                            [len(p) + 1 for p in peers])
        for i, k in enumerate(ks):
            started[k] = (send[i], recv[i], bufs[i], bufs[len(ks) + i], peers[i])
        return tok

    token = begin("gather_start_first", order[:1], 0.0)
    token = begin("gather_start_rest", [conv_key] + order[1:], token[0, 0])

    def arrive(k, after):
        send, recv, src, land, peers = started[k]
        arrived[k] = _copy_wait(f"gather_{k[0]}_{k[1]}_arrived", [src, land], [send], [recv],
                                _gather_plan([peers], [slot_of(k)]), after)

    queue = list(order)

    def advance(after):
        if not queue:
            return 0.0
        k = queue.pop(0)
        arrive(k, after)
        forwarded[k] = _copy_start(f"gather_{k[0]}_{k[1]}_forward", [arrived[k][1]], _forward_plan(slot_of(k)),
                                   [len(OTHER_CHIPS)])
        return forwarded[k][3][0, 0]

    def weights(n, l, after):
        k = (n, l)
        if k not in gathered:
            if k not in forwarded:
                advance(after)
            send_b, recv_b, (land,), _ = forwarded[k]
            (gathered[k],) = _copy_wait(f"gather_{n}_{l}_done", [land], send_b, recv_b, _forward_plan(slot_of(k)),
                                        after)
        return gathered[k]

    pending = {}

    def hand_over(n, l, g):
        shard = shards[n].shape[1:]
        send, recv, bufs, tok = _copy_start(f"send_grad_{n}_{l}", [g, lax.empty((N_DEV,) + shard, g.dtype)],
                                            _scatter_plan(slot_of((n, l))), [len(ALL_PEERS) + 1])
        pending[(n, l)] = (send, recv, bufs)
        return tok[0, 0]

    def received(k, after):
        send, recv, bufs = pending[k]
        return _copy_wait(f"recv_grad_{k[0]}_{k[1]}", bufs, send, recv, _scatter_plan(slot_of(k)), after)[1]

    arrive(conv_key, token)
    cw_all = arrived[conv_key][1]
    nup = w_up.shape[2]
    cw_shards = cw_all.reshape(N_DEV, -1)[:, :DEPTH * 3 * nup].reshape(N_DEV, DEPTH, 3, nup)
    conv_w_full = cw_shards.transpose(1, 2, 0, 3).reshape(DEPTH, 3, N_DEV * nup)

    loss_local, dx, d_ln_final, sgrads = _local_step(
        x[0], loss_target[0], dict(small, ln_attn=ln_attn + token[0, 0]), weights, conv_w_full, hand_over, advance)

    stacked = [jnp.stack([sgrads[l][n] for l in range(DEPTH)]) for n in SMALL_NAMES + ("conv_w",)] + [d_ln_final]
    shapes = [a.shape for a in stacked]
    mine = _pack(stacked)
    send_s, recv_s, bufs_s, _ = _copy_start("gather_small_grads_start", [mine, lax.empty((N_DEV,) + mine.shape, F32)],
                                            _gather_plan([ALL_PEERS], [_lead_slot]), [len(ALL_PEERS) + 1])

    big, after = {}, dx
    moments = dict(w_in=(m_w_in, v_w_in), w_out=(m_w_out, v_w_out), w_up=(m_w_up, v_w_up), w_down=(m_w_down, v_w_down))
    for n in reversed(names):
        parts = (received((n, 0), after), received((n, 1), after))
        big[n] = _adamw_sharded(f"adamw_{n}", shards[n], *moments[n], parts)
        after = big[n][1]

    _, everyone = _copy_wait("gather_small_grads_done", bufs_s, send_s, recv_s,
                             _gather_plan([ALL_PEERS], [_lead_slot]), after)
    g_small = _unpack(_sum_devices("sum_small_grads", everyone), shapes)
    g = dict(zip(SMALL_NAMES + ("conv_w", "ln_final"), g_small))
    g["conv_w"] = lax.dynamic_slice_in_dim(g["conv_w"], me * nup, nup, axis=2)

    snames = SMALL_NAMES + ("conv_w", "ln_final")
    sw = dict(small, conv_w=conv_w)
    sm = dict(ln_attn=m_ln_attn, sink_b=m_sink_b, rpb_c=m_rpb_c, mix_gain=m_mix_gain, ln_ffn=m_ln_ffn,
              conv_b=m_conv_b, conv_w=m_conv_w, ln_final=m_ln_final)
    sv = dict(ln_attn=v_ln_attn, sink_b=v_sink_b, rpb_c=v_rpb_c, mix_gain=v_mix_gain, ln_ffn=v_ln_ffn,
              conv_b=v_conv_b, conv_w=v_conv_w, ln_final=v_ln_final)
    s_delta, s_m, s_v = (dict(zip(snames, out)) for out in _adamw_small(
        "adamw_small", [sw[n] for n in snames], [g[n] for n in snames], [sm[n] for n in snames],
        [sv[n] for n in snames]))

    loss = lax.psum(loss_local, ("x", "y", "c"))
    outputs = ("ln_attn", "w_in", "sink_b", "rpb_c", "mix_gain", "w_out", "ln_ffn", "w_up", "conv_w", "conv_b",
               "w_down", "ln_final")
    grads = [big[n][0] if n in big else g[n] for n in outputs]
    deltas = [big[n][1] if n in big else s_delta[n] for n in outputs]
    new_m = [big[n][2] if n in big else s_m[n] for n in outputs]
    new_v = [big[n][3] if n in big else s_v[n] for n in outputs]
    return (loss, dx[None], *grads, *deltas, *new_m, *new_v)
```

```python
import functools

import jax
import jax.numpy as jnp
from jax import lax
from jax.experimental import pallas as pl
from jax.experimental.pallas import tpu as pltpu

F32 = jnp.float32
BF16 = jnp.bfloat16

N_DEV = 8
T = 2048
D = 2048
DEPTH = 2
HD = 64
HA, HB, HKV, HC = 12, 10, 2, 10
WA, WB, WKV, WC = HA * HD, HB * HD, HKV * HD, HC * HD
IN_COLS = 3 * WA + WB + 2 * WKV + 3 * WC
DFF = 5632
GRID_W = 64
ROWS = T // GRID_W
NA_ROWS, NA_COLS = 8, 16
WINDOW_B = 128
EPS = 1e-6
NEG = -1e30
ROPE_THETA = 10000.0
LANE = 128
VMEM_LIMIT = 56 * 1024 * 1024

ADAM_LR, ADAM_B1, ADAM_B2, ADAM_EPS, ADAM_WD, ADAM_STEP = 0.001, 0.9, 0.999, 1e-08, 0.01, 10

GROUPS = (("qa", WA, True, True), ("ka", WA, True, False), ("va", WA, False, False),
          ("qb", WB, True, True), ("kb", WKV, True, False), ("vb", WKV, False, False),
          ("qc", WC, False, True), ("kc", WC, False, False), ("vc", WC, False, False))


def _params(sem=None):
    return pltpu.CompilerParams(dimension_semantics=sem, vmem_limit_bytes=VMEM_LIMIT)


HBM_SPEC = pl.BlockSpec(memory_space=pltpu.HBM)
SEM_SPEC = pl.BlockSpec(memory_space=pltpu.SEMAPHORE)
DATAFLOW = pltpu.SideEffectType.DATAFLOW_SIDE_EFFECTING


ALL_PEERS = tuple((p >> 2 & 1, p >> 1 & 1, p & 1) for p in range(1, N_DEV))
OTHER_CHIPS = ((1, 0, 0), (0, 1, 0), (1, 1, 0))
NEAR_PEERS = ((0, 0, 1),) + OTHER_CHIPS


def _flip(x, y, c, f):
    return (1 - x if f[0] else x, 1 - y if f[1] else y, 1 - c if f[2] else c)


def _index(pos):
    return 4 * pos[0] + 2 * pos[1] + pos[2]


class _LocalCopy:
    def __init__(self, src, dst, sem):
        self.copy = pltpu.make_async_copy(src, dst, sem)

    def start(self):
        self.copy.start()

    def wait_send(self):
        self.copy.wait()

    def wait_recv(self):
        pass


def _descriptors(plan, bufs, send_sems, recv_sems):
    x, y, c = lax.axis_index("x"), lax.axis_index("y"), lax.axis_index("c")
    return [_LocalCopy(src, dst, send_sems[g].at[i]) if partner is None else
            pltpu.make_async_remote_copy(src_ref=src, dst_ref=dst, send_sem=send_sems[g].at[i],
                                         recv_sem=recv_sems[g].at[i], device_id=partner,
                                         device_id_type=pl.DeviceIdType.MESH)
            for g, copies in enumerate(plan(bufs, x, y, c)) for i, (src, dst, partner) in enumerate(copies)]


def _copy_start(name, bufs, plan, sizes):
    nb, ng = len(bufs), len(sizes)

    def body(*refs):
        for d in _descriptors(plan, refs[:nb], refs[nb:nb + ng], refs[nb + ng:nb + 2 * ng]):
            d.start()
        refs[2 * nb + 2 * ng][...] = jnp.zeros((8, LANE), F32)

    outs = pl.pallas_call(
        body, name=name,
        out_shape=[pltpu.SemaphoreType.DMA((s,)) for s in sizes] * 2 + [pltpu.HBM(b.shape, b.dtype) for b in bufs]
        + [jax.ShapeDtypeStruct((8, LANE), F32)],
        in_specs=[HBM_SPEC] * nb,
        out_specs=[SEM_SPEC] * (2 * ng) + [HBM_SPEC] * nb + [pl.BlockSpec(memory_space=pltpu.VMEM)],
        input_output_aliases={i: 2 * ng + i for i in range(nb)},
        compiler_params=pltpu.CompilerParams(has_side_effects=DATAFLOW),
    )(*[pltpu.with_memory_space_constraint(b, pltpu.HBM) for b in bufs])
    return outs[:ng], outs[ng:2 * ng], outs[2 * ng:2 * ng + nb], outs[2 * ng + nb]


def _copy_wait(name, bufs, send_sems, recv_sems, plan, after):
    nb, ng = len(bufs), len(send_sems)
    after = list(after) if isinstance(after, (list, tuple)) else [after]

    def body(*refs):
        for d in _descriptors(plan, refs[:nb], refs[nb:nb + ng], refs[nb + ng:nb + 2 * ng]):
            d.wait_send()
            d.wait_recv()

    return pl.pallas_call(
        body, name=name, out_shape=[pltpu.HBM(b.shape, b.dtype) for b in bufs],
        in_specs=[HBM_SPEC] * nb + [SEM_SPEC] * (2 * ng) + [pl.BlockSpec(memory_space=pl.ANY)] * len(after),
        out_specs=[HBM_SPEC] * nb, input_output_aliases={i: i for i in range(nb)},
        compiler_params=pltpu.CompilerParams(has_side_effects=DATAFLOW),
    )(*bufs, *send_sems, *recv_sems, *after)


def _lead_slot(ref, k):
    return ref.at[k]


def _col_slot(width):
    return lambda ref, k: ref.at[:, pl.ds(pl.multiple_of(k * width, LANE), width)]


def _gather_plan(peer_sets, slots):
    def plan(bufs, x, y, c):
        n = len(peer_sets)
        return [[(bufs[i], slots[i](bufs[n + i], _index((x, y, c))), _flip(x, y, c, f)) for f in peers]
                + [(bufs[i], slots[i](bufs[n + i], _index((x, y, c))), None)] for i, peers in enumerate(peer_sets)]
    return plan


def _forward_plan(slot):
    def plan(bufs, x, y, c):
        pieces = [slot(bufs[0], _index(_flip(x, y, c, f))) for f in OTHER_CHIPS]
        return [[(p, p, _flip(x, y, c, (0, 0, 1))) for p in pieces]]
    return plan


def _scatter_plan(slot):
    def plan(bufs, x, y, c):
        me = _index((x, y, c))
        peers = [_flip(x, y, c, f) for f in ALL_PEERS]
        return [[(slot(bufs[0], _index(p)), bufs[1].at[me], p) for p in peers]
                + [(slot(bufs[0], me), bufs[1].at[me], None)]]
    return plan


def _flat2(v):
    return v.reshape(-1, v.shape[-1])


def _matmul(name, kind, a, a_spec, b, b_spec, out_shape, out_spec, grid, res=None, res_spec=None, acc_shape=None):
    dims = {"nn": (((1,), (0,)), ((), ())), "nt": NT_DIMS, "nts": NT_DIMS, "tn": (((0,), (0,)), ((), ()))}[kind]
    nred = grid[-1]

    def body(*refs):
        if res is None:
            a_ref, b_ref, o_ref = refs[:3]
            r_ref = None
        else:
            a_ref, b_ref, r_ref, o_ref = refs[:4]
        if kind == "nts":
            n = b_ref.shape[-1]
            part = sum(lax.dot_general(a_ref[:, blk * n:(blk + 1) * n], b_ref[blk], dims, preferred_element_type=F32)
                       for blk in range(b_ref.shape[0]))
        else:
            part = lax.dot_general(_flat2(a_ref[...]), _flat2(b_ref[...]), dims, preferred_element_type=F32)

        def finish(total):
            if r_ref is not None:
                total = total + r_ref[...]
            o_ref[...] = total.reshape(o_ref.shape).astype(o_ref.dtype)

        if nred == 1:
            finish(part)
        else:
            acc_ref = refs[-1]
            k = pl.program_id(len(grid) - 1)

            @pl.when(k == 0)
            def _():
                acc_ref[...] = part

            @pl.when(jnp.logical_and(k > 0, k < nred - 1))
            def _():
                acc_ref[...] += part

            @pl.when(k == nred - 1)
            def _():
                finish(acc_ref[...] + part)

    ins, specs = [a, b], [a_spec, b_spec]
    if res is not None:
        ins.append(res)
        specs.append(res_spec)
    scratch = [] if nred == 1 else [pltpu.VMEM(acc_shape, F32)]
    return pl.pallas_call(
        body, name=name, grid=grid, in_specs=specs, out_specs=out_spec, out_shape=out_shape, scratch_shapes=scratch,
        compiler_params=_params(("parallel",) * (len(grid) - 1) + ("arbitrary",)),
    )(*ins)


def _nn_cols(name, a, w, tn):
    k, n = w.shape
    tm = 1024
    return _matmul(
        name, "nn", a, pl.BlockSpec((tm, k), lambda j, i, r: (i, 0)),
        w, pl.BlockSpec((k, tn), lambda j, i, r: (0, j)),
        jax.ShapeDtypeStruct((T, n), F32), pl.BlockSpec((tm, tn), lambda j, i, r: (i, j)),
        (n // tn, T // tm, 1))


def _nn_rows(name, a, wg, res, s, tn, tm):
    _, kj, n = wg.shape
    return _matmul(
        name, "nn", a, pl.BlockSpec((tm, s * kj), lambda j, i, r: (i, r)),
        wg, pl.BlockSpec((s, kj, tn), lambda j, i, r: (r, 0, j)),
        jax.ShapeDtypeStruct((T, n), F32), pl.BlockSpec((tm, tn), lambda j, i, r: (i, j)),
        (n // tn, T // tm, N_DEV // s), res=res, res_spec=pl.BlockSpec((tm, tn), lambda j, i, r: (i, j)),
        acc_shape=(tm, tn))


def _nt_cols(name, dc, dc_spec_of, w, nc):
    k, n = w.shape
    tm = tk = 1024
    return _matmul(
        name, "nt", dc, dc_spec_of(tm, nc),
        w, pl.BlockSpec((tk, nc), lambda kt, i, j: (kt, j)),
        jax.ShapeDtypeStruct((T, k), F32), pl.BlockSpec((tm, tk), lambda kt, i, j: (i, kt)),
        (k // tk, T // tm, n // nc), acc_shape=(tm, tk))


def _nt_rows(name, dc, wg, s, tm):
    _, kj, n = wg.shape
    return _matmul(
        name, "nt", dc, pl.BlockSpec((tm, n), lambda kt, i, r: (i, 0)),
        wg, pl.BlockSpec((s, kj, n), lambda kt, i, r: (kt, 0, 0)),
        jax.ShapeDtypeStruct((T, N_DEV * kj), F32), pl.BlockSpec((tm, s * kj), lambda kt, i, r: (i, kt)),
        (N_DEV // s, T // tm, 1))


def _tn_cols(name, a, dc, dc_spec_of, n, tn):
    k = a.shape[1]
    tk = 512
    return _matmul(
        name, "tn", a, pl.BlockSpec((T, tk), lambda j, kt, r: (0, kt)),
        dc, dc_spec_of(T, tn),
        jax.ShapeDtypeStruct((k, n), BF16), pl.BlockSpec((tk, tn), lambda j, kt, r: (kt, j)),
        (n // tn, k // tk, 1))


def _tn_rows(name, a, dc, kj, s, tn):
    n = dc.shape[1]
    return _matmul(
        name, "tn", a, pl.BlockSpec((T, s * kj), lambda kt, j, r: (0, kt)),
        dc, pl.BlockSpec((T, tn), lambda kt, j, r: (0, j)),
        jax.ShapeDtypeStruct((N_DEV, kj, n), BF16), pl.BlockSpec((s, kj, tn), lambda kt, j, r: (kt, 0, j)),
        (N_DEV // s, n // tn, 1))


TR = 256


def _rows(width):
    return pl.BlockSpec((TR, width), lambda i: (i, 0))


def _whole(shape):
    return pl.BlockSpec(shape, lambda i: (0,) * len(shape))


def _rmsnorm_fwd(name, x, g):
    def body(x_ref, g_ref, o_ref):
        xv = x_ref[...]
        r = lax.rsqrt(jnp.mean(xv * xv, axis=-1, keepdims=True) + EPS)
        o_ref[...] = ((xv * r) * g_ref[...]).astype(BF16)

    return pl.pallas_call(
        body, name=name, grid=(T // TR,), in_specs=[_rows(D), _whole((1, D))], out_specs=_rows(D),
        out_shape=jax.ShapeDtypeStruct((T, D), BF16), compiler_params=_params(("parallel",)),
    )(x, g)


def _rms_bwd_math(dy, xv, g):
    r = lax.rsqrt(jnp.mean(xv * xv, axis=-1, keepdims=True) + EPS)
    xhat = xv * r
    dxhat = dy * g
    dx = r * (dxhat - xhat * jnp.mean(dxhat * xhat, axis=-1, keepdims=True))
    return dx, dy * xhat


def _accumulate(ref, val):
    @pl.when(pl.program_id(0) == 0)
    def _():
        ref[...] = val

    @pl.when(pl.program_id(0) > 0)
    def _():
        ref[...] += val


def _rmsnorm_bwd(name, dy, x, g, res):
    def body(dy_ref, x_ref, g_ref, res_ref, dx_ref, dxb_ref, dg_ref):
        dx, dgr = _rms_bwd_math(dy_ref[...], x_ref[...], g_ref[...])
        tot = res_ref[...] + dx
        dx_ref[...] = tot
        dxb_ref[...] = tot.astype(BF16)
        _accumulate(dg_ref, jnp.sum(dgr, axis=0, keepdims=True))

    return pl.pallas_call(
        body, name=name, grid=(T // TR,), in_specs=[_rows(D), _rows(D), _whole((1, D)), _rows(D)],
        out_specs=[_rows(D), _rows(D), _whole((1, D))],
        out_shape=[jax.ShapeDtypeStruct((T, D), F32), jax.ShapeDtypeStruct((T, D), BF16),
                   jax.ShapeDtypeStruct((1, D), F32)],
        compiler_params=_params(("arbitrary",)),
    )(dy, x, g, res)


def _loss_head(x, g, target):
    def body(x_ref, g_ref, t_ref, loss_ref, dx_ref, dxb_ref, dg_ref):
        xv, gv = x_ref[...], g_ref[...]
        r = lax.rsqrt(jnp.mean(xv * xv, axis=-1, keepdims=True) + EPS)
        err = (xv * r) * gv - t_ref[...]
        part = 0.5 * jnp.sum(jnp.mean(err * err, axis=-1, keepdims=True))
        dx, dgr = _rms_bwd_math(err * (1.0 / D), xv, gv)
        dx_ref[...] = dx
        dxb_ref[...] = dx.astype(BF16)
        _accumulate(dg_ref, jnp.sum(dgr, axis=0, keepdims=True))
        _accumulate(loss_ref, jnp.full((8, LANE), part, F32))

    return pl.pallas_call(
        body, name="loss_head", grid=(T // TR,), in_specs=[_rows(D), _whole((1, D)), _rows(D)],
        out_specs=[_whole((8, LANE)), _rows(D), _rows(D), _whole((1, D))],
        out_shape=[jax.ShapeDtypeStruct((8, LANE), F32), jax.ShapeDtypeStruct((T, D), F32),
                   jax.ShapeDtypeStruct((T, D), BF16), jax.ShapeDtypeStruct((1, D), F32)],
        compiler_params=_params(("arbitrary",)),
    )(x, g, target)


MIX_OFFS = ((0, WA), (WA, WB), (WA + WB, WC))


def _mix_fwd(name, oa, ob, oc, gain):
    def body(oa_ref, ob_ref, oc_ref, g_ref, o_ref):
        for ref, (off, w) in zip((oa_ref, ob_ref, oc_ref), MIX_OFFS):
            o = ref[...]
            r = lax.rsqrt(jnp.mean(o * o, axis=-1, keepdims=True) + EPS)
            o_ref[:, off:off + w] = ((o * r) * g_ref[:, off:off + w]).astype(BF16)

    return pl.pallas_call(
        body, name=name, grid=(T // TR,), in_specs=[_rows(WA), _rows(WB), _rows(WC), _whole((1, D))],
        out_specs=_rows(D), out_shape=jax.ShapeDtypeStruct((T, D), BF16), compiler_params=_params(("parallel",)),
    )(oa, ob, oc, gain)


def _mix_bwd(name, dmixed, oa, ob, oc, gain):
    def body(dm_ref, oa_ref, ob_ref, oc_ref, g_ref, doa_ref, dob_ref, doc_ref, dg_ref):
        dgs = []
        for ref, dref, (off, w) in zip((oa_ref, ob_ref, oc_ref), (doa_ref, dob_ref, doc_ref), MIX_OFFS):
            dx, dgr = _rms_bwd_math(dm_ref[:, off:off + w], ref[...], g_ref[:, off:off + w])
            dref[...] = dx
            dgs.append(jnp.sum(dgr, axis=0, keepdims=True))
        _accumulate(dg_ref, jnp.concatenate(dgs, axis=1))

    return pl.pallas_call(
        body, name=name, grid=(T // TR,),
        in_specs=[_rows(D), _rows(WA), _rows(WB), _rows(WC), _whole((1, D))],
        out_specs=[_rows(WA), _rows(WB), _rows(WC), _whole((1, D))],
        out_shape=[jax.ShapeDtypeStruct((T, WA), F32), jax.ShapeDtypeStruct((T, WB), F32),
                   jax.ShapeDtypeStruct((T, WC), F32), jax.ShapeDtypeStruct((1, D), F32)],
        compiler_params=_params(("arbitrary",)),
    )(dmixed, oa, ob, oc, gain)


def _rope_tables():
    inv_freq = ROPE_THETA ** (-jnp.arange(0, HD, 2, dtype=F32) / HD)
    ang = jnp.arange(T, dtype=F32)[:, None] * inv_freq[None, :]
    cos, sin = jnp.cos(ang), jnp.sin(ang)
    cos2 = jnp.tile(jnp.concatenate([cos, cos], axis=1), (1, LANE // HD))
    sin2 = jnp.tile(jnp.concatenate([-sin, sin], axis=1), (1, LANE // HD))
    return cos2, sin2


def _rot_half(v):
    lane = lax.broadcasted_iota(jnp.int32, v.shape, 1)
    return jnp.where(lane % HD < HD // 2, pltpu.roll(v, LANE - HD // 2, 1), pltpu.roll(v, HD // 2, 1))


def _rope_fwd(name, proj, cos2, sin2):
    def body(p_ref, c_ref, s_ref, *outs):
        cv, sv = c_ref[...], s_ref[...]
        off = 0
        for o_ref, (_, w, rot, is_q) in zip(outs, GROUPS):
            for b in range(w // LANE):
                v = p_ref[:, off + b * LANE:off + (b + 1) * LANE]
                if rot:
                    v = v * cv + _rot_half(v) * sv
                if is_q:
                    v = v * (HD ** -0.5)
                o_ref[:, b * LANE:(b + 1) * LANE] = v.astype(BF16)
            off += w

    return pl.pallas_call(
        body, name=name, grid=(T // TR,), in_specs=[_rows(IN_COLS), _rows(LANE), _rows(LANE)],
        out_specs=[_rows(w) for _, w, _, _ in GROUPS],
        out_shape=[jax.ShapeDtypeStruct((T, w), BF16) for _, w, _, _ in GROUPS],
        compiler_params=_params(("parallel",)),
    )(proj, cos2, sin2)


def _rope_bwd(name, grads, cos2, sin2):
    def body(*refs):
        ins, (c_ref, s_ref, o_ref) = refs[:9], refs[9:]
        cv, sv = c_ref[...], s_ref[...]
        off = 0
        for d_ref, (_, w, rot, is_q) in zip(ins, GROUPS):
            for b in range(w // LANE):
                v = d_ref[:, b * LANE:(b + 1) * LANE]
                if is_q:
                    v = v * (HD ** -0.5)
                if rot:
                    v = v * cv + _rot_half(v * sv)
                o_ref[:, off + b * LANE:off + (b + 1) * LANE] = v.astype(BF16)
            off += w

    return pl.pallas_call(
        body, name=name, grid=(T // TR,), in_specs=[_rows(w) for _, w, _, _ in GROUPS] + [_rows(LANE), _rows(LANE)],
        out_specs=_rows(IN_COLS), out_shape=jax.ShapeDtypeStruct((T, IN_COLS), BF16),
        compiler_params=_params(("parallel",)),
    )(*grads, cos2, sin2)


NT_DIMS = (((1,), (1,)), ((), ()))
TN_DIMS = (((0,), (0,)), ((), ()))


def _scores(q, k, bias, valid):
    s = lax.dot_general(q, k, NT_DIMS, preferred_element_type=F32)
    if bias is not None:
        s = s + bias
    if valid is not None:
        s = jnp.where(valid, s, NEG)
    return s


def _heads_fwd(heads):
    scores = [_scores(h["q"], h["k"], h.get("bias"), h.get("valid")) for h in heads]
    soft = []
    for s, h in zip(scores, heads):
        m = jnp.max(s, axis=1, keepdims=True)
        e = jnp.exp(s - m)
        l = jnp.sum(e, axis=1, keepdims=True)
        if h.get("sink") is not None:
            l = l + jnp.exp(h["sink"] - m)
        soft.append((e.astype(BF16), l, m + jnp.log(l)))
    return [(jnp.dot(e, h["v"], preferred_element_type=F32) / l, lse) for (e, l, lse), h in zip(soft, heads)]


def _heads_bwd(heads):
    dobs = [h["do"].astype(BF16) for h in heads]
    scores = [_scores(h["q"], h["k"], h.get("bias"), h.get("valid")) for h in heads]
    dps = [lax.dot_general(dob, h["v"], NT_DIMS, preferred_element_type=F32) for dob, h in zip(dobs, heads)]
    mid = []
    for s, dp, h in zip(scores, dps, heads):
        p = jnp.exp(s - h["lse"])
        delta = jnp.sum(h["do"] * h["o"], axis=1, keepdims=True)
        ds = p * (dp - delta)
        dsink = None if h.get("sink") is None else -jnp.exp(h["sink"] - h["lse"]) * delta
        mid.append((p.astype(BF16), ds, dsink))
    out = []
    for (pb, ds, dsink), dob, h in zip(mid, dobs, heads):
        dsb = ds.astype(BF16)
        out.append((jnp.dot(dsb, h["k"], preferred_element_type=F32),
                    lax.dot_general(dsb, h["q"], TN_DIMS, preferred_element_type=F32),
                    lax.dot_general(pb, dob, TN_DIMS, preferred_element_type=F32), ds, dsink))
    return out


def _per_head(cols):
    return jnp.concatenate([jnp.broadcast_to(c, (c.shape[0], HD)) for c in cols], axis=1)


DILATIONS = ((128, 1), (512, 4), (2048, 16))


def _dilation_bias():
    def body(o_ref):
        t = pl.program_id(0) * TR + lax.broadcasted_iota(jnp.int32, (TR, T), 0)
        ad = jnp.abs(t - lax.broadcasted_iota(jnp.int32, (TR, T), 1))
        count = jnp.zeros((TR, T), jnp.int32)
        for window, r in DILATIONS:
            count += jnp.where(((ad & (r - 1)) == 0) & (ad <= window // 2), 1, 0)
        logs = jnp.where(count == 2, jnp.log(2.0), jnp.where(count == 3, jnp.log(3.0), 0.0)).astype(F32)
        o_ref[...] = jnp.where(count == 0, NEG, logs)

    return pl.pallas_call(
        body, name="dilation_bias", grid=(T // TR,), out_specs=_rows(T),
        out_shape=jax.ShapeDtypeStruct((T, T), F32), compiler_params=_params(("parallel",)),
    )()


BQ_A = 256


def _attn_a_fwd(name, qa, ka, va, bias):
    def body(q_ref, k_ref, v_ref, b_ref, o_ref, lse_ref):
        b = b_ref[...]
        outs = _heads_fwd([dict(q=q_ref[:, h * HD:(h + 1) * HD], k=k_ref[:, h * HD:(h + 1) * HD],
                                v=v_ref[:, h * HD:(h + 1) * HD], bias=b) for h in range(2)])
        o_ref[...] = jnp.concatenate([o for o, _ in outs], axis=1)
        lse_ref[...] = _per_head([lse for _, lse in outs])

    qs = pl.BlockSpec((BQ_A, LANE), lambda p, i: (i, p))
    ks = pl.BlockSpec((T, LANE), lambda p, i: (0, p))
    return pl.pallas_call(
        body, name=name, grid=(HA // 2, T // BQ_A),
        in_specs=[qs, ks, ks, pl.BlockSpec((BQ_A, T), lambda p, i: (i, 0))], out_specs=[qs, qs],
        out_shape=[jax.ShapeDtypeStruct((T, WA), F32)] * 2, compiler_params=_params(("parallel", "parallel")),
    )(qa, ka, va, bias)


def _attn_a_bwd(name, qa, ka, va, oa, lse, doa, bias):
    def body(q_ref, k_ref, v_ref, o_ref, lse_ref, do_ref, b_ref, dq_ref, dk_ref, dv_ref):
        b = b_ref[...]
        sls = [slice(h * HD, (h + 1) * HD) for h in range(2)]
        res = _heads_bwd([dict(q=q_ref[:, sl], k=k_ref[:, sl], v=v_ref[:, sl], o=o_ref[:, sl], do=do_ref[:, sl],
                               lse=lse_ref[:, sl.start:sl.start + 1], bias=b) for sl in sls])
        dq_ref[...] = jnp.concatenate([r[0] for r in res], axis=1)
        dk2, dv2 = jnp.concatenate([r[1] for r in res], axis=1), jnp.concatenate([r[2] for r in res], axis=1)

        @pl.when(pl.program_id(1) == 0)
        def _():
            dk_ref[...] = dk2
            dv_ref[...] = dv2

        @pl.when(pl.program_id(1) > 0)
        def _():
            dk_ref[...] += dk2
            dv_ref[...] += dv2

    qs = pl.BlockSpec((BQ_A, LANE), lambda p, i: (i, p))
    ks = pl.BlockSpec((T, LANE), lambda p, i: (0, p))
    return pl.pallas_call(
        body, name=name, grid=(HA // 2, T // BQ_A),
        in_specs=[qs, ks, ks, qs, qs, qs, pl.BlockSpec((BQ_A, T), lambda p, i: (i, 0))], out_specs=[qs, ks, ks],
        out_shape=[jax.ShapeDtypeStruct((T, WA), F32)] * 3, compiler_params=_params(("parallel", "arbitrary")),
    )(qa, ka, va, oa, lse, doa, bias)


BQ_B = 128
SPAN_B = BQ_B + 2 * WINDOW_B


def _window_b(i):
    start = pl.multiple_of(jnp.clip(i * BQ_B - WINDOW_B, 0, T - SPAN_B), BQ_B)
    qpos = i * BQ_B + lax.broadcasted_iota(jnp.int32, (BQ_B, SPAN_B), 0)
    kpos = start + lax.broadcasted_iota(jnp.int32, (BQ_B, SPAN_B), 1)
    return start, jnp.abs(qpos - kpos) <= WINDOW_B


GROUP_B = HB // HKV


def _stack_group(ref, g):
    return jnp.concatenate([ref[:, h * HD:(h + 1) * HD] for h in range(g * GROUP_B, (g + 1) * GROUP_B)], axis=0)


def _sink_column(sink_ref, g):
    return jnp.concatenate([jnp.full((BQ_B, 1), sink_ref[h], F32) for h in range(g * GROUP_B, (g + 1) * GROUP_B)],
                           axis=0)


def _unstack(stacked):
    return [s[j * BQ_B:(j + 1) * BQ_B] for s in stacked for j in range(GROUP_B)]


def _attn_b_fwd(name, qb, kb, vb, sink):
    def body(sink_ref, q_ref, k_ref, v_ref, o_ref, lse_ref):
        start, valid = _window_b(pl.program_id(0))
        valid = jnp.concatenate([valid] * GROUP_B, axis=0)
        kw, vw = k_ref[pl.ds(start, SPAN_B), :], v_ref[pl.ds(start, SPAN_B), :]
        outs = _heads_fwd([dict(q=_stack_group(q_ref, g), k=kw[:, g * HD:(g + 1) * HD], v=vw[:, g * HD:(g + 1) * HD],
                                valid=valid, sink=_sink_column(sink_ref, g)) for g in range(HKV)])
        o_ref[...] = jnp.concatenate(_unstack([o for o, _ in outs]), axis=1)
        lse_ref[...] = _per_head(_unstack([lse for _, lse in outs]))

    qs = pl.BlockSpec((BQ_B, WB), lambda i: (i, 0))
    return pl.pallas_call(
        body, name=name, grid=(T // BQ_B,),
        in_specs=[pl.BlockSpec(memory_space=pltpu.SMEM), qs, _whole((T, WKV)), _whole((T, WKV))],
        out_specs=[qs, qs],
        out_shape=[jax.ShapeDtypeStruct((T, WB), F32)] * 2, compiler_params=_params(("parallel",)),
    )(sink, qb, kb, vb)


def _attn_b_bwd(name, qb, kb, vb, ob, lse, dob, sink):
    def body(sink_ref, q_ref, k_ref, v_ref, o_ref, lse_ref, do_ref, dq_ref, dk_ref, dv_ref, dsink_ref):
        i = pl.program_id(0)
        start, valid = _window_b(i)
        valid = jnp.concatenate([valid] * GROUP_B, axis=0)
        kw, vw = k_ref[pl.ds(start, SPAN_B), :], v_ref[pl.ds(start, SPAN_B), :]
        res = _heads_bwd([dict(q=_stack_group(q_ref, g), k=kw[:, g * HD:(g + 1) * HD], v=vw[:, g * HD:(g + 1) * HD],
                               o=_stack_group(o_ref, g), do=_stack_group(do_ref, g),
                               lse=jnp.concatenate([lse_ref[:, h * HD:h * HD + 1]
                                                    for h in range(g * GROUP_B, (g + 1) * GROUP_B)], axis=0),
                               valid=valid, sink=_sink_column(sink_ref, g)) for g in range(HKV)])
        dks, dvs = [r[1] for r in res], [r[2] for r in res]
        lane = lax.broadcasted_iota(jnp.int32, (1, LANE), 1)
        dsink = jnp.zeros((1, LANE), F32)
        for h, rows in enumerate(_unstack([r[4] for r in res])):
            dsink += jnp.where(lane == h, jnp.sum(rows), 0.0)
        dq_ref[...] = jnp.concatenate(_unstack([r[0] for r in res]), axis=1)

        @pl.when(i == 0)
        def _():
            dk_ref[...] = jnp.zeros_like(dk_ref)
            dv_ref[...] = jnp.zeros_like(dv_ref)
            dsink_ref[...] = jnp.zeros_like(dsink_ref)

        dk_ref[pl.ds(start, SPAN_B), :] += jnp.concatenate(dks, axis=1)
        dv_ref[pl.ds(start, SPAN_B), :] += jnp.concatenate(dvs, axis=1)
        dsink_ref[...] += dsink

    qs = pl.BlockSpec((BQ_B, WB), lambda i: (i, 0))
    return pl.pallas_call(
        body, name=name, grid=(T // BQ_B,),
        in_specs=[pl.BlockSpec(memory_space=pltpu.SMEM), qs, _whole((T, WKV)), _whole((T, WKV)), qs, qs, qs],
        out_specs=[qs, _whole((T, WKV)), _whole((T, WKV)), _whole((1, LANE))],
        out_shape=[jax.ShapeDtypeStruct((T, WB), F32), jax.ShapeDtypeStruct((T, WKV), F32),
                   jax.ShapeDtypeStruct((T, WKV), F32), jax.ShapeDtypeStruct((1, LANE), F32)],
        compiler_params=_params(("arbitrary",)),
    )(sink, qb, kb, vb, ob, lse, dob)


SPAN_C = NA_ROWS * GRID_W


def _row_start(r):
    return jnp.clip(r - NA_ROWS // 2, 0, ROWS - NA_ROWS)


def _off_index(r):
    return _row_start(r) - r + (NA_ROWS - 1)


N_TAB = 16
RPS = 4


def _rpb_tables(name, rpb):
    circ = jnp.concatenate([rpb[..., NA_COLS - 1:], jnp.zeros(rpb.shape[:2] + (LANE - (2 * NA_COLS - 1),), F32),
                            rpb[..., :NA_COLS - 1]], axis=-1)
    circ = jnp.pad(circ, ((0, 0), (0, N_TAB + 1 - circ.shape[1]), (0, 0)))

    def body(w_ref, o_ref):
        c = lax.broadcasted_iota(jnp.int32, (GRID_W, LANE), 0)
        lane = lax.broadcasted_iota(jnp.int32, (GRID_W, LANE), 1)
        cs = jnp.clip(c - NA_COLS // 2, 0, GRID_W - NA_COLS)
        valid = (lane % GRID_W >= cs) & (lane % GRID_W < cs + NA_COLS)
        toep = [pltpu.roll(jnp.broadcast_to(w_ref[a:a + 1, :], (GRID_W, LANE)), 0, 1, stride=1, stride_axis=0)
                for a in range(N_TAB + 1)]
        for a in range(N_TAB):
            pair = jnp.where(lane < GRID_W, toep[a], pltpu.roll(toep[a + 1], GRID_W, 1))
            o_ref[a] = jnp.where(valid, pair, NEG)

    return pl.pallas_call(
        body, name=name, grid=(HC,),
        in_specs=[pl.BlockSpec((None, N_TAB + 1, LANE), lambda h: (h, 0, 0))],
        out_specs=pl.BlockSpec((None, N_TAB, GRID_W, LANE), lambda h: (h, 0, 0, 0)),
        out_shape=jax.ShapeDtypeStruct((HC, N_TAB, GRID_W, LANE), F32), compiler_params=_params(("parallel",)),
    )(circ)


def _bias_c(t_ref, h, d):
    return jnp.concatenate([t_ref[h, d + k] for k in range(0, NA_ROWS, 2)], axis=1)


def _attn_c_fwd(name, qc, kc, vc, tables):
    def body(q_ref, k_ref, v_ref, t_ref, o_ref, lse_ref):
        heads = []
        for rr in range(RPS):
            r = pl.program_id(1) * RPS + rr
            rows = slice(rr * GRID_W, (rr + 1) * GRID_W)
            start = pl.multiple_of(_row_start(r) * GRID_W, GRID_W)
            kw, vw = k_ref[pl.ds(start, SPAN_C), :], v_ref[pl.ds(start, SPAN_C), :]
            heads += [dict(q=q_ref[rows, h * HD:(h + 1) * HD], k=kw[:, h * HD:(h + 1) * HD], v=vw[:, h * HD:(h + 1) * HD],
                           bias=_bias_c(t_ref, h, _off_index(r))) for h in range(2)]
        outs = _heads_fwd(heads)
        for rr in range(RPS):
            rows = slice(rr * GRID_W, (rr + 1) * GRID_W)
            o_ref[rows, :] = jnp.concatenate([o for o, _ in outs[2 * rr:2 * rr + 2]], axis=1)
            lse_ref[rows, :] = _per_head([lse for _, lse in outs[2 * rr:2 * rr + 2]])

    qs = pl.BlockSpec((RPS * GRID_W, LANE), lambda p, r: (r, p))
    ks = pl.BlockSpec((T, LANE), lambda p, r: (0, p))
    ts = pl.BlockSpec((2, N_TAB, GRID_W, LANE), lambda p, r: (p, 0, 0, 0))
    return pl.pallas_call(
        body, name=name, grid=(HC // 2, ROWS // RPS), in_specs=[qs, ks, ks, ts], out_specs=[qs, qs],
        out_shape=[jax.ShapeDtypeStruct((T, WC), F32)] * 2, compiler_params=_params(("parallel", "parallel")),
    )(qc, kc, vc, tables)


def _attn_c_bwd(name, qc, kc, vc, oc, lse, doc, tables):
    def body(q_ref, k_ref, v_ref, o_ref, lse_ref, do_ref, t_ref, dq_ref, dk_ref, dv_ref, dt_ref):
        @pl.when(pl.program_id(1) == 0)
        def _():
            dk_ref[...] = jnp.zeros_like(dk_ref)
            dv_ref[...] = jnp.zeros_like(dv_ref)
            dt_ref[...] = jnp.zeros_like(dt_ref)

        heads, where = [], []
        for rr in range(RPS):
            r = pl.program_id(1) * RPS + rr
            rows = slice(rr * GRID_W, (rr + 1) * GRID_W)
            d = _off_index(r)
            start = pl.multiple_of(_row_start(r) * GRID_W, GRID_W)
            kw, vw = k_ref[pl.ds(start, SPAN_C), :], v_ref[pl.ds(start, SPAN_C), :]
            where.append((rows, d, start))
            for h in range(2):
                sl = slice(h * HD, (h + 1) * HD)
                heads.append(dict(q=q_ref[rows, sl], k=kw[:, sl], v=vw[:, sl], o=o_ref[rows, sl], do=do_ref[rows, sl],
                                  lse=lse_ref[rows, h * HD:h * HD + 1], bias=_bias_c(t_ref, h, d)))
        res = _heads_bwd(heads)
        for rr, (rows, d, start) in enumerate(where):
            pair = res[2 * rr:2 * rr + 2]
            for h in range(2):
                for k in range(0, NA_ROWS, 2):
                    dt_ref[h, d + k] += pair[h][3][:, k * GRID_W:(k + 2) * GRID_W]
            dq_ref[rows, :] = jnp.concatenate([p[0] for p in pair], axis=1)
            dk_ref[pl.ds(start, SPAN_C), :] += jnp.concatenate([p[1] for p in pair], axis=1)
            dv_ref[pl.ds(start, SPAN_C), :] += jnp.concatenate([p[2] for p in pair], axis=1)

    qs = pl.BlockSpec((RPS * GRID_W, LANE), lambda p, r: (r, p))
    ks = pl.BlockSpec((T, LANE), lambda p, r: (0, p))
    ts = pl.BlockSpec((2, N_TAB, GRID_W, LANE), lambda p, r: (p, 0, 0, 0))
    return pl.pallas_call(
        body, name=name, grid=(HC // 2, ROWS // RPS), in_specs=[qs, ks, ks, qs, qs, qs, ts],
        out_specs=[qs, ks, ks, ts],
        out_shape=[jax.ShapeDtypeStruct((T, WC), F32)] * 3 + [jax.ShapeDtypeStruct((HC, N_TAB, GRID_W, LANE), F32)],
        compiler_params=_params(("parallel", "arbitrary")),
    )(qc, kc, vc, oc, lse, doc, tables)


def _split3(v):
    hi = v.astype(BF16)
    r1 = v - hi.astype(F32)
    mid = r1.astype(BF16)
    lo = (r1 - mid.astype(F32)).astype(BF16)
    return hi, mid, lo


def _rpb_reduce(name, dtables):
    x = dtables.reshape(HC, N_TAB, GRID_W * LANE)
    c = jnp.arange(GRID_W)[:, None]
    lane = jnp.arange(LANE)[None, :]
    col = (lane // GRID_W) * LANE + jnp.clip(lane % GRID_W - c + (NA_COLS - 1), 0, 2 * NA_COLS - 2)
    col_onehot = (col.reshape(-1)[:, None] == jnp.arange(2 * LANE)[None, :]).astype(BF16)
    a2 = jnp.arange(N_TAB)[None, :]
    row_onehot = jnp.concatenate([(jnp.arange(16)[:, None] == a2 + u) & (a2 < 2 * NA_ROWS - 2) for u in range(2)],
                                 axis=1).astype(BF16)

    def body(x_ref, e_ref, f_ref, o_ref):
        y = sum(jnp.dot(part, e_ref[...], preferred_element_type=F32) for part in _split3(x_ref[...]))
        z = jnp.concatenate([y[:, :LANE], y[:, LANE:]], axis=0)
        o_ref[...] = sum(jnp.dot(f_ref[...], part, preferred_element_type=F32) for part in _split3(z))

    out = pl.pallas_call(
        body, name=name, grid=(HC,),
        in_specs=[pl.BlockSpec((None, N_TAB, GRID_W * LANE), lambda h: (h, 0, 0)),
                  _whole((GRID_W * LANE, 2 * LANE)), _whole((16, 2 * N_TAB))],
        out_specs=pl.BlockSpec((None, 16, LANE), lambda h: (h, 0, 0)),
        out_shape=jax.ShapeDtypeStruct((HC, 16, LANE), F32), compiler_params=_params(("parallel",)),
    )(x, col_onehot, row_onehot)
    return out[:, :2 * NA_ROWS - 1, :2 * NA_COLS - 1]


TC = 128
NCB = DFF // TC
CHUNK = 128
MARGIN = 8


def _shift_down(v, rows):
    return jnp.where(rows == 0, 0.0, pltpu.roll(v, 1, 0))


def _shift_up(v, rows):
    return jnp.where(rows == T - 1, 0.0, pltpu.roll(v, T - 1, 0))


def _conv(v, w, b, rows):
    return _shift_down(v, rows) * w[0:1] + v * w[1:2] + _shift_up(v, rows) * w[2:3] + b


def _ffn_specs():
    gate = lambda shape: pl.BlockSpec(shape, lambda j: (0, j))
    val = lambda shape: pl.BlockSpec(shape, lambda j: (0, j + NCB))
    return [gate((T, TC)), val((T, TC)), gate((3, TC)), val((3, TC)), gate((1, TC)), val((1, TC))]


def _ffn_mid_fwd(name, up, conv_w, conv_b):
    def body(xg_ref, xv_ref, wg_ref, wv_ref, bg_ref, bv_ref, o_ref):
        rows = lax.broadcasted_iota(jnp.int32, (T, TC), 0)
        ug = _conv(xg_ref[...], wg_ref[...], bg_ref[...], rows)
        uv = _conv(xv_ref[...], wv_ref[...], bv_ref[...], rows)
        o_ref[...] = (ug * jax.nn.sigmoid(ug) * uv).astype(BF16)

    return pl.pallas_call(
        body, name=name, grid=(NCB,), in_specs=_ffn_specs(), out_specs=pl.BlockSpec((T, TC), lambda j: (0, j)),
        out_shape=jax.ShapeDtypeStruct((T, DFF), BF16), compiler_params=_params(("parallel",)),
    )(up, up, conv_w, conv_w, conv_b, conv_b)


def _ffn_mid_bwd(name, dact, up, conv_w, conv_b):
    window = CHUNK + 2 * MARGIN
    centre = slice(MARGIN, MARGIN + CHUNK)

    def shifted(v):
        return pltpu.roll(v, 1, 0), pltpu.roll(v, window - 1, 0)

    def fold(v):
        return jnp.sum(v[centre].reshape(CHUNK // 8, 8, TC), axis=0)

    def body(da_ref, xg_ref, xv_ref, wg_ref, wv_ref, bg_ref, bv_ref, dx_ref, dw_ref, db_ref, dap, xgp, xvp):
        for src, pad in ((da_ref, dap), (xg_ref, xgp), (xv_ref, xvp)):
            pad[0:MARGIN, :] = jnp.zeros((MARGIN, TC), F32)
            pad[MARGIN:MARGIN + T, :] = src[...]
            pad[MARGIN + T:, :] = jnp.zeros((MARGIN, TC), F32)
        wg, wv, bg, bv = wg_ref[...], wv_ref[...], bg_ref[...], bv_ref[...]

        def chunk(c, sums):
            r0 = pl.multiple_of(c * CHUNK, CHUNK)
            da, xg, xv = dap[pl.ds(r0, window), :], xgp[pl.ds(r0, window), :], xvp[pl.ds(r0, window), :]
            xg_prev, xg_next = shifted(xg)
            xv_prev, xv_next = shifted(xv)
            ug = xg_prev * wg[0:1] + xg * wg[1:2] + xg_next * wg[2:3] + bg
            uv = xv_prev * wv[0:1] + xv * wv[1:2] + xv_next * wv[2:3] + bv
            sg = jax.nn.sigmoid(ug)
            dug = da * uv * (sg * (1.0 + ug * (1.0 - sg)))
            duv = da * (ug * sg)
            out = []
            for half, (x_prev, x, x_next, w, du) in enumerate(((xg_prev, xg, xg_next, wg, dug),
                                                               (xv_prev, xv, xv_next, wv, duv))):
                du_prev, du_next = shifted(du)
                dx = du_next * w[0:1] + du * w[1:2] + du_prev * w[2:3]
                dx_ref[half, pl.ds(r0, CHUNK), :] = dx[centre].astype(BF16)
                out += [fold(x_prev * du), fold(x * du), fold(x_next * du), fold(du)]
            return tuple(s + o for s, o in zip(sums, out))

        sums = lax.fori_loop(0, T // CHUNK, chunk, tuple(jnp.zeros((8, TC), F32) for _ in range(8)))
        rows = [jnp.sum(s, axis=0, keepdims=True) for s in sums]
        for half in range(2):
            dw_ref[half] = jnp.concatenate(rows[4 * half:4 * half + 3], axis=0)
            db_ref[half] = rows[4 * half + 3]

    return pl.pallas_call(
        body, name=name, grid=(NCB,), in_specs=[pl.BlockSpec((T, TC), lambda j: (0, j))] + _ffn_specs(),
        out_specs=[pl.BlockSpec((2, T, TC), lambda j: (0, 0, j)), pl.BlockSpec((2, 3, TC), lambda j: (0, 0, j)),
                   pl.BlockSpec((2, 1, TC), lambda j: (0, 0, j))],
        out_shape=[jax.ShapeDtypeStruct((2, T, DFF), BF16), jax.ShapeDtypeStruct((2, 3, DFF), F32),
                   jax.ShapeDtypeStruct((2, 1, DFF), F32)],
        scratch_shapes=[pltpu.VMEM((T + 2 * MARGIN, TC), F32)] * 3,
        compiler_params=_params(("parallel",)),
    )(dact, up, up, conv_w, conv_w, conv_b, conv_b)


def _dup_spec(tm, nj):
    per = DFF // nj
    return pl.BlockSpec((None, tm, nj), lambda a, b, j: (j // per, 0 if tm == T else b, j % per))


def _dup_spec_tn(tm, nj):
    per = DFF // nj
    return pl.BlockSpec((None, tm, nj), lambda j, kt, r: (j // per, 0, j % per))


def _adamw_math(w, g, m, v):
    m = ADAM_B1 * m + (1.0 - ADAM_B1) * g
    v = ADAM_B2 * v + (1.0 - ADAM_B2) * (g * g)
    m_hat = m / (1.0 - ADAM_B1 ** ADAM_STEP)
    v_hat = v / (1.0 - ADAM_B2 ** ADAM_STEP)
    delta = -ADAM_LR * (m_hat / (jnp.sqrt(v_hat) + ADAM_EPS) + ADAM_WD * w)
    return delta, m, v


ADAM_BLOCK = 256 * 1408


def _adamw_sharded(name, w, m, v, parts):
    _, r, c = w.shape
    tr = max(t for t in range(16, r + 1, 16) if r % t == 0 and t * c <= ADAM_BLOCK)

    def body(w_ref, m_ref, v_ref, p0_ref, p1_ref, g_ref, d_ref, nm_ref, nv_ref):
        def run(p_ref):
            g = p_ref[0].astype(F32)
            for k in range(1, N_DEV):
                g = g + p_ref[k].astype(F32)
            d, nm, nv = _adamw_math(w_ref[...], g, m_ref[...], v_ref[...])
            g_ref[...] = g
            d_ref[...] = d
            nm_ref[...] = nm
            nv_ref[...] = nv

        @pl.when(pl.program_id(0) == 0)
        def _():
            run(p0_ref)

        @pl.when(pl.program_id(0) == 1)
        def _():
            run(p1_ref)

    ws = pl.BlockSpec((None, tr, c), lambda l, i: (l, i, 0))
    p0 = pl.BlockSpec((N_DEV, tr, c), lambda l, i: (0, jnp.where(l == 0, i, r // tr - 1), 0))
    p1 = pl.BlockSpec((N_DEV, tr, c), lambda l, i: (0, jnp.where(l == 1, i, 0), 0))
    return pl.pallas_call(
        body, name=name, grid=(DEPTH, r // tr), in_specs=[ws, ws, ws, p0, p1], out_specs=[ws] * 4,
        out_shape=[jax.ShapeDtypeStruct(w.shape, F32)] * 4, compiler_params=_params(("arbitrary", "arbitrary")),
    )(w, m, v, *parts)


def _sum_devices(name, parts):
    r = parts.shape[1]

    def body(p_ref, o_ref):
        g = p_ref[0]
        for k in range(1, N_DEV):
            g = g + p_ref[k]
        o_ref[...] = g

    return pl.pallas_call(
        body, name=name, in_specs=[pl.BlockSpec((N_DEV, r, LANE), lambda: (0, 0, 0))],
        out_specs=pl.BlockSpec((r, LANE), lambda: (0, 0)), out_shape=jax.ShapeDtypeStruct((r, LANE), F32),
        compiler_params=_params(),
    )(parts)


def _adamw_small(name, ws, gs, ms, vs):
    n = len(ws)
    shapes = [w.shape for w in ws]
    ws, gs, ms, vs = ([a.reshape(1, -1) if a.ndim == 1 else a for a in arrs] for arrs in (ws, gs, ms, vs))
    specs = [pl.BlockSpec(memory_space=pltpu.VMEM)] * n

    def body(*refs):
        for i in range(n):
            w_ref, g_ref, m_ref, v_ref = (refs[k * n + i] for k in range(4))
            d, nm, nv = _adamw_math(w_ref[...], g_ref[...], m_ref[...], v_ref[...])
            refs[4 * n + i][...] = d
            refs[5 * n + i][...] = nm
            refs[6 * n + i][...] = nv

    outs = pl.pallas_call(
        body, name=name, in_specs=specs * 4, out_specs=specs * 3,
        out_shape=[jax.ShapeDtypeStruct(w.shape, F32) for w in ws] * 3, compiler_params=_params(),
    )(*ws, *gs, *ms, *vs)
    outs = [o.reshape(shapes[i % n]) for i, o in enumerate(outs)]
    return outs[:n], outs[n:2 * n], outs[2 * n:]


def _pack(arrays):
    flat = jnp.concatenate([a.reshape(-1) for a in arrays])
    pad = (-flat.shape[0]) % (8 * LANE)
    return jnp.pad(flat, (0, pad)).reshape(-1, LANE)


def _unpack(buf, shapes):
    flat, out, off = buf.reshape(-1), [], 0
    for s in shapes:
        n = 1
        for d in s:
            n *= d
        out.append(flat[off:off + n].reshape(s))
        off += n
    return out


def _local_step(x, target, small, weights, conv_w_full, hand_over, used):
    cos2, sin2 = _rope_tables()
    bias_a = _dilation_bias()
    tables = [_rpb_tables(f"rpb_tables_{l}", small["rpb_c"][l]) for l in range(DEPTH)]
    saved, carry = [], 0.0
    for l in range(DEPTH):
        g1, g2 = small["ln_attn"][l][None] + carry, small["ln_ffn"][l][None]
        gain, sink, cb = small["mix_gain"][l][None], small["sink_b"][l], small["conv_b"][l][None]
        cw = conv_w_full[l]
        bias = tables[l]
        h1 = _rmsnorm_fwd(f"norm_attn_{l}", x, g1)
        proj = _nn_cols(f"proj_in_{l}", h1, weights("w_in", l, [h1, cos2, sin2, bias_a] + tables if l == 0 else h1), 1024)
        zero = used(proj)
        qa, ka, va, qb, kb, vb, qc, kc, vc = _rope_fwd(f"rope_{l}", proj, cos2, sin2)
        oa, lse_a = _attn_a_fwd(f"attn_a_{l}", qa, ka, va, bias_a)
        ob, lse_b = _attn_b_fwd(f"attn_b_{l}", qb, kb, vb, sink + zero)
        oc, lse_c = _attn_c_fwd(f"attn_c_{l}", qc, kc, vc, bias)
        mixed = _mix_fwd(f"mix_{l}", oa, ob, oc, gain)
        x_mid = _nn_rows(f"proj_out_{l}", mixed, weights("w_out", l, mixed), x, 8, 512, T)
        h2 = _rmsnorm_fwd(f"norm_ffn_{l}", x_mid, g2 + used(x_mid))
        up = _nn_cols(f"ffn_up_{l}", h2, weights("w_up", l, h2), 1024)
        act = _ffn_mid_fwd(f"ffn_mid_{l}", up, cw, cb + used(up))
        x_out = _nn_rows(f"ffn_down_{l}", act, weights("w_down", l, act), x_mid, 4, 1024, 1024)
        carry = used(x_out)
        saved.append(dict(x=x, h1=h1, qkv=(qa, ka, va, qb, kb, vb, qc, kc, vc), o=(oa, ob, oc), lse=(lse_a, lse_b, lse_c), mixed=mixed,
                          x_mid=x_mid, h2=h2, up=up, act=act, g1=g1, g2=g2, gain=gain, sink=sink, cb=cb, cw=cw, bias=bias))
        x = x_out

    loss8, dx, dxb, d_ln_final = _loss_head(x, small["ln_final"][None], target)
    sgrads = [None] * DEPTH
    for l in reversed(range(DEPTH)):
        s = saved[l]
        qa, ka, va, qb, kb, vb, qc, kc, vc = s["qkv"]
        oa, ob, oc = s["o"]
        wg_in, wg_out = weights("w_in", l, None), weights("w_out", l, None)
        wg_up, wg_down = weights("w_up", l, None), weights("w_down", l, None)
        g_down = _tn_rows(f"wgrad_down_{l}", s["act"], dxb, wg_down.shape[1], 2, 512)
        zero = hand_over("w_down", l, g_down)
        dact = _nt_rows(f"dgrad_down_{l}", dxb, wg_down, 4, 512)
        dup, d_cw, d_cb = _ffn_mid_bwd(f"ffn_mid_bwd_{l}", dact, s["up"], s["cw"], s["cb"] + zero)
        g_up = _tn_cols(f"wgrad_up_{l}", s["h2"], dup, _dup_spec_tn, 2 * DFF, DFF // 2)
        zero = hand_over("w_up", l, g_up)
        dh2 = _nt_cols(f"dgrad_up_{l}", dup, _dup_spec, wg_up, DFF // 2)
        dx, dxb, d_g2 = _rmsnorm_bwd(f"norm_ffn_bwd_{l}", dh2, s["x_mid"], s["g2"] + zero, dx)
        g_out = _tn_rows(f"wgrad_out_{l}", s["mixed"], dxb, wg_out.shape[1], 2, D)
        zero = hand_over("w_out", l, g_out)
        dmixed = _nt_rows(f"dgrad_out_{l}", dxb, wg_out, 2, T)
        doa, dob, doc, d_gain = _mix_bwd(f"mix_bwd_{l}", dmixed, oa, ob, oc, s["gain"] + zero)
        lse_a, lse_b, lse_c = s["lse"]
        dqa, dka, dva = _attn_a_bwd(f"attn_a_bwd_{l}", qa, ka, va, oa, lse_a, doa, bias_a)
        dqb, dkb, dvb, d_sink = _attn_b_bwd(f"attn_b_bwd_{l}", qb, kb, vb, ob, lse_b, dob, s["sink"])
        dqc, dkc, dvc, d_bias = _attn_c_bwd(f"attn_c_bwd_{l}", qc, kc, vc, oc, lse_c, doc, s["bias"])
        d_rpb = _rpb_reduce(f"rpb_reduce_{l}", d_bias)
        dproj = _rope_bwd(f"rope_bwd_{l}", (dqa, dka, dva, dqb, dkb, dvb, dqc, dkc, dvc), cos2, sin2)
        g_in = _tn_cols(f"wgrad_in_{l}", s["h1"], dproj,
                        lambda tm, tn: pl.BlockSpec((tm, tn), lambda j, kt, r: (0, j)), IN_COLS, 1024)
        zero = hand_over("w_in", l, g_in)
        dh1 = _nt_cols(f"dgrad_in_{l}", dproj, lambda tm, nc: pl.BlockSpec((tm, nc), lambda kt, i, j: (i, j)), wg_in,
                       IN_COLS // 2)
        dx, dxb, d_g1 = _rmsnorm_bwd(f"norm_attn_bwd_{l}", dh1, s["x"], s["g1"] + zero, dx)
        sgrads[l] = dict(ln_attn=d_g1[0], sink_b=d_sink[0, :HB], rpb_c=d_rpb, mix_gain=d_gain[0], ln_ffn=d_g2[0],
                         conv_w=d_cw.transpose(1, 0, 2).reshape(3, 2 * DFF), conv_b=d_cb.reshape(2 * DFF))
    return loss8[0, 0], dx, d_ln_final[0], sgrads


SMALL_NAMES = ("ln_attn", "sink_b", "rpb_c", "mix_gain", "ln_ffn", "conv_b")


def kernel(x, ln_attn, w_in, sink_b, rpb_c, mix_gain, w_out, ln_ffn, w_up, conv_w, conv_b, w_down, ln_final, loss_target, m_ln_attn, m_w_in, m_sink_b, m_rpb_c, m_mix_gain, m_w_out, m_ln_ffn, m_w_up, m_conv_w, m_conv_b, m_w_down, m_ln_final, v_ln_attn, v_w_in, v_sink_b, v_rpb_c, v_mix_gain, v_w_out, v_ln_ffn, v_w_up, v_conv_w, v_conv_b, v_w_down, v_ln_final):
    me = 4 * lax.axis_index("x") + 2 * lax.axis_index("y") + lax.axis_index("c")
    small = dict(ln_attn=ln_attn, sink_b=sink_b, rpb_c=rpb_c, mix_gain=mix_gain, ln_ffn=ln_ffn, conv_b=conv_b,
                 ln_final=ln_final)

    names = ("w_in", "w_out", "w_up", "w_down")
    shards = dict(w_in=w_in, w_out=w_out, w_up=w_up, w_down=w_down)
    order = [(n, l) for l in range(DEPTH) for n in names]
    conv_key = ("conv_w", 0)
    started, arrived, forwarded, gathered = {}, {}, {}, {}

    def side_by_side(k):
        return k[0] in ("w_in", "w_up")

    def slot_of(k):
        return _col_slot(shards[k[0]].shape[2]) if side_by_side(k) else _lead_slot

    def begin(name, ks, zero):
        srcs = [_pack([conv_w]) + zero if k == conv_key else (shards[k[0]][k[1]] + zero).astype(BF16) for k in ks]
        lands = [lax.empty((s.shape[0], N_DEV * s.shape[1]) if side_by_side(k) else (N_DEV,) + s.shape, s.dtype)
                 for k, s in zip(ks, srcs)]
        peers = [ALL_PEERS if k == conv_key else NEAR_PEERS for k in ks]
        send, recv, bufs, tok = _copy_start(name, srcs + lands, _gather_plan(peers, [slot_of(k) for k in ks]),
                                            [len(p) + 1 for p in peers])
        for i, k in enumerate(ks):
            started[k] = (send[i], recv[i], bufs[i], bufs[len(ks) + i], peers[i])
        return tok

    token = begin("gather_start_first", order[:1], 0.0)
    token = begin("gather_start_rest", [conv_key] + order[1:], token[0, 0])

    def arrive(k, after):
        send, recv, src, land, peers = started[k]
        arrived[k] = _copy_wait(f"gather_{k[0]}_{k[1]}_arrived", [src, land], [send], [recv],
                                _gather_plan([peers], [slot_of(k)]), after)

    queue = list(order)

    def advance(after):
        if not queue:
            return 0.0
        k = queue.pop(0)
        arrive(k, after)
        forwarded[k] = _copy_start(f"gather_{k[0]}_{k[1]}_forward", [arrived[k][1]], _forward_plan(slot_of(k)),
                                   [len(OTHER_CHIPS)])
        return forwarded[k][3][0, 0]

    def weights(n, l, after):
        k = (n, l)
        if k not in gathered:
            if k not in forwarded:
                advance(after)
            send_b, recv_b, (land,), _ = forwarded[k]
            (gathered[k],) = _copy_wait(f"gather_{n}_{l}_done", [land], send_b, recv_b, _forward_plan(slot_of(k)),
                                        after)
        return gathered[k]

    pending = {}

    def hand_over(n, l, g):
        shard = shards[n].shape[1:]
        send, recv, bufs, tok = _copy_start(f"send_grad_{n}_{l}", [g, lax.empty((N_DEV,) + shard, g.dtype)],
                                            _scatter_plan(slot_of((n, l))), [len(ALL_PEERS) + 1])
        pending[(n, l)] = (send, recv, bufs)
        return tok[0, 0]

    def received(k, after):
        send, recv, bufs = pending[k]
        return _copy_wait(f"recv_grad_{k[0]}_{k[1]}", bufs, send, recv, _scatter_plan(slot_of(k)), after)[1]

    arrive(conv_key, token)
    cw_all = arrived[conv_key][1]
    nup = w_up.shape[2]
    cw_shards = cw_all.reshape(N_DEV, -1)[:, :DEPTH * 3 * nup].reshape(N_DEV, DEPTH, 3, nup)
    conv_w_full = cw_shards.transpose(1, 2, 0, 3).reshape(DEPTH, 3, N_DEV * nup)

    loss_local, dx, d_ln_final, sgrads = _local_step(
        x[0], loss_target[0], dict(small, ln_attn=ln_attn + token[0, 0]), weights, conv_w_full, hand_over, advance)

    stacked = [jnp.stack([sgrads[l][n] for l in range(DEPTH)]) for n in SMALL_NAMES + ("conv_w",)] + [d_ln_final]
    shapes = [a.shape for a in stacked]
    mine = _pack(stacked)
    send_s, recv_s, bufs_s, _ = _copy_start("gather_small_grads_start", [mine, lax.empty((N_DEV,) + mine.shape, F32)],
                                            _gather_plan([ALL_PEERS], [_lead_slot]), [len(ALL_PEERS) + 1])

    big, after = {}, dx
    moments = dict(w_in=(m_w_in, v_w_in), w_out=(m_w_out, v_w_out), w_up=(m_w_up, v_w_up), w_down=(m_w_down, v_w_down))
    for n in reversed(names):
        parts = (received((n, 0), after), received((n, 1), after))
        big[n] = _adamw_sharded(f"adamw_{n}", shards[n], *moments[n], parts)
        after = big[n][1]

    _, everyone = _copy_wait("gather_small_grads_done", bufs_s, send_s, recv_s,
                             _gather_plan([ALL_PEERS], [_lead_slot]), after)
    g_small = _unpack(_sum_devices("sum_small_grads", everyone), shapes)
    g = dict(zip(SMALL_NAMES + ("conv_w", "ln_final"), g_small))
    g["conv_w"] = lax.dynamic_slice_in_dim(g["conv_w"], me * nup, nup, axis=2)

    snames = SMALL_NAMES + ("conv_w", "ln_final")
    sw = dict(small, conv_w=conv_w)
    sm = dict(ln_attn=m_ln_attn, sink_b=m_sink_b, rpb_c=m_rpb_c, mix_gain=m_mix_gain, ln_ffn=m_ln_ffn,
              conv_b=m_conv_b, conv_w=m_conv_w, ln_final=m_ln_final)
    sv = dict(ln_attn=v_ln_attn, sink_b=v_sink_b, rpb_c=v_rpb_c, mix_gain=v_mix_gain, ln_ffn=v_ln_ffn,
              conv_b=v_conv_b, conv_w=v_conv_w, ln_final=v_ln_final)
    s_delta, s_m, s_v = (dict(zip(snames, out)) for out in _adamw_small(
        "adamw_small", [sw[n] for n in snames], [g[n] for n in snames], [sm[n] for n in snames],
        [sv[n] for n in snames]))

    loss = lax.psum(loss_local, ("x", "y", "c"))
    outputs = ("ln_attn", "w_in", "sink_b", "rpb_c", "mix_gain", "w_out", "ln_ffn", "w_up", "conv_w", "conv_b",
               "w_down", "ln_final")
    grads = [big[n][0] if n in big else g[n] for n in outputs]
    deltas = [big[n][1] if n in big else s_delta[n] for n in outputs]
    new_m = [big[n][2] if n in big else s_m[n] for n in outputs]
    new_v = [big[n][3] if n in big else s_v[n] for n in outputs]
    return (loss, dx[None], *grads, *deltas, *new_m, *new_v)
```

```python
import functools

import jax
import jax.numpy as jnp
from jax import lax
from jax.experimental import pallas as pl
from jax.experimental.pallas import tpu as pltpu

F32 = jnp.float32
BF16 = jnp.bfloat16

N_DEV = 8
T = 2048
D = 2048
DEPTH = 2
HD = 64
HA, HB, HKV, HC = 12, 10, 2, 10
WA, WB, WKV, WC = HA * HD, HB * HD, HKV * HD, HC * HD
IN_COLS = 3 * WA + WB + 2 * WKV + 3 * WC
DFF = 5632
GRID_W = 64
ROWS = T // GRID_W
NA_ROWS, NA_COLS = 8, 16
WINDOW_B = 128
EPS = 1e-6
NEG = -1e30
ROPE_THETA = 10000.0
LANE = 128
VMEM_LIMIT = 56 * 1024 * 1024

ADAM_LR, ADAM_B1, ADAM_B2, ADAM_EPS, ADAM_WD, ADAM_STEP = 0.001, 0.9, 0.999, 1e-08, 0.01, 10

GROUPS = (("qa", WA, True, True), ("ka", WA, True, False), ("va", WA, False, False),
          ("qb", WB, True, True), ("kb", WKV, True, False), ("vb", WKV, False, False),
          ("qc", WC, False, True), ("kc", WC, False, False), ("vc", WC, False, False))


def _params(sem=None):
    return pltpu.CompilerParams(dimension_semantics=sem, vmem_limit_bytes=VMEM_LIMIT)


HBM_SPEC = pl.BlockSpec(memory_space=pltpu.HBM)
SEM_SPEC = pl.BlockSpec(memory_space=pltpu.SEMAPHORE)
DATAFLOW = pltpu.SideEffectType.DATAFLOW_SIDE_EFFECTING


ALL_PEERS = tuple((p >> 2 & 1, p >> 1 & 1, p & 1) for p in range(1, N_DEV))
OTHER_CHIPS = ((1, 0, 0), (0, 1, 0), (1, 1, 0))
NEAR_PEERS = ((0, 0, 1),) + OTHER_CHIPS


def _flip(x, y, c, f):
    return (1 - x if f[0] else x, 1 - y if f[1] else y, 1 - c if f[2] else c)


def _index(pos):
    return 4 * pos[0] + 2 * pos[1] + pos[2]


class _LocalCopy:
    def __init__(self, src, dst, sem):
        self.copy = pltpu.make_async_copy(src, dst, sem)

    def start(self):
        self.copy.start()

    def wait_send(self):
        self.copy.wait()

    def wait_recv(self):
        pass


def _descriptors(plan, bufs, send_sems, recv_sems):
    x, y, c = lax.axis_index("x"), lax.axis_index("y"), lax.axis_index("c")
    return [_LocalCopy(src, dst, send_sems[g].at[i]) if partner is None else
            pltpu.make_async_remote_copy(src_ref=src, dst_ref=dst, send_sem=send_sems[g].at[i],
                                         recv_sem=recv_sems[g].at[i], device_id=partner,
                                         device_id_type=pl.DeviceIdType.MESH)
            for g, copies in enumerate(plan(bufs, x, y, c)) for i, (src, dst, partner) in enumerate(copies)]


def _copy_start(name, bufs, plan, sizes):
    nb, ng = len(bufs), len(sizes)

    def body(*refs):
        for d in _descriptors(plan, refs[:nb], refs[nb:nb + ng], refs[nb + ng:nb + 2 * ng]):
            d.start()
        refs[2 * nb + 2 * ng][...] = jnp.zeros((8, LANE), F32)

    outs = pl.pallas_call(
        body, name=name,
        out_shape=[pltpu.SemaphoreType.DMA((s,)) for s in sizes] * 2 + [pltpu.HBM(b.shape, b.dtype) for b in bufs]
        + [jax.ShapeDtypeStruct((8, LANE), F32)],
        in_specs=[HBM_SPEC] * nb,
        out_specs=[SEM_SPEC] * (2 * ng) + [HBM_SPEC] * nb + [pl.BlockSpec(memory_space=pltpu.VMEM)],
        input_output_aliases={i: 2 * ng + i for i in range(nb)},
        compiler_params=pltpu.CompilerParams(has_side_effects=DATAFLOW),
    )(*[pltpu.with_memory_space_constraint(b, pltpu.HBM) for b in bufs])
    return outs[:ng], outs[ng:2 * ng], outs[2 * ng:2 * ng + nb], outs[2 * ng + nb]


def _copy_wait(name, bufs, send_sems, recv_sems, plan, after):
    nb, ng = len(bufs), len(send_sems)
    after = list(after) if isinstance(after, (list, tuple)) else [after]

    def body(*refs):
        for d in _descriptors(plan, refs[:nb], refs[nb:nb + ng], refs[nb + ng:nb + 2 * ng]):
            d.wait_send()
            d.wait_recv()

    return pl.pallas_call(
        body, name=name, out_shape=[pltpu.HBM(b.shape, b.dtype) for b in bufs],
        in_specs=[HBM_SPEC] * nb + [SEM_SPEC] * (2 * ng) + [pl.BlockSpec(memory_space=pl.ANY)] * len(after),
        out_specs=[HBM_SPEC] * nb, input_output_aliases={i: i for i in range(nb)},
        compiler_params=pltpu.CompilerParams(has_side_effects=DATAFLOW),
    )(*bufs, *send_sems, *recv_sems, *after)


def _lead_slot(ref, k):
    return ref.at[k]


def _col_slot(width):
    return lambda ref, k: ref.at[:, pl.ds(pl.multiple_of(k * width, LANE), width)]


def _gather_plan(peer_sets, slots):
    def plan(bufs, x, y, c):
        n = len(peer_sets)
        return [[(bufs[i], slots[i](bufs[n + i], _index((x, y, c))), _flip(x, y, c, f)) for f in peers]
                + [(bufs[i], slots[i](bufs[n + i], _index((x, y, c))), None)] for i, peers in enumerate(peer_sets)]
    return plan


def _forward_plan(slot):
    def plan(bufs, x, y, c):
        pieces = [slot(bufs[0], _index(_flip(x, y, c, f))) for f in OTHER_CHIPS]
        return [[(p, p, _flip(x, y, c, (0, 0, 1))) for p in pieces]]
    return plan


def _scatter_plan(slot):
    def plan(bufs, x, y, c):
        me = _index((x, y, c))
        peers = [_flip(x, y, c, f) for f in ALL_PEERS]
        return [[(slot(bufs[0], _index(p)), bufs[1].at[me], p) for p in peers]
                + [(slot(bufs[0], me), bufs[1].at[me], None)]]
    return plan


def _flat2(v):
    return v.reshape(-1, v.shape[-1])


def _matmul(name, kind, a, a_spec, b, b_spec, out_shape, out_spec, grid, res=None, res_spec=None, acc_shape=None):
    dims = {"nn": (((1,), (0,)), ((), ())), "nt": NT_DIMS, "nts": NT_DIMS, "tn": (((0,), (0,)), ((), ()))}[kind]
    nred = grid[-1]

    def body(*refs):
        if res is None:
            a_ref, b_ref, o_ref = refs[:3]
            r_ref = None
        else:
            a_ref, b_ref, r_ref, o_ref = refs[:4]
        if kind == "nts":
            n = b_ref.shape[-1]
            part = sum(lax.dot_general(a_ref[:, blk * n:(blk + 1) * n], b_ref[blk], dims, preferred_element_type=F32)
                       for blk in range(b_ref.shape[0]))
        else:
            part = lax.dot_general(_flat2(a_ref[...]), _flat2(b_ref[...]), dims, preferred_element_type=F32)

        def finish(total):
            if r_ref is not None:
                total = total + r_ref[...]
            o_ref[...] = total.reshape(o_ref.shape).astype(o_ref.dtype)

        if nred == 1:
            finish(part)
        else:
            acc_ref = refs[-1]
            k = pl.program_id(len(grid) - 1)

            @pl.when(k == 0)
            def _():
                acc_ref[...] = part

            @pl.when(jnp.logical_and(k > 0, k < nred - 1))
            def _():
                acc_ref[...] += part

            @pl.when(k == nred - 1)
            def _():
                finish(acc_ref[...] + part)

    ins, specs = [a, b], [a_spec, b_spec]
    if res is not None:
        ins.append(res)
        specs.append(res_spec)
    scratch = [] if nred == 1 else [pltpu.VMEM(acc_shape, F32)]
    return pl.pallas_call(
        body, name=name, grid=grid, in_specs=specs, out_specs=out_spec, out_shape=out_shape, scratch_shapes=scratch,
        compiler_params=_params(("parallel",) * (len(grid) - 1) + ("arbitrary",)),
    )(*ins)


def _nn_rows(name, a, wg, res, s, tn, tm):
    _, kj, n = wg.shape
    return _matmul(
        name, "nn", a, pl.BlockSpec((tm, s * kj), lambda j, i, r: (i, r)),
        wg, pl.BlockSpec((s, kj, tn), lambda j, i, r: (r, 0, j)),
        jax.ShapeDtypeStruct((T, n), F32), pl.BlockSpec((tm, tn), lambda j, i, r: (i, j)),
        (n // tn, T // tm, N_DEV // s), res=res, res_spec=pl.BlockSpec((tm, tn), lambda j, i, r: (i, j)),
        acc_shape=(tm, tn))


def _nt_cols(name, dc, dc_spec_of, w, nc):
    k, n = w.shape
    tm = tk = 1024
    return _matmul(
        name, "nt", dc, dc_spec_of(tm, nc),
        w, pl.BlockSpec((tk, nc), lambda kt, i, j: (kt, j)),
        jax.ShapeDtypeStruct((T, k), F32), pl.BlockSpec((tm, tk), lambda kt, i, j: (i, kt)),
        (k // tk, T // tm, n // nc), acc_shape=(tm, tk))


def _nt_rows(name, dc, wg, s, tm):
    _, kj, n = wg.shape
    return _matmul(
        name, "nt", dc, pl.BlockSpec((tm, n), lambda kt, i, r: (i, 0)),
        wg, pl.BlockSpec((s, kj, n), lambda kt, i, r: (kt, 0, 0)),
        jax.ShapeDtypeStruct((T, N_DEV * kj), F32), pl.BlockSpec((tm, s * kj), lambda kt, i, r: (i, kt)),
        (N_DEV // s, T // tm, 1))


def _tn_cols(name, a, dc, dc_spec_of, n, tn):
    k = a.shape[1]
    tk = 512
    return _matmul(
        name, "tn", a, pl.BlockSpec((T, tk), lambda j, kt, r: (0, kt)),
        dc, dc_spec_of(T, tn),
        jax.ShapeDtypeStruct((k, n), BF16), pl.BlockSpec((tk, tn), lambda j, kt, r: (kt, j)),
        (n // tn, k // tk, 1))


def _tn_rows(name, a, dc, kj, s, tn):
    n = dc.shape[1]
    return _matmul(
        name, "tn", a, pl.BlockSpec((T, s * kj), lambda kt, j, r: (0, kt)),
        dc, pl.BlockSpec((T, tn), lambda kt, j, r: (0, j)),
        jax.ShapeDtypeStruct((N_DEV, kj, n), BF16), pl.BlockSpec((s, kj, tn), lambda kt, j, r: (kt, 0, j)),
        (N_DEV // s, n // tn, 1))


TR = 256


def _rows(width):
    return pl.BlockSpec((TR, width), lambda i: (i, 0))


def _whole(shape):
    return pl.BlockSpec(shape, lambda i: (0,) * len(shape))


def _rmsnorm_rows(x_ref, g_ref):
    xv = x_ref[...]
    r = lax.rsqrt(jnp.mean(xv * xv, axis=-1, keepdims=True) + EPS)
    return ((xv * r) * g_ref[...]).astype(BF16)


def _prologue_matmul(name, prologue, ins, widths, w, w_block, w_index, tn, res=None):
    tm = 1024
    n = w.shape[-1]
    ni = len(ins)

    def body(*refs):
        w_ref = refs[ni]
        r_ref = refs[ni + 1] if res is not None else None
        h_ref, o_ref, h_scr = refs[-3:]

        @pl.when(pl.program_id(1) == 0)
        def _():
            h = prologue(*refs[:ni])
            h_scr[...] = h
            h_ref[...] = h

        total = jnp.dot(h_scr[...], _flat2(w_ref[...]), preferred_element_type=F32)
        if r_ref is not None:
            total = total + r_ref[...]
        o_ref[...] = total

    tile = pl.BlockSpec((tm, tn), lambda i, j: (i, j))
    specs = [pl.BlockSpec((1, D), lambda i, j: (0, 0)) if wd is None else pl.BlockSpec((tm, wd), lambda i, j: (i, 0))
             for wd in widths]
    specs.append(pl.BlockSpec(w_block, lambda i, j: w_index(j)))
    operands = list(ins) + [w]
    if res is not None:
        specs.append(tile)
        operands.append(res)
    return pl.pallas_call(
        body, name=name, grid=(T // tm, n // tn), in_specs=specs,
        out_specs=[pl.BlockSpec((tm, D), lambda i, j: (i, 0)), tile],
        out_shape=[jax.ShapeDtypeStruct((T, D), BF16), jax.ShapeDtypeStruct((T, n), F32)],
        scratch_shapes=[pltpu.VMEM((tm, D), BF16)], compiler_params=_params(("parallel", "arbitrary")),
    )(*operands)


def _rms_bwd_math(dy, xv, g):
    r = lax.rsqrt(jnp.mean(xv * xv, axis=-1, keepdims=True) + EPS)
    xhat = xv * r
    dxhat = dy * g
    dx = r * (dxhat - xhat * jnp.mean(dxhat * xhat, axis=-1, keepdims=True))
    return dx, dy * xhat


def _accumulate(ref, val):
    @pl.when(pl.program_id(0) == 0)
    def _():
        ref[...] = val

    @pl.when(pl.program_id(0) > 0)
    def _():
        ref[...] += val


def _rmsnorm_bwd(name, dy, x, g, res):
    def body(dy_ref, x_ref, g_ref, res_ref, dx_ref, dxb_ref, dg_ref):
        dx, dgr = _rms_bwd_math(dy_ref[...], x_ref[...], g_ref[...])
        tot = res_ref[...] + dx
        dx_ref[...] = tot
        dxb_ref[...] = tot.astype(BF16)
        _accumulate(dg_ref, jnp.sum(dgr, axis=0, keepdims=True))

    return pl.pallas_call(
        body, name=name, grid=(T // TR,), in_specs=[_rows(D), _rows(D), _whole((1, D)), _rows(D)],
        out_specs=[_rows(D), _rows(D), _whole((1, D))],
        out_shape=[jax.ShapeDtypeStruct((T, D), F32), jax.ShapeDtypeStruct((T, D), BF16),
                   jax.ShapeDtypeStruct((1, D), F32)],
        compiler_params=_params(("arbitrary",)),
    )(dy, x, g, res)


def _loss_head(x, g, target):
    def body(x_ref, g_ref, t_ref, loss_ref, dx_ref, dxb_ref, dg_ref):
        xv, gv = x_ref[...], g_ref[...]
        r = lax.rsqrt(jnp.mean(xv * xv, axis=-1, keepdims=True) + EPS)
        err = (xv * r) * gv - t_ref[...]
        part = 0.5 * jnp.sum(jnp.mean(err * err, axis=-1, keepdims=True))
        dx, dgr = _rms_bwd_math(err * (1.0 / D), xv, gv)
        dx_ref[...] = dx
        dxb_ref[...] = dx.astype(BF16)
        _accumulate(dg_ref, jnp.sum(dgr, axis=0, keepdims=True))
        _accumulate(loss_ref, jnp.full((8, LANE), part, F32))

    return pl.pallas_call(
        body, name="loss_head", grid=(T // TR,), in_specs=[_rows(D), _whole((1, D)), _rows(D)],
        out_specs=[_whole((8, LANE)), _rows(D), _rows(D), _whole((1, D))],
        out_shape=[jax.ShapeDtypeStruct((8, LANE), F32), jax.ShapeDtypeStruct((T, D), F32),
                   jax.ShapeDtypeStruct((T, D), BF16), jax.ShapeDtypeStruct((1, D), F32)],
        compiler_params=_params(("arbitrary",)),
    )(x, g, target)


MIX_OFFS = ((0, WA), (WA, WB), (WA + WB, WC))


def _mix_rows(oa_ref, ob_ref, oc_ref, g_ref):
    parts = []
    for ref, (off, w) in zip((oa_ref, ob_ref, oc_ref), MIX_OFFS):
        o = ref[...]
        r = lax.rsqrt(jnp.mean(o * o, axis=-1, keepdims=True) + EPS)
        parts.append(((o * r) * g_ref[:, off:off + w]).astype(BF16))
    return jnp.concatenate(parts, axis=1)


def _mix_bwd(name, dmixed, oa, ob, oc, gain):
    def body(dm_ref, oa_ref, ob_ref, oc_ref, g_ref, doa_ref, dob_ref, doc_ref, dg_ref):
        dgs = []
        for ref, dref, (off, w) in zip((oa_ref, ob_ref, oc_ref), (doa_ref, dob_ref, doc_ref), MIX_OFFS):
            dx, dgr = _rms_bwd_math(dm_ref[:, off:off + w], ref[...], g_ref[:, off:off + w])
            dref[...] = dx
            dgs.append(jnp.sum(dgr, axis=0, keepdims=True))
        _accumulate(dg_ref, jnp.concatenate(dgs, axis=1))

    return pl.pallas_call(
        body, name=name, grid=(T // TR,),
        in_specs=[_rows(D), _rows(WA), _rows(WB), _rows(WC), _whole((1, D))],
        out_specs=[_rows(WA), _rows(WB), _rows(WC), _whole((1, D))],
        out_shape=[jax.ShapeDtypeStruct((T, WA), F32), jax.ShapeDtypeStruct((T, WB), F32),
                   jax.ShapeDtypeStruct((T, WC), F32), jax.ShapeDtypeStruct((1, D), F32)],
        compiler_params=_params(("arbitrary",)),
    )(dmixed, oa, ob, oc, gain)


def _rope_tables():
    inv_freq = ROPE_THETA ** (-jnp.arange(0, HD, 2, dtype=F32) / HD)
    ang = jnp.arange(T, dtype=F32)[:, None] * inv_freq[None, :]
    cos, sin = jnp.cos(ang), jnp.sin(ang)
    cos2 = jnp.tile(jnp.concatenate([cos, cos], axis=1), (1, LANE // HD))
    sin2 = jnp.tile(jnp.concatenate([-sin, sin], axis=1), (1, LANE // HD))
    return cos2, sin2


def _rot_half(v):
    lane = lax.broadcasted_iota(jnp.int32, v.shape, 1)
    return jnp.where(lane % HD < HD // 2, pltpu.roll(v, LANE - HD // 2, 1), pltpu.roll(v, HD // 2, 1))


def _rope_fwd(name, proj, cos2, sin2):
    def body(p_ref, c_ref, s_ref, *outs):
        cv, sv = c_ref[...], s_ref[...]
        off = 0
        for o_ref, (_, w, rot, is_q) in zip(outs, GROUPS):
            for b in range(w // LANE):
                v = p_ref[:, off + b * LANE:off + (b + 1) * LANE]
                if rot:
                    v = v * cv + _rot_half(v) * sv
                if is_q:
                    v = v * (HD ** -0.5)
                o_ref[:, b * LANE:(b + 1) * LANE] = v.astype(BF16)
            off += w

    return pl.pallas_call(
        body, name=name, grid=(T // TR,), in_specs=[_rows(IN_COLS), _rows(LANE), _rows(LANE)],
        out_specs=[_rows(w) for _, w, _, _ in GROUPS],
        out_shape=[jax.ShapeDtypeStruct((T, w), BF16) for _, w, _, _ in GROUPS],
        compiler_params=_params(("parallel",)),
    )(proj, cos2, sin2)


def _rope_bwd(name, grads, cos2, sin2):
    def body(*refs):
        ins, (c_ref, s_ref, o_ref) = refs[:9], refs[9:]
        cv, sv = c_ref[...], s_ref[...]
        off = 0
        for d_ref, (_, w, rot, is_q) in zip(ins, GROUPS):
            for b in range(w // LANE):
                v = d_ref[:, b * LANE:(b + 1) * LANE]
                if is_q:
                    v = v * (HD ** -0.5)
                if rot:
                    v = v * cv + _rot_half(v * sv)
                o_ref[:, off + b * LANE:off + (b + 1) * LANE] = v.astype(BF16)
            off += w

    return pl.pallas_call(
        body, name=name, grid=(T // TR,), in_specs=[_rows(w) for _, w, _, _ in GROUPS] + [_rows(LANE), _rows(LANE)],
        out_specs=_rows(IN_COLS), out_shape=jax.ShapeDtypeStruct((T, IN_COLS), BF16),
        compiler_params=_params(("parallel",)),
    )(*grads, cos2, sin2)


NT_DIMS = (((1,), (1,)), ((), ()))
TN_DIMS = (((0,), (0,)), ((), ()))


def _scores(q, k, bias, valid):
    s = lax.dot_general(q, k, NT_DIMS, preferred_element_type=F32)
    if bias is not None:
        s = s + bias
    if valid is not None:
        s = jnp.where(valid, s, NEG)
    return s


def _heads_fwd(heads):
    scores = [_scores(h["q"], h["k"], h.get("bias"), h.get("valid")) for h in heads]
    soft = []
    for s, h in zip(scores, heads):
        m = jnp.max(s, axis=1, keepdims=True)
        e = jnp.exp(s - m)
        l = jnp.sum(e, axis=1, keepdims=True)
        if h.get("sink") is not None:
            l = l + jnp.exp(h["sink"] - m)
        soft.append((e.astype(BF16), l, m + jnp.log(l)))
    return [(jnp.dot(e, h["v"], preferred_element_type=F32) / l, lse) for (e, l, lse), h in zip(soft, heads)]


def _heads_bwd(heads):
    dobs = [h["do"].astype(BF16) for h in heads]
    scores = [_scores(h["q"], h["k"], h.get("bias"), h.get("valid")) for h in heads]
    dps = [lax.dot_general(dob, h["v"], NT_DIMS, preferred_element_type=F32) for dob, h in zip(dobs, heads)]
    mid = []
    for s, dp, h in zip(scores, dps, heads):
        p = jnp.exp(s - h["lse"])
        delta = jnp.sum(h["do"] * h["o"], axis=1, keepdims=True)
        ds = p * (dp - delta)
        dsink = None if h.get("sink") is None else -jnp.exp(h["sink"] - h["lse"]) * delta
        mid.append((p.astype(BF16), ds, dsink))
    out = []
    for (pb, ds, dsink), dob, h in zip(mid, dobs, heads):
        dsb = ds.astype(BF16)
        out.append((jnp.dot(dsb, h["k"], preferred_element_type=F32),
                    lax.dot_general(dsb, h["q"], TN_DIMS, preferred_element_type=F32),
                    lax.dot_general(pb, dob, TN_DIMS, preferred_element_type=F32), ds, dsink))
    return out


def _per_head(cols):
    return jnp.concatenate([jnp.broadcast_to(c, (c.shape[0], HD)) for c in cols], axis=1)


DILATIONS = ((128, 1), (512, 4), (2048, 16))


def _dilation_bias():
    def body(o_ref):
        t = pl.program_id(0) * TR + lax.broadcasted_iota(jnp.int32, (TR, T), 0)
        ad = jnp.abs(t - lax.broadcasted_iota(jnp.int32, (TR, T), 1))
        count = jnp.zeros((TR, T), jnp.int32)
        for window, r in DILATIONS:
            count += jnp.where(((ad & (r - 1)) == 0) & (ad <= window // 2), 1, 0)
        logs = jnp.where(count == 2, jnp.log(2.0), jnp.where(count == 3, jnp.log(3.0), 0.0)).astype(F32)
        o_ref[...] = jnp.where(count == 0, NEG, logs)

    return pl.pallas_call(
        body, name="dilation_bias", grid=(T // TR,), out_specs=_rows(T),
        out_shape=jax.ShapeDtypeStruct((T, T), F32), compiler_params=_params(("parallel",)),
    )()


BQ_A = 256


def _attn_a_fwd(name, qa, ka, va, bias):
    def body(q_ref, k_ref, v_ref, b_ref, o_ref, lse_ref):
        b = b_ref[...]
        outs = _heads_fwd([dict(q=q_ref[:, h * HD:(h + 1) * HD], k=k_ref[:, h * HD:(h + 1) * HD],
                                v=v_ref[:, h * HD:(h + 1) * HD], bias=b) for h in range(2)])
        o_ref[...] = jnp.concatenate([o for o, _ in outs], axis=1)
        lse_ref[...] = _per_head([lse for _, lse in outs])

    qs = pl.BlockSpec((BQ_A, LANE), lambda p, i: (i, p))
    ks = pl.BlockSpec((T, LANE), lambda p, i: (0, p))
    return pl.pallas_call(
        body, name=name, grid=(HA // 2, T // BQ_A),
        in_specs=[qs, ks, ks, pl.BlockSpec((BQ_A, T), lambda p, i: (i, 0))], out_specs=[qs, qs],
        out_shape=[jax.ShapeDtypeStruct((T, WA), F32)] * 2, compiler_params=_params(("parallel", "parallel")),
    )(qa, ka, va, bias)


def _attn_a_bwd(name, qa, ka, va, oa, lse, doa, bias):
    def body(q_ref, k_ref, v_ref, o_ref, lse_ref, do_ref, b_ref, dq_ref, dk_ref, dv_ref):
        b = b_ref[...]
        sls = [slice(h * HD, (h + 1) * HD) for h in range(2)]
        res = _heads_bwd([dict(q=q_ref[:, sl], k=k_ref[:, sl], v=v_ref[:, sl], o=o_ref[:, sl], do=do_ref[:, sl],
                               lse=lse_ref[:, sl.start:sl.start + 1], bias=b) for sl in sls])
        dq_ref[...] = jnp.concatenate([r[0] for r in res], axis=1)
        dk2, dv2 = jnp.concatenate([r[1] for r in res], axis=1), jnp.concatenate([r[2] for r in res], axis=1)

        @pl.when(pl.program_id(1) == 0)
        def _():
            dk_ref[...] = dk2
            dv_ref[...] = dv2

        @pl.when(pl.program_id(1) > 0)
        def _():
            dk_ref[...] += dk2
            dv_ref[...] += dv2

    qs = pl.BlockSpec((BQ_A, LANE), lambda p, i: (i, p))
    ks = pl.BlockSpec((T, LANE), lambda p, i: (0, p))
    return pl.pallas_call(
        body, name=name, grid=(HA // 2, T // BQ_A),
        in_specs=[qs, ks, ks, qs, qs, qs, pl.BlockSpec((BQ_A, T), lambda p, i: (i, 0))], out_specs=[qs, ks, ks],
        out_shape=[jax.ShapeDtypeStruct((T, WA), F32)] * 3, compiler_params=_params(("parallel", "arbitrary")),
    )(qa, ka, va, oa, lse, doa, bias)


BQ_B = 128
SPAN_B = BQ_B + 2 * WINDOW_B


def _window_b(i):
    start = pl.multiple_of(jnp.clip(i * BQ_B - WINDOW_B, 0, T - SPAN_B), BQ_B)
    qpos = i * BQ_B + lax.broadcasted_iota(jnp.int32, (BQ_B, SPAN_B), 0)
    kpos = start + lax.broadcasted_iota(jnp.int32, (BQ_B, SPAN_B), 1)
    return start, jnp.abs(qpos - kpos) <= WINDOW_B


GROUP_B = HB // HKV


def _stack_group(ref, g):
    return jnp.concatenate([ref[:, h * HD:(h + 1) * HD] for h in range(g * GROUP_B, (g + 1) * GROUP_B)], axis=0)


def _sink_column(sink_ref, g):
    return jnp.concatenate([jnp.full((BQ_B, 1), sink_ref[h], F32) for h in range(g * GROUP_B, (g + 1) * GROUP_B)],
                           axis=0)


def _unstack(stacked):
    return [s[j * BQ_B:(j + 1) * BQ_B] for s in stacked for j in range(GROUP_B)]


def _attn_b_fwd(name, qb, kb, vb, sink):
    def body(sink_ref, q_ref, k_ref, v_ref, o_ref, lse_ref):
        start, valid = _window_b(pl.program_id(0))
        valid = jnp.concatenate([valid] * GROUP_B, axis=0)
        kw, vw = k_ref[pl.ds(start, SPAN_B), :], v_ref[pl.ds(start, SPAN_B), :]
        outs = _heads_fwd([dict(q=_stack_group(q_ref, g), k=kw[:, g * HD:(g + 1) * HD], v=vw[:, g * HD:(g + 1) * HD],
                                valid=valid, sink=_sink_column(sink_ref, g)) for g in range(HKV)])
        o_ref[...] = jnp.concatenate(_unstack([o for o, _ in outs]), axis=1)
        lse_ref[...] = _per_head(_unstack([lse for _, lse in outs]))

    qs = pl.BlockSpec((BQ_B, WB), lambda i: (i, 0))
    return pl.pallas_call(
        body, name=name, grid=(T // BQ_B,),
        in_specs=[pl.BlockSpec(memory_space=pltpu.SMEM), qs, _whole((T, WKV)), _whole((T, WKV))],
        out_specs=[qs, qs],
        out_shape=[jax.ShapeDtypeStruct((T, WB), F32)] * 2, compiler_params=_params(("parallel",)),
    )(sink, qb, kb, vb)


def _attn_b_bwd(name, qb, kb, vb, ob, lse, dob, sink):
    def body(sink_ref, q_ref, k_ref, v_ref, o_ref, lse_ref, do_ref, dq_ref, dk_ref, dv_ref, dsink_ref):
        i = pl.program_id(0)
        start, valid = _window_b(i)
        valid = jnp.concatenate([valid] * GROUP_B, axis=0)
        kw, vw = k_ref[pl.ds(start, SPAN_B), :], v_ref[pl.ds(start, SPAN_B), :]
        res = _heads_bwd([dict(q=_stack_group(q_ref, g), k=kw[:, g * HD:(g + 1) * HD], v=vw[:, g * HD:(g + 1) * HD],
                               o=_stack_group(o_ref, g), do=_stack_group(do_ref, g),
                               lse=jnp.concatenate([lse_ref[:, h * HD:h * HD + 1]
                                                    for h in range(g * GROUP_B, (g + 1) * GROUP_B)], axis=0),
                               valid=valid, sink=_sink_column(sink_ref, g)) for g in range(HKV)])
        dks, dvs = [r[1] for r in res], [r[2] for r in res]
        lane = lax.broadcasted_iota(jnp.int32, (1, LANE), 1)
        dsink = jnp.zeros((1, LANE), F32)
        for h, rows in enumerate(_unstack([r[4] for r in res])):
            dsink += jnp.where(lane == h, jnp.sum(rows), 0.0)
        dq_ref[...] = jnp.concatenate(_unstack([r[0] for r in res]), axis=1)

        @pl.when(i == 0)
        def _():
            dk_ref[...] = jnp.zeros_like(dk_ref)
            dv_ref[...] = jnp.zeros_like(dv_ref)
            dsink_ref[...] = jnp.zeros_like(dsink_ref)

        dk_ref[pl.ds(start, SPAN_B), :] += jnp.concatenate(dks, axis=1)
        dv_ref[pl.ds(start, SPAN_B), :] += jnp.concatenate(dvs, axis=1)
        dsink_ref[...] += dsink

    qs = pl.BlockSpec((BQ_B, WB), lambda i: (i, 0))
    return pl.pallas_call(
        body, name=name, grid=(T // BQ_B,),
        in_specs=[pl.BlockSpec(memory_space=pltpu.SMEM), qs, _whole((T, WKV)), _whole((T, WKV)), qs, qs, qs],
        out_specs=[qs, _whole((T, WKV)), _whole((T, WKV)), _whole((1, LANE))],
        out_shape=[jax.ShapeDtypeStruct((T, WB), F32), jax.ShapeDtypeStruct((T, WKV), F32),
                   jax.ShapeDtypeStruct((T, WKV), F32), jax.ShapeDtypeStruct((1, LANE), F32)],
        compiler_params=_params(("arbitrary",)),
    )(sink, qb, kb, vb, ob, lse, dob)


SPAN_C = NA_ROWS * GRID_W


def _row_start(r):
    return jnp.clip(r - NA_ROWS // 2, 0, ROWS - NA_ROWS)


def _off_index(r):
    return _row_start(r) - r + (NA_ROWS - 1)


N_TAB = 16
RPS = 4


def _rpb_tables(name, rpb):
    circ = jnp.concatenate([rpb[..., NA_COLS - 1:], jnp.zeros(rpb.shape[:2] + (LANE - (2 * NA_COLS - 1),), F32),
                            rpb[..., :NA_COLS - 1]], axis=-1)
    circ = jnp.pad(circ, ((0, 0), (0, N_TAB + 1 - circ.shape[1]), (0, 0)))

    def body(w_ref, o_ref):
        c = lax.broadcasted_iota(jnp.int32, (GRID_W, LANE), 0)
        lane = lax.broadcasted_iota(jnp.int32, (GRID_W, LANE), 1)
        cs = jnp.clip(c - NA_COLS // 2, 0, GRID_W - NA_COLS)
        valid = (lane % GRID_W >= cs) & (lane % GRID_W < cs + NA_COLS)
        toep = [pltpu.roll(jnp.broadcast_to(w_ref[a:a + 1, :], (GRID_W, LANE)), 0, 1, stride=1, stride_axis=0)
                for a in range(N_TAB + 1)]
        for a in range(N_TAB):
            pair = jnp.where(lane < GRID_W, toep[a], pltpu.roll(toep[a + 1], GRID_W, 1))
            o_ref[a] = jnp.where(valid, pair, NEG)

    return pl.pallas_call(
        body, name=name, grid=(HC,),
        in_specs=[pl.BlockSpec((None, N_TAB + 1, LANE), lambda h: (h, 0, 0))],
        out_specs=pl.BlockSpec((None, N_TAB, GRID_W, LANE), lambda h: (h, 0, 0, 0)),
        out_shape=jax.ShapeDtypeStruct((HC, N_TAB, GRID_W, LANE), F32), compiler_params=_params(("parallel",)),
    )(circ)


def _bias_c(t_ref, h, d):
    return jnp.concatenate([t_ref[h, d + k] for k in range(0, NA_ROWS, 2)], axis=1)


def _attn_c_fwd(name, qc, kc, vc, tables):
    def body(q_ref, k_ref, v_ref, t_ref, o_ref, lse_ref):
        heads = []
        for rr in range(RPS):
            r = pl.program_id(1) * RPS + rr
            rows = slice(rr * GRID_W, (rr + 1) * GRID_W)
            start = pl.multiple_of(_row_start(r) * GRID_W, GRID_W)
            kw, vw = k_ref[pl.ds(start, SPAN_C), :], v_ref[pl.ds(start, SPAN_C), :]
            heads += [dict(q=q_ref[rows, h * HD:(h + 1) * HD], k=kw[:, h * HD:(h + 1) * HD], v=vw[:, h * HD:(h + 1) * HD],
                           bias=_bias_c(t_ref, h, _off_index(r))) for h in range(2)]
        outs = _heads_fwd(heads)
        for rr in range(RPS):
            rows = slice(rr * GRID_W, (rr + 1) * GRID_W)
            o_ref[rows, :] = jnp.concatenate([o for o, _ in outs[2 * rr:2 * rr + 2]], axis=1)
            lse_ref[rows, :] = _per_head([lse for _, lse in outs[2 * rr:2 * rr + 2]])

    qs = pl.BlockSpec((RPS * GRID_W, LANE), lambda p, r: (r, p))
    ks = pl.BlockSpec((T, LANE), lambda p, r: (0, p))
    ts = pl.BlockSpec((2, N_TAB, GRID_W, LANE), lambda p, r: (p, 0, 0, 0))
    return pl.pallas_call(
        body, name=name, grid=(HC // 2, ROWS // RPS), in_specs=[qs, ks, ks, ts], out_specs=[qs, qs],
        out_shape=[jax.ShapeDtypeStruct((T, WC), F32)] * 2, compiler_params=_params(("parallel", "parallel")),
    )(qc, kc, vc, tables)


def _attn_c_bwd(name, qc, kc, vc, oc, lse, doc, tables):
    def body(q_ref, k_ref, v_ref, o_ref, lse_ref, do_ref, t_ref, dq_ref, dk_ref, dv_ref, dt_ref):
        @pl.when(pl.program_id(1) == 0)
        def _():
            dk_ref[...] = jnp.zeros_like(dk_ref)
            dv_ref[...] = jnp.zeros_like(dv_ref)
            dt_ref[...] = jnp.zeros_like(dt_ref)

        heads, where = [], []
        for rr in range(RPS):
            r = pl.program_id(1) * RPS + rr
            rows = slice(rr * GRID_W, (rr + 1) * GRID_W)
            d = _off_index(r)
            start = pl.multiple_of(_row_start(r) * GRID_W, GRID_W)
            kw, vw = k_ref[pl.ds(start, SPAN_C), :], v_ref[pl.ds(start, SPAN_C), :]
            where.append((rows, d, start))
            for h in range(2):
                sl = slice(h * HD, (h + 1) * HD)
                heads.append(dict(q=q_ref[rows, sl], k=kw[:, sl], v=vw[:, sl], o=o_ref[rows, sl], do=do_ref[rows, sl],
                                  lse=lse_ref[rows, h * HD:h * HD + 1], bias=_bias_c(t_ref, h, d)))
        res = _heads_bwd(heads)
        for rr, (rows, d, start) in enumerate(where):
            pair = res[2 * rr:2 * rr + 2]
            for h in range(2):
                for k in range(0, NA_ROWS, 2):
                    dt_ref[h, d + k] += pair[h][3][:, k * GRID_W:(k + 2) * GRID_W]
            dq_ref[rows, :] = jnp.concatenate([p[0] for p in pair], axis=1)
            dk_ref[pl.ds(start, SPAN_C), :] += jnp.concatenate([p[1] for p in pair], axis=1)
            dv_ref[pl.ds(start, SPAN_C), :] += jnp.concatenate([p[2] for p in pair], axis=1)

    qs = pl.BlockSpec((RPS * GRID_W, LANE), lambda p, r: (r, p))
    ks = pl.BlockSpec((T, LANE), lambda p, r: (0, p))
    ts = pl.BlockSpec((2, N_TAB, GRID_W, LANE), lambda p, r: (p, 0, 0, 0))
    return pl.pallas_call(
        body, name=name, grid=(HC // 2, ROWS // RPS), in_specs=[qs, ks, ks, qs, qs, qs, ts],
        out_specs=[qs, ks, ks, ts],
        out_shape=[jax.ShapeDtypeStruct((T, WC), F32)] * 3 + [jax.ShapeDtypeStruct((HC, N_TAB, GRID_W, LANE), F32)],
        compiler_params=_params(("parallel", "arbitrary")),
    )(qc, kc, vc, oc, lse, doc, tables)


def _split3(v):
    hi = v.astype(BF16)
    r1 = v - hi.astype(F32)
    mid = r1.astype(BF16)
    lo = (r1 - mid.astype(F32)).astype(BF16)
    return hi, mid, lo


def _rpb_reduce(name, dtables):
    x = dtables.reshape(HC, N_TAB, GRID_W * LANE)
    c = jnp.arange(GRID_W)[:, None]
    lane = jnp.arange(LANE)[None, :]
    col = (lane // GRID_W) * LANE + jnp.clip(lane % GRID_W - c + (NA_COLS - 1), 0, 2 * NA_COLS - 2)
    col_onehot = (col.reshape(-1)[:, None] == jnp.arange(2 * LANE)[None, :]).astype(BF16)
    a2 = jnp.arange(N_TAB)[None, :]
    row_onehot = jnp.concatenate([(jnp.arange(16)[:, None] == a2 + u) & (a2 < 2 * NA_ROWS - 2) for u in range(2)],
                                 axis=1).astype(BF16)

    def body(x_ref, e_ref, f_ref, o_ref):
        y = sum(jnp.dot(part, e_ref[...], preferred_element_type=F32) for part in _split3(x_ref[...]))
        z = jnp.concatenate([y[:, :LANE], y[:, LANE:]], axis=0)
        o_ref[...] = sum(jnp.dot(f_ref[...], part, preferred_element_type=F32) for part in _split3(z))

    out = pl.pallas_call(
        body, name=name, grid=(HC,),
        in_specs=[pl.BlockSpec((None, N_TAB, GRID_W * LANE), lambda h: (h, 0, 0)),
                  _whole((GRID_W * LANE, 2 * LANE)), _whole((16, 2 * N_TAB))],
        out_specs=pl.BlockSpec((None, 16, LANE), lambda h: (h, 0, 0)),
        out_shape=jax.ShapeDtypeStruct((HC, 16, LANE), F32), compiler_params=_params(("parallel",)),
    )(x, col_onehot, row_onehot)
    return out[:, :2 * NA_ROWS - 1, :2 * NA_COLS - 1]


TC = 128
NCB = DFF // TC
CHUNK = 128
MARGIN = 8


def _shift_down(v, rows):
    return jnp.where(rows == 0, 0.0, pltpu.roll(v, 1, 0))


def _shift_up(v, rows):
    return jnp.where(rows == T - 1, 0.0, pltpu.roll(v, T - 1, 0))


def _conv(v, w, b, rows):
    return _shift_down(v, rows) * w[0:1] + v * w[1:2] + _shift_up(v, rows) * w[2:3] + b


def _ffn_specs():
    gate = lambda shape: pl.BlockSpec(shape, lambda j: (0, j))
    val = lambda shape: pl.BlockSpec(shape, lambda j: (0, j + NCB))
    return [gate((T, TC)), val((T, TC)), gate((3, TC)), val((3, TC)), gate((1, TC)), val((1, TC))]


def _ffn_mid_fwd(name, up, conv_w, conv_b):
    def body(xg_ref, xv_ref, wg_ref, wv_ref, bg_ref, bv_ref, o_ref):
        rows = lax.broadcasted_iota(jnp.int32, (T, TC), 0)
        ug = _conv(xg_ref[...], wg_ref[...], bg_ref[...], rows)
        uv = _conv(xv_ref[...], wv_ref[...], bv_ref[...], rows)
        o_ref[...] = (ug * jax.nn.sigmoid(ug) * uv).astype(BF16)

    return pl.pallas_call(
        body, name=name, grid=(NCB,), in_specs=_ffn_specs(), out_specs=pl.BlockSpec((T, TC), lambda j: (0, j)),
        out_shape=jax.ShapeDtypeStruct((T, DFF), BF16), compiler_params=_params(("parallel",)),
    )(up, up, conv_w, conv_w, conv_b, conv_b)


def _ffn_mid_bwd(name, dact, up, conv_w, conv_b):
    window = CHUNK + 2 * MARGIN
    centre = slice(MARGIN, MARGIN + CHUNK)

    def shifted(v):
        return pltpu.roll(v, 1, 0), pltpu.roll(v, window - 1, 0)

    def fold(v):
        return jnp.sum(v[centre].reshape(CHUNK // 8, 8, TC), axis=0)

    def body(da_ref, xg_ref, xv_ref, wg_ref, wv_ref, bg_ref, bv_ref, dx_ref, dw_ref, db_ref, dap, xgp, xvp):
        for src, pad in ((da_ref, dap), (xg_ref, xgp), (xv_ref, xvp)):
            pad[0:MARGIN, :] = jnp.zeros((MARGIN, TC), F32)
            pad[MARGIN:MARGIN + T, :] = src[...]
            pad[MARGIN + T:, :] = jnp.zeros((MARGIN, TC), F32)
        wg, wv, bg, bv = wg_ref[...], wv_ref[...], bg_ref[...], bv_ref[...]

        def chunk(c, sums):
            r0 = pl.multiple_of(c * CHUNK, CHUNK)
            da, xg, xv = dap[pl.ds(r0, window), :], xgp[pl.ds(r0, window), :], xvp[pl.ds(r0, window), :]
            xg_prev, xg_next = shifted(xg)
            xv_prev, xv_next = shifted(xv)
            ug = xg_prev * wg[0:1] + xg * wg[1:2] + xg_next * wg[2:3] + bg
            uv = xv_prev * wv[0:1] + xv * wv[1:2] + xv_next * wv[2:3] + bv
            sg = jax.nn.sigmoid(ug)
            dug = da * uv * (sg * (1.0 + ug * (1.0 - sg)))
            duv = da * (ug * sg)
            out = []
            for half, (x_prev, x, x_next, w, du) in enumerate(((xg_prev, xg, xg_next, wg, dug),
                                                               (xv_prev, xv, xv_next, wv, duv))):
                du_prev, du_next = shifted(du)
                dx = du_next * w[0:1] + du * w[1:2] + du_prev * w[2:3]
                dx_ref[half, pl.ds(r0, CHUNK), :] = dx[centre].astype(BF16)
                out += [fold(x_prev * du), fold(x * du), fold(x_next * du), fold(du)]
            return tuple(s + o for s, o in zip(sums, out))

        sums = lax.fori_loop(0, T // CHUNK, chunk, tuple(jnp.zeros((8, TC), F32) for _ in range(8)))
        rows = [jnp.sum(s, axis=0, keepdims=True) for s in sums]
        for half in range(2):
            dw_ref[half] = jnp.concatenate(rows[4 * half:4 * half + 3], axis=0)
            db_ref[half] = rows[4 * half + 3]

    return pl.pallas_call(
        body, name=name, grid=(NCB,), in_specs=[pl.BlockSpec((T, TC), lambda j: (0, j))] + _ffn_specs(),
        out_specs=[pl.BlockSpec((2, T, TC), lambda j: (0, 0, j)), pl.BlockSpec((2, 3, TC), lambda j: (0, 0, j)),
                   pl.BlockSpec((2, 1, TC), lambda j: (0, 0, j))],
        out_shape=[jax.ShapeDtypeStruct((2, T, DFF), BF16), jax.ShapeDtypeStruct((2, 3, DFF), F32),
                   jax.ShapeDtypeStruct((2, 1, DFF), F32)],
        scratch_shapes=[pltpu.VMEM((T + 2 * MARGIN, TC), F32)] * 3,
        compiler_params=_params(("parallel",)),
    )(dact, up, up, conv_w, conv_w, conv_b, conv_b)


def _dup_spec(tm, nj):
    per = DFF // nj
    return pl.BlockSpec((None, tm, nj), lambda a, b, j: (j // per, 0 if tm == T else b, j % per))


def _dup_spec_tn(tm, nj):
    per = DFF // nj
    return pl.BlockSpec((None, tm, nj), lambda j, kt, r: (j // per, 0, j % per))


def _adamw_math(w, g, m, v):
    m = ADAM_B1 * m + (1.0 - ADAM_B1) * g
    v = ADAM_B2 * v + (1.0 - ADAM_B2) * (g * g)
    m_hat = m / (1.0 - ADAM_B1 ** ADAM_STEP)
    v_hat = v / (1.0 - ADAM_B2 ** ADAM_STEP)
    delta = -ADAM_LR * (m_hat / (jnp.sqrt(v_hat) + ADAM_EPS) + ADAM_WD * w)
    return delta, m, v


ADAM_BLOCK = 256 * 1408


def _adamw_sharded(name, w, m, v, parts):
    _, r, c = w.shape
    tr = max(t for t in range(16, r + 1, 16) if r % t == 0 and t * c <= ADAM_BLOCK)

    def body(w_ref, m_ref, v_ref, p0_ref, p1_ref, g_ref, d_ref, nm_ref, nv_ref):
        def run(p_ref):
            g = p_ref[0].astype(F32)
            for k in range(1, N_DEV):
                g = g + p_ref[k].astype(F32)
            d, nm, nv = _adamw_math(w_ref[...], g, m_ref[...], v_ref[...])
            g_ref[...] = g
            d_ref[...] = d
            nm_ref[...] = nm
            nv_ref[...] = nv

        @pl.when(pl.program_id(0) == 0)
        def _():
            run(p0_ref)

        @pl.when(pl.program_id(0) == 1)
        def _():
            run(p1_ref)

    ws = pl.BlockSpec((None, tr, c), lambda l, i: (l, i, 0))
    p0 = pl.BlockSpec((N_DEV, tr, c), lambda l, i: (0, jnp.where(l == 0, i, r // tr - 1), 0))
    p1 = pl.BlockSpec((N_DEV, tr, c), lambda l, i: (0, jnp.where(l == 1, i, 0), 0))
    return pl.pallas_call(
        body, name=name, grid=(DEPTH, r // tr), in_specs=[ws, ws, ws, p0, p1], out_specs=[ws] * 4,
        out_shape=[jax.ShapeDtypeStruct(w.shape, F32)] * 4, compiler_params=_params(("arbitrary", "arbitrary")),
    )(w, m, v, *parts)


def _sum_devices(name, parts):
    r = parts.shape[1]

    def body(p_ref, o_ref):
        g = p_ref[0]
        for k in range(1, N_DEV):
            g = g + p_ref[k]
        o_ref[...] = g

    return pl.pallas_call(
        body, name=name, in_specs=[pl.BlockSpec((N_DEV, r, LANE), lambda: (0, 0, 0))],
        out_specs=pl.BlockSpec((r, LANE), lambda: (0, 0)), out_shape=jax.ShapeDtypeStruct((r, LANE), F32),
        compiler_params=_params(),
    )(parts)


def _adamw_small(name, ws, gs, ms, vs):
    n = len(ws)
    shapes = [w.shape for w in ws]
    ws, gs, ms, vs = ([a.reshape(1, -1) if a.ndim == 1 else a for a in arrs] for arrs in (ws, gs, ms, vs))
    specs = [pl.BlockSpec(memory_space=pltpu.VMEM)] * n

    def body(*refs):
        for i in range(n):
            w_ref, g_ref, m_ref, v_ref = (refs[k * n + i] for k in range(4))
            d, nm, nv = _adamw_math(w_ref[...], g_ref[...], m_ref[...], v_ref[...])
            refs[4 * n + i][...] = d
            refs[5 * n + i][...] = nm
            refs[6 * n + i][...] = nv

    outs = pl.pallas_call(
        body, name=name, in_specs=specs * 4, out_specs=specs * 3,
        out_shape=[jax.ShapeDtypeStruct(w.shape, F32) for w in ws] * 3, compiler_params=_params(),
    )(*ws, *gs, *ms, *vs)
    outs = [o.reshape(shapes[i % n]) for i, o in enumerate(outs)]
    return outs[:n], outs[n:2 * n], outs[2 * n:]


def _pack(arrays):
    flat = jnp.concatenate([a.reshape(-1) for a in arrays])
    pad = (-flat.shape[0]) % (8 * LANE)
    return jnp.pad(flat, (0, pad)).reshape(-1, LANE)


def _unpack(buf, shapes):
    flat, out, off = buf.reshape(-1), [], 0
    for s in shapes:
        n = 1
        for d in s:
            n *= d
        out.append(flat[off:off + n].reshape(s))
        off += n
    return out


def _local_step(x, target, small, weights, conv_w_full, hand_over, used):
    cos2, sin2 = _rope_tables()
    bias_a = _dilation_bias()
    tables = [_rpb_tables(f"rpb_tables_{l}", small["rpb_c"][l]) for l in range(DEPTH)]
    saved, carry = [], 0.0
    for l in range(DEPTH):
        g1, g2 = small["ln_attn"][l][None] + carry, small["ln_ffn"][l][None]
        gain, sink, cb = small["mix_gain"][l][None], small["sink_b"][l], small["conv_b"][l][None]
        cw = conv_w_full[l]
        bias = tables[l]
        h1, proj = _prologue_matmul(f"proj_in_{l}", _rmsnorm_rows, [x, g1], [D, None],
                                    weights("w_in", l, [cos2, sin2, bias_a] + tables if l == 0 else x),
                                    (D, 512), lambda j: (0, j), 512)
        zero = used(proj)
        qa, ka, va, qb, kb, vb, qc, kc, vc = _rope_fwd(f"rope_{l}", proj, cos2, sin2)
        oa, lse_a = _attn_a_fwd(f"attn_a_{l}", qa, ka, va, bias_a)
        ob, lse_b = _attn_b_fwd(f"attn_b_{l}", qb, kb, vb, sink + zero)
        oc, lse_c = _attn_c_fwd(f"attn_c_{l}", qc, kc, vc, bias)
        mixed, x_mid = _prologue_matmul(f"proj_out_{l}", _mix_rows, [oa, ob, oc, gain], [WA, WB, WC, None],
                                        weights("w_out", l, oc), (N_DEV, D // N_DEV, 512), lambda j: (0, 0, j), 512,
                                        res=x)
        h2, up = _prologue_matmul(f"ffn_up_{l}", _rmsnorm_rows, [x_mid, g2 + used(x_mid)], [D, None],
                                  weights("w_up", l, x_mid), (D, 512), lambda j: (0, j), 512)
        act = _ffn_mid_fwd(f"ffn_mid_{l}", up, cw, cb + used(up))
        x_out = _nn_rows(f"ffn_down_{l}", act, weights("w_down", l, act), x_mid, 4, 1024, 1024)
        carry = used(x_out)
        saved.append(dict(x=x, h1=h1, qkv=(qa, ka, va, qb, kb, vb, qc, kc, vc), o=(oa, ob, oc), lse=(lse_a, lse_b, lse_c), mixed=mixed,
                          x_mid=x_mid, h2=h2, up=up, act=act, g1=g1, g2=g2, gain=gain, sink=sink, cb=cb, cw=cw, bias=bias))
        x = x_out

    loss8, dx, dxb, d_ln_final = _loss_head(x, small["ln_final"][None], target)
    sgrads = [None] * DEPTH
    for l in reversed(range(DEPTH)):
        s = saved[l]
        qa, ka, va, qb, kb, vb, qc, kc, vc = s["qkv"]
        oa, ob, oc = s["o"]
        wg_in, wg_out = weights("w_in", l, None), weights("w_out", l, None)
        wg_up, wg_down = weights("w_up", l, None), weights("w_down", l, None)
        g_down = _tn_rows(f"wgrad_down_{l}", s["act"], dxb, wg_down.shape[1], 2, 512)
        zero = hand_over("w_down", l, g_down)
        dact = _nt_rows(f"dgrad_down_{l}", dxb, wg_down, 4, 512)
        dup, d_cw, d_cb = _ffn_mid_bwd(f"ffn_mid_bwd_{l}", dact, s["up"], s["cw"], s["cb"] + zero)
        g_up = _tn_cols(f"wgrad_up_{l}", s["h2"], dup, _dup_spec_tn, 2 * DFF, DFF // 2)
        zero = hand_over("w_up", l, g_up)
        dh2 = _nt_cols(f"dgrad_up_{l}", dup, _dup_spec, wg_up, DFF // 2)
        dx, dxb, d_g2 = _rmsnorm_bwd(f"norm_ffn_bwd_{l}", dh2, s["x_mid"], s["g2"] + zero, dx)
        g_out = _tn_rows(f"wgrad_out_{l}", s["mixed"], dxb, wg_out.shape[1], 2, D)
        zero = hand_over("w_out", l, g_out)
        dmixed = _nt_rows(f"dgrad_out_{l}", dxb, wg_out, 2, T)
        doa, dob, doc, d_gain = _mix_bwd(f"mix_bwd_{l}", dmixed, oa, ob, oc, s["gain"] + zero)
        lse_a, lse_b, lse_c = s["lse"]
        dqa, dka, dva = _attn_a_bwd(f"attn_a_bwd_{l}", qa, ka, va, oa, lse_a, doa, bias_a)
        dqb, dkb, dvb, d_sink = _attn_b_bwd(f"attn_b_bwd_{l}", qb, kb, vb, ob, lse_b, dob, s["sink"])
        dqc, dkc, dvc, d_bias = _attn_c_bwd(f"attn_c_bwd_{l}", qc, kc, vc, oc, lse_c, doc, s["bias"])
        d_rpb = _rpb_reduce(f"rpb_reduce_{l}", d_bias)
        dproj = _rope_bwd(f"rope_bwd_{l}", (dqa, dka, dva, dqb, dkb, dvb, dqc, dkc, dvc), cos2, sin2)
        g_in = _tn_cols(f"wgrad_in_{l}", s["h1"], dproj,
                        lambda tm, tn: pl.BlockSpec((tm, tn), lambda j, kt, r: (0, j)), IN_COLS, 1024)
        zero = hand_over("w_in", l, g_in)
        dh1 = _nt_cols(f"dgrad_in_{l}", dproj, lambda tm, nc: pl.BlockSpec((tm, nc), lambda kt, i, j: (i, j)), wg_in,
                       IN_COLS // 2)
        dx, dxb, d_g1 = _rmsnorm_bwd(f"norm_attn_bwd_{l}", dh1, s["x"], s["g1"] + zero, dx)
        sgrads[l] = dict(ln_attn=d_g1[0], sink_b=d_sink[0, :HB], rpb_c=d_rpb, mix_gain=d_gain[0], ln_ffn=d_g2[0],
                         conv_w=d_cw.transpose(1, 0, 2).reshape(3, 2 * DFF), conv_b=d_cb.reshape(2 * DFF))
    return loss8[0, 0], dx, d_ln_final[0], sgrads


SMALL_NAMES = ("ln_attn", "sink_b", "rpb_c", "mix_gain", "ln_ffn", "conv_b")


def kernel(x, ln_attn, w_in, sink_b, rpb_c, mix_gain, w_out, ln_ffn, w_up, conv_w, conv_b, w_down, ln_final, loss_target, m_ln_attn, m_w_in, m_sink_b, m_rpb_c, m_mix_gain, m_w_out, m_ln_ffn, m_w_up, m_conv_w, m_conv_b, m_w_down, m_ln_final, v_ln_attn, v_w_in, v_sink_b, v_rpb_c, v_mix_gain, v_w_out, v_ln_ffn, v_w_up, v_conv_w, v_conv_b, v_w_down, v_ln_final):
    me = 4 * lax.axis_index("x") + 2 * lax.axis_index("y") + lax.axis_index("c")
    small = dict(ln_attn=ln_attn, sink_b=sink_b, rpb_c=rpb_c, mix_gain=mix_gain, ln_ffn=ln_ffn, conv_b=conv_b,
                 ln_final=ln_final)

    names = ("w_in", "w_out", "w_up", "w_down")
    shards = dict(w_in=w_in, w_out=w_out, w_up=w_up, w_down=w_down)
    order = [(n, l) for l in range(DEPTH) for n in names]
    conv_key = ("conv_w", 0)
    started, arrived, forwarded, gathered = {}, {}, {}, {}

    def side_by_side(k):
        return k[0] in ("w_in", "w_up")

    def slot_of(k):
        return _col_slot(shards[k[0]].shape[2]) if side_by_side(k) else _lead_slot

    def begin(name, ks, zero):
        srcs = [_pack([conv_w]) + zero if k == conv_key else (shards[k[0]][k[1]] + zero).astype(BF16) for k in ks]
        lands = [lax.empty((s.shape[0], N_DEV * s.shape[1]) if side_by_side(k) else (N_DEV,) + s.shape, s.dtype)
                 for k, s in zip(ks, srcs)]
        peers = [ALL_PEERS if k == conv_key else NEAR_PEERS for k in ks]
        send, recv, bufs, tok = _copy_start(name, srcs + lands, _gather_plan(peers, [slot_of(k) for k in ks]),
                                            [len(p) + 1 for p in peers])
        for i, k in enumerate(ks):
            started[k] = (send[i], recv[i], bufs[i], bufs[len(ks) + i], peers[i])
        return tok

    token = begin("gather_start_first", order[:1], 0.0)
    token = begin("gather_start_rest", [conv_key] + order[1:], token[0, 0])

    def arrive(k, after):
        send, recv, src, land, peers = started[k]
        arrived[k] = _copy_wait(f"gather_{k[0]}_{k[1]}_arrived", [src, land], [send], [recv],
                                _gather_plan([peers], [slot_of(k)]), after)

    queue = list(order)

    def advance(after):
        if not queue:
            return 0.0
        k = queue.pop(0)
        arrive(k, after)
        forwarded[k] = _copy_start(f"gather_{k[0]}_{k[1]}_forward", [arrived[k][1]], _forward_plan(slot_of(k)),
                                   [len(OTHER_CHIPS)])
        return forwarded[k][3][0, 0]

    def weights(n, l, after):
        k = (n, l)
        if k not in gathered:
            if k not in forwarded:
                advance(after)
            send_b, recv_b, (land,), _ = forwarded[k]
            (gathered[k],) = _copy_wait(f"gather_{n}_{l}_done", [land], send_b, recv_b, _forward_plan(slot_of(k)),
                                        after)
        return gathered[k]

    pending = {}

    def hand_over(n, l, g):
        shard = shards[n].shape[1:]
        send, recv, bufs, tok = _copy_start(f"send_grad_{n}_{l}", [g, lax.empty((N_DEV,) + shard, g.dtype)],
                                            _scatter_plan(slot_of((n, l))), [len(ALL_PEERS) + 1])
        pending[(n, l)] = (send, recv, bufs)
        return tok[0, 0]

    def received(k, after):
        send, recv, bufs = pending[k]
        return _copy_wait(f"recv_grad_{k[0]}_{k[1]}", bufs, send, recv, _scatter_plan(slot_of(k)), after)[1]

    arrive(conv_key, token)
    cw_all = arrived[conv_key][1]
    nup = w_up.shape[2]
    cw_shards = cw_all.reshape(N_DEV, -1)[:, :DEPTH * 3 * nup].reshape(N_DEV, DEPTH, 3, nup)
    conv_w_full = cw_shards.transpose(1, 2, 0, 3).reshape(DEPTH, 3, N_DEV * nup)

    loss_local, dx, d_ln_final, sgrads = _local_step(
        x[0], loss_target[0], dict(small, ln_attn=ln_attn + token[0, 0]), weights, conv_w_full, hand_over, advance)

    stacked = [jnp.stack([sgrads[l][n] for l in range(DEPTH)]) for n in SMALL_NAMES + ("conv_w",)] + [d_ln_final]
    shapes = [a.shape for a in stacked]
    mine = _pack(stacked)
    send_s, recv_s, bufs_s, _ = _copy_start("gather_small_grads_start", [mine, lax.empty((N_DEV,) + mine.shape, F32)],
                                            _gather_plan([ALL_PEERS], [_lead_slot]), [len(ALL_PEERS) + 1])

    big, after = {}, dx
    moments = dict(w_in=(m_w_in, v_w_in), w_out=(m_w_out, v_w_out), w_up=(m_w_up, v_w_up), w_down=(m_w_down, v_w_down))
    for n in reversed(names):
        parts = (received((n, 0), after), received((n, 1), after))
        big[n] = _adamw_sharded(f"adamw_{n}", shards[n], *moments[n], parts)
        after = big[n][1]

    _, everyone = _copy_wait("gather_small_grads_done", bufs_s, send_s, recv_s,
                             _gather_plan([ALL_PEERS], [_lead_slot]), after)
    g_small = _unpack(_sum_devices("sum_small_grads", everyone), shapes)
    g = dict(zip(SMALL_NAMES + ("conv_w", "ln_final"), g_small))
    g["conv_w"] = lax.dynamic_slice_in_dim(g["conv_w"], me * nup, nup, axis=2)

    snames = SMALL_NAMES + ("conv_w", "ln_final")
    sw = dict(small, conv_w=conv_w)
    sm = dict(ln_attn=m_ln_attn, sink_b=m_sink_b, rpb_c=m_rpb_c, mix_gain=m_mix_gain, ln_ffn=m_ln_ffn,
              conv_b=m_conv_b, conv_w=m_conv_w, ln_final=m_ln_final)
    sv = dict(ln_attn=v_ln_attn, sink_b=v_sink_b, rpb_c=v_rpb_c, mix_gain=v_mix_gain, ln_ffn=v_ln_ffn,
              conv_b=v_conv_b, conv_w=v_conv_w, ln_final=v_ln_final)
    s_delta, s_m, s_v = (dict(zip(snames, out)) for out in _adamw_small(
        "adamw_small", [sw[n] for n in snames], [g[n] for n in snames], [sm[n] for n in snames],
        [sv[n] for n in snames]))

    loss = lax.psum(loss_local, ("x", "y", "c"))
    outputs = ("ln_attn", "w_in", "sink_b", "rpb_c", "mix_gain", "w_out", "ln_ffn", "w_up", "conv_w", "conv_b",
               "w_down", "ln_final")
    grads = [big[n][0] if n in big else g[n] for n in outputs]
    deltas = [big[n][1] if n in big else s_delta[n] for n in outputs]
    new_m = [big[n][2] if n in big else s_m[n] for n in outputs]
    new_v = [big[n][3] if n in big else s_v[n] for n in outputs]
    return (loss, dx[None], *grads, *deltas, *new_m, *new_v)
```

```python
import functools

import jax
import jax.numpy as jnp
from jax import lax
from jax.experimental import pallas as pl
from jax.experimental.pallas import tpu as pltpu

F32 = jnp.float32
BF16 = jnp.bfloat16

N_DEV = 8
T = 2048
D = 2048
DEPTH = 2
HD = 64
HA, HB, HKV, HC = 12, 10, 2, 10
WA, WB, WKV, WC = HA * HD, HB * HD, HKV * HD, HC * HD
IN_COLS = 3 * WA + WB + 2 * WKV + 3 * WC
DFF = 5632
GRID_W = 64
ROWS = T // GRID_W
NA_ROWS, NA_COLS = 8, 16
WINDOW_B = 128
EPS = 1e-6
NEG = -1e30
ROPE_THETA = 10000.0
LANE = 128
VMEM_LIMIT = 56 * 1024 * 1024

ADAM_LR, ADAM_B1, ADAM_B2, ADAM_EPS, ADAM_WD, ADAM_STEP = 0.001, 0.9, 0.999, 1e-08, 0.01, 10

GROUPS = (("qa", WA, True, True), ("ka", WA, True, False), ("va", WA, False, False),
          ("qb", WB, True, True), ("kb", WKV, True, False), ("vb", WKV, False, False),
          ("qc", WC, False, True), ("kc", WC, False, False), ("vc", WC, False, False))


def _params(sem=None):
    return pltpu.CompilerParams(dimension_semantics=sem, vmem_limit_bytes=VMEM_LIMIT)


HBM_SPEC = pl.BlockSpec(memory_space=pltpu.HBM)
SEM_SPEC = pl.BlockSpec(memory_space=pltpu.SEMAPHORE)
DATAFLOW = pltpu.SideEffectType.DATAFLOW_SIDE_EFFECTING


ALL_PEERS = tuple((p >> 2 & 1, p >> 1 & 1, p & 1) for p in range(1, N_DEV))
OTHER_CHIPS = ((1, 0, 0), (0, 1, 0), (1, 1, 0))
NEAR_PEERS = ((0, 0, 1),) + OTHER_CHIPS


def _flip(x, y, c, f):
    return (1 - x if f[0] else x, 1 - y if f[1] else y, 1 - c if f[2] else c)


def _index(pos):
    return 4 * pos[0] + 2 * pos[1] + pos[2]


class _LocalCopy:
    def __init__(self, src, dst, sem):
        self.copy = pltpu.make_async_copy(src, dst, sem)

    def start(self):
        self.copy.start()

    def wait_send(self):
        self.copy.wait()

    def wait_recv(self):
        pass


def _descriptors(plan, bufs, send_sems, recv_sems):
    x, y, c = lax.axis_index("x"), lax.axis_index("y"), lax.axis_index("c")
    return [_LocalCopy(src, dst, send_sems[g].at[i]) if partner is None else
            pltpu.make_async_remote_copy(src_ref=src, dst_ref=dst, send_sem=send_sems[g].at[i],
                                         recv_sem=recv_sems[g].at[i], device_id=partner,
                                         device_id_type=pl.DeviceIdType.MESH)
            for g, copies in enumerate(plan(bufs, x, y, c)) for i, (src, dst, partner) in enumerate(copies)]


def _copy_start(name, bufs, plan, sizes):
    nb, ng = len(bufs), len(sizes)

    def body(*refs):
        for d in _descriptors(plan, refs[:nb], refs[nb:nb + ng], refs[nb + ng:nb + 2 * ng]):
            d.start()
        refs[2 * nb + 2 * ng][...] = jnp.zeros((8, LANE), F32)

    outs = pl.pallas_call(
        body, name=name,
        out_shape=[pltpu.SemaphoreType.DMA((s,)) for s in sizes] * 2 + [pltpu.HBM(b.shape, b.dtype) for b in bufs]
        + [jax.ShapeDtypeStruct((8, LANE), F32)],
        in_specs=[HBM_SPEC] * nb,
        out_specs=[SEM_SPEC] * (2 * ng) + [HBM_SPEC] * nb + [pl.BlockSpec(memory_space=pltpu.VMEM)],
        input_output_aliases={i: 2 * ng + i for i in range(nb)},
        compiler_params=pltpu.CompilerParams(has_side_effects=DATAFLOW),
    )(*[pltpu.with_memory_space_constraint(b, pltpu.HBM) for b in bufs])
    return outs[:ng], outs[ng:2 * ng], outs[2 * ng:2 * ng + nb], outs[2 * ng + nb]


def _copy_wait(name, bufs, send_sems, recv_sems, plan, after):
    nb, ng = len(bufs), len(send_sems)
    after = list(after) if isinstance(after, (list, tuple)) else [after]

    def body(*refs):
        for d in _descriptors(plan, refs[:nb], refs[nb:nb + ng], refs[nb + ng:nb + 2 * ng]):
            d.wait_send()
            d.wait_recv()

    return pl.pallas_call(
        body, name=name, out_shape=[pltpu.HBM(b.shape, b.dtype) for b in bufs],
        in_specs=[HBM_SPEC] * nb + [SEM_SPEC] * (2 * ng) + [pl.BlockSpec(memory_space=pl.ANY)] * len(after),
        out_specs=[HBM_SPEC] * nb, input_output_aliases={i: i for i in range(nb)},
        compiler_params=pltpu.CompilerParams(has_side_effects=DATAFLOW),
    )(*bufs, *send_sems, *recv_sems, *after)


def _lead_slot(ref, k):
    return ref.at[k]


def _col_slot(width):
    return lambda ref, k: ref.at[:, pl.ds(pl.multiple_of(k * width, LANE), width)]


def _gather_plan(peer_sets, slots):
    def plan(bufs, x, y, c):
        n = len(peer_sets)
        return [[(bufs[i], slots[i](bufs[n + i], _index((x, y, c))), _flip(x, y, c, f)) for f in peers]
                + [(bufs[i], slots[i](bufs[n + i], _index((x, y, c))), None)] for i, peers in enumerate(peer_sets)]
    return plan


def _forward_plan(slot):
    def plan(bufs, x, y, c):
        pieces = [slot(bufs[0], _index(_flip(x, y, c, f))) for f in OTHER_CHIPS]
        return [[(p, p, _flip(x, y, c, (0, 0, 1))) for p in pieces]]
    return plan


def _scatter_plan(slot):
    def plan(bufs, x, y, c):
        me = _index((x, y, c))
        peers = [_flip(x, y, c, f) for f in ALL_PEERS]
        return [[(slot(bufs[0], _index(p)), bufs[1].at[me], p) for p in peers]
                + [(slot(bufs[0], me), bufs[1].at[me], None)]]
    return plan


def _flat2(v):
    return v.reshape(-1, v.shape[-1])


def _matmul(name, kind, a, a_spec, b, b_spec, out_shape, out_spec, grid, res=None, res_spec=None, acc_shape=None):
    dims = {"nn": (((1,), (0,)), ((), ())), "nt": NT_DIMS, "nts": NT_DIMS, "tn": (((0,), (0,)), ((), ()))}[kind]
    nred = grid[-1]

    def body(*refs):
        if res is None:
            a_ref, b_ref, o_ref = refs[:3]
            r_ref = None
        else:
            a_ref, b_ref, r_ref, o_ref = refs[:4]
        if kind == "nts":
            n = b_ref.shape[-1]
            part = sum(lax.dot_general(a_ref[:, blk * n:(blk + 1) * n], b_ref[blk], dims, preferred_element_type=F32)
                       for blk in range(b_ref.shape[0]))
        else:
            part = lax.dot_general(_flat2(a_ref[...]), _flat2(b_ref[...]), dims, preferred_element_type=F32)

        def finish(total):
            if r_ref is not None:
                total = total + r_ref[...]
            o_ref[...] = total.reshape(o_ref.shape).astype(o_ref.dtype)

        if nred == 1:
            finish(part)
        else:
            acc_ref = refs[-1]
            k = pl.program_id(len(grid) - 1)

            @pl.when(k == 0)
            def _():
                acc_ref[...] = part

            @pl.when(jnp.logical_and(k > 0, k < nred - 1))
            def _():
                acc_ref[...] += part

            @pl.when(k == nred - 1)
            def _():
                finish(acc_ref[...] + part)

    ins, specs = [a, b], [a_spec, b_spec]
    if res is not None:
        ins.append(res)
        specs.append(res_spec)
    scratch = [] if nred == 1 else [pltpu.VMEM(acc_shape, F32)]
    return pl.pallas_call(
        body, name=name, grid=grid, in_specs=specs, out_specs=out_spec, out_shape=out_shape, scratch_shapes=scratch,
        compiler_params=_params(("parallel",) * (len(grid) - 1) + ("arbitrary",)),
    )(*ins)


def _nn_rows(name, a, wg, res, s, tn, tm):
    _, kj, n = wg.shape
    return _matmul(
        name, "nn", a, pl.BlockSpec((tm, s * kj), lambda j, i, r: (i, r)),
        wg, pl.BlockSpec((s, kj, tn), lambda j, i, r: (r, 0, j)),
        jax.ShapeDtypeStruct((T, n), F32), pl.BlockSpec((tm, tn), lambda j, i, r: (i, j)),
        (n // tn, T // tm, N_DEV // s), res=res, res_spec=pl.BlockSpec((tm, tn), lambda j, i, r: (i, j)),
        acc_shape=(tm, tn))


def _nt_cols(name, dc, dc_spec_of, w, nc):
    k, n = w.shape
    tm = tk = 1024
    return _matmul(
        name, "nt", dc, dc_spec_of(tm, nc),
        w, pl.BlockSpec((tk, nc), lambda kt, i, j: (kt, j)),
        jax.ShapeDtypeStruct((T, k), F32), pl.BlockSpec((tm, tk), lambda kt, i, j: (i, kt)),
        (k // tk, T // tm, n // nc), acc_shape=(tm, tk))


def _nt_rows(name, dc, wg, s, tm):
    _, kj, n = wg.shape
    return _matmul(
        name, "nt", dc, pl.BlockSpec((tm, n), lambda kt, i, r: (i, 0)),
        wg, pl.BlockSpec((s, kj, n), lambda kt, i, r: (kt, 0, 0)),
        jax.ShapeDtypeStruct((T, N_DEV * kj), F32), pl.BlockSpec((tm, s * kj), lambda kt, i, r: (i, kt)),
        (N_DEV // s, T // tm, 1))


def _tn_cols(name, a, dc, dc_spec_of, n, tn):
    k = a.shape[1]
    tk = 512
    return _matmul(
        name, "tn", a, pl.BlockSpec((T, tk), lambda j, kt, r: (0, kt)),
        dc, dc_spec_of(T, tn),
        jax.ShapeDtypeStruct((k, n), BF16), pl.BlockSpec((tk, tn), lambda j, kt, r: (kt, j)),
        (n // tn, k // tk, 1))


def _tn_rows(name, a, dc, kj, s, tn):
    n = dc.shape[1]
    return _matmul(
        name, "tn", a, pl.BlockSpec((T, s * kj), lambda kt, j, r: (0, kt)),
        dc, pl.BlockSpec((T, tn), lambda kt, j, r: (0, j)),
        jax.ShapeDtypeStruct((N_DEV, kj, n), BF16), pl.BlockSpec((s, kj, tn), lambda kt, j, r: (kt, 0, j)),
        (N_DEV // s, n // tn, 1))


TR = 256


def _rows(width):
    return pl.BlockSpec((TR, width), lambda i: (i, 0))


def _whole(shape):
    return pl.BlockSpec(shape, lambda i: (0,) * len(shape))


def _rmsnorm_rows(x_ref, g_ref):
    xv = x_ref[...]
    r = lax.rsqrt(jnp.mean(xv * xv, axis=-1, keepdims=True) + EPS)
    return ((xv * r) * g_ref[...]).astype(BF16)


def _prologue_matmul(name, prologue, ins, widths, w, w_block, w_index, tn, res=None):
    tm = 1024
    n = w.shape[-1]
    ni = len(ins)

    def body(*refs):
        w_ref = refs[ni]
        r_ref = refs[ni + 1] if res is not None else None
        h_ref, o_ref, h_scr = refs[-3:]

        @pl.when(pl.program_id(1) == 0)
        def _():
            h = prologue(*refs[:ni])
            h_scr[...] = h
            h_ref[...] = h

        total = jnp.dot(h_scr[...], _flat2(w_ref[...]), preferred_element_type=F32)
        if r_ref is not None:
            total = total + r_ref[...]
        o_ref[...] = total

    tile = pl.BlockSpec((tm, tn), lambda i, j: (i, j))
    specs = [pl.BlockSpec((1, D), lambda i, j: (0, 0)) if wd is None else pl.BlockSpec((tm, wd), lambda i, j: (i, 0))
             for wd in widths]
    specs.append(pl.BlockSpec(w_block, lambda i, j: w_index(j)))
    operands = list(ins) + [w]
    if res is not None:
        specs.append(tile)
        operands.append(res)
    return pl.pallas_call(
        body, name=name, grid=(T // tm, n // tn), in_specs=specs,
        out_specs=[pl.BlockSpec((tm, D), lambda i, j: (i, 0)), tile],
        out_shape=[jax.ShapeDtypeStruct((T, D), BF16), jax.ShapeDtypeStruct((T, n), F32)],
        scratch_shapes=[pltpu.VMEM((tm, D), BF16)], compiler_params=_params(("parallel", "arbitrary")),
    )(*operands)


def _rms_bwd_math(dy, xv, g):
    r = lax.rsqrt(jnp.mean(xv * xv, axis=-1, keepdims=True) + EPS)
    xhat = xv * r
    dxhat = dy * g
    dx = r * (dxhat - xhat * jnp.mean(dxhat * xhat, axis=-1, keepdims=True))
    return dx, dy * xhat


def _accumulate(ref, val):
    @pl.when(pl.program_id(0) == 0)
    def _():
        ref[...] = val

    @pl.when(pl.program_id(0) > 0)
    def _():
        ref[...] += val


def _rmsnorm_bwd(name, dy, x, g, res):
    def body(dy_ref, x_ref, g_ref, res_ref, dx_ref, dxb_ref, dg_ref):
        dx, dgr = _rms_bwd_math(dy_ref[...], x_ref[...], g_ref[...])
        tot = res_ref[...] + dx
        dx_ref[...] = tot
        dxb_ref[...] = tot.astype(BF16)
        _accumulate(dg_ref, jnp.sum(dgr, axis=0, keepdims=True))

    return pl.pallas_call(
        body, name=name, grid=(T // TR,), in_specs=[_rows(D), _rows(D), _whole((1, D)), _rows(D)],
        out_specs=[_rows(D), _rows(D), _whole((1, D))],
        out_shape=[jax.ShapeDtypeStruct((T, D), F32), jax.ShapeDtypeStruct((T, D), BF16),
                   jax.ShapeDtypeStruct((1, D), F32)],
        compiler_params=_params(("arbitrary",)),
    )(dy, x, g, res)


def _loss_head(x, g, target):
    def body(x_ref, g_ref, t_ref, loss_ref, dx_ref, dxb_ref, dg_ref):
        xv, gv = x_ref[...], g_ref[...]
        r = lax.rsqrt(jnp.mean(xv * xv, axis=-1, keepdims=True) + EPS)
        err = (xv * r) * gv - t_ref[...]
        part = 0.5 * jnp.sum(jnp.mean(err * err, axis=-1, keepdims=True))
        dx, dgr = _rms_bwd_math(err * (1.0 / D), xv, gv)
        dx_ref[...] = dx
        dxb_ref[...] = dx.astype(BF16)
        _accumulate(dg_ref, jnp.sum(dgr, axis=0, keepdims=True))
        _accumulate(loss_ref, jnp.full((8, LANE), part, F32))

    return pl.pallas_call(
        body, name="loss_head", grid=(T // TR,), in_specs=[_rows(D), _whole((1, D)), _rows(D)],
        out_specs=[_whole((8, LANE)), _rows(D), _rows(D), _whole((1, D))],
        out_shape=[jax.ShapeDtypeStruct((8, LANE), F32), jax.ShapeDtypeStruct((T, D), F32),
                   jax.ShapeDtypeStruct((T, D), BF16), jax.ShapeDtypeStruct((1, D), F32)],
        compiler_params=_params(("arbitrary",)),
    )(x, g, target)


MIX_OFFS = ((0, WA), (WA, WB), (WA + WB, WC))


def _mix_rows(oa_ref, ob_ref, oc_ref, g_ref):
    parts = []
    for ref, (off, w) in zip((oa_ref, ob_ref, oc_ref), MIX_OFFS):
        o = ref[...]
        r = lax.rsqrt(jnp.mean(o * o, axis=-1, keepdims=True) + EPS)
        parts.append(((o * r) * g_ref[:, off:off + w]).astype(BF16))
    return jnp.concatenate(parts, axis=1)


def _mix_bwd(name, dmixed, oa, ob, oc, gain):
    def body(dm_ref, oa_ref, ob_ref, oc_ref, g_ref, doa_ref, dob_ref, doc_ref, dg_ref):
        dgs = []
        for ref, dref, (off, w) in zip((oa_ref, ob_ref, oc_ref), (doa_ref, dob_ref, doc_ref), MIX_OFFS):
            dx, dgr = _rms_bwd_math(dm_ref[:, off:off + w], ref[...], g_ref[:, off:off + w])
            dref[...] = dx
            dgs.append(jnp.sum(dgr, axis=0, keepdims=True))
        _accumulate(dg_ref, jnp.concatenate(dgs, axis=1))

    return pl.pallas_call(
        body, name=name, grid=(T // TR,),
        in_specs=[_rows(D), _rows(WA), _rows(WB), _rows(WC), _whole((1, D))],
        out_specs=[_rows(WA), _rows(WB), _rows(WC), _whole((1, D))],
        out_shape=[jax.ShapeDtypeStruct((T, WA), F32), jax.ShapeDtypeStruct((T, WB), F32),
                   jax.ShapeDtypeStruct((T, WC), F32), jax.ShapeDtypeStruct((1, D), F32)],
        compiler_params=_params(("arbitrary",)),
    )(dmixed, oa, ob, oc, gain)


def _rope_tables():
    inv_freq = ROPE_THETA ** (-jnp.arange(0, HD, 2, dtype=F32) / HD)
    ang = jnp.arange(T, dtype=F32)[:, None] * inv_freq[None, :]
    cos, sin = jnp.cos(ang), jnp.sin(ang)
    cos2 = jnp.tile(jnp.concatenate([cos, cos], axis=1), (1, LANE // HD))
    sin2 = jnp.tile(jnp.concatenate([-sin, sin], axis=1), (1, LANE // HD))
    return cos2, sin2


def _rot_half(v):
    lane = lax.broadcasted_iota(jnp.int32, v.shape, 1)
    return jnp.where(lane % HD < HD // 2, pltpu.roll(v, LANE - HD // 2, 1), pltpu.roll(v, HD // 2, 1))


def _rope_fwd(name, proj, cos2, sin2):
    def body(p_ref, c_ref, s_ref, *outs):
        cv, sv = c_ref[...], s_ref[...]
        off = 0
        for o_ref, (_, w, rot, is_q) in zip(outs, GROUPS):
            for b in range(w // LANE):
                v = p_ref[:, off + b * LANE:off + (b + 1) * LANE]
                if rot:
                    v = v * cv + _rot_half(v) * sv
                if is_q:
                    v = v * (HD ** -0.5)
                o_ref[:, b * LANE:(b + 1) * LANE] = v.astype(BF16)
            off += w

    return pl.pallas_call(
        body, name=name, grid=(T // TR,), in_specs=[_rows(IN_COLS), _rows(LANE), _rows(LANE)],
        out_specs=[_rows(w) for _, w, _, _ in GROUPS],
        out_shape=[jax.ShapeDtypeStruct((T, w), BF16) for _, w, _, _ in GROUPS],
        compiler_params=_params(("parallel",)),
    )(proj, cos2, sin2)


def _rope_bwd(name, grads, cos2, sin2):
    def body(*refs):
        ins, (c_ref, s_ref, o_ref) = refs[:9], refs[9:]
        cv, sv = c_ref[...], s_ref[...]
        off = 0
        for d_ref, (_, w, rot, is_q) in zip(ins, GROUPS):
            for b in range(w // LANE):
                v = d_ref[:, b * LANE:(b + 1) * LANE]
                if is_q:
                    v = v * (HD ** -0.5)
                if rot:
                    v = v * cv + _rot_half(v * sv)
                o_ref[:, off + b * LANE:off + (b + 1) * LANE] = v.astype(BF16)
            off += w

    return pl.pallas_call(
        body, name=name, grid=(T // TR,), in_specs=[_rows(w) for _, w, _, _ in GROUPS] + [_rows(LANE), _rows(LANE)],
        out_specs=_rows(IN_COLS), out_shape=jax.ShapeDtypeStruct((T, IN_COLS), BF16),
        compiler_params=_params(("parallel",)),
    )(*grads, cos2, sin2)


NT_DIMS = (((1,), (1,)), ((), ()))
TN_DIMS = (((0,), (0,)), ((), ()))


def _scores(q, k, bias, valid):
    s = lax.dot_general(q, k, NT_DIMS, preferred_element_type=F32)
    if bias is not None:
        s = s + bias
    if valid is not None:
        s = jnp.where(valid, s, NEG)
    return s


def _heads_fwd(heads):
    scores = [_scores(h["q"], h["k"], h.get("bias"), h.get("valid")) for h in heads]
    soft = []
    for s, h in zip(scores, heads):
        m = jnp.max(s, axis=1, keepdims=True)
        e = jnp.exp(s - m)
        l = jnp.sum(e, axis=1, keepdims=True)
        if h.get("sink") is not None:
            l = l + jnp.exp(h["sink"] - m)
        soft.append((e.astype(BF16), l, m + jnp.log(l)))
    return [(jnp.dot(e, h["v"], preferred_element_type=F32) / l, lse) for (e, l, lse), h in zip(soft, heads)]


def _heads_bwd(heads):
    dobs = [h["do"].astype(BF16) for h in heads]
    scores = [_scores(h["q"], h["k"], h.get("bias"), h.get("valid")) for h in heads]
    dps = [lax.dot_general(dob, h["v"], NT_DIMS, preferred_element_type=F32) for dob, h in zip(dobs, heads)]
    mid = []
    for s, dp, h in zip(scores, dps, heads):
        p = jnp.exp(s - h["lse"])
        delta = jnp.sum(h["do"] * h["o"], axis=1, keepdims=True)
        ds = p * (dp - delta)
        dsink = None if h.get("sink") is None else -jnp.exp(h["sink"] - h["lse"]) * delta
        mid.append((p.astype(BF16), ds, dsink))
    out = []
    for (pb, ds, dsink), dob, h in zip(mid, dobs, heads):
        dsb = ds.astype(BF16)
        out.append((jnp.dot(dsb, h["k"], preferred_element_type=F32),
                    lax.dot_general(dsb, h["q"], TN_DIMS, preferred_element_type=F32),
                    lax.dot_general(pb, dob, TN_DIMS, preferred_element_type=F32), ds, dsink))
    return out


def _per_head(cols):
    return jnp.concatenate([jnp.broadcast_to(c, (c.shape[0], HD)) for c in cols], axis=1)


DILATIONS = ((128, 1), (512, 4), (2048, 16))


def _dilation_bias():
    def body(o_ref):
        t = pl.program_id(0) * TR + lax.broadcasted_iota(jnp.int32, (TR, T), 0)
        ad = jnp.abs(t - lax.broadcasted_iota(jnp.int32, (TR, T), 1))
        count = jnp.zeros((TR, T), jnp.int32)
        for window, r in DILATIONS:
            count += jnp.where(((ad & (r - 1)) == 0) & (ad <= window // 2), 1, 0)
        logs = jnp.where(count == 2, jnp.log(2.0), jnp.where(count == 3, jnp.log(3.0), 0.0)).astype(F32)
        o_ref[...] = jnp.where(count == 0, NEG, logs)

    return pl.pallas_call(
        body, name="dilation_bias", grid=(T // TR,), out_specs=_rows(T),
        out_shape=jax.ShapeDtypeStruct((T, T), F32), compiler_params=_params(("parallel",)),
    )()


BQ_A = 256


def _attn_a_fwd(name, qa, ka, va, bias):
    def body(q_ref, k_ref, v_ref, b_ref, o_ref, lse_ref):
        b = b_ref[...]
        outs = _heads_fwd([dict(q=q_ref[:, h * HD:(h + 1) * HD], k=k_ref[:, h * HD:(h + 1) * HD],
                                v=v_ref[:, h * HD:(h + 1) * HD], bias=b) for h in range(2)])
        o_ref[...] = jnp.concatenate([o for o, _ in outs], axis=1)
        lse_ref[...] = _per_head([lse for _, lse in outs])

    qs = pl.BlockSpec((BQ_A, LANE), lambda p, i: (i, p))
    ks = pl.BlockSpec((T, LANE), lambda p, i: (0, p))
    return pl.pallas_call(
        body, name=name, grid=(HA // 2, T // BQ_A),
        in_specs=[qs, ks, ks, pl.BlockSpec((BQ_A, T), lambda p, i: (i, 0))], out_specs=[qs, qs],
        out_shape=[jax.ShapeDtypeStruct((T, WA), F32)] * 2, compiler_params=_params(("parallel", "parallel")),
    )(qa, ka, va, bias)


def _attn_a_bwd(name, qa, ka, va, oa, lse, doa, bias):
    def body(q_ref, k_ref, v_ref, o_ref, lse_ref, do_ref, b_ref, dq_ref, dk_ref, dv_ref):
        b = b_ref[...]
        sls = [slice(h * HD, (h + 1) * HD) for h in range(2)]
        res = _heads_bwd([dict(q=q_ref[:, sl], k=k_ref[:, sl], v=v_ref[:, sl], o=o_ref[:, sl], do=do_ref[:, sl],
                               lse=lse_ref[:, sl.start:sl.start + 1], bias=b) for sl in sls])
        dq_ref[...] = jnp.concatenate([r[0] for r in res], axis=1)
        dk2, dv2 = jnp.concatenate([r[1] for r in res], axis=1), jnp.concatenate([r[2] for r in res], axis=1)

        @pl.when(pl.program_id(1) == 0)
        def _():
            dk_ref[...] = dk2
            dv_ref[...] = dv2

        @pl.when(pl.program_id(1) > 0)
        def _():
            dk_ref[...] += dk2
            dv_ref[...] += dv2

    qs = pl.BlockSpec((BQ_A, LANE), lambda p, i: (i, p))
    ks = pl.BlockSpec((T, LANE), lambda p, i: (0, p))
    return pl.pallas_call(
        body, name=name, grid=(HA // 2, T // BQ_A),
        in_specs=[qs, ks, ks, qs, qs, qs, pl.BlockSpec((BQ_A, T), lambda p, i: (i, 0))], out_specs=[qs, ks, ks],
        out_shape=[jax.ShapeDtypeStruct((T, WA), F32)] * 3, compiler_params=_params(("parallel", "arbitrary")),
    )(qa, ka, va, oa, lse, doa, bias)


BQ_B = 128
SPAN_B = BQ_B + 2 * WINDOW_B


def _window_b(i):
    start = pl.multiple_of(jnp.clip(i * BQ_B - WINDOW_B, 0, T - SPAN_B), BQ_B)
    qpos = i * BQ_B + lax.broadcasted_iota(jnp.int32, (BQ_B, SPAN_B), 0)
    kpos = start + lax.broadcasted_iota(jnp.int32, (BQ_B, SPAN_B), 1)
    return start, jnp.abs(qpos - kpos) <= WINDOW_B


GROUP_B = HB // HKV


def _stack_group(ref, g):
    return jnp.concatenate([ref[:, h * HD:(h + 1) * HD] for h in range(g * GROUP_B, (g + 1) * GROUP_B)], axis=0)


def _sink_column(sink_ref, g):
    return jnp.concatenate([jnp.full((BQ_B, 1), sink_ref[h], F32) for h in range(g * GROUP_B, (g + 1) * GROUP_B)],
                           axis=0)


def _unstack(stacked):
    return [s[j * BQ_B:(j + 1) * BQ_B] for s in stacked for j in range(GROUP_B)]


def _attn_b_fwd(name, qb, kb, vb, sink):
    def body(sink_ref, q_ref, k_ref, v_ref, o_ref, lse_ref):
        start, valid = _window_b(pl.program_id(0))
        valid = jnp.concatenate([valid] * GROUP_B, axis=0)
        kw, vw = k_ref[pl.ds(start, SPAN_B), :], v_ref[pl.ds(start, SPAN_B), :]
        outs = _heads_fwd([dict(q=_stack_group(q_ref, g), k=kw[:, g * HD:(g + 1) * HD], v=vw[:, g * HD:(g + 1) * HD],
                                valid=valid, sink=_sink_column(sink_ref, g)) for g in range(HKV)])
        o_ref[...] = jnp.concatenate(_unstack([o for o, _ in outs]), axis=1)
        lse_ref[...] = _per_head(_unstack([lse for _, lse in outs]))

    qs = pl.BlockSpec((BQ_B, WB), lambda i: (i, 0))
    return pl.pallas_call(
        body, name=name, grid=(T // BQ_B,),
        in_specs=[pl.BlockSpec(memory_space=pltpu.SMEM), qs, _whole((T, WKV)), _whole((T, WKV))],
        out_specs=[qs, qs],
        out_shape=[jax.ShapeDtypeStruct((T, WB), F32)] * 2, compiler_params=_params(("parallel",)),
    )(sink, qb, kb, vb)


def _attn_b_bwd(name, qb, kb, vb, ob, lse, dob, sink):
    def body(sink_ref, q_ref, k_ref, v_ref, o_ref, lse_ref, do_ref, dq_ref, dk_ref, dv_ref, dsink_ref):
        i = pl.program_id(0)
        start, valid = _window_b(i)
        valid = jnp.concatenate([valid] * GROUP_B, axis=0)
        kw, vw = k_ref[pl.ds(start, SPAN_B), :], v_ref[pl.ds(start, SPAN_B), :]
        res = _heads_bwd([dict(q=_stack_group(q_ref, g), k=kw[:, g * HD:(g + 1) * HD], v=vw[:, g * HD:(g + 1) * HD],
                               o=_stack_group(o_ref, g), do=_stack_group(do_ref, g),
                               lse=jnp.concatenate([lse_ref[:, h * HD:h * HD + 1]
                                                    for h in range(g * GROUP_B, (g + 1) * GROUP_B)], axis=0),
                               valid=valid, sink=_sink_column(sink_ref, g)) for g in range(HKV)])
        dks, dvs = [r[1] for r in res], [r[2] for r in res]
        lane = lax.broadcasted_iota(jnp.int32, (1, LANE), 1)
        dsink = jnp.zeros((1, LANE), F32)
        for h, rows in enumerate(_unstack([r[4] for r in res])):
            dsink += jnp.where(lane == h, jnp.sum(rows), 0.0)
        dq_ref[...] = jnp.concatenate(_unstack([r[0] for r in res]), axis=1)

        @pl.when(i == 0)
        def _():
            dk_ref[...] = jnp.zeros_like(dk_ref)
            dv_ref[...] = jnp.zeros_like(dv_ref)
            dsink_ref[...] = jnp.zeros_like(dsink_ref)

        dk_ref[pl.ds(start, SPAN_B), :] += jnp.concatenate(dks, axis=1)
        dv_ref[pl.ds(start, SPAN_B), :] += jnp.concatenate(dvs, axis=1)
        dsink_ref[...] += dsink

    qs = pl.BlockSpec((BQ_B, WB), lambda i: (i, 0))
    return pl.pallas_call(
        body, name=name, grid=(T // BQ_B,),
        in_specs=[pl.BlockSpec(memory_space=pltpu.SMEM), qs, _whole((T, WKV)), _whole((T, WKV)), qs, qs, qs],
        out_specs=[qs, _whole((T, WKV)), _whole((T, WKV)), _whole((1, LANE))],
        out_shape=[jax.ShapeDtypeStruct((T, WB), F32), jax.ShapeDtypeStruct((T, WKV), F32),
                   jax.ShapeDtypeStruct((T, WKV), F32), jax.ShapeDtypeStruct((1, LANE), F32)],
        compiler_params=_params(("arbitrary",)),
    )(sink, qb, kb, vb, ob, lse, dob)


SPAN_C = NA_ROWS * GRID_W


def _row_start(r):
    return jnp.clip(r - NA_ROWS // 2, 0, ROWS - NA_ROWS)


def _off_index(r):
    return _row_start(r) - r + (NA_ROWS - 1)


N_TAB = 16
RPS = 4


def _rpb_tables(name, rpb):
    circ = jnp.concatenate([rpb[..., NA_COLS - 1:], jnp.zeros(rpb.shape[:2] + (LANE - (2 * NA_COLS - 1),), F32),
                            rpb[..., :NA_COLS - 1]], axis=-1)
    circ = jnp.pad(circ, ((0, 0), (0, N_TAB + 1 - circ.shape[1]), (0, 0)))

    def body(w_ref, o_ref):
        c = lax.broadcasted_iota(jnp.int32, (GRID_W, LANE), 0)
        lane = lax.broadcasted_iota(jnp.int32, (GRID_W, LANE), 1)
        cs = jnp.clip(c - NA_COLS // 2, 0, GRID_W - NA_COLS)
        valid = (lane % GRID_W >= cs) & (lane % GRID_W < cs + NA_COLS)
        toep = [pltpu.roll(jnp.broadcast_to(w_ref[a:a + 1, :], (GRID_W, LANE)), 0, 1, stride=1, stride_axis=0)
                for a in range(N_TAB + 1)]
        for a in range(N_TAB):
            pair = jnp.where(lane < GRID_W, toep[a], pltpu.roll(toep[a + 1], GRID_W, 1))
            o_ref[a] = jnp.where(valid, pair, NEG)

    return pl.pallas_call(
        body, name=name, grid=(HC,),
        in_specs=[pl.BlockSpec((None, N_TAB + 1, LANE), lambda h: (h, 0, 0))],
        out_specs=pl.BlockSpec((None, N_TAB, GRID_W, LANE), lambda h: (h, 0, 0, 0)),
        out_shape=jax.ShapeDtypeStruct((HC, N_TAB, GRID_W, LANE), F32), compiler_params=_params(("parallel",)),
    )(circ)


def _bias_c(t_ref, h, d):
    return jnp.concatenate([t_ref[h, d + k] for k in range(0, NA_ROWS, 2)], axis=1)


def _attn_c_fwd(name, qc, kc, vc, tables):
    def body(q_ref, k_ref, v_ref, t_ref, o_ref, lse_ref):
        heads = []
        for rr in range(RPS):
            r = pl.program_id(1) * RPS + rr
            rows = slice(rr * GRID_W, (rr + 1) * GRID_W)
            start = pl.multiple_of(_row_start(r) * GRID_W, GRID_W)
            kw, vw = k_ref[pl.ds(start, SPAN_C), :], v_ref[pl.ds(start, SPAN_C), :]
            heads += [dict(q=q_ref[rows, h * HD:(h + 1) * HD], k=kw[:, h * HD:(h + 1) * HD], v=vw[:, h * HD:(h + 1) * HD],
                           bias=_bias_c(t_ref, h, _off_index(r))) for h in range(2)]
        outs = _heads_fwd(heads)
        for rr in range(RPS):
            rows = slice(rr * GRID_W, (rr + 1) * GRID_W)
            o_ref[rows, :] = jnp.concatenate([o for o, _ in outs[2 * rr:2 * rr + 2]], axis=1)
            lse_ref[rows, :] = _per_head([lse for _, lse in outs[2 * rr:2 * rr + 2]])

    qs = pl.BlockSpec((RPS * GRID_W, LANE), lambda p, r: (r, p))
    ks = pl.BlockSpec((T, LANE), lambda p, r: (0, p))
    ts = pl.BlockSpec((2, N_TAB, GRID_W, LANE), lambda p, r: (p, 0, 0, 0))
    return pl.pallas_call(
        body, name=name, grid=(HC // 2, ROWS // RPS), in_specs=[qs, ks, ks, ts], out_specs=[qs, qs],
        out_shape=[jax.ShapeDtypeStruct((T, WC), F32)] * 2, compiler_params=_params(("parallel", "parallel")),
    )(qc, kc, vc, tables)


def _attn_c_bwd(name, qc, kc, vc, oc, lse, doc, tables):
    def body(q_ref, k_ref, v_ref, o_ref, lse_ref, do_ref, t_ref, dq_ref, dk_ref, dv_ref, dt_ref):
        @pl.when(pl.program_id(1) == 0)
        def _():
            dk_ref[...] = jnp.zeros_like(dk_ref)
            dv_ref[...] = jnp.zeros_like(dv_ref)
            dt_ref[...] = jnp.zeros_like(dt_ref)

        heads, where = [], []
        for rr in range(RPS):
            r = pl.program_id(1) * RPS + rr
            rows = slice(rr * GRID_W, (rr + 1) * GRID_W)
            d = _off_index(r)
            start = pl.multiple_of(_row_start(r) * GRID_W, GRID_W)
            kw, vw = k_ref[pl.ds(start, SPAN_C), :], v_ref[pl.ds(start, SPAN_C), :]
            where.append((rows, d, start))
            for h in range(2):
                sl = slice(h * HD, (h + 1) * HD)
                heads.append(dict(q=q_ref[rows, sl], k=kw[:, sl], v=vw[:, sl], o=o_ref[rows, sl], do=do_ref[rows, sl],
                                  lse=lse_ref[rows, h * HD:h * HD + 1], bias=_bias_c(t_ref, h, d)))
        res = _heads_bwd(heads)
        for rr, (rows, d, start) in enumerate(where):
            pair = res[2 * rr:2 * rr + 2]
            for h in range(2):
                for k in range(0, NA_ROWS, 2):
                    dt_ref[h, d + k] += pair[h][3][:, k * GRID_W:(k + 2) * GRID_W]
            dq_ref[rows, :] = jnp.concatenate([p[0] for p in pair], axis=1)
            dk_ref[pl.ds(start, SPAN_C), :] += jnp.concatenate([p[1] for p in pair], axis=1)
            dv_ref[pl.ds(start, SPAN_C), :] += jnp.concatenate([p[2] for p in pair], axis=1)

    qs = pl.BlockSpec((RPS * GRID_W, LANE), lambda p, r: (r, p))
    ks = pl.BlockSpec((T, LANE), lambda p, r: (0, p))
    ts = pl.BlockSpec((2, N_TAB, GRID_W, LANE), lambda p, r: (p, 0, 0, 0))
    return pl.pallas_call(
        body, name=name, grid=(HC // 2, ROWS // RPS), in_specs=[qs, ks, ks, qs, qs, qs, ts],
        out_specs=[qs, ks, ks, ts],
        out_shape=[jax.ShapeDtypeStruct((T, WC), F32)] * 3 + [jax.ShapeDtypeStruct((HC, N_TAB, GRID_W, LANE), F32)],
        compiler_params=_params(("parallel", "arbitrary")),
    )(qc, kc, vc, oc, lse, doc, tables)


def _split3(v):
    hi = v.astype(BF16)
    r1 = v - hi.astype(F32)
    mid = r1.astype(BF16)
    lo = (r1 - mid.astype(F32)).astype(BF16)
    return hi, mid, lo


def _rpb_reduce(name, dtables):
    x = dtables.reshape(HC, N_TAB, GRID_W * LANE)
    c = jnp.arange(GRID_W)[:, None]
    lane = jnp.arange(LANE)[None, :]
    col = (lane // GRID_W) * LANE + jnp.clip(lane % GRID_W - c + (NA_COLS - 1), 0, 2 * NA_COLS - 2)
    col_onehot = (col.reshape(-1)[:, None] == jnp.arange(2 * LANE)[None, :]).astype(BF16)
    a2 = jnp.arange(N_TAB)[None, :]
    row_onehot = jnp.concatenate([(jnp.arange(16)[:, None] == a2 + u) & (a2 < 2 * NA_ROWS - 2) for u in range(2)],
                                 axis=1).astype(BF16)

    def body(x_ref, e_ref, f_ref, o_ref):
        y = sum(jnp.dot(part, e_ref[...], preferred_element_type=F32) for part in _split3(x_ref[...]))
        z = jnp.concatenate([y[:, :LANE], y[:, LANE:]], axis=0)
        o_ref[...] = sum(jnp.dot(f_ref[...], part, preferred_element_type=F32) for part in _split3(z))

    out = pl.pallas_call(
        body, name=name, grid=(HC,),
        in_specs=[pl.BlockSpec((None, N_TAB, GRID_W * LANE), lambda h: (h, 0, 0)),
                  _whole((GRID_W * LANE, 2 * LANE)), _whole((16, 2 * N_TAB))],
        out_specs=pl.BlockSpec((None, 16, LANE), lambda h: (h, 0, 0)),
        out_shape=jax.ShapeDtypeStruct((HC, 16, LANE), F32), compiler_params=_params(("parallel",)),
    )(x, col_onehot, row_onehot)
    return out[:, :2 * NA_ROWS - 1, :2 * NA_COLS - 1]


TC = 128
NCB = DFF // TC
CHUNK = 128
MARGIN = 8


def _shift_down(v, rows):
    return jnp.where(rows == 0, 0.0, pltpu.roll(v, 1, 0))


def _shift_up(v, rows):
    return jnp.where(rows == T - 1, 0.0, pltpu.roll(v, T - 1, 0))


def _conv(v, w, b, rows):
    return _shift_down(v, rows) * w[0:1] + v * w[1:2] + _shift_up(v, rows) * w[2:3] + b


def _ffn_specs():
    gate = lambda shape: pl.BlockSpec(shape, lambda j: (0, j))
    val = lambda shape: pl.BlockSpec(shape, lambda j: (0, j + NCB))
    return [gate((T, TC)), val((T, TC)), gate((3, TC)), val((3, TC)), gate((1, TC)), val((1, TC))]


def _ffn_mid_fwd(name, up, conv_w, conv_b):
    def body(xg_ref, xv_ref, wg_ref, wv_ref, bg_ref, bv_ref, o_ref):
        rows = lax.broadcasted_iota(jnp.int32, (T, TC), 0)
        ug = _conv(xg_ref[...], wg_ref[...], bg_ref[...], rows)
        uv = _conv(xv_ref[...], wv_ref[...], bv_ref[...], rows)
        o_ref[...] = (ug * jax.nn.sigmoid(ug) * uv).astype(BF16)

    return pl.pallas_call(
        body, name=name, grid=(NCB,), in_specs=_ffn_specs(), out_specs=pl.BlockSpec((T, TC), lambda j: (0, j)),
        out_shape=jax.ShapeDtypeStruct((T, DFF), BF16), compiler_params=_params(("parallel",)),
    )(up, up, conv_w, conv_w, conv_b, conv_b)


def _ffn_mid_bwd(name, dact, up, conv_w, conv_b):
    window = CHUNK + 2 * MARGIN
    centre = slice(MARGIN, MARGIN + CHUNK)

    def shifted(v):
        return pltpu.roll(v, 1, 0), pltpu.roll(v, window - 1, 0)

    def fold(v):
        return jnp.sum(v[centre].reshape(CHUNK // 8, 8, TC), axis=0)

    def body(da_ref, xg_ref, xv_ref, wg_ref, wv_ref, bg_ref, bv_ref, dx_ref, dw_ref, db_ref, dap, xgp, xvp):
        for src, pad in ((da_ref, dap), (xg_ref, xgp), (xv_ref, xvp)):
            pad[0:MARGIN, :] = jnp.zeros((MARGIN, TC), F32)
            pad[MARGIN:MARGIN + T, :] = src[...]
            pad[MARGIN + T:, :] = jnp.zeros((MARGIN, TC), F32)
        wg, wv, bg, bv = wg_ref[...], wv_ref[...], bg_ref[...], bv_ref[...]

        def chunk(c, sums):
            r0 = pl.multiple_of(c * CHUNK, CHUNK)
            da, xg, xv = dap[pl.ds(r0, window), :], xgp[pl.ds(r0, window), :], xvp[pl.ds(r0, window), :]
            xg_prev, xg_next = shifted(xg)
            xv_prev, xv_next = shifted(xv)
            ug = xg_prev * wg[0:1] + xg * wg[1:2] + xg_next * wg[2:3] + bg
            uv = xv_prev * wv[0:1] + xv * wv[1:2] + xv_next * wv[2:3] + bv
            sg = jax.nn.sigmoid(ug)
            dug = da * uv * (sg * (1.0 + ug * (1.0 - sg)))
            duv = da * (ug * sg)
            out = []
            for half, (x_prev, x, x_next, w, du) in enumerate(((xg_prev, xg, xg_next, wg, dug),
                                                               (xv_prev, xv, xv_next, wv, duv))):
                du_prev, du_next = shifted(du)
                dx = du_next * w[0:1] + du * w[1:2] + du_prev * w[2:3]
                dx_ref[half, pl.ds(r0, CHUNK), :] = dx[centre].astype(BF16)
                out += [fold(x_prev * du), fold(x * du), fold(x_next * du), fold(du)]
            return tuple(s + o for s, o in zip(sums, out))

        sums = lax.fori_loop(0, T // CHUNK, chunk, tuple(jnp.zeros((8, TC), F32) for _ in range(8)))
        rows = [jnp.sum(s, axis=0, keepdims=True) for s in sums]
        for half in range(2):
            dw_ref[half] = jnp.concatenate(rows[4 * half:4 * half + 3], axis=0)
            db_ref[half] = rows[4 * half + 3]

    return pl.pallas_call(
        body, name=name, grid=(NCB,), in_specs=[pl.BlockSpec((T, TC), lambda j: (0, j))] + _ffn_specs(),
        out_specs=[pl.BlockSpec((2, T, TC), lambda j: (0, 0, j)), pl.BlockSpec((2, 3, TC), lambda j: (0, 0, j)),
                   pl.BlockSpec((2, 1, TC), lambda j: (0, 0, j))],
        out_shape=[jax.ShapeDtypeStruct((2, T, DFF), BF16), jax.ShapeDtypeStruct((2, 3, DFF), F32),
                   jax.ShapeDtypeStruct((2, 1, DFF), F32)],
        scratch_shapes=[pltpu.VMEM((T + 2 * MARGIN, TC), F32)] * 3,
        compiler_params=_params(("parallel",)),
    )(dact, up, up, conv_w, conv_w, conv_b, conv_b)


def _dup_spec(tm, nj):
    per = DFF // nj
    return pl.BlockSpec((None, tm, nj), lambda a, b, j: (j // per, 0 if tm == T else b, j % per))


def _dup_spec_tn(tm, nj):
    per = DFF // nj
    return pl.BlockSpec((None, tm, nj), lambda j, kt, r: (j // per, 0, j % per))


def _adamw_math(w, g, m, v):
    m = ADAM_B1 * m + (1.0 - ADAM_B1) * g
    v = ADAM_B2 * v + (1.0 - ADAM_B2) * (g * g)
    m_hat = m / (1.0 - ADAM_B1 ** ADAM_STEP)
    v_hat = v / (1.0 - ADAM_B2 ** ADAM_STEP)
    delta = -ADAM_LR * (m_hat / (jnp.sqrt(v_hat) + ADAM_EPS) + ADAM_WD * w)
    return delta, m, v


ADAM_BLOCK = 256 * 1408


def _adamw_sharded(name, w, m, v, parts):
    _, r, c = w.shape
    tr = max(t for t in range(16, r + 1, 16) if r % t == 0 and t * c <= ADAM_BLOCK)

    def body(w_ref, m_ref, v_ref, p0_ref, p1_ref, g_ref, d_ref, nm_ref, nv_ref):
        def run(p_ref):
            g = p_ref[0].astype(F32)
            for k in range(1, N_DEV):
                g = g + p_ref[k].astype(F32)
            d, nm, nv = _adamw_math(w_ref[...], g, m_ref[...], v_ref[...])
            g_ref[...] = g
            d_ref[...] = d
            nm_ref[...] = nm
            nv_ref[...] = nv

        @pl.when(pl.program_id(0) == 0)
        def _():
            run(p0_ref)

        @pl.when(pl.program_id(0) == 1)
        def _():
            run(p1_ref)

    ws = pl.BlockSpec((None, tr, c), lambda l, i: (l, i, 0))
    p0 = pl.BlockSpec((N_DEV, tr, c), lambda l, i: (0, jnp.where(l == 0, i, r // tr - 1), 0))
    p1 = pl.BlockSpec((N_DEV, tr, c), lambda l, i: (0, jnp.where(l == 1, i, 0), 0))
    return pl.pallas_call(
        body, name=name, grid=(DEPTH, r // tr), in_specs=[ws, ws, ws, p0, p1], out_specs=[ws] * 4,
        out_shape=[jax.ShapeDtypeStruct(w.shape, F32)] * 4, compiler_params=_params(("arbitrary", "arbitrary")),
    )(w, m, v, *parts)


def _sum_devices(name, parts):
    r = parts.shape[1]

    def body(p_ref, o_ref):
        g = p_ref[0]
        for k in range(1, N_DEV):
            g = g + p_ref[k]
        o_ref[...] = g

    return pl.pallas_call(
        body, name=name, in_specs=[pl.BlockSpec((N_DEV, r, LANE), lambda: (0, 0, 0))],
        out_specs=pl.BlockSpec((r, LANE), lambda: (0, 0)), out_shape=jax.ShapeDtypeStruct((r, LANE), F32),
        compiler_params=_params(),
    )(parts)


def _adamw_small(name, ws, gs, ms, vs):
    n = len(ws)
    shapes = [w.shape for w in ws]
    ws, gs, ms, vs = ([a.reshape(1, -1) if a.ndim == 1 else a for a in arrs] for arrs in (ws, gs, ms, vs))
    specs = [pl.BlockSpec(memory_space=pltpu.VMEM)] * n

    def body(*refs):
        for i in range(n):
            w_ref, g_ref, m_ref, v_ref = (refs[k * n + i] for k in range(4))
            d, nm, nv = _adamw_math(w_ref[...], g_ref[...], m_ref[...], v_ref[...])
            refs[4 * n + i][...] = d
            refs[5 * n + i][...] = nm
            refs[6 * n + i][...] = nv

    outs = pl.pallas_call(
        body, name=name, in_specs=specs * 4, out_specs=specs * 3,
        out_shape=[jax.ShapeDtypeStruct(w.shape, F32) for w in ws] * 3, compiler_params=_params(),
    )(*ws, *gs, *ms, *vs)
    outs = [o.reshape(shapes[i % n]) for i, o in enumerate(outs)]
    return outs[:n], outs[n:2 * n], outs[2 * n:]


def _pack(arrays):
    flat = jnp.concatenate([a.reshape(-1) for a in arrays])
    pad = (-flat.shape[0]) % (8 * LANE)
    return jnp.pad(flat, (0, pad)).reshape(-1, LANE)


def _unpack(buf, shapes):
    flat, out, off = buf.reshape(-1), [], 0
    for s in shapes:
        n = 1
        for d in s:
            n *= d
        out.append(flat[off:off + n].reshape(s))
        off += n
    return out


def _local_step(x, target, small, weights, conv_w_full, hand_over, used):
    cos2, sin2 = _rope_tables()
    bias_a = _dilation_bias()
    tables = [_rpb_tables(f"rpb_tables_{l}", small["rpb_c"][l]) for l in range(DEPTH)]
    saved, carry = [], 0.0
    for l in range(DEPTH):
        g1, g2 = small["ln_attn"][l][None] + carry, small["ln_ffn"][l][None]
        gain, sink, cb = small["mix_gain"][l][None], small["sink_b"][l], small["conv_b"][l][None]
        cw = conv_w_full[l]
        bias = tables[l]
        h1, proj = _prologue_matmul(f"proj_in_{l}", _rmsnorm_rows, [x, g1], [D, None],
                                    weights("w_in", l, [cos2, sin2, bias_a] + tables if l == 0 else x),
                                    (D, 1024), lambda j: (0, j), 1024)
        zero = used(f"proj_in_{l}", proj)
        qa, ka, va, qb, kb, vb, qc, kc, vc = _rope_fwd(f"rope_{l}", proj, cos2, sin2)
        oa, lse_a = _attn_a_fwd(f"attn_a_{l}", qa, ka, va, bias_a)
        ob, lse_b = _attn_b_fwd(f"attn_b_{l}", qb, kb, vb, sink + zero)
        oc, lse_c = _attn_c_fwd(f"attn_c_{l}", qc, kc, vc, bias)
        mixed, x_mid = _prologue_matmul(f"proj_out_{l}", _mix_rows, [oa, ob, oc, gain + used(f"attn_{l}", oc)],
                                        [WA, WB, WC, None],
                                        weights("w_out", l, oc), (N_DEV, D // N_DEV, 512), lambda j: (0, 0, j), 512,
                                        res=x)
        h2, up = _prologue_matmul(f"ffn_up_{l}", _rmsnorm_rows, [x_mid, g2 + used(f"proj_out_{l}", x_mid)], [D, None],
                                  weights("w_up", l, x_mid), (D, 1024), lambda j: (0, j), 1024)
        act = _ffn_mid_fwd(f"ffn_mid_{l}", up, cw, cb + used(f"ffn_up_{l}", up))
        x_out = _nn_rows(f"ffn_down_{l}", act, weights("w_down", l, act), x_mid, 4, 1024, 1024)
        carry = used(f"ffn_down_{l}", x_out)
        saved.append(dict(x=x, h1=h1, qkv=(qa, ka, va, qb, kb, vb, qc, kc, vc), o=(oa, ob, oc), lse=(lse_a, lse_b, lse_c), mixed=mixed,
                          x_mid=x_mid, h2=h2, up=up, act=act, g1=g1, g2=g2, gain=gain, sink=sink, cb=cb, cw=cw, bias=bias))
        x = x_out

    loss8, dx, dxb, d_ln_final = _loss_head(x, small["ln_final"][None], target)
    sgrads = [None] * DEPTH
    for l in reversed(range(DEPTH)):
        s = saved[l]
        qa, ka, va, qb, kb, vb, qc, kc, vc = s["qkv"]
        oa, ob, oc = s["o"]
        wg_in, wg_out = weights("w_in", l, None), weights("w_out", l, None)
        wg_up, wg_down = weights("w_up", l, None), weights("w_down", l, None)
        g_down = _tn_rows(f"wgrad_down_{l}", s["act"], dxb, wg_down.shape[1], 2, 512)
        zero = hand_over("w_down", l, g_down)
        dact = _nt_rows(f"dgrad_down_{l}", dxb, wg_down, 4, 512)
        dup, d_cw, d_cb = _ffn_mid_bwd(f"ffn_mid_bwd_{l}", dact, s["up"], s["cw"], s["cb"] + zero)
        g_up = _tn_cols(f"wgrad_up_{l}", s["h2"], dup, _dup_spec_tn, 2 * DFF, DFF // 2)
        zero = hand_over("w_up", l, g_up)
        dh2 = _nt_cols(f"dgrad_up_{l}", dup, _dup_spec, wg_up, DFF // 2)
        dx, dxb, d_g2 = _rmsnorm_bwd(f"norm_ffn_bwd_{l}", dh2, s["x_mid"], s["g2"] + zero, dx)
        g_out = _tn_rows(f"wgrad_out_{l}", s["mixed"], dxb, wg_out.shape[1], 2, D)
        zero = hand_over("w_out", l, g_out)
        dmixed = _nt_rows(f"dgrad_out_{l}", dxb, wg_out, 2, T)
        doa, dob, doc, d_gain = _mix_bwd(f"mix_bwd_{l}", dmixed, oa, ob, oc, s["gain"] + zero)
        lse_a, lse_b, lse_c = s["lse"]
        dqa, dka, dva = _attn_a_bwd(f"attn_a_bwd_{l}", qa, ka, va, oa, lse_a, doa, bias_a)
        dqb, dkb, dvb, d_sink = _attn_b_bwd(f"attn_b_bwd_{l}", qb, kb, vb, ob, lse_b, dob, s["sink"])
        dqc, dkc, dvc, d_bias = _attn_c_bwd(f"attn_c_bwd_{l}", qc, kc, vc, oc, lse_c, doc, s["bias"])
        d_rpb = _rpb_reduce(f"rpb_reduce_{l}", d_bias)
        dproj = _rope_bwd(f"rope_bwd_{l}", (dqa, dka, dva, dqb, dkb, dvb, dqc, dkc, dvc), cos2, sin2)
        g_in = _tn_cols(f"wgrad_in_{l}", s["h1"], dproj,
                        lambda tm, tn: pl.BlockSpec((tm, tn), lambda j, kt, r: (0, j)), IN_COLS, 1024)
        zero = hand_over("w_in", l, g_in)
        dh1 = _nt_cols(f"dgrad_in_{l}", dproj, lambda tm, nc: pl.BlockSpec((tm, nc), lambda kt, i, j: (i, j)), wg_in,
                       IN_COLS // 2)
        dx, dxb, d_g1 = _rmsnorm_bwd(f"norm_attn_bwd_{l}", dh1, s["x"], s["g1"] + zero, dx)
        sgrads[l] = dict(ln_attn=d_g1[0], sink_b=d_sink[0, :HB], rpb_c=d_rpb, mix_gain=d_gain[0], ln_ffn=d_g2[0],
                         conv_w=d_cw.transpose(1, 0, 2).reshape(3, 2 * DFF), conv_b=d_cb.reshape(2 * DFF))
    return loss8[0, 0], dx, d_ln_final[0], sgrads


SMALL_NAMES = ("ln_attn", "sink_b", "rpb_c", "mix_gain", "ln_ffn", "conv_b")


def kernel(x, ln_attn, w_in, sink_b, rpb_c, mix_gain, w_out, ln_ffn, w_up, conv_w, conv_b, w_down, ln_final, loss_target, m_ln_attn, m_w_in, m_sink_b, m_rpb_c, m_mix_gain, m_w_out, m_ln_ffn, m_w_up, m_conv_w, m_conv_b, m_w_down, m_ln_final, v_ln_attn, v_w_in, v_sink_b, v_rpb_c, v_mix_gain, v_w_out, v_ln_ffn, v_w_up, v_conv_w, v_conv_b, v_w_down, v_ln_final):
    me = 4 * lax.axis_index("x") + 2 * lax.axis_index("y") + lax.axis_index("c")
    small = dict(ln_attn=ln_attn, sink_b=sink_b, rpb_c=rpb_c, mix_gain=mix_gain, ln_ffn=ln_ffn, conv_b=conv_b,
                 ln_final=ln_final)

    names = ("w_in", "w_out", "w_up", "w_down")
    shards = dict(w_in=w_in, w_out=w_out, w_up=w_up, w_down=w_down)
    order = [(n, l) for l in range(DEPTH) for n in names]
    conv_key = ("conv_w", 0)
    started, arrived, forwarded, gathered = {}, {}, {}, {}

    def side_by_side(k):
        return k[0] in ("w_in", "w_up")

    def slot_of(k):
        return _col_slot(shards[k[0]].shape[2]) if side_by_side(k) else _lead_slot

    def begin(name, ks, zero):
        srcs = [_pack([conv_w]) + zero if k == conv_key else (shards[k[0]][k[1]] + zero).astype(BF16) for k in ks]
        lands = [lax.empty((s.shape[0], N_DEV * s.shape[1]) if side_by_side(k) else (N_DEV,) + s.shape, s.dtype)
                 for k, s in zip(ks, srcs)]
        peers = [ALL_PEERS if k == conv_key else NEAR_PEERS for k in ks]
        send, recv, bufs, tok = _copy_start(name, srcs + lands, _gather_plan(peers, [slot_of(k) for k in ks]),
                                            [len(p) + 1 for p in peers])
        for i, k in enumerate(ks):
            started[k] = (send[i], recv[i], bufs[i], bufs[len(ks) + i], peers[i])
        return tok

    token = begin("gather_start_first", order[:1], 0.0)
    token = begin("gather_start_rest", [conv_key] + order[1:], token[0, 0])

    def arrive(k, after):
        send, recv, src, land, peers = started[k]
        arrived[k] = _copy_wait(f"gather_{k[0]}_{k[1]}_arrived", [src, land], [send], [recv],
                                _gather_plan([peers], [slot_of(k)]), after)

    queue = list(order)

    def advance(after):
        if not queue:
            return 0.0
        k = queue.pop(0)
        arrive(k, after)
        forwarded[k] = _copy_start(f"gather_{k[0]}_{k[1]}_forward", [arrived[k][1]], _forward_plan(slot_of(k)),
                                   [len(OTHER_CHIPS)])
        return forwarded[k][3][0, 0]

    pass_on_behind = ("proj_in_0", "attn_0", "proj_out_0", "ffn_up_0", "ffn_down_0", "attn_1", "proj_out_1")

    def used(point, result):
        return advance(result) if point in pass_on_behind else 0.0

    def weights(n, l, after):
        k = (n, l)
        if k not in gathered:
            if k not in forwarded:
                advance(after)
            send_b, recv_b, (land,), _ = forwarded[k]
            (gathered[k],) = _copy_wait(f"gather_{n}_{l}_done", [land], send_b, recv_b, _forward_plan(slot_of(k)),
                                        after)
        return gathered[k]

    pending = {}

    def hand_over(n, l, g):
        shard = shards[n].shape[1:]
        send, recv, bufs, tok = _copy_start(f"send_grad_{n}_{l}", [g, lax.empty((N_DEV,) + shard, g.dtype)],
                                            _scatter_plan(slot_of((n, l))), [len(ALL_PEERS) + 1])
        pending[(n, l)] = (send, recv, bufs)
        return tok[0, 0]

    def received(k, after):
        send, recv, bufs = pending[k]
        return _copy_wait(f"recv_grad_{k[0]}_{k[1]}", bufs, send, recv, _scatter_plan(slot_of(k)), after)[1]

    arrive(conv_key, token)
    cw_all = arrived[conv_key][1]
    nup = w_up.shape[2]
    cw_shards = cw_all.reshape(N_DEV, -1)[:, :DEPTH * 3 * nup].reshape(N_DEV, DEPTH, 3, nup)
    conv_w_full = cw_shards.transpose(1, 2, 0, 3).reshape(DEPTH, 3, N_DEV * nup)

    loss_local, dx, d_ln_final, sgrads = _local_step(
        x[0], loss_target[0], dict(small, ln_attn=ln_attn + token[0, 0]), weights, conv_w_full, hand_over, used)

    stacked = [jnp.stack([sgrads[l][n] for l in range(DEPTH)]) for n in SMALL_NAMES + ("conv_w",)] + [d_ln_final]
    shapes = [a.shape for a in stacked]
    mine = _pack(stacked)
    send_s, recv_s, bufs_s, _ = _copy_start("gather_small_grads_start", [mine, lax.empty((N_DEV,) + mine.shape, F32)],
                                            _gather_plan([ALL_PEERS], [_lead_slot]), [len(ALL_PEERS) + 1])

    big, after = {}, dx
    moments = dict(w_in=(m_w_in, v_w_in), w_out=(m_w_out, v_w_out), w_up=(m_w_up, v_w_up), w_down=(m_w_down, v_w_down))
    for n in reversed(names):
        parts = (received((n, 0), after), received((n, 1), after))
        big[n] = _adamw_sharded(f"adamw_{n}", shards[n], *moments[n], parts)
        after = big[n][1]

    _, everyone = _copy_wait("gather_small_grads_done", bufs_s, send_s, recv_s,
                             _gather_plan([ALL_PEERS], [_lead_slot]), after)
    g_small = _unpack(_sum_devices("sum_small_grads", everyone), shapes)
    g = dict(zip(SMALL_NAMES + ("conv_w", "ln_final"), g_small))
    g["conv_w"] = lax.dynamic_slice_in_dim(g["conv_w"], me * nup, nup, axis=2)

    snames = SMALL_NAMES + ("conv_w", "ln_final")
    sw = dict(small, conv_w=conv_w)
    sm = dict(ln_attn=m_ln_attn, sink_b=m_sink_b, rpb_c=m_rpb_c, mix_gain=m_mix_gain, ln_ffn=m_ln_ffn,
              conv_b=m_conv_b, conv_w=m_conv_w, ln_final=m_ln_final)
    sv = dict(ln_attn=v_ln_attn, sink_b=v_sink_b, rpb_c=v_rpb_c, mix_gain=v_mix_gain, ln_ffn=v_ln_ffn,
              conv_b=v_conv_b, conv_w=v_conv_w, ln_final=v_ln_final)
    s_delta, s_m, s_v = (dict(zip(snames, out)) for out in _adamw_small(
        "adamw_small", [sw[n] for n in snames], [g[n] for n in snames], [sm[n] for n in snames],
        [sv[n] for n in snames]))

    loss = lax.psum(loss_local, ("x", "y", "c"))
    outputs = ("ln_attn", "w_in", "sink_b", "rpb_c", "mix_gain", "w_out", "ln_ffn", "w_up", "conv_w", "conv_b",
               "w_down", "ln_final")
    grads = [big[n][0] if n in big else g[n] for n in outputs]
    deltas = [big[n][1] if n in big else s_delta[n] for n in outputs]
    new_m = [big[n][2] if n in big else s_m[n] for n in outputs]
    new_v = [big[n][3] if n in big else s_v[n] for n in outputs]
    return (loss, dx[None], *grads, *deltas, *new_m, *new_v)
```

```python
import functools

import jax
import jax.numpy as jnp
from jax import lax
from jax.experimental import pallas as pl
from jax.experimental.pallas import tpu as pltpu

F32 = jnp.float32
BF16 = jnp.bfloat16

N_DEV = 8
T = 2048
D = 2048
DEPTH = 2
HD = 64
HA, HB, HKV, HC = 12, 10, 2, 10
WA, WB, WKV, WC = HA * HD, HB * HD, HKV * HD, HC * HD
IN_COLS = 3 * WA + WB + 2 * WKV + 3 * WC
DFF = 5632
GRID_W = 64
ROWS = T // GRID_W
NA_ROWS, NA_COLS = 8, 16
WINDOW_B = 128
EPS = 1e-6
NEG = -1e30
ROPE_THETA = 10000.0
LANE = 128
VMEM_LIMIT = 56 * 1024 * 1024

ADAM_LR, ADAM_B1, ADAM_B2, ADAM_EPS, ADAM_WD, ADAM_STEP = 0.001, 0.9, 0.999, 1e-08, 0.01, 10

GROUPS = (("qa", WA, True, True), ("ka", WA, True, False), ("va", WA, False, False),
          ("qb", WB, True, True), ("kb", WKV, True, False), ("vb", WKV, False, False),
          ("qc", WC, False, True), ("kc", WC, False, False), ("vc", WC, False, False))


def _params(sem=None):
    return pltpu.CompilerParams(dimension_semantics=sem, vmem_limit_bytes=VMEM_LIMIT)


HBM_SPEC = pl.BlockSpec(memory_space=pltpu.HBM)
SEM_SPEC = pl.BlockSpec(memory_space=pltpu.SEMAPHORE)
DATAFLOW = pltpu.SideEffectType.DATAFLOW_SIDE_EFFECTING


ALL_PEERS = tuple((p >> 2 & 1, p >> 1 & 1, p & 1) for p in range(1, N_DEV))
OTHER_CHIPS = ((1, 0, 0), (0, 1, 0), (1, 1, 0))
NEAR_PEERS = ((0, 0, 1),) + OTHER_CHIPS


def _flip(x, y, c, f):
    return (1 - x if f[0] else x, 1 - y if f[1] else y, 1 - c if f[2] else c)


def _index(pos):
    return 4 * pos[0] + 2 * pos[1] + pos[2]


class _LocalCopy:
    def __init__(self, src, dst, sem):
        self.copy = pltpu.make_async_copy(src, dst, sem)

    def start(self):
        self.copy.start()

    def wait_send(self):
        self.copy.wait()

    def wait_recv(self):
        pass


def _descriptors(plan, bufs, send_sems, recv_sems):
    x, y, c = lax.axis_index("x"), lax.axis_index("y"), lax.axis_index("c")
    return [_LocalCopy(src, dst, send_sems[g].at[i]) if partner is None else
            pltpu.make_async_remote_copy(src_ref=src, dst_ref=dst, send_sem=send_sems[g].at[i],
                                         recv_sem=recv_sems[g].at[i], device_id=partner,
                                         device_id_type=pl.DeviceIdType.MESH)
            for g, copies in enumerate(plan(bufs, x, y, c)) for i, (src, dst, partner) in enumerate(copies)]


def _copy_start(name, bufs, plan, sizes):
    nb, ng = len(bufs), len(sizes)

    def body(*refs):
        for d in _descriptors(plan, refs[:nb], refs[nb:nb + ng], refs[nb + ng:nb + 2 * ng]):
            d.start()
        refs[2 * nb + 2 * ng][...] = jnp.zeros((8, LANE), F32)

    outs = pl.pallas_call(
        body, name=name,
        out_shape=[pltpu.SemaphoreType.DMA((s,)) for s in sizes] * 2 + [pltpu.HBM(b.shape, b.dtype) for b in bufs]
        + [jax.ShapeDtypeStruct((8, LANE), F32)],
        in_specs=[HBM_SPEC] * nb,
        out_specs=[SEM_SPEC] * (2 * ng) + [HBM_SPEC] * nb + [pl.BlockSpec(memory_space=pltpu.VMEM)],
        input_output_aliases={i: 2 * ng + i for i in range(nb)},
        compiler_params=pltpu.CompilerParams(has_side_effects=DATAFLOW),
    )(*[pltpu.with_memory_space_constraint(b, pltpu.HBM) for b in bufs])
    return outs[:ng], outs[ng:2 * ng], outs[2 * ng:2 * ng + nb], outs[2 * ng + nb]


def _copy_wait(name, bufs, send_sems, recv_sems, plan, after):
    nb, ng = len(bufs), len(send_sems)
    after = list(after) if isinstance(after, (list, tuple)) else [after]

    def body(*refs):
        for d in _descriptors(plan, refs[:nb], refs[nb:nb + ng], refs[nb + ng:nb + 2 * ng]):
            d.wait_send()
            d.wait_recv()

    return pl.pallas_call(
        body, name=name, out_shape=[pltpu.HBM(b.shape, b.dtype) for b in bufs],
        in_specs=[HBM_SPEC] * nb + [SEM_SPEC] * (2 * ng) + [pl.BlockSpec(memory_space=pl.ANY)] * len(after),
        out_specs=[HBM_SPEC] * nb, input_output_aliases={i: i for i in range(nb)},
        compiler_params=pltpu.CompilerParams(has_side_effects=DATAFLOW),
    )(*bufs, *send_sems, *recv_sems, *after)


def _lead_slot(ref, k):
    return ref.at[k]


def _col_slot(width):
    return lambda ref, k: ref.at[:, pl.ds(pl.multiple_of(k * width, LANE), width)]


def _gather_plan(peer_sets, slots):
    def plan(bufs, x, y, c):
        n = len(peer_sets)
        return [[(bufs[i], slots[i](bufs[n + i], _index((x, y, c))), _flip(x, y, c, f)) for f in peers]
                + [(bufs[i], slots[i](bufs[n + i], _index((x, y, c))), None)] for i, peers in enumerate(peer_sets)]
    return plan


def _forward_plan(slot):
    def plan(bufs, x, y, c):
        pieces = [slot(bufs[0], _index(_flip(x, y, c, f))) for f in OTHER_CHIPS]
        return [[(p, p, _flip(x, y, c, (0, 0, 1))) for p in pieces]]
    return plan


def _scatter_plan(slot):
    def plan(bufs, x, y, c):
        me = _index((x, y, c))
        peers = [_flip(x, y, c, f) for f in ALL_PEERS]
        return [[(slot(bufs[0], _index(p)), bufs[1].at[me], p) for p in peers]
                + [(slot(bufs[0], me), bufs[1].at[me], None)]]
    return plan


def _flat2(v):
    return v.reshape(-1, v.shape[-1])


def _matmul(name, kind, a, a_spec, b, b_spec, out_shape, out_spec, grid, res=None, res_spec=None, acc_shape=None):
    dims = {"nn": (((1,), (0,)), ((), ())), "nt": NT_DIMS, "nts": NT_DIMS, "tn": (((0,), (0,)), ((), ()))}[kind]
    nred = grid[-1]

    def body(*refs):
        if res is None:
            a_ref, b_ref, o_ref = refs[:3]
            r_ref = None
        else:
            a_ref, b_ref, r_ref, o_ref = refs[:4]
        if kind == "nts":
            n = b_ref.shape[-1]
            part = sum(lax.dot_general(a_ref[:, blk * n:(blk + 1) * n], b_ref[blk], dims, preferred_element_type=F32)
                       for blk in range(b_ref.shape[0]))
        else:
            part = lax.dot_general(_flat2(a_ref[...]), _flat2(b_ref[...]), dims, preferred_element_type=F32)

        def finish(total):
            if r_ref is not None:
                total = total + r_ref[...]
            o_ref[...] = total.reshape(o_ref.shape).astype(o_ref.dtype)

        if nred == 1:
            finish(part)
        else:
            acc_ref = refs[-1]
            k = pl.program_id(len(grid) - 1)

            @pl.when(k == 0)
            def _():
                acc_ref[...] = part

            @pl.when(jnp.logical_and(k > 0, k < nred - 1))
            def _():
                acc_ref[...] += part

            @pl.when(k == nred - 1)
            def _():
                finish(acc_ref[...] + part)

    ins, specs = [a, b], [a_spec, b_spec]
    if res is not None:
        ins.append(res)
        specs.append(res_spec)
    scratch = [] if nred == 1 else [pltpu.VMEM(acc_shape, F32)]
    return pl.pallas_call(
        body, name=name, grid=grid, in_specs=specs, out_specs=out_spec, out_shape=out_shape, scratch_shapes=scratch,
        compiler_params=_params(("parallel",) * (len(grid) - 1) + ("arbitrary",)),
    )(*ins)


def _nn_rows(name, a, wg, res, s, tn, tm):
    _, kj, n = wg.shape
    return _matmul(
        name, "nn", a, pl.BlockSpec((tm, s * kj), lambda j, i, r: (i, r)),
        wg, pl.BlockSpec((s, kj, tn), lambda j, i, r: (r, 0, j)),
        jax.ShapeDtypeStruct((T, n), F32), pl.BlockSpec((tm, tn), lambda j, i, r: (i, j)),
        (n // tn, T // tm, N_DEV // s), res=res, res_spec=pl.BlockSpec((tm, tn), lambda j, i, r: (i, j)),
        acc_shape=(tm, tn))


def _nt_cols(name, dc, dc_spec_of, w, nc):
    k, n = w.shape
    tm = tk = 1024
    return _matmul(
        name, "nt", dc, dc_spec_of(tm, nc),
        w, pl.BlockSpec((tk, nc), lambda kt, i, j: (kt, j)),
        jax.ShapeDtypeStruct((T, k), F32), pl.BlockSpec((tm, tk), lambda kt, i, j: (i, kt)),
        (k // tk, T // tm, n // nc), acc_shape=(tm, tk))


def _nt_rows(name, dc, wg, s, tm):
    _, kj, n = wg.shape
    return _matmul(
        name, "nt", dc, pl.BlockSpec((tm, n), lambda kt, i, r: (i, 0)),
        wg, pl.BlockSpec((s, kj, n), lambda kt, i, r: (kt, 0, 0)),
        jax.ShapeDtypeStruct((T, N_DEV * kj), F32), pl.BlockSpec((tm, s * kj), lambda kt, i, r: (i, kt)),
        (N_DEV // s, T // tm, 1))


def _tn_cols(name, a, dc, dc_spec_of, n, tn):
    k = a.shape[1]
    tk = 512
    return _matmul(
        name, "tn", a, pl.BlockSpec((T, tk), lambda j, kt, r: (0, kt)),
        dc, dc_spec_of(T, tn),
        jax.ShapeDtypeStruct((k, n), BF16), pl.BlockSpec((tk, tn), lambda j, kt, r: (kt, j)),
        (n // tn, k // tk, 1))


def _tn_rows(name, a, dc, kj, s, tn):
    n = dc.shape[1]
    return _matmul(
        name, "tn", a, pl.BlockSpec((T, s * kj), lambda kt, j, r: (0, kt)),
        dc, pl.BlockSpec((T, tn), lambda kt, j, r: (0, j)),
        jax.ShapeDtypeStruct((N_DEV, kj, n), BF16), pl.BlockSpec((s, kj, tn), lambda kt, j, r: (kt, 0, j)),
        (N_DEV // s, n // tn, 1))


TR = 256


def _rows(width):
    return pl.BlockSpec((TR, width), lambda i: (i, 0))


def _whole(shape):
    return pl.BlockSpec(shape, lambda i: (0,) * len(shape))


def _rmsnorm_rows(x_ref, g_ref):
    xv = x_ref[...]
    r = lax.rsqrt(jnp.mean(xv * xv, axis=-1, keepdims=True) + EPS)
    return ((xv * r) * g_ref[...]).astype(BF16)


def _prologue_matmul(name, prologue, ins, widths, w, w_block, w_index, tn, res=None):
    tm = 1024
    n = w.shape[-1]
    ni = len(ins)

    def body(*refs):
        w_ref = refs[ni]
        r_ref = refs[ni + 1] if res is not None else None
        h_ref, o_ref, h_scr = refs[-3:]

        @pl.when(pl.program_id(1) == 0)
        def _():
            h = prologue(*refs[:ni])
            h_scr[...] = h
            h_ref[...] = h

        total = jnp.dot(h_scr[...], _flat2(w_ref[...]), preferred_element_type=F32)
        if r_ref is not None:
            total = total + r_ref[...]
        o_ref[...] = total

    tile = pl.BlockSpec((tm, tn), lambda i, j: (i, j))
    specs = [pl.BlockSpec((1, D), lambda i, j: (0, 0)) if wd is None else pl.BlockSpec((tm, wd), lambda i, j: (i, 0))
             for wd in widths]
    specs.append(pl.BlockSpec(w_block, lambda i, j: w_index(j)))
    operands = list(ins) + [w]
    if res is not None:
        specs.append(tile)
        operands.append(res)
    return pl.pallas_call(
        body, name=name, grid=(T // tm, n // tn), in_specs=specs,
        out_specs=[pl.BlockSpec((tm, D), lambda i, j: (i, 0)), tile],
        out_shape=[jax.ShapeDtypeStruct((T, D), BF16), jax.ShapeDtypeStruct((T, n), F32)],
        scratch_shapes=[pltpu.VMEM((tm, D), BF16)], compiler_params=_params(("parallel", "arbitrary")),
    )(*operands)


def _rms_bwd_math(dy, xv, g):
    r = lax.rsqrt(jnp.mean(xv * xv, axis=-1, keepdims=True) + EPS)
    xhat = xv * r
    dxhat = dy * g
    dx = r * (dxhat - xhat * jnp.mean(dxhat * xhat, axis=-1, keepdims=True))
    return dx, dy * xhat


def _accumulate(ref, val):
    @pl.when(pl.program_id(0) == 0)
    def _():
        ref[...] = val

    @pl.when(pl.program_id(0) > 0)
    def _():
        ref[...] += val


def _rmsnorm_bwd(name, dy, x, g, res):
    def body(dy_ref, x_ref, g_ref, res_ref, dx_ref, dxb_ref, dg_ref):
        dx, dgr = _rms_bwd_math(dy_ref[...], x_ref[...], g_ref[...])
        tot = res_ref[...] + dx
        dx_ref[...] = tot
        dxb_ref[...] = tot.astype(BF16)
        _accumulate(dg_ref, jnp.sum(dgr, axis=0, keepdims=True))

    return pl.pallas_call(
        body, name=name, grid=(T // TR,), in_specs=[_rows(D), _rows(D), _whole((1, D)), _rows(D)],
        out_specs=[_rows(D), _rows(D), _whole((1, D))],
        out_shape=[jax.ShapeDtypeStruct((T, D), F32), jax.ShapeDtypeStruct((T, D), BF16),
                   jax.ShapeDtypeStruct((1, D), F32)],
        compiler_params=_params(("arbitrary",)),
    )(dy, x, g, res)


def _loss_head(x, g, target):
    def body(x_ref, g_ref, t_ref, loss_ref, dx_ref, dxb_ref, dg_ref):
        xv, gv = x_ref[...], g_ref[...]
        r = lax.rsqrt(jnp.mean(xv * xv, axis=-1, keepdims=True) + EPS)
        err = (xv * r) * gv - t_ref[...]
        part = 0.5 * jnp.sum(jnp.mean(err * err, axis=-1, keepdims=True))
        dx, dgr = _rms_bwd_math(err * (1.0 / D), xv, gv)
        dx_ref[...] = dx
        dxb_ref[...] = dx.astype(BF16)
        _accumulate(dg_ref, jnp.sum(dgr, axis=0, keepdims=True))
        _accumulate(loss_ref, jnp.full((8, LANE), part, F32))

    return pl.pallas_call(
        body, name="loss_head", grid=(T // TR,), in_specs=[_rows(D), _whole((1, D)), _rows(D)],
        out_specs=[_whole((8, LANE)), _rows(D), _rows(D), _whole((1, D))],
        out_shape=[jax.ShapeDtypeStruct((8, LANE), F32), jax.ShapeDtypeStruct((T, D), F32),
                   jax.ShapeDtypeStruct((T, D), BF16), jax.ShapeDtypeStruct((1, D), F32)],
        compiler_params=_params(("arbitrary",)),
    )(x, g, target)


MIX_OFFS = ((0, WA), (WA, WB), (WA + WB, WC))


def _mix_rows(oa_ref, ob_ref, oc_ref, g_ref):
    parts = []
    for ref, (off, w) in zip((oa_ref, ob_ref, oc_ref), MIX_OFFS):
        o = ref[...]
        r = lax.rsqrt(jnp.mean(o * o, axis=-1, keepdims=True) + EPS)
        parts.append(((o * r) * g_ref[:, off:off + w]).astype(BF16))
    return jnp.concatenate(parts, axis=1)


def _mix_bwd(name, dmixed, oa, ob, oc, gain):
    def body(dm_ref, oa_ref, ob_ref, oc_ref, g_ref, doa_ref, dob_ref, doc_ref, dg_ref):
        dgs = []
        for ref, dref, (off, w) in zip((oa_ref, ob_ref, oc_ref), (doa_ref, dob_ref, doc_ref), MIX_OFFS):
            dx, dgr = _rms_bwd_math(dm_ref[:, off:off + w], ref[...], g_ref[:, off:off + w])
            dref[...] = dx
            dgs.append(jnp.sum(dgr, axis=0, keepdims=True))
        _accumulate(dg_ref, jnp.concatenate(dgs, axis=1))

    return pl.pallas_call(
        body, name=name, grid=(T // TR,),
        in_specs=[_rows(D), _rows(WA), _rows(WB), _rows(WC), _whole((1, D))],
        out_specs=[_rows(WA), _rows(WB), _rows(WC), _whole((1, D))],
        out_shape=[jax.ShapeDtypeStruct((T, WA), F32), jax.ShapeDtypeStruct((T, WB), F32),
                   jax.ShapeDtypeStruct((T, WC), F32), jax.ShapeDtypeStruct((1, D), F32)],
        compiler_params=_params(("arbitrary",)),
    )(dmixed, oa, ob, oc, gain)


def _rope_tables():
    inv_freq = ROPE_THETA ** (-jnp.arange(0, HD, 2, dtype=F32) / HD)
    ang = jnp.arange(T, dtype=F32)[:, None] * inv_freq[None, :]
    cos, sin = jnp.cos(ang), jnp.sin(ang)
    cos2 = jnp.tile(jnp.concatenate([cos, cos], axis=1), (1, LANE // HD))
    sin2 = jnp.tile(jnp.concatenate([-sin, sin], axis=1), (1, LANE // HD))
    return cos2, sin2


def _rot_half(v):
    lane = lax.broadcasted_iota(jnp.int32, v.shape, 1)
    return jnp.where(lane % HD < HD // 2, pltpu.roll(v, LANE - HD // 2, 1), pltpu.roll(v, HD // 2, 1))


def _rope_fwd(name, proj, cos2, sin2):
    def body(p_ref, c_ref, s_ref, *outs):
        cv, sv = c_ref[...], s_ref[...]
        off = 0
        for o_ref, (_, w, rot, is_q) in zip(outs, GROUPS):
            for b in range(w // LANE):
                v = p_ref[:, off + b * LANE:off + (b + 1) * LANE]
                if rot:
                    v = v * cv + _rot_half(v) * sv
                if is_q:
                    v = v * (HD ** -0.5)
                o_ref[:, b * LANE:(b + 1) * LANE] = v.astype(BF16)
            off += w

    return pl.pallas_call(
        body, name=name, grid=(T // TR,), in_specs=[_rows(IN_COLS), _rows(LANE), _rows(LANE)],
        out_specs=[_rows(w) for _, w, _, _ in GROUPS],
        out_shape=[jax.ShapeDtypeStruct((T, w), BF16) for _, w, _, _ in GROUPS],
        compiler_params=_params(("parallel",)),
    )(proj, cos2, sin2)


def _rope_bwd(name, grads, cos2, sin2):
    def body(*refs):
        ins, (c_ref, s_ref, o_ref) = refs[:9], refs[9:]
        cv, sv = c_ref[...], s_ref[...]
        off = 0
        for d_ref, (_, w, rot, is_q) in zip(ins, GROUPS):
            for b in range(w // LANE):
                v = d_ref[:, b * LANE:(b + 1) * LANE]
                if is_q:
                    v = v * (HD ** -0.5)
                if rot:
                    v = v * cv + _rot_half(v * sv)
                o_ref[:, off + b * LANE:off + (b + 1) * LANE] = v.astype(BF16)
            off += w

    return pl.pallas_call(
        body, name=name, grid=(T // TR,), in_specs=[_rows(w) for _, w, _, _ in GROUPS] + [_rows(LANE), _rows(LANE)],
        out_specs=_rows(IN_COLS), out_shape=jax.ShapeDtypeStruct((T, IN_COLS), BF16),
        compiler_params=_params(("parallel",)),
    )(*grads, cos2, sin2)


NT_DIMS = (((1,), (1,)), ((), ()))
TN_DIMS = (((0,), (0,)), ((), ()))


def _scores(q, k, bias, valid):
    s = lax.dot_general(q, k, NT_DIMS, preferred_element_type=F32)
    if bias is not None:
        s = s + bias
    if valid is not None:
        s = jnp.where(valid, s, NEG)
    return s


def _heads_fwd(heads):
    scores = [_scores(h["q"], h["k"], h.get("bias"), h.get("valid")) for h in heads]
    soft = []
    for s, h in zip(scores, heads):
        m = jnp.max(s, axis=1, keepdims=True)
        e = jnp.exp(s - m)
        l = jnp.sum(e, axis=1, keepdims=True)
        if h.get("sink") is not None:
            l = l + jnp.exp(h["sink"] - m)
        soft.append((e.astype(BF16), l, m + jnp.log(l)))
    return [(jnp.dot(e, h["v"], preferred_element_type=F32) / l, lse) for (e, l, lse), h in zip(soft, heads)]


def _heads_bwd(heads):
    dobs = [h["do"].astype(BF16) for h in heads]
    scores = [_scores(h["q"], h["k"], h.get("bias"), h.get("valid")) for h in heads]
    dps = [lax.dot_general(dob, h["v"], NT_DIMS, preferred_element_type=F32) for dob, h in zip(dobs, heads)]
    mid = []
    for s, dp, h in zip(scores, dps, heads):
        p = jnp.exp(s - h["lse"])
        delta = jnp.sum(h["do"] * h["o"], axis=1, keepdims=True)
        ds = p * (dp - delta)
        dsink = None if h.get("sink") is None else -jnp.exp(h["sink"] - h["lse"]) * delta
        mid.append((p.astype(BF16), ds, dsink))
    out = []
    for (pb, ds, dsink), dob, h in zip(mid, dobs, heads):
        dsb = ds.astype(BF16)
        out.append((jnp.dot(dsb, h["k"], preferred_element_type=F32),
                    lax.dot_general(dsb, h["q"], TN_DIMS, preferred_element_type=F32),
                    lax.dot_general(pb, dob, TN_DIMS, preferred_element_type=F32), ds, dsink))
    return out


def _per_head(cols):
    return jnp.concatenate([jnp.broadcast_to(c, (c.shape[0], HD)) for c in cols], axis=1)


DILATIONS = ((128, 1), (512, 4), (2048, 16))


def _dilation_bias():
    def body(o_ref):
        t = pl.program_id(0) * TR + lax.broadcasted_iota(jnp.int32, (TR, T), 0)
        ad = jnp.abs(t - lax.broadcasted_iota(jnp.int32, (TR, T), 1))
        count = jnp.zeros((TR, T), jnp.int32)
        for window, r in DILATIONS:
            count += jnp.where(((ad & (r - 1)) == 0) & (ad <= window // 2), 1, 0)
        logs = jnp.where(count == 2, jnp.log(2.0), jnp.where(count == 3, jnp.log(3.0), 0.0)).astype(F32)
        o_ref[...] = jnp.where(count == 0, NEG, logs)

    return pl.pallas_call(
        body, name="dilation_bias", grid=(T // TR,), out_specs=_rows(T),
        out_shape=jax.ShapeDtypeStruct((T, T), F32), compiler_params=_params(("parallel",)),
    )()


BQ_A = 256


def _attn_a_fwd(name, qa, ka, va, bias):
    def body(q_ref, k_ref, v_ref, b_ref, o_ref, lse_ref):
        b = b_ref[...]
        outs = _heads_fwd([dict(q=q_ref[:, h * HD:(h + 1) * HD], k=k_ref[:, h * HD:(h + 1) * HD],
                                v=v_ref[:, h * HD:(h + 1) * HD], bias=b) for h in range(2)])
        o_ref[...] = jnp.concatenate([o for o, _ in outs], axis=1)
        lse_ref[...] = _per_head([lse for _, lse in outs])

    qs = pl.BlockSpec((BQ_A, LANE), lambda p, i: (i, p))
    ks = pl.BlockSpec((T, LANE), lambda p, i: (0, p))
    return pl.pallas_call(
        body, name=name, grid=(HA // 2, T // BQ_A),
        in_specs=[qs, ks, ks, pl.BlockSpec((BQ_A, T), lambda p, i: (i, 0))], out_specs=[qs, qs],
        out_shape=[jax.ShapeDtypeStruct((T, WA), F32)] * 2, compiler_params=_params(("parallel", "parallel")),
    )(qa, ka, va, bias)


def _attn_a_bwd(name, qa, ka, va, oa, lse, doa, bias):
    def body(q_ref, k_ref, v_ref, o_ref, lse_ref, do_ref, b_ref, dq_ref, dk_ref, dv_ref):
        b = b_ref[...]
        sls = [slice(h * HD, (h + 1) * HD) for h in range(2)]
        res = _heads_bwd([dict(q=q_ref[:, sl], k=k_ref[:, sl], v=v_ref[:, sl], o=o_ref[:, sl], do=do_ref[:, sl],
                               lse=lse_ref[:, sl.start:sl.start + 1], bias=b) for sl in sls])
        dq_ref[...] = jnp.concatenate([r[0] for r in res], axis=1)
        dk2, dv2 = jnp.concatenate([r[1] for r in res], axis=1), jnp.concatenate([r[2] for r in res], axis=1)

        @pl.when(pl.program_id(1) == 0)
        def _():
            dk_ref[...] = dk2
            dv_ref[...] = dv2

        @pl.when(pl.program_id(1) > 0)
        def _():
            dk_ref[...] += dk2
            dv_ref[...] += dv2

    qs = pl.BlockSpec((BQ_A, LANE), lambda p, i: (i, p))
    ks = pl.BlockSpec((T, LANE), lambda p, i: (0, p))
    return pl.pallas_call(
        body, name=name, grid=(HA // 2, T // BQ_A),
        in_specs=[qs, ks, ks, qs, qs, qs, pl.BlockSpec((BQ_A, T), lambda p, i: (i, 0))], out_specs=[qs, ks, ks],
        out_shape=[jax.ShapeDtypeStruct((T, WA), F32)] * 3, compiler_params=_params(("parallel", "arbitrary")),
    )(qa, ka, va, oa, lse, doa, bias)


BQ_B = 128
SPAN_B = BQ_B + 2 * WINDOW_B


def _window_b(i):
    start = pl.multiple_of(jnp.clip(i * BQ_B - WINDOW_B, 0, T - SPAN_B), BQ_B)
    qpos = i * BQ_B + lax.broadcasted_iota(jnp.int32, (BQ_B, SPAN_B), 0)
    kpos = start + lax.broadcasted_iota(jnp.int32, (BQ_B, SPAN_B), 1)
    return start, jnp.abs(qpos - kpos) <= WINDOW_B


GROUP_B = HB // HKV


def _stack_group(ref, g):
    return jnp.concatenate([ref[:, h * HD:(h + 1) * HD] for h in range(g * GROUP_B, (g + 1) * GROUP_B)], axis=0)


def _sink_column(sink_ref, g):
    return jnp.concatenate([jnp.full((BQ_B, 1), sink_ref[h], F32) for h in range(g * GROUP_B, (g + 1) * GROUP_B)],
                           axis=0)


def _unstack(stacked):
    return [s[j * BQ_B:(j + 1) * BQ_B] for s in stacked for j in range(GROUP_B)]


def _attn_b_fwd(name, qb, kb, vb, sink):
    def body(sink_ref, q_ref, k_ref, v_ref, o_ref, lse_ref):
        start, valid = _window_b(pl.program_id(0))
        valid = jnp.concatenate([valid] * GROUP_B, axis=0)
        kw, vw = k_ref[pl.ds(start, SPAN_B), :], v_ref[pl.ds(start, SPAN_B), :]
        outs = _heads_fwd([dict(q=_stack_group(q_ref, g), k=kw[:, g * HD:(g + 1) * HD], v=vw[:, g * HD:(g + 1) * HD],
                                valid=valid, sink=_sink_column(sink_ref, g)) for g in range(HKV)])
        o_ref[...] = jnp.concatenate(_unstack([o for o, _ in outs]), axis=1)
        lse_ref[...] = _per_head(_unstack([lse for _, lse in outs]))

    qs = pl.BlockSpec((BQ_B, WB), lambda i: (i, 0))
    return pl.pallas_call(
        body, name=name, grid=(T // BQ_B,),
        in_specs=[pl.BlockSpec(memory_space=pltpu.SMEM), qs, _whole((T, WKV)), _whole((T, WKV))],
        out_specs=[qs, qs],
        out_shape=[jax.ShapeDtypeStruct((T, WB), F32)] * 2, compiler_params=_params(("parallel",)),
    )(sink, qb, kb, vb)


def _attn_b_bwd(name, qb, kb, vb, ob, lse, dob, sink):
    def body(sink_ref, q_ref, k_ref, v_ref, o_ref, lse_ref, do_ref, dq_ref, dk_ref, dv_ref, dsink_ref):
        i = pl.program_id(0)
        start, valid = _window_b(i)
        valid = jnp.concatenate([valid] * GROUP_B, axis=0)
        kw, vw = k_ref[pl.ds(start, SPAN_B), :], v_ref[pl.ds(start, SPAN_B), :]
        res = _heads_bwd([dict(q=_stack_group(q_ref, g), k=kw[:, g * HD:(g + 1) * HD], v=vw[:, g * HD:(g + 1) * HD],
                               o=_stack_group(o_ref, g), do=_stack_group(do_ref, g),
                               lse=jnp.concatenate([lse_ref[:, h * HD:h * HD + 1]
                                                    for h in range(g * GROUP_B, (g + 1) * GROUP_B)], axis=0),
                               valid=valid, sink=_sink_column(sink_ref, g)) for g in range(HKV)])
        dks, dvs = [r[1] for r in res], [r[2] for r in res]
        lane = lax.broadcasted_iota(jnp.int32, (1, LANE), 1)
        dsink = jnp.zeros((1, LANE), F32)
        for h, rows in enumerate(_unstack([r[4] for r in res])):
            dsink += jnp.where(lane == h, jnp.sum(rows), 0.0)
        dq_ref[...] = jnp.concatenate(_unstack([r[0] for r in res]), axis=1)

        @pl.when(i == 0)
        def _():
            dk_ref[...] = jnp.zeros_like(dk_ref)
            dv_ref[...] = jnp.zeros_like(dv_ref)
            dsink_ref[...] = jnp.zeros_like(dsink_ref)

        dk_ref[pl.ds(start, SPAN_B), :] += jnp.concatenate(dks, axis=1)
        dv_ref[pl.ds(start, SPAN_B), :] += jnp.concatenate(dvs, axis=1)
        dsink_ref[...] += dsink

    qs = pl.BlockSpec((BQ_B, WB), lambda i: (i, 0))
    return pl.pallas_call(
        body, name=name, grid=(T // BQ_B,),
        in_specs=[pl.BlockSpec(memory_space=pltpu.SMEM), qs, _whole((T, WKV)), _whole((T, WKV)), qs, qs, qs],
        out_specs=[qs, _whole((T, WKV)), _whole((T, WKV)), _whole((1, LANE))],
        out_shape=[jax.ShapeDtypeStruct((T, WB), F32), jax.ShapeDtypeStruct((T, WKV), F32),
                   jax.ShapeDtypeStruct((T, WKV), F32), jax.ShapeDtypeStruct((1, LANE), F32)],
        compiler_params=_params(("arbitrary",)),
    )(sink, qb, kb, vb, ob, lse, dob)


SPAN_C = NA_ROWS * GRID_W


def _row_start(r):
    return jnp.clip(r - NA_ROWS // 2, 0, ROWS - NA_ROWS)


def _off_index(r):
    return _row_start(r) - r + (NA_ROWS - 1)


N_TAB = 16
RPS = 4


def _rpb_tables(name, rpb):
    circ = jnp.concatenate([rpb[..., NA_COLS - 1:], jnp.zeros(rpb.shape[:2] + (LANE - (2 * NA_COLS - 1),), F32),
                            rpb[..., :NA_COLS - 1]], axis=-1)
    circ = jnp.pad(circ, ((0, 0), (0, N_TAB + 1 - circ.shape[1]), (0, 0)))

    def body(w_ref, o_ref):
        c = lax.broadcasted_iota(jnp.int32, (GRID_W, LANE), 0)
        lane = lax.broadcasted_iota(jnp.int32, (GRID_W, LANE), 1)
        cs = jnp.clip(c - NA_COLS // 2, 0, GRID_W - NA_COLS)
        valid = (lane % GRID_W >= cs) & (lane % GRID_W < cs + NA_COLS)
        toep = [pltpu.roll(jnp.broadcast_to(w_ref[a:a + 1, :], (GRID_W, LANE)), 0, 1, stride=1, stride_axis=0)
                for a in range(N_TAB + 1)]
        for a in range(N_TAB):
            pair = jnp.where(lane < GRID_W, toep[a], pltpu.roll(toep[a + 1], GRID_W, 1))
            o_ref[a] = jnp.where(valid, pair, NEG)

    return pl.pallas_call(
        body, name=name, grid=(HC,),
        in_specs=[pl.BlockSpec((None, N_TAB + 1, LANE), lambda h: (h, 0, 0))],
        out_specs=pl.BlockSpec((None, N_TAB, GRID_W, LANE), lambda h: (h, 0, 0, 0)),
        out_shape=jax.ShapeDtypeStruct((HC, N_TAB, GRID_W, LANE), F32), compiler_params=_params(("parallel",)),
    )(circ)


def _bias_c(t_ref, h, d):
    return jnp.concatenate([t_ref[h, d + k] for k in range(0, NA_ROWS, 2)], axis=1)


def _attn_c_fwd(name, qc, kc, vc, tables):
    def body(q_ref, k_ref, v_ref, t_ref, o_ref, lse_ref):
        heads = []
        for rr in range(RPS):
            r = pl.program_id(1) * RPS + rr
            rows = slice(rr * GRID_W, (rr + 1) * GRID_W)
            start = pl.multiple_of(_row_start(r) * GRID_W, GRID_W)
            kw, vw = k_ref[pl.ds(start, SPAN_C), :], v_ref[pl.ds(start, SPAN_C), :]
            heads += [dict(q=q_ref[rows, h * HD:(h + 1) * HD], k=kw[:, h * HD:(h + 1) * HD], v=vw[:, h * HD:(h + 1) * HD],
                           bias=_bias_c(t_ref, h, _off_index(r))) for h in range(2)]
        outs = _heads_fwd(heads)
        for rr in range(RPS):
            rows = slice(rr * GRID_W, (rr + 1) * GRID_W)
            o_ref[rows, :] = jnp.concatenate([o for o, _ in outs[2 * rr:2 * rr + 2]], axis=1)
            lse_ref[rows, :] = _per_head([lse for _, lse in outs[2 * rr:2 * rr + 2]])

    qs = pl.BlockSpec((RPS * GRID_W, LANE), lambda p, r: (r, p))
    ks = pl.BlockSpec((T, LANE), lambda p, r: (0, p))
    ts = pl.BlockSpec((2, N_TAB, GRID_W, LANE), lambda p, r: (p, 0, 0, 0))
    return pl.pallas_call(
        body, name=name, grid=(HC // 2, ROWS // RPS), in_specs=[qs, ks, ks, ts], out_specs=[qs, qs],
        out_shape=[jax.ShapeDtypeStruct((T, WC), F32)] * 2, compiler_params=_params(("parallel", "parallel")),
    )(qc, kc, vc, tables)


def _attn_c_bwd(name, qc, kc, vc, oc, lse, doc, tables):
    def body(q_ref, k_ref, v_ref, o_ref, lse_ref, do_ref, t_ref, dq_ref, dk_ref, dv_ref, dt_ref):
        @pl.when(pl.program_id(1) == 0)
        def _():
            dk_ref[...] = jnp.zeros_like(dk_ref)
            dv_ref[...] = jnp.zeros_like(dv_ref)
            dt_ref[...] = jnp.zeros_like(dt_ref)

        heads, where = [], []
        for rr in range(RPS):
            r = pl.program_id(1) * RPS + rr
            rows = slice(rr * GRID_W, (rr + 1) * GRID_W)
            d = _off_index(r)
            start = pl.multiple_of(_row_start(r) * GRID_W, GRID_W)
            kw, vw = k_ref[pl.ds(start, SPAN_C), :], v_ref[pl.ds(start, SPAN_C), :]
            where.append((rows, d, start))
            for h in range(2):
                sl = slice(h * HD, (h + 1) * HD)
                heads.append(dict(q=q_ref[rows, sl], k=kw[:, sl], v=vw[:, sl], o=o_ref[rows, sl], do=do_ref[rows, sl],
                                  lse=lse_ref[rows, h * HD:h * HD + 1], bias=_bias_c(t_ref, h, d)))
        res = _heads_bwd(heads)
        for rr, (rows, d, start) in enumerate(where):
            pair = res[2 * rr:2 * rr + 2]
            for h in range(2):
                for k in range(0, NA_ROWS, 2):
                    dt_ref[h, d + k] += pair[h][3][:, k * GRID_W:(k + 2) * GRID_W]
            dq_ref[rows, :] = jnp.concatenate([p[0] for p in pair], axis=1)
            dk_ref[pl.ds(start, SPAN_C), :] += jnp.concatenate([p[1] for p in pair], axis=1)
            dv_ref[pl.ds(start, SPAN_C), :] += jnp.concatenate([p[2] for p in pair], axis=1)

    qs = pl.BlockSpec((RPS * GRID_W, LANE), lambda p, r: (r, p))
    ks = pl.BlockSpec((T, LANE), lambda p, r: (0, p))
    ts = pl.BlockSpec((2, N_TAB, GRID_W, LANE), lambda p, r: (p, 0, 0, 0))
    return pl.pallas_call(
        body, name=name, grid=(HC // 2, ROWS // RPS), in_specs=[qs, ks, ks, qs, qs, qs, ts],
        out_specs=[qs, ks, ks, ts],
        out_shape=[jax.ShapeDtypeStruct((T, WC), F32)] * 3 + [jax.ShapeDtypeStruct((HC, N_TAB, GRID_W, LANE), F32)],
        compiler_params=_params(("parallel", "arbitrary")),
    )(qc, kc, vc, oc, lse, doc, tables)


def _split3(v):
    hi = v.astype(BF16)
    r1 = v - hi.astype(F32)
    mid = r1.astype(BF16)
    lo = (r1 - mid.astype(F32)).astype(BF16)
    return hi, mid, lo


def _rpb_reduce(name, dtables):
    x = dtables.reshape(HC, N_TAB, GRID_W * LANE)
    c = jnp.arange(GRID_W)[:, None]
    lane = jnp.arange(LANE)[None, :]
    col = (lane // GRID_W) * LANE + jnp.clip(lane % GRID_W - c + (NA_COLS - 1), 0, 2 * NA_COLS - 2)
    col_onehot = (col.reshape(-1)[:, None] == jnp.arange(2 * LANE)[None, :]).astype(BF16)
    a2 = jnp.arange(N_TAB)[None, :]
    row_onehot = jnp.concatenate([(jnp.arange(16)[:, None] == a2 + u) & (a2 < 2 * NA_ROWS - 2) for u in range(2)],
                                 axis=1).astype(BF16)

    def body(x_ref, e_ref, f_ref, o_ref):
        y = sum(jnp.dot(part, e_ref[...], preferred_element_type=F32) for part in _split3(x_ref[...]))
        z = jnp.concatenate([y[:, :LANE], y[:, LANE:]], axis=0)
        o_ref[...] = sum(jnp.dot(f_ref[...], part, preferred_element_type=F32) for part in _split3(z))

    out = pl.pallas_call(
        body, name=name, grid=(HC,),
        in_specs=[pl.BlockSpec((None, N_TAB, GRID_W * LANE), lambda h: (h, 0, 0)),
                  _whole((GRID_W * LANE, 2 * LANE)), _whole((16, 2 * N_TAB))],
        out_specs=pl.BlockSpec((None, 16, LANE), lambda h: (h, 0, 0)),
        out_shape=jax.ShapeDtypeStruct((HC, 16, LANE), F32), compiler_params=_params(("parallel",)),
    )(x, col_onehot, row_onehot)
    return out[:, :2 * NA_ROWS - 1, :2 * NA_COLS - 1]


TC = 128
NCB = DFF // TC
CHUNK = 128
MARGIN = 8


def _shift_down(v, rows):
    return jnp.where(rows == 0, 0.0, pltpu.roll(v, 1, 0))


def _shift_up(v, rows):
    return jnp.where(rows == T - 1, 0.0, pltpu.roll(v, T - 1, 0))


def _conv(v, w, b, rows):
    return _shift_down(v, rows) * w[0:1] + v * w[1:2] + _shift_up(v, rows) * w[2:3] + b


def _ffn_specs():
    gate = lambda shape: pl.BlockSpec(shape, lambda j: (0, j))
    val = lambda shape: pl.BlockSpec(shape, lambda j: (0, j + NCB))
    return [gate((T, TC)), val((T, TC)), gate((3, TC)), val((3, TC)), gate((1, TC)), val((1, TC))]


def _ffn_mid_fwd(name, up, conv_w, conv_b):
    def body(xg_ref, xv_ref, wg_ref, wv_ref, bg_ref, bv_ref, o_ref):
        rows = lax.broadcasted_iota(jnp.int32, (T, TC), 0)
        ug = _conv(xg_ref[...], wg_ref[...], bg_ref[...], rows)
        uv = _conv(xv_ref[...], wv_ref[...], bv_ref[...], rows)
        o_ref[...] = (ug * jax.nn.sigmoid(ug) * uv).astype(BF16)

    return pl.pallas_call(
        body, name=name, grid=(NCB,), in_specs=_ffn_specs(), out_specs=pl.BlockSpec((T, TC), lambda j: (0, j)),
        out_shape=jax.ShapeDtypeStruct((T, DFF), BF16), compiler_params=_params(("parallel",)),
    )(up, up, conv_w, conv_w, conv_b, conv_b)


def _ffn_mid_bwd(name, dact, up, conv_w, conv_b):
    window = CHUNK + 2 * MARGIN
    centre = slice(MARGIN, MARGIN + CHUNK)

    def shifted(v):
        return pltpu.roll(v, 1, 0), pltpu.roll(v, window - 1, 0)

    def fold(v):
        return jnp.sum(v[centre].reshape(CHUNK // 8, 8, TC), axis=0)

    def body(da_ref, xg_ref, xv_ref, wg_ref, wv_ref, bg_ref, bv_ref, dx_ref, dw_ref, db_ref, dap, xgp, xvp):
        for src, pad in ((da_ref, dap), (xg_ref, xgp), (xv_ref, xvp)):
            pad[0:MARGIN, :] = jnp.zeros((MARGIN, TC), F32)
            pad[MARGIN:MARGIN + T, :] = src[...]
            pad[MARGIN + T:, :] = jnp.zeros((MARGIN, TC), F32)
        wg, wv, bg, bv = wg_ref[...], wv_ref[...], bg_ref[...], bv_ref[...]

        def chunk(c, sums):
            r0 = pl.multiple_of(c * CHUNK, CHUNK)
            da, xg, xv = dap[pl.ds(r0, window), :], xgp[pl.ds(r0, window), :], xvp[pl.ds(r0, window), :]
            xg_prev, xg_next = shifted(xg)
            xv_prev, xv_next = shifted(xv)
            ug = xg_prev * wg[0:1] + xg * wg[1:2] + xg_next * wg[2:3] + bg
            uv = xv_prev * wv[0:1] + xv * wv[1:2] + xv_next * wv[2:3] + bv
            sg = jax.nn.sigmoid(ug)
            dug = da * uv * (sg * (1.0 + ug * (1.0 - sg)))
            duv = da * (ug * sg)
            out = []
            for half, (x_prev, x, x_next, w, du) in enumerate(((xg_prev, xg, xg_next, wg, dug),
                                                               (xv_prev, xv, xv_next, wv, duv))):
                du_prev, du_next = shifted(du)
                dx = du_next * w[0:1] + du * w[1:2] + du_prev * w[2:3]
                dx_ref[half, pl.ds(r0, CHUNK), :] = dx[centre].astype(BF16)
                out += [fold(x_prev * du), fold(x * du), fold(x_next * du), fold(du)]
            return tuple(s + o for s, o in zip(sums, out))

        sums = lax.fori_loop(0, T // CHUNK, chunk, tuple(jnp.zeros((8, TC), F32) for _ in range(8)))
        rows = [jnp.sum(s, axis=0, keepdims=True) for s in sums]
        for half in range(2):
            dw_ref[half] = jnp.concatenate(rows[4 * half:4 * half + 3], axis=0)
            db_ref[half] = rows[4 * half + 3]

    return pl.pallas_call(
        body, name=name, grid=(NCB,), in_specs=[pl.BlockSpec((T, TC), lambda j: (0, j))] + _ffn_specs(),
        out_specs=[pl.BlockSpec((2, T, TC), lambda j: (0, 0, j)), pl.BlockSpec((2, 3, TC), lambda j: (0, 0, j)),
                   pl.BlockSpec((2, 1, TC), lambda j: (0, 0, j))],
        out_shape=[jax.ShapeDtypeStruct((2, T, DFF), BF16), jax.ShapeDtypeStruct((2, 3, DFF), F32),
                   jax.ShapeDtypeStruct((2, 1, DFF), F32)],
        scratch_shapes=[pltpu.VMEM((T + 2 * MARGIN, TC), F32)] * 3,
        compiler_params=_params(("parallel",)),
    )(dact, up, up, conv_w, conv_w, conv_b, conv_b)


def _dup_spec(tm, nj):
    per = DFF // nj
    return pl.BlockSpec((None, tm, nj), lambda a, b, j: (j // per, 0 if tm == T else b, j % per))


def _dup_spec_tn(tm, nj):
    per = DFF // nj
    return pl.BlockSpec((None, tm, nj), lambda j, kt, r: (j // per, 0, j % per))


def _adamw_math(w, g, m, v):
    m = ADAM_B1 * m + (1.0 - ADAM_B1) * g
    v = ADAM_B2 * v + (1.0 - ADAM_B2) * (g * g)
    m_hat = m / (1.0 - ADAM_B1 ** ADAM_STEP)
    v_hat = v / (1.0 - ADAM_B2 ** ADAM_STEP)
    delta = -ADAM_LR * (m_hat / (jnp.sqrt(v_hat) + ADAM_EPS) + ADAM_WD * w)
    return delta, m, v


ADAM_BLOCK = 256 * 1408


def _adamw_sharded(name, w, m, v, parts):
    _, r, c = w.shape
    tr = max(t for t in range(16, r + 1, 16) if r % t == 0 and t * c <= ADAM_BLOCK)

    def body(w_ref, m_ref, v_ref, p0_ref, p1_ref, g_ref, d_ref, nm_ref, nv_ref):
        def run(p_ref):
            g = p_ref[0].astype(F32)
            for k in range(1, N_DEV):
                g = g + p_ref[k].astype(F32)
            d, nm, nv = _adamw_math(w_ref[...], g, m_ref[...], v_ref[...])
            g_ref[...] = g
            d_ref[...] = d
            nm_ref[...] = nm
            nv_ref[...] = nv

        @pl.when(pl.program_id(0) == 0)
        def _():
            run(p0_ref)

        @pl.when(pl.program_id(0) == 1)
        def _():
            run(p1_ref)

    ws = pl.BlockSpec((None, tr, c), lambda l, i: (l, i, 0))
    p0 = pl.BlockSpec((N_DEV, tr, c), lambda l, i: (0, jnp.where(l == 0, i, r // tr - 1), 0))
    p1 = pl.BlockSpec((N_DEV, tr, c), lambda l, i: (0, jnp.where(l == 1, i, 0), 0))
    return pl.pallas_call(
        body, name=name, grid=(DEPTH, r // tr), in_specs=[ws, ws, ws, p0, p1], out_specs=[ws] * 4,
        out_shape=[jax.ShapeDtypeStruct(w.shape, F32)] * 4, compiler_params=_params(("arbitrary", "arbitrary")),
    )(w, m, v, *parts)


def _sum_devices(name, parts):
    r = parts.shape[1]

    def body(p_ref, o_ref):
        g = p_ref[0]
        for k in range(1, N_DEV):
            g = g + p_ref[k]
        o_ref[...] = g

    return pl.pallas_call(
        body, name=name, in_specs=[pl.BlockSpec((N_DEV, r, LANE), lambda: (0, 0, 0))],
        out_specs=pl.BlockSpec((r, LANE), lambda: (0, 0)), out_shape=jax.ShapeDtypeStruct((r, LANE), F32),
        compiler_params=_params(),
    )(parts)


def _adamw_small(name, ws, gs, ms, vs):
    n = len(ws)
    shapes = [w.shape for w in ws]
    ws, gs, ms, vs = ([a.reshape(1, -1) if a.ndim == 1 else a for a in arrs] for arrs in (ws, gs, ms, vs))
    specs = [pl.BlockSpec(memory_space=pltpu.VMEM)] * n

    def body(*refs):
        for i in range(n):
            w_ref, g_ref, m_ref, v_ref = (refs[k * n + i] for k in range(4))
            d, nm, nv = _adamw_math(w_ref[...], g_ref[...], m_ref[...], v_ref[...])
            refs[4 * n + i][...] = d
            refs[5 * n + i][...] = nm
            refs[6 * n + i][...] = nv

    outs = pl.pallas_call(
        body, name=name, in_specs=specs * 4, out_specs=specs * 3,
        out_shape=[jax.ShapeDtypeStruct(w.shape, F32) for w in ws] * 3, compiler_params=_params(),
    )(*ws, *gs, *ms, *vs)
    outs = [o.reshape(shapes[i % n]) for i, o in enumerate(outs)]
    return outs[:n], outs[n:2 * n], outs[2 * n:]


def _pack(arrays):
    flat = jnp.concatenate([a.reshape(-1) for a in arrays])
    pad = (-flat.shape[0]) % (8 * LANE)
    return jnp.pad(flat, (0, pad)).reshape(-1, LANE)


def _unpack(buf, shapes):
    flat, out, off = buf.reshape(-1), [], 0
    for s in shapes:
        n = 1
        for d in s:
            n *= d
        out.append(flat[off:off + n].reshape(s))
        off += n
    return out


def _local_step(x, target, small, weights, conv_w_full, hand_over, used):
    cos2, sin2 = _rope_tables()
    bias_a = _dilation_bias()
    tables = [_rpb_tables(f"rpb_tables_{l}", small["rpb_c"][l]) for l in range(DEPTH)]
    saved, carry = [], 0.0
    for l in range(DEPTH):
        g1, g2 = small["ln_attn"][l][None] + carry, small["ln_ffn"][l][None]
        gain, sink, cb = small["mix_gain"][l][None], small["sink_b"][l], small["conv_b"][l][None]
        cw = conv_w_full[l]
        bias = tables[l]
        h1, proj = _prologue_matmul(f"proj_in_{l}", _rmsnorm_rows, [x, g1], [D, None],
                                    weights("w_in", l, [cos2, sin2, bias_a] + tables if l == 0 else x),
                                    (D, 1024), lambda j: (0, j), 1024)
        zero = used(f"proj_in_{l}", proj)
        qa, ka, va, qb, kb, vb, qc, kc, vc = _rope_fwd(f"rope_{l}", proj, cos2, sin2)
        oa, lse_a = _attn_a_fwd(f"attn_a_{l}", qa, ka, va, bias_a)
        ob, lse_b = _attn_b_fwd(f"attn_b_{l}", qb, kb, vb, sink + zero)
        oc, lse_c = _attn_c_fwd(f"attn_c_{l}", qc, kc, vc, bias)
        mixed, x_mid = _prologue_matmul(f"proj_out_{l}", _mix_rows, [oa, ob, oc, gain + used(f"attn_{l}", oc)],
                                        [WA, WB, WC, None],
                                        weights("w_out", l, oc), (N_DEV, D // N_DEV, 512), lambda j: (0, 0, j), 512,
                                        res=x)
        h2, up = _prologue_matmul(f"ffn_up_{l}", _rmsnorm_rows, [x_mid, g2 + used(f"proj_out_{l}", x_mid)], [D, None],
                                  weights("w_up", l, x_mid), (D, 1024), lambda j: (0, j), 1024)
        act = _ffn_mid_fwd(f"ffn_mid_{l}", up, cw, cb + used(f"ffn_up_{l}", up))
        x_out = _nn_rows(f"ffn_down_{l}", act, weights("w_down", l, act), x_mid, 4, 1024, 1024)
        carry = used(f"ffn_down_{l}", x_out)
        saved.append(dict(x=x, h1=h1, qkv=(qa, ka, va, qb, kb, vb, qc, kc, vc), o=(oa, ob, oc), lse=(lse_a, lse_b, lse_c), mixed=mixed,
                          x_mid=x_mid, h2=h2, up=up, act=act, g1=g1, g2=g2, gain=gain, sink=sink, cb=cb, cw=cw, bias=bias))
        x = x_out

    loss8, dx, dxb, d_ln_final = _loss_head(x, small["ln_final"][None], target)
    sgrads = [None] * DEPTH
    for l in reversed(range(DEPTH)):
        s = saved[l]
        qa, ka, va, qb, kb, vb, qc, kc, vc = s["qkv"]
        oa, ob, oc = s["o"]
        wg_in, wg_out = weights("w_in", l, None), weights("w_out", l, None)
        wg_up, wg_down = weights("w_up", l, None), weights("w_down", l, None)
        g_down = _tn_rows(f"wgrad_down_{l}", s["act"], dxb, wg_down.shape[1], 2, 512)
        zero = hand_over("w_down", l, g_down)
        dact = _nt_rows(f"dgrad_down_{l}", dxb, wg_down, 4, 512)
        dup, d_cw, d_cb = _ffn_mid_bwd(f"ffn_mid_bwd_{l}", dact, s["up"], s["cw"], s["cb"] + zero)
        g_up = _tn_cols(f"wgrad_up_{l}", s["h2"], dup, _dup_spec_tn, 2 * DFF, DFF // 2)
        zero = hand_over("w_up", l, g_up)
        dh2 = _nt_cols(f"dgrad_up_{l}", dup, _dup_spec, wg_up, DFF // 2)
        dx, dxb, d_g2 = _rmsnorm_bwd(f"norm_ffn_bwd_{l}", dh2, s["x_mid"], s["g2"] + zero, dx)
        g_out = _tn_rows(f"wgrad_out_{l}", s["mixed"], dxb, wg_out.shape[1], 2, D)
        zero = hand_over("w_out", l, g_out)
        dmixed = _nt_rows(f"dgrad_out_{l}", dxb, wg_out, 2, T)
        doa, dob, doc, d_gain = _mix_bwd(f"mix_bwd_{l}", dmixed, oa, ob, oc, s["gain"] + zero)
        lse_a, lse_b, lse_c = s["lse"]
        dqa, dka, dva = _attn_a_bwd(f"attn_a_bwd_{l}", qa, ka, va, oa, lse_a, doa, bias_a)
        dqb, dkb, dvb, d_sink = _attn_b_bwd(f"attn_b_bwd_{l}", qb, kb, vb, ob, lse_b, dob, s["sink"])
        dqc, dkc, dvc, d_bias = _attn_c_bwd(f"attn_c_bwd_{l}", qc, kc, vc, oc, lse_c, doc, s["bias"])
        d_rpb = _rpb_reduce(f"rpb_reduce_{l}", d_bias)
        dproj = _rope_bwd(f"rope_bwd_{l}", (dqa, dka, dva, dqb, dkb, dvb, dqc, dkc, dvc), cos2, sin2)
        g_in = _tn_cols(f"wgrad_in_{l}", s["h1"], dproj,
                        lambda tm, tn: pl.BlockSpec((tm, tn), lambda j, kt, r: (0, j)), IN_COLS, 1024)
        zero = hand_over("w_in", l, g_in)
        dh1 = _nt_cols(f"dgrad_in_{l}", dproj, lambda tm, nc: pl.BlockSpec((tm, nc), lambda kt, i, j: (i, j)), wg_in,
                       IN_COLS // 2)
        dx, dxb, d_g1 = _rmsnorm_bwd(f"norm_attn_bwd_{l}", dh1, s["x"], s["g1"] + zero, dx)
        sgrads[l] = dict(ln_attn=d_g1[0], sink_b=d_sink[0, :HB], rpb_c=d_rpb, mix_gain=d_gain[0], ln_ffn=d_g2[0],
                         conv_w=d_cw.transpose(1, 0, 2).reshape(3, 2 * DFF), conv_b=d_cb.reshape(2 * DFF))
    return loss8[0, 0], dx, d_ln_final[0], sgrads


SMALL_NAMES = ("ln_attn", "sink_b", "rpb_c", "mix_gain", "ln_ffn", "conv_b")


def kernel(x, ln_attn, w_in, sink_b, rpb_c, mix_gain, w_out, ln_ffn, w_up, conv_w, conv_b, w_down, ln_final, loss_target, m_ln_attn, m_w_in, m_sink_b, m_rpb_c, m_mix_gain, m_w_out, m_ln_ffn, m_w_up, m_conv_w, m_conv_b, m_w_down, m_ln_final, v_ln_attn, v_w_in, v_sink_b, v_rpb_c, v_mix_gain, v_w_out, v_ln_ffn, v_w_up, v_conv_w, v_conv_b, v_w_down, v_ln_final):
    me = 4 * lax.axis_index("x") + 2 * lax.axis_index("y") + lax.axis_index("c")
    small = dict(ln_attn=ln_attn, sink_b=sink_b, rpb_c=rpb_c, mix_gain=mix_gain, ln_ffn=ln_ffn, conv_b=conv_b,
                 ln_final=ln_final)

    names = ("w_in", "w_out", "w_up", "w_down")
    shards = dict(w_in=w_in, w_out=w_out, w_up=w_up, w_down=w_down)
    order = [(n, l) for l in range(DEPTH) for n in names]
    conv_key = ("conv_w", 0)
    started, arrived, forwarded, gathered = {}, {}, {}, {}

    def side_by_side(k):
        return k[0] in ("w_in", "w_up")

    def slot_of(k):
        return _col_slot(shards[k[0]].shape[2]) if side_by_side(k) else _lead_slot

    def begin(name, ks, zero):
        srcs = [_pack([conv_w]) + zero if k == conv_key else (shards[k[0]][k[1]] + zero).astype(BF16) for k in ks]
        lands = [lax.empty((s.shape[0], N_DEV * s.shape[1]) if side_by_side(k) else (N_DEV,) + s.shape, s.dtype)
                 for k, s in zip(ks, srcs)]
        peers = [ALL_PEERS if k == conv_key else NEAR_PEERS for k in ks]
        send, recv, bufs, tok = _copy_start(name, srcs + lands, _gather_plan(peers, [slot_of(k) for k in ks]),
                                            [len(p) + 1 for p in peers])
        for i, k in enumerate(ks):
            started[k] = (send[i], recv[i], bufs[i], bufs[len(ks) + i], peers[i])
        return tok

    token = begin("gather_start_first", order[:1], 0.0)
    token = begin("gather_start_rest", [conv_key] + order[1:], token[0, 0])

    def arrive(k, after):
        send, recv, src, land, peers = started[k]
        arrived[k] = _copy_wait(f"gather_{k[0]}_{k[1]}_arrived", [src, land], [send], [recv],
                                _gather_plan([peers], [slot_of(k)]), after)

    queue = list(order)

    def advance(after):
        if not queue:
            return 0.0
        k = queue.pop(0)
        arrive(k, after)
        forwarded[k] = _copy_start(f"gather_{k[0]}_{k[1]}_forward", [arrived[k][1]], _forward_plan(slot_of(k)),
                                   [len(OTHER_CHIPS)])
        return forwarded[k][3][0, 0]

    pass_on_behind = ("proj_in_0", "attn_0", "ffn_up_0", "ffn_down_0", "proj_in_1", "attn_1", "ffn_up_1")

    def used(point, result):
        return advance(result) if point in pass_on_behind else 0.0

    def weights(n, l, after):
        k = (n, l)
        if k not in gathered:
            if k not in forwarded:
                advance(after)
            send_b, recv_b, (land,), _ = forwarded[k]
            (gathered[k],) = _copy_wait(f"gather_{n}_{l}_done", [land], send_b, recv_b, _forward_plan(slot_of(k)),
                                        after)
        return gathered[k]

    pending = {}

    def hand_over(n, l, g):
        shard = shards[n].shape[1:]
        send, recv, bufs, tok = _copy_start(f"send_grad_{n}_{l}", [g, lax.empty((N_DEV,) + shard, g.dtype)],
                                            _scatter_plan(slot_of((n, l))), [len(ALL_PEERS) + 1])
        pending[(n, l)] = (send, recv, bufs)
        return tok[0, 0]

    def received(k, after):
        send, recv, bufs = pending[k]
        return _copy_wait(f"recv_grad_{k[0]}_{k[1]}", bufs, send, recv, _scatter_plan(slot_of(k)), after)[1]

    arrive(conv_key, token)
    cw_all = arrived[conv_key][1]
    nup = w_up.shape[2]
    cw_shards = cw_all.reshape(N_DEV, -1)[:, :DEPTH * 3 * nup].reshape(N_DEV, DEPTH, 3, nup)
    conv_w_full = cw_shards.transpose(1, 2, 0, 3).reshape(DEPTH, 3, N_DEV * nup)

    loss_local, dx, d_ln_final, sgrads = _local_step(
        x[0], loss_target[0], dict(small, ln_attn=ln_attn + token[0, 0]), weights, conv_w_full, hand_over, used)

    stacked = [jnp.stack([sgrads[l][n] for l in range(DEPTH)]) for n in SMALL_NAMES + ("conv_w",)] + [d_ln_final]
    shapes = [a.shape for a in stacked]
    mine = _pack(stacked)
    send_s, recv_s, bufs_s, _ = _copy_start("gather_small_grads_start", [mine, lax.empty((N_DEV,) + mine.shape, F32)],
                                            _gather_plan([ALL_PEERS], [_lead_slot]), [len(ALL_PEERS) + 1])

    big, after = {}, dx
    moments = dict(w_in=(m_w_in, v_w_in), w_out=(m_w_out, v_w_out), w_up=(m_w_up, v_w_up), w_down=(m_w_down, v_w_down))
    for n in reversed(names):
        parts = (received((n, 0), after), received((n, 1), after))
        big[n] = _adamw_sharded(f"adamw_{n}", shards[n], *moments[n], parts)
        after = big[n][1]

    _, everyone = _copy_wait("gather_small_grads_done", bufs_s, send_s, recv_s,
                             _gather_plan([ALL_PEERS], [_lead_slot]), after)
    g_small = _unpack(_sum_devices("sum_small_grads", everyone), shapes)
    g = dict(zip(SMALL_NAMES + ("conv_w", "ln_final"), g_small))
    g["conv_w"] = lax.dynamic_slice_in_dim(g["conv_w"], me * nup, nup, axis=2)

    snames = SMALL_NAMES + ("conv_w", "ln_final")
    sw = dict(small, conv_w=conv_w)
    sm = dict(ln_attn=m_ln_attn, sink_b=m_sink_b, rpb_c=m_rpb_c, mix_gain=m_mix_gain, ln_ffn=m_ln_ffn,
              conv_b=m_conv_b, conv_w=m_conv_w, ln_final=m_ln_final)
    sv = dict(ln_attn=v_ln_attn, sink_b=v_sink_b, rpb_c=v_rpb_c, mix_gain=v_mix_gain, ln_ffn=v_ln_ffn,
              conv_b=v_conv_b, conv_w=v_conv_w, ln_final=v_ln_final)
    s_delta, s_m, s_v = (dict(zip(snames, out)) for out in _adamw_small(
        "adamw_small", [sw[n] for n in snames], [g[n] for n in snames], [sm[n] for n in snames],
        [sv[n] for n in snames]))

    loss = lax.psum(loss_local, ("x", "y", "c"))
    outputs = ("ln_attn", "w_in", "sink_b", "rpb_c", "mix_gain", "w_out", "ln_ffn", "w_up", "conv_w", "conv_b",
               "w_down", "ln_final")
    grads = [big[n][0] if n in big else g[n] for n in outputs]
    deltas = [big[n][1] if n in big else s_delta[n] for n in outputs]
    new_m = [big[n][2] if n in big else s_m[n] for n in outputs]
    new_v = [big[n][3] if n in big else s_v[n] for n in outputs]
    return (loss, dx[None], *grads, *deltas, *new_m, *new_v)
```

```python
import functools

import jax
import jax.numpy as jnp
from jax import lax
from jax.experimental import pallas as pl
from jax.experimental.pallas import tpu as pltpu

F32 = jnp.float32
BF16 = jnp.bfloat16

N_DEV = 8
T = 2048
D = 2048
DEPTH = 2
HD = 64
HA, HB, HKV, HC = 12, 10, 2, 10
WA, WB, WKV, WC = HA * HD, HB * HD, HKV * HD, HC * HD
IN_COLS = 3 * WA + WB + 2 * WKV + 3 * WC
DFF = 5632
GRID_W = 64
ROWS = T // GRID_W
NA_ROWS, NA_COLS = 8, 16
WINDOW_B = 128
EPS = 1e-6
NEG = -1e30
ROPE_THETA = 10000.0
LANE = 128
VMEM_LIMIT = 56 * 1024 * 1024

ADAM_LR, ADAM_B1, ADAM_B2, ADAM_EPS, ADAM_WD, ADAM_STEP = 0.001, 0.9, 0.999, 1e-08, 0.01, 10

GROUPS = (("qa", WA, True, True), ("ka", WA, True, False), ("va", WA, False, False),
          ("qb", WB, True, True), ("kb", WKV, True, False), ("vb", WKV, False, False),
          ("qc", WC, False, True), ("kc", WC, False, False), ("vc", WC, False, False))


def _params(sem=None):
    return pltpu.CompilerParams(dimension_semantics=sem, vmem_limit_bytes=VMEM_LIMIT)


HBM_SPEC = pl.BlockSpec(memory_space=pltpu.HBM)
SEM_SPEC = pl.BlockSpec(memory_space=pltpu.SEMAPHORE)
DATAFLOW = pltpu.SideEffectType.DATAFLOW_SIDE_EFFECTING


ALL_PEERS = tuple((p >> 2 & 1, p >> 1 & 1, p & 1) for p in range(1, N_DEV))
OTHER_CHIPS = ((1, 0, 0), (0, 1, 0), (1, 1, 0))
NEAR_PEERS = ((0, 0, 1),) + OTHER_CHIPS


def _flip(x, y, c, f):
    return (1 - x if f[0] else x, 1 - y if f[1] else y, 1 - c if f[2] else c)


def _index(pos):
    return 4 * pos[0] + 2 * pos[1] + pos[2]


class _LocalCopy:
    def __init__(self, src, dst, sem):
        self.copy = pltpu.make_async_copy(src, dst, sem)

    def start(self):
        self.copy.start()

    def wait_send(self):
        self.copy.wait()

    def wait_recv(self):
        pass


def _descriptors(plan, bufs, send_sems, recv_sems):
    x, y, c = lax.axis_index("x"), lax.axis_index("y"), lax.axis_index("c")
    return [_LocalCopy(src, dst, send_sems[g].at[i]) if partner is None else
            pltpu.make_async_remote_copy(src_ref=src, dst_ref=dst, send_sem=send_sems[g].at[i],
                                         recv_sem=recv_sems[g].at[i], device_id=partner,
                                         device_id_type=pl.DeviceIdType.MESH)
            for g, copies in enumerate(plan(bufs, x, y, c)) for i, (src, dst, partner) in enumerate(copies)]


def _copy_start(name, bufs, plan, sizes):
    nb, ng = len(bufs), len(sizes)

    def body(*refs):
        for d in _descriptors(plan, refs[:nb], refs[nb:nb + ng], refs[nb + ng:nb + 2 * ng]):
            d.start()
        refs[2 * nb + 2 * ng][...] = jnp.zeros((8, LANE), F32)

    outs = pl.pallas_call(
        body, name=name,
        out_shape=[pltpu.SemaphoreType.DMA((s,)) for s in sizes] * 2 + [pltpu.HBM(b.shape, b.dtype) for b in bufs]
        + [jax.ShapeDtypeStruct((8, LANE), F32)],
        in_specs=[HBM_SPEC] * nb,
        out_specs=[SEM_SPEC] * (2 * ng) + [HBM_SPEC] * nb + [pl.BlockSpec(memory_space=pltpu.VMEM)],
        input_output_aliases={i: 2 * ng + i for i in range(nb)},
        compiler_params=pltpu.CompilerParams(has_side_effects=DATAFLOW),
    )(*[pltpu.with_memory_space_constraint(b, pltpu.HBM) for b in bufs])
    return outs[:ng], outs[ng:2 * ng], outs[2 * ng:2 * ng + nb], outs[2 * ng + nb]


def _copy_wait(name, bufs, send_sems, recv_sems, plan, after):
    nb, ng = len(bufs), len(send_sems)
    after = list(after) if isinstance(after, (list, tuple)) else [after]

    def body(*refs):
        for d in _descriptors(plan, refs[:nb], refs[nb:nb + ng], refs[nb + ng:nb + 2 * ng]):
            d.wait_send()
            d.wait_recv()

    return pl.pallas_call(
        body, name=name, out_shape=[pltpu.HBM(b.shape, b.dtype) for b in bufs],
        in_specs=[HBM_SPEC] * nb + [SEM_SPEC] * (2 * ng) + [pl.BlockSpec(memory_space=pl.ANY)] * len(after),
        out_specs=[HBM_SPEC] * nb, input_output_aliases={i: i for i in range(nb)},
        compiler_params=pltpu.CompilerParams(has_side_effects=DATAFLOW),
    )(*bufs, *send_sems, *recv_sems, *after)


def _lead_slot(ref, k):
    return ref.at[k]


def _col_slot(width):
    return lambda ref, k: ref.at[:, pl.ds(pl.multiple_of(k * width, LANE), width)]


def _gather_plan(peer_sets, slots):
    def plan(bufs, x, y, c):
        n = len(peer_sets)
        return [[(bufs[i], slots[i](bufs[n + i], _index((x, y, c))), _flip(x, y, c, f)) for f in peers]
                + [(bufs[i], slots[i](bufs[n + i], _index((x, y, c))), None)] for i, peers in enumerate(peer_sets)]
    return plan


def _forward_plan(slot):
    def plan(bufs, x, y, c):
        pieces = [slot(bufs[0], _index(_flip(x, y, c, f))) for f in OTHER_CHIPS]
        return [[(p, p, _flip(x, y, c, (0, 0, 1))) for p in pieces]]
    return plan


def _scatter_plan(slot):
    def plan(bufs, x, y, c):
        me = _index((x, y, c))
        peers = [_flip(x, y, c, f) for f in ALL_PEERS]
        return [[(slot(bufs[0], _index(p)), bufs[1].at[me], p) for p in peers]
                + [(slot(bufs[0], me), bufs[1].at[me], None)]]
    return plan


def _flat2(v):
    return v.reshape(-1, v.shape[-1])


def _matmul(name, kind, a, a_spec, b, b_spec, out_shape, out_spec, grid, res=None, res_spec=None, acc_shape=None):
    dims = {"nn": (((1,), (0,)), ((), ())), "nt": NT_DIMS, "nts": NT_DIMS, "tn": (((0,), (0,)), ((), ()))}[kind]
    nred = grid[-1]

    def body(*refs):
        if res is None:
            a_ref, b_ref, o_ref = refs[:3]
            r_ref = None
        else:
            a_ref, b_ref, r_ref, o_ref = refs[:4]
        if kind == "nts":
            n = b_ref.shape[-1]
            part = sum(lax.dot_general(a_ref[:, blk * n:(blk + 1) * n], b_ref[blk], dims, preferred_element_type=F32)
                       for blk in range(b_ref.shape[0]))
        else:
            part = lax.dot_general(_flat2(a_ref[...]), _flat2(b_ref[...]), dims, preferred_element_type=F32)

        def finish(total):
            if r_ref is not None:
                total = total + r_ref[...]
            o_ref[...] = total.reshape(o_ref.shape).astype(o_ref.dtype)

        if nred == 1:
            finish(part)
        else:
            acc_ref = refs[-1]
            k = pl.program_id(len(grid) - 1)

            @pl.when(k == 0)
            def _():
                acc_ref[...] = part

            @pl.when(jnp.logical_and(k > 0, k < nred - 1))
            def _():
                acc_ref[...] += part

            @pl.when(k == nred - 1)
            def _():
                finish(acc_ref[...] + part)

    ins, specs = [a, b], [a_spec, b_spec]
    if res is not None:
        ins.append(res)
        specs.append(res_spec)
    scratch = [] if nred == 1 else [pltpu.VMEM(acc_shape, F32)]
    return pl.pallas_call(
        body, name=name, grid=grid, in_specs=specs, out_specs=out_spec, out_shape=out_shape, scratch_shapes=scratch,
        compiler_params=_params(("parallel",) * (len(grid) - 1) + ("arbitrary",)),
    )(*ins)


def _nn_rows(name, a, wg, res, s, tn, tm):
    _, kj, n = wg.shape
    return _matmul(
        name, "nn", a, pl.BlockSpec((tm, s * kj), lambda j, i, r: (i, r)),
        wg, pl.BlockSpec((s, kj, tn), lambda j, i, r: (r, 0, j)),
        jax.ShapeDtypeStruct((T, n), F32), pl.BlockSpec((tm, tn), lambda j, i, r: (i, j)),
        (n // tn, T // tm, N_DEV // s), res=res, res_spec=pl.BlockSpec((tm, tn), lambda j, i, r: (i, j)),
        acc_shape=(tm, tn))


def _nt_cols(name, dc, dc_spec_of, w, nc):
    k, n = w.shape
    tm = tk = 1024
    return _matmul(
        name, "nt", dc, dc_spec_of(tm, nc),
        w, pl.BlockSpec((tk, nc), lambda kt, i, j: (kt, j)),
        jax.ShapeDtypeStruct((T, k), F32), pl.BlockSpec((tm, tk), lambda kt, i, j: (i, kt)),
        (k // tk, T // tm, n // nc), acc_shape=(tm, tk))


def _nt_rows(name, dc, wg, s, tm):
    _, kj, n = wg.shape
    return _matmul(
        name, "nt", dc, pl.BlockSpec((tm, n), lambda kt, i, r: (i, 0)),
        wg, pl.BlockSpec((s, kj, n), lambda kt, i, r: (kt, 0, 0)),
        jax.ShapeDtypeStruct((T, N_DEV * kj), F32), pl.BlockSpec((tm, s * kj), lambda kt, i, r: (i, kt)),
        (N_DEV // s, T // tm, 1))


def _tn_cols(name, a, dc, dc_spec_of, n, tn):
    k = a.shape[1]
    tk = 512
    return _matmul(
        name, "tn", a, pl.BlockSpec((T, tk), lambda j, kt, r: (0, kt)),
        dc, dc_spec_of(T, tn),
        jax.ShapeDtypeStruct((k, n), BF16), pl.BlockSpec((tk, tn), lambda j, kt, r: (kt, j)),
        (n // tn, k // tk, 1))


def _tn_rows(name, a, dc, kj, s, tn):
    n = dc.shape[1]
    return _matmul(
        name, "tn", a, pl.BlockSpec((T, s * kj), lambda kt, j, r: (0, kt)),
        dc, pl.BlockSpec((T, tn), lambda kt, j, r: (0, j)),
        jax.ShapeDtypeStruct((N_DEV, kj, n), BF16), pl.BlockSpec((s, kj, tn), lambda kt, j, r: (kt, 0, j)),
        (N_DEV // s, n // tn, 1))


TR = 256


def _rows(width):
    return pl.BlockSpec((TR, width), lambda i: (i, 0))


def _whole(shape):
    return pl.BlockSpec(shape, lambda i: (0,) * len(shape))


def _rmsnorm_rows(x_ref, g_ref):
    xv = x_ref[...]
    r = lax.rsqrt(jnp.mean(xv * xv, axis=-1, keepdims=True) + EPS)
    return ((xv * r) * g_ref[...]).astype(BF16)


def _prologue_matmul(name, prologue, ins, widths, w, w_block, w_index, tn, res=None):
    tm = 1024
    n = w.shape[-1]
    ni = len(ins)

    def body(*refs):
        w_ref = refs[ni]
        r_ref = refs[ni + 1] if res is not None else None
        h_ref, o_ref, h_scr = refs[-3:]

        @pl.when(pl.program_id(1) == 0)
        def _():
            h = prologue(*refs[:ni])
            h_scr[...] = h
            h_ref[...] = h

        total = jnp.dot(h_scr[...], _flat2(w_ref[...]), preferred_element_type=F32)
        if r_ref is not None:
            total = total + r_ref[...]
        o_ref[...] = total

    tile = pl.BlockSpec((tm, tn), lambda i, j: (i, j))
    specs = [pl.BlockSpec((1, D), lambda i, j: (0, 0)) if wd is None else pl.BlockSpec((tm, wd), lambda i, j: (i, 0))
             for wd in widths]
    specs.append(pl.BlockSpec(w_block, lambda i, j: w_index(j)))
    operands = list(ins) + [w]
    if res is not None:
        specs.append(tile)
        operands.append(res)
    return pl.pallas_call(
        body, name=name, grid=(T // tm, n // tn), in_specs=specs,
        out_specs=[pl.BlockSpec((tm, D), lambda i, j: (i, 0)), tile],
        out_shape=[jax.ShapeDtypeStruct((T, D), BF16), jax.ShapeDtypeStruct((T, n), F32)],
        scratch_shapes=[pltpu.VMEM((tm, D), BF16)], compiler_params=_params(("parallel", "arbitrary")),
    )(*operands)


def _rms_bwd_math(dy, xv, g):
    r = lax.rsqrt(jnp.mean(xv * xv, axis=-1, keepdims=True) + EPS)
    xhat = xv * r
    dxhat = dy * g
    dx = r * (dxhat - xhat * jnp.mean(dxhat * xhat, axis=-1, keepdims=True))
    return dx, dy * xhat


def _accumulate(ref, val):
    @pl.when(pl.program_id(0) == 0)
    def _():
        ref[...] = val

    @pl.when(pl.program_id(0) > 0)
    def _():
        ref[...] += val


def _rmsnorm_bwd(name, dy, x, g, res):
    def body(dy_ref, x_ref, g_ref, res_ref, dx_ref, dxb_ref, dg_ref):
        dx, dgr = _rms_bwd_math(dy_ref[...], x_ref[...], g_ref[...])
        tot = res_ref[...] + dx
        dx_ref[...] = tot
        dxb_ref[...] = tot.astype(BF16)
        _accumulate(dg_ref, jnp.sum(dgr, axis=0, keepdims=True))

    return pl.pallas_call(
        body, name=name, grid=(T // TR,), in_specs=[_rows(D), _rows(D), _whole((1, D)), _rows(D)],
        out_specs=[_rows(D), _rows(D), _whole((1, D))],
        out_shape=[jax.ShapeDtypeStruct((T, D), F32), jax.ShapeDtypeStruct((T, D), BF16),
                   jax.ShapeDtypeStruct((1, D), F32)],
        compiler_params=_params(("arbitrary",)),
    )(dy, x, g, res)


def _loss_head(x, g, target):
    def body(x_ref, g_ref, t_ref, loss_ref, dx_ref, dxb_ref, dg_ref):
        xv, gv = x_ref[...], g_ref[...]
        r = lax.rsqrt(jnp.mean(xv * xv, axis=-1, keepdims=True) + EPS)
        err = (xv * r) * gv - t_ref[...]
        part = 0.5 * jnp.sum(jnp.mean(err * err, axis=-1, keepdims=True))
        dx, dgr = _rms_bwd_math(err * (1.0 / D), xv, gv)
        dx_ref[...] = dx
        dxb_ref[...] = dx.astype(BF16)
        _accumulate(dg_ref, jnp.sum(dgr, axis=0, keepdims=True))
        _accumulate(loss_ref, jnp.full((8, LANE), part, F32))

    return pl.pallas_call(
        body, name="loss_head", grid=(T // TR,), in_specs=[_rows(D), _whole((1, D)), _rows(D)],
        out_specs=[_whole((8, LANE)), _rows(D), _rows(D), _whole((1, D))],
        out_shape=[jax.ShapeDtypeStruct((8, LANE), F32), jax.ShapeDtypeStruct((T, D), F32),
                   jax.ShapeDtypeStruct((T, D), BF16), jax.ShapeDtypeStruct((1, D), F32)],
        compiler_params=_params(("arbitrary",)),
    )(x, g, target)


MIX_OFFS = ((0, WA), (WA, WB), (WA + WB, WC))


def _mix_rows(oa_ref, ob_ref, oc_ref, g_ref):
    parts = []
    for ref, (off, w) in zip((oa_ref, ob_ref, oc_ref), MIX_OFFS):
        o = ref[...]
        r = lax.rsqrt(jnp.mean(o * o, axis=-1, keepdims=True) + EPS)
        parts.append(((o * r) * g_ref[:, off:off + w]).astype(BF16))
    return jnp.concatenate(parts, axis=1)


def _mix_bwd(name, dmixed, oa, ob, oc, gain):
    def body(dm_ref, oa_ref, ob_ref, oc_ref, g_ref, doa_ref, dob_ref, doc_ref, dg_ref):
        dgs = []
        for ref, dref, (off, w) in zip((oa_ref, ob_ref, oc_ref), (doa_ref, dob_ref, doc_ref), MIX_OFFS):
            dx, dgr = _rms_bwd_math(dm_ref[:, off:off + w], ref[...], g_ref[:, off:off + w])
            dref[...] = dx
            dgs.append(jnp.sum(dgr, axis=0, keepdims=True))
        _accumulate(dg_ref, jnp.concatenate(dgs, axis=1))

    return pl.pallas_call(
        body, name=name, grid=(T // TR,),
        in_specs=[_rows(D), _rows(WA), _rows(WB), _rows(WC), _whole((1, D))],
        out_specs=[_rows(WA), _rows(WB), _rows(WC), _whole((1, D))],
        out_shape=[jax.ShapeDtypeStruct((T, WA), F32), jax.ShapeDtypeStruct((T, WB), F32),
                   jax.ShapeDtypeStruct((T, WC), F32), jax.ShapeDtypeStruct((1, D), F32)],
        compiler_params=_params(("arbitrary",)),
    )(dmixed, oa, ob, oc, gain)


def _rope_tables():
    inv_freq = ROPE_THETA ** (-jnp.arange(0, HD, 2, dtype=F32) / HD)
    ang = jnp.arange(T, dtype=F32)[:, None] * inv_freq[None, :]
    cos, sin = jnp.cos(ang), jnp.sin(ang)
    cos2 = jnp.tile(jnp.concatenate([cos, cos], axis=1), (1, LANE // HD))
    sin2 = jnp.tile(jnp.concatenate([-sin, sin], axis=1), (1, LANE // HD))
    return cos2, sin2


def _rot_half(v):
    lane = lax.broadcasted_iota(jnp.int32, v.shape, 1)
    return jnp.where(lane % HD < HD // 2, pltpu.roll(v, LANE - HD // 2, 1), pltpu.roll(v, HD // 2, 1))


def _rope_fwd(name, proj, cos2, sin2):
    def body(p_ref, c_ref, s_ref, *outs):
        cv, sv = c_ref[...], s_ref[...]
        off = 0
        for o_ref, (_, w, rot, is_q) in zip(outs, GROUPS):
            for b in range(w // LANE):
                v = p_ref[:, off + b * LANE:off + (b + 1) * LANE]
                if rot:
                    v = v * cv + _rot_half(v) * sv
                if is_q:
                    v = v * (HD ** -0.5)
                o_ref[:, b * LANE:(b + 1) * LANE] = v.astype(BF16)
            off += w

    return pl.pallas_call(
        body, name=name, grid=(T // TR,), in_specs=[_rows(IN_COLS), _rows(LANE), _rows(LANE)],
        out_specs=[_rows(w) for _, w, _, _ in GROUPS],
        out_shape=[jax.ShapeDtypeStruct((T, w), BF16) for _, w, _, _ in GROUPS],
        compiler_params=_params(("parallel",)),
    )(proj, cos2, sin2)


def _rope_bwd(name, grads, cos2, sin2):
    def body(*refs):
        ins, (c_ref, s_ref, o_ref) = refs[:9], refs[9:]
        cv, sv = c_ref[...], s_ref[...]
        off = 0
        for d_ref, (_, w, rot, is_q) in zip(ins, GROUPS):
            for b in range(w // LANE):
                v = d_ref[:, b * LANE:(b + 1) * LANE]
                if is_q:
                    v = v * (HD ** -0.5)
                if rot:
                    v = v * cv + _rot_half(v * sv)
                o_ref[:, off + b * LANE:off + (b + 1) * LANE] = v.astype(BF16)
            off += w

    return pl.pallas_call(
        body, name=name, grid=(T // TR,), in_specs=[_rows(w) for _, w, _, _ in GROUPS] + [_rows(LANE), _rows(LANE)],
        out_specs=_rows(IN_COLS), out_shape=jax.ShapeDtypeStruct((T, IN_COLS), BF16),
        compiler_params=_params(("parallel",)),
    )(*grads, cos2, sin2)


NT_DIMS = (((1,), (1,)), ((), ()))
TN_DIMS = (((0,), (0,)), ((), ()))


def _scores(q, k, bias, valid):
    s = lax.dot_general(q, k, NT_DIMS, preferred_element_type=F32)
    if bias is not None:
        s = s + bias
    if valid is not None:
        s = jnp.where(valid, s, NEG)
    return s


def _heads_fwd(heads):
    scores = [_scores(h["q"], h["k"], h.get("bias"), h.get("valid")) for h in heads]
    soft = []
    for s, h in zip(scores, heads):
        m = jnp.max(s, axis=1, keepdims=True)
        e = jnp.exp(s - m)
        l = jnp.sum(e, axis=1, keepdims=True)
        if h.get("sink") is not None:
            l = l + jnp.exp(h["sink"] - m)
        soft.append((e.astype(BF16), l, m + jnp.log(l)))
    return [(jnp.dot(e, h["v"], preferred_element_type=F32) / l, lse) for (e, l, lse), h in zip(soft, heads)]


def _heads_bwd(heads):
    dobs = [h["do"].astype(BF16) for h in heads]
    scores = [_scores(h["q"], h["k"], h.get("bias"), h.get("valid")) for h in heads]
    dps = [lax.dot_general(dob, h["v"], NT_DIMS, preferred_element_type=F32) for dob, h in zip(dobs, heads)]
    mid = []
    for s, dp, h in zip(scores, dps, heads):
        p = jnp.exp(s - h["lse"])
        delta = jnp.sum(h["do"] * h["o"], axis=1, keepdims=True)
        ds = p * (dp - delta)
        dsink = None if h.get("sink") is None else -jnp.exp(h["sink"] - h["lse"]) * delta
        mid.append((p.astype(BF16), ds, dsink))
    out = []
    for (pb, ds, dsink), dob, h in zip(mid, dobs, heads):
        dsb = ds.astype(BF16)
        out.append((jnp.dot(dsb, h["k"], preferred_element_type=F32),
                    lax.dot_general(dsb, h["q"], TN_DIMS, preferred_element_type=F32),
                    lax.dot_general(pb, dob, TN_DIMS, preferred_element_type=F32), ds, dsink))
    return out


def _per_head(cols):
    return jnp.concatenate([jnp.broadcast_to(c, (c.shape[0], HD)) for c in cols], axis=1)


DILATIONS = ((128, 1), (512, 4), (2048, 16))


BQ_A = 256
REACH_A = max(window // 2 for window, _ in DILATIONS)


def _first_key(i):
    return jnp.maximum(i * BQ_A - REACH_A, 0)


def _key_window_groups():
    groups = {}
    for i in range(T // BQ_A):
        width = min(T, (i + 1) * BQ_A + REACH_A) - max(i * BQ_A - REACH_A, 0)
        groups.setdefault(width, []).append(i)
    return groups


def _per_window(i, fn):
    for width, tiles in _key_window_groups().items():
        hit = functools.reduce(jnp.logical_or, [i == t for t in tiles])
        pl.when(hit)(functools.partial(fn, pl.multiple_of(_first_key(i), BQ_A), width))


def _dilation_bias():
    def body(o_ref):
        i = pl.program_id(0)
        t = i * BQ_A + lax.broadcasted_iota(jnp.int32, (BQ_A, T), 0)
        ad = jnp.abs(t - (_first_key(i) + lax.broadcasted_iota(jnp.int32, (BQ_A, T), 1)))
        count = jnp.zeros((BQ_A, T), jnp.int32)
        for window, r in DILATIONS:
            count += jnp.where(((ad & (r - 1)) == 0) & (ad <= window // 2), 1, 0)
        logs = jnp.where(count == 2, jnp.log(2.0), jnp.where(count == 3, jnp.log(3.0), 0.0)).astype(F32)
        o_ref[...] = jnp.where(count == 0, NEG, logs)

    return pl.pallas_call(
        body, name="dilation_bias", grid=(T // BQ_A,), out_specs=pl.BlockSpec((BQ_A, T), lambda i: (i, 0)),
        out_shape=jax.ShapeDtypeStruct((T, T), F32), compiler_params=_params(("parallel",)),
    )()


def _attn_a_fwd(name, qa, ka, va, bias):
    def body(q_ref, k_ref, v_ref, b_ref, o_ref, lse_ref):
        def tile(first, width):
            b = b_ref[:, :width]
            outs = _heads_fwd([dict(q=q_ref[:, h * HD:(h + 1) * HD], k=k_ref[pl.ds(first, width), h * HD:(h + 1) * HD],
                                    v=v_ref[pl.ds(first, width), h * HD:(h + 1) * HD], bias=b) for h in range(2)])
            o_ref[...] = jnp.concatenate([o for o, _ in outs], axis=1)
            lse_ref[...] = _per_head([lse for _, lse in outs])

        _per_window(pl.program_id(1), tile)

    qs = pl.BlockSpec((BQ_A, LANE), lambda p, i: (i, p))
    ks = pl.BlockSpec((T, LANE), lambda p, i: (0, p))
    return pl.pallas_call(
        body, name=name, grid=(HA // 2, T // BQ_A),
        in_specs=[qs, ks, ks, pl.BlockSpec((BQ_A, T), lambda p, i: (i, 0))], out_specs=[qs, qs],
        out_shape=[jax.ShapeDtypeStruct((T, WA), F32)] * 2, compiler_params=_params(("parallel", "parallel")),
    )(qa, ka, va, bias)


def _attn_a_bwd(name, qa, ka, va, oa, lse, doa, bias):
    def body(q_ref, k_ref, v_ref, o_ref, lse_ref, do_ref, b_ref, dq_ref, dk_ref, dv_ref):
        @pl.when(pl.program_id(1) == 0)
        def _():
            dk_ref[...] = jnp.zeros_like(dk_ref)
            dv_ref[...] = jnp.zeros_like(dv_ref)

        def tile(first, width):
            b = b_ref[:, :width]
            keys = pl.ds(first, width)
            sls = [slice(h * HD, (h + 1) * HD) for h in range(2)]
            res = _heads_bwd([dict(q=q_ref[:, sl], k=k_ref[keys, sl], v=v_ref[keys, sl], o=o_ref[:, sl],
                                   do=do_ref[:, sl], lse=lse_ref[:, sl.start:sl.start + 1], bias=b) for sl in sls])
            dq_ref[...] = jnp.concatenate([r[0] for r in res], axis=1)
            dk_ref[keys, :] += jnp.concatenate([r[1] for r in res], axis=1)
            dv_ref[keys, :] += jnp.concatenate([r[2] for r in res], axis=1)

        _per_window(pl.program_id(1), tile)

    qs = pl.BlockSpec((BQ_A, LANE), lambda p, i: (i, p))
    ks = pl.BlockSpec((T, LANE), lambda p, i: (0, p))
    return pl.pallas_call(
        body, name=name, grid=(HA // 2, T // BQ_A),
        in_specs=[qs, ks, ks, qs, qs, qs, pl.BlockSpec((BQ_A, T), lambda p, i: (i, 0))], out_specs=[qs, ks, ks],
        out_shape=[jax.ShapeDtypeStruct((T, WA), F32)] * 3, compiler_params=_params(("parallel", "arbitrary")),
    )(qa, ka, va, oa, lse, doa, bias)


BQ_B = 128
SPAN_B = BQ_B + 2 * WINDOW_B


def _window_b(i):
    start = pl.multiple_of(jnp.clip(i * BQ_B - WINDOW_B, 0, T - SPAN_B), BQ_B)
    qpos = i * BQ_B + lax.broadcasted_iota(jnp.int32, (BQ_B, SPAN_B), 0)
    kpos = start + lax.broadcasted_iota(jnp.int32, (BQ_B, SPAN_B), 1)
    return start, jnp.abs(qpos - kpos) <= WINDOW_B


GROUP_B = HB // HKV


def _stack_group(ref, g):
    return jnp.concatenate([ref[:, h * HD:(h + 1) * HD] for h in range(g * GROUP_B, (g + 1) * GROUP_B)], axis=0)


def _sink_column(sink_ref, g):
    return jnp.concatenate([jnp.full((BQ_B, 1), sink_ref[h], F32) for h in range(g * GROUP_B, (g + 1) * GROUP_B)],
                           axis=0)


def _unstack(stacked):
    return [s[j * BQ_B:(j + 1) * BQ_B] for s in stacked for j in range(GROUP_B)]


def _attn_b_fwd(name, qb, kb, vb, sink):
    def body(sink_ref, q_ref, k_ref, v_ref, o_ref, lse_ref):
        start, valid = _window_b(pl.program_id(0))
        valid = jnp.concatenate([valid] * GROUP_B, axis=0)
        kw, vw = k_ref[pl.ds(start, SPAN_B), :], v_ref[pl.ds(start, SPAN_B), :]
        outs = _heads_fwd([dict(q=_stack_group(q_ref, g), k=kw[:, g * HD:(g + 1) * HD], v=vw[:, g * HD:(g + 1) * HD],
                                valid=valid, sink=_sink_column(sink_ref, g)) for g in range(HKV)])
        o_ref[...] = jnp.concatenate(_unstack([o for o, _ in outs]), axis=1)
        lse_ref[...] = _per_head(_unstack([lse for _, lse in outs]))

    qs = pl.BlockSpec((BQ_B, WB), lambda i: (i, 0))
    return pl.pallas_call(
        body, name=name, grid=(T // BQ_B,),
        in_specs=[pl.BlockSpec(memory_space=pltpu.SMEM), qs, _whole((T, WKV)), _whole((T, WKV))],
        out_specs=[qs, qs],
        out_shape=[jax.ShapeDtypeStruct((T, WB), F32)] * 2, compiler_params=_params(("parallel",)),
    )(sink, qb, kb, vb)


def _attn_b_bwd(name, qb, kb, vb, ob, lse, dob, sink):
    def body(sink_ref, q_ref, k_ref, v_ref, o_ref, lse_ref, do_ref, dq_ref, dk_ref, dv_ref, dsink_ref):
        i = pl.program_id(0)
        start, valid = _window_b(i)
        valid = jnp.concatenate([valid] * GROUP_B, axis=0)
        kw, vw = k_ref[pl.ds(start, SPAN_B), :], v_ref[pl.ds(start, SPAN_B), :]
        res = _heads_bwd([dict(q=_stack_group(q_ref, g), k=kw[:, g * HD:(g + 1) * HD], v=vw[:, g * HD:(g + 1) * HD],
                               o=_stack_group(o_ref, g), do=_stack_group(do_ref, g),
                               lse=jnp.concatenate([lse_ref[:, h * HD:h * HD + 1]
                                                    for h in range(g * GROUP_B, (g + 1) * GROUP_B)], axis=0),
                               valid=valid, sink=_sink_column(sink_ref, g)) for g in range(HKV)])
        dks, dvs = [r[1] for r in res], [r[2] for r in res]
        lane = lax.broadcasted_iota(jnp.int32, (1, LANE), 1)
        dsink = jnp.zeros((1, LANE), F32)
        for h, rows in enumerate(_unstack([r[4] for r in res])):
            dsink += jnp.where(lane == h, jnp.sum(rows), 0.0)
        dq_ref[...] = jnp.concatenate(_unstack([r[0] for r in res]), axis=1)

        @pl.when(i == 0)
        def _():
            dk_ref[...] = jnp.zeros_like(dk_ref)
            dv_ref[...] = jnp.zeros_like(dv_ref)
            dsink_ref[...] = jnp.zeros_like(dsink_ref)

        dk_ref[pl.ds(start, SPAN_B), :] += jnp.concatenate(dks, axis=1)
        dv_ref[pl.ds(start, SPAN_B), :] += jnp.concatenate(dvs, axis=1)
        dsink_ref[...] += dsink

    qs = pl.BlockSpec((BQ_B, WB), lambda i: (i, 0))
    return pl.pallas_call(
        body, name=name, grid=(T // BQ_B,),
        in_specs=[pl.BlockSpec(memory_space=pltpu.SMEM), qs, _whole((T, WKV)), _whole((T, WKV)), qs, qs, qs],
        out_specs=[qs, _whole((T, WKV)), _whole((T, WKV)), _whole((1, LANE))],
        out_shape=[jax.ShapeDtypeStruct((T, WB), F32), jax.ShapeDtypeStruct((T, WKV), F32),
                   jax.ShapeDtypeStruct((T, WKV), F32), jax.ShapeDtypeStruct((1, LANE), F32)],
        compiler_params=_params(("arbitrary",)),
    )(sink, qb, kb, vb, ob, lse, dob)


SPAN_C = NA_ROWS * GRID_W


def _row_start(r):
    return jnp.clip(r - NA_ROWS // 2, 0, ROWS - NA_ROWS)


def _off_index(r):
    return _row_start(r) - r + (NA_ROWS - 1)


N_TAB = 16
RPS = 4


def _rpb_tables(name, rpb):
    circ = jnp.concatenate([rpb[..., NA_COLS - 1:], jnp.zeros(rpb.shape[:2] + (LANE - (2 * NA_COLS - 1),), F32),
                            rpb[..., :NA_COLS - 1]], axis=-1)
    circ = jnp.pad(circ, ((0, 0), (0, N_TAB + 1 - circ.shape[1]), (0, 0)))

    def body(w_ref, o_ref):
        c = lax.broadcasted_iota(jnp.int32, (GRID_W, LANE), 0)
        lane = lax.broadcasted_iota(jnp.int32, (GRID_W, LANE), 1)
        cs = jnp.clip(c - NA_COLS // 2, 0, GRID_W - NA_COLS)
        valid = (lane % GRID_W >= cs) & (lane % GRID_W < cs + NA_COLS)
        toep = [pltpu.roll(jnp.broadcast_to(w_ref[a:a + 1, :], (GRID_W, LANE)), 0, 1, stride=1, stride_axis=0)
                for a in range(N_TAB + 1)]
        for a in range(N_TAB):
            pair = jnp.where(lane < GRID_W, toep[a], pltpu.roll(toep[a + 1], GRID_W, 1))
            o_ref[a] = jnp.where(valid, pair, NEG)

    return pl.pallas_call(
        body, name=name, grid=(HC,),
        in_specs=[pl.BlockSpec((None, N_TAB + 1, LANE), lambda h: (h, 0, 0))],
        out_specs=pl.BlockSpec((None, N_TAB, GRID_W, LANE), lambda h: (h, 0, 0, 0)),
        out_shape=jax.ShapeDtypeStruct((HC, N_TAB, GRID_W, LANE), F32), compiler_params=_params(("parallel",)),
    )(circ)


def _bias_c(t_ref, h, d):
    return jnp.concatenate([t_ref[h, d + k] for k in range(0, NA_ROWS, 2)], axis=1)


def _attn_c_fwd(name, qc, kc, vc, tables):
    def body(q_ref, k_ref, v_ref, t_ref, o_ref, lse_ref):
        heads = []
        for rr in range(RPS):
            r = pl.program_id(1) * RPS + rr
            rows = slice(rr * GRID_W, (rr + 1) * GRID_W)
            start = pl.multiple_of(_row_start(r) * GRID_W, GRID_W)
            kw, vw = k_ref[pl.ds(start, SPAN_C), :], v_ref[pl.ds(start, SPAN_C), :]
            heads += [dict(q=q_ref[rows, h * HD:(h + 1) * HD], k=kw[:, h * HD:(h + 1) * HD], v=vw[:, h * HD:(h + 1) * HD],
                           bias=_bias_c(t_ref, h, _off_index(r))) for h in range(2)]
        outs = _heads_fwd(heads)
        for rr in range(RPS):
            rows = slice(rr * GRID_W, (rr + 1) * GRID_W)
            o_ref[rows, :] = jnp.concatenate([o for o, _ in outs[2 * rr:2 * rr + 2]], axis=1)
            lse_ref[rows, :] = _per_head([lse for _, lse in outs[2 * rr:2 * rr + 2]])

    qs = pl.BlockSpec((RPS * GRID_W, LANE), lambda p, r: (r, p))
    ks = pl.BlockSpec((T, LANE), lambda p, r: (0, p))
    ts = pl.BlockSpec((2, N_TAB, GRID_W, LANE), lambda p, r: (p, 0, 0, 0))
    return pl.pallas_call(
        body, name=name, grid=(HC // 2, ROWS // RPS), in_specs=[qs, ks, ks, ts], out_specs=[qs, qs],
        out_shape=[jax.ShapeDtypeStruct((T, WC), F32)] * 2, compiler_params=_params(("parallel", "parallel")),
    )(qc, kc, vc, tables)


def _attn_c_bwd(name, qc, kc, vc, oc, lse, doc, tables):
    def body(q_ref, k_ref, v_ref, o_ref, lse_ref, do_ref, t_ref, dq_ref, dk_ref, dv_ref, dt_ref):
        @pl.when(pl.program_id(1) == 0)
        def _():
            dk_ref[...] = jnp.zeros_like(dk_ref)
            dv_ref[...] = jnp.zeros_like(dv_ref)
            dt_ref[...] = jnp.zeros_like(dt_ref)

        heads, where = [], []
        for rr in range(RPS):
            r = pl.program_id(1) * RPS + rr
            rows = slice(rr * GRID_W, (rr + 1) * GRID_W)
            d = _off_index(r)
            start = pl.multiple_of(_row_start(r) * GRID_W, GRID_W)
            kw, vw = k_ref[pl.ds(start, SPAN_C), :], v_ref[pl.ds(start, SPAN_C), :]
            where.append((rows, d, start))
            for h in range(2):
                sl = slice(h * HD, (h + 1) * HD)
                heads.append(dict(q=q_ref[rows, sl], k=kw[:, sl], v=vw[:, sl], o=o_ref[rows, sl], do=do_ref[rows, sl],
                                  lse=lse_ref[rows, h * HD:h * HD + 1], bias=_bias_c(t_ref, h, d)))
        res = _heads_bwd(heads)
        for rr, (rows, d, start) in enumerate(where):
            pair = res[2 * rr:2 * rr + 2]
            for h in range(2):
                for k in range(0, NA_ROWS, 2):
                    dt_ref[h, d + k] += pair[h][3][:, k * GRID_W:(k + 2) * GRID_W]
            dq_ref[rows, :] = jnp.concatenate([p[0] for p in pair], axis=1)
            dk_ref[pl.ds(start, SPAN_C), :] += jnp.concatenate([p[1] for p in pair], axis=1)
            dv_ref[pl.ds(start, SPAN_C), :] += jnp.concatenate([p[2] for p in pair], axis=1)

    qs = pl.BlockSpec((RPS * GRID_W, LANE), lambda p, r: (r, p))
    ks = pl.BlockSpec((T, LANE), lambda p, r: (0, p))
    ts = pl.BlockSpec((2, N_TAB, GRID_W, LANE), lambda p, r: (p, 0, 0, 0))
    return pl.pallas_call(
        body, name=name, grid=(HC // 2, ROWS // RPS), in_specs=[qs, ks, ks, qs, qs, qs, ts],
        out_specs=[qs, ks, ks, ts],
        out_shape=[jax.ShapeDtypeStruct((T, WC), F32)] * 3 + [jax.ShapeDtypeStruct((HC, N_TAB, GRID_W, LANE), F32)],
        compiler_params=_params(("parallel", "arbitrary")),
    )(qc, kc, vc, oc, lse, doc, tables)


def _split3(v):
    hi = v.astype(BF16)
    r1 = v - hi.astype(F32)
    mid = r1.astype(BF16)
    lo = (r1 - mid.astype(F32)).astype(BF16)
    return hi, mid, lo


def _rpb_reduce(name, dtables):
    x = dtables.reshape(HC, N_TAB, GRID_W * LANE)
    c = jnp.arange(GRID_W)[:, None]
    lane = jnp.arange(LANE)[None, :]
    col = (lane // GRID_W) * LANE + jnp.clip(lane % GRID_W - c + (NA_COLS - 1), 0, 2 * NA_COLS - 2)
    col_onehot = (col.reshape(-1)[:, None] == jnp.arange(2 * LANE)[None, :]).astype(BF16)
    a2 = jnp.arange(N_TAB)[None, :]
    row_onehot = jnp.concatenate([(jnp.arange(16)[:, None] == a2 + u) & (a2 < 2 * NA_ROWS - 2) for u in range(2)],
                                 axis=1).astype(BF16)

    def body(x_ref, e_ref, f_ref, o_ref):
        y = sum(jnp.dot(part, e_ref[...], preferred_element_type=F32) for part in _split3(x_ref[...]))
        z = jnp.concatenate([y[:, :LANE], y[:, LANE:]], axis=0)
        o_ref[...] = sum(jnp.dot(f_ref[...], part, preferred_element_type=F32) for part in _split3(z))

    out = pl.pallas_call(
        body, name=name, grid=(HC,),
        in_specs=[pl.BlockSpec((None, N_TAB, GRID_W * LANE), lambda h: (h, 0, 0)),
                  _whole((GRID_W * LANE, 2 * LANE)), _whole((16, 2 * N_TAB))],
        out_specs=pl.BlockSpec((None, 16, LANE), lambda h: (h, 0, 0)),
        out_shape=jax.ShapeDtypeStruct((HC, 16, LANE), F32), compiler_params=_params(("parallel",)),
    )(x, col_onehot, row_onehot)
    return out[:, :2 * NA_ROWS - 1, :2 * NA_COLS - 1]


TC = 128
NCB = DFF // TC
CHUNK = 128
MARGIN = 8


def _shift_down(v, rows):
    return jnp.where(rows == 0, 0.0, pltpu.roll(v, 1, 0))


def _shift_up(v, rows):
    return jnp.where(rows == T - 1, 0.0, pltpu.roll(v, T - 1, 0))


def _conv(v, w, b, rows):
    return _shift_down(v, rows) * w[0:1] + v * w[1:2] + _shift_up(v, rows) * w[2:3] + b


def _ffn_specs():
    gate = lambda shape: pl.BlockSpec(shape, lambda j: (0, j))
    val = lambda shape: pl.BlockSpec(shape, lambda j: (0, j + NCB))
    return [gate((T, TC)), val((T, TC)), gate((3, TC)), val((3, TC)), gate((1, TC)), val((1, TC))]


def _ffn_mid_fwd(name, up, conv_w, conv_b):
    def body(xg_ref, xv_ref, wg_ref, wv_ref, bg_ref, bv_ref, o_ref):
        rows = lax.broadcasted_iota(jnp.int32, (T, TC), 0)
        ug = _conv(xg_ref[...], wg_ref[...], bg_ref[...], rows)
        uv = _conv(xv_ref[...], wv_ref[...], bv_ref[...], rows)
        o_ref[...] = (ug * jax.nn.sigmoid(ug) * uv).astype(BF16)

    return pl.pallas_call(
        body, name=name, grid=(NCB,), in_specs=_ffn_specs(), out_specs=pl.BlockSpec((T, TC), lambda j: (0, j)),
        out_shape=jax.ShapeDtypeStruct((T, DFF), BF16), compiler_params=_params(("parallel",)),
    )(up, up, conv_w, conv_w, conv_b, conv_b)


def _ffn_mid_bwd(name, dact, up, conv_w, conv_b):
    window = CHUNK + 2 * MARGIN
    centre = slice(MARGIN, MARGIN + CHUNK)

    def shifted(v):
        return pltpu.roll(v, 1, 0), pltpu.roll(v, window - 1, 0)

    def fold(v):
        return jnp.sum(v[centre].reshape(CHUNK // 8, 8, TC), axis=0)

    def body(da_ref, xg_ref, xv_ref, wg_ref, wv_ref, bg_ref, bv_ref, dx_ref, dw_ref, db_ref, dap, xgp, xvp):
        for src, pad in ((da_ref, dap), (xg_ref, xgp), (xv_ref, xvp)):
            pad[0:MARGIN, :] = jnp.zeros((MARGIN, TC), F32)
            pad[MARGIN:MARGIN + T, :] = src[...]
            pad[MARGIN + T:, :] = jnp.zeros((MARGIN, TC), F32)
        wg, wv, bg, bv = wg_ref[...], wv_ref[...], bg_ref[...], bv_ref[...]

        def chunk(c, sums):
            r0 = pl.multiple_of(c * CHUNK, CHUNK)
            da, xg, xv = dap[pl.ds(r0, window), :], xgp[pl.ds(r0, window), :], xvp[pl.ds(r0, window), :]
            xg_prev, xg_next = shifted(xg)
            xv_prev, xv_next = shifted(xv)
            ug = xg_prev * wg[0:1] + xg * wg[1:2] + xg_next * wg[2:3] + bg
            uv = xv_prev * wv[0:1] + xv * wv[1:2] + xv_next * wv[2:3] + bv
            sg = jax.nn.sigmoid(ug)
            dug = da * uv * (sg * (1.0 + ug * (1.0 - sg)))
            duv = da * (ug * sg)
            out = []
            for half, (x_prev, x, x_next, w, du) in enumerate(((xg_prev, xg, xg_next, wg, dug),
                                                               (xv_prev, xv, xv_next, wv, duv))):
                du_prev, du_next = shifted(du)
                dx = du_next * w[0:1] + du * w[1:2] + du_prev * w[2:3]
                dx_ref[half, pl.ds(r0, CHUNK), :] = dx[centre].astype(BF16)
                out += [fold(x_prev * du), fold(x * du), fold(x_next * du), fold(du)]
            return tuple(s + o for s, o in zip(sums, out))

        sums = lax.fori_loop(0, T // CHUNK, chunk, tuple(jnp.zeros((8, TC), F32) for _ in range(8)))
        rows = [jnp.sum(s, axis=0, keepdims=True) for s in sums]
        for half in range(2):
            dw_ref[half] = jnp.concatenate(rows[4 * half:4 * half + 3], axis=0)
            db_ref[half] = rows[4 * half + 3]

    return pl.pallas_call(
        body, name=name, grid=(NCB,), in_specs=[pl.BlockSpec((T, TC), lambda j: (0, j))] + _ffn_specs(),
        out_specs=[pl.BlockSpec((2, T, TC), lambda j: (0, 0, j)), pl.BlockSpec((2, 3, TC), lambda j: (0, 0, j)),
                   pl.BlockSpec((2, 1, TC), lambda j: (0, 0, j))],
        out_shape=[jax.ShapeDtypeStruct((2, T, DFF), BF16), jax.ShapeDtypeStruct((2, 3, DFF), F32),
                   jax.ShapeDtypeStruct((2, 1, DFF), F32)],
        scratch_shapes=[pltpu.VMEM((T + 2 * MARGIN, TC), F32)] * 3,
        compiler_params=_params(("parallel",)),
    )(dact, up, up, conv_w, conv_w, conv_b, conv_b)


def _dup_spec(tm, nj):
    per = DFF // nj
    return pl.BlockSpec((None, tm, nj), lambda a, b, j: (j // per, 0 if tm == T else b, j % per))


def _dup_spec_tn(tm, nj):
    per = DFF // nj
    return pl.BlockSpec((None, tm, nj), lambda j, kt, r: (j // per, 0, j % per))


def _adamw_math(w, g, m, v):
    m = ADAM_B1 * m + (1.0 - ADAM_B1) * g
    v = ADAM_B2 * v + (1.0 - ADAM_B2) * (g * g)
    m_hat = m / (1.0 - ADAM_B1 ** ADAM_STEP)
    v_hat = v / (1.0 - ADAM_B2 ** ADAM_STEP)
    delta = -ADAM_LR * (m_hat / (jnp.sqrt(v_hat) + ADAM_EPS) + ADAM_WD * w)
    return delta, m, v


ADAM_BLOCK = 256 * 1408


def _adamw_sharded(name, w, m, v, parts):
    _, r, c = w.shape
    tr = max(t for t in range(16, r + 1, 16) if r % t == 0 and t * c <= ADAM_BLOCK)

    def body(w_ref, m_ref, v_ref, p0_ref, p1_ref, g_ref, d_ref, nm_ref, nv_ref):
        def run(p_ref):
            g = p_ref[0].astype(F32)
            for k in range(1, N_DEV):
                g = g + p_ref[k].astype(F32)
            d, nm, nv = _adamw_math(w_ref[...], g, m_ref[...], v_ref[...])
            g_ref[...] = g
            d_ref[...] = d
            nm_ref[...] = nm
            nv_ref[...] = nv

        @pl.when(pl.program_id(0) == 0)
        def _():
            run(p0_ref)

        @pl.when(pl.program_id(0) == 1)
        def _():
            run(p1_ref)

    ws = pl.BlockSpec((None, tr, c), lambda l, i: (l, i, 0))
    p0 = pl.BlockSpec((N_DEV, tr, c), lambda l, i: (0, jnp.where(l == 0, i, r // tr - 1), 0))
    p1 = pl.BlockSpec((N_DEV, tr, c), lambda l, i: (0, jnp.where(l == 1, i, 0), 0))
    return pl.pallas_call(
        body, name=name, grid=(DEPTH, r // tr), in_specs=[ws, ws, ws, p0, p1], out_specs=[ws] * 4,
        out_shape=[jax.ShapeDtypeStruct(w.shape, F32)] * 4, compiler_params=_params(("arbitrary", "arbitrary")),
    )(w, m, v, *parts)


def _sum_devices(name, parts):
    r = parts.shape[1]

    def body(p_ref, o_ref):
        g = p_ref[0]
        for k in range(1, N_DEV):
            g = g + p_ref[k]
        o_ref[...] = g

    return pl.pallas_call(
        body, name=name, in_specs=[pl.BlockSpec((N_DEV, r, LANE), lambda: (0, 0, 0))],
        out_specs=pl.BlockSpec((r, LANE), lambda: (0, 0)), out_shape=jax.ShapeDtypeStruct((r, LANE), F32),
        compiler_params=_params(),
    )(parts)


def _adamw_small(name, ws, gs, ms, vs):
    n = len(ws)
    shapes = [w.shape for w in ws]
    ws, gs, ms, vs = ([a.reshape(1, -1) if a.ndim == 1 else a for a in arrs] for arrs in (ws, gs, ms, vs))
    specs = [pl.BlockSpec(memory_space=pltpu.VMEM)] * n

    def body(*refs):
        for i in range(n):
            w_ref, g_ref, m_ref, v_ref = (refs[k * n + i] for k in range(4))
            d, nm, nv = _adamw_math(w_ref[...], g_ref[...], m_ref[...], v_ref[...])
            refs[4 * n + i][...] = d
            refs[5 * n + i][...] = nm
            refs[6 * n + i][...] = nv

    outs = pl.pallas_call(
        body, name=name, in_specs=specs * 4, out_specs=specs * 3,
        out_shape=[jax.ShapeDtypeStruct(w.shape, F32) for w in ws] * 3, compiler_params=_params(),
    )(*ws, *gs, *ms, *vs)
    outs = [o.reshape(shapes[i % n]) for i, o in enumerate(outs)]
    return outs[:n], outs[n:2 * n], outs[2 * n:]


def _pack(arrays):
    flat = jnp.concatenate([a.reshape(-1) for a in arrays])
    pad = (-flat.shape[0]) % (8 * LANE)
    return jnp.pad(flat, (0, pad)).reshape(-1, LANE)


def _unpack(buf, shapes):
    flat, out, off = buf.reshape(-1), [], 0
    for s in shapes:
        n = 1
        for d in s:
            n *= d
        out.append(flat[off:off + n].reshape(s))
        off += n
    return out


def _local_step(x, target, small, weights, conv_w_full, hand_over, used):
    cos2, sin2 = _rope_tables()
    bias_a = _dilation_bias()
    tables = [_rpb_tables(f"rpb_tables_{l}", small["rpb_c"][l]) for l in range(DEPTH)]
    saved, carry = [], 0.0
    for l in range(DEPTH):
        g1, g2 = small["ln_attn"][l][None] + carry, small["ln_ffn"][l][None]
        gain, sink, cb = small["mix_gain"][l][None], small["sink_b"][l], small["conv_b"][l][None]
        cw = conv_w_full[l]
        bias = tables[l]
        h1, proj = _prologue_matmul(f"proj_in_{l}", _rmsnorm_rows, [x, g1], [D, None],
                                    weights("w_in", l, [cos2, sin2, bias_a] + tables if l == 0 else x),
                                    (D, 1024), lambda j: (0, j), 1024)
        zero = used(f"proj_in_{l}", proj)
        qa, ka, va, qb, kb, vb, qc, kc, vc = _rope_fwd(f"rope_{l}", proj, cos2, sin2)
        oa, lse_a = _attn_a_fwd(f"attn_a_{l}", qa, ka, va, bias_a)
        ob, lse_b = _attn_b_fwd(f"attn_b_{l}", qb, kb, vb, sink + zero)
        oc, lse_c = _attn_c_fwd(f"attn_c_{l}", qc, kc, vc, bias)
        mixed, x_mid = _prologue_matmul(f"proj_out_{l}", _mix_rows, [oa, ob, oc, gain + used(f"attn_{l}", oc)],
                                        [WA, WB, WC, None],
                                        weights("w_out", l, oc), (N_DEV, D // N_DEV, 512), lambda j: (0, 0, j), 512,
                                        res=x)
        h2, up = _prologue_matmul(f"ffn_up_{l}", _rmsnorm_rows, [x_mid, g2 + used(f"proj_out_{l}", x_mid)], [D, None],
                                  weights("w_up", l, x_mid), (D, 1024), lambda j: (0, j), 1024)
        act = _ffn_mid_fwd(f"ffn_mid_{l}", up, cw, cb + used(f"ffn_up_{l}", up))
        x_out = _nn_rows(f"ffn_down_{l}", act, weights("w_down", l, act), x_mid, 4, 1024, 1024)
        carry = used(f"ffn_down_{l}", x_out)
        saved.append(dict(x=x, h1=h1, qkv=(qa, ka, va, qb, kb, vb, qc, kc, vc), o=(oa, ob, oc), lse=(lse_a, lse_b, lse_c), mixed=mixed,
                          x_mid=x_mid, h2=h2, up=up, act=act, g1=g1, g2=g2, gain=gain, sink=sink, cb=cb, cw=cw, bias=bias))
        x = x_out

    loss8, dx, dxb, d_ln_final = _loss_head(x, small["ln_final"][None], target)
    sgrads = [None] * DEPTH
    for l in reversed(range(DEPTH)):
        s = saved[l]
        qa, ka, va, qb, kb, vb, qc, kc, vc = s["qkv"]
        oa, ob, oc = s["o"]
        wg_in, wg_out = weights("w_in", l, None), weights("w_out", l, None)
        wg_up, wg_down = weights("w_up", l, None), weights("w_down", l, None)
        g_down = _tn_rows(f"wgrad_down_{l}", s["act"], dxb, wg_down.shape[1], 2, 512)
        zero = hand_over("w_down", l, g_down)
        dact = _nt_rows(f"dgrad_down_{l}", dxb, wg_down, 4, 512)
        dup, d_cw, d_cb = _ffn_mid_bwd(f"ffn_mid_bwd_{l}", dact, s["up"], s["cw"], s["cb"] + zero)
        g_up = _tn_cols(f"wgrad_up_{l}", s["h2"], dup, _dup_spec_tn, 2 * DFF, DFF // 2)
        zero = hand_over("w_up", l, g_up)
        dh2 = _nt_cols(f"dgrad_up_{l}", dup, _dup_spec, wg_up, DFF // 2)
        dx, dxb, d_g2 = _rmsnorm_bwd(f"norm_ffn_bwd_{l}", dh2, s["x_mid"], s["g2"] + zero, dx)
        g_out = _tn_rows(f"wgrad_out_{l}", s["mixed"], dxb, wg_out.shape[1], 2, D)
        zero = hand_over("w_out", l, g_out)
        dmixed = _nt_rows(f"dgrad_out_{l}", dxb, wg_out, 2, T)
        doa, dob, doc, d_gain = _mix_bwd(f"mix_bwd_{l}", dmixed, oa, ob, oc, s["gain"] + zero)
        lse_a, lse_b, lse_c = s["lse"]
        dqa, dka, dva = _attn_a_bwd(f"attn_a_bwd_{l}", qa, ka, va, oa, lse_a, doa, bias_a)
        dqb, dkb, dvb, d_sink = _attn_b_bwd(f"attn_b_bwd_{l}", qb, kb, vb, ob, lse_b, dob, s["sink"])
        dqc, dkc, dvc, d_bias = _attn_c_bwd(f"attn_c_bwd_{l}", qc, kc, vc, oc, lse_c, doc, s["bias"])
        d_rpb = _rpb_reduce(f"rpb_reduce_{l}", d_bias)
        dproj = _rope_bwd(f"rope_bwd_{l}", (dqa, dka, dva, dqb, dkb, dvb, dqc, dkc, dvc), cos2, sin2)
        g_in = _tn_cols(f"wgrad_in_{l}", s["h1"], dproj,
                        lambda tm, tn: pl.BlockSpec((tm, tn), lambda j, kt, r: (0, j)), IN_COLS, 1024)
        zero = hand_over("w_in", l, g_in)
        dh1 = _nt_cols(f"dgrad_in_{l}", dproj, lambda tm, nc: pl.BlockSpec((tm, nc), lambda kt, i, j: (i, j)), wg_in,
                       IN_COLS // 2)
        dx, dxb, d_g1 = _rmsnorm_bwd(f"norm_attn_bwd_{l}", dh1, s["x"], s["g1"] + zero, dx)
        sgrads[l] = dict(ln_attn=d_g1[0], sink_b=d_sink[0, :HB], rpb_c=d_rpb, mix_gain=d_gain[0], ln_ffn=d_g2[0],
                         conv_w=d_cw.transpose(1, 0, 2).reshape(3, 2 * DFF), conv_b=d_cb.reshape(2 * DFF))
    return loss8[0, 0], dx, d_ln_final[0], sgrads


SMALL_NAMES = ("ln_attn", "sink_b", "rpb_c", "mix_gain", "ln_ffn", "conv_b")


def kernel(x, ln_attn, w_in, sink_b, rpb_c, mix_gain, w_out, ln_ffn, w_up, conv_w, conv_b, w_down, ln_final, loss_target, m_ln_attn, m_w_in, m_sink_b, m_rpb_c, m_mix_gain, m_w_out, m_ln_ffn, m_w_up, m_conv_w, m_conv_b, m_w_down, m_ln_final, v_ln_attn, v_w_in, v_sink_b, v_rpb_c, v_mix_gain, v_w_out, v_ln_ffn, v_w_up, v_conv_w, v_conv_b, v_w_down, v_ln_final):
    me = 4 * lax.axis_index("x") + 2 * lax.axis_index("y") + lax.axis_index("c")
    small = dict(ln_attn=ln_attn, sink_b=sink_b, rpb_c=rpb_c, mix_gain=mix_gain, ln_ffn=ln_ffn, conv_b=conv_b,
                 ln_final=ln_final)

    names = ("w_in", "w_out", "w_up", "w_down")
    shards = dict(w_in=w_in, w_out=w_out, w_up=w_up, w_down=w_down)
    order = [(n, l) for l in range(DEPTH) for n in names]
    conv_key = ("conv_w", 0)
    started, arrived, forwarded, gathered = {}, {}, {}, {}

    def side_by_side(k):
        return k[0] in ("w_in", "w_up")

    def slot_of(k):
        return _col_slot(shards[k[0]].shape[2]) if side_by_side(k) else _lead_slot

    def begin(name, ks, zero):
        srcs = [_pack([conv_w]) + zero if k == conv_key else (shards[k[0]][k[1]] + zero).astype(BF16) for k in ks]
        lands = [lax.empty((s.shape[0], N_DEV * s.shape[1]) if side_by_side(k) else (N_DEV,) + s.shape, s.dtype)
                 for k, s in zip(ks, srcs)]
        peers = [ALL_PEERS if k == conv_key else NEAR_PEERS for k in ks]
        send, recv, bufs, tok = _copy_start(name, srcs + lands, _gather_plan(peers, [slot_of(k) for k in ks]),
                                            [len(p) + 1 for p in peers])
        for i, k in enumerate(ks):
            started[k] = (send[i], recv[i], bufs[i], bufs[len(ks) + i], peers[i])
        return tok

    token = begin("gather_start_first", order[:1], 0.0)
    token = begin("gather_start_rest", [conv_key] + order[1:], token[0, 0])

    def arrive(k, after):
        send, recv, src, land, peers = started[k]
        arrived[k] = _copy_wait(f"gather_{k[0]}_{k[1]}_arrived", [src, land], [send], [recv],
                                _gather_plan([peers], [slot_of(k)]), after)

    queue = list(order)

    def advance(after):
        if not queue:
            return 0.0
        k = queue.pop(0)
        arrive(k, after)
        forwarded[k] = _copy_start(f"gather_{k[0]}_{k[1]}_forward", [arrived[k][1]], _forward_plan(slot_of(k)),
                                   [len(OTHER_CHIPS)])
        return forwarded[k][3][0, 0]

    pass_on_behind = ("proj_in_0", "attn_0", "ffn_up_0", "ffn_down_0", "proj_in_1", "attn_1", "ffn_up_1")

    def used(point, result):
        return advance(result) if point in pass_on_behind else 0.0

    def weights(n, l, after):
        k = (n, l)
        if k not in gathered:
            if k not in forwarded:
                advance(after)
            send_b, recv_b, (land,), _ = forwarded[k]
            (gathered[k],) = _copy_wait(f"gather_{n}_{l}_done", [land], send_b, recv_b, _forward_plan(slot_of(k)),
                                        after)
        return gathered[k]

    pending = {}

    def hand_over(n, l, g):
        shard = shards[n].shape[1:]
        send, recv, bufs, tok = _copy_start(f"send_grad_{n}_{l}", [g, lax.empty((N_DEV,) + shard, g.dtype)],
                                            _scatter_plan(slot_of((n, l))), [len(ALL_PEERS) + 1])
        pending[(n, l)] = (send, recv, bufs)
        return tok[0, 0]

    def received(k, after):
        send, recv, bufs = pending[k]
        return _copy_wait(f"recv_grad_{k[0]}_{k[1]}", bufs, send, recv, _scatter_plan(slot_of(k)), after)[1]

    arrive(conv_key, token)
    cw_all = arrived[conv_key][1]
    nup = w_up.shape[2]
    cw_shards = cw_all.reshape(N_DEV, -1)[:, :DEPTH * 3 * nup].reshape(N_DEV, DEPTH, 3, nup)
    conv_w_full = cw_shards.transpose(1, 2, 0, 3).reshape(DEPTH, 3, N_DEV * nup)

    loss_local, dx, d_ln_final, sgrads = _local_step(
        x[0], loss_target[0], dict(small, ln_attn=ln_attn + token[0, 0]), weights, conv_w_full, hand_over, used)

    stacked = [jnp.stack([sgrads[l][n] for l in range(DEPTH)]) for n in SMALL_NAMES + ("conv_w",)] + [d_ln_final]
    shapes = [a.shape for a in stacked]
    mine = _pack(stacked)
    send_s, recv_s, bufs_s, _ = _copy_start("gather_small_grads_start", [mine, lax.empty((N_DEV,) + mine.shape, F32)],
                                            _gather_plan([ALL_PEERS], [_lead_slot]), [len(ALL_PEERS) + 1])

    big, after = {}, dx
    moments = dict(w_in=(m_w_in, v_w_in), w_out=(m_w_out, v_w_out), w_up=(m_w_up, v_w_up), w_down=(m_w_down, v_w_down))
    for n in reversed(names):
        parts = (received((n, 0), after), received((n, 1), after))
        big[n] = _adamw_sharded(f"adamw_{n}", shards[n], *moments[n], parts)
        after = big[n][1]

    _, everyone = _copy_wait("gather_small_grads_done", bufs_s, send_s, recv_s,
                             _gather_plan([ALL_PEERS], [_lead_slot]), after)
    g_small = _unpack(_sum_devices("sum_small_grads", everyone), shapes)
    g = dict(zip(SMALL_NAMES + ("conv_w", "ln_final"), g_small))
    g["conv_w"] = lax.dynamic_slice_in_dim(g["conv_w"], me * nup, nup, axis=2)

    snames = SMALL_NAMES + ("conv_w", "ln_final")
    sw = dict(small, conv_w=conv_w)
    sm = dict(ln_attn=m_ln_attn, sink_b=m_sink_b, rpb_c=m_rpb_c, mix_gain=m_mix_gain, ln_ffn=m_ln_ffn,
              conv_b=m_conv_b, conv_w=m_conv_w, ln_final=m_ln_final)
    sv = dict(ln_attn=v_ln_attn, sink_b=v_sink_b, rpb_c=v_rpb_c, mix_gain=v_mix_gain, ln_ffn=v_ln_ffn,
              conv_b=v_conv_b, conv_w=v_conv_w, ln_final=v_ln_final)
    s_delta, s_m, s_v = (dict(zip(snames, out)) for out in _adamw_small(
        "adamw_small", [sw[n] for n in snames], [g[n] for n in snames], [sm[n] for n in snames],
        [sv[n] for n in snames]))

    loss = lax.psum(loss_local, ("x", "y", "c"))
    outputs = ("ln_attn", "w_in", "sink_b", "rpb_c", "mix_gain", "w_out", "ln_ffn", "w_up", "conv_w", "conv_b",
               "w_down", "ln_final")
    grads = [big[n][0] if n in big else g[n] for n in outputs]
    deltas = [big[n][1] if n in big else s_delta[n] for n in outputs]
    new_m = [big[n][2] if n in big else s_m[n] for n in outputs]
    new_v = [big[n][3] if n in big else s_v[n] for n in outputs]
    return (loss, dx[None], *grads, *deltas, *new_m, *new_v)
```

```python
import functools

import jax
import jax.numpy as jnp
from jax import lax
from jax.experimental import pallas as pl
from jax.experimental.pallas import tpu as pltpu

F32 = jnp.float32
BF16 = jnp.bfloat16

N_DEV = 8
T = 2048
D = 2048
DEPTH = 2
HD = 64
HA, HB, HKV, HC = 12, 10, 2, 10
WA, WB, WKV, WC = HA * HD, HB * HD, HKV * HD, HC * HD
IN_COLS = 3 * WA + WB + 2 * WKV + 3 * WC
DFF = 5632
GRID_W = 64
ROWS = T // GRID_W
NA_ROWS, NA_COLS = 8, 16
WINDOW_B = 128
EPS = 1e-6
NEG = -1e30
ROPE_THETA = 10000.0
LANE = 128
VMEM_LIMIT = 56 * 1024 * 1024

ADAM_LR, ADAM_B1, ADAM_B2, ADAM_EPS, ADAM_WD, ADAM_STEP = 0.001, 0.9, 0.999, 1e-08, 0.01, 10

GROUPS = (("qa", WA, True, True), ("ka", WA, True, False), ("va", WA, False, False),
          ("qb", WB, True, True), ("kb", WKV, True, False), ("vb", WKV, False, False),
          ("qc", WC, False, True), ("kc", WC, False, False), ("vc", WC, False, False))


def _params(sem=None):
    return pltpu.CompilerParams(dimension_semantics=sem, vmem_limit_bytes=VMEM_LIMIT)


HBM_SPEC = pl.BlockSpec(memory_space=pltpu.HBM)
SEM_SPEC = pl.BlockSpec(memory_space=pltpu.SEMAPHORE)
DATAFLOW = pltpu.SideEffectType.DATAFLOW_SIDE_EFFECTING


ALL_PEERS = tuple((p >> 2 & 1, p >> 1 & 1, p & 1) for p in range(1, N_DEV))
OTHER_CHIPS = ((1, 0, 0), (0, 1, 0), (1, 1, 0))
NEAR_PEERS = ((0, 0, 1),) + OTHER_CHIPS


def _flip(x, y, c, f):
    return (1 - x if f[0] else x, 1 - y if f[1] else y, 1 - c if f[2] else c)


def _index(pos):
    return 4 * pos[0] + 2 * pos[1] + pos[2]


class _LocalCopy:
    def __init__(self, src, dst, sem):
        self.copy = pltpu.make_async_copy(src, dst, sem)

    def start(self):
        self.copy.start()

    def wait_send(self):
        self.copy.wait()

    def wait_recv(self):
        pass


def _descriptors(plan, bufs, send_sems, recv_sems):
    x, y, c = lax.axis_index("x"), lax.axis_index("y"), lax.axis_index("c")
    return [_LocalCopy(src, dst, send_sems[g].at[i]) if partner is None else
            pltpu.make_async_remote_copy(src_ref=src, dst_ref=dst, send_sem=send_sems[g].at[i],
                                         recv_sem=recv_sems[g].at[i], device_id=partner,
                                         device_id_type=pl.DeviceIdType.MESH)
            for g, copies in enumerate(plan(bufs, x, y, c)) for i, (src, dst, partner) in enumerate(copies)]


def _copy_start(name, bufs, plan, sizes):
    nb, ng = len(bufs), len(sizes)

    def body(*refs):
        for d in _descriptors(plan, refs[:nb], refs[nb:nb + ng], refs[nb + ng:nb + 2 * ng]):
            d.start()
        refs[2 * nb + 2 * ng][...] = jnp.zeros((8, LANE), F32)

    outs = pl.pallas_call(
        body, name=name,
        out_shape=[pltpu.SemaphoreType.DMA((s,)) for s in sizes] * 2 + [pltpu.HBM(b.shape, b.dtype) for b in bufs]
        + [jax.ShapeDtypeStruct((8, LANE), F32)],
        in_specs=[HBM_SPEC] * nb,
        out_specs=[SEM_SPEC] * (2 * ng) + [HBM_SPEC] * nb + [pl.BlockSpec(memory_space=pltpu.VMEM)],
        input_output_aliases={i: 2 * ng + i for i in range(nb)},
        compiler_params=pltpu.CompilerParams(has_side_effects=DATAFLOW),
    )(*[pltpu.with_memory_space_constraint(b, pltpu.HBM) for b in bufs])
    return outs[:ng], outs[ng:2 * ng], outs[2 * ng:2 * ng + nb], outs[2 * ng + nb]


def _copy_wait(name, bufs, send_sems, recv_sems, plan, after):
    nb, ng = len(bufs), len(send_sems)
    after = list(after) if isinstance(after, (list, tuple)) else [after]

    def body(*refs):
        for d in _descriptors(plan, refs[:nb], refs[nb:nb + ng], refs[nb + ng:nb + 2 * ng]):
            d.wait_send()
            d.wait_recv()

    return pl.pallas_call(
        body, name=name, out_shape=[pltpu.HBM(b.shape, b.dtype) for b in bufs],
        in_specs=[HBM_SPEC] * nb + [SEM_SPEC] * (2 * ng) + [pl.BlockSpec(memory_space=pl.ANY)] * len(after),
        out_specs=[HBM_SPEC] * nb, input_output_aliases={i: i for i in range(nb)},
        compiler_params=pltpu.CompilerParams(has_side_effects=DATAFLOW),
    )(*bufs, *send_sems, *recv_sems, *after)


def _lead_slot(ref, k):
    return ref.at[k]


def _col_slot(width):
    return lambda ref, k: ref.at[:, pl.ds(pl.multiple_of(k * width, LANE), width)]


def _gather_plan(peer_sets, slots):
    def plan(bufs, x, y, c):
        n = len(peer_sets)
        return [[(bufs[i], slots[i](bufs[n + i], _index((x, y, c))), _flip(x, y, c, f)) for f in peers]
                + [(bufs[i], slots[i](bufs[n + i], _index((x, y, c))), None)] for i, peers in enumerate(peer_sets)]
    return plan


def _forward_plan(slot):
    def plan(bufs, x, y, c):
        pieces = [slot(bufs[0], _index(_flip(x, y, c, f))) for f in OTHER_CHIPS]
        return [[(p, p, _flip(x, y, c, (0, 0, 1))) for p in pieces]]
    return plan


def _scatter_plan(slot):
    def plan(bufs, x, y, c):
        me = _index((x, y, c))
        peers = [_flip(x, y, c, f) for f in ALL_PEERS]
        return [[(slot(bufs[0], _index(p)), bufs[1].at[me], p) for p in peers]
                + [(slot(bufs[0], me), bufs[1].at[me], None)]]
    return plan


def _flat2(v):
    return v.reshape(-1, v.shape[-1])


def _matmul(name, kind, a, a_spec, b, b_spec, out_shape, out_spec, grid, res=None, res_spec=None, acc_shape=None):
    dims = {"nn": (((1,), (0,)), ((), ())), "nt": NT_DIMS, "nts": NT_DIMS, "tn": (((0,), (0,)), ((), ()))}[kind]
    nred = grid[-1]

    def body(*refs):
        if res is None:
            a_ref, b_ref, o_ref = refs[:3]
            r_ref = None
        else:
            a_ref, b_ref, r_ref, o_ref = refs[:4]
        if kind == "nts":
            n = b_ref.shape[-1]
            part = sum(lax.dot_general(a_ref[:, blk * n:(blk + 1) * n], b_ref[blk], dims, preferred_element_type=F32)
                       for blk in range(b_ref.shape[0]))
        else:
            part = lax.dot_general(_flat2(a_ref[...]), _flat2(b_ref[...]), dims, preferred_element_type=F32)

        def finish(total):
            if r_ref is not None:
                total = total + r_ref[...]
            o_ref[...] = total.reshape(o_ref.shape).astype(o_ref.dtype)

        if nred == 1:
            finish(part)
        else:
            acc_ref = refs[-1]
            k = pl.program_id(len(grid) - 1)

            @pl.when(k == 0)
            def _():
                acc_ref[...] = part

            @pl.when(jnp.logical_and(k > 0, k < nred - 1))
            def _():
                acc_ref[...] += part

            @pl.when(k == nred - 1)
            def _():
                finish(acc_ref[...] + part)

    ins, specs = [a, b], [a_spec, b_spec]
    if res is not None:
        ins.append(res)
        specs.append(res_spec)
    scratch = [] if nred == 1 else [pltpu.VMEM(acc_shape, F32)]
    return pl.pallas_call(
        body, name=name, grid=grid, in_specs=specs, out_specs=out_spec, out_shape=out_shape, scratch_shapes=scratch,
        compiler_params=_params(("parallel",) * (len(grid) - 1) + ("arbitrary",)),
    )(*ins)


def _nn_rows(name, a, wg, res, s, tn, tm):
    _, kj, n = wg.shape
    return _matmul(
        name, "nn", a, pl.BlockSpec((tm, s * kj), lambda j, i, r: (i, r)),
        wg, pl.BlockSpec((s, kj, tn), lambda j, i, r: (r, 0, j)),
        jax.ShapeDtypeStruct((T, n), F32), pl.BlockSpec((tm, tn), lambda j, i, r: (i, j)),
        (n // tn, T // tm, N_DEV // s), res=res, res_spec=pl.BlockSpec((tm, tn), lambda j, i, r: (i, j)),
        acc_shape=(tm, tn))


def _nt_cols(name, dc, dc_spec_of, w, nc):
    k, n = w.shape
    tm = tk = 1024
    return _matmul(
        name, "nt", dc, dc_spec_of(tm, nc),
        w, pl.BlockSpec((tk, nc), lambda kt, i, j: (kt, j)),
        jax.ShapeDtypeStruct((T, k), F32), pl.BlockSpec((tm, tk), lambda kt, i, j: (i, kt)),
        (k // tk, T // tm, n // nc), acc_shape=(tm, tk))


def _nt_rows(name, dc, wg, s, tm):
    _, kj, n = wg.shape
    return _matmul(
        name, "nt", dc, pl.BlockSpec((tm, n), lambda kt, i, r: (i, 0)),
        wg, pl.BlockSpec((s, kj, n), lambda kt, i, r: (kt, 0, 0)),
        jax.ShapeDtypeStruct((T, N_DEV * kj), F32), pl.BlockSpec((tm, s * kj), lambda kt, i, r: (i, kt)),
        (N_DEV // s, T // tm, 1))


def _tn_cols(name, a, dc, dc_spec_of, n, tn):
    k = a.shape[1]
    tk = 512
    return _matmul(
        name, "tn", a, pl.BlockSpec((T, tk), lambda j, kt, r: (0, kt)),
        dc, dc_spec_of(T, tn),
        jax.ShapeDtypeStruct((k, n), BF16), pl.BlockSpec((tk, tn), lambda j, kt, r: (kt, j)),
        (n // tn, k // tk, 1))


def _tn_rows(name, a, dc, kj, s, tn):
    n = dc.shape[1]
    return _matmul(
        name, "tn", a, pl.BlockSpec((T, s * kj), lambda kt, j, r: (0, kt)),
        dc, pl.BlockSpec((T, tn), lambda kt, j, r: (0, j)),
        jax.ShapeDtypeStruct((N_DEV, kj, n), BF16), pl.BlockSpec((s, kj, tn), lambda kt, j, r: (kt, 0, j)),
        (N_DEV // s, n // tn, 1))


TR = 256


def _rows(width):
    return pl.BlockSpec((TR, width), lambda i: (i, 0))


def _whole(shape):
    return pl.BlockSpec(shape, lambda i: (0,) * len(shape))


def _rmsnorm_rows(x_ref, g_ref):
    xv = x_ref[...]
    r = lax.rsqrt(jnp.mean(xv * xv, axis=-1, keepdims=True) + EPS)
    return ((xv * r) * g_ref[...]).astype(BF16)


def _prologue_matmul(name, prologue, ins, widths, w, w_block, w_index, tn, res=None, epilogue=None, extras=(),
                     out_dtype=F32):
    tm = 1024
    n = w.shape[-1]
    ni = len(ins)

    def body(*refs):
        w_ref = refs[ni]
        r_ref = refs[ni + 1] if res is not None else None
        x_refs = refs[ni + 1 + (res is not None):len(refs) - 3]
        h_ref, o_ref, h_scr = refs[-3:]

        @pl.when(pl.program_id(1) == 0)
        def _():
            h = prologue(*refs[:ni])
            h_scr[...] = h
            h_ref[...] = h

        total = jnp.dot(h_scr[...], _flat2(w_ref[...]), preferred_element_type=F32)
        if r_ref is not None:
            total = total + r_ref[...]
        if epilogue is not None:
            total = epilogue(pl.program_id(1), total, *x_refs)
        o_ref[...] = total.astype(out_dtype)

    tile = pl.BlockSpec((tm, tn), lambda i, j: (i, j))
    specs = [pl.BlockSpec((1, D), lambda i, j: (0, 0)) if wd is None else pl.BlockSpec((tm, wd), lambda i, j: (i, 0))
             for wd in widths]
    specs.append(pl.BlockSpec(w_block, lambda i, j: w_index(j)))
    operands = list(ins) + [w]
    if res is not None:
        specs.append(tile)
        operands.append(res)
    specs += [pl.BlockSpec((tm, LANE), lambda i, j: (i, 0))] * len(extras)
    operands += list(extras)
    return pl.pallas_call(
        body, name=name, grid=(T // tm, n // tn), in_specs=specs,
        out_specs=[pl.BlockSpec((tm, D), lambda i, j: (i, 0)), tile],
        out_shape=[jax.ShapeDtypeStruct((T, D), BF16), jax.ShapeDtypeStruct((T, n), out_dtype)],
        scratch_shapes=[pltpu.VMEM((tm, D), BF16)], compiler_params=_params(("parallel", "arbitrary")),
    )(*operands)


def _rms_bwd_math(dy, xv, g):
    r = lax.rsqrt(jnp.mean(xv * xv, axis=-1, keepdims=True) + EPS)
    xhat = xv * r
    dxhat = dy * g
    dx = r * (dxhat - xhat * jnp.mean(dxhat * xhat, axis=-1, keepdims=True))
    return dx, dy * xhat


def _accumulate(ref, val):
    @pl.when(pl.program_id(0) == 0)
    def _():
        ref[...] = val

    @pl.when(pl.program_id(0) > 0)
    def _():
        ref[...] += val


def _rmsnorm_bwd(name, dy, x, g, res):
    def body(dy_ref, x_ref, g_ref, res_ref, dx_ref, dxb_ref, dg_ref):
        dx, dgr = _rms_bwd_math(dy_ref[...], x_ref[...], g_ref[...])
        tot = res_ref[...] + dx
        dx_ref[...] = tot
        dxb_ref[...] = tot.astype(BF16)
        _accumulate(dg_ref, jnp.sum(dgr, axis=0, keepdims=True))

    return pl.pallas_call(
        body, name=name, grid=(T // TR,), in_specs=[_rows(D), _rows(D), _whole((1, D)), _rows(D)],
        out_specs=[_rows(D), _rows(D), _whole((1, D))],
        out_shape=[jax.ShapeDtypeStruct((T, D), F32), jax.ShapeDtypeStruct((T, D), BF16),
                   jax.ShapeDtypeStruct((1, D), F32)],
        compiler_params=_params(("arbitrary",)),
    )(dy, x, g, res)


def _loss_head(x, g, target):
    def body(x_ref, g_ref, t_ref, loss_ref, dx_ref, dxb_ref, dg_ref):
        xv, gv = x_ref[...], g_ref[...]
        r = lax.rsqrt(jnp.mean(xv * xv, axis=-1, keepdims=True) + EPS)
        err = (xv * r) * gv - t_ref[...]
        part = 0.5 * jnp.sum(jnp.mean(err * err, axis=-1, keepdims=True))
        dx, dgr = _rms_bwd_math(err * (1.0 / D), xv, gv)
        dx_ref[...] = dx
        dxb_ref[...] = dx.astype(BF16)
        _accumulate(dg_ref, jnp.sum(dgr, axis=0, keepdims=True))
        _accumulate(loss_ref, jnp.full((8, LANE), part, F32))

    return pl.pallas_call(
        body, name="loss_head", grid=(T // TR,), in_specs=[_rows(D), _whole((1, D)), _rows(D)],
        out_specs=[_whole((8, LANE)), _rows(D), _rows(D), _whole((1, D))],
        out_shape=[jax.ShapeDtypeStruct((8, LANE), F32), jax.ShapeDtypeStruct((T, D), F32),
                   jax.ShapeDtypeStruct((T, D), BF16), jax.ShapeDtypeStruct((1, D), F32)],
        compiler_params=_params(("arbitrary",)),
    )(x, g, target)


MIX_OFFS = ((0, WA), (WA, WB), (WA + WB, WC))


def _mix_rows(oa_ref, ob_ref, oc_ref, g_ref):
    parts = []
    for ref, (off, w) in zip((oa_ref, ob_ref, oc_ref), MIX_OFFS):
        o = ref[...]
        r = lax.rsqrt(jnp.mean(o * o, axis=-1, keepdims=True) + EPS)
        parts.append(((o * r) * g_ref[:, off:off + w]).astype(BF16))
    return jnp.concatenate(parts, axis=1)


def _mix_bwd(name, dmixed, oa, ob, oc, gain):
    def body(dm_ref, oa_ref, ob_ref, oc_ref, g_ref, doa_ref, dob_ref, doc_ref, dg_ref):
        dgs = []
        for ref, dref, (off, w) in zip((oa_ref, ob_ref, oc_ref), (doa_ref, dob_ref, doc_ref), MIX_OFFS):
            dx, dgr = _rms_bwd_math(dm_ref[:, off:off + w], ref[...], g_ref[:, off:off + w])
            dref[...] = dx
            dgs.append(jnp.sum(dgr, axis=0, keepdims=True))
        _accumulate(dg_ref, jnp.concatenate(dgs, axis=1))

    return pl.pallas_call(
        body, name=name, grid=(T // TR,),
        in_specs=[_rows(D), _rows(WA), _rows(WB), _rows(WC), _whole((1, D))],
        out_specs=[_rows(WA), _rows(WB), _rows(WC), _whole((1, D))],
        out_shape=[jax.ShapeDtypeStruct((T, WA), F32), jax.ShapeDtypeStruct((T, WB), F32),
                   jax.ShapeDtypeStruct((T, WC), F32), jax.ShapeDtypeStruct((1, D), F32)],
        compiler_params=_params(("arbitrary",)),
    )(dmixed, oa, ob, oc, gain)


def _rope_tables():
    inv_freq = ROPE_THETA ** (-jnp.arange(0, HD, 2, dtype=F32) / HD)
    ang = jnp.arange(T, dtype=F32)[:, None] * inv_freq[None, :]
    cos, sin = jnp.cos(ang), jnp.sin(ang)
    cos2 = jnp.tile(jnp.concatenate([cos, cos], axis=1), (1, LANE // HD))
    sin2 = jnp.tile(jnp.concatenate([-sin, sin], axis=1), (1, LANE // HD))
    return cos2, sin2


def _rot_half(v):
    lane = lax.broadcasted_iota(jnp.int32, v.shape, 1)
    return jnp.where(lane % HD < HD // 2, pltpu.roll(v, LANE - HD // 2, 1), pltpu.roll(v, HD // 2, 1))


BLOCK_KINDS = tuple((rot, is_q) for _, w, rot, is_q in GROUPS for _ in range(w // LANE))
BLOCK_OF = {name: sum(w for _, w, _, _ in GROUPS[:g]) // LANE for g, (name, _, _, _) in enumerate(GROUPS)}


def _any_tile(j, tiles):
    return functools.reduce(jnp.logical_or, [j == t for t in tiles]) if tiles else False


def _rope_epilogue(j, tile, c_ref, s_ref):
    cv, sv = c_ref[...], s_ref[...]
    per, n_tiles = tile.shape[1] // LANE, IN_COLS // tile.shape[1]
    out = []
    for b in range(per):
        v = tile[:, b * LANE:(b + 1) * LANE]
        rot = _any_tile(j, [t for t in range(n_tiles) if BLOCK_KINDS[t * per + b][0]])
        is_q = _any_tile(j, [t for t in range(n_tiles) if BLOCK_KINDS[t * per + b][1]])
        if rot is not False:
            v = jnp.where(rot, v * cv + _rot_half(v) * sv, v)
        if is_q is not False:
            v = v * jnp.where(is_q, HD ** -0.5, 1.0)
        out.append(v)
    return jnp.concatenate(out, axis=1)


def _rope_bwd(name, grads, cos2, sin2):
    def body(*refs):
        ins, (c_ref, s_ref, o_ref) = refs[:9], refs[9:]
        cv, sv = c_ref[...], s_ref[...]
        off = 0
        for d_ref, (_, w, rot, is_q) in zip(ins, GROUPS):
            for b in range(w // LANE):
                v = d_ref[:, b * LANE:(b + 1) * LANE]
                if is_q:
                    v = v * (HD ** -0.5)
                if rot:
                    v = v * cv + _rot_half(v * sv)
                o_ref[:, off + b * LANE:off + (b + 1) * LANE] = v.astype(BF16)
            off += w

    return pl.pallas_call(
        body, name=name, grid=(T // TR,), in_specs=[_rows(w) for _, w, _, _ in GROUPS] + [_rows(LANE), _rows(LANE)],
        out_specs=_rows(IN_COLS), out_shape=jax.ShapeDtypeStruct((T, IN_COLS), BF16),
        compiler_params=_params(("parallel",)),
    )(*grads, cos2, sin2)


NT_DIMS = (((1,), (1,)), ((), ()))
TN_DIMS = (((0,), (0,)), ((), ()))


def _scores(q, k, bias, valid):
    s = lax.dot_general(q, k, NT_DIMS, preferred_element_type=F32)
    if bias is not None:
        s = s + bias
    if valid is not None:
        s = jnp.where(valid, s, NEG)
    return s


def _heads_fwd(heads):
    scores = [_scores(h["q"], h["k"], h.get("bias"), h.get("valid")) for h in heads]
    soft = []
    for s, h in zip(scores, heads):
        m = jnp.max(s, axis=1, keepdims=True)
        e = jnp.exp(s - m)
        l = jnp.sum(e, axis=1, keepdims=True)
        if h.get("sink") is not None:
            l = l + jnp.exp(h["sink"] - m)
        soft.append((e.astype(BF16), l, m + jnp.log(l)))
    return [(jnp.dot(e, h["v"], preferred_element_type=F32) / l, lse) for (e, l, lse), h in zip(soft, heads)]


def _heads_bwd(heads):
    dobs = [h["do"].astype(BF16) for h in heads]
    scores = [_scores(h["q"], h["k"], h.get("bias"), h.get("valid")) for h in heads]
    dps = [lax.dot_general(dob, h["v"], NT_DIMS, preferred_element_type=F32) for dob, h in zip(dobs, heads)]
    mid = []
    for s, dp, h in zip(scores, dps, heads):
        p = jnp.exp(s - h["lse"])
        delta = jnp.sum(h["do"] * h["o"], axis=1, keepdims=True)
        ds = p * (dp - delta)
        dsink = None if h.get("sink") is None else -jnp.exp(h["sink"] - h["lse"]) * delta
        mid.append((p.astype(BF16), ds, dsink))
    out = []
    for (pb, ds, dsink), dob, h in zip(mid, dobs, heads):
        dsb = ds.astype(BF16)
        out.append((jnp.dot(dsb, h["k"], preferred_element_type=F32),
                    lax.dot_general(dsb, h["q"], TN_DIMS, preferred_element_type=F32),
                    lax.dot_general(pb, dob, TN_DIMS, preferred_element_type=F32), ds, dsink))
    return out


def _per_head(cols):
    return jnp.concatenate([jnp.broadcast_to(c, (c.shape[0], HD)) for c in cols], axis=1)


DILATIONS = ((128, 1), (512, 4), (2048, 16))


BQ_A = 256
REACH_A = max(window // 2 for window, _ in DILATIONS)


def _first_key(i):
    return jnp.maximum(i * BQ_A - REACH_A, 0)


def _key_window_groups():
    groups = {}
    for i in range(T // BQ_A):
        width = min(T, (i + 1) * BQ_A + REACH_A) - max(i * BQ_A - REACH_A, 0)
        groups.setdefault(width, []).append(i)
    return groups


def _per_window(i, fn):
    for width, tiles in _key_window_groups().items():
        hit = functools.reduce(jnp.logical_or, [i == t for t in tiles])
        pl.when(hit)(functools.partial(fn, pl.multiple_of(_first_key(i), BQ_A), width))


def _dilation_bias():
    def body(o_ref):
        i = pl.program_id(0)
        t = i * BQ_A + lax.broadcasted_iota(jnp.int32, (BQ_A, T), 0)
        ad = jnp.abs(t - (_first_key(i) + lax.broadcasted_iota(jnp.int32, (BQ_A, T), 1)))
        count = jnp.zeros((BQ_A, T), jnp.int32)
        for window, r in DILATIONS:
            count += jnp.where(((ad & (r - 1)) == 0) & (ad <= window // 2), 1, 0)
        logs = jnp.where(count == 2, jnp.log(2.0), jnp.where(count == 3, jnp.log(3.0), 0.0)).astype(F32)
        o_ref[...] = jnp.where(count == 0, NEG, logs)

    return pl.pallas_call(
        body, name="dilation_bias", grid=(T // BQ_A,), out_specs=pl.BlockSpec((BQ_A, T), lambda i: (i, 0)),
        out_shape=jax.ShapeDtypeStruct((T, T), F32), compiler_params=_params(("parallel",)),
    )()


def _qkv_rows(rows, group):
    return pl.BlockSpec((rows, LANE), lambda p, i: (i, BLOCK_OF[group] + p))


def _qkv_all(group):
    return pl.BlockSpec((T, LANE), lambda p, i: (0, BLOCK_OF[group] + p))


def _attn_a_fwd(name, qkv, bias):
    def body(q_ref, k_ref, v_ref, b_ref, o_ref, lse_ref):
        def tile(first, width):
            b = b_ref[:, :width]
            outs = _heads_fwd([dict(q=q_ref[:, h * HD:(h + 1) * HD], k=k_ref[pl.ds(first, width), h * HD:(h + 1) * HD],
                                    v=v_ref[pl.ds(first, width), h * HD:(h + 1) * HD], bias=b) for h in range(2)])
            o_ref[...] = jnp.concatenate([o for o, _ in outs], axis=1)
            lse_ref[...] = _per_head([lse for _, lse in outs])

        _per_window(pl.program_id(1), tile)

    qs = pl.BlockSpec((BQ_A, LANE), lambda p, i: (i, p))
    ks = pl.BlockSpec((T, LANE), lambda p, i: (0, p))
    return pl.pallas_call(
        body, name=name, grid=(HA // 2, T // BQ_A),
        in_specs=[_qkv_rows(BQ_A, "qa"), _qkv_all("ka"), _qkv_all("va"), pl.BlockSpec((BQ_A, T), lambda p, i: (i, 0))],
        out_specs=[qs, qs],
        out_shape=[jax.ShapeDtypeStruct((T, WA), F32)] * 2, compiler_params=_params(("parallel", "parallel")),
    )(qkv, qkv, qkv, bias)


def _attn_a_bwd(name, qkv, oa, lse, doa, bias):
    def body(q_ref, k_ref, v_ref, o_ref, lse_ref, do_ref, b_ref, dq_ref, dk_ref, dv_ref):
        @pl.when(pl.program_id(1) == 0)
        def _():
            dk_ref[...] = jnp.zeros_like(dk_ref)
            dv_ref[...] = jnp.zeros_like(dv_ref)

        def tile(first, width):
            b = b_ref[:, :width]
            keys = pl.ds(first, width)
            sls = [slice(h * HD, (h + 1) * HD) for h in range(2)]
            res = _heads_bwd([dict(q=q_ref[:, sl], k=k_ref[keys, sl], v=v_ref[keys, sl], o=o_ref[:, sl],
                                   do=do_ref[:, sl], lse=lse_ref[:, sl.start:sl.start + 1], bias=b) for sl in sls])
            dq_ref[...] = jnp.concatenate([r[0] for r in res], axis=1)
            dk_ref[keys, :] += jnp.concatenate([r[1] for r in res], axis=1)
            dv_ref[keys, :] += jnp.concatenate([r[2] for r in res], axis=1)

        _per_window(pl.program_id(1), tile)

    qs = pl.BlockSpec((BQ_A, LANE), lambda p, i: (i, p))
    ks = pl.BlockSpec((T, LANE), lambda p, i: (0, p))
    return pl.pallas_call(
        body, name=name, grid=(HA // 2, T // BQ_A),
        in_specs=[_qkv_rows(BQ_A, "qa"), _qkv_all("ka"), _qkv_all("va"), qs, qs, qs,
                  pl.BlockSpec((BQ_A, T), lambda p, i: (i, 0))], out_specs=[qs, ks, ks],
        out_shape=[jax.ShapeDtypeStruct((T, WA), F32)] * 3, compiler_params=_params(("parallel", "arbitrary")),
    )(qkv, qkv, qkv, oa, lse, doa, bias)


BQ_B = 128
SPAN_B = BQ_B + 2 * WINDOW_B


def _window_b(i):
    start = pl.multiple_of(jnp.clip(i * BQ_B - WINDOW_B, 0, T - SPAN_B), BQ_B)
    qpos = i * BQ_B + lax.broadcasted_iota(jnp.int32, (BQ_B, SPAN_B), 0)
    kpos = start + lax.broadcasted_iota(jnp.int32, (BQ_B, SPAN_B), 1)
    return start, jnp.abs(qpos - kpos) <= WINDOW_B


GROUP_B = HB // HKV


def _stack_group(ref, g):
    return jnp.concatenate([ref[:, h * HD:(h + 1) * HD] for h in range(g * GROUP_B, (g + 1) * GROUP_B)], axis=0)


def _sink_column(sink_ref, g):
    return jnp.concatenate([jnp.full((BQ_B, 1), sink_ref[h], F32) for h in range(g * GROUP_B, (g + 1) * GROUP_B)],
                           axis=0)


def _unstack(stacked):
    return [s[j * BQ_B:(j + 1) * BQ_B] for s in stacked for j in range(GROUP_B)]


def _attn_b_fwd(name, qb, kb, vb, sink):
    def body(sink_ref, q_ref, k_ref, v_ref, o_ref, lse_ref):
        start, valid = _window_b(pl.program_id(0))
        valid = jnp.concatenate([valid] * GROUP_B, axis=0)
        kw, vw = k_ref[pl.ds(start, SPAN_B), :], v_ref[pl.ds(start, SPAN_B), :]
        outs = _heads_fwd([dict(q=_stack_group(q_ref, g), k=kw[:, g * HD:(g + 1) * HD], v=vw[:, g * HD:(g + 1) * HD],
                                valid=valid, sink=_sink_column(sink_ref, g)) for g in range(HKV)])
        o_ref[...] = jnp.concatenate(_unstack([o for o, _ in outs]), axis=1)
        lse_ref[...] = _per_head(_unstack([lse for _, lse in outs]))

    qs = pl.BlockSpec((BQ_B, WB), lambda i: (i, 0))
    return pl.pallas_call(
        body, name=name, grid=(T // BQ_B,),
        in_specs=[pl.BlockSpec(memory_space=pltpu.SMEM), qs, _whole((T, WKV)), _whole((T, WKV))],
        out_specs=[qs, qs],
        out_shape=[jax.ShapeDtypeStruct((T, WB), F32)] * 2, compiler_params=_params(("parallel",)),
    )(sink, qb, kb, vb)


def _attn_b_bwd(name, qb, kb, vb, ob, lse, dob, sink):
    def body(sink_ref, q_ref, k_ref, v_ref, o_ref, lse_ref, do_ref, dq_ref, dk_ref, dv_ref, dsink_ref):
        i = pl.program_id(0)
        start, valid = _window_b(i)
        valid = jnp.concatenate([valid] * GROUP_B, axis=0)
        kw, vw = k_ref[pl.ds(start, SPAN_B), :], v_ref[pl.ds(start, SPAN_B), :]
        res = _heads_bwd([dict(q=_stack_group(q_ref, g), k=kw[:, g * HD:(g + 1) * HD], v=vw[:, g * HD:(g + 1) * HD],
                               o=_stack_group(o_ref, g), do=_stack_group(do_ref, g),
                               lse=jnp.concatenate([lse_ref[:, h * HD:h * HD + 1]
                                                    for h in range(g * GROUP_B, (g + 1) * GROUP_B)], axis=0),
                               valid=valid, sink=_sink_column(sink_ref, g)) for g in range(HKV)])
        dks, dvs = [r[1] for r in res], [r[2] for r in res]
        lane = lax.broadcasted_iota(jnp.int32, (1, LANE), 1)
        dsink = jnp.zeros((1, LANE), F32)
        for h, rows in enumerate(_unstack([r[4] for r in res])):
            dsink += jnp.where(lane == h, jnp.sum(rows), 0.0)
        dq_ref[...] = jnp.concatenate(_unstack([r[0] for r in res]), axis=1)

        @pl.when(i == 0)
        def _():
            dk_ref[...] = jnp.zeros_like(dk_ref)
            dv_ref[...] = jnp.zeros_like(dv_ref)
            dsink_ref[...] = jnp.zeros_like(dsink_ref)

        dk_ref[pl.ds(start, SPAN_B), :] += jnp.concatenate(dks, axis=1)
        dv_ref[pl.ds(start, SPAN_B), :] += jnp.concatenate(dvs, axis=1)
        dsink_ref[...] += dsink

    qs = pl.BlockSpec((BQ_B, WB), lambda i: (i, 0))
    return pl.pallas_call(
        body, name=name, grid=(T // BQ_B,),
        in_specs=[pl.BlockSpec(memory_space=pltpu.SMEM), qs, _whole((T, WKV)), _whole((T, WKV)), qs, qs, qs],
        out_specs=[qs, _whole((T, WKV)), _whole((T, WKV)), _whole((1, LANE))],
        out_shape=[jax.ShapeDtypeStruct((T, WB), F32), jax.ShapeDtypeStruct((T, WKV), F32),
                   jax.ShapeDtypeStruct((T, WKV), F32), jax.ShapeDtypeStruct((1, LANE), F32)],
        compiler_params=_params(("arbitrary",)),
    )(sink, qb, kb, vb, ob, lse, dob)


SPAN_C = NA_ROWS * GRID_W


def _row_start(r):
    return jnp.clip(r - NA_ROWS // 2, 0, ROWS - NA_ROWS)


def _off_index(r):
    return _row_start(r) - r + (NA_ROWS - 1)


N_TAB = 16
RPS = 4


def _rpb_tables(name, rpb):
    circ = jnp.concatenate([rpb[..., NA_COLS - 1:], jnp.zeros(rpb.shape[:2] + (LANE - (2 * NA_COLS - 1),), F32),
                            rpb[..., :NA_COLS - 1]], axis=-1)
    circ = jnp.pad(circ, ((0, 0), (0, N_TAB + 1 - circ.shape[1]), (0, 0)))

    def body(w_ref, o_ref):
        c = lax.broadcasted_iota(jnp.int32, (GRID_W, LANE), 0)
        lane = lax.broadcasted_iota(jnp.int32, (GRID_W, LANE), 1)
        cs = jnp.clip(c - NA_COLS // 2, 0, GRID_W - NA_COLS)
        valid = (lane % GRID_W >= cs) & (lane % GRID_W < cs + NA_COLS)
        toep = [pltpu.roll(jnp.broadcast_to(w_ref[a:a + 1, :], (GRID_W, LANE)), 0, 1, stride=1, stride_axis=0)
                for a in range(N_TAB + 1)]
        for a in range(N_TAB):
            pair = jnp.where(lane < GRID_W, toep[a], pltpu.roll(toep[a + 1], GRID_W, 1))
            o_ref[a] = jnp.where(valid, pair, NEG)

    return pl.pallas_call(
        body, name=name, grid=(HC,),
        in_specs=[pl.BlockSpec((None, N_TAB + 1, LANE), lambda h: (h, 0, 0))],
        out_specs=pl.BlockSpec((None, N_TAB, GRID_W, LANE), lambda h: (h, 0, 0, 0)),
        out_shape=jax.ShapeDtypeStruct((HC, N_TAB, GRID_W, LANE), F32), compiler_params=_params(("parallel",)),
    )(circ)


def _bias_c(t_ref, h, d):
    return jnp.concatenate([t_ref[h, d + k] for k in range(0, NA_ROWS, 2)], axis=1)


def _attn_c_fwd(name, qkv, tables):
    def body(q_ref, k_ref, v_ref, t_ref, o_ref, lse_ref):
        heads = []
        for rr in range(RPS):
            r = pl.program_id(1) * RPS + rr
            rows = slice(rr * GRID_W, (rr + 1) * GRID_W)
            start = pl.multiple_of(_row_start(r) * GRID_W, GRID_W)
            kw, vw = k_ref[pl.ds(start, SPAN_C), :], v_ref[pl.ds(start, SPAN_C), :]
            heads += [dict(q=q_ref[rows, h * HD:(h + 1) * HD], k=kw[:, h * HD:(h + 1) * HD], v=vw[:, h * HD:(h + 1) * HD],
                           bias=_bias_c(t_ref, h, _off_index(r))) for h in range(2)]
        outs = _heads_fwd(heads)
        for rr in range(RPS):
            rows = slice(rr * GRID_W, (rr + 1) * GRID_W)
            o_ref[rows, :] = jnp.concatenate([o for o, _ in outs[2 * rr:2 * rr + 2]], axis=1)
            lse_ref[rows, :] = _per_head([lse for _, lse in outs[2 * rr:2 * rr + 2]])

    qs = pl.BlockSpec((RPS * GRID_W, LANE), lambda p, r: (r, p))
    ks = pl.BlockSpec((T, LANE), lambda p, r: (0, p))
    ts = pl.BlockSpec((2, N_TAB, GRID_W, LANE), lambda p, r: (p, 0, 0, 0))
    return pl.pallas_call(
        body, name=name, grid=(HC // 2, ROWS // RPS),
        in_specs=[_qkv_rows(RPS * GRID_W, "qc"), _qkv_all("kc"), _qkv_all("vc"), ts], out_specs=[qs, qs],
        out_shape=[jax.ShapeDtypeStruct((T, WC), F32)] * 2, compiler_params=_params(("parallel", "parallel")),
    )(qkv, qkv, qkv, tables)


def _attn_c_bwd(name, qkv, oc, lse, doc, tables):
    def body(q_ref, k_ref, v_ref, o_ref, lse_ref, do_ref, t_ref, dq_ref, dk_ref, dv_ref, dt_ref):
        @pl.when(pl.program_id(1) == 0)
        def _():
            dk_ref[...] = jnp.zeros_like(dk_ref)
            dv_ref[...] = jnp.zeros_like(dv_ref)
            dt_ref[...] = jnp.zeros_like(dt_ref)

        heads, where = [], []
        for rr in range(RPS):
            r = pl.program_id(1) * RPS + rr
            rows = slice(rr * GRID_W, (rr + 1) * GRID_W)
            d = _off_index(r)
            start = pl.multiple_of(_row_start(r) * GRID_W, GRID_W)
            kw, vw = k_ref[pl.ds(start, SPAN_C), :], v_ref[pl.ds(start, SPAN_C), :]
            where.append((rows, d, start))
            for h in range(2):
                sl = slice(h * HD, (h + 1) * HD)
                heads.append(dict(q=q_ref[rows, sl], k=kw[:, sl], v=vw[:, sl], o=o_ref[rows, sl], do=do_ref[rows, sl],
                                  lse=lse_ref[rows, h * HD:h * HD + 1], bias=_bias_c(t_ref, h, d)))
        res = _heads_bwd(heads)
        for rr, (rows, d, start) in enumerate(where):
            pair = res[2 * rr:2 * rr + 2]
            for h in range(2):
                for k in range(0, NA_ROWS, 2):
                    dt_ref[h, d + k] += pair[h][3][:, k * GRID_W:(k + 2) * GRID_W]
            dq_ref[rows, :] = jnp.concatenate([p[0] for p in pair], axis=1)
            dk_ref[pl.ds(start, SPAN_C), :] += jnp.concatenate([p[1] for p in pair], axis=1)
            dv_ref[pl.ds(start, SPAN_C), :] += jnp.concatenate([p[2] for p in pair], axis=1)

    qs = pl.BlockSpec((RPS * GRID_W, LANE), lambda p, r: (r, p))
    ks = pl.BlockSpec((T, LANE), lambda p, r: (0, p))
    ts = pl.BlockSpec((2, N_TAB, GRID_W, LANE), lambda p, r: (p, 0, 0, 0))
    return pl.pallas_call(
        body, name=name, grid=(HC // 2, ROWS // RPS),
        in_specs=[_qkv_rows(RPS * GRID_W, "qc"), _qkv_all("kc"), _qkv_all("vc"), qs, qs, qs, ts],
        out_specs=[qs, ks, ks, ts],
        out_shape=[jax.ShapeDtypeStruct((T, WC), F32)] * 3 + [jax.ShapeDtypeStruct((HC, N_TAB, GRID_W, LANE), F32)],
        compiler_params=_params(("parallel", "arbitrary")),
    )(qkv, qkv, qkv, oc, lse, doc, tables)


def _split3(v):
    hi = v.astype(BF16)
    r1 = v - hi.astype(F32)
    mid = r1.astype(BF16)
    lo = (r1 - mid.astype(F32)).astype(BF16)
    return hi, mid, lo


def _rpb_reduce(name, dtables):
    x = dtables.reshape(HC, N_TAB, GRID_W * LANE)
    c = jnp.arange(GRID_W)[:, None]
    lane = jnp.arange(LANE)[None, :]
    col = (lane // GRID_W) * LANE + jnp.clip(lane % GRID_W - c + (NA_COLS - 1), 0, 2 * NA_COLS - 2)
    col_onehot = (col.reshape(-1)[:, None] == jnp.arange(2 * LANE)[None, :]).astype(BF16)
    a2 = jnp.arange(N_TAB)[None, :]
    row_onehot = jnp.concatenate([(jnp.arange(16)[:, None] == a2 + u) & (a2 < 2 * NA_ROWS - 2) for u in range(2)],
                                 axis=1).astype(BF16)

    def body(x_ref, e_ref, f_ref, o_ref):
        y = sum(jnp.dot(part, e_ref[...], preferred_element_type=F32) for part in _split3(x_ref[...]))
        z = jnp.concatenate([y[:, :LANE], y[:, LANE:]], axis=0)
        o_ref[...] = sum(jnp.dot(f_ref[...], part, preferred_element_type=F32) for part in _split3(z))

    out = pl.pallas_call(
        body, name=name, grid=(HC,),
        in_specs=[pl.BlockSpec((None, N_TAB, GRID_W * LANE), lambda h: (h, 0, 0)),
                  _whole((GRID_W * LANE, 2 * LANE)), _whole((16, 2 * N_TAB))],
        out_specs=pl.BlockSpec((None, 16, LANE), lambda h: (h, 0, 0)),
        out_shape=jax.ShapeDtypeStruct((HC, 16, LANE), F32), compiler_params=_params(("parallel",)),
    )(x, col_onehot, row_onehot)
    return out[:, :2 * NA_ROWS - 1, :2 * NA_COLS - 1]


TC = 128
NCB = DFF // TC
CHUNK = 128
MARGIN = 8


def _shift_down(v, rows):
    return jnp.where(rows == 0, 0.0, pltpu.roll(v, 1, 0))


def _shift_up(v, rows):
    return jnp.where(rows == T - 1, 0.0, pltpu.roll(v, T - 1, 0))


def _conv(v, w, b, rows):
    return _shift_down(v, rows) * w[0:1] + v * w[1:2] + _shift_up(v, rows) * w[2:3] + b


def _ffn_specs():
    gate = lambda shape: pl.BlockSpec(shape, lambda j: (0, j))
    val = lambda shape: pl.BlockSpec(shape, lambda j: (0, j + NCB))
    return [gate((T, TC)), val((T, TC)), gate((3, TC)), val((3, TC)), gate((1, TC)), val((1, TC))]


def _ffn_mid_fwd(name, up, conv_w, conv_b):
    def body(xg_ref, xv_ref, wg_ref, wv_ref, bg_ref, bv_ref, o_ref):
        rows = lax.broadcasted_iota(jnp.int32, (T, TC), 0)
        ug = _conv(xg_ref[...], wg_ref[...], bg_ref[...], rows)
        uv = _conv(xv_ref[...], wv_ref[...], bv_ref[...], rows)
        o_ref[...] = (ug * jax.nn.sigmoid(ug) * uv).astype(BF16)

    return pl.pallas_call(
        body, name=name, grid=(NCB,), in_specs=_ffn_specs(), out_specs=pl.BlockSpec((T, TC), lambda j: (0, j)),
        out_shape=jax.ShapeDtypeStruct((T, DFF), BF16), compiler_params=_params(("parallel",)),
    )(up, up, conv_w, conv_w, conv_b, conv_b)


def _ffn_mid_bwd(name, dact, up, conv_w, conv_b):
    window = CHUNK + 2 * MARGIN
    centre = slice(MARGIN, MARGIN + CHUNK)

    def shifted(v):
        return pltpu.roll(v, 1, 0), pltpu.roll(v, window - 1, 0)

    def fold(v):
        return jnp.sum(v[centre].reshape(CHUNK // 8, 8, TC), axis=0)

    def body(da_ref, xg_ref, xv_ref, wg_ref, wv_ref, bg_ref, bv_ref, dx_ref, dw_ref, db_ref, dap, xgp, xvp):
        for src, pad in ((da_ref, dap), (xg_ref, xgp), (xv_ref, xvp)):
            pad[0:MARGIN, :] = jnp.zeros((MARGIN, TC), F32)
            pad[MARGIN:MARGIN + T, :] = src[...]
            pad[MARGIN + T:, :] = jnp.zeros((MARGIN, TC), F32)
        wg, wv, bg, bv = wg_ref[...], wv_ref[...], bg_ref[...], bv_ref[...]

        def chunk(c, sums):
            r0 = pl.multiple_of(c * CHUNK, CHUNK)
            da, xg, xv = dap[pl.ds(r0, window), :], xgp[pl.ds(r0, window), :], xvp[pl.ds(r0, window), :]
            xg_prev, xg_next = shifted(xg)
            xv_prev, xv_next = shifted(xv)
            ug = xg_prev * wg[0:1] + xg * wg[1:2] + xg_next * wg[2:3] + bg
            uv = xv_prev * wv[0:1] + xv * wv[1:2] + xv_next * wv[2:3] + bv
            sg = jax.nn.sigmoid(ug)
            dug = da * uv * (sg * (1.0 + ug * (1.0 - sg)))
            duv = da * (ug * sg)
            out = []
            for half, (x_prev, x, x_next, w, du) in enumerate(((xg_prev, xg, xg_next, wg, dug),
                                                               (xv_prev, xv, xv_next, wv, duv))):
                du_prev, du_next = shifted(du)
                dx = du_next * w[0:1] + du * w[1:2] + du_prev * w[2:3]
                dx_ref[half, pl.ds(r0, CHUNK), :] = dx[centre].astype(BF16)
                out += [fold(x_prev * du), fold(x * du), fold(x_next * du), fold(du)]
            return tuple(s + o for s, o in zip(sums, out))

        sums = lax.fori_loop(0, T // CHUNK, chunk, tuple(jnp.zeros((8, TC), F32) for _ in range(8)))
        rows = [jnp.sum(s, axis=0, keepdims=True) for s in sums]
        for half in range(2):
            dw_ref[half] = jnp.concatenate(rows[4 * half:4 * half + 3], axis=0)
            db_ref[half] = rows[4 * half + 3]

    return pl.pallas_call(
        body, name=name, grid=(NCB,), in_specs=[pl.BlockSpec((T, TC), lambda j: (0, j))] + _ffn_specs(),
        out_specs=[pl.BlockSpec((2, T, TC), lambda j: (0, 0, j)), pl.BlockSpec((2, 3, TC), lambda j: (0, 0, j)),
                   pl.BlockSpec((2, 1, TC), lambda j: (0, 0, j))],
        out_shape=[jax.ShapeDtypeStruct((2, T, DFF), BF16), jax.ShapeDtypeStruct((2, 3, DFF), F32),
                   jax.ShapeDtypeStruct((2, 1, DFF), F32)],
        scratch_shapes=[pltpu.VMEM((T + 2 * MARGIN, TC), F32)] * 3,
        compiler_params=_params(("parallel",)),
    )(dact, up, up, conv_w, conv_w, conv_b, conv_b)


def _dup_spec(tm, nj):
    per = DFF // nj
    return pl.BlockSpec((None, tm, nj), lambda a, b, j: (j // per, 0 if tm == T else b, j % per))


def _dup_spec_tn(tm, nj):
    per = DFF // nj
    return pl.BlockSpec((None, tm, nj), lambda j, kt, r: (j // per, 0, j % per))


def _adamw_math(w, g, m, v):
    m = ADAM_B1 * m + (1.0 - ADAM_B1) * g
    v = ADAM_B2 * v + (1.0 - ADAM_B2) * (g * g)
    m_hat = m / (1.0 - ADAM_B1 ** ADAM_STEP)
    v_hat = v / (1.0 - ADAM_B2 ** ADAM_STEP)
    delta = -ADAM_LR * (m_hat / (jnp.sqrt(v_hat) + ADAM_EPS) + ADAM_WD * w)
    return delta, m, v


ADAM_BLOCK = 256 * 1408


def _adamw_sharded(name, w, m, v, parts):
    _, r, c = w.shape
    tr = max(t for t in range(16, r + 1, 16) if r % t == 0 and t * c <= ADAM_BLOCK)

    def body(w_ref, m_ref, v_ref, p0_ref, p1_ref, g_ref, d_ref, nm_ref, nv_ref):
        def run(p_ref):
            g = p_ref[0].astype(F32)
            for k in range(1, N_DEV):
                g = g + p_ref[k].astype(F32)
            d, nm, nv = _adamw_math(w_ref[...], g, m_ref[...], v_ref[...])
            g_ref[...] = g
            d_ref[...] = d
            nm_ref[...] = nm
            nv_ref[...] = nv

        @pl.when(pl.program_id(0) == 0)
        def _():
            run(p0_ref)

        @pl.when(pl.program_id(0) == 1)
        def _():
            run(p1_ref)

    ws = pl.BlockSpec((None, tr, c), lambda l, i: (l, i, 0))
    p0 = pl.BlockSpec((N_DEV, tr, c), lambda l, i: (0, jnp.where(l == 0, i, r // tr - 1), 0))
    p1 = pl.BlockSpec((N_DEV, tr, c), lambda l, i: (0, jnp.where(l == 1, i, 0), 0))
    return pl.pallas_call(
        body, name=name, grid=(DEPTH, r // tr), in_specs=[ws, ws, ws, p0, p1], out_specs=[ws] * 4,
        out_shape=[jax.ShapeDtypeStruct(w.shape, F32)] * 4, compiler_params=_params(("arbitrary", "arbitrary")),
    )(w, m, v, *parts)


def _sum_devices(name, parts):
    r = parts.shape[1]

    def body(p_ref, o_ref):
        g = p_ref[0]
        for k in range(1, N_DEV):
            g = g + p_ref[k]
        o_ref[...] = g

    return pl.pallas_call(
        body, name=name, in_specs=[pl.BlockSpec((N_DEV, r, LANE), lambda: (0, 0, 0))],
        out_specs=pl.BlockSpec((r, LANE), lambda: (0, 0)), out_shape=jax.ShapeDtypeStruct((r, LANE), F32),
        compiler_params=_params(),
    )(parts)


def _adamw_small(name, ws, gs, ms, vs):
    n = len(ws)
    shapes = [w.shape for w in ws]
    ws, gs, ms, vs = ([a.reshape(1, -1) if a.ndim == 1 else a for a in arrs] for arrs in (ws, gs, ms, vs))
    specs = [pl.BlockSpec(memory_space=pltpu.VMEM)] * n

    def body(*refs):
        for i in range(n):
            w_ref, g_ref, m_ref, v_ref = (refs[k * n + i] for k in range(4))
            d, nm, nv = _adamw_math(w_ref[...], g_ref[...], m_ref[...], v_ref[...])
            refs[4 * n + i][...] = d
            refs[5 * n + i][...] = nm
            refs[6 * n + i][...] = nv

    outs = pl.pallas_call(
        body, name=name, in_specs=specs * 4, out_specs=specs * 3,
        out_shape=[jax.ShapeDtypeStruct(w.shape, F32) for w in ws] * 3, compiler_params=_params(),
    )(*ws, *gs, *ms, *vs)
    outs = [o.reshape(shapes[i % n]) for i, o in enumerate(outs)]
    return outs[:n], outs[n:2 * n], outs[2 * n:]


def _pack(arrays):
    flat = jnp.concatenate([a.reshape(-1) for a in arrays])
    pad = (-flat.shape[0]) % (8 * LANE)
    return jnp.pad(flat, (0, pad)).reshape(-1, LANE)


def _unpack(buf, shapes):
    flat, out, off = buf.reshape(-1), [], 0
    for s in shapes:
        n = 1
        for d in s:
            n *= d
        out.append(flat[off:off + n].reshape(s))
        off += n
    return out


def _local_step(x, target, small, weights, conv_w_full, hand_over, used):
    cos2, sin2 = _rope_tables()
    bias_a = _dilation_bias()
    tables = [_rpb_tables(f"rpb_tables_{l}", small["rpb_c"][l]) for l in range(DEPTH)]
    saved, carry = [], 0.0
    for l in range(DEPTH):
        g1, g2 = small["ln_attn"][l][None] + carry, small["ln_ffn"][l][None]
        gain, sink, cb = small["mix_gain"][l][None], small["sink_b"][l], small["conv_b"][l][None]
        cw = conv_w_full[l]
        bias = tables[l]
        h1, qkv = _prologue_matmul(f"proj_in_{l}", _rmsnorm_rows, [x, g1], [D, None],
                                   weights("w_in", l, [cos2, sin2, bias_a] + tables if l == 0 else x),
                                   (D, 1024), lambda j: (0, j), 1024, epilogue=_rope_epilogue, extras=(cos2, sin2),
                                   out_dtype=BF16)
        zero = used(f"proj_in_{l}", qkv)
        qb, kb, vb = (qkv[:, BLOCK_OF[n] * LANE:BLOCK_OF[n] * LANE + w] for n, w in (("qb", WB), ("kb", WKV), ("vb", WKV)))
        oa, lse_a = _attn_a_fwd(f"attn_a_{l}", qkv, bias_a)
        ob, lse_b = _attn_b_fwd(f"attn_b_{l}", qb, kb, vb, sink + zero)
        oc, lse_c = _attn_c_fwd(f"attn_c_{l}", qkv, bias)
        mixed, x_mid = _prologue_matmul(f"proj_out_{l}", _mix_rows, [oa, ob, oc, gain + used(f"attn_{l}", oc)],
                                        [WA, WB, WC, None],
                                        weights("w_out", l, oc), (N_DEV, D // N_DEV, 512), lambda j: (0, 0, j), 512,
                                        res=x)
        h2, up = _prologue_matmul(f"ffn_up_{l}", _rmsnorm_rows, [x_mid, g2 + used(f"proj_out_{l}", x_mid)], [D, None],
                                  weights("w_up", l, x_mid), (D, 1024), lambda j: (0, j), 1024)
        act = _ffn_mid_fwd(f"ffn_mid_{l}", up, cw, cb + used(f"ffn_up_{l}", up))
        x_out = _nn_rows(f"ffn_down_{l}", act, weights("w_down", l, act), x_mid, 4, 1024, 1024)
        carry = used(f"ffn_down_{l}", x_out)
        saved.append(dict(x=x, h1=h1, qkv=(qkv, qb, kb, vb), o=(oa, ob, oc), lse=(lse_a, lse_b, lse_c), mixed=mixed,
                          x_mid=x_mid, h2=h2, up=up, act=act, g1=g1, g2=g2, gain=gain, sink=sink, cb=cb, cw=cw, bias=bias))
        x = x_out

    loss8, dx, dxb, d_ln_final = _loss_head(x, small["ln_final"][None], target)
    sgrads = [None] * DEPTH
    for l in reversed(range(DEPTH)):
        s = saved[l]
        qkv, qb, kb, vb = s["qkv"]
        oa, ob, oc = s["o"]
        wg_in, wg_out = weights("w_in", l, None), weights("w_out", l, None)
        wg_up, wg_down = weights("w_up", l, None), weights("w_down", l, None)
        g_down = _tn_rows(f"wgrad_down_{l}", s["act"], dxb, wg_down.shape[1], 2, 512)
        zero = hand_over("w_down", l, g_down)
        dact = _nt_rows(f"dgrad_down_{l}", dxb, wg_down, 4, 512)
        dup, d_cw, d_cb = _ffn_mid_bwd(f"ffn_mid_bwd_{l}", dact, s["up"], s["cw"], s["cb"] + zero)
        g_up = _tn_cols(f"wgrad_up_{l}", s["h2"], dup, _dup_spec_tn, 2 * DFF, DFF // 2)
        zero = hand_over("w_up", l, g_up)
        dh2 = _nt_cols(f"dgrad_up_{l}", dup, _dup_spec, wg_up, DFF // 2)
        dx, dxb, d_g2 = _rmsnorm_bwd(f"norm_ffn_bwd_{l}", dh2, s["x_mid"], s["g2"] + zero, dx)
        g_out = _tn_rows(f"wgrad_out_{l}", s["mixed"], dxb, wg_out.shape[1], 2, D)
        zero = hand_over("w_out", l, g_out)
        dmixed = _nt_rows(f"dgrad_out_{l}", dxb, wg_out, 2, T)
        doa, dob, doc, d_gain = _mix_bwd(f"mix_bwd_{l}", dmixed, oa, ob, oc, s["gain"] + zero)
        lse_a, lse_b, lse_c = s["lse"]
        dqa, dka, dva = _attn_a_bwd(f"attn_a_bwd_{l}", qkv, oa, lse_a, doa, bias_a)
        dqb, dkb, dvb, d_sink = _attn_b_bwd(f"attn_b_bwd_{l}", qb, kb, vb, ob, lse_b, dob, s["sink"])
        dqc, dkc, dvc, d_bias = _attn_c_bwd(f"attn_c_bwd_{l}", qkv, oc, lse_c, doc, s["bias"])
        d_rpb = _rpb_reduce(f"rpb_reduce_{l}", d_bias)
        dproj = _rope_bwd(f"rope_bwd_{l}", (dqa, dka, dva, dqb, dkb, dvb, dqc, dkc, dvc), cos2, sin2)
        g_in = _tn_cols(f"wgrad_in_{l}", s["h1"], dproj,
                        lambda tm, tn: pl.BlockSpec((tm, tn), lambda j, kt, r: (0, j)), IN_COLS, 1024)
        zero = hand_over("w_in", l, g_in)
        dh1 = _nt_cols(f"dgrad_in_{l}", dproj, lambda tm, nc: pl.BlockSpec((tm, nc), lambda kt, i, j: (i, j)), wg_in,
                       IN_COLS // 2)
        dx, dxb, d_g1 = _rmsnorm_bwd(f"norm_attn_bwd_{l}", dh1, s["x"], s["g1"] + zero, dx)
        sgrads[l] = dict(ln_attn=d_g1[0], sink_b=d_sink[0, :HB], rpb_c=d_rpb, mix_gain=d_gain[0], ln_ffn=d_g2[0],
                         conv_w=d_cw.transpose(1, 0, 2).reshape(3, 2 * DFF), conv_b=d_cb.reshape(2 * DFF))
    return loss8[0, 0], dx, d_ln_final[0], sgrads


SMALL_NAMES = ("ln_attn", "sink_b", "rpb_c", "mix_gain", "ln_ffn", "conv_b")


def kernel(x, ln_attn, w_in, sink_b, rpb_c, mix_gain, w_out, ln_ffn, w_up, conv_w, conv_b, w_down, ln_final, loss_target, m_ln_attn, m_w_in, m_sink_b, m_rpb_c, m_mix_gain, m_w_out, m_ln_ffn, m_w_up, m_conv_w, m_conv_b, m_w_down, m_ln_final, v_ln_attn, v_w_in, v_sink_b, v_rpb_c, v_mix_gain, v_w_out, v_ln_ffn, v_w_up, v_conv_w, v_conv_b, v_w_down, v_ln_final):
    me = 4 * lax.axis_index("x") + 2 * lax.axis_index("y") + lax.axis_index("c")
    small = dict(ln_attn=ln_attn, sink_b=sink_b, rpb_c=rpb_c, mix_gain=mix_gain, ln_ffn=ln_ffn, conv_b=conv_b,
                 ln_final=ln_final)

    names = ("w_in", "w_out", "w_up", "w_down")
    shards = dict(w_in=w_in, w_out=w_out, w_up=w_up, w_down=w_down)
    order = [(n, l) for l in range(DEPTH) for n in names]
    conv_key = ("conv_w", 0)
    started, arrived, forwarded, gathered = {}, {}, {}, {}

    def side_by_side(k):
        return k[0] in ("w_in", "w_up")

    def slot_of(k):
        return _col_slot(shards[k[0]].shape[2]) if side_by_side(k) else _lead_slot

    def begin(name, ks, zero):
        srcs = [_pack([conv_w]) + zero if k == conv_key else (shards[k[0]][k[1]] + zero).astype(BF16) for k in ks]
        lands = [lax.empty((s.shape[0], N_DEV * s.shape[1]) if side_by_side(k) else (N_DEV,) + s.shape, s.dtype)
                 for k, s in zip(ks, srcs)]
        peers = [ALL_PEERS if k == conv_key else NEAR_PEERS for k in ks]
        send, recv, bufs, tok = _copy_start(name, srcs + lands, _gather_plan(peers, [slot_of(k) for k in ks]),
                                            [len(p) + 1 for p in peers])
        for i, k in enumerate(ks):
            started[k] = (send[i], recv[i], bufs[i], bufs[len(ks) + i], peers[i])
        return tok

    token = begin("gather_start_first", order[:1], 0.0)
    token = begin("gather_start_rest", [conv_key] + order[1:], token[0, 0])

    def arrive(k, after):
        send, recv, src, land, peers = started[k]
        arrived[k] = _copy_wait(f"gather_{k[0]}_{k[1]}_arrived", [src, land], [send], [recv],
                                _gather_plan([peers], [slot_of(k)]), after)

    queue = list(order)

    def advance(after):
        if not queue:
            return 0.0
        k = queue.pop(0)
        arrive(k, after)
        forwarded[k] = _copy_start(f"gather_{k[0]}_{k[1]}_forward", [arrived[k][1]], _forward_plan(slot_of(k)),
                                   [len(OTHER_CHIPS)])
        return forwarded[k][3][0, 0]

    pass_on_behind = ("proj_in_0", "attn_0", "ffn_up_0", "ffn_down_0", "proj_in_1", "attn_1", "ffn_up_1")

    def used(point, result):
        return advance(result) if point in pass_on_behind else 0.0

    def weights(n, l, after):
        k = (n, l)
        if k not in gathered:
            if k not in forwarded:
                advance(after)
            send_b, recv_b, (land,), _ = forwarded[k]
            (gathered[k],) = _copy_wait(f"gather_{n}_{l}_done", [land], send_b, recv_b, _forward_plan(slot_of(k)),
                                        after)
        return gathered[k]

    pending = {}

    def hand_over(n, l, g):
        shard = shards[n].shape[1:]
        send, recv, bufs, tok = _copy_start(f"send_grad_{n}_{l}", [g, lax.empty((N_DEV,) + shard, g.dtype)],
                                            _scatter_plan(slot_of((n, l))), [len(ALL_PEERS) + 1])
        pending[(n, l)] = (send, recv, bufs)
        return tok[0, 0]

    def received(k, after):
        send, recv, bufs = pending[k]
        return _copy_wait(f"recv_grad_{k[0]}_{k[1]}", bufs, send, recv, _scatter_plan(slot_of(k)), after)[1]

    arrive(conv_key, token)
    cw_all = arrived[conv_key][1]
    nup = w_up.shape[2]
    cw_shards = cw_all.reshape(N_DEV, -1)[:, :DEPTH * 3 * nup].reshape(N_DEV, DEPTH, 3, nup)
    conv_w_full = cw_shards.transpose(1, 2, 0, 3).reshape(DEPTH, 3, N_DEV * nup)

    loss_local, dx, d_ln_final, sgrads = _local_step(
        x[0], loss_target[0], dict(small, ln_attn=ln_attn + token[0, 0]), weights, conv_w_full, hand_over, used)

    stacked = [jnp.stack([sgrads[l][n] for l in range(DEPTH)]) for n in SMALL_NAMES + ("conv_w",)] + [d_ln_final]
    shapes = [a.shape for a in stacked]
    mine = _pack(stacked)
    send_s, recv_s, bufs_s, _ = _copy_start("gather_small_grads_start", [mine, lax.empty((N_DEV,) + mine.shape, F32)],
                                            _gather_plan([ALL_PEERS], [_lead_slot]), [len(ALL_PEERS) + 1])

    big, after = {}, dx
    moments = dict(w_in=(m_w_in, v_w_in), w_out=(m_w_out, v_w_out), w_up=(m_w_up, v_w_up), w_down=(m_w_down, v_w_down))
    for n in reversed(names):
        parts = (received((n, 0), after), received((n, 1), after))
        big[n] = _adamw_sharded(f"adamw_{n}", shards[n], *moments[n], parts)
        after = big[n][1]

    _, everyone = _copy_wait("gather_small_grads_done", bufs_s, send_s, recv_s,
                             _gather_plan([ALL_PEERS], [_lead_slot]), after)
    g_small = _unpack(_sum_devices("sum_small_grads", everyone), shapes)
    g = dict(zip(SMALL_NAMES + ("conv_w", "ln_final"), g_small))
    g["conv_w"] = lax.dynamic_slice_in_dim(g["conv_w"], me * nup, nup, axis=2)

    snames = SMALL_NAMES + ("conv_w", "ln_final")
    sw = dict(small, conv_w=conv_w)
    sm = dict(ln_attn=m_ln_attn, sink_b=m_sink_b, rpb_c=m_rpb_c, mix_gain=m_mix_gain, ln_ffn=m_ln_ffn,
              conv_b=m_conv_b, conv_w=m_conv_w, ln_final=m_ln_final)
    sv = dict(ln_attn=v_ln_attn, sink_b=v_sink_b, rpb_c=v_rpb_c, mix_gain=v_mix_gain, ln_ffn=v_ln_ffn,
              conv_b=v_conv_b, conv_w=v_conv_w, ln_final=v_ln_final)
    s_delta, s_m, s_v = (dict(zip(snames, out)) for out in _adamw_small(
        "adamw_small", [sw[n] for n in snames], [g[n] for n in snames], [sm[n] for n in snames],
        [sv[n] for n in snames]))

    loss = lax.psum(loss_local, ("x", "y", "c"))
    outputs = ("ln_attn", "w_in", "sink_b", "rpb_c", "mix_gain", "w_out", "ln_ffn", "w_up", "conv_w", "conv_b",
               "w_down", "ln_final")
    grads = [big[n][0] if n in big else g[n] for n in outputs]
    deltas = [big[n][1] if n in big else s_delta[n] for n in outputs]
    new_m = [big[n][2] if n in big else s_m[n] for n in outputs]
    new_v = [big[n][3] if n in big else s_v[n] for n in outputs]
    return (loss, dx[None], *grads, *deltas, *new_m, *new_v)
```

```python
import functools

import jax
import jax.numpy as jnp
from jax import lax
from jax.experimental import pallas as pl
from jax.experimental.pallas import tpu as pltpu

F32 = jnp.float32
BF16 = jnp.bfloat16

N_DEV = 8
T = 2048
D = 2048
DEPTH = 2
HD = 64
HA, HB, HKV, HC = 12, 10, 2, 10
WA, WB, WKV, WC = HA * HD, HB * HD, HKV * HD, HC * HD
IN_COLS = 3 * WA + WB + 2 * WKV + 3 * WC
DFF = 5632
GRID_W = 64
ROWS = T // GRID_W
NA_ROWS, NA_COLS = 8, 16
WINDOW_B = 128
EPS = 1e-6
NEG = -1e30
ROPE_THETA = 10000.0
LANE = 128
VMEM_LIMIT = 56 * 1024 * 1024

ADAM_LR, ADAM_B1, ADAM_B2, ADAM_EPS, ADAM_WD, ADAM_STEP = 0.001, 0.9, 0.999, 1e-08, 0.01, 10

GROUPS = (("qa", WA, True, True), ("ka", WA, True, False), ("va", WA, False, False),
          ("qb", WB, True, True), ("kb", WKV, True, False), ("vb", WKV, False, False),
          ("qc", WC, False, True), ("kc", WC, False, False), ("vc", WC, False, False))


def _params(sem=None):
    return pltpu.CompilerParams(dimension_semantics=sem, vmem_limit_bytes=VMEM_LIMIT)


HBM_SPEC = pl.BlockSpec(memory_space=pltpu.HBM)
SEM_SPEC = pl.BlockSpec(memory_space=pltpu.SEMAPHORE)
DATAFLOW = pltpu.SideEffectType.DATAFLOW_SIDE_EFFECTING


ALL_PEERS = tuple((p >> 2 & 1, p >> 1 & 1, p & 1) for p in range(1, N_DEV))
OTHER_CHIPS = ((1, 0, 0), (0, 1, 0), (1, 1, 0))
NEAR_PEERS = ((0, 0, 1),) + OTHER_CHIPS


def _flip(x, y, c, f):
    return (1 - x if f[0] else x, 1 - y if f[1] else y, 1 - c if f[2] else c)


def _index(pos):
    return 4 * pos[0] + 2 * pos[1] + pos[2]


class _LocalCopy:
    def __init__(self, src, dst, sem):
        self.copy = pltpu.make_async_copy(src, dst, sem)

    def start(self):
        self.copy.start()

    def wait_send(self):
        self.copy.wait()

    def wait_recv(self):
        pass


def _descriptors(plan, bufs, send_sems, recv_sems):
    x, y, c = lax.axis_index("x"), lax.axis_index("y"), lax.axis_index("c")
    return [_LocalCopy(src, dst, send_sems[g].at[i]) if partner is None else
            pltpu.make_async_remote_copy(src_ref=src, dst_ref=dst, send_sem=send_sems[g].at[i],
                                         recv_sem=recv_sems[g].at[i], device_id=partner,
                                         device_id_type=pl.DeviceIdType.MESH)
            for g, copies in enumerate(plan(bufs, x, y, c)) for i, (src, dst, partner) in enumerate(copies)]


def _copy_start(name, bufs, plan, sizes):
    nb, ng = len(bufs), len(sizes)

    def body(*refs):
        for d in _descriptors(plan, refs[:nb], refs[nb:nb + ng], refs[nb + ng:nb + 2 * ng]):
            d.start()
        refs[2 * nb + 2 * ng][...] = jnp.zeros((8, LANE), F32)

    outs = pl.pallas_call(
        body, name=name,
        out_shape=[pltpu.SemaphoreType.DMA((s,)) for s in sizes] * 2 + [pltpu.HBM(b.shape, b.dtype) for b in bufs]
        + [jax.ShapeDtypeStruct((8, LANE), F32)],
        in_specs=[HBM_SPEC] * nb,
        out_specs=[SEM_SPEC] * (2 * ng) + [HBM_SPEC] * nb + [pl.BlockSpec(memory_space=pltpu.VMEM)],
        input_output_aliases={i: 2 * ng + i for i in range(nb)},
        compiler_params=pltpu.CompilerParams(has_side_effects=DATAFLOW),
    )(*[pltpu.with_memory_space_constraint(b, pltpu.HBM) for b in bufs])
    return outs[:ng], outs[ng:2 * ng], outs[2 * ng:2 * ng + nb], outs[2 * ng + nb]


def _copy_wait(name, bufs, send_sems, recv_sems, plan, after):
    nb, ng = len(bufs), len(send_sems)
    after = list(after) if isinstance(after, (list, tuple)) else [after]

    def body(*refs):
        for d in _descriptors(plan, refs[:nb], refs[nb:nb + ng], refs[nb + ng:nb + 2 * ng]):
            d.wait_send()
            d.wait_recv()

    return pl.pallas_call(
        body, name=name, out_shape=[pltpu.HBM(b.shape, b.dtype) for b in bufs],
        in_specs=[HBM_SPEC] * nb + [SEM_SPEC] * (2 * ng) + [pl.BlockSpec(memory_space=pl.ANY)] * len(after),
        out_specs=[HBM_SPEC] * nb, input_output_aliases={i: i for i in range(nb)},
        compiler_params=pltpu.CompilerParams(has_side_effects=DATAFLOW),
    )(*bufs, *send_sems, *recv_sems, *after)


def _lead_slot(ref, k):
    return ref.at[k]


def _col_slot(width):
    return lambda ref, k: ref.at[:, pl.ds(pl.multiple_of(k * width, LANE), width)]


def _gather_plan(peer_sets, slots):
    def plan(bufs, x, y, c):
        n = len(peer_sets)
        return [[(bufs[i], slots[i](bufs[n + i], _index((x, y, c))), _flip(x, y, c, f)) for f in peers]
                + [(bufs[i], slots[i](bufs[n + i], _index((x, y, c))), None)] for i, peers in enumerate(peer_sets)]
    return plan


def _forward_plan(slot):
    def plan(bufs, x, y, c):
        pieces = [slot(bufs[0], _index(_flip(x, y, c, f))) for f in OTHER_CHIPS]
        return [[(p, p, _flip(x, y, c, (0, 0, 1))) for p in pieces]]
    return plan


def _scatter_plan(slot):
    def plan(bufs, x, y, c):
        me = _index((x, y, c))
        peers = [_flip(x, y, c, f) for f in ALL_PEERS]
        return [[(slot(bufs[0], _index(p)), bufs[1].at[me], p) for p in peers]
                + [(slot(bufs[0], me), bufs[1].at[me], None)]]
    return plan


def _flat2(v):
    return v.reshape(-1, v.shape[-1])


def _matmul(name, kind, a, a_spec, b, b_spec, out_shape, out_spec, grid, res=None, res_spec=None, acc_shape=None):
    dims = {"nn": (((1,), (0,)), ((), ())), "nt": NT_DIMS, "nts": NT_DIMS, "tn": (((0,), (0,)), ((), ()))}[kind]
    nred = grid[-1]

    def body(*refs):
        if res is None:
            a_ref, b_ref, o_ref = refs[:3]
            r_ref = None
        else:
            a_ref, b_ref, r_ref, o_ref = refs[:4]
        if kind == "nts":
            n = b_ref.shape[-1]
            part = sum(lax.dot_general(a_ref[:, blk * n:(blk + 1) * n], b_ref[blk], dims, preferred_element_type=F32)
                       for blk in range(b_ref.shape[0]))
        else:
            part = lax.dot_general(_flat2(a_ref[...]), _flat2(b_ref[...]), dims, preferred_element_type=F32)

        def finish(total):
            if r_ref is not None:
                total = total + r_ref[...]
            o_ref[...] = total.reshape(o_ref.shape).astype(o_ref.dtype)

        if nred == 1:
            finish(part)
        else:
            acc_ref = refs[-1]
            k = pl.program_id(len(grid) - 1)

            @pl.when(k == 0)
            def _():
                acc_ref[...] = part

            @pl.when(jnp.logical_and(k > 0, k < nred - 1))
            def _():
                acc_ref[...] += part

            @pl.when(k == nred - 1)
            def _():
                finish(acc_ref[...] + part)

    ins, specs = [a, b], [a_spec, b_spec]
    if res is not None:
        ins.append(res)
        specs.append(res_spec)
    scratch = [] if nred == 1 else [pltpu.VMEM(acc_shape, F32)]
    return pl.pallas_call(
        body, name=name, grid=grid, in_specs=specs, out_specs=out_spec, out_shape=out_shape, scratch_shapes=scratch,
        compiler_params=_params(("parallel",) * (len(grid) - 1) + ("arbitrary",)),
    )(*ins)


def _nn_rows(name, a, wg, res, s, tn, tm):
    _, kj, n = wg.shape
    return _matmul(
        name, "nn", a, pl.BlockSpec((tm, s * kj), lambda j, i, r: (i, r)),
        wg, pl.BlockSpec((s, kj, tn), lambda j, i, r: (r, 0, j)),
        jax.ShapeDtypeStruct((T, n), F32), pl.BlockSpec((tm, tn), lambda j, i, r: (i, j)),
        (n // tn, T // tm, N_DEV // s), res=res, res_spec=pl.BlockSpec((tm, tn), lambda j, i, r: (i, j)),
        acc_shape=(tm, tn))


def _nt_cols(name, dc, dc_spec_of, w, nc):
    k, n = w.shape
    tm = tk = 1024
    return _matmul(
        name, "nt", dc, dc_spec_of(tm, nc),
        w, pl.BlockSpec((tk, nc), lambda kt, i, j: (kt, j)),
        jax.ShapeDtypeStruct((T, k), F32), pl.BlockSpec((tm, tk), lambda kt, i, j: (i, kt)),
        (k // tk, T // tm, n // nc), acc_shape=(tm, tk))


def _nt_rows(name, dc, wg, s, tm):
    _, kj, n = wg.shape
    return _matmul(
        name, "nt", dc, pl.BlockSpec((tm, n), lambda kt, i, r: (i, 0)),
        wg, pl.BlockSpec((s, kj, n), lambda kt, i, r: (kt, 0, 0)),
        jax.ShapeDtypeStruct((T, N_DEV * kj), F32), pl.BlockSpec((tm, s * kj), lambda kt, i, r: (i, kt)),
        (N_DEV // s, T // tm, 1))


def _tn_cols(name, a, dc, dc_spec_of, n, tn):
    k = a.shape[1]
    tk = 512
    return _matmul(
        name, "tn", a, pl.BlockSpec((T, tk), lambda j, kt, r: (0, kt)),
        dc, dc_spec_of(T, tn),
        jax.ShapeDtypeStruct((k, n), BF16), pl.BlockSpec((tk, tn), lambda j, kt, r: (kt, j)),
        (n // tn, k // tk, 1))


def _tn_rows(name, a, dc, kj, s, tn):
    n = dc.shape[1]
    return _matmul(
        name, "tn", a, pl.BlockSpec((T, s * kj), lambda kt, j, r: (0, kt)),
        dc, pl.BlockSpec((T, tn), lambda kt, j, r: (0, j)),
        jax.ShapeDtypeStruct((N_DEV, kj, n), BF16), pl.BlockSpec((s, kj, tn), lambda kt, j, r: (kt, 0, j)),
        (N_DEV // s, n // tn, 1))


TR = 256


def _rows(width):
    return pl.BlockSpec((TR, width), lambda i: (i, 0))


def _whole(shape):
    return pl.BlockSpec(shape, lambda i: (0,) * len(shape))


def _rmsnorm_rows(x_ref, g_ref):
    xv = x_ref[...]
    r = lax.rsqrt(jnp.mean(xv * xv, axis=-1, keepdims=True) + EPS)
    return ((xv * r) * g_ref[...]).astype(BF16)


SUB = 256


def _prologue_matmul(name, prologue, ins, widths, w, w_block, w_index, tn, res=None, epilogue=None, extras=(),
                     out_dtype=F32, blocked_out=False):
    tm = 1024
    n = w.shape[-1]
    ni = len(ins)

    def body(*refs):
        w_ref = refs[ni]
        r_ref = refs[ni + 1] if res is not None else None
        x_refs = refs[ni + 1 + (res is not None):len(refs) - 3]
        h_ref, o_ref, h_scr = refs[-3:]

        @pl.when(pl.program_id(1) == 0)
        def _():
            h = prologue(*refs[:ni])
            h_scr[...] = h
            h_ref[...] = h

        for sub in range(tn // SUB):
            cols = slice(sub * SUB, (sub + 1) * SUB)
            w_cols = w_ref[(slice(None),) * (len(w_ref.shape) - 1) + (cols,)]
            part = jnp.dot(h_scr[...], _flat2(w_cols), preferred_element_type=F32)
            if r_ref is not None:
                part = part + r_ref[:, cols]
            if epilogue is not None:
                part = epilogue(pl.program_id(1) * (tn // SUB) + sub, part, *x_refs)
            if blocked_out:
                for b in range(SUB // LANE):
                    o_ref[sub * (SUB // LANE) + b] = part[:, b * LANE:(b + 1) * LANE].astype(out_dtype)
            else:
                o_ref[:, cols] = part.astype(out_dtype)

    tile = pl.BlockSpec((tm, tn), lambda i, j: (i, j))
    out_tile = pl.BlockSpec((tn // LANE, tm, LANE), lambda i, j: (j, i, 0)) if blocked_out else tile
    out_full = (n // LANE, T, LANE) if blocked_out else (T, n)
    specs = [pl.BlockSpec((1, D), lambda i, j: (0, 0)) if wd is None else pl.BlockSpec((tm, wd), lambda i, j: (i, 0))
             for wd in widths]
    specs.append(pl.BlockSpec(w_block, lambda i, j: w_index(j)))
    operands = list(ins) + [w]
    if res is not None:
        specs.append(tile)
        operands.append(res)
    specs += [pl.BlockSpec((tm, LANE), lambda i, j: (i, 0))] * len(extras)
    operands += list(extras)
    return pl.pallas_call(
        body, name=name, grid=(T // tm, n // tn), in_specs=specs,
        out_specs=[pl.BlockSpec((tm, D), lambda i, j: (i, 0)), out_tile],
        out_shape=[jax.ShapeDtypeStruct((T, D), BF16), jax.ShapeDtypeStruct(out_full, out_dtype)],
        scratch_shapes=[pltpu.VMEM((tm, D), BF16)], compiler_params=_params(("parallel", "arbitrary")),
    )(*operands)


def _rms_bwd_math(dy, xv, g):
    r = lax.rsqrt(jnp.mean(xv * xv, axis=-1, keepdims=True) + EPS)
    xhat = xv * r
    dxhat = dy * g
    dx = r * (dxhat - xhat * jnp.mean(dxhat * xhat, axis=-1, keepdims=True))
    return dx, dy * xhat


def _accumulate(ref, val):
    @pl.when(pl.program_id(0) == 0)
    def _():
        ref[...] = val

    @pl.when(pl.program_id(0) > 0)
    def _():
        ref[...] += val


def _rmsnorm_bwd(name, dy, x, g, res):
    def body(dy_ref, x_ref, g_ref, res_ref, dx_ref, dxb_ref, dg_ref):
        dx, dgr = _rms_bwd_math(dy_ref[...], x_ref[...], g_ref[...])
        tot = res_ref[...] + dx
        dx_ref[...] = tot
        dxb_ref[...] = tot.astype(BF16)
        _accumulate(dg_ref, jnp.sum(dgr, axis=0, keepdims=True))

    return pl.pallas_call(
        body, name=name, grid=(T // TR,), in_specs=[_rows(D), _rows(D), _whole((1, D)), _rows(D)],
        out_specs=[_rows(D), _rows(D), _whole((1, D))],
        out_shape=[jax.ShapeDtypeStruct((T, D), F32), jax.ShapeDtypeStruct((T, D), BF16),
                   jax.ShapeDtypeStruct((1, D), F32)],
        compiler_params=_params(("arbitrary",)),
    )(dy, x, g, res)


def _loss_head(x, g, target):
    def body(x_ref, g_ref, t_ref, loss_ref, dx_ref, dxb_ref, dg_ref):
        xv, gv = x_ref[...], g_ref[...]
        r = lax.rsqrt(jnp.mean(xv * xv, axis=-1, keepdims=True) + EPS)
        err = (xv * r) * gv - t_ref[...]
        part = 0.5 * jnp.sum(jnp.mean(err * err, axis=-1, keepdims=True))
        dx, dgr = _rms_bwd_math(err * (1.0 / D), xv, gv)
        dx_ref[...] = dx
        dxb_ref[...] = dx.astype(BF16)
        _accumulate(dg_ref, jnp.sum(dgr, axis=0, keepdims=True))
        _accumulate(loss_ref, jnp.full((8, LANE), part, F32))

    return pl.pallas_call(
        body, name="loss_head", grid=(T // TR,), in_specs=[_rows(D), _whole((1, D)), _rows(D)],
        out_specs=[_whole((8, LANE)), _rows(D), _rows(D), _whole((1, D))],
        out_shape=[jax.ShapeDtypeStruct((8, LANE), F32), jax.ShapeDtypeStruct((T, D), F32),
                   jax.ShapeDtypeStruct((T, D), BF16), jax.ShapeDtypeStruct((1, D), F32)],
        compiler_params=_params(("arbitrary",)),
    )(x, g, target)


MIX_OFFS = ((0, WA), (WA, WB), (WA + WB, WC))


def _mix_rows(oa_ref, ob_ref, oc_ref, g_ref):
    parts = []
    for ref, (off, w) in zip((oa_ref, ob_ref, oc_ref), MIX_OFFS):
        o = ref[...]
        r = lax.rsqrt(jnp.mean(o * o, axis=-1, keepdims=True) + EPS)
        parts.append(((o * r) * g_ref[:, off:off + w]).astype(BF16))
    return jnp.concatenate(parts, axis=1)


def _mix_bwd(name, dmixed, oa, ob, oc, gain):
    def body(dm_ref, oa_ref, ob_ref, oc_ref, g_ref, doa_ref, dob_ref, doc_ref, dg_ref):
        dgs = []
        for ref, dref, (off, w) in zip((oa_ref, ob_ref, oc_ref), (doa_ref, dob_ref, doc_ref), MIX_OFFS):
            dx, dgr = _rms_bwd_math(dm_ref[:, off:off + w], ref[...], g_ref[:, off:off + w])
            dref[...] = dx
            dgs.append(jnp.sum(dgr, axis=0, keepdims=True))
        _accumulate(dg_ref, jnp.concatenate(dgs, axis=1))

    return pl.pallas_call(
        body, name=name, grid=(T // TR,),
        in_specs=[_rows(D), _rows(WA), _rows(WB), _rows(WC), _whole((1, D))],
        out_specs=[_rows(WA), _rows(WB), _rows(WC), _whole((1, D))],
        out_shape=[jax.ShapeDtypeStruct((T, WA), F32), jax.ShapeDtypeStruct((T, WB), F32),
                   jax.ShapeDtypeStruct((T, WC), F32), jax.ShapeDtypeStruct((1, D), F32)],
        compiler_params=_params(("arbitrary",)),
    )(dmixed, oa, ob, oc, gain)


def _rope_tables():
    inv_freq = ROPE_THETA ** (-jnp.arange(0, HD, 2, dtype=F32) / HD)
    ang = jnp.arange(T, dtype=F32)[:, None] * inv_freq[None, :]
    cos, sin = jnp.cos(ang), jnp.sin(ang)
    cos2 = jnp.tile(jnp.concatenate([cos, cos], axis=1), (1, LANE // HD))
    sin2 = jnp.tile(jnp.concatenate([-sin, sin], axis=1), (1, LANE // HD))
    return cos2, sin2


def _rot_half(v):
    lane = lax.broadcasted_iota(jnp.int32, v.shape, 1)
    return jnp.where(lane % HD < HD // 2, pltpu.roll(v, LANE - HD // 2, 1), pltpu.roll(v, HD // 2, 1))


BLOCK_KINDS = tuple((rot, is_q) for _, w, rot, is_q in GROUPS for _ in range(w // LANE))
BLOCK_OF = {name: sum(w for _, w, _, _ in GROUPS[:g]) // LANE for g, (name, _, _, _) in enumerate(GROUPS)}


def _any_tile(j, tiles):
    return functools.reduce(jnp.logical_or, [j == t for t in tiles]) if tiles else False


def _rope_epilogue(j, tile, c_ref, s_ref):
    cv, sv = c_ref[...], s_ref[...]
    per, n_tiles = tile.shape[1] // LANE, IN_COLS // tile.shape[1]
    out = []
    for b in range(per):
        v = tile[:, b * LANE:(b + 1) * LANE]
        rot = _any_tile(j, [t for t in range(n_tiles) if BLOCK_KINDS[t * per + b][0]])
        is_q = _any_tile(j, [t for t in range(n_tiles) if BLOCK_KINDS[t * per + b][1]])
        if rot is not False:
            v = jnp.where(rot, v * cv + _rot_half(v) * sv, v)
        if is_q is not False:
            v = v * jnp.where(is_q, HD ** -0.5, 1.0)
        out.append(v)
    return jnp.concatenate(out, axis=1)


def _rope_bwd(name, grads, cos2, sin2):
    def body(*refs):
        ins, (c_ref, s_ref, o_ref) = refs[:9], refs[9:]
        cv, sv = c_ref[...], s_ref[...]
        off = 0
        for d_ref, (_, w, rot, is_q) in zip(ins, GROUPS):
            for b in range(w // LANE):
                v = d_ref[:, b * LANE:(b + 1) * LANE]
                if is_q:
                    v = v * (HD ** -0.5)
                if rot:
                    v = v * cv + _rot_half(v * sv)
                o_ref[:, off + b * LANE:off + (b + 1) * LANE] = v.astype(BF16)
            off += w

    return pl.pallas_call(
        body, name=name, grid=(T // TR,), in_specs=[_rows(w) for _, w, _, _ in GROUPS] + [_rows(LANE), _rows(LANE)],
        out_specs=_rows(IN_COLS), out_shape=jax.ShapeDtypeStruct((T, IN_COLS), BF16),
        compiler_params=_params(("parallel",)),
    )(*grads, cos2, sin2)


NT_DIMS = (((1,), (1,)), ((), ()))
TN_DIMS = (((0,), (0,)), ((), ()))


def _scores(q, k, bias, valid):
    s = lax.dot_general(q, k, NT_DIMS, preferred_element_type=F32)
    if bias is not None:
        s = s + bias
    if valid is not None:
        s = jnp.where(valid, s, NEG)
    return s


def _heads_fwd(heads):
    scores = [_scores(h["q"], h["k"], h.get("bias"), h.get("valid")) for h in heads]
    soft = []
    for s, h in zip(scores, heads):
        m = jnp.max(s, axis=1, keepdims=True)
        e = jnp.exp(s - m)
        l = jnp.sum(e, axis=1, keepdims=True)
        if h.get("sink") is not None:
            l = l + jnp.exp(h["sink"] - m)
        soft.append((e.astype(BF16), l, m + jnp.log(l)))
    return [(jnp.dot(e, h["v"], preferred_element_type=F32) / l, lse) for (e, l, lse), h in zip(soft, heads)]


def _heads_bwd(heads):
    dobs = [h["do"].astype(BF16) for h in heads]
    scores = [_scores(h["q"], h["k"], h.get("bias"), h.get("valid")) for h in heads]
    dps = [lax.dot_general(dob, h["v"], NT_DIMS, preferred_element_type=F32) for dob, h in zip(dobs, heads)]
    mid = []
    for s, dp, h in zip(scores, dps, heads):
        p = jnp.exp(s - h["lse"])
        delta = jnp.sum(h["do"] * h["o"], axis=1, keepdims=True)
        ds = p * (dp - delta)
        dsink = None if h.get("sink") is None else -jnp.exp(h["sink"] - h["lse"]) * delta
        mid.append((p.astype(BF16), ds, dsink))
    out = []
    for (pb, ds, dsink), dob, h in zip(mid, dobs, heads):
        dsb = ds.astype(BF16)
        out.append((jnp.dot(dsb, h["k"], preferred_element_type=F32),
                    lax.dot_general(dsb, h["q"], TN_DIMS, preferred_element_type=F32),
                    lax.dot_general(pb, dob, TN_DIMS, preferred_element_type=F32), ds, dsink))
    return out


def _per_head(cols):
    return jnp.concatenate([jnp.broadcast_to(c, (c.shape[0], HD)) for c in cols], axis=1)


DILATIONS = ((128, 1), (512, 4), (2048, 16))


BQ_A = 256
REACH_A = max(window // 2 for window, _ in DILATIONS)


def _first_key(i):
    return jnp.maximum(i * BQ_A - REACH_A, 0)


def _key_window_groups():
    groups = {}
    for i in range(T // BQ_A):
        width = min(T, (i + 1) * BQ_A + REACH_A) - max(i * BQ_A - REACH_A, 0)
        groups.setdefault(width, []).append(i)
    return groups


def _per_window(i, fn):
    for width, tiles in _key_window_groups().items():
        hit = functools.reduce(jnp.logical_or, [i == t for t in tiles])
        pl.when(hit)(functools.partial(fn, pl.multiple_of(_first_key(i), BQ_A), width))


def _dilation_bias():
    def body(o_ref):
        i = pl.program_id(0)
        t = i * BQ_A + lax.broadcasted_iota(jnp.int32, (BQ_A, T), 0)
        ad = jnp.abs(t - (_first_key(i) + lax.broadcasted_iota(jnp.int32, (BQ_A, T), 1)))
        count = jnp.zeros((BQ_A, T), jnp.int32)
        for window, r in DILATIONS:
            count += jnp.where(((ad & (r - 1)) == 0) & (ad <= window // 2), 1, 0)
        logs = jnp.where(count == 2, jnp.log(2.0), jnp.where(count == 3, jnp.log(3.0), 0.0)).astype(F32)
        o_ref[...] = jnp.where(count == 0, NEG, logs)

    return pl.pallas_call(
        body, name="dilation_bias", grid=(T // BQ_A,), out_specs=pl.BlockSpec((BQ_A, T), lambda i: (i, 0)),
        out_shape=jax.ShapeDtypeStruct((T, T), F32), compiler_params=_params(("parallel",)),
    )()


def _qkv_rows(rows, group):
    return pl.BlockSpec((rows, LANE), lambda p, i: (i, BLOCK_OF[group] + p))


def _qkv_all(group):
    return pl.BlockSpec((T, LANE), lambda p, i: (0, BLOCK_OF[group] + p))


def _attn_a_fwd(name, qkv, bias):
    def body(q_ref, k_ref, v_ref, b_ref, o_ref, lse_ref):
        def tile(first, width):
            b = b_ref[:, :width]
            outs = _heads_fwd([dict(q=q_ref[:, h * HD:(h + 1) * HD], k=k_ref[pl.ds(first, width), h * HD:(h + 1) * HD],
                                    v=v_ref[pl.ds(first, width), h * HD:(h + 1) * HD], bias=b) for h in range(2)])
            o_ref[...] = jnp.concatenate([o for o, _ in outs], axis=1)
            lse_ref[...] = _per_head([lse for _, lse in outs])

        _per_window(pl.program_id(1), tile)

    qs = pl.BlockSpec((BQ_A, LANE), lambda p, i: (i, p))
    ks = pl.BlockSpec((T, LANE), lambda p, i: (0, p))
    return pl.pallas_call(
        body, name=name, grid=(HA // 2, T // BQ_A),
        in_specs=[_qkv_rows(BQ_A, "qa"), _qkv_all("ka"), _qkv_all("va"), pl.BlockSpec((BQ_A, T), lambda p, i: (i, 0))],
        out_specs=[qs, qs],
        out_shape=[jax.ShapeDtypeStruct((T, WA), F32)] * 2, compiler_params=_params(("parallel", "parallel")),
    )(qkv, qkv, qkv, bias)


def _attn_a_bwd(name, qkv, oa, lse, doa, bias):
    def body(q_ref, k_ref, v_ref, o_ref, lse_ref, do_ref, b_ref, dq_ref, dk_ref, dv_ref):
        @pl.when(pl.program_id(1) == 0)
        def _():
            dk_ref[...] = jnp.zeros_like(dk_ref)
            dv_ref[...] = jnp.zeros_like(dv_ref)

        def tile(first, width):
            b = b_ref[:, :width]
            keys = pl.ds(first, width)
            sls = [slice(h * HD, (h + 1) * HD) for h in range(2)]
            res = _heads_bwd([dict(q=q_ref[:, sl], k=k_ref[keys, sl], v=v_ref[keys, sl], o=o_ref[:, sl],
                                   do=do_ref[:, sl], lse=lse_ref[:, sl.start:sl.start + 1], bias=b) for sl in sls])
            dq_ref[...] = jnp.concatenate([r[0] for r in res], axis=1)
            dk_ref[keys, :] += jnp.concatenate([r[1] for r in res], axis=1)
            dv_ref[keys, :] += jnp.concatenate([r[2] for r in res], axis=1)

        _per_window(pl.program_id(1), tile)

    qs = pl.BlockSpec((BQ_A, LANE), lambda p, i: (i, p))
    ks = pl.BlockSpec((T, LANE), lambda p, i: (0, p))
    return pl.pallas_call(
        body, name=name, grid=(HA // 2, T // BQ_A),
        in_specs=[_qkv_rows(BQ_A, "qa"), _qkv_all("ka"), _qkv_all("va"), qs, qs, qs,
                  pl.BlockSpec((BQ_A, T), lambda p, i: (i, 0))], out_specs=[qs, ks, ks],
        out_shape=[jax.ShapeDtypeStruct((T, WA), F32)] * 3, compiler_params=_params(("parallel", "arbitrary")),
    )(qkv, qkv, qkv, oa, lse, doa, bias)


BQ_B = 128
SPAN_B = BQ_B + 2 * WINDOW_B


def _window_b(i):
    start = pl.multiple_of(jnp.clip(i * BQ_B - WINDOW_B, 0, T - SPAN_B), BQ_B)
    qpos = i * BQ_B + lax.broadcasted_iota(jnp.int32, (BQ_B, SPAN_B), 0)
    kpos = start + lax.broadcasted_iota(jnp.int32, (BQ_B, SPAN_B), 1)
    return start, jnp.abs(qpos - kpos) <= WINDOW_B


GROUP_B = HB // HKV


def _stack_group(ref, g):
    return jnp.concatenate([ref[:, h * HD:(h + 1) * HD] for h in range(g * GROUP_B, (g + 1) * GROUP_B)], axis=0)


def _sink_column(sink_ref, g):
    return jnp.concatenate([jnp.full((BQ_B, 1), sink_ref[h], F32) for h in range(g * GROUP_B, (g + 1) * GROUP_B)],
                           axis=0)


def _unstack(stacked):
    return [s[j * BQ_B:(j + 1) * BQ_B] for s in stacked for j in range(GROUP_B)]


def _attn_b_fwd(name, qb, kb, vb, sink):
    def body(sink_ref, q_ref, k_ref, v_ref, o_ref, lse_ref):
        start, valid = _window_b(pl.program_id(0))
        valid = jnp.concatenate([valid] * GROUP_B, axis=0)
        kw, vw = k_ref[pl.ds(start, SPAN_B), :], v_ref[pl.ds(start, SPAN_B), :]
        outs = _heads_fwd([dict(q=_stack_group(q_ref, g), k=kw[:, g * HD:(g + 1) * HD], v=vw[:, g * HD:(g + 1) * HD],
                                valid=valid, sink=_sink_column(sink_ref, g)) for g in range(HKV)])
        o_ref[...] = jnp.concatenate(_unstack([o for o, _ in outs]), axis=1)
        lse_ref[...] = _per_head(_unstack([lse for _, lse in outs]))

    qs = pl.BlockSpec((BQ_B, WB), lambda i: (i, 0))
    return pl.pallas_call(
        body, name=name, grid=(T // BQ_B,),
        in_specs=[pl.BlockSpec(memory_space=pltpu.SMEM), qs, _whole((T, WKV)), _whole((T, WKV))],
        out_specs=[qs, qs],
        out_shape=[jax.ShapeDtypeStruct((T, WB), F32)] * 2, compiler_params=_params(("parallel",)),
    )(sink, qb, kb, vb)


def _attn_b_bwd(name, qb, kb, vb, ob, lse, dob, sink):
    def body(sink_ref, q_ref, k_ref, v_ref, o_ref, lse_ref, do_ref, dq_ref, dk_ref, dv_ref, dsink_ref):
        i = pl.program_id(0)
        start, valid = _window_b(i)
        valid = jnp.concatenate([valid] * GROUP_B, axis=0)
        kw, vw = k_ref[pl.ds(start, SPAN_B), :], v_ref[pl.ds(start, SPAN_B), :]
        res = _heads_bwd([dict(q=_stack_group(q_ref, g), k=kw[:, g * HD:(g + 1) * HD], v=vw[:, g * HD:(g + 1) * HD],
                               o=_stack_group(o_ref, g), do=_stack_group(do_ref, g),
                               lse=jnp.concatenate([lse_ref[:, h * HD:h * HD + 1]
                                                    for h in range(g * GROUP_B, (g + 1) * GROUP_B)], axis=0),
                               valid=valid, sink=_sink_column(sink_ref, g)) for g in range(HKV)])
        dks, dvs = [r[1] for r in res], [r[2] for r in res]
        lane = lax.broadcasted_iota(jnp.int32, (1, LANE), 1)
        dsink = jnp.zeros((1, LANE), F32)
        for h, rows in enumerate(_unstack([r[4] for r in res])):
            dsink += jnp.where(lane == h, jnp.sum(rows), 0.0)
        dq_ref[...] = jnp.concatenate(_unstack([r[0] for r in res]), axis=1)

        @pl.when(i == 0)
        def _():
            dk_ref[...] = jnp.zeros_like(dk_ref)
            dv_ref[...] = jnp.zeros_like(dv_ref)
            dsink_ref[...] = jnp.zeros_like(dsink_ref)

        dk_ref[pl.ds(start, SPAN_B), :] += jnp.concatenate(dks, axis=1)
        dv_ref[pl.ds(start, SPAN_B), :] += jnp.concatenate(dvs, axis=1)
        dsink_ref[...] += dsink

    qs = pl.BlockSpec((BQ_B, WB), lambda i: (i, 0))
    return pl.pallas_call(
        body, name=name, grid=(T // BQ_B,),
        in_specs=[pl.BlockSpec(memory_space=pltpu.SMEM), qs, _whole((T, WKV)), _whole((T, WKV)), qs, qs, qs],
        out_specs=[qs, _whole((T, WKV)), _whole((T, WKV)), _whole((1, LANE))],
        out_shape=[jax.ShapeDtypeStruct((T, WB), F32), jax.ShapeDtypeStruct((T, WKV), F32),
                   jax.ShapeDtypeStruct((T, WKV), F32), jax.ShapeDtypeStruct((1, LANE), F32)],
        compiler_params=_params(("arbitrary",)),
    )(sink, qb, kb, vb, ob, lse, dob)


SPAN_C = NA_ROWS * GRID_W


def _row_start(r):
    return jnp.clip(r - NA_ROWS // 2, 0, ROWS - NA_ROWS)


def _off_index(r):
    return _row_start(r) - r + (NA_ROWS - 1)


N_TAB = 16
RPS = 4


def _rpb_tables(name, rpb):
    circ = jnp.concatenate([rpb[..., NA_COLS - 1:], jnp.zeros(rpb.shape[:2] + (LANE - (2 * NA_COLS - 1),), F32),
                            rpb[..., :NA_COLS - 1]], axis=-1)
    circ = jnp.pad(circ, ((0, 0), (0, N_TAB + 1 - circ.shape[1]), (0, 0)))

    def body(w_ref, o_ref):
        c = lax.broadcasted_iota(jnp.int32, (GRID_W, LANE), 0)
        lane = lax.broadcasted_iota(jnp.int32, (GRID_W, LANE), 1)
        cs = jnp.clip(c - NA_COLS // 2, 0, GRID_W - NA_COLS)
        valid = (lane % GRID_W >= cs) & (lane % GRID_W < cs + NA_COLS)
        toep = [pltpu.roll(jnp.broadcast_to(w_ref[a:a + 1, :], (GRID_W, LANE)), 0, 1, stride=1, stride_axis=0)
                for a in range(N_TAB + 1)]
        for a in range(N_TAB):
            pair = jnp.where(lane < GRID_W, toep[a], pltpu.roll(toep[a + 1], GRID_W, 1))
            o_ref[a] = jnp.where(valid, pair, NEG)

    return pl.pallas_call(
        body, name=name, grid=(HC,),
        in_specs=[pl.BlockSpec((None, N_TAB + 1, LANE), lambda h: (h, 0, 0))],
        out_specs=pl.BlockSpec((None, N_TAB, GRID_W, LANE), lambda h: (h, 0, 0, 0)),
        out_shape=jax.ShapeDtypeStruct((HC, N_TAB, GRID_W, LANE), F32), compiler_params=_params(("parallel",)),
    )(circ)


def _bias_c(t_ref, h, d):
    return jnp.concatenate([t_ref[h, d + k] for k in range(0, NA_ROWS, 2)], axis=1)


def _attn_c_fwd(name, qkv, tables):
    def body(q_ref, k_ref, v_ref, t_ref, o_ref, lse_ref):
        heads = []
        for rr in range(RPS):
            r = pl.program_id(1) * RPS + rr
            rows = slice(rr * GRID_W, (rr + 1) * GRID_W)
            start = pl.multiple_of(_row_start(r) * GRID_W, GRID_W)
            kw, vw = k_ref[pl.ds(start, SPAN_C), :], v_ref[pl.ds(start, SPAN_C), :]
            heads += [dict(q=q_ref[rows, h * HD:(h + 1) * HD], k=kw[:, h * HD:(h + 1) * HD], v=vw[:, h * HD:(h + 1) * HD],
                           bias=_bias_c(t_ref, h, _off_index(r))) for h in range(2)]
        outs = _heads_fwd(heads)
        for rr in range(RPS):
            rows = slice(rr * GRID_W, (rr + 1) * GRID_W)
            o_ref[rows, :] = jnp.concatenate([o for o, _ in outs[2 * rr:2 * rr + 2]], axis=1)
            lse_ref[rows, :] = _per_head([lse for _, lse in outs[2 * rr:2 * rr + 2]])

    qs = pl.BlockSpec((RPS * GRID_W, LANE), lambda p, r: (r, p))
    ks = pl.BlockSpec((T, LANE), lambda p, r: (0, p))
    ts = pl.BlockSpec((2, N_TAB, GRID_W, LANE), lambda p, r: (p, 0, 0, 0))
    return pl.pallas_call(
        body, name=name, grid=(HC // 2, ROWS // RPS),
        in_specs=[_qkv_rows(RPS * GRID_W, "qc"), _qkv_all("kc"), _qkv_all("vc"), ts], out_specs=[qs, qs],
        out_shape=[jax.ShapeDtypeStruct((T, WC), F32)] * 2, compiler_params=_params(("parallel", "parallel")),
    )(qkv, qkv, qkv, tables)


def _attn_c_bwd(name, qkv, oc, lse, doc, tables):
    def body(q_ref, k_ref, v_ref, o_ref, lse_ref, do_ref, t_ref, dq_ref, dk_ref, dv_ref, dt_ref):
        @pl.when(pl.program_id(1) == 0)
        def _():
            dk_ref[...] = jnp.zeros_like(dk_ref)
            dv_ref[...] = jnp.zeros_like(dv_ref)
            dt_ref[...] = jnp.zeros_like(dt_ref)

        heads, where = [], []
        for rr in range(RPS):
            r = pl.program_id(1) * RPS + rr
            rows = slice(rr * GRID_W, (rr + 1) * GRID_W)
            d = _off_index(r)
            start = pl.multiple_of(_row_start(r) * GRID_W, GRID_W)
            kw, vw = k_ref[pl.ds(start, SPAN_C), :], v_ref[pl.ds(start, SPAN_C), :]
            where.append((rows, d, start))
            for h in range(2):
                sl = slice(h * HD, (h + 1) * HD)
                heads.append(dict(q=q_ref[rows, sl], k=kw[:, sl], v=vw[:, sl], o=o_ref[rows, sl], do=do_ref[rows, sl],
                                  lse=lse_ref[rows, h * HD:h * HD + 1], bias=_bias_c(t_ref, h, d)))
        res = _heads_bwd(heads)
        for rr, (rows, d, start) in enumerate(where):
            pair = res[2 * rr:2 * rr + 2]
            for h in range(2):
                for k in range(0, NA_ROWS, 2):
                    dt_ref[h, d + k] += pair[h][3][:, k * GRID_W:(k + 2) * GRID_W]
            dq_ref[rows, :] = jnp.concatenate([p[0] for p in pair], axis=1)
            dk_ref[pl.ds(start, SPAN_C), :] += jnp.concatenate([p[1] for p in pair], axis=1)
            dv_ref[pl.ds(start, SPAN_C), :] += jnp.concatenate([p[2] for p in pair], axis=1)

    qs = pl.BlockSpec((RPS * GRID_W, LANE), lambda p, r: (r, p))
    ks = pl.BlockSpec((T, LANE), lambda p, r: (0, p))
    ts = pl.BlockSpec((2, N_TAB, GRID_W, LANE), lambda p, r: (p, 0, 0, 0))
    return pl.pallas_call(
        body, name=name, grid=(HC // 2, ROWS // RPS),
        in_specs=[_qkv_rows(RPS * GRID_W, "qc"), _qkv_all("kc"), _qkv_all("vc"), qs, qs, qs, ts],
        out_specs=[qs, ks, ks, ts],
        out_shape=[jax.ShapeDtypeStruct((T, WC), F32)] * 3 + [jax.ShapeDtypeStruct((HC, N_TAB, GRID_W, LANE), F32)],
        compiler_params=_params(("parallel", "arbitrary")),
    )(qkv, qkv, qkv, oc, lse, doc, tables)


def _split3(v):
    hi = v.astype(BF16)
    r1 = v - hi.astype(F32)
    mid = r1.astype(BF16)
    lo = (r1 - mid.astype(F32)).astype(BF16)
    return hi, mid, lo


def _rpb_reduce(name, dtables):
    x = dtables.reshape(HC, N_TAB, GRID_W * LANE)
    c = jnp.arange(GRID_W)[:, None]
    lane = jnp.arange(LANE)[None, :]
    col = (lane // GRID_W) * LANE + jnp.clip(lane % GRID_W - c + (NA_COLS - 1), 0, 2 * NA_COLS - 2)
    col_onehot = (col.reshape(-1)[:, None] == jnp.arange(2 * LANE)[None, :]).astype(BF16)
    a2 = jnp.arange(N_TAB)[None, :]
    row_onehot = jnp.concatenate([(jnp.arange(16)[:, None] == a2 + u) & (a2 < 2 * NA_ROWS - 2) for u in range(2)],
                                 axis=1).astype(BF16)

    def body(x_ref, e_ref, f_ref, o_ref):
        y = sum(jnp.dot(part, e_ref[...], preferred_element_type=F32) for part in _split3(x_ref[...]))
        z = jnp.concatenate([y[:, :LANE], y[:, LANE:]], axis=0)
        o_ref[...] = sum(jnp.dot(f_ref[...], part, preferred_element_type=F32) for part in _split3(z))

    out = pl.pallas_call(
        body, name=name, grid=(HC,),
        in_specs=[pl.BlockSpec((None, N_TAB, GRID_W * LANE), lambda h: (h, 0, 0)),
                  _whole((GRID_W * LANE, 2 * LANE)), _whole((16, 2 * N_TAB))],
        out_specs=pl.BlockSpec((None, 16, LANE), lambda h: (h, 0, 0)),
        out_shape=jax.ShapeDtypeStruct((HC, 16, LANE), F32), compiler_params=_params(("parallel",)),
    )(x, col_onehot, row_onehot)
    return out[:, :2 * NA_ROWS - 1, :2 * NA_COLS - 1]


TC = 128
NCB = DFF // TC
CHUNK = 128
MARGIN = 8


def _shift_down(v, rows):
    return jnp.where(rows == 0, 0.0, pltpu.roll(v, 1, 0))


def _shift_up(v, rows):
    return jnp.where(rows == T - 1, 0.0, pltpu.roll(v, T - 1, 0))


def _conv(v, w, b, rows):
    return _shift_down(v, rows) * w[0:1] + v * w[1:2] + _shift_up(v, rows) * w[2:3] + b


def _ffn_specs():
    gate = lambda shape: pl.BlockSpec(shape, lambda j: (0, j))
    val = lambda shape: pl.BlockSpec(shape, lambda j: (0, j + NCB))
    return [pl.BlockSpec((None, T, TC), lambda j: (j, 0, 0)), pl.BlockSpec((None, T, TC), lambda j: (j + NCB, 0, 0)),
            gate((3, TC)), val((3, TC)), gate((1, TC)), val((1, TC))]


def _ffn_mid_fwd(name, up, conv_w, conv_b):
    def body(xg_ref, xv_ref, wg_ref, wv_ref, bg_ref, bv_ref, o_ref):
        rows = lax.broadcasted_iota(jnp.int32, (T, TC), 0)
        ug = _conv(xg_ref[...], wg_ref[...], bg_ref[...], rows)
        uv = _conv(xv_ref[...], wv_ref[...], bv_ref[...], rows)
        o_ref[...] = (ug * jax.nn.sigmoid(ug) * uv).astype(BF16)

    return pl.pallas_call(
        body, name=name, grid=(NCB,), in_specs=_ffn_specs(), out_specs=pl.BlockSpec((T, TC), lambda j: (0, j)),
        out_shape=jax.ShapeDtypeStruct((T, DFF), BF16), compiler_params=_params(("parallel",)),
    )(up, up, conv_w, conv_w, conv_b, conv_b)


def _ffn_mid_bwd(name, dact, up, conv_w, conv_b):
    window = CHUNK + 2 * MARGIN
    centre = slice(MARGIN, MARGIN + CHUNK)

    def shifted(v):
        return pltpu.roll(v, 1, 0), pltpu.roll(v, window - 1, 0)

    def fold(v):
        return jnp.sum(v[centre].reshape(CHUNK // 8, 8, TC), axis=0)

    def body(da_ref, xg_ref, xv_ref, wg_ref, wv_ref, bg_ref, bv_ref, dx_ref, dw_ref, db_ref, dap, xgp, xvp):
        for src, pad in ((da_ref, dap), (xg_ref, xgp), (xv_ref, xvp)):
            pad[0:MARGIN, :] = jnp.zeros((MARGIN, TC), F32)
            pad[MARGIN:MARGIN + T, :] = src[...]
            pad[MARGIN + T:, :] = jnp.zeros((MARGIN, TC), F32)
        wg, wv, bg, bv = wg_ref[...], wv_ref[...], bg_ref[...], bv_ref[...]

        def chunk(c, sums):
            r0 = pl.multiple_of(c * CHUNK, CHUNK)
            da, xg, xv = dap[pl.ds(r0, window), :], xgp[pl.ds(r0, window), :], xvp[pl.ds(r0, window), :]
            xg_prev, xg_next = shifted(xg)
            xv_prev, xv_next = shifted(xv)
            ug = xg_prev * wg[0:1] + xg * wg[1:2] + xg_next * wg[2:3] + bg
            uv = xv_prev * wv[0:1] + xv * wv[1:2] + xv_next * wv[2:3] + bv
            sg = jax.nn.sigmoid(ug)
            dug = da * uv * (sg * (1.0 + ug * (1.0 - sg)))
            duv = da * (ug * sg)
            out = []
            for half, (x_prev, x, x_next, w, du) in enumerate(((xg_prev, xg, xg_next, wg, dug),
                                                               (xv_prev, xv, xv_next, wv, duv))):
                du_prev, du_next = shifted(du)
                dx = du_next * w[0:1] + du * w[1:2] + du_prev * w[2:3]
                dx_ref[half, pl.ds(r0, CHUNK), :] = dx[centre].astype(BF16)
                out += [fold(x_prev * du), fold(x * du), fold(x_next * du), fold(du)]
            return tuple(s + o for s, o in zip(sums, out))

        sums = lax.fori_loop(0, T // CHUNK, chunk, tuple(jnp.zeros((8, TC), F32) for _ in range(8)))
        rows = [jnp.sum(s, axis=0, keepdims=True) for s in sums]
        for half in range(2):
            dw_ref[half] = jnp.concatenate(rows[4 * half:4 * half + 3], axis=0)
            db_ref[half] = rows[4 * half + 3]

    return pl.pallas_call(
        body, name=name, grid=(NCB,), in_specs=[pl.BlockSpec((T, TC), lambda j: (0, j))] + _ffn_specs(),
        out_specs=[pl.BlockSpec((2, T, TC), lambda j: (0, 0, j)), pl.BlockSpec((2, 3, TC), lambda j: (0, 0, j)),
                   pl.BlockSpec((2, 1, TC), lambda j: (0, 0, j))],
        out_shape=[jax.ShapeDtypeStruct((2, T, DFF), BF16), jax.ShapeDtypeStruct((2, 3, DFF), F32),
                   jax.ShapeDtypeStruct((2, 1, DFF), F32)],
        scratch_shapes=[pltpu.VMEM((T + 2 * MARGIN, TC), F32)] * 3,
        compiler_params=_params(("parallel",)),
    )(dact, up, up, conv_w, conv_w, conv_b, conv_b)


def _dup_spec(tm, nj):
    per = DFF // nj
    return pl.BlockSpec((None, tm, nj), lambda a, b, j: (j // per, 0 if tm == T else b, j % per))


def _dup_spec_tn(tm, nj):
    per = DFF // nj
    return pl.BlockSpec((None, tm, nj), lambda j, kt, r: (j // per, 0, j % per))


def _adamw_math(w, g, m, v):
    m = ADAM_B1 * m + (1.0 - ADAM_B1) * g
    v = ADAM_B2 * v + (1.0 - ADAM_B2) * (g * g)
    m_hat = m / (1.0 - ADAM_B1 ** ADAM_STEP)
    v_hat = v / (1.0 - ADAM_B2 ** ADAM_STEP)
    delta = -ADAM_LR * (m_hat / (jnp.sqrt(v_hat) + ADAM_EPS) + ADAM_WD * w)
    return delta, m, v


ADAM_BLOCK = 256 * 1408


def _adamw_sharded(name, w, m, v, parts):
    _, r, c = w.shape
    tr = max(t for t in range(16, r + 1, 16) if r % t == 0 and t * c <= ADAM_BLOCK)

    def body(w_ref, m_ref, v_ref, p0_ref, p1_ref, g_ref, d_ref, nm_ref, nv_ref):
        def run(p_ref):
            g = p_ref[0].astype(F32)
            for k in range(1, N_DEV):
                g = g + p_ref[k].astype(F32)
            d, nm, nv = _adamw_math(w_ref[...], g, m_ref[...], v_ref[...])
            g_ref[...] = g
            d_ref[...] = d
            nm_ref[...] = nm
            nv_ref[...] = nv

        @pl.when(pl.program_id(0) == 0)
        def _():
            run(p0_ref)

        @pl.when(pl.program_id(0) == 1)
        def _():
            run(p1_ref)

    ws = pl.BlockSpec((None, tr, c), lambda l, i: (l, i, 0))
    p0 = pl.BlockSpec((N_DEV, tr, c), lambda l, i: (0, jnp.where(l == 0, i, r // tr - 1), 0))
    p1 = pl.BlockSpec((N_DEV, tr, c), lambda l, i: (0, jnp.where(l == 1, i, 0), 0))
    return pl.pallas_call(
        body, name=name, grid=(DEPTH, r // tr), in_specs=[ws, ws, ws, p0, p1], out_specs=[ws] * 4,
        out_shape=[jax.ShapeDtypeStruct(w.shape, F32)] * 4, compiler_params=_params(("arbitrary", "arbitrary")),
    )(w, m, v, *parts)


def _sum_devices(name, parts):
    r = parts.shape[1]

    def body(p_ref, o_ref):
        g = p_ref[0]
        for k in range(1, N_DEV):
            g = g + p_ref[k]
        o_ref[...] = g

    return pl.pallas_call(
        body, name=name, in_specs=[pl.BlockSpec((N_DEV, r, LANE), lambda: (0, 0, 0))],
        out_specs=pl.BlockSpec((r, LANE), lambda: (0, 0)), out_shape=jax.ShapeDtypeStruct((r, LANE), F32),
        compiler_params=_params(),
    )(parts)


def _adamw_small(name, ws, gs, ms, vs):
    n = len(ws)
    shapes = [w.shape for w in ws]
    ws, gs, ms, vs = ([a.reshape(1, -1) if a.ndim == 1 else a for a in arrs] for arrs in (ws, gs, ms, vs))
    specs = [pl.BlockSpec(memory_space=pltpu.VMEM)] * n

    def body(*refs):
        for i in range(n):
            w_ref, g_ref, m_ref, v_ref = (refs[k * n + i] for k in range(4))
            d, nm, nv = _adamw_math(w_ref[...], g_ref[...], m_ref[...], v_ref[...])
            refs[4 * n + i][...] = d
            refs[5 * n + i][...] = nm
            refs[6 * n + i][...] = nv

    outs = pl.pallas_call(
        body, name=name, in_specs=specs * 4, out_specs=specs * 3,
        out_shape=[jax.ShapeDtypeStruct(w.shape, F32) for w in ws] * 3, compiler_params=_params(),
    )(*ws, *gs, *ms, *vs)
    outs = [o.reshape(shapes[i % n]) for i, o in enumerate(outs)]
    return outs[:n], outs[n:2 * n], outs[2 * n:]


def _pack(arrays):
    flat = jnp.concatenate([a.reshape(-1) for a in arrays])
    pad = (-flat.shape[0]) % (8 * LANE)
    return jnp.pad(flat, (0, pad)).reshape(-1, LANE)


def _unpack(buf, shapes):
    flat, out, off = buf.reshape(-1), [], 0
    for s in shapes:
        n = 1
        for d in s:
            n *= d
        out.append(flat[off:off + n].reshape(s))
        off += n
    return out


def _local_step(x, target, small, weights, conv_w_full, hand_over, used):
    cos2, sin2 = _rope_tables()
    bias_a = _dilation_bias()
    tables = [_rpb_tables(f"rpb_tables_{l}", small["rpb_c"][l]) for l in range(DEPTH)]
    saved, carry = [], 0.0
    for l in range(DEPTH):
        g1, g2 = small["ln_attn"][l][None] + carry, small["ln_ffn"][l][None]
        gain, sink, cb = small["mix_gain"][l][None], small["sink_b"][l], small["conv_b"][l][None]
        cw = conv_w_full[l]
        bias = tables[l]
        h1, qkv = _prologue_matmul(f"proj_in_{l}", _rmsnorm_rows, [x, g1], [D, None],
                                   weights("w_in", l, [cos2, sin2, bias_a] + tables if l == 0 else x),
                                   (D, 1024), lambda j: (0, j), 1024, epilogue=_rope_epilogue, extras=(cos2, sin2),
                                   out_dtype=BF16)
        zero = used(f"proj_in_{l}", qkv)
        qb, kb, vb = (qkv[:, BLOCK_OF[n] * LANE:BLOCK_OF[n] * LANE + w] for n, w in (("qb", WB), ("kb", WKV), ("vb", WKV)))
        oa, lse_a = _attn_a_fwd(f"attn_a_{l}", qkv, bias_a)
        ob, lse_b = _attn_b_fwd(f"attn_b_{l}", qb, kb, vb, sink + zero)
        oc, lse_c = _attn_c_fwd(f"attn_c_{l}", qkv, bias)
        mixed, x_mid = _prologue_matmul(f"proj_out_{l}", _mix_rows, [oa, ob, oc, gain + used(f"attn_{l}", oc)],
                                        [WA, WB, WC, None],
                                        weights("w_out", l, oc), (N_DEV, D // N_DEV, 512), lambda j: (0, 0, j), 512,
                                        res=x)
        h2, up = _prologue_matmul(f"ffn_up_{l}", _rmsnorm_rows, [x_mid, g2 + used(f"proj_out_{l}", x_mid)], [D, None],
                                  weights("w_up", l, x_mid), (D, 1024), lambda j: (0, j), 1024, blocked_out=True)
        act = _ffn_mid_fwd(f"ffn_mid_{l}", up, cw, cb + used(f"ffn_up_{l}", up))
        x_out = _nn_rows(f"ffn_down_{l}", act, weights("w_down", l, act), x_mid, 4, 1024, 1024)
        carry = used(f"ffn_down_{l}", x_out)
        saved.append(dict(x=x, h1=h1, qkv=(qkv, qb, kb, vb), o=(oa, ob, oc), lse=(lse_a, lse_b, lse_c), mixed=mixed,
                          x_mid=x_mid, h2=h2, up=up, act=act, g1=g1, g2=g2, gain=gain, sink=sink, cb=cb, cw=cw, bias=bias))
        x = x_out

    loss8, dx, dxb, d_ln_final = _loss_head(x, small["ln_final"][None], target)
    sgrads = [None] * DEPTH
    for l in reversed(range(DEPTH)):
        s = saved[l]
        qkv, qb, kb, vb = s["qkv"]
        oa, ob, oc = s["o"]
        wg_in, wg_out = weights("w_in", l, None), weights("w_out", l, None)
        wg_up, wg_down = weights("w_up", l, None), weights("w_down", l, None)
        g_down = _tn_rows(f"wgrad_down_{l}", s["act"], dxb, wg_down.shape[1], 2, 512)
        zero = hand_over("w_down", l, g_down)
        dact = _nt_rows(f"dgrad_down_{l}", dxb, wg_down, 4, 512)
        dup, d_cw, d_cb = _ffn_mid_bwd(f"ffn_mid_bwd_{l}", dact, s["up"], s["cw"], s["cb"] + zero)
        g_up = _tn_cols(f"wgrad_up_{l}", s["h2"], dup, _dup_spec_tn, 2 * DFF, DFF // 2)
        zero = hand_over("w_up", l, g_up)
        dh2 = _nt_cols(f"dgrad_up_{l}", dup, _dup_spec, wg_up, DFF // 2)
        dx, dxb, d_g2 = _rmsnorm_bwd(f"norm_ffn_bwd_{l}", dh2, s["x_mid"], s["g2"] + zero, dx)
        g_out = _tn_rows(f"wgrad_out_{l}", s["mixed"], dxb, wg_out.shape[1], 2, D)
        zero = hand_over("w_out", l, g_out)
        dmixed = _nt_rows(f"dgrad_out_{l}", dxb, wg_out, 2, T)
        doa, dob, doc, d_gain = _mix_bwd(f"mix_bwd_{l}", dmixed, oa, ob, oc, s["gain"] + zero)
        lse_a, lse_b, lse_c = s["lse"]
        dqa, dka, dva = _attn_a_bwd(f"attn_a_bwd_{l}", qkv, oa, lse_a, doa, bias_a)
        dqb, dkb, dvb, d_sink = _attn_b_bwd(f"attn_b_bwd_{l}", qb, kb, vb, ob, lse_b, dob, s["sink"])
        dqc, dkc, dvc, d_bias = _attn_c_bwd(f"attn_c_bwd_{l}", qkv, oc, lse_c, doc, s["bias"])
        d_rpb = _rpb_reduce(f"rpb_reduce_{l}", d_bias)
        dproj = _rope_bwd(f"rope_bwd_{l}", (dqa, dka, dva, dqb, dkb, dvb, dqc, dkc, dvc), cos2, sin2)
        g_in = _tn_cols(f"wgrad_in_{l}", s["h1"], dproj,
                        lambda tm, tn: pl.BlockSpec((tm, tn), lambda j, kt, r: (0, j)), IN_COLS, 1024)
        zero = hand_over("w_in", l, g_in)
        dh1 = _nt_cols(f"dgrad_in_{l}", dproj, lambda tm, nc: pl.BlockSpec((tm, nc), lambda kt, i, j: (i, j)), wg_in,
                       IN_COLS // 2)
        dx, dxb, d_g1 = _rmsnorm_bwd(f"norm_attn_bwd_{l}", dh1, s["x"], s["g1"] + zero, dx)
        sgrads[l] = dict(ln_attn=d_g1[0], sink_b=d_sink[0, :HB], rpb_c=d_rpb, mix_gain=d_gain[0], ln_ffn=d_g2[0],
                         conv_w=d_cw.transpose(1, 0, 2).reshape(3, 2 * DFF), conv_b=d_cb.reshape(2 * DFF))
    return loss8[0, 0], dx, d_ln_final[0], sgrads


SMALL_NAMES = ("ln_attn", "sink_b", "rpb_c", "mix_gain", "ln_ffn", "conv_b")


def kernel(x, ln_attn, w_in, sink_b, rpb_c, mix_gain, w_out, ln_ffn, w_up, conv_w, conv_b, w_down, ln_final, loss_target, m_ln_attn, m_w_in, m_sink_b, m_rpb_c, m_mix_gain, m_w_out, m_ln_ffn, m_w_up, m_conv_w, m_conv_b, m_w_down, m_ln_final, v_ln_attn, v_w_in, v_sink_b, v_rpb_c, v_mix_gain, v_w_out, v_ln_ffn, v_w_up, v_conv_w, v_conv_b, v_w_down, v_ln_final):
    me = 4 * lax.axis_index("x") + 2 * lax.axis_index("y") + lax.axis_index("c")
    small = dict(ln_attn=ln_attn, sink_b=sink_b, rpb_c=rpb_c, mix_gain=mix_gain, ln_ffn=ln_ffn, conv_b=conv_b,
                 ln_final=ln_final)

    names = ("w_in", "w_out", "w_up", "w_down")
    shards = dict(w_in=w_in, w_out=w_out, w_up=w_up, w_down=w_down)
    order = [(n, l) for l in range(DEPTH) for n in names]
    conv_key = ("conv_w", 0)
    started, arrived, forwarded, gathered = {}, {}, {}, {}

    def side_by_side(k):
        return k[0] in ("w_in", "w_up")

    def slot_of(k):
        return _col_slot(shards[k[0]].shape[2]) if side_by_side(k) else _lead_slot

    def begin(name, ks, zero):
        srcs = [_pack([conv_w]) + zero if k == conv_key else (shards[k[0]][k[1]] + zero).astype(BF16) for k in ks]
        lands = [lax.empty((s.shape[0], N_DEV * s.shape[1]) if side_by_side(k) else (N_DEV,) + s.shape, s.dtype)
                 for k, s in zip(ks, srcs)]
        peers = [ALL_PEERS if k == conv_key else NEAR_PEERS for k in ks]
        send, recv, bufs, tok = _copy_start(name, srcs + lands, _gather_plan(peers, [slot_of(k) for k in ks]),
                                            [len(p) + 1 for p in peers])
        for i, k in enumerate(ks):
            started[k] = (send[i], recv[i], bufs[i], bufs[len(ks) + i], peers[i])
        return tok

    token = begin("gather_start_first", order[:1], 0.0)
    token = begin("gather_start_rest", [conv_key] + order[1:], token[0, 0])

    def arrive(k, after):
        send, recv, src, land, peers = started[k]
        arrived[k] = _copy_wait(f"gather_{k[0]}_{k[1]}_arrived", [src, land], [send], [recv],
                                _gather_plan([peers], [slot_of(k)]), after)

    queue = list(order)

    def advance(after):
        if not queue:
            return 0.0
        k = queue.pop(0)
        arrive(k, after)
        forwarded[k] = _copy_start(f"gather_{k[0]}_{k[1]}_forward", [arrived[k][1]], _forward_plan(slot_of(k)),
                                   [len(OTHER_CHIPS)])
        return forwarded[k][3][0, 0]

    pass_on_behind = ("proj_in_0", "attn_0", "ffn_up_0", "ffn_down_0", "proj_in_1", "attn_1", "ffn_up_1")

    def used(point, result):
        return advance(result) if point in pass_on_behind else 0.0

    def weights(n, l, after):
        k = (n, l)
        if k not in gathered:
            if k not in forwarded:
                advance(after)
            send_b, recv_b, (land,), _ = forwarded[k]
            (gathered[k],) = _copy_wait(f"gather_{n}_{l}_done", [land], send_b, recv_b, _forward_plan(slot_of(k)),
                                        after)
        return gathered[k]

    pending = {}

    def hand_over(n, l, g):
        shard = shards[n].shape[1:]
        send, recv, bufs, tok = _copy_start(f"send_grad_{n}_{l}", [g, lax.empty((N_DEV,) + shard, g.dtype)],
                                            _scatter_plan(slot_of((n, l))), [len(ALL_PEERS) + 1])
        pending[(n, l)] = (send, recv, bufs)
        return tok[0, 0]

    def received(k, after):
        send, recv, bufs = pending[k]
        return _copy_wait(f"recv_grad_{k[0]}_{k[1]}", bufs, send, recv, _scatter_plan(slot_of(k)), after)[1]

    arrive(conv_key, token)
    cw_all = arrived[conv_key][1]
    nup = w_up.shape[2]
    cw_shards = cw_all.reshape(N_DEV, -1)[:, :DEPTH * 3 * nup].reshape(N_DEV, DEPTH, 3, nup)
    conv_w_full = cw_shards.transpose(1, 2, 0, 3).reshape(DEPTH, 3, N_DEV * nup)

    loss_local, dx, d_ln_final, sgrads = _local_step(
        x[0], loss_target[0], dict(small, ln_attn=ln_attn + token[0, 0]), weights, conv_w_full, hand_over, used)

    stacked = [jnp.stack([sgrads[l][n] for l in range(DEPTH)]) for n in SMALL_NAMES + ("conv_w",)] + [d_ln_final]
    shapes = [a.shape for a in stacked]
    mine = _pack(stacked)
    send_s, recv_s, bufs_s, _ = _copy_start("gather_small_grads_start", [mine, lax.empty((N_DEV,) + mine.shape, F32)],
                                            _gather_plan([ALL_PEERS], [_lead_slot]), [len(ALL_PEERS) + 1])

    big, after = {}, dx
    moments = dict(w_in=(m_w_in, v_w_in), w_out=(m_w_out, v_w_out), w_up=(m_w_up, v_w_up), w_down=(m_w_down, v_w_down))
    for n in reversed(names):
        parts = (received((n, 0), after), received((n, 1), after))
        big[n] = _adamw_sharded(f"adamw_{n}", shards[n], *moments[n], parts)
        after = big[n][1]

    _, everyone = _copy_wait("gather_small_grads_done", bufs_s, send_s, recv_s,
                             _gather_plan([ALL_PEERS], [_lead_slot]), after)
    g_small = _unpack(_sum_devices("sum_small_grads", everyone), shapes)
    g = dict(zip(SMALL_NAMES + ("conv_w", "ln_final"), g_small))
    g["conv_w"] = lax.dynamic_slice_in_dim(g["conv_w"], me * nup, nup, axis=2)

    snames = SMALL_NAMES + ("conv_w", "ln_final")
    sw = dict(small, conv_w=conv_w)
    sm = dict(ln_attn=m_ln_attn, sink_b=m_sink_b, rpb_c=m_rpb_c, mix_gain=m_mix_gain, ln_ffn=m_ln_ffn,
              conv_b=m_conv_b, conv_w=m_conv_w, ln_final=m_ln_final)
    sv = dict(ln_attn=v_ln_attn, sink_b=v_sink_b, rpb_c=v_rpb_c, mix_gain=v_mix_gain, ln_ffn=v_ln_ffn,
              conv_b=v_conv_b, conv_w=v_conv_w, ln_final=v_ln_final)
    s_delta, s_m, s_v = (dict(zip(snames, out)) for out in _adamw_small(
        "adamw_small", [sw[n] for n in snames], [g[n] for n in snames], [sm[n] for n in snames],
        [sv[n] for n in snames]))

    loss = lax.psum(loss_local, ("x", "y", "c"))
    outputs = ("ln_attn", "w_in", "sink_b", "rpb_c", "mix_gain", "w_out", "ln_ffn", "w_up", "conv_w", "conv_b",
               "w_down", "ln_final")
    grads = [big[n][0] if n in big else g[n] for n in outputs]
    deltas = [big[n][1] if n in big else s_delta[n] for n in outputs]
    new_m = [big[n][2] if n in big else s_m[n] for n in outputs]
    new_v = [big[n][3] if n in big else s_v[n] for n in outputs]
    return (loss, dx[None], *grads, *deltas, *new_m, *new_v)
```

```python
import functools

import jax
import jax.numpy as jnp
from jax import lax
from jax.experimental import pallas as pl
from jax.experimental.pallas import tpu as pltpu

F32 = jnp.float32
BF16 = jnp.bfloat16

N_DEV = 8
T = 2048
D = 2048
DEPTH = 2
HD = 64
HA, HB, HKV, HC = 12, 10, 2, 10
WA, WB, WKV, WC = HA * HD, HB * HD, HKV * HD, HC * HD
IN_COLS = 3 * WA + WB + 2 * WKV + 3 * WC
DFF = 5632
GRID_W = 64
ROWS = T // GRID_W
NA_ROWS, NA_COLS = 8, 16
WINDOW_B = 128
EPS = 1e-6
NEG = -1e30
ROPE_THETA = 10000.0
LANE = 128
VMEM_LIMIT = 56 * 1024 * 1024

ADAM_LR, ADAM_B1, ADAM_B2, ADAM_EPS, ADAM_WD, ADAM_STEP = 0.001, 0.9, 0.999, 1e-08, 0.01, 10

GROUPS = (("qa", WA, True, True), ("ka", WA, True, False), ("va", WA, False, False),
          ("qb", WB, True, True), ("kb", WKV, True, False), ("vb", WKV, False, False),
          ("qc", WC, False, True), ("kc", WC, False, False), ("vc", WC, False, False))


def _params(sem=None):
    return pltpu.CompilerParams(dimension_semantics=sem, vmem_limit_bytes=VMEM_LIMIT)


HBM_SPEC = pl.BlockSpec(memory_space=pltpu.HBM)
SEM_SPEC = pl.BlockSpec(memory_space=pltpu.SEMAPHORE)
DATAFLOW = pltpu.SideEffectType.DATAFLOW_SIDE_EFFECTING


ALL_PEERS = tuple((p >> 2 & 1, p >> 1 & 1, p & 1) for p in range(1, N_DEV))
OTHER_CHIPS = ((1, 0, 0), (0, 1, 0), (1, 1, 0))
NEAR_PEERS = ((0, 0, 1),) + OTHER_CHIPS


def _flip(x, y, c, f):
    return (1 - x if f[0] else x, 1 - y if f[1] else y, 1 - c if f[2] else c)


def _index(pos):
    return 4 * pos[0] + 2 * pos[1] + pos[2]


class _LocalCopy:
    def __init__(self, src, dst, sem):
        self.copy = pltpu.make_async_copy(src, dst, sem)

    def start(self):
        self.copy.start()

    def wait_send(self):
        self.copy.wait()

    def wait_recv(self):
        pass


def _descriptors(plan, bufs, send_sems, recv_sems):
    x, y, c = lax.axis_index("x"), lax.axis_index("y"), lax.axis_index("c")
    return [_LocalCopy(src, dst, send_sems[g].at[i]) if partner is None else
            pltpu.make_async_remote_copy(src_ref=src, dst_ref=dst, send_sem=send_sems[g].at[i],
                                         recv_sem=recv_sems[g].at[i], device_id=partner,
                                         device_id_type=pl.DeviceIdType.MESH)
            for g, copies in enumerate(plan(bufs, x, y, c)) for i, (src, dst, partner) in enumerate(copies)]


def _copy_start(name, bufs, plan, sizes):
    nb, ng = len(bufs), len(sizes)

    def body(*refs):
        for d in _descriptors(plan, refs[:nb], refs[nb:nb + ng], refs[nb + ng:nb + 2 * ng]):
            d.start()
        refs[2 * nb + 2 * ng][...] = jnp.zeros((8, LANE), F32)

    outs = pl.pallas_call(
        body, name=name,
        out_shape=[pltpu.SemaphoreType.DMA((s,)) for s in sizes] * 2 + [pltpu.HBM(b.shape, b.dtype) for b in bufs]
        + [jax.ShapeDtypeStruct((8, LANE), F32)],
        in_specs=[HBM_SPEC] * nb,
        out_specs=[SEM_SPEC] * (2 * ng) + [HBM_SPEC] * nb + [pl.BlockSpec(memory_space=pltpu.VMEM)],
        input_output_aliases={i: 2 * ng + i for i in range(nb)},
        compiler_params=pltpu.CompilerParams(has_side_effects=DATAFLOW),
    )(*[pltpu.with_memory_space_constraint(b, pltpu.HBM) for b in bufs])
    return outs[:ng], outs[ng:2 * ng], outs[2 * ng:2 * ng + nb], outs[2 * ng + nb]


def _copy_wait(name, bufs, send_sems, recv_sems, plan, after):
    nb, ng = len(bufs), len(send_sems)
    after = list(after) if isinstance(after, (list, tuple)) else [after]

    def body(*refs):
        for d in _descriptors(plan, refs[:nb], refs[nb:nb + ng], refs[nb + ng:nb + 2 * ng]):
            d.wait_send()
            d.wait_recv()

    return pl.pallas_call(
        body, name=name, out_shape=[pltpu.HBM(b.shape, b.dtype) for b in bufs],
        in_specs=[HBM_SPEC] * nb + [SEM_SPEC] * (2 * ng) + [pl.BlockSpec(memory_space=pl.ANY)] * len(after),
        out_specs=[HBM_SPEC] * nb, input_output_aliases={i: i for i in range(nb)},
        compiler_params=pltpu.CompilerParams(has_side_effects=DATAFLOW),
    )(*bufs, *send_sems, *recv_sems, *after)


def _lead_slot(ref, k):
    return ref.at[k]


def _col_slot(width):
    return lambda ref, k: ref.at[:, pl.ds(pl.multiple_of(k * width, LANE), width)]


def _gather_plan(peer_sets, slots):
    def plan(bufs, x, y, c):
        n = len(peer_sets)
        return [[(bufs[i], slots[i](bufs[n + i], _index((x, y, c))), _flip(x, y, c, f)) for f in peers]
                + [(bufs[i], slots[i](bufs[n + i], _index((x, y, c))), None)] for i, peers in enumerate(peer_sets)]
    return plan


def _forward_plan(slot):
    def plan(bufs, x, y, c):
        pieces = [slot(bufs[0], _index(_flip(x, y, c, f))) for f in OTHER_CHIPS]
        return [[(p, p, _flip(x, y, c, (0, 0, 1))) for p in pieces]]
    return plan


def _scatter_plan(slot):
    def plan(bufs, x, y, c):
        me = _index((x, y, c))
        peers = [_flip(x, y, c, f) for f in ALL_PEERS]
        return [[(slot(bufs[0], _index(p)), bufs[1].at[me], p) for p in peers]
                + [(slot(bufs[0], me), bufs[1].at[me], None)]]
    return plan


def _flat2(v):
    return v.reshape(-1, v.shape[-1])


def _matmul(name, kind, a, a_spec, b, b_spec, out_shape, out_spec, grid, res=None, res_spec=None, acc_shape=None):
    dims = {"nn": (((1,), (0,)), ((), ())), "nt": NT_DIMS, "nts": NT_DIMS, "tn": (((0,), (0,)), ((), ()))}[kind]
    nred = grid[-1]

    def body(*refs):
        if res is None:
            a_ref, b_ref, o_ref = refs[:3]
            r_ref = None
        else:
            a_ref, b_ref, r_ref, o_ref = refs[:4]
        if kind == "nts":
            n = b_ref.shape[-1]
            part = sum(lax.dot_general(a_ref[:, blk * n:(blk + 1) * n], b_ref[blk], dims, preferred_element_type=F32)
                       for blk in range(b_ref.shape[0]))
        else:
            part = lax.dot_general(_flat2(a_ref[...]), _flat2(b_ref[...]), dims, preferred_element_type=F32)

        def finish(total):
            if r_ref is not None:
                total = total + r_ref[...]
            o_ref[...] = total.reshape(o_ref.shape).astype(o_ref.dtype)

        if nred == 1:
            finish(part)
        else:
            acc_ref = refs[-1]
            k = pl.program_id(len(grid) - 1)

            @pl.when(k == 0)
            def _():
                acc_ref[...] = part

            @pl.when(jnp.logical_and(k > 0, k < nred - 1))
            def _():
                acc_ref[...] += part

            @pl.when(k == nred - 1)
            def _():
                finish(acc_ref[...] + part)

    ins, specs = [a, b], [a_spec, b_spec]
    if res is not None:
        ins.append(res)
        specs.append(res_spec)
    scratch = [] if nred == 1 else [pltpu.VMEM(acc_shape, F32)]
    return pl.pallas_call(
        body, name=name, grid=grid, in_specs=specs, out_specs=out_spec, out_shape=out_shape, scratch_shapes=scratch,
        compiler_params=_params(("parallel",) * (len(grid) - 1) + ("arbitrary",)),
    )(*ins)


def _nn_rows(name, a, wg, res, s, tn, tm):
    _, kj, n = wg.shape
    return _matmul(
        name, "nn", a, pl.BlockSpec((tm, s * kj), lambda j, i, r: (i, r)),
        wg, pl.BlockSpec((s, kj, tn), lambda j, i, r: (r, 0, j)),
        jax.ShapeDtypeStruct((T, n), F32), pl.BlockSpec((tm, tn), lambda j, i, r: (i, j)),
        (n // tn, T // tm, N_DEV // s), res=res, res_spec=pl.BlockSpec((tm, tn), lambda j, i, r: (i, j)),
        acc_shape=(tm, tn))


def _nt_cols(name, dc, dc_spec_of, w, nc):
    k, n = w.shape
    tm = tk = 1024
    return _matmul(
        name, "nt", dc, dc_spec_of(tm, nc),
        w, pl.BlockSpec((tk, nc), lambda kt, i, j: (kt, j)),
        jax.ShapeDtypeStruct((T, k), F32), pl.BlockSpec((tm, tk), lambda kt, i, j: (i, kt)),
        (k // tk, T // tm, n // nc), acc_shape=(tm, tk))


def _nt_rows(name, dc, wg, s, tm):
    _, kj, n = wg.shape
    return _matmul(
        name, "nt", dc, pl.BlockSpec((tm, n), lambda kt, i, r: (i, 0)),
        wg, pl.BlockSpec((s, kj, n), lambda kt, i, r: (kt, 0, 0)),
        jax.ShapeDtypeStruct((T, N_DEV * kj), F32), pl.BlockSpec((tm, s * kj), lambda kt, i, r: (i, kt)),
        (N_DEV // s, T // tm, 1))


def _tn_cols(name, a, dc, dc_spec_of, n, tn):
    k = a.shape[1]
    tk = 512
    return _matmul(
        name, "tn", a, pl.BlockSpec((T, tk), lambda j, kt, r: (0, kt)),
        dc, dc_spec_of(T, tn),
        jax.ShapeDtypeStruct((k, n), BF16), pl.BlockSpec((tk, tn), lambda j, kt, r: (kt, j)),
        (n // tn, k // tk, 1))


def _tn_rows(name, a, dc, kj, s, tn):
    n = dc.shape[1]
    return _matmul(
        name, "tn", a, pl.BlockSpec((T, s * kj), lambda kt, j, r: (0, kt)),
        dc, pl.BlockSpec((T, tn), lambda kt, j, r: (0, j)),
        jax.ShapeDtypeStruct((N_DEV, kj, n), BF16), pl.BlockSpec((s, kj, tn), lambda kt, j, r: (kt, 0, j)),
        (N_DEV // s, n // tn, 1))


TR = 512


def _rows(width):
    return pl.BlockSpec((TR, width), lambda i: (i, 0))


def _whole(shape):
    return pl.BlockSpec(shape, lambda i: (0,) * len(shape))


def _rmsnorm_rows(x_ref, g_ref):
    xv = x_ref[...]
    r = lax.rsqrt(jnp.mean(xv * xv, axis=-1, keepdims=True) + EPS)
    return ((xv * r) * g_ref[...]).astype(BF16)


SUB = 256


def _prologue_matmul(name, prologue, ins, widths, w, w_block, w_index, tn, res=None, epilogue=None, extras=(),
                     out_dtype=F32, blocked_out=False):
    tm = 1024
    n = w.shape[-1]
    ni = len(ins)

    def body(*refs):
        w_ref = refs[ni]
        r_ref = refs[ni + 1] if res is not None else None
        x_refs = refs[ni + 1 + (res is not None):len(refs) - 3]
        h_ref, o_ref, h_scr = refs[-3:]

        @pl.when(pl.program_id(1) == 0)
        def _():
            h = prologue(*refs[:ni])
            h_scr[...] = h
            h_ref[...] = h

        for sub in range(tn // SUB):
            cols = slice(sub * SUB, (sub + 1) * SUB)
            w_cols = w_ref[(slice(None),) * (len(w_ref.shape) - 1) + (cols,)]
            part = jnp.dot(h_scr[...], _flat2(w_cols), preferred_element_type=F32)
            if r_ref is not None:
                part = part + r_ref[:, cols]
            if epilogue is not None:
                part = epilogue(pl.program_id(1) * (tn // SUB) + sub, part, *x_refs)
            if blocked_out:
                for b in range(SUB // LANE):
                    o_ref[sub * (SUB // LANE) + b] = part[:, b * LANE:(b + 1) * LANE].astype(out_dtype)
            else:
                o_ref[:, cols] = part.astype(out_dtype)

    tile = pl.BlockSpec((tm, tn), lambda i, j: (i, j))
    out_tile = pl.BlockSpec((tn // LANE, tm, LANE), lambda i, j: (j, i, 0)) if blocked_out else tile
    out_full = (n // LANE, T, LANE) if blocked_out else (T, n)
    specs = [pl.BlockSpec((1, D), lambda i, j: (0, 0)) if wd is None else pl.BlockSpec((tm, wd), lambda i, j: (i, 0))
             for wd in widths]
    specs.append(pl.BlockSpec(w_block, lambda i, j: w_index(j)))
    operands = list(ins) + [w]
    if res is not None:
        specs.append(tile)
        operands.append(res)
    specs += [pl.BlockSpec((tm, LANE), lambda i, j: (i, 0))] * len(extras)
    operands += list(extras)
    return pl.pallas_call(
        body, name=name, grid=(T // tm, n // tn), in_specs=specs,
        out_specs=[pl.BlockSpec((tm, D), lambda i, j: (i, 0)), out_tile],
        out_shape=[jax.ShapeDtypeStruct((T, D), BF16), jax.ShapeDtypeStruct(out_full, out_dtype)],
        scratch_shapes=[pltpu.VMEM((tm, D), BF16)], compiler_params=_params(("parallel", "arbitrary")),
    )(*operands)


def _rms_bwd_math(dy, xv, g):
    r = lax.rsqrt(jnp.mean(xv * xv, axis=-1, keepdims=True) + EPS)
    xhat = xv * r
    dxhat = dy * g
    dx = r * (dxhat - xhat * jnp.mean(dxhat * xhat, axis=-1, keepdims=True))
    return dx, dy * xhat


def _accumulate(ref, val):
    @pl.when(pl.program_id(0) == 0)
    def _():
        ref[...] = val

    @pl.when(pl.program_id(0) > 0)
    def _():
        ref[...] += val


def _rmsnorm_bwd(name, dy, x, g, res):
    def body(dy_ref, x_ref, g_ref, res_ref, dx_ref, dxb_ref, dg_ref):
        dx, dgr = _rms_bwd_math(dy_ref[...], x_ref[...], g_ref[...])
        tot = res_ref[...] + dx
        dx_ref[...] = tot
        dxb_ref[...] = tot.astype(BF16)
        _accumulate(dg_ref, jnp.sum(dgr, axis=0, keepdims=True))

    return pl.pallas_call(
        body, name=name, grid=(T // TR,), in_specs=[_rows(D), _rows(D), _whole((1, D)), _rows(D)],
        out_specs=[_rows(D), _rows(D), _whole((1, D))],
        out_shape=[jax.ShapeDtypeStruct((T, D), F32), jax.ShapeDtypeStruct((T, D), BF16),
                   jax.ShapeDtypeStruct((1, D), F32)],
        compiler_params=_params(("arbitrary",)),
    )(dy, x, g, res)


def _loss_head(x, g, target):
    def body(x_ref, g_ref, t_ref, loss_ref, dx_ref, dxb_ref, dg_ref):
        xv, gv = x_ref[...], g_ref[...]
        r = lax.rsqrt(jnp.mean(xv * xv, axis=-1, keepdims=True) + EPS)
        err = (xv * r) * gv - t_ref[...]
        part = 0.5 * jnp.sum(jnp.mean(err * err, axis=-1, keepdims=True))
        dx, dgr = _rms_bwd_math(err * (1.0 / D), xv, gv)
        dx_ref[...] = dx
        dxb_ref[...] = dx.astype(BF16)
        _accumulate(dg_ref, jnp.sum(dgr, axis=0, keepdims=True))
        _accumulate(loss_ref, jnp.full((8, LANE), part, F32))

    return pl.pallas_call(
        body, name="loss_head", grid=(T // TR,), in_specs=[_rows(D), _whole((1, D)), _rows(D)],
        out_specs=[_whole((8, LANE)), _rows(D), _rows(D), _whole((1, D))],
        out_shape=[jax.ShapeDtypeStruct((8, LANE), F32), jax.ShapeDtypeStruct((T, D), F32),
                   jax.ShapeDtypeStruct((T, D), BF16), jax.ShapeDtypeStruct((1, D), F32)],
        compiler_params=_params(("arbitrary",)),
    )(x, g, target)


MIX_OFFS = ((0, WA), (WA, WB), (WA + WB, WC))


def _mix_rows(oa_ref, ob_ref, oc_ref, g_ref):
    parts = []
    for ref, (off, w) in zip((oa_ref, ob_ref, oc_ref), MIX_OFFS):
        o = ref[...]
        r = lax.rsqrt(jnp.mean(o * o, axis=-1, keepdims=True) + EPS)
        parts.append(((o * r) * g_ref[:, off:off + w]).astype(BF16))
    return jnp.concatenate(parts, axis=1)


def _mix_bwd(name, dmixed, oa, ob, oc, gain):
    def body(dm_ref, oa_ref, ob_ref, oc_ref, g_ref, doa_ref, dob_ref, doc_ref, dg_ref):
        dgs = []
        for ref, dref, (off, w) in zip((oa_ref, ob_ref, oc_ref), (doa_ref, dob_ref, doc_ref), MIX_OFFS):
            dx, dgr = _rms_bwd_math(dm_ref[:, off:off + w], ref[...], g_ref[:, off:off + w])
            dref[...] = dx
            dgs.append(jnp.sum(dgr, axis=0, keepdims=True))
        _accumulate(dg_ref, jnp.concatenate(dgs, axis=1))

    return pl.pallas_call(
        body, name=name, grid=(T // TR,),
        in_specs=[_rows(D), _rows(WA), _rows(WB), _rows(WC), _whole((1, D))],
        out_specs=[_rows(WA), _rows(WB), _rows(WC), _whole((1, D))],
        out_shape=[jax.ShapeDtypeStruct((T, WA), F32), jax.ShapeDtypeStruct((T, WB), F32),
                   jax.ShapeDtypeStruct((T, WC), F32), jax.ShapeDtypeStruct((1, D), F32)],
        compiler_params=_params(("arbitrary",)),
    )(dmixed, oa, ob, oc, gain)


def _rope_tables():
    inv_freq = ROPE_THETA ** (-jnp.arange(0, HD, 2, dtype=F32) / HD)
    ang = jnp.arange(T, dtype=F32)[:, None] * inv_freq[None, :]
    cos, sin = jnp.cos(ang), jnp.sin(ang)
    cos2 = jnp.tile(jnp.concatenate([cos, cos], axis=1), (1, LANE // HD))
    sin2 = jnp.tile(jnp.concatenate([-sin, sin], axis=1), (1, LANE // HD))
    return cos2, sin2


def _rot_half(v):
    lane = lax.broadcasted_iota(jnp.int32, v.shape, 1)
    return jnp.where(lane % HD < HD // 2, pltpu.roll(v, LANE - HD // 2, 1), pltpu.roll(v, HD // 2, 1))


BLOCK_KINDS = tuple((rot, is_q) for _, w, rot, is_q in GROUPS for _ in range(w // LANE))
BLOCK_OF = {name: sum(w for _, w, _, _ in GROUPS[:g]) // LANE for g, (name, _, _, _) in enumerate(GROUPS)}


def _any_tile(j, tiles):
    return functools.reduce(jnp.logical_or, [j == t for t in tiles]) if tiles else False


def _rope_epilogue(j, tile, c_ref, s_ref):
    cv, sv = c_ref[...], s_ref[...]
    per, n_tiles = tile.shape[1] // LANE, IN_COLS // tile.shape[1]
    out = []
    for b in range(per):
        v = tile[:, b * LANE:(b + 1) * LANE]
        rot = _any_tile(j, [t for t in range(n_tiles) if BLOCK_KINDS[t * per + b][0]])
        is_q = _any_tile(j, [t for t in range(n_tiles) if BLOCK_KINDS[t * per + b][1]])
        if rot is not False:
            v = jnp.where(rot, v * cv + _rot_half(v) * sv, v)
        if is_q is not False:
            v = v * jnp.where(is_q, HD ** -0.5, 1.0)
        out.append(v)
    return jnp.concatenate(out, axis=1)


def _rope_bwd(name, grads, cos2, sin2):
    def body(*refs):
        ins, (c_ref, s_ref, o_ref) = refs[:9], refs[9:]
        cv, sv = c_ref[...], s_ref[...]
        off = 0
        for d_ref, (_, w, rot, is_q) in zip(ins, GROUPS):
            for b in range(w // LANE):
                v = d_ref[:, b * LANE:(b + 1) * LANE]
                if is_q:
                    v = v * (HD ** -0.5)
                if rot:
                    v = v * cv + _rot_half(v * sv)
                o_ref[:, off + b * LANE:off + (b + 1) * LANE] = v.astype(BF16)
            off += w

    return pl.pallas_call(
        body, name=name, grid=(T // TR,), in_specs=[_rows(w) for _, w, _, _ in GROUPS] + [_rows(LANE), _rows(LANE)],
        out_specs=_rows(IN_COLS), out_shape=jax.ShapeDtypeStruct((T, IN_COLS), BF16),
        compiler_params=_params(("parallel",)),
    )(*grads, cos2, sin2)


NT_DIMS = (((1,), (1,)), ((), ()))
TN_DIMS = (((0,), (0,)), ((), ()))


def _scores(q, k, bias, valid):
    s = lax.dot_general(q, k, NT_DIMS, preferred_element_type=F32)
    if bias is not None:
        s = s + bias
    if valid is not None:
        s = jnp.where(valid, s, NEG)
    return s


def _heads_fwd(heads):
    scores = [_scores(h["q"], h["k"], h.get("bias"), h.get("valid")) for h in heads]
    soft = []
    for s, h in zip(scores, heads):
        m = jnp.max(s, axis=1, keepdims=True)
        e = jnp.exp(s - m)
        l = jnp.sum(e, axis=1, keepdims=True)
        if h.get("sink") is not None:
            l = l + jnp.exp(h["sink"] - m)
        soft.append((e.astype(BF16), l, m + jnp.log(l)))
    return [(jnp.dot(e, h["v"], preferred_element_type=F32) / l, lse) for (e, l, lse), h in zip(soft, heads)]


def _heads_bwd(heads):
    dobs = [h["do"].astype(BF16) for h in heads]
    scores = [_scores(h["q"], h["k"], h.get("bias"), h.get("valid")) for h in heads]
    dps = [lax.dot_general(dob, h["v"], NT_DIMS, preferred_element_type=F32) for dob, h in zip(dobs, heads)]
    mid = []
    for s, dp, h in zip(scores, dps, heads):
        p = jnp.exp(s - h["lse"])
        delta = jnp.sum(h["do"] * h["o"], axis=1, keepdims=True)
        ds = p * (dp - delta)
        dsink = None if h.get("sink") is None else -jnp.exp(h["sink"] - h["lse"]) * delta
        mid.append((p.astype(BF16), ds, dsink))
    out = []
    for (pb, ds, dsink), dob, h in zip(mid, dobs, heads):
        dsb = ds.astype(BF16)
        out.append((jnp.dot(dsb, h["k"], preferred_element_type=F32),
                    lax.dot_general(dsb, h["q"], TN_DIMS, preferred_element_type=F32),
                    lax.dot_general(pb, dob, TN_DIMS, preferred_element_type=F32), ds, dsink))
    return out


def _per_head(cols):
    return jnp.concatenate([jnp.broadcast_to(c, (c.shape[0], HD)) for c in cols], axis=1)


DILATIONS = ((128, 1), (512, 4), (2048, 16))


BQ_A = 256
REACH_A = max(window // 2 for window, _ in DILATIONS)


def _first_key(i):
    return jnp.maximum(i * BQ_A - REACH_A, 0)


def _key_window_groups():
    groups = {}
    for i in range(T // BQ_A):
        width = min(T, (i + 1) * BQ_A + REACH_A) - max(i * BQ_A - REACH_A, 0)
        groups.setdefault(width, []).append(i)
    return groups


def _per_window(i, fn):
    for width, tiles in _key_window_groups().items():
        hit = functools.reduce(jnp.logical_or, [i == t for t in tiles])
        pl.when(hit)(functools.partial(fn, pl.multiple_of(_first_key(i), BQ_A), width))


def _dilation_bias():
    def body(o_ref):
        i = pl.program_id(0)
        t = i * BQ_A + lax.broadcasted_iota(jnp.int32, (BQ_A, T), 0)
        ad = jnp.abs(t - (_first_key(i) + lax.broadcasted_iota(jnp.int32, (BQ_A, T), 1)))
        count = jnp.zeros((BQ_A, T), jnp.int32)
        for window, r in DILATIONS:
            count += jnp.where(((ad & (r - 1)) == 0) & (ad <= window // 2), 1, 0)
        logs = jnp.where(count == 2, jnp.log(2.0), jnp.where(count == 3, jnp.log(3.0), 0.0)).astype(F32)
        o_ref[...] = jnp.where(count == 0, NEG, logs)

    return pl.pallas_call(
        body, name="dilation_bias", grid=(T // BQ_A,), out_specs=pl.BlockSpec((BQ_A, T), lambda i: (i, 0)),
        out_shape=jax.ShapeDtypeStruct((T, T), F32), compiler_params=_params(("parallel",)),
    )()


def _qkv_rows(rows, group):
    return pl.BlockSpec((rows, LANE), lambda p, i: (i, BLOCK_OF[group] + p))


def _qkv_all(group):
    return pl.BlockSpec((T, LANE), lambda p, i: (0, BLOCK_OF[group] + p))


def _attn_a_fwd(name, qkv, bias):
    def body(q_ref, k_ref, v_ref, b_ref, o_ref, lse_ref):
        def tile(first, width):
            b = b_ref[:, :width]
            outs = _heads_fwd([dict(q=q_ref[:, h * HD:(h + 1) * HD], k=k_ref[pl.ds(first, width), h * HD:(h + 1) * HD],
                                    v=v_ref[pl.ds(first, width), h * HD:(h + 1) * HD], bias=b) for h in range(2)])
            o_ref[...] = jnp.concatenate([o for o, _ in outs], axis=1)
            lse_ref[...] = _per_head([lse for _, lse in outs])

        _per_window(pl.program_id(1), tile)

    qs = pl.BlockSpec((BQ_A, LANE), lambda p, i: (i, p))
    ks = pl.BlockSpec((T, LANE), lambda p, i: (0, p))
    return pl.pallas_call(
        body, name=name, grid=(HA // 2, T // BQ_A),
        in_specs=[_qkv_rows(BQ_A, "qa"), _qkv_all("ka"), _qkv_all("va"), pl.BlockSpec((BQ_A, T), lambda p, i: (i, 0))],
        out_specs=[qs, qs],
        out_shape=[jax.ShapeDtypeStruct((T, WA), F32)] * 2, compiler_params=_params(("parallel", "parallel")),
    )(qkv, qkv, qkv, bias)


def _attn_a_bwd(name, qkv, oa, lse, doa, bias):
    def body(q_ref, k_ref, v_ref, o_ref, lse_ref, do_ref, b_ref, dq_ref, dk_ref, dv_ref):
        @pl.when(pl.program_id(1) == 0)
        def _():
            dk_ref[...] = jnp.zeros_like(dk_ref)
            dv_ref[...] = jnp.zeros_like(dv_ref)

        def tile(first, width):
            b = b_ref[:, :width]
            keys = pl.ds(first, width)
            sls = [slice(h * HD, (h + 1) * HD) for h in range(2)]
            res = _heads_bwd([dict(q=q_ref[:, sl], k=k_ref[keys, sl], v=v_ref[keys, sl], o=o_ref[:, sl],
                                   do=do_ref[:, sl], lse=lse_ref[:, sl.start:sl.start + 1], bias=b) for sl in sls])
            dq_ref[...] = jnp.concatenate([r[0] for r in res], axis=1)
            dk_ref[keys, :] += jnp.concatenate([r[1] for r in res], axis=1)
            dv_ref[keys, :] += jnp.concatenate([r[2] for r in res], axis=1)

        _per_window(pl.program_id(1), tile)

    qs = pl.BlockSpec((BQ_A, LANE), lambda p, i: (i, p))
    ks = pl.BlockSpec((T, LANE), lambda p, i: (0, p))
    return pl.pallas_call(
        body, name=name, grid=(HA // 2, T // BQ_A),
        in_specs=[_qkv_rows(BQ_A, "qa"), _qkv_all("ka"), _qkv_all("va"), qs, qs, qs,
                  pl.BlockSpec((BQ_A, T), lambda p, i: (i, 0))], out_specs=[qs, ks, ks],
        out_shape=[jax.ShapeDtypeStruct((T, WA), F32)] * 3, compiler_params=_params(("parallel", "arbitrary")),
    )(qkv, qkv, qkv, oa, lse, doa, bias)


BQ_B = 128
SPAN_B = BQ_B + 2 * WINDOW_B


def _window_b(i):
    start = pl.multiple_of(jnp.clip(i * BQ_B - WINDOW_B, 0, T - SPAN_B), BQ_B)
    qpos = i * BQ_B + lax.broadcasted_iota(jnp.int32, (BQ_B, SPAN_B), 0)
    kpos = start + lax.broadcasted_iota(jnp.int32, (BQ_B, SPAN_B), 1)
    return start, jnp.abs(qpos - kpos) <= WINDOW_B


GROUP_B = HB // HKV


def _stack_group(ref, g):
    return jnp.concatenate([ref[:, h * HD:(h + 1) * HD] for h in range(g * GROUP_B, (g + 1) * GROUP_B)], axis=0)


def _sink_column(sink_ref, g):
    return jnp.concatenate([jnp.full((BQ_B, 1), sink_ref[h], F32) for h in range(g * GROUP_B, (g + 1) * GROUP_B)],
                           axis=0)


def _unstack(stacked):
    return [s[j * BQ_B:(j + 1) * BQ_B] for s in stacked for j in range(GROUP_B)]


def _attn_b_fwd(name, qb, kb, vb, sink):
    def body(sink_ref, q_ref, k_ref, v_ref, o_ref, lse_ref):
        start, valid = _window_b(pl.program_id(0))
        valid = jnp.concatenate([valid] * GROUP_B, axis=0)
        kw, vw = k_ref[pl.ds(start, SPAN_B), :], v_ref[pl.ds(start, SPAN_B), :]
        outs = _heads_fwd([dict(q=_stack_group(q_ref, g), k=kw[:, g * HD:(g + 1) * HD], v=vw[:, g * HD:(g + 1) * HD],
                                valid=valid, sink=_sink_column(sink_ref, g)) for g in range(HKV)])
        o_ref[...] = jnp.concatenate(_unstack([o for o, _ in outs]), axis=1)
        lse_ref[...] = _per_head(_unstack([lse for _, lse in outs]))

    qs = pl.BlockSpec((BQ_B, WB), lambda i: (i, 0))
    return pl.pallas_call(
        body, name=name, grid=(T // BQ_B,),
        in_specs=[pl.BlockSpec(memory_space=pltpu.SMEM), qs, _whole((T, WKV)), _whole((T, WKV))],
        out_specs=[qs, qs],
        out_shape=[jax.ShapeDtypeStruct((T, WB), F32)] * 2, compiler_params=_params(("parallel",)),
    )(sink, qb, kb, vb)


def _attn_b_bwd(name, qb, kb, vb, ob, lse, dob, sink):
    def body(sink_ref, q_ref, k_ref, v_ref, o_ref, lse_ref, do_ref, dq_ref, dk_ref, dv_ref, dsink_ref):
        i = pl.program_id(0)
        start, valid = _window_b(i)
        valid = jnp.concatenate([valid] * GROUP_B, axis=0)
        kw, vw = k_ref[pl.ds(start, SPAN_B), :], v_ref[pl.ds(start, SPAN_B), :]
        res = _heads_bwd([dict(q=_stack_group(q_ref, g), k=kw[:, g * HD:(g + 1) * HD], v=vw[:, g * HD:(g + 1) * HD],
                               o=_stack_group(o_ref, g), do=_stack_group(do_ref, g),
                               lse=jnp.concatenate([lse_ref[:, h * HD:h * HD + 1]
                                                    for h in range(g * GROUP_B, (g + 1) * GROUP_B)], axis=0),
                               valid=valid, sink=_sink_column(sink_ref, g)) for g in range(HKV)])
        dks, dvs = [r[1] for r in res], [r[2] for r in res]
        lane = lax.broadcasted_iota(jnp.int32, (1, LANE), 1)
        dsink = jnp.zeros((1, LANE), F32)
        for h, rows in enumerate(_unstack([r[4] for r in res])):
            dsink += jnp.where(lane == h, jnp.sum(rows), 0.0)
        dq_ref[...] = jnp.concatenate(_unstack([r[0] for r in res]), axis=1)

        @pl.when(i == 0)
        def _():
            dk_ref[...] = jnp.zeros_like(dk_ref)
            dv_ref[...] = jnp.zeros_like(dv_ref)
            dsink_ref[...] = jnp.zeros_like(dsink_ref)

        dk_ref[pl.ds(start, SPAN_B), :] += jnp.concatenate(dks, axis=1)
        dv_ref[pl.ds(start, SPAN_B), :] += jnp.concatenate(dvs, axis=1)
        dsink_ref[...] += dsink

    qs = pl.BlockSpec((BQ_B, WB), lambda i: (i, 0))
    return pl.pallas_call(
        body, name=name, grid=(T // BQ_B,),
        in_specs=[pl.BlockSpec(memory_space=pltpu.SMEM), qs, _whole((T, WKV)), _whole((T, WKV)), qs, qs, qs],
        out_specs=[qs, _whole((T, WKV)), _whole((T, WKV)), _whole((1, LANE))],
        out_shape=[jax.ShapeDtypeStruct((T, WB), F32), jax.ShapeDtypeStruct((T, WKV), F32),
                   jax.ShapeDtypeStruct((T, WKV), F32), jax.ShapeDtypeStruct((1, LANE), F32)],
        compiler_params=_params(("arbitrary",)),
    )(sink, qb, kb, vb, ob, lse, dob)


SPAN_C = NA_ROWS * GRID_W


def _row_start(r):
    return jnp.clip(r - NA_ROWS // 2, 0, ROWS - NA_ROWS)


def _off_index(r):
    return _row_start(r) - r + (NA_ROWS - 1)


N_TAB = 16
RPS = 4


def _rpb_tables(name, rpb):
    circ = jnp.concatenate([rpb[..., NA_COLS - 1:], jnp.zeros(rpb.shape[:2] + (LANE - (2 * NA_COLS - 1),), F32),
                            rpb[..., :NA_COLS - 1]], axis=-1)
    circ = jnp.pad(circ, ((0, 0), (0, N_TAB + 1 - circ.shape[1]), (0, 0)))

    def body(w_ref, o_ref):
        c = lax.broadcasted_iota(jnp.int32, (GRID_W, LANE), 0)
        lane = lax.broadcasted_iota(jnp.int32, (GRID_W, LANE), 1)
        cs = jnp.clip(c - NA_COLS // 2, 0, GRID_W - NA_COLS)
        valid = (lane % GRID_W >= cs) & (lane % GRID_W < cs + NA_COLS)
        toep = [pltpu.roll(jnp.broadcast_to(w_ref[a:a + 1, :], (GRID_W, LANE)), 0, 1, stride=1, stride_axis=0)
                for a in range(N_TAB + 1)]
        for a in range(N_TAB):
            pair = jnp.where(lane < GRID_W, toep[a], pltpu.roll(toep[a + 1], GRID_W, 1))
            o_ref[a] = jnp.where(valid, pair, NEG)

    return pl.pallas_call(
        body, name=name, grid=(HC,),
        in_specs=[pl.BlockSpec((None, N_TAB + 1, LANE), lambda h: (h, 0, 0))],
        out_specs=pl.BlockSpec((None, N_TAB, GRID_W, LANE), lambda h: (h, 0, 0, 0)),
        out_shape=jax.ShapeDtypeStruct((HC, N_TAB, GRID_W, LANE), F32), compiler_params=_params(("parallel",)),
    )(circ)


def _bias_c(t_ref, h, d):
    return jnp.concatenate([t_ref[h, d + k] for k in range(0, NA_ROWS, 2)], axis=1)


def _attn_c_fwd(name, qkv, tables):
    def body(q_ref, k_ref, v_ref, t_ref, o_ref, lse_ref):
        heads = []
        for rr in range(RPS):
            r = pl.program_id(1) * RPS + rr
            rows = slice(rr * GRID_W, (rr + 1) * GRID_W)
            start = pl.multiple_of(_row_start(r) * GRID_W, GRID_W)
            kw, vw = k_ref[pl.ds(start, SPAN_C), :], v_ref[pl.ds(start, SPAN_C), :]
            heads += [dict(q=q_ref[rows, h * HD:(h + 1) * HD], k=kw[:, h * HD:(h + 1) * HD], v=vw[:, h * HD:(h + 1) * HD],
                           bias=_bias_c(t_ref, h, _off_index(r))) for h in range(2)]
        outs = _heads_fwd(heads)
        for rr in range(RPS):
            rows = slice(rr * GRID_W, (rr + 1) * GRID_W)
            o_ref[rows, :] = jnp.concatenate([o for o, _ in outs[2 * rr:2 * rr + 2]], axis=1)
            lse_ref[rows, :] = _per_head([lse for _, lse in outs[2 * rr:2 * rr + 2]])

    qs = pl.BlockSpec((RPS * GRID_W, LANE), lambda p, r: (r, p))
    ks = pl.BlockSpec((T, LANE), lambda p, r: (0, p))
    ts = pl.BlockSpec((2, N_TAB, GRID_W, LANE), lambda p, r: (p, 0, 0, 0))
    return pl.pallas_call(
        body, name=name, grid=(HC // 2, ROWS // RPS),
        in_specs=[_qkv_rows(RPS * GRID_W, "qc"), _qkv_all("kc"), _qkv_all("vc"), ts], out_specs=[qs, qs],
        out_shape=[jax.ShapeDtypeStruct((T, WC), F32)] * 2, compiler_params=_params(("parallel", "parallel")),
    )(qkv, qkv, qkv, tables)


def _attn_c_bwd(name, qkv, oc, lse, doc, tables):
    def body(q_ref, k_ref, v_ref, o_ref, lse_ref, do_ref, t_ref, dq_ref, dk_ref, dv_ref, dt_ref):
        @pl.when(pl.program_id(1) == 0)
        def _():
            dk_ref[...] = jnp.zeros_like(dk_ref)
            dv_ref[...] = jnp.zeros_like(dv_ref)
            dt_ref[...] = jnp.zeros_like(dt_ref)

        heads, where = [], []
        for rr in range(RPS):
            r = pl.program_id(1) * RPS + rr
            rows = slice(rr * GRID_W, (rr + 1) * GRID_W)
            d = _off_index(r)
            start = pl.multiple_of(_row_start(r) * GRID_W, GRID_W)
            kw, vw = k_ref[pl.ds(start, SPAN_C), :], v_ref[pl.ds(start, SPAN_C), :]
            where.append((rows, d, start))
            for h in range(2):
                sl = slice(h * HD, (h + 1) * HD)
                heads.append(dict(q=q_ref[rows, sl], k=kw[:, sl], v=vw[:, sl], o=o_ref[rows, sl], do=do_ref[rows, sl],
                                  lse=lse_ref[rows, h * HD:h * HD + 1], bias=_bias_c(t_ref, h, d)))
        res = _heads_bwd(heads)
        for rr, (rows, d, start) in enumerate(where):
            pair = res[2 * rr:2 * rr + 2]
            for h in range(2):
                for k in range(0, NA_ROWS, 2):
                    dt_ref[h, d + k] += pair[h][3][:, k * GRID_W:(k + 2) * GRID_W]
            dq_ref[rows, :] = jnp.concatenate([p[0] for p in pair], axis=1)
            dk_ref[pl.ds(start, SPAN_C), :] += jnp.concatenate([p[1] for p in pair], axis=1)
            dv_ref[pl.ds(start, SPAN_C), :] += jnp.concatenate([p[2] for p in pair], axis=1)

    qs = pl.BlockSpec((RPS * GRID_W, LANE), lambda p, r: (r, p))
    ks = pl.BlockSpec((T, LANE), lambda p, r: (0, p))
    ts = pl.BlockSpec((2, N_TAB, GRID_W, LANE), lambda p, r: (p, 0, 0, 0))
    return pl.pallas_call(
        body, name=name, grid=(HC // 2, ROWS // RPS),
        in_specs=[_qkv_rows(RPS * GRID_W, "qc"), _qkv_all("kc"), _qkv_all("vc"), qs, qs, qs, ts],
        out_specs=[qs, ks, ks, ts],
        out_shape=[jax.ShapeDtypeStruct((T, WC), F32)] * 3 + [jax.ShapeDtypeStruct((HC, N_TAB, GRID_W, LANE), F32)],
        compiler_params=_params(("parallel", "arbitrary")),
    )(qkv, qkv, qkv, oc, lse, doc, tables)


def _split3(v):
    hi = v.astype(BF16)
    r1 = v - hi.astype(F32)
    mid = r1.astype(BF16)
    lo = (r1 - mid.astype(F32)).astype(BF16)
    return hi, mid, lo


def _rpb_reduce(name, dtables):
    x = dtables.reshape(HC, N_TAB, GRID_W * LANE)
    c = jnp.arange(GRID_W)[:, None]
    lane = jnp.arange(LANE)[None, :]
    col = (lane // GRID_W) * LANE + jnp.clip(lane % GRID_W - c + (NA_COLS - 1), 0, 2 * NA_COLS - 2)
    col_onehot = (col.reshape(-1)[:, None] == jnp.arange(2 * LANE)[None, :]).astype(BF16)
    a2 = jnp.arange(N_TAB)[None, :]
    row_onehot = jnp.concatenate([(jnp.arange(16)[:, None] == a2 + u) & (a2 < 2 * NA_ROWS - 2) for u in range(2)],
                                 axis=1).astype(BF16)

    def body(x_ref, e_ref, f_ref, o_ref):
        y = sum(jnp.dot(part, e_ref[...], preferred_element_type=F32) for part in _split3(x_ref[...]))
        z = jnp.concatenate([y[:, :LANE], y[:, LANE:]], axis=0)
        o_ref[...] = sum(jnp.dot(f_ref[...], part, preferred_element_type=F32) for part in _split3(z))

    out = pl.pallas_call(
        body, name=name, grid=(HC,),
        in_specs=[pl.BlockSpec((None, N_TAB, GRID_W * LANE), lambda h: (h, 0, 0)),
                  _whole((GRID_W * LANE, 2 * LANE)), _whole((16, 2 * N_TAB))],
        out_specs=pl.BlockSpec((None, 16, LANE), lambda h: (h, 0, 0)),
        out_shape=jax.ShapeDtypeStruct((HC, 16, LANE), F32), compiler_params=_params(("parallel",)),
    )(x, col_onehot, row_onehot)
    return out[:, :2 * NA_ROWS - 1, :2 * NA_COLS - 1]


TC = 128
CHUNK = 128
MARGIN = 8


def _shift_down(v, rows):
    return jnp.where(rows == 0, 0.0, pltpu.roll(v, 1, 0))


def _shift_up(v, rows):
    return jnp.where(rows == T - 1, 0.0, pltpu.roll(v, T - 1, 0))


def _conv(v, w, b, rows):
    return _shift_down(v, rows) * w[0:1] + v * w[1:2] + _shift_up(v, rows) * w[2:3] + b


FWD_BLOCKS = 4
BWD_BLOCKS = 2


def _ffn_mid_fwd(name, up, conv_w, conv_b):
    wide = FWD_BLOCKS * TC

    def body(xg_ref, xv_ref, wg_ref, wv_ref, bg_ref, bv_ref, o_ref):
        rows = lax.broadcasted_iota(jnp.int32, (T, TC), 0)
        for b in range(FWD_BLOCKS):
            lanes = slice(b * TC, (b + 1) * TC)
            ug = _conv(xg_ref[b], wg_ref[:, lanes], bg_ref[:, lanes], rows)
            uv = _conv(xv_ref[b], wv_ref[:, lanes], bv_ref[:, lanes], rows)
            o_ref[:, lanes] = (ug * jax.nn.sigmoid(ug) * uv).astype(BF16)

    gate = lambda shape: pl.BlockSpec(shape, lambda j: (0, j))
    val = lambda shape: pl.BlockSpec(shape, lambda j: (0, j + DFF // wide))
    return pl.pallas_call(
        body, name=name, grid=(DFF // wide,),
        in_specs=[pl.BlockSpec((FWD_BLOCKS, T, TC), lambda j: (j, 0, 0)),
                  pl.BlockSpec((FWD_BLOCKS, T, TC), lambda j: (j + DFF // wide, 0, 0)),
                  gate((3, wide)), val((3, wide)), gate((1, wide)), val((1, wide))],
        out_specs=pl.BlockSpec((T, wide), lambda j: (0, j)),
        out_shape=jax.ShapeDtypeStruct((T, DFF), BF16), compiler_params=_params(("parallel",)),
    )(up, up, conv_w, conv_w, conv_b, conv_b)


def _ffn_mid_bwd(name, dact, up, conv_w, conv_b):
    window = CHUNK + 2 * MARGIN
    centre = slice(MARGIN, MARGIN + CHUNK)

    def shifted(v):
        return pltpu.roll(v, 1, 0), pltpu.roll(v, window - 1, 0)

    def fold(v):
        return jnp.sum(v[centre].reshape(CHUNK // 8, 8, TC), axis=0)

    wide = BWD_BLOCKS * TC

    def body(da_ref, xg_ref, xv_ref, wg_ref, wv_ref, bg_ref, bv_ref, dx_ref, dw_ref, db_ref, dap, xgp, xvp):
        for b in range(BWD_BLOCKS):
            block(b, da_ref, xg_ref, xv_ref, wg_ref, wv_ref, bg_ref, bv_ref, dx_ref, dw_ref, db_ref, dap, xgp, xvp)

    def block(b, da_ref, xg_ref, xv_ref, wg_ref, wv_ref, bg_ref, bv_ref, dx_ref, dw_ref, db_ref, dap, xgp, xvp):
        lanes = slice(b * TC, (b + 1) * TC)
        for src, pad in ((da_ref[:, lanes], dap), (xg_ref[b], xgp), (xv_ref[b], xvp)):
            pad[0:MARGIN, :] = jnp.zeros((MARGIN, TC), F32)
            pad[MARGIN:MARGIN + T, :] = src
            pad[MARGIN + T:, :] = jnp.zeros((MARGIN, TC), F32)
        wg, wv, bg, bv = wg_ref[:, lanes], wv_ref[:, lanes], bg_ref[:, lanes], bv_ref[:, lanes]

        def chunk(c, sums):
            r0 = pl.multiple_of(c * CHUNK, CHUNK)
            da, xg, xv = dap[pl.ds(r0, window), :], xgp[pl.ds(r0, window), :], xvp[pl.ds(r0, window), :]
            xg_prev, xg_next = shifted(xg)
            xv_prev, xv_next = shifted(xv)
            ug = xg_prev * wg[0:1] + xg * wg[1:2] + xg_next * wg[2:3] + bg
            uv = xv_prev * wv[0:1] + xv * wv[1:2] + xv_next * wv[2:3] + bv
            sg = jax.nn.sigmoid(ug)
            dug = da * uv * (sg * (1.0 + ug * (1.0 - sg)))
            duv = da * (ug * sg)
            out = []
            for half, (x_prev, x, x_next, w, du) in enumerate(((xg_prev, xg, xg_next, wg, dug),
                                                               (xv_prev, xv, xv_next, wv, duv))):
                du_prev, du_next = shifted(du)
                dx = du_next * w[0:1] + du * w[1:2] + du_prev * w[2:3]
                dx_ref[half, pl.ds(r0, CHUNK), lanes] = dx[centre].astype(BF16)
                out += [fold(x_prev * du), fold(x * du), fold(x_next * du), fold(du)]
            return tuple(s + o for s, o in zip(sums, out))

        sums = lax.fori_loop(0, T // CHUNK, chunk, tuple(jnp.zeros((8, TC), F32) for _ in range(8)))
        rows = [jnp.sum(s, axis=0, keepdims=True) for s in sums]
        for half in range(2):
            dw_ref[half, :, lanes] = jnp.concatenate(rows[4 * half:4 * half + 3], axis=0)
            db_ref[half, :, lanes] = rows[4 * half + 3]

    gate = lambda shape: pl.BlockSpec(shape, lambda j: (0, j))
    val = lambda shape: pl.BlockSpec(shape, lambda j: (0, j + DFF // wide))
    return pl.pallas_call(
        body, name=name, grid=(DFF // wide,),
        in_specs=[gate((T, wide)), pl.BlockSpec((BWD_BLOCKS, T, TC), lambda j: (j, 0, 0)),
                  pl.BlockSpec((BWD_BLOCKS, T, TC), lambda j: (j + DFF // wide, 0, 0)),
                  gate((3, wide)), val((3, wide)), gate((1, wide)), val((1, wide))],
        out_specs=[pl.BlockSpec((2, T, wide), lambda j: (0, 0, j)), pl.BlockSpec((2, 3, wide), lambda j: (0, 0, j)),
                   pl.BlockSpec((2, 1, wide), lambda j: (0, 0, j))],
        out_shape=[jax.ShapeDtypeStruct((2, T, DFF), BF16), jax.ShapeDtypeStruct((2, 3, DFF), F32),
                   jax.ShapeDtypeStruct((2, 1, DFF), F32)],
        scratch_shapes=[pltpu.VMEM((T + 2 * MARGIN, TC), F32)] * 3,
        compiler_params=_params(("parallel",)),
    )(dact, up, up, conv_w, conv_w, conv_b, conv_b)


def _dup_spec(tm, nj):
    per = DFF // nj
    return pl.BlockSpec((None, tm, nj), lambda a, b, j: (j // per, 0 if tm == T else b, j % per))


def _dup_spec_tn(tm, nj):
    per = DFF // nj
    return pl.BlockSpec((None, tm, nj), lambda j, kt, r: (j // per, 0, j % per))


def _adamw_math(w, g, m, v):
    m = ADAM_B1 * m + (1.0 - ADAM_B1) * g
    v = ADAM_B2 * v + (1.0 - ADAM_B2) * (g * g)
    m_hat = m / (1.0 - ADAM_B1 ** ADAM_STEP)
    v_hat = v / (1.0 - ADAM_B2 ** ADAM_STEP)
    delta = -ADAM_LR * (m_hat / (jnp.sqrt(v_hat) + ADAM_EPS) + ADAM_WD * w)
    return delta, m, v


ADAM_BLOCK = 256 * 1408


def _adamw_sharded(name, w, m, v, parts):
    _, r, c = w.shape
    tr = max(t for t in range(16, r + 1, 16) if r % t == 0 and t * c <= ADAM_BLOCK)

    def body(w_ref, m_ref, v_ref, p0_ref, p1_ref, g_ref, d_ref, nm_ref, nv_ref):
        def run(p_ref):
            g = p_ref[0].astype(F32)
            for k in range(1, N_DEV):
                g = g + p_ref[k].astype(F32)
            d, nm, nv = _adamw_math(w_ref[...], g, m_ref[...], v_ref[...])
            g_ref[...] = g
            d_ref[...] = d
            nm_ref[...] = nm
            nv_ref[...] = nv

        @pl.when(pl.program_id(0) == 0)
        def _():
            run(p0_ref)

        @pl.when(pl.program_id(0) == 1)
        def _():
            run(p1_ref)

    ws = pl.BlockSpec((None, tr, c), lambda l, i: (l, i, 0))
    p0 = pl.BlockSpec((N_DEV, tr, c), lambda l, i: (0, jnp.where(l == 0, i, r // tr - 1), 0))
    p1 = pl.BlockSpec((N_DEV, tr, c), lambda l, i: (0, jnp.where(l == 1, i, 0), 0))
    return pl.pallas_call(
        body, name=name, grid=(DEPTH, r // tr), in_specs=[ws, ws, ws, p0, p1], out_specs=[ws] * 4,
        out_shape=[jax.ShapeDtypeStruct(w.shape, F32)] * 4, compiler_params=_params(("arbitrary", "arbitrary")),
    )(w, m, v, *parts)


def _sum_devices(name, parts):
    r = parts.shape[1]

    def body(p_ref, o_ref):
        g = p_ref[0]
        for k in range(1, N_DEV):
            g = g + p_ref[k]
        o_ref[...] = g

    return pl.pallas_call(
        body, name=name, in_specs=[pl.BlockSpec((N_DEV, r, LANE), lambda: (0, 0, 0))],
        out_specs=pl.BlockSpec((r, LANE), lambda: (0, 0)), out_shape=jax.ShapeDtypeStruct((r, LANE), F32),
        compiler_params=_params(),
    )(parts)


def _adamw_small(name, ws, gs, ms, vs):
    n = len(ws)
    shapes = [w.shape for w in ws]
    ws, gs, ms, vs = ([a.reshape(1, -1) if a.ndim == 1 else a for a in arrs] for arrs in (ws, gs, ms, vs))
    specs = [pl.BlockSpec(memory_space=pltpu.VMEM)] * n

    def body(*refs):
        for i in range(n):
            w_ref, g_ref, m_ref, v_ref = (refs[k * n + i] for k in range(4))
            d, nm, nv = _adamw_math(w_ref[...], g_ref[...], m_ref[...], v_ref[...])
            refs[4 * n + i][...] = d
            refs[5 * n + i][...] = nm
            refs[6 * n + i][...] = nv

    outs = pl.pallas_call(
        body, name=name, in_specs=specs * 4, out_specs=specs * 3,
        out_shape=[jax.ShapeDtypeStruct(w.shape, F32) for w in ws] * 3, compiler_params=_params(),
    )(*ws, *gs, *ms, *vs)
    outs = [o.reshape(shapes[i % n]) for i, o in enumerate(outs)]
    return outs[:n], outs[n:2 * n], outs[2 * n:]


def _pack(arrays):
    flat = jnp.concatenate([a.reshape(-1) for a in arrays])
    pad = (-flat.shape[0]) % (8 * LANE)
    return jnp.pad(flat, (0, pad)).reshape(-1, LANE)


def _unpack(buf, shapes):
    flat, out, off = buf.reshape(-1), [], 0
    for s in shapes:
        n = 1
        for d in s:
            n *= d
        out.append(flat[off:off + n].reshape(s))
        off += n
    return out


def _local_step(x, target, small, weights, conv_w_full, hand_over, used):
    cos2, sin2 = _rope_tables()
    bias_a = _dilation_bias()
    tables = [_rpb_tables(f"rpb_tables_{l}", small["rpb_c"][l]) for l in range(DEPTH)]
    saved, carry = [], 0.0
    for l in range(DEPTH):
        g1, g2 = small["ln_attn"][l][None] + carry, small["ln_ffn"][l][None]
        gain, sink, cb = small["mix_gain"][l][None], small["sink_b"][l], small["conv_b"][l][None]
        cw = conv_w_full[l]
        bias = tables[l]
        h1, qkv = _prologue_matmul(f"proj_in_{l}", _rmsnorm_rows, [x, g1], [D, None],
                                   weights("w_in", l, [cos2, sin2, bias_a] + tables if l == 0 else x),
                                   (D, 1024), lambda j: (0, j), 1024, epilogue=_rope_epilogue, extras=(cos2, sin2),
                                   out_dtype=BF16)
        zero = used(f"proj_in_{l}", qkv)
        qb, kb, vb = (qkv[:, BLOCK_OF[n] * LANE:BLOCK_OF[n] * LANE + w] for n, w in (("qb", WB), ("kb", WKV), ("vb", WKV)))
        oa, lse_a = _attn_a_fwd(f"attn_a_{l}", qkv, bias_a)
        ob, lse_b = _attn_b_fwd(f"attn_b_{l}", qb, kb, vb, sink + zero)
        oc, lse_c = _attn_c_fwd(f"attn_c_{l}", qkv, bias)
        mixed, x_mid = _prologue_matmul(f"proj_out_{l}", _mix_rows, [oa, ob, oc, gain + used(f"attn_{l}", oc)],
                                        [WA, WB, WC, None],
                                        weights("w_out", l, oc), (N_DEV, D // N_DEV, 512), lambda j: (0, 0, j), 512,
                                        res=x)
        h2, up = _prologue_matmul(f"ffn_up_{l}", _rmsnorm_rows, [x_mid, g2 + used(f"proj_out_{l}", x_mid)], [D, None],
                                  weights("w_up", l, x_mid), (D, 1024), lambda j: (0, j), 1024, blocked_out=True)
        act = _ffn_mid_fwd(f"ffn_mid_{l}", up, cw, cb + used(f"ffn_up_{l}", up))
        x_out = _nn_rows(f"ffn_down_{l}", act, weights("w_down", l, act), x_mid, 4, 1024, 1024)
        carry = used(f"ffn_down_{l}", x_out)
        saved.append(dict(x=x, h1=h1, qkv=(qkv, qb, kb, vb), o=(oa, ob, oc), lse=(lse_a, lse_b, lse_c), mixed=mixed,
                          x_mid=x_mid, h2=h2, up=up, act=act, g1=g1, g2=g2, gain=gain, sink=sink, cb=cb, cw=cw, bias=bias))
        x = x_out

    loss8, dx, dxb, d_ln_final = _loss_head(x, small["ln_final"][None], target)
    sgrads = [None] * DEPTH
    for l in reversed(range(DEPTH)):
        s = saved[l]
        qkv, qb, kb, vb = s["qkv"]
        oa, ob, oc = s["o"]
        wg_in, wg_out = weights("w_in", l, None), weights("w_out", l, None)
        wg_up, wg_down = weights("w_up", l, None), weights("w_down", l, None)
        g_down = _tn_rows(f"wgrad_down_{l}", s["act"], dxb, wg_down.shape[1], 2, 512)
        zero = hand_over("w_down", l, g_down)
        dact = _nt_rows(f"dgrad_down_{l}", dxb, wg_down, 4, 512)
        dup, d_cw, d_cb = _ffn_mid_bwd(f"ffn_mid_bwd_{l}", dact, s["up"], s["cw"], s["cb"] + zero)
        g_up = _tn_cols(f"wgrad_up_{l}", s["h2"], dup, _dup_spec_tn, 2 * DFF, DFF // 2)
        zero = hand_over("w_up", l, g_up)
        dh2 = _nt_cols(f"dgrad_up_{l}", dup, _dup_spec, wg_up, DFF // 2)
        dx, dxb, d_g2 = _rmsnorm_bwd(f"norm_ffn_bwd_{l}", dh2, s["x_mid"], s["g2"] + zero, dx)
        g_out = _tn_rows(f"wgrad_out_{l}", s["mixed"], dxb, wg_out.shape[1], 2, D)
        zero = hand_over("w_out", l, g_out)
        dmixed = _nt_rows(f"dgrad_out_{l}", dxb, wg_out, 2, T)
        doa, dob, doc, d_gain = _mix_bwd(f"mix_bwd_{l}", dmixed, oa, ob, oc, s["gain"] + zero)
        lse_a, lse_b, lse_c = s["lse"]
        dqa, dka, dva = _attn_a_bwd(f"attn_a_bwd_{l}", qkv, oa, lse_a, doa, bias_a)
        dqb, dkb, dvb, d_sink = _attn_b_bwd(f"attn_b_bwd_{l}", qb, kb, vb, ob, lse_b, dob, s["sink"])
        dqc, dkc, dvc, d_bias = _attn_c_bwd(f"attn_c_bwd_{l}", qkv, oc, lse_c, doc, s["bias"])
        d_rpb = _rpb_reduce(f"rpb_reduce_{l}", d_bias)
        dproj = _rope_bwd(f"rope_bwd_{l}", (dqa, dka, dva, dqb, dkb, dvb, dqc, dkc, dvc), cos2, sin2)
        g_in = _tn_cols(f"wgrad_in_{l}", s["h1"], dproj,
                        lambda tm, tn: pl.BlockSpec((tm, tn), lambda j, kt, r: (0, j)), IN_COLS, 1024)
        zero = hand_over("w_in", l, g_in)
        dh1 = _nt_cols(f"dgrad_in_{l}", dproj, lambda tm, nc: pl.BlockSpec((tm, nc), lambda kt, i, j: (i, j)), wg_in,
                       IN_COLS // 2)
        dx, dxb, d_g1 = _rmsnorm_bwd(f"norm_attn_bwd_{l}", dh1, s["x"], s["g1"] + zero, dx)
        sgrads[l] = dict(ln_attn=d_g1[0], sink_b=d_sink[0, :HB], rpb_c=d_rpb, mix_gain=d_gain[0], ln_ffn=d_g2[0],
                         conv_w=d_cw.transpose(1, 0, 2).reshape(3, 2 * DFF), conv_b=d_cb.reshape(2 * DFF))
    return loss8[0, 0], dx, d_ln_final[0], sgrads


SMALL_NAMES = ("ln_attn", "sink_b", "rpb_c", "mix_gain", "ln_ffn", "conv_b")


def kernel(x, ln_attn, w_in, sink_b, rpb_c, mix_gain, w_out, ln_ffn, w_up, conv_w, conv_b, w_down, ln_final, loss_target, m_ln_attn, m_w_in, m_sink_b, m_rpb_c, m_mix_gain, m_w_out, m_ln_ffn, m_w_up, m_conv_w, m_conv_b, m_w_down, m_ln_final, v_ln_attn, v_w_in, v_sink_b, v_rpb_c, v_mix_gain, v_w_out, v_ln_ffn, v_w_up, v_conv_w, v_conv_b, v_w_down, v_ln_final):
    me = 4 * lax.axis_index("x") + 2 * lax.axis_index("y") + lax.axis_index("c")
    small = dict(ln_attn=ln_attn, sink_b=sink_b, rpb_c=rpb_c, mix_gain=mix_gain, ln_ffn=ln_ffn, conv_b=conv_b,
                 ln_final=ln_final)

    names = ("w_in", "w_out", "w_up", "w_down")
    shards = dict(w_in=w_in, w_out=w_out, w_up=w_up, w_down=w_down)
    order = [(n, l) for l in range(DEPTH) for n in names]
    conv_key = ("conv_w", 0)
    started, arrived, forwarded, gathered = {}, {}, {}, {}

    def side_by_side(k):
        return k[0] in ("w_in", "w_up")

    def slot_of(k):
        return _col_slot(shards[k[0]].shape[2]) if side_by_side(k) else _lead_slot

    def begin(name, ks, zero):
        srcs = [_pack([conv_w]) + zero if k == conv_key else (shards[k[0]][k[1]] + zero).astype(BF16) for k in ks]
        lands = [lax.empty((s.shape[0], N_DEV * s.shape[1]) if side_by_side(k) else (N_DEV,) + s.shape, s.dtype)
                 for k, s in zip(ks, srcs)]
        peers = [ALL_PEERS if k == conv_key else NEAR_PEERS for k in ks]
        send, recv, bufs, tok = _copy_start(name, srcs + lands, _gather_plan(peers, [slot_of(k) for k in ks]),
                                            [len(p) + 1 for p in peers])
        for i, k in enumerate(ks):
            started[k] = (send[i], recv[i], bufs[i], bufs[len(ks) + i], peers[i])
        return tok

    token = begin("gather_start_first", order[:1], 0.0)
    token = begin("gather_start_rest", [conv_key] + order[1:], token[0, 0])

    def arrive(k, after):
        send, recv, src, land, peers = started[k]
        arrived[k] = _copy_wait(f"gather_{k[0]}_{k[1]}_arrived", [src, land], [send], [recv],
                                _gather_plan([peers], [slot_of(k)]), after)

    queue = list(order)

    def advance(after):
        if not queue:
            return 0.0
        k = queue.pop(0)
        arrive(k, after)
        forwarded[k] = _copy_start(f"gather_{k[0]}_{k[1]}_forward", [arrived[k][1]], _forward_plan(slot_of(k)),
                                   [len(OTHER_CHIPS)])
        return forwarded[k][3][0, 0]

    pass_on_behind = ("proj_in_0", "attn_0", "ffn_up_0", "ffn_down_0", "proj_in_1", "attn_1", "ffn_up_1")

    def used(point, result):
        return advance(result) if point in pass_on_behind else 0.0

    def weights(n, l, after):
        k = (n, l)
        if k not in gathered:
            if k not in forwarded:
                advance(after)
            send_b, recv_b, (land,), _ = forwarded[k]
            (gathered[k],) = _copy_wait(f"gather_{n}_{l}_done", [land], send_b, recv_b, _forward_plan(slot_of(k)),
                                        after)
        return gathered[k]

    pending = {}

    def hand_over(n, l, g):
        shard = shards[n].shape[1:]
        send, recv, bufs, tok = _copy_start(f"send_grad_{n}_{l}", [g, lax.empty((N_DEV,) + shard, g.dtype)],
                                            _scatter_plan(slot_of((n, l))), [len(ALL_PEERS) + 1])
        pending[(n, l)] = (send, recv, bufs)
        return tok[0, 0]

    def received(k, after):
        send, recv, bufs = pending[k]
        return _copy_wait(f"recv_grad_{k[0]}_{k[1]}", bufs, send, recv, _scatter_plan(slot_of(k)), after)[1]

    arrive(conv_key, token)
    cw_all = arrived[conv_key][1]
    nup = w_up.shape[2]
    cw_shards = cw_all.reshape(N_DEV, -1)[:, :DEPTH * 3 * nup].reshape(N_DEV, DEPTH, 3, nup)
    conv_w_full = cw_shards.transpose(1, 2, 0, 3).reshape(DEPTH, 3, N_DEV * nup)

    loss_local, dx, d_ln_final, sgrads = _local_step(
        x[0], loss_target[0], dict(small, ln_attn=ln_attn + token[0, 0]), weights, conv_w_full, hand_over, used)

    stacked = [jnp.stack([sgrads[l][n] for l in range(DEPTH)]) for n in SMALL_NAMES + ("conv_w",)] + [d_ln_final]
    shapes = [a.shape for a in stacked]
    mine = _pack(stacked)
    send_s, recv_s, bufs_s, _ = _copy_start("gather_small_grads_start", [mine, lax.empty((N_DEV,) + mine.shape, F32)],
                                            _gather_plan([ALL_PEERS], [_lead_slot]), [len(ALL_PEERS) + 1])

    big, after = {}, dx
    moments = dict(w_in=(m_w_in, v_w_in), w_out=(m_w_out, v_w_out), w_up=(m_w_up, v_w_up), w_down=(m_w_down, v_w_down))
    for n in reversed(names):
        parts = (received((n, 0), after), received((n, 1), after))
        big[n] = _adamw_sharded(f"adamw_{n}", shards[n], *moments[n], parts)
        after = big[n][1]

    _, everyone = _copy_wait("gather_small_grads_done", bufs_s, send_s, recv_s,
                             _gather_plan([ALL_PEERS], [_lead_slot]), after)
    g_small = _unpack(_sum_devices("sum_small_grads", everyone), shapes)
    g = dict(zip(SMALL_NAMES + ("conv_w", "ln_final"), g_small))
    g["conv_w"] = lax.dynamic_slice_in_dim(g["conv_w"], me * nup, nup, axis=2)

    snames = SMALL_NAMES + ("conv_w", "ln_final")
    sw = dict(small, conv_w=conv_w)
    sm = dict(ln_attn=m_ln_attn, sink_b=m_sink_b, rpb_c=m_rpb_c, mix_gain=m_mix_gain, ln_ffn=m_ln_ffn,
              conv_b=m_conv_b, conv_w=m_conv_w, ln_final=m_ln_final)
    sv = dict(ln_attn=v_ln_attn, sink_b=v_sink_b, rpb_c=v_rpb_c, mix_gain=v_mix_gain, ln_ffn=v_ln_ffn,
              conv_b=v_conv_b, conv_w=v_conv_w, ln_final=v_ln_final)
    s_delta, s_m, s_v = (dict(zip(snames, out)) for out in _adamw_small(
        "adamw_small", [sw[n] for n in snames], [g[n] for n in snames], [sm[n] for n in snames],
        [sv[n] for n in snames]))

    loss = lax.psum(loss_local, ("x", "y", "c"))
    outputs = ("ln_attn", "w_in", "sink_b", "rpb_c", "mix_gain", "w_out", "ln_ffn", "w_up", "conv_w", "conv_b",
               "w_down", "ln_final")
    grads = [big[n][0] if n in big else g[n] for n in outputs]
    deltas = [big[n][1] if n in big else s_delta[n] for n in outputs]
    new_m = [big[n][2] if n in big else s_m[n] for n in outputs]
    new_v = [big[n][3] if n in big else s_v[n] for n in outputs]
    return (loss, dx[None], *grads, *deltas, *new_m, *new_v)
```

```python
import functools

import jax
import jax.numpy as jnp
from jax import lax
from jax.experimental import pallas as pl
from jax.experimental.pallas import tpu as pltpu

F32 = jnp.float32
BF16 = jnp.bfloat16

N_DEV = 8
T = 2048
D = 2048
DEPTH = 2
HD = 64
HA, HB, HKV, HC = 12, 10, 2, 10
WA, WB, WKV, WC = HA * HD, HB * HD, HKV * HD, HC * HD
IN_COLS = 3 * WA + WB + 2 * WKV + 3 * WC
DFF = 5632
GRID_W = 64
ROWS = T // GRID_W
NA_ROWS, NA_COLS = 8, 16
WINDOW_B = 128
EPS = 1e-6
NEG = -1e30
ROPE_THETA = 10000.0
LANE = 128
VMEM_LIMIT = 56 * 1024 * 1024

ADAM_LR, ADAM_B1, ADAM_B2, ADAM_EPS, ADAM_WD, ADAM_STEP = 0.001, 0.9, 0.999, 1e-08, 0.01, 10

GROUPS = (("qa", WA, True, True), ("ka", WA, True, False), ("va", WA, False, False),
          ("qb", WB, True, True), ("kb", WKV, True, False), ("vb", WKV, False, False),
          ("qc", WC, False, True), ("kc", WC, False, False), ("vc", WC, False, False))


def _params(sem=None):
    return pltpu.CompilerParams(dimension_semantics=sem, vmem_limit_bytes=VMEM_LIMIT)


HBM_SPEC = pl.BlockSpec(memory_space=pltpu.HBM)
SEM_SPEC = pl.BlockSpec(memory_space=pltpu.SEMAPHORE)
DATAFLOW = pltpu.SideEffectType.DATAFLOW_SIDE_EFFECTING


ALL_PEERS = tuple((p >> 2 & 1, p >> 1 & 1, p & 1) for p in range(1, N_DEV))
OTHER_CHIPS = ((1, 0, 0), (0, 1, 0), (1, 1, 0))
NEAR_PEERS = ((0, 0, 1),) + OTHER_CHIPS


def _flip(x, y, c, f):
    return (1 - x if f[0] else x, 1 - y if f[1] else y, 1 - c if f[2] else c)


def _index(pos):
    return 4 * pos[0] + 2 * pos[1] + pos[2]


class _LocalCopy:
    def __init__(self, src, dst, sem):
        self.copy = pltpu.make_async_copy(src, dst, sem)

    def start(self):
        self.copy.start()

    def wait_send(self):
        self.copy.wait()

    def wait_recv(self):
        pass


def _descriptors(plan, bufs, send_sems, recv_sems):
    x, y, c = lax.axis_index("x"), lax.axis_index("y"), lax.axis_index("c")
    return [_LocalCopy(src, dst, send_sems[g].at[i]) if partner is None else
            pltpu.make_async_remote_copy(src_ref=src, dst_ref=dst, send_sem=send_sems[g].at[i],
                                         recv_sem=recv_sems[g].at[i], device_id=partner,
                                         device_id_type=pl.DeviceIdType.MESH)
            for g, copies in enumerate(plan(bufs, x, y, c)) for i, (src, dst, partner) in enumerate(copies)]


def _copy_start(name, bufs, plan, sizes):
    nb, ng = len(bufs), len(sizes)

    def body(*refs):
        for d in _descriptors(plan, refs[:nb], refs[nb:nb + ng], refs[nb + ng:nb + 2 * ng]):
            d.start()
        refs[2 * nb + 2 * ng][...] = jnp.zeros((8, LANE), F32)

    outs = pl.pallas_call(
        body, name=name,
        out_shape=[pltpu.SemaphoreType.DMA((s,)) for s in sizes] * 2 + [pltpu.HBM(b.shape, b.dtype) for b in bufs]
        + [jax.ShapeDtypeStruct((8, LANE), F32)],
        in_specs=[HBM_SPEC] * nb,
        out_specs=[SEM_SPEC] * (2 * ng) + [HBM_SPEC] * nb + [pl.BlockSpec(memory_space=pltpu.VMEM)],
        input_output_aliases={i: 2 * ng + i for i in range(nb)},
        compiler_params=pltpu.CompilerParams(has_side_effects=DATAFLOW),
    )(*[pltpu.with_memory_space_constraint(b, pltpu.HBM) for b in bufs])
    return outs[:ng], outs[ng:2 * ng], outs[2 * ng:2 * ng + nb], outs[2 * ng + nb]


def _copy_wait(name, bufs, send_sems, recv_sems, plan, after):
    nb, ng = len(bufs), len(send_sems)
    after = list(after) if isinstance(after, (list, tuple)) else [after]

    def body(*refs):
        for d in _descriptors(plan, refs[:nb], refs[nb:nb + ng], refs[nb + ng:nb + 2 * ng]):
            d.wait_send()
            d.wait_recv()

    return pl.pallas_call(
        body, name=name, out_shape=[pltpu.HBM(b.shape, b.dtype) for b in bufs],
        in_specs=[HBM_SPEC] * nb + [SEM_SPEC] * (2 * ng) + [pl.BlockSpec(memory_space=pl.ANY)] * len(after),
        out_specs=[HBM_SPEC] * nb, input_output_aliases={i: i for i in range(nb)},
        compiler_params=pltpu.CompilerParams(has_side_effects=DATAFLOW),
    )(*bufs, *send_sems, *recv_sems, *after)


def _lead_slot(ref, k):
    return ref.at[k]


def _col_slot(width):
    return lambda ref, k: ref.at[:, pl.ds(pl.multiple_of(k * width, LANE), width)]


def _gather_plan(peer_sets, slots):
    def plan(bufs, x, y, c):
        n = len(peer_sets)
        return [[(bufs[i], slots[i](bufs[n + i], _index((x, y, c))), _flip(x, y, c, f)) for f in peers]
                + [(bufs[i], slots[i](bufs[n + i], _index((x, y, c))), None)] for i, peers in enumerate(peer_sets)]
    return plan


def _forward_plan(slot):
    def plan(bufs, x, y, c):
        pieces = [slot(bufs[0], _index(_flip(x, y, c, f))) for f in OTHER_CHIPS]
        return [[(p, p, _flip(x, y, c, (0, 0, 1))) for p in pieces]]
    return plan


def _scatter_plan(slot):
    def plan(bufs, x, y, c):
        me = _index((x, y, c))
        peers = [_flip(x, y, c, f) for f in ALL_PEERS]
        return [[(slot(bufs[0], _index(p)), bufs[1].at[me], p) for p in peers]
                + [(slot(bufs[0], me), bufs[1].at[me], None)]]
    return plan


def _flat2(v):
    return v.reshape(-1, v.shape[-1])


def _matmul(name, kind, a, a_spec, b, b_spec, out_shape, out_spec, grid, res=None, res_spec=None, acc_shape=None):
    dims = {"nn": (((1,), (0,)), ((), ())), "nt": NT_DIMS, "nts": NT_DIMS, "tn": (((0,), (0,)), ((), ()))}[kind]
    nred = grid[-1]

    def body(*refs):
        if res is None:
            a_ref, b_ref, o_ref = refs[:3]
            r_ref = None
        else:
            a_ref, b_ref, r_ref, o_ref = refs[:4]
        if kind == "nts":
            n = b_ref.shape[-1]
            part = sum(lax.dot_general(a_ref[:, blk * n:(blk + 1) * n], b_ref[blk], dims, preferred_element_type=F32)
                       for blk in range(b_ref.shape[0]))
        else:
            part = lax.dot_general(_flat2(a_ref[...]), _flat2(b_ref[...]), dims, preferred_element_type=F32)

        def finish(total):
            if r_ref is not None:
                total = total + r_ref[...]
            o_ref[...] = total.reshape(o_ref.shape).astype(o_ref.dtype)

        if nred == 1:
            finish(part)
        else:
            acc_ref = refs[-1]
            k = pl.program_id(len(grid) - 1)

            @pl.when(k == 0)
            def _():
                acc_ref[...] = part

            @pl.when(jnp.logical_and(k > 0, k < nred - 1))
            def _():
                acc_ref[...] += part

            @pl.when(k == nred - 1)
            def _():
                finish(acc_ref[...] + part)

    ins, specs = [a, b], [a_spec, b_spec]
    if res is not None:
        ins.append(res)
        specs.append(res_spec)
    scratch = [] if nred == 1 else [pltpu.VMEM(acc_shape, F32)]
    return pl.pallas_call(
        body, name=name, grid=grid, in_specs=specs, out_specs=out_spec, out_shape=out_shape, scratch_shapes=scratch,
        compiler_params=_params(("parallel",) * (len(grid) - 1) + ("arbitrary",)),
    )(*ins)


def _nn_rows(name, a, wg, res, s, tn, tm):
    _, kj, n = wg.shape
    return _matmul(
        name, "nn", a, pl.BlockSpec((tm, s * kj), lambda j, i, r: (i, r)),
        wg, pl.BlockSpec((s, kj, tn), lambda j, i, r: (r, 0, j)),
        jax.ShapeDtypeStruct((T, n), F32), pl.BlockSpec((tm, tn), lambda j, i, r: (i, j)),
        (n // tn, T // tm, N_DEV // s), res=res, res_spec=pl.BlockSpec((tm, tn), lambda j, i, r: (i, j)),
        acc_shape=(tm, tn))


def _nt_cols(name, dc, dc_spec_of, w, nc):
    k, n = w.shape
    tm = tk = 1024
    return _matmul(
        name, "nt", dc, dc_spec_of(tm, nc),
        w, pl.BlockSpec((tk, nc), lambda kt, i, j: (kt, j)),
        jax.ShapeDtypeStruct((T, k), F32), pl.BlockSpec((tm, tk), lambda kt, i, j: (i, kt)),
        (k // tk, T // tm, n // nc), acc_shape=(tm, tk))


def _nt_rows(name, dc, wg, s, tm):
    _, kj, n = wg.shape
    return _matmul(
        name, "nt", dc, pl.BlockSpec((tm, n), lambda kt, i, r: (i, 0)),
        wg, pl.BlockSpec((s, kj, n), lambda kt, i, r: (kt, 0, 0)),
        jax.ShapeDtypeStruct((T, N_DEV * kj), F32), pl.BlockSpec((tm, s * kj), lambda kt, i, r: (i, kt)),
        (N_DEV // s, T // tm, 1))


def _tn_cols(name, a, dc, dc_spec_of, n, tn):
    k = a.shape[1]
    tk = 512
    return _matmul(
        name, "tn", a, pl.BlockSpec((T, tk), lambda j, kt, r: (0, kt)),
        dc, dc_spec_of(T, tn),
        jax.ShapeDtypeStruct((k, n), BF16), pl.BlockSpec((tk, tn), lambda j, kt, r: (kt, j)),
        (n // tn, k // tk, 1))


def _tn_rows(name, a, dc, kj, s, tn):
    n = dc.shape[1]
    return _matmul(
        name, "tn", a, pl.BlockSpec((T, s * kj), lambda kt, j, r: (0, kt)),
        dc, pl.BlockSpec((T, tn), lambda kt, j, r: (0, j)),
        jax.ShapeDtypeStruct((N_DEV, kj, n), BF16), pl.BlockSpec((s, kj, tn), lambda kt, j, r: (kt, 0, j)),
        (N_DEV // s, n // tn, 1))


TR = 512


def _rows(width):
    return pl.BlockSpec((TR, width), lambda i: (i, 0))


def _whole(shape):
    return pl.BlockSpec(shape, lambda i: (0,) * len(shape))


def _rmsnorm_rows(x_ref, g_ref):
    xv = x_ref[...]
    r = lax.rsqrt(jnp.mean(xv * xv, axis=-1, keepdims=True) + EPS)
    return ((xv * r) * g_ref[...]).astype(BF16)


SUB = 256


def _prologue_matmul(name, prologue, ins, widths, w, w_block, w_index, tn, res=None, epilogue=None, extras=(),
                     out_dtype=F32, blocked_out=False):
    tm = 1024
    n = w.shape[-1]
    ni = len(ins)

    def body(*refs):
        w_ref = refs[ni]
        r_ref = refs[ni + 1] if res is not None else None
        x_refs = refs[ni + 1 + (res is not None):len(refs) - 3]
        h_ref, o_ref, h_scr = refs[-3:]

        @pl.when(pl.program_id(1) == 0)
        def _():
            h = prologue(*refs[:ni])
            h_scr[...] = h
            h_ref[...] = h

        for sub in range(tn // SUB):
            cols = slice(sub * SUB, (sub + 1) * SUB)
            w_cols = w_ref[(slice(None),) * (len(w_ref.shape) - 1) + (cols,)]
            part = jnp.dot(h_scr[...], _flat2(w_cols), preferred_element_type=F32)
            if r_ref is not None:
                part = part + r_ref[:, cols]
            if epilogue is not None:
                part = epilogue(pl.program_id(1) * (tn // SUB) + sub, part, *x_refs)
            if blocked_out:
                for b in range(SUB // LANE):
                    o_ref[sub * (SUB // LANE) + b] = part[:, b * LANE:(b + 1) * LANE].astype(out_dtype)
            else:
                o_ref[:, cols] = part.astype(out_dtype)

    tile = pl.BlockSpec((tm, tn), lambda i, j: (i, j))
    out_tile = pl.BlockSpec((tn // LANE, tm, LANE), lambda i, j: (j, i, 0)) if blocked_out else tile
    out_full = (n // LANE, T, LANE) if blocked_out else (T, n)
    specs = [pl.BlockSpec((1, D), lambda i, j: (0, 0)) if wd is None else pl.BlockSpec((tm, wd), lambda i, j: (i, 0))
             for wd in widths]
    specs.append(pl.BlockSpec(w_block, lambda i, j: w_index(j)))
    operands = list(ins) + [w]
    if res is not None:
        specs.append(tile)
        operands.append(res)
    specs += [pl.BlockSpec((tm, LANE), lambda i, j: (i, 0))] * len(extras)
    operands += list(extras)
    return pl.pallas_call(
        body, name=name, grid=(T // tm, n // tn), in_specs=specs,
        out_specs=[pl.BlockSpec((tm, D), lambda i, j: (i, 0)), out_tile],
        out_shape=[jax.ShapeDtypeStruct((T, D), BF16), jax.ShapeDtypeStruct(out_full, out_dtype)],
        scratch_shapes=[pltpu.VMEM((tm, D), BF16)], compiler_params=_params(("parallel", "arbitrary")),
    )(*operands)


def _rms_bwd_math(dy, xv, g):
    r = lax.rsqrt(jnp.mean(xv * xv, axis=-1, keepdims=True) + EPS)
    xhat = xv * r
    dxhat = dy * g
    dx = r * (dxhat - xhat * jnp.mean(dxhat * xhat, axis=-1, keepdims=True))
    return dx, dy * xhat


def _accumulate(ref, val):
    @pl.when(pl.program_id(0) == 0)
    def _():
        ref[...] = val

    @pl.when(pl.program_id(0) > 0)
    def _():
        ref[...] += val


def _rmsnorm_bwd(name, dy, x, g, res):
    def body(dy_ref, x_ref, g_ref, res_ref, dx_ref, dxb_ref, dg_ref):
        dx, dgr = _rms_bwd_math(dy_ref[...], x_ref[...], g_ref[...])
        tot = res_ref[...] + dx
        dx_ref[...] = tot
        dxb_ref[...] = tot.astype(BF16)
        _accumulate(dg_ref, jnp.sum(dgr, axis=0, keepdims=True))

    return pl.pallas_call(
        body, name=name, grid=(T // TR,), in_specs=[_rows(D), _rows(D), _whole((1, D)), _rows(D)],
        out_specs=[_rows(D), _rows(D), _whole((1, D))],
        out_shape=[jax.ShapeDtypeStruct((T, D), F32), jax.ShapeDtypeStruct((T, D), BF16),
                   jax.ShapeDtypeStruct((1, D), F32)],
        compiler_params=_params(("arbitrary",)),
    )(dy, x, g, res)


def _loss_head(x, g, target):
    def body(x_ref, g_ref, t_ref, loss_ref, dx_ref, dxb_ref, dg_ref):
        xv, gv = x_ref[...], g_ref[...]
        r = lax.rsqrt(jnp.mean(xv * xv, axis=-1, keepdims=True) + EPS)
        err = (xv * r) * gv - t_ref[...]
        part = 0.5 * jnp.sum(jnp.mean(err * err, axis=-1, keepdims=True))
        dx, dgr = _rms_bwd_math(err * (1.0 / D), xv, gv)
        dx_ref[...] = dx
        dxb_ref[...] = dx.astype(BF16)
        _accumulate(dg_ref, jnp.sum(dgr, axis=0, keepdims=True))
        _accumulate(loss_ref, jnp.full((8, LANE), part, F32))

    return pl.pallas_call(
        body, name="loss_head", grid=(T // TR,), in_specs=[_rows(D), _whole((1, D)), _rows(D)],
        out_specs=[_whole((8, LANE)), _rows(D), _rows(D), _whole((1, D))],
        out_shape=[jax.ShapeDtypeStruct((8, LANE), F32), jax.ShapeDtypeStruct((T, D), F32),
                   jax.ShapeDtypeStruct((T, D), BF16), jax.ShapeDtypeStruct((1, D), F32)],
        compiler_params=_params(("arbitrary",)),
    )(x, g, target)


MIX_OFFS = ((0, WA), (WA, WB), (WA + WB, WC))


def _mix_rows(oa_ref, ob_ref, oc_ref, g_ref):
    parts = []
    for ref, (off, w) in zip((oa_ref, ob_ref, oc_ref), MIX_OFFS):
        o = ref[...]
        r = lax.rsqrt(jnp.mean(o * o, axis=-1, keepdims=True) + EPS)
        parts.append(((o * r) * g_ref[:, off:off + w]).astype(BF16))
    return jnp.concatenate(parts, axis=1)


def _mix_bwd(name, dmixed, oa, ob, oc, gain):
    def body(dm_ref, oa_ref, ob_ref, oc_ref, g_ref, doa_ref, dob_ref, doc_ref, dg_ref):
        dgs = []
        for ref, dref, (off, w) in zip((oa_ref, ob_ref, oc_ref), (doa_ref, dob_ref, doc_ref), MIX_OFFS):
            dx, dgr = _rms_bwd_math(dm_ref[:, off:off + w], ref[...], g_ref[:, off:off + w])
            dref[...] = dx
            dgs.append(jnp.sum(dgr, axis=0, keepdims=True))
        _accumulate(dg_ref, jnp.concatenate(dgs, axis=1))

    return pl.pallas_call(
        body, name=name, grid=(T // TR,),
        in_specs=[_rows(D), _rows(WA), _rows(WB), _rows(WC), _whole((1, D))],
        out_specs=[_rows(WA), _rows(WB), _rows(WC), _whole((1, D))],
        out_shape=[jax.ShapeDtypeStruct((T, WA), F32), jax.ShapeDtypeStruct((T, WB), F32),
                   jax.ShapeDtypeStruct((T, WC), F32), jax.ShapeDtypeStruct((1, D), F32)],
        compiler_params=_params(("arbitrary",)),
    )(dmixed, oa, ob, oc, gain)


def _rope_tables():
    inv_freq = ROPE_THETA ** (-jnp.arange(0, HD, 2, dtype=F32) / HD)
    ang = jnp.arange(T, dtype=F32)[:, None] * inv_freq[None, :]
    cos, sin = jnp.cos(ang), jnp.sin(ang)
    cos2 = jnp.tile(jnp.concatenate([cos, cos], axis=1), (1, LANE // HD))
    sin2 = jnp.tile(jnp.concatenate([-sin, sin], axis=1), (1, LANE // HD))
    return cos2, sin2


def _rot_half(v):
    lane = lax.broadcasted_iota(jnp.int32, v.shape, 1)
    return jnp.where(lane % HD < HD // 2, pltpu.roll(v, LANE - HD // 2, 1), pltpu.roll(v, HD // 2, 1))


BLOCK_KINDS = tuple((rot, is_q) for _, w, rot, is_q in GROUPS for _ in range(w // LANE))
BLOCK_OF = {name: sum(w for _, w, _, _ in GROUPS[:g]) // LANE for g, (name, _, _, _) in enumerate(GROUPS)}


def _any_tile(j, tiles):
    return functools.reduce(jnp.logical_or, [j == t for t in tiles]) if tiles else False


def _rope_epilogue(j, tile, c_ref, s_ref):
    cv, sv = c_ref[...], s_ref[...]
    per, n_tiles = tile.shape[1] // LANE, IN_COLS // tile.shape[1]
    out = []
    for b in range(per):
        v = tile[:, b * LANE:(b + 1) * LANE]
        rot = _any_tile(j, [t for t in range(n_tiles) if BLOCK_KINDS[t * per + b][0]])
        is_q = _any_tile(j, [t for t in range(n_tiles) if BLOCK_KINDS[t * per + b][1]])
        if rot is not False:
            v = jnp.where(rot, v * cv + _rot_half(v) * sv, v)
        if is_q is not False:
            v = v * jnp.where(is_q, HD ** -0.5, 1.0)
        out.append(v)
    return jnp.concatenate(out, axis=1)


def _rope_bwd(name, grads, cos2, sin2):
    def body(*refs):
        ins, (c_ref, s_ref, o_ref) = refs[:9], refs[9:]
        cv, sv = c_ref[...], s_ref[...]
        off = 0
        for d_ref, (_, w, rot, is_q) in zip(ins, GROUPS):
            for b in range(w // LANE):
                v = d_ref[:, b * LANE:(b + 1) * LANE]
                if is_q:
                    v = v * (HD ** -0.5)
                if rot:
                    v = v * cv + _rot_half(v * sv)
                o_ref[:, off + b * LANE:off + (b + 1) * LANE] = v.astype(BF16)
            off += w

    return pl.pallas_call(
        body, name=name, grid=(T // TR,), in_specs=[_rows(w) for _, w, _, _ in GROUPS] + [_rows(LANE), _rows(LANE)],
        out_specs=_rows(IN_COLS), out_shape=jax.ShapeDtypeStruct((T, IN_COLS), BF16),
        compiler_params=_params(("parallel",)),
    )(*grads, cos2, sin2)


NT_DIMS = (((1,), (1,)), ((), ()))
TN_DIMS = (((0,), (0,)), ((), ()))


def _scores(q, k, bias, valid):
    s = lax.dot_general(q, k, NT_DIMS, preferred_element_type=F32)
    if bias is not None:
        s = s + bias
    if valid is not None:
        s = jnp.where(valid, s, NEG)
    return s


def _heads_fwd(heads):
    scores = [_scores(h["q"], h["k"], h.get("bias"), h.get("valid")) for h in heads]
    soft = []
    for s, h in zip(scores, heads):
        m = jnp.max(s, axis=1, keepdims=True)
        e = jnp.exp(s - m)
        l = jnp.sum(e, axis=1, keepdims=True)
        if h.get("sink") is not None:
            l = l + jnp.exp(h["sink"] - m)
        soft.append((e.astype(BF16), l, m + jnp.log(l)))
    return [(jnp.dot(e, h["v"], preferred_element_type=F32) / l, lse) for (e, l, lse), h in zip(soft, heads)]


def _heads_bwd(heads):
    dobs = [h["do"].astype(BF16) for h in heads]
    scores = [_scores(h["q"], h["k"], h.get("bias"), h.get("valid")) for h in heads]
    dps = [lax.dot_general(dob, h["v"], NT_DIMS, preferred_element_type=F32) for dob, h in zip(dobs, heads)]
    mid = []
    for s, dp, h in zip(scores, dps, heads):
        p = jnp.exp(s - h["lse"])
        delta = jnp.sum(h["do"] * h["o"], axis=1, keepdims=True)
        ds = p * (dp - delta)
        dsink = None if h.get("sink") is None else -jnp.exp(h["sink"] - h["lse"]) * delta
        mid.append((p.astype(BF16), ds, dsink))
    out = []
    for (pb, ds, dsink), dob, h in zip(mid, dobs, heads):
        dsb = ds.astype(BF16)
        out.append((jnp.dot(dsb, h["k"], preferred_element_type=F32),
                    lax.dot_general(dsb, h["q"], TN_DIMS, preferred_element_type=F32),
                    lax.dot_general(pb, dob, TN_DIMS, preferred_element_type=F32), ds, dsink))
    return out


def _per_head(cols):
    return jnp.concatenate([jnp.broadcast_to(c, (c.shape[0], HD)) for c in cols], axis=1)


DILATIONS = ((128, 1), (512, 4), (2048, 16))


BQ_A = 256
REACH_A = max(window // 2 for window, _ in DILATIONS)


def _first_key(i):
    return jnp.maximum(i * BQ_A - REACH_A, 0)


def _key_window_groups():
    groups = {}
    for i in range(T // BQ_A):
        width = min(T, (i + 1) * BQ_A + REACH_A) - max(i * BQ_A - REACH_A, 0)
        groups.setdefault(width, []).append(i)
    return groups


def _per_window(i, fn):
    for width, tiles in _key_window_groups().items():
        hit = functools.reduce(jnp.logical_or, [i == t for t in tiles])
        pl.when(hit)(functools.partial(fn, pl.multiple_of(_first_key(i), BQ_A), width))


def _dilation_bias():
    def body(o_ref):
        i = pl.program_id(0)
        t = i * BQ_A + lax.broadcasted_iota(jnp.int32, (BQ_A, T), 0)
        ad = jnp.abs(t - (_first_key(i) + lax.broadcasted_iota(jnp.int32, (BQ_A, T), 1)))
        count = jnp.zeros((BQ_A, T), jnp.int32)
        for window, r in DILATIONS:
            count += jnp.where(((ad & (r - 1)) == 0) & (ad <= window // 2), 1, 0)
        logs = jnp.where(count == 2, jnp.log(2.0), jnp.where(count == 3, jnp.log(3.0), 0.0)).astype(F32)
        o_ref[...] = jnp.where(count == 0, NEG, logs)

    return pl.pallas_call(
        body, name="dilation_bias", grid=(T // BQ_A,), out_specs=pl.BlockSpec((BQ_A, T), lambda i: (i, 0)),
        out_shape=jax.ShapeDtypeStruct((T, T), F32), compiler_params=_params(("parallel",)),
    )()


def _qkv_rows(rows, group):
    return pl.BlockSpec((rows, LANE), lambda p, i: (i, BLOCK_OF[group] + p))


def _qkv_all(group):
    return pl.BlockSpec((T, LANE), lambda p, i: (0, BLOCK_OF[group] + p))


def _attn_a_fwd(name, qkv, bias):
    def body(q_ref, k_ref, v_ref, b_ref, o_ref, lse_ref):
        def tile(first, width):
            b = b_ref[:, :width]
            outs = _heads_fwd([dict(q=q_ref[:, h * HD:(h + 1) * HD], k=k_ref[pl.ds(first, width), h * HD:(h + 1) * HD],
                                    v=v_ref[pl.ds(first, width), h * HD:(h + 1) * HD], bias=b) for h in range(2)])
            o_ref[...] = jnp.concatenate([o for o, _ in outs], axis=1)
            lse_ref[...] = _per_head([lse for _, lse in outs])

        _per_window(pl.program_id(1), tile)

    qs = pl.BlockSpec((BQ_A, LANE), lambda p, i: (i, p))
    ks = pl.BlockSpec((T, LANE), lambda p, i: (0, p))
    return pl.pallas_call(
        body, name=name, grid=(HA // 2, T // BQ_A),
        in_specs=[_qkv_rows(BQ_A, "qa"), _qkv_all("ka"), _qkv_all("va"), pl.BlockSpec((BQ_A, T), lambda p, i: (i, 0))],
        out_specs=[qs, qs],
        out_shape=[jax.ShapeDtypeStruct((T, WA), F32)] * 2, compiler_params=_params(("parallel", "parallel")),
    )(qkv, qkv, qkv, bias)


def _attn_a_bwd(name, qkv, oa, lse, doa, bias):
    def body(q_ref, k_ref, v_ref, o_ref, lse_ref, do_ref, b_ref, dq_ref, dk_ref, dv_ref):
        @pl.when(pl.program_id(1) == 0)
        def _():
            dk_ref[...] = jnp.zeros_like(dk_ref)
            dv_ref[...] = jnp.zeros_like(dv_ref)

        def tile(first, width):
            b = b_ref[:, :width]
            keys = pl.ds(first, width)
            sls = [slice(h * HD, (h + 1) * HD) for h in range(2)]
            res = _heads_bwd([dict(q=q_ref[:, sl], k=k_ref[keys, sl], v=v_ref[keys, sl], o=o_ref[:, sl],
                                   do=do_ref[:, sl], lse=lse_ref[:, sl.start:sl.start + 1], bias=b) for sl in sls])
            dq_ref[...] = jnp.concatenate([r[0] for r in res], axis=1)
            dk_ref[keys, :] += jnp.concatenate([r[1] for r in res], axis=1)
            dv_ref[keys, :] += jnp.concatenate([r[2] for r in res], axis=1)

        _per_window(pl.program_id(1), tile)

    qs = pl.BlockSpec((BQ_A, LANE), lambda p, i: (i, p))
    ks = pl.BlockSpec((T, LANE), lambda p, i: (0, p))
    return pl.pallas_call(
        body, name=name, grid=(HA // 2, T // BQ_A),
        in_specs=[_qkv_rows(BQ_A, "qa"), _qkv_all("ka"), _qkv_all("va"), qs, qs, qs,
                  pl.BlockSpec((BQ_A, T), lambda p, i: (i, 0))], out_specs=[qs, ks, ks],
        out_shape=[jax.ShapeDtypeStruct((T, WA), F32)] * 3, compiler_params=_params(("parallel", "arbitrary")),
    )(qkv, qkv, qkv, oa, lse, doa, bias)


BQ_B = 128
SPAN_B = BQ_B + 2 * WINDOW_B


def _window_b(i):
    start = pl.multiple_of(jnp.clip(i * BQ_B - WINDOW_B, 0, T - SPAN_B), BQ_B)
    qpos = i * BQ_B + lax.broadcasted_iota(jnp.int32, (BQ_B, SPAN_B), 0)
    kpos = start + lax.broadcasted_iota(jnp.int32, (BQ_B, SPAN_B), 1)
    return start, jnp.abs(qpos - kpos) <= WINDOW_B


GROUP_B = HB // HKV


def _stack_group(ref, g):
    return jnp.concatenate([ref[:, h * HD:(h + 1) * HD] for h in range(g * GROUP_B, (g + 1) * GROUP_B)], axis=0)


def _sink_column(sink_ref, g):
    return jnp.concatenate([jnp.full((BQ_B, 1), sink_ref[h], F32) for h in range(g * GROUP_B, (g + 1) * GROUP_B)],
                           axis=0)


def _unstack(stacked):
    return [s[j * BQ_B:(j + 1) * BQ_B] for s in stacked for j in range(GROUP_B)]


def _attn_b_fwd(name, qb, kb, vb, sink):
    def body(sink_ref, q_ref, k_ref, v_ref, o_ref, lse_ref):
        start, valid = _window_b(pl.program_id(0))
        valid = jnp.concatenate([valid] * GROUP_B, axis=0)
        kw, vw = k_ref[pl.ds(start, SPAN_B), :], v_ref[pl.ds(start, SPAN_B), :]
        outs = _heads_fwd([dict(q=_stack_group(q_ref, g), k=kw[:, g * HD:(g + 1) * HD], v=vw[:, g * HD:(g + 1) * HD],
                                valid=valid, sink=_sink_column(sink_ref, g)) for g in range(HKV)])
        o_ref[...] = jnp.concatenate(_unstack([o for o, _ in outs]), axis=1)
        lse_ref[...] = _per_head(_unstack([lse for _, lse in outs]))

    qs = pl.BlockSpec((BQ_B, WB), lambda i: (i, 0))
    return pl.pallas_call(
        body, name=name, grid=(T // BQ_B,),
        in_specs=[pl.BlockSpec(memory_space=pltpu.SMEM), qs, _whole((T, WKV)), _whole((T, WKV))],
        out_specs=[qs, qs],
        out_shape=[jax.ShapeDtypeStruct((T, WB), F32)] * 2, compiler_params=_params(("parallel",)),
    )(sink, qb, kb, vb)


def _attn_b_bwd(name, qb, kb, vb, ob, lse, dob, sink):
    def body(sink_ref, q_ref, k_ref, v_ref, o_ref, lse_ref, do_ref, dq_ref, dk_ref, dv_ref, dsink_ref):
        i = pl.program_id(0)
        start, valid = _window_b(i)
        valid = jnp.concatenate([valid] * GROUP_B, axis=0)
        kw, vw = k_ref[pl.ds(start, SPAN_B), :], v_ref[pl.ds(start, SPAN_B), :]
        res = _heads_bwd([dict(q=_stack_group(q_ref, g), k=kw[:, g * HD:(g + 1) * HD], v=vw[:, g * HD:(g + 1) * HD],
                               o=_stack_group(o_ref, g), do=_stack_group(do_ref, g),
                               lse=jnp.concatenate([lse_ref[:, h * HD:h * HD + 1]
                                                    for h in range(g * GROUP_B, (g + 1) * GROUP_B)], axis=0),
                               valid=valid, sink=_sink_column(sink_ref, g)) for g in range(HKV)])
        dks, dvs = [r[1] for r in res], [r[2] for r in res]
        lane = lax.broadcasted_iota(jnp.int32, (1, LANE), 1)
        dsink = jnp.zeros((1, LANE), F32)
        for h, rows in enumerate(_unstack([r[4] for r in res])):
            dsink += jnp.where(lane == h, jnp.sum(rows), 0.0)
        dq_ref[...] = jnp.concatenate(_unstack([r[0] for r in res]), axis=1)

        @pl.when(i == 0)
        def _():
            dk_ref[...] = jnp.zeros_like(dk_ref)
            dv_ref[...] = jnp.zeros_like(dv_ref)
            dsink_ref[...] = jnp.zeros_like(dsink_ref)

        dk_ref[pl.ds(start, SPAN_B), :] += jnp.concatenate(dks, axis=1)
        dv_ref[pl.ds(start, SPAN_B), :] += jnp.concatenate(dvs, axis=1)
        dsink_ref[...] += dsink

    qs = pl.BlockSpec((BQ_B, WB), lambda i: (i, 0))
    return pl.pallas_call(
        body, name=name, grid=(T // BQ_B,),
        in_specs=[pl.BlockSpec(memory_space=pltpu.SMEM), qs, _whole((T, WKV)), _whole((T, WKV)), qs, qs, qs],
        out_specs=[qs, _whole((T, WKV)), _whole((T, WKV)), _whole((1, LANE))],
        out_shape=[jax.ShapeDtypeStruct((T, WB), F32), jax.ShapeDtypeStruct((T, WKV), F32),
                   jax.ShapeDtypeStruct((T, WKV), F32), jax.ShapeDtypeStruct((1, LANE), F32)],
        compiler_params=_params(("arbitrary",)),
    )(sink, qb, kb, vb, ob, lse, dob)


SPAN_C = NA_ROWS * GRID_W


def _row_start(r):
    return jnp.clip(r - NA_ROWS // 2, 0, ROWS - NA_ROWS)


def _off_index(r):
    return _row_start(r) - r + (NA_ROWS - 1)


N_TAB = 16
RPS = 4


def _rpb_tables(name, rpb):
    circ = jnp.concatenate([rpb[..., NA_COLS - 1:], jnp.zeros(rpb.shape[:2] + (LANE - (2 * NA_COLS - 1),), F32),
                            rpb[..., :NA_COLS - 1]], axis=-1)
    circ = jnp.pad(circ, ((0, 0), (0, N_TAB + 1 - circ.shape[1]), (0, 0)))

    def body(w_ref, o_ref):
        c = lax.broadcasted_iota(jnp.int32, (GRID_W, LANE), 0)
        lane = lax.broadcasted_iota(jnp.int32, (GRID_W, LANE), 1)
        cs = jnp.clip(c - NA_COLS // 2, 0, GRID_W - NA_COLS)
        valid = (lane % GRID_W >= cs) & (lane % GRID_W < cs + NA_COLS)
        toep = [pltpu.roll(jnp.broadcast_to(w_ref[a:a + 1, :], (GRID_W, LANE)), 0, 1, stride=1, stride_axis=0)
                for a in range(N_TAB + 1)]
        for a in range(N_TAB):
            pair = jnp.where(lane < GRID_W, toep[a], pltpu.roll(toep[a + 1], GRID_W, 1))
            o_ref[a] = jnp.where(valid, pair, NEG)

    return pl.pallas_call(
        body, name=name, grid=(HC,),
        in_specs=[pl.BlockSpec((None, N_TAB + 1, LANE), lambda h: (h, 0, 0))],
        out_specs=pl.BlockSpec((None, N_TAB, GRID_W, LANE), lambda h: (h, 0, 0, 0)),
        out_shape=jax.ShapeDtypeStruct((HC, N_TAB, GRID_W, LANE), F32), compiler_params=_params(("parallel",)),
    )(circ)


def _bias_c(t_ref, h, d):
    return jnp.concatenate([t_ref[h, d + k] for k in range(0, NA_ROWS, 2)], axis=1)


def _attn_c_fwd(name, qkv, tables):
    def body(q_ref, k_ref, v_ref, t_ref, o_ref, lse_ref):
        heads = []
        for rr in range(RPS):
            r = pl.program_id(1) * RPS + rr
            rows = slice(rr * GRID_W, (rr + 1) * GRID_W)
            start = pl.multiple_of(_row_start(r) * GRID_W, GRID_W)
            kw, vw = k_ref[pl.ds(start, SPAN_C), :], v_ref[pl.ds(start, SPAN_C), :]
            heads += [dict(q=q_ref[rows, h * HD:(h + 1) * HD], k=kw[:, h * HD:(h + 1) * HD], v=vw[:, h * HD:(h + 1) * HD],
                           bias=_bias_c(t_ref, h, _off_index(r))) for h in range(2)]
        outs = _heads_fwd(heads)
        for rr in range(RPS):
            rows = slice(rr * GRID_W, (rr + 1) * GRID_W)
            o_ref[rows, :] = jnp.concatenate([o for o, _ in outs[2 * rr:2 * rr + 2]], axis=1)
            lse_ref[rows, :] = _per_head([lse for _, lse in outs[2 * rr:2 * rr + 2]])

    qs = pl.BlockSpec((RPS * GRID_W, LANE), lambda p, r: (r, p))
    ks = pl.BlockSpec((T, LANE), lambda p, r: (0, p))
    ts = pl.BlockSpec((2, N_TAB, GRID_W, LANE), lambda p, r: (p, 0, 0, 0))
    return pl.pallas_call(
        body, name=name, grid=(HC // 2, ROWS // RPS),
        in_specs=[_qkv_rows(RPS * GRID_W, "qc"), _qkv_all("kc"), _qkv_all("vc"), ts], out_specs=[qs, qs],
        out_shape=[jax.ShapeDtypeStruct((T, WC), F32)] * 2, compiler_params=_params(("parallel", "parallel")),
    )(qkv, qkv, qkv, tables)


def _attn_c_bwd(name, qkv, oc, lse, doc, tables):
    def body(q_ref, k_ref, v_ref, o_ref, lse_ref, do_ref, t_ref, dq_ref, dk_ref, dv_ref, dt_ref):
        @pl.when(pl.program_id(1) == 0)
        def _():
            dk_ref[...] = jnp.zeros_like(dk_ref)
            dv_ref[...] = jnp.zeros_like(dv_ref)
            dt_ref[...] = jnp.zeros_like(dt_ref)

        heads, where = [], []
        for rr in range(RPS):
            r = pl.program_id(1) * RPS + rr
            rows = slice(rr * GRID_W, (rr + 1) * GRID_W)
            d = _off_index(r)
            start = pl.multiple_of(_row_start(r) * GRID_W, GRID_W)
            kw, vw = k_ref[pl.ds(start, SPAN_C), :], v_ref[pl.ds(start, SPAN_C), :]
            where.append((rows, d, start))
            for h in range(2):
                sl = slice(h * HD, (h + 1) * HD)
                heads.append(dict(q=q_ref[rows, sl], k=kw[:, sl], v=vw[:, sl], o=o_ref[rows, sl], do=do_ref[rows, sl],
                                  lse=lse_ref[rows, h * HD:h * HD + 1], bias=_bias_c(t_ref, h, d)))
        res = _heads_bwd(heads)
        for rr, (rows, d, start) in enumerate(where):
            pair = res[2 * rr:2 * rr + 2]
            for h in range(2):
                for k in range(0, NA_ROWS, 2):
                    dt_ref[h, d + k] += pair[h][3][:, k * GRID_W:(k + 2) * GRID_W]
            dq_ref[rows, :] = jnp.concatenate([p[0] for p in pair], axis=1)
            dk_ref[pl.ds(start, SPAN_C), :] += jnp.concatenate([p[1] for p in pair], axis=1)
            dv_ref[pl.ds(start, SPAN_C), :] += jnp.concatenate([p[2] for p in pair], axis=1)

    qs = pl.BlockSpec((RPS * GRID_W, LANE), lambda p, r: (r, p))
    ks = pl.BlockSpec((T, LANE), lambda p, r: (0, p))
    ts = pl.BlockSpec((2, N_TAB, GRID_W, LANE), lambda p, r: (p, 0, 0, 0))
    return pl.pallas_call(
        body, name=name, grid=(HC // 2, ROWS // RPS),
        in_specs=[_qkv_rows(RPS * GRID_W, "qc"), _qkv_all("kc"), _qkv_all("vc"), qs, qs, qs, ts],
        out_specs=[qs, ks, ks, ts],
        out_shape=[jax.ShapeDtypeStruct((T, WC), F32)] * 3 + [jax.ShapeDtypeStruct((HC, N_TAB, GRID_W, LANE), F32)],
        compiler_params=_params(("parallel", "arbitrary")),
    )(qkv, qkv, qkv, oc, lse, doc, tables)


def _split3(v):
    hi = v.astype(BF16)
    r1 = v - hi.astype(F32)
    mid = r1.astype(BF16)
    lo = (r1 - mid.astype(F32)).astype(BF16)
    return hi, mid, lo


def _rpb_reduce(name, dtables):
    x = dtables.reshape(HC, N_TAB, GRID_W * LANE)
    c = jnp.arange(GRID_W)[:, None]
    lane = jnp.arange(LANE)[None, :]
    col = (lane // GRID_W) * LANE + jnp.clip(lane % GRID_W - c + (NA_COLS - 1), 0, 2 * NA_COLS - 2)
    col_onehot = (col.reshape(-1)[:, None] == jnp.arange(2 * LANE)[None, :]).astype(BF16)
    a2 = jnp.arange(N_TAB)[None, :]
    row_onehot = jnp.concatenate([(jnp.arange(16)[:, None] == a2 + u) & (a2 < 2 * NA_ROWS - 2) for u in range(2)],
                                 axis=1).astype(BF16)

    def body(x_ref, e_ref, f_ref, o_ref):
        y = sum(jnp.dot(part, e_ref[...], preferred_element_type=F32) for part in _split3(x_ref[...]))
        z = jnp.concatenate([y[:, :LANE], y[:, LANE:]], axis=0)
        o_ref[...] = sum(jnp.dot(f_ref[...], part, preferred_element_type=F32) for part in _split3(z))

    out = pl.pallas_call(
        body, name=name, grid=(HC,),
        in_specs=[pl.BlockSpec((None, N_TAB, GRID_W * LANE), lambda h: (h, 0, 0)),
                  _whole((GRID_W * LANE, 2 * LANE)), _whole((16, 2 * N_TAB))],
        out_specs=pl.BlockSpec((None, 16, LANE), lambda h: (h, 0, 0)),
        out_shape=jax.ShapeDtypeStruct((HC, 16, LANE), F32), compiler_params=_params(("parallel",)),
    )(x, col_onehot, row_onehot)
    return out[:, :2 * NA_ROWS - 1, :2 * NA_COLS - 1]


TC = 128
CHUNK = 128
MARGIN = 8


def _shift_down(v, rows):
    return jnp.where(rows == 0, 0.0, pltpu.roll(v, 1, 0))


def _shift_up(v, rows):
    return jnp.where(rows == T - 1, 0.0, pltpu.roll(v, T - 1, 0))


def _conv(v, w, b, rows):
    return _shift_down(v, rows) * w[0:1] + v * w[1:2] + _shift_up(v, rows) * w[2:3] + b


FWD_BLOCKS = 4
BWD_BLOCKS = 1


def _ffn_mid_fwd(name, up, conv_w, conv_b):
    wide = FWD_BLOCKS * TC

    def body(xg_ref, xv_ref, wg_ref, wv_ref, bg_ref, bv_ref, o_ref):
        rows = lax.broadcasted_iota(jnp.int32, (T, TC), 0)
        for b in range(FWD_BLOCKS):
            lanes = slice(b * TC, (b + 1) * TC)
            ug = _conv(xg_ref[b], wg_ref[:, lanes], bg_ref[:, lanes], rows)
            uv = _conv(xv_ref[b], wv_ref[:, lanes], bv_ref[:, lanes], rows)
            o_ref[:, lanes] = (ug * jax.nn.sigmoid(ug) * uv).astype(BF16)

    gate = lambda shape: pl.BlockSpec(shape, lambda j: (0, j))
    val = lambda shape: pl.BlockSpec(shape, lambda j: (0, j + DFF // wide))
    return pl.pallas_call(
        body, name=name, grid=(DFF // wide,),
        in_specs=[pl.BlockSpec((FWD_BLOCKS, T, TC), lambda j: (j, 0, 0)),
                  pl.BlockSpec((FWD_BLOCKS, T, TC), lambda j: (j + DFF // wide, 0, 0)),
                  gate((3, wide)), val((3, wide)), gate((1, wide)), val((1, wide))],
        out_specs=pl.BlockSpec((T, wide), lambda j: (0, j)),
        out_shape=jax.ShapeDtypeStruct((T, DFF), BF16), compiler_params=_params(("parallel",)),
    )(up, up, conv_w, conv_w, conv_b, conv_b)


def _ffn_mid_bwd(name, dact, up, conv_w, conv_b):
    window = CHUNK + 2 * MARGIN
    centre = slice(MARGIN, MARGIN + CHUNK)

    def shifted(v):
        return pltpu.roll(v, 1, 0), pltpu.roll(v, window - 1, 0)

    def fold(v):
        return jnp.sum(v[centre].reshape(CHUNK // 8, 8, TC), axis=0)

    wide = BWD_BLOCKS * TC

    def body(da_ref, xg_ref, xv_ref, wg_ref, wv_ref, bg_ref, bv_ref, dx_ref, dw_ref, db_ref, dap, xgp, xvp):
        for b in range(BWD_BLOCKS):
            block(b, da_ref, xg_ref, xv_ref, wg_ref, wv_ref, bg_ref, bv_ref, dx_ref, dw_ref, db_ref, dap, xgp, xvp)

    def block(b, da_ref, xg_ref, xv_ref, wg_ref, wv_ref, bg_ref, bv_ref, dx_ref, dw_ref, db_ref, dap, xgp, xvp):
        lanes = slice(b * TC, (b + 1) * TC)
        for src, pad in ((da_ref[:, lanes], dap), (xg_ref[b], xgp), (xv_ref[b], xvp)):
            pad[0:MARGIN, :] = jnp.zeros((MARGIN, TC), F32)
            pad[MARGIN:MARGIN + T, :] = src
            pad[MARGIN + T:, :] = jnp.zeros((MARGIN, TC), F32)
        wg, wv, bg, bv = wg_ref[:, lanes], wv_ref[:, lanes], bg_ref[:, lanes], bv_ref[:, lanes]

        def chunk(c, sums):
            r0 = pl.multiple_of(c * CHUNK, CHUNK)
            da, xg, xv = dap[pl.ds(r0, window), :], xgp[pl.ds(r0, window), :], xvp[pl.ds(r0, window), :]
            xg_prev, xg_next = shifted(xg)
            xv_prev, xv_next = shifted(xv)
            ug = xg_prev * wg[0:1] + xg * wg[1:2] + xg_next * wg[2:3] + bg
            uv = xv_prev * wv[0:1] + xv * wv[1:2] + xv_next * wv[2:3] + bv
            sg = jax.nn.sigmoid(ug)
            dug = da * uv * (sg * (1.0 + ug * (1.0 - sg)))
            duv = da * (ug * sg)
            out = []
            for half, (x_prev, x, x_next, w, du) in enumerate(((xg_prev, xg, xg_next, wg, dug),
                                                               (xv_prev, xv, xv_next, wv, duv))):
                du_prev, du_next = shifted(du)
                dx = du_next * w[0:1] + du * w[1:2] + du_prev * w[2:3]
                dx_ref[half, pl.ds(r0, CHUNK), lanes] = dx[centre].astype(BF16)
                out += [fold(x_prev * du), fold(x * du), fold(x_next * du), fold(du)]
            return tuple(s + o for s, o in zip(sums, out))

        sums = lax.fori_loop(0, T // CHUNK, chunk, tuple(jnp.zeros((8, TC), F32) for _ in range(8)))
        rows = [jnp.sum(s, axis=0, keepdims=True) for s in sums]
        for half in range(2):
            dw_ref[half, :, lanes] = jnp.concatenate(rows[4 * half:4 * half + 3], axis=0)
            db_ref[half, :, lanes] = rows[4 * half + 3]

    gate = lambda shape: pl.BlockSpec(shape, lambda j: (0, j))
    val = lambda shape: pl.BlockSpec(shape, lambda j: (0, j + DFF // wide))
    return pl.pallas_call(
        body, name=name, grid=(DFF // wide,),
        in_specs=[gate((T, wide)), pl.BlockSpec((BWD_BLOCKS, T, TC), lambda j: (j, 0, 0)),
                  pl.BlockSpec((BWD_BLOCKS, T, TC), lambda j: (j + DFF // wide, 0, 0)),
                  gate((3, wide)), val((3, wide)), gate((1, wide)), val((1, wide))],
        out_specs=[pl.BlockSpec((2, T, wide), lambda j: (0, 0, j)), pl.BlockSpec((2, 3, wide), lambda j: (0, 0, j)),
                   pl.BlockSpec((2, 1, wide), lambda j: (0, 0, j))],
        out_shape=[jax.ShapeDtypeStruct((2, T, DFF), BF16), jax.ShapeDtypeStruct((2, 3, DFF), F32),
                   jax.ShapeDtypeStruct((2, 1, DFF), F32)],
        scratch_shapes=[pltpu.VMEM((T + 2 * MARGIN, TC), F32)] * 3,
        compiler_params=_params(("parallel",)),
    )(dact, up, up, conv_w, conv_w, conv_b, conv_b)


def _dup_spec(tm, nj):
    per = DFF // nj
    return pl.BlockSpec((None, tm, nj), lambda a, b, j: (j // per, 0 if tm == T else b, j % per))


def _dup_spec_tn(tm, nj):
    per = DFF // nj
    return pl.BlockSpec((None, tm, nj), lambda j, kt, r: (j // per, 0, j % per))


def _adamw_math(w, g, m, v):
    m = ADAM_B1 * m + (1.0 - ADAM_B1) * g
    v = ADAM_B2 * v + (1.0 - ADAM_B2) * (g * g)
    m_hat = m / (1.0 - ADAM_B1 ** ADAM_STEP)
    v_hat = v / (1.0 - ADAM_B2 ** ADAM_STEP)
    delta = -ADAM_LR * (m_hat / (jnp.sqrt(v_hat) + ADAM_EPS) + ADAM_WD * w)
    return delta, m, v


ADAM_BLOCK = 256 * 1408


def _adamw_sharded(name, w, m, v, parts):
    _, r, c = w.shape
    tr = max(t for t in range(16, r + 1, 16) if r % t == 0 and t * c <= ADAM_BLOCK)

    def body(w_ref, m_ref, v_ref, p0_ref, p1_ref, g_ref, d_ref, nm_ref, nv_ref):
        def run(p_ref):
            g = p_ref[0].astype(F32)
            for k in range(1, N_DEV):
                g = g + p_ref[k].astype(F32)
            d, nm, nv = _adamw_math(w_ref[...], g, m_ref[...], v_ref[...])
            g_ref[...] = g
            d_ref[...] = d
            nm_ref[...] = nm
            nv_ref[...] = nv

        @pl.when(pl.program_id(0) == 0)
        def _():
            run(p0_ref)

        @pl.when(pl.program_id(0) == 1)
        def _():
            run(p1_ref)

    ws = pl.BlockSpec((None, tr, c), lambda l, i: (l, i, 0))
    p0 = pl.BlockSpec((N_DEV, tr, c), lambda l, i: (0, jnp.where(l == 0, i, r // tr - 1), 0))
    p1 = pl.BlockSpec((N_DEV, tr, c), lambda l, i: (0, jnp.where(l == 1, i, 0), 0))
    return pl.pallas_call(
        body, name=name, grid=(DEPTH, r // tr), in_specs=[ws, ws, ws, p0, p1], out_specs=[ws] * 4,
        out_shape=[jax.ShapeDtypeStruct(w.shape, F32)] * 4, compiler_params=_params(("arbitrary", "arbitrary")),
    )(w, m, v, *parts)


def _sum_devices(name, parts):
    r = parts.shape[1]

    def body(p_ref, o_ref):
        g = p_ref[0]
        for k in range(1, N_DEV):
            g = g + p_ref[k]
        o_ref[...] = g

    return pl.pallas_call(
        body, name=name, in_specs=[pl.BlockSpec((N_DEV, r, LANE), lambda: (0, 0, 0))],
        out_specs=pl.BlockSpec((r, LANE), lambda: (0, 0)), out_shape=jax.ShapeDtypeStruct((r, LANE), F32),
        compiler_params=_params(),
    )(parts)


def _adamw_small(name, ws, gs, ms, vs):
    n = len(ws)
    shapes = [w.shape for w in ws]
    ws, gs, ms, vs = ([a.reshape(1, -1) if a.ndim == 1 else a for a in arrs] for arrs in (ws, gs, ms, vs))
    specs = [pl.BlockSpec(memory_space=pltpu.VMEM)] * n

    def body(*refs):
        for i in range(n):
            w_ref, g_ref, m_ref, v_ref = (refs[k * n + i] for k in range(4))
            d, nm, nv = _adamw_math(w_ref[...], g_ref[...], m_ref[...], v_ref[...])
            refs[4 * n + i][...] = d
            refs[5 * n + i][...] = nm
            refs[6 * n + i][...] = nv

    outs = pl.pallas_call(
        body, name=name, in_specs=specs * 4, out_specs=specs * 3,
        out_shape=[jax.ShapeDtypeStruct(w.shape, F32) for w in ws] * 3, compiler_params=_params(),
    )(*ws, *gs, *ms, *vs)
    outs = [o.reshape(shapes[i % n]) for i, o in enumerate(outs)]
    return outs[:n], outs[n:2 * n], outs[2 * n:]


def _pack(arrays):
    flat = jnp.concatenate([a.reshape(-1) for a in arrays])
    pad = (-flat.shape[0]) % (8 * LANE)
    return jnp.pad(flat, (0, pad)).reshape(-1, LANE)


def _unpack(buf, shapes):
    flat, out, off = buf.reshape(-1), [], 0
    for s in shapes:
        n = 1
        for d in s:
            n *= d
        out.append(flat[off:off + n].reshape(s))
        off += n
    return out


def _local_step(x, target, small, weights, conv_w_full, hand_over, used):
    cos2, sin2 = _rope_tables()
    bias_a = _dilation_bias()
    tables = [_rpb_tables(f"rpb_tables_{l}", small["rpb_c"][l]) for l in range(DEPTH)]
    saved, carry = [], 0.0
    for l in range(DEPTH):
        g1, g2 = small["ln_attn"][l][None] + carry, small["ln_ffn"][l][None]
        gain, sink, cb = small["mix_gain"][l][None], small["sink_b"][l], small["conv_b"][l][None]
        cw = conv_w_full[l]
        bias = tables[l]
        h1, qkv = _prologue_matmul(f"proj_in_{l}", _rmsnorm_rows, [x, g1], [D, None],
                                   weights("w_in", l, [cos2, sin2, bias_a] + tables if l == 0 else x),
                                   (D, 1024), lambda j: (0, j), 1024, epilogue=_rope_epilogue, extras=(cos2, sin2),
                                   out_dtype=BF16)
        zero = used(f"proj_in_{l}", qkv)
        qb, kb, vb = (qkv[:, BLOCK_OF[n] * LANE:BLOCK_OF[n] * LANE + w] for n, w in (("qb", WB), ("kb", WKV), ("vb", WKV)))
        oa, lse_a = _attn_a_fwd(f"attn_a_{l}", qkv, bias_a)
        ob, lse_b = _attn_b_fwd(f"attn_b_{l}", qb, kb, vb, sink + zero)
        oc, lse_c = _attn_c_fwd(f"attn_c_{l}", qkv, bias)
        mixed, x_mid = _prologue_matmul(f"proj_out_{l}", _mix_rows, [oa, ob, oc, gain + used(f"attn_{l}", oc)],
                                        [WA, WB, WC, None],
                                        weights("w_out", l, oc), (N_DEV, D // N_DEV, 512), lambda j: (0, 0, j), 512,
                                        res=x)
        h2, up = _prologue_matmul(f"ffn_up_{l}", _rmsnorm_rows, [x_mid, g2 + used(f"proj_out_{l}", x_mid)], [D, None],
                                  weights("w_up", l, x_mid), (D, 1024), lambda j: (0, j), 1024, blocked_out=True)
        act = _ffn_mid_fwd(f"ffn_mid_{l}", up, cw, cb + used(f"ffn_up_{l}", up))
        x_out = _nn_rows(f"ffn_down_{l}", act, weights("w_down", l, act), x_mid, 4, 1024, 1024)
        carry = used(f"ffn_down_{l}", x_out)
        saved.append(dict(x=x, h1=h1, qkv=(qkv, qb, kb, vb), o=(oa, ob, oc), lse=(lse_a, lse_b, lse_c), mixed=mixed,
                          x_mid=x_mid, h2=h2, up=up, act=act, g1=g1, g2=g2, gain=gain, sink=sink, cb=cb, cw=cw, bias=bias))
        x = x_out

    loss8, dx, dxb, d_ln_final = _loss_head(x, small["ln_final"][None], target)
    sgrads = [None] * DEPTH
    for l in reversed(range(DEPTH)):
        s = saved[l]
        qkv, qb, kb, vb = s["qkv"]
        oa, ob, oc = s["o"]
        wg_in, wg_out = weights("w_in", l, None), weights("w_out", l, None)
        wg_up, wg_down = weights("w_up", l, None), weights("w_down", l, None)
        g_down = _tn_rows(f"wgrad_down_{l}", s["act"], dxb, wg_down.shape[1], 2, 512)
        zero = hand_over("w_down", l, g_down)
        dact = _nt_rows(f"dgrad_down_{l}", dxb, wg_down, 4, 512)
        dup, d_cw, d_cb = _ffn_mid_bwd(f"ffn_mid_bwd_{l}", dact, s["up"], s["cw"], s["cb"] + zero)
        g_up = _tn_cols(f"wgrad_up_{l}", s["h2"], dup, _dup_spec_tn, 2 * DFF, DFF // 2)
        zero = hand_over("w_up", l, g_up)
        dh2 = _nt_cols(f"dgrad_up_{l}", dup, _dup_spec, wg_up, DFF // 2)
        dx, dxb, d_g2 = _rmsnorm_bwd(f"norm_ffn_bwd_{l}", dh2, s["x_mid"], s["g2"] + zero, dx)
        g_out = _tn_rows(f"wgrad_out_{l}", s["mixed"], dxb, wg_out.shape[1], 2, D)
        zero = hand_over("w_out", l, g_out)
        dmixed = _nt_rows(f"dgrad_out_{l}", dxb, wg_out, 2, T)
        doa, dob, doc, d_gain = _mix_bwd(f"mix_bwd_{l}", dmixed, oa, ob, oc, s["gain"] + zero)
        lse_a, lse_b, lse_c = s["lse"]
        dqa, dka, dva = _attn_a_bwd(f"attn_a_bwd_{l}", qkv, oa, lse_a, doa, bias_a)
        dqb, dkb, dvb, d_sink = _attn_b_bwd(f"attn_b_bwd_{l}", qb, kb, vb, ob, lse_b, dob, s["sink"])
        dqc, dkc, dvc, d_bias = _attn_c_bwd(f"attn_c_bwd_{l}", qkv, oc, lse_c, doc, s["bias"])
        d_rpb = _rpb_reduce(f"rpb_reduce_{l}", d_bias)
        dproj = _rope_bwd(f"rope_bwd_{l}", (dqa, dka, dva, dqb, dkb, dvb, dqc, dkc, dvc), cos2, sin2)
        g_in = _tn_cols(f"wgrad_in_{l}", s["h1"], dproj,
                        lambda tm, tn: pl.BlockSpec((tm, tn), lambda j, kt, r: (0, j)), IN_COLS, 1024)
        zero = hand_over("w_in", l, g_in)
        dh1 = _nt_cols(f"dgrad_in_{l}", dproj, lambda tm, nc: pl.BlockSpec((tm, nc), lambda kt, i, j: (i, j)), wg_in,
                       IN_COLS // 2)
        dx, dxb, d_g1 = _rmsnorm_bwd(f"norm_attn_bwd_{l}", dh1, s["x"], s["g1"] + zero, dx)
        sgrads[l] = dict(ln_attn=d_g1[0], sink_b=d_sink[0, :HB], rpb_c=d_rpb, mix_gain=d_gain[0], ln_ffn=d_g2[0],
                         conv_w=d_cw.transpose(1, 0, 2).reshape(3, 2 * DFF), conv_b=d_cb.reshape(2 * DFF))
    return loss8[0, 0], dx, d_ln_final[0], sgrads


SMALL_NAMES = ("ln_attn", "sink_b", "rpb_c", "mix_gain", "ln_ffn", "conv_b")


def kernel(x, ln_attn, w_in, sink_b, rpb_c, mix_gain, w_out, ln_ffn, w_up, conv_w, conv_b, w_down, ln_final, loss_target, m_ln_attn, m_w_in, m_sink_b, m_rpb_c, m_mix_gain, m_w_out, m_ln_ffn, m_w_up, m_conv_w, m_conv_b, m_w_down, m_ln_final, v_ln_attn, v_w_in, v_sink_b, v_rpb_c, v_mix_gain, v_w_out, v_ln_ffn, v_w_up, v_conv_w, v_conv_b, v_w_down, v_ln_final):
    me = 4 * lax.axis_index("x") + 2 * lax.axis_index("y") + lax.axis_index("c")
    small = dict(ln_attn=ln_attn, sink_b=sink_b, rpb_c=rpb_c, mix_gain=mix_gain, ln_ffn=ln_ffn, conv_b=conv_b,
                 ln_final=ln_final)

    names = ("w_in", "w_out", "w_up", "w_down")
    shards = dict(w_in=w_in, w_out=w_out, w_up=w_up, w_down=w_down)
    order = [(n, l) for l in range(DEPTH) for n in names]
    conv_key = ("conv_w", 0)
    started, arrived, forwarded, gathered = {}, {}, {}, {}

    def side_by_side(k):
        return k[0] in ("w_in", "w_up")

    def slot_of(k):
        return _col_slot(shards[k[0]].shape[2]) if side_by_side(k) else _lead_slot

    def begin(name, ks, zero):
        srcs = [_pack([conv_w]) + zero if k == conv_key else (shards[k[0]][k[1]] + zero).astype(BF16) for k in ks]
        lands = [lax.empty((s.shape[0], N_DEV * s.shape[1]) if side_by_side(k) else (N_DEV,) + s.shape, s.dtype)
                 for k, s in zip(ks, srcs)]
        peers = [ALL_PEERS if k == conv_key else NEAR_PEERS for k in ks]
        send, recv, bufs, tok = _copy_start(name, srcs + lands, _gather_plan(peers, [slot_of(k) for k in ks]),
                                            [len(p) + 1 for p in peers])
        for i, k in enumerate(ks):
            started[k] = (send[i], recv[i], bufs[i], bufs[len(ks) + i], peers[i])
        return tok

    token = begin("gather_start_first", order[:1], 0.0)
    token = begin("gather_start_rest", [conv_key] + order[1:], token[0, 0])

    def arrive(k, after):
        send, recv, src, land, peers = started[k]
        arrived[k] = _copy_wait(f"gather_{k[0]}_{k[1]}_arrived", [src, land], [send], [recv],
                                _gather_plan([peers], [slot_of(k)]), after)

    queue = list(order)

    def advance(after):
        if not queue:
            return 0.0
        k = queue.pop(0)
        arrive(k, after)
        forwarded[k] = _copy_start(f"gather_{k[0]}_{k[1]}_forward", [arrived[k][1]], _forward_plan(slot_of(k)),
                                   [len(OTHER_CHIPS)])
        return forwarded[k][3][0, 0]

    pass_on_behind = ("proj_in_0", "attn_0", "ffn_up_0", "ffn_down_0", "proj_in_1", "attn_1", "ffn_up_1")

    def used(point, result):
        return advance(result) if point in pass_on_behind else 0.0

    def weights(n, l, after):
        k = (n, l)
        if k not in gathered:
            if k not in forwarded:
                advance(after)
            send_b, recv_b, (land,), _ = forwarded[k]
            (gathered[k],) = _copy_wait(f"gather_{n}_{l}_done", [land], send_b, recv_b, _forward_plan(slot_of(k)),
                                        after)
        return gathered[k]

    pending = {}

    def hand_over(n, l, g):
        shard = shards[n].shape[1:]
        send, recv, bufs, tok = _copy_start(f"send_grad_{n}_{l}", [g, lax.empty((N_DEV,) + shard, g.dtype)],
                                            _scatter_plan(slot_of((n, l))), [len(ALL_PEERS) + 1])
        pending[(n, l)] = (send, recv, bufs)
        return tok[0, 0]

    def received(k, after):
        send, recv, bufs = pending[k]
        return _copy_wait(f"recv_grad_{k[0]}_{k[1]}", bufs, send, recv, _scatter_plan(slot_of(k)), after)[1]

    arrive(conv_key, token)
    cw_all = arrived[conv_key][1]
    nup = w_up.shape[2]
    cw_shards = cw_all.reshape(N_DEV, -1)[:, :DEPTH * 3 * nup].reshape(N_DEV, DEPTH, 3, nup)
    conv_w_full = cw_shards.transpose(1, 2, 0, 3).reshape(DEPTH, 3, N_DEV * nup)

    loss_local, dx, d_ln_final, sgrads = _local_step(
        x[0], loss_target[0], dict(small, ln_attn=ln_attn + token[0, 0]), weights, conv_w_full, hand_over, used)

    stacked = [jnp.stack([sgrads[l][n] for l in range(DEPTH)]) for n in SMALL_NAMES + ("conv_w",)] + [d_ln_final]
    shapes = [a.shape for a in stacked]
    mine = _pack(stacked)
    send_s, recv_s, bufs_s, _ = _copy_start("gather_small_grads_start", [mine, lax.empty((N_DEV,) + mine.shape, F32)],
                                            _gather_plan([ALL_PEERS], [_lead_slot]), [len(ALL_PEERS) + 1])

    big, after = {}, dx
    moments = dict(w_in=(m_w_in, v_w_in), w_out=(m_w_out, v_w_out), w_up=(m_w_up, v_w_up), w_down=(m_w_down, v_w_down))
    for n in reversed(names):
        parts = (received((n, 0), after), received((n, 1), after))
        big[n] = _adamw_sharded(f"adamw_{n}", shards[n], *moments[n], parts)
        after = big[n][1]

    _, everyone = _copy_wait("gather_small_grads_done", bufs_s, send_s, recv_s,
                             _gather_plan([ALL_PEERS], [_lead_slot]), after)
    g_small = _unpack(_sum_devices("sum_small_grads", everyone), shapes)
    g = dict(zip(SMALL_NAMES + ("conv_w", "ln_final"), g_small))
    g["conv_w"] = lax.dynamic_slice_in_dim(g["conv_w"], me * nup, nup, axis=2)

    snames = SMALL_NAMES + ("conv_w", "ln_final")
    sw = dict(small, conv_w=conv_w)
    sm = dict(ln_attn=m_ln_attn, sink_b=m_sink_b, rpb_c=m_rpb_c, mix_gain=m_mix_gain, ln_ffn=m_ln_ffn,
              conv_b=m_conv_b, conv_w=m_conv_w, ln_final=m_ln_final)
    sv = dict(ln_attn=v_ln_attn, sink_b=v_sink_b, rpb_c=v_rpb_c, mix_gain=v_mix_gain, ln_ffn=v_ln_ffn,
              conv_b=v_conv_b, conv_w=v_conv_w, ln_final=v_ln_final)
    s_delta, s_m, s_v = (dict(zip(snames, out)) for out in _adamw_small(
        "adamw_small", [sw[n] for n in snames], [g[n] for n in snames], [sm[n] for n in snames],
        [sv[n] for n in snames]))

    loss = lax.psum(loss_local, ("x", "y", "c"))
    outputs = ("ln_attn", "w_in", "sink_b", "rpb_c", "mix_gain", "w_out", "ln_ffn", "w_up", "conv_w", "conv_b",
               "w_down", "ln_final")
    grads = [big[n][0] if n in big else g[n] for n in outputs]
    deltas = [big[n][1] if n in big else s_delta[n] for n in outputs]
    new_m = [big[n][2] if n in big else s_m[n] for n in outputs]
    new_v = [big[n][3] if n in big else s_v[n] for n in outputs]
    return (loss, dx[None], *grads, *deltas, *new_m, *new_v)
```

```python
import functools

import jax
import jax.numpy as jnp
from jax import lax
from jax.experimental import pallas as pl
from jax.experimental.pallas import tpu as pltpu

F32 = jnp.float32
BF16 = jnp.bfloat16

N_DEV = 8
T = 2048
D = 2048
DEPTH = 2
HD = 64
HA, HB, HKV, HC = 12, 10, 2, 10
WA, WB, WKV, WC = HA * HD, HB * HD, HKV * HD, HC * HD
IN_COLS = 3 * WA + WB + 2 * WKV + 3 * WC
DFF = 5632
GRID_W = 64
ROWS = T // GRID_W
NA_ROWS, NA_COLS = 8, 16
WINDOW_B = 128
EPS = 1e-6
NEG = -1e30
ROPE_THETA = 10000.0
LANE = 128
VMEM_LIMIT = 56 * 1024 * 1024

ADAM_LR, ADAM_B1, ADAM_B2, ADAM_EPS, ADAM_WD, ADAM_STEP = 0.001, 0.9, 0.999, 1e-08, 0.01, 10

GROUPS = (("qa", WA, True, True), ("ka", WA, True, False), ("va", WA, False, False),
          ("qb", WB, True, True), ("kb", WKV, True, False), ("vb", WKV, False, False),
          ("qc", WC, False, True), ("kc", WC, False, False), ("vc", WC, False, False))


def _params(sem=None):
    return pltpu.CompilerParams(dimension_semantics=sem, vmem_limit_bytes=VMEM_LIMIT)


HBM_SPEC = pl.BlockSpec(memory_space=pltpu.HBM)
SEM_SPEC = pl.BlockSpec(memory_space=pltpu.SEMAPHORE)
DATAFLOW = pltpu.SideEffectType.DATAFLOW_SIDE_EFFECTING


ALL_PEERS = tuple((p >> 2 & 1, p >> 1 & 1, p & 1) for p in range(1, N_DEV))
OTHER_CHIPS = ((1, 0, 0), (0, 1, 0), (1, 1, 0))
NEAR_PEERS = ((0, 0, 1),) + OTHER_CHIPS


def _flip(x, y, c, f):
    return (1 - x if f[0] else x, 1 - y if f[1] else y, 1 - c if f[2] else c)


def _index(pos):
    return 4 * pos[0] + 2 * pos[1] + pos[2]


class _LocalCopy:
    def __init__(self, src, dst, sem):
        self.copy = pltpu.make_async_copy(src, dst, sem)

    def start(self):
        self.copy.start()

    def wait_send(self):
        self.copy.wait()

    def wait_recv(self):
        pass


def _descriptors(plan, bufs, send_sems, recv_sems):
    x, y, c = lax.axis_index("x"), lax.axis_index("y"), lax.axis_index("c")
    return [_LocalCopy(src, dst, send_sems[g].at[i]) if partner is None else
            pltpu.make_async_remote_copy(src_ref=src, dst_ref=dst, send_sem=send_sems[g].at[i],
                                         recv_sem=recv_sems[g].at[i], device_id=partner,
                                         device_id_type=pl.DeviceIdType.MESH)
            for g, copies in enumerate(plan(bufs, x, y, c)) for i, (src, dst, partner) in enumerate(copies)]


def _copy_start(name, bufs, plan, sizes):
    nb, ng = len(bufs), len(sizes)

    def body(*refs):
        for d in _descriptors(plan, refs[:nb], refs[nb:nb + ng], refs[nb + ng:nb + 2 * ng]):
            d.start()
        refs[2 * nb + 2 * ng][...] = jnp.zeros((8, LANE), F32)

    outs = pl.pallas_call(
        body, name=name,
        out_shape=[pltpu.SemaphoreType.DMA((s,)) for s in sizes] * 2 + [pltpu.HBM(b.shape, b.dtype) for b in bufs]
        + [jax.ShapeDtypeStruct((8, LANE), F32)],
        in_specs=[HBM_SPEC] * nb,
        out_specs=[SEM_SPEC] * (2 * ng) + [HBM_SPEC] * nb + [pl.BlockSpec(memory_space=pltpu.VMEM)],
        input_output_aliases={i: 2 * ng + i for i in range(nb)},
        compiler_params=pltpu.CompilerParams(has_side_effects=DATAFLOW),
    )(*[pltpu.with_memory_space_constraint(b, pltpu.HBM) for b in bufs])
    return outs[:ng], outs[ng:2 * ng], outs[2 * ng:2 * ng + nb], outs[2 * ng + nb]


def _copy_wait(name, bufs, send_sems, recv_sems, plan, after):
    nb, ng = len(bufs), len(send_sems)
    after = list(after) if isinstance(after, (list, tuple)) else [after]

    def body(*refs):
        for d in _descriptors(plan, refs[:nb], refs[nb:nb + ng], refs[nb + ng:nb + 2 * ng]):
            d.wait_send()
            d.wait_recv()

    return pl.pallas_call(
        body, name=name, out_shape=[pltpu.HBM(b.shape, b.dtype) for b in bufs],
        in_specs=[HBM_SPEC] * nb + [SEM_SPEC] * (2 * ng) + [pl.BlockSpec(memory_space=pl.ANY)] * len(after),
        out_specs=[HBM_SPEC] * nb, input_output_aliases={i: i for i in range(nb)},
        compiler_params=pltpu.CompilerParams(has_side_effects=DATAFLOW),
    )(*bufs, *send_sems, *recv_sems, *after)


def _lead_slot(ref, k):
    return ref.at[k]


def _col_slot(width):
    return lambda ref, k: ref.at[:, pl.ds(pl.multiple_of(k * width, LANE), width)]


def _gather_plan(peer_sets, slots):
    def plan(bufs, x, y, c):
        n = len(peer_sets)
        return [[(bufs[i], slots[i](bufs[n + i], _index((x, y, c))), _flip(x, y, c, f)) for f in peers]
                + [(bufs[i], slots[i](bufs[n + i], _index((x, y, c))), None)] for i, peers in enumerate(peer_sets)]
    return plan


def _forward_plan(slot):
    def plan(bufs, x, y, c):
        pieces = [slot(bufs[0], _index(_flip(x, y, c, f))) for f in OTHER_CHIPS]
        return [[(p, p, _flip(x, y, c, (0, 0, 1))) for p in pieces]]
    return plan


def _scatter_plan(slot):
    def plan(bufs, x, y, c):
        me = _index((x, y, c))
        peers = [_flip(x, y, c, f) for f in ALL_PEERS]
        return [[(slot(bufs[0], _index(p)), bufs[1].at[me], p) for p in peers]
                + [(slot(bufs[0], me), bufs[1].at[me], None)]]
    return plan


def _flat2(v):
    return v.reshape(-1, v.shape[-1])


def _matmul(name, kind, a, a_spec, b, b_spec, out_shape, out_spec, grid, res=None, res_spec=None, acc_shape=None):
    dims = {"nn": (((1,), (0,)), ((), ())), "nt": NT_DIMS, "nts": NT_DIMS, "tn": (((0,), (0,)), ((), ()))}[kind]
    nred = grid[-1]

    def body(*refs):
        if res is None:
            a_ref, b_ref, o_ref = refs[:3]
            r_ref = None
        else:
            a_ref, b_ref, r_ref, o_ref = refs[:4]
        if kind == "nts":
            n = b_ref.shape[-1]
            part = sum(lax.dot_general(a_ref[:, blk * n:(blk + 1) * n], b_ref[blk], dims, preferred_element_type=F32)
                       for blk in range(b_ref.shape[0]))
        else:
            part = lax.dot_general(_flat2(a_ref[...]), _flat2(b_ref[...]), dims, preferred_element_type=F32)

        def finish(total):
            if r_ref is not None:
                total = total + r_ref[...]
            o_ref[...] = total.reshape(o_ref.shape).astype(o_ref.dtype)

        if nred == 1:
            finish(part)
        else:
            acc_ref = refs[-1]
            k = pl.program_id(len(grid) - 1)

            @pl.when(k == 0)
            def _():
                acc_ref[...] = part

            @pl.when(jnp.logical_and(k > 0, k < nred - 1))
            def _():
                acc_ref[...] += part

            @pl.when(k == nred - 1)
            def _():
                finish(acc_ref[...] + part)

    ins, specs = [a, b], [a_spec, b_spec]
    if res is not None:
        ins.append(res)
        specs.append(res_spec)
    scratch = [] if nred == 1 else [pltpu.VMEM(acc_shape, F32)]
    return pl.pallas_call(
        body, name=name, grid=grid, in_specs=specs, out_specs=out_spec, out_shape=out_shape, scratch_shapes=scratch,
        compiler_params=_params(("parallel",) * (len(grid) - 1) + ("arbitrary",)),
    )(*ins)


def _nn_rows(name, a, wg, res, s, tn, tm):
    _, kj, n = wg.shape
    return _matmul(
        name, "nn", a, pl.BlockSpec((tm, s * kj), lambda j, i, r: (i, r)),
        wg, pl.BlockSpec((s, kj, tn), lambda j, i, r: (r, 0, j)),
        jax.ShapeDtypeStruct((T, n), F32), pl.BlockSpec((tm, tn), lambda j, i, r: (i, j)),
        (n // tn, T // tm, N_DEV // s), res=res, res_spec=pl.BlockSpec((tm, tn), lambda j, i, r: (i, j)),
        acc_shape=(tm, tn))


def _nt_cols(name, dc, dc_spec_of, w, nc):
    k, n = w.shape
    tm = tk = 1024
    return _matmul(
        name, "nt", dc, dc_spec_of(tm, nc),
        w, pl.BlockSpec((tk, nc), lambda kt, i, j: (kt, j)),
        jax.ShapeDtypeStruct((T, k), F32), pl.BlockSpec((tm, tk), lambda kt, i, j: (i, kt)),
        (k // tk, T // tm, n // nc), acc_shape=(tm, tk))


def _nt_rows(name, dc, wg, s, tm):
    _, kj, n = wg.shape
    return _matmul(
        name, "nt", dc, pl.BlockSpec((tm, n), lambda kt, i, r: (i, 0)),
        wg, pl.BlockSpec((s, kj, n), lambda kt, i, r: (kt, 0, 0)),
        jax.ShapeDtypeStruct((T, N_DEV * kj), F32), pl.BlockSpec((tm, s * kj), lambda kt, i, r: (i, kt)),
        (N_DEV // s, T // tm, 1))


def _tn_cols(name, a, dc, dc_spec_of, n, tn):
    k = a.shape[1]
    tk = 512
    return _matmul(
        name, "tn", a, pl.BlockSpec((T, tk), lambda j, kt, r: (0, kt)),
        dc, dc_spec_of(T, tn),
        jax.ShapeDtypeStruct((k, n), BF16), pl.BlockSpec((tk, tn), lambda j, kt, r: (kt, j)),
        (n // tn, k // tk, 1))


def _tn_rows(name, a, dc, kj, s, tn):
    n = dc.shape[1]
    return _matmul(
        name, "tn", a, pl.BlockSpec((T, s * kj), lambda kt, j, r: (0, kt)),
        dc, pl.BlockSpec((T, tn), lambda kt, j, r: (0, j)),
        jax.ShapeDtypeStruct((N_DEV, kj, n), BF16), pl.BlockSpec((s, kj, tn), lambda kt, j, r: (kt, 0, j)),
        (N_DEV // s, n // tn, 1))


TR = 512


def _rows(width):
    return pl.BlockSpec((TR, width), lambda i: (i, 0))


def _whole(shape):
    return pl.BlockSpec(shape, lambda i: (0,) * len(shape))


def _rmsnorm_rows(x_ref, g_ref):
    xv = x_ref[...]
    r = lax.rsqrt(jnp.mean(xv * xv, axis=-1, keepdims=True) + EPS)
    return ((xv * r) * g_ref[...]).astype(BF16)


SUB = 256


def _prologue_matmul(name, prologue, ins, widths, w, w_block, w_index, tn, res=None, epilogue=None, extras=(),
                     out_dtype=F32, blocked_out=False):
    tm = 1024
    n = w.shape[-1]
    ni = len(ins)

    def body(*refs):
        w_ref = refs[ni]
        r_ref = refs[ni + 1] if res is not None else None
        x_refs = refs[ni + 1 + (res is not None):len(refs) - 3]
        h_ref, o_ref, h_scr = refs[-3:]

        @pl.when(pl.program_id(1) == 0)
        def _():
            h = prologue(*refs[:ni])
            h_scr[...] = h
            h_ref[...] = h

        for sub in range(tn // SUB):
            cols = slice(sub * SUB, (sub + 1) * SUB)
            w_cols = w_ref[(slice(None),) * (len(w_ref.shape) - 1) + (cols,)]
            part = jnp.dot(h_scr[...], _flat2(w_cols), preferred_element_type=F32)
            if r_ref is not None:
                part = part + r_ref[:, cols]
            if epilogue is not None:
                part = epilogue(pl.program_id(1) * (tn // SUB) + sub, part, *x_refs)
            if blocked_out:
                for b in range(SUB // LANE):
                    o_ref[sub * (SUB // LANE) + b] = part[:, b * LANE:(b + 1) * LANE].astype(out_dtype)
            else:
                o_ref[:, cols] = part.astype(out_dtype)

    tile = pl.BlockSpec((tm, tn), lambda i, j: (i, j))
    out_tile = pl.BlockSpec((tn // LANE, tm, LANE), lambda i, j: (j, i, 0)) if blocked_out else tile
    out_full = (n // LANE, T, LANE) if blocked_out else (T, n)
    specs = [pl.BlockSpec((1, D), lambda i, j: (0, 0)) if wd is None else pl.BlockSpec((tm, wd), lambda i, j: (i, 0))
             for wd in widths]
    specs.append(pl.BlockSpec(w_block, lambda i, j: w_index(j)))
    operands = list(ins) + [w]
    if res is not None:
        specs.append(tile)
        operands.append(res)
    specs += [pl.BlockSpec((tm, LANE), lambda i, j: (i, 0))] * len(extras)
    operands += list(extras)
    return pl.pallas_call(
        body, name=name, grid=(T // tm, n // tn), in_specs=specs,
        out_specs=[pl.BlockSpec((tm, D), lambda i, j: (i, 0)), out_tile],
        out_shape=[jax.ShapeDtypeStruct((T, D), BF16), jax.ShapeDtypeStruct(out_full, out_dtype)],
        scratch_shapes=[pltpu.VMEM((tm, D), BF16)], compiler_params=_params(("parallel", "arbitrary")),
    )(*operands)


def _rms_bwd_math(dy, xv, g):
    r = lax.rsqrt(jnp.mean(xv * xv, axis=-1, keepdims=True) + EPS)
    xhat = xv * r
    dxhat = dy * g
    dx = r * (dxhat - xhat * jnp.mean(dxhat * xhat, axis=-1, keepdims=True))
    return dx, dy * xhat


def _accumulate(ref, val):
    @pl.when(pl.program_id(0) == 0)
    def _():
        ref[...] = val

    @pl.when(pl.program_id(0) > 0)
    def _():
        ref[...] += val


def _rmsnorm_bwd(name, dy, x, g, res):
    def body(dy_ref, x_ref, g_ref, res_ref, dx_ref, dxb_ref, dg_ref):
        dx, dgr = _rms_bwd_math(dy_ref[...], x_ref[...], g_ref[...])
        tot = res_ref[...] + dx
        dx_ref[...] = tot
        dxb_ref[...] = tot.astype(BF16)
        _accumulate(dg_ref, jnp.sum(dgr, axis=0, keepdims=True))

    return pl.pallas_call(
        body, name=name, grid=(T // TR,), in_specs=[_rows(D), _rows(D), _whole((1, D)), _rows(D)],
        out_specs=[_rows(D), _rows(D), _whole((1, D))],
        out_shape=[jax.ShapeDtypeStruct((T, D), F32), jax.ShapeDtypeStruct((T, D), BF16),
                   jax.ShapeDtypeStruct((1, D), F32)],
        compiler_params=_params(("arbitrary",)),
    )(dy, x, g, res)


def _loss_head(x, g, target):
    def body(x_ref, g_ref, t_ref, loss_ref, dx_ref, dxb_ref, dg_ref):
        xv, gv = x_ref[...], g_ref[...]
        r = lax.rsqrt(jnp.mean(xv * xv, axis=-1, keepdims=True) + EPS)
        err = (xv * r) * gv - t_ref[...]
        part = 0.5 * jnp.sum(jnp.mean(err * err, axis=-1, keepdims=True))
        dx, dgr = _rms_bwd_math(err * (1.0 / D), xv, gv)
        dx_ref[...] = dx
        dxb_ref[...] = dx.astype(BF16)
        _accumulate(dg_ref, jnp.sum(dgr, axis=0, keepdims=True))
        _accumulate(loss_ref, jnp.full((8, LANE), part, F32))

    return pl.pallas_call(
        body, name="loss_head", grid=(T // TR,), in_specs=[_rows(D), _whole((1, D)), _rows(D)],
        out_specs=[_whole((8, LANE)), _rows(D), _rows(D), _whole((1, D))],
        out_shape=[jax.ShapeDtypeStruct((8, LANE), F32), jax.ShapeDtypeStruct((T, D), F32),
                   jax.ShapeDtypeStruct((T, D), BF16), jax.ShapeDtypeStruct((1, D), F32)],
        compiler_params=_params(("arbitrary",)),
    )(x, g, target)


MIX_OFFS = ((0, WA), (WA, WB), (WA + WB, WC))


def _mix_rows(oa_ref, ob_ref, oc_ref, g_ref):
    parts = []
    for ref, (off, w) in zip((oa_ref, ob_ref, oc_ref), MIX_OFFS):
        o = ref[...]
        r = lax.rsqrt(jnp.mean(o * o, axis=-1, keepdims=True) + EPS)
        parts.append(((o * r) * g_ref[:, off:off + w]).astype(BF16))
    return jnp.concatenate(parts, axis=1)


def _mix_bwd(name, dmixed, oa, ob, oc, gain):
    def body(dm_ref, oa_ref, ob_ref, oc_ref, g_ref, doa_ref, dob_ref, doc_ref, dg_ref):
        dgs = []
        for ref, dref, (off, w) in zip((oa_ref, ob_ref, oc_ref), (doa_ref, dob_ref, doc_ref), MIX_OFFS):
            dx, dgr = _rms_bwd_math(dm_ref[:, off:off + w], ref[...], g_ref[:, off:off + w])
            dref[...] = dx
            dgs.append(jnp.sum(dgr, axis=0, keepdims=True))
        _accumulate(dg_ref, jnp.concatenate(dgs, axis=1))

    return pl.pallas_call(
        body, name=name, grid=(T // TR,),
        in_specs=[_rows(D), _rows(WA), _rows(WB), _rows(WC), _whole((1, D))],
        out_specs=[_rows(WA), _rows(WB), _rows(WC), _whole((1, D))],
        out_shape=[jax.ShapeDtypeStruct((T, WA), F32), jax.ShapeDtypeStruct((T, WB), F32),
                   jax.ShapeDtypeStruct((T, WC), F32), jax.ShapeDtypeStruct((1, D), F32)],
        compiler_params=_params(("arbitrary",)),
    )(dmixed, oa, ob, oc, gain)


def _rope_tables():
    inv_freq = ROPE_THETA ** (-jnp.arange(0, HD, 2, dtype=F32) / HD)
    ang = jnp.arange(T, dtype=F32)[:, None] * inv_freq[None, :]
    cos, sin = jnp.cos(ang), jnp.sin(ang)
    cos2 = jnp.tile(jnp.concatenate([cos, cos], axis=1), (1, LANE // HD))
    sin2 = jnp.tile(jnp.concatenate([-sin, sin], axis=1), (1, LANE // HD))
    return cos2, sin2


def _rot_half(v):
    lane = lax.broadcasted_iota(jnp.int32, v.shape, 1)
    return jnp.where(lane % HD < HD // 2, pltpu.roll(v, LANE - HD // 2, 1), pltpu.roll(v, HD // 2, 1))


BLOCK_KINDS = tuple((rot, is_q) for _, w, rot, is_q in GROUPS for _ in range(w // LANE))
BLOCK_OF = {name: sum(w for _, w, _, _ in GROUPS[:g]) // LANE for g, (name, _, _, _) in enumerate(GROUPS)}


def _any_tile(j, tiles):
    return functools.reduce(jnp.logical_or, [j == t for t in tiles]) if tiles else False


def _rope_epilogue(j, tile, c_ref, s_ref):
    cv, sv = c_ref[...], s_ref[...]
    per, n_tiles = tile.shape[1] // LANE, IN_COLS // tile.shape[1]
    out = []
    for b in range(per):
        v = tile[:, b * LANE:(b + 1) * LANE]
        rot = _any_tile(j, [t for t in range(n_tiles) if BLOCK_KINDS[t * per + b][0]])
        is_q = _any_tile(j, [t for t in range(n_tiles) if BLOCK_KINDS[t * per + b][1]])
        if rot is not False:
            v = jnp.where(rot, v * cv + _rot_half(v) * sv, v)
        if is_q is not False:
            v = v * jnp.where(is_q, HD ** -0.5, 1.0)
        out.append(v)
    return jnp.concatenate(out, axis=1)


def _rope_bwd(name, grads, cos2, sin2):
    def body(*refs):
        ins, (c_ref, s_ref, o_ref) = refs[:9], refs[9:]
        cv, sv = c_ref[...], s_ref[...]
        off = 0
        for d_ref, (_, w, rot, is_q) in zip(ins, GROUPS):
            for b in range(w // LANE):
                v = d_ref[:, b * LANE:(b + 1) * LANE]
                if is_q:
                    v = v * (HD ** -0.5)
                if rot:
                    v = v * cv + _rot_half(v * sv)
                o_ref[:, off + b * LANE:off + (b + 1) * LANE] = v.astype(BF16)
            off += w

    return pl.pallas_call(
        body, name=name, grid=(T // TR,), in_specs=[_rows(w) for _, w, _, _ in GROUPS] + [_rows(LANE), _rows(LANE)],
        out_specs=_rows(IN_COLS), out_shape=jax.ShapeDtypeStruct((T, IN_COLS), BF16),
        compiler_params=_params(("parallel",)),
    )(*grads, cos2, sin2)


NT_DIMS = (((1,), (1,)), ((), ()))
TN_DIMS = (((0,), (0,)), ((), ()))


def _scores(q, k, bias, valid):
    s = lax.dot_general(q, k, NT_DIMS, preferred_element_type=F32)
    if bias is not None:
        s = s + bias
    if valid is not None:
        s = jnp.where(valid, s, NEG)
    return s


def _heads_fwd(heads):
    scores = [_scores(h["q"], h["k"], h.get("bias"), h.get("valid")) for h in heads]
    soft = []
    for s, h in zip(scores, heads):
        m = jnp.max(s, axis=1, keepdims=True)
        e = jnp.exp(s - m)
        l = jnp.sum(e, axis=1, keepdims=True)
        if h.get("sink") is not None:
            l = l + jnp.exp(h["sink"] - m)
        soft.append((e.astype(BF16), l, m + jnp.log(l)))
    return [(jnp.dot(e, h["v"], preferred_element_type=F32) / l, lse) for (e, l, lse), h in zip(soft, heads)]


def _heads_bwd(heads):
    dobs = [h["do"].astype(BF16) for h in heads]
    scores = [_scores(h["q"], h["k"], h.get("bias"), h.get("valid")) for h in heads]
    dps = [lax.dot_general(dob, h["v"], NT_DIMS, preferred_element_type=F32) for dob, h in zip(dobs, heads)]
    mid = []
    for s, dp, h in zip(scores, dps, heads):
        p = jnp.exp(s - h["lse"])
        delta = jnp.sum(h["do"] * h["o"], axis=1, keepdims=True)
        ds = p * (dp - delta)
        dsink = None if h.get("sink") is None else -jnp.exp(h["sink"] - h["lse"]) * delta
        mid.append((p.astype(BF16), ds, dsink))
    out = []
    for (pb, ds, dsink), dob, h in zip(mid, dobs, heads):
        dsb = ds.astype(BF16)
        out.append((jnp.dot(dsb, h["k"], preferred_element_type=F32),
                    lax.dot_general(dsb, h["q"], TN_DIMS, preferred_element_type=F32),
                    lax.dot_general(pb, dob, TN_DIMS, preferred_element_type=F32), ds, dsink))
    return out


def _per_head(cols):
    return jnp.concatenate([jnp.broadcast_to(c, (c.shape[0], HD)) for c in cols], axis=1)


DILATIONS = ((128, 1), (512, 4), (2048, 16))


BQ_A = 256
REACH_A = max(window // 2 for window, _ in DILATIONS)


def _first_key(i):
    return jnp.maximum(i * BQ_A - REACH_A, 0)


def _key_window_groups():
    groups = {}
    for i in range(T // BQ_A):
        width = min(T, (i + 1) * BQ_A + REACH_A) - max(i * BQ_A - REACH_A, 0)
        groups.setdefault(width, []).append(i)
    return groups


def _per_window(i, fn):
    for width, tiles in _key_window_groups().items():
        hit = functools.reduce(jnp.logical_or, [i == t for t in tiles])
        pl.when(hit)(functools.partial(fn, pl.multiple_of(_first_key(i), BQ_A), width))


def _dilation_bias():
    def body(o_ref):
        i = pl.program_id(0)
        t = i * BQ_A + lax.broadcasted_iota(jnp.int32, (BQ_A, T), 0)
        ad = jnp.abs(t - (_first_key(i) + lax.broadcasted_iota(jnp.int32, (BQ_A, T), 1)))
        count = jnp.zeros((BQ_A, T), jnp.int32)
        for window, r in DILATIONS:
            count += jnp.where(((ad & (r - 1)) == 0) & (ad <= window // 2), 1, 0)
        logs = jnp.where(count == 2, jnp.log(2.0), jnp.where(count == 3, jnp.log(3.0), 0.0)).astype(F32)
        o_ref[...] = jnp.where(count == 0, NEG, logs)

    return pl.pallas_call(
        body, name="dilation_bias", grid=(T // BQ_A,), out_specs=pl.BlockSpec((BQ_A, T), lambda i: (i, 0)),
        out_shape=jax.ShapeDtypeStruct((T, T), F32), compiler_params=_params(("parallel",)),
    )()


def _qkv_rows(rows, group):
    return pl.BlockSpec((rows, LANE), lambda p, i: (i, BLOCK_OF[group] + p))


def _qkv_all(group):
    return pl.BlockSpec((T, LANE), lambda p, i: (0, BLOCK_OF[group] + p))


def _attn_a_fwd(name, qkv, bias):
    def body(q_ref, k_ref, v_ref, b_ref, o_ref, lse_ref):
        def tile(first, width):
            b = b_ref[:, :width]
            outs = _heads_fwd([dict(q=q_ref[:, h * HD:(h + 1) * HD], k=k_ref[pl.ds(first, width), h * HD:(h + 1) * HD],
                                    v=v_ref[pl.ds(first, width), h * HD:(h + 1) * HD], bias=b) for h in range(2)])
            o_ref[...] = jnp.concatenate([o for o, _ in outs], axis=1)
            lse_ref[...] = _per_head([lse for _, lse in outs])

        _per_window(pl.program_id(1), tile)

    qs = pl.BlockSpec((BQ_A, LANE), lambda p, i: (i, p))
    ks = pl.BlockSpec((T, LANE), lambda p, i: (0, p))
    return pl.pallas_call(
        body, name=name, grid=(HA // 2, T // BQ_A),
        in_specs=[_qkv_rows(BQ_A, "qa"), _qkv_all("ka"), _qkv_all("va"), pl.BlockSpec((BQ_A, T), lambda p, i: (i, 0))],
        out_specs=[qs, qs],
        out_shape=[jax.ShapeDtypeStruct((T, WA), F32)] * 2, compiler_params=_params(("parallel", "parallel")),
    )(qkv, qkv, qkv, bias)


def _attn_a_bwd(name, qkv, oa, lse, doa, bias):
    def body(q_ref, k_ref, v_ref, o_ref, lse_ref, do_ref, b_ref, dq_ref, dk_ref, dv_ref):
        @pl.when(pl.program_id(1) == 0)
        def _():
            dk_ref[...] = jnp.zeros_like(dk_ref)
            dv_ref[...] = jnp.zeros_like(dv_ref)

        def tile(first, width):
            b = b_ref[:, :width]
            keys = pl.ds(first, width)
            sls = [slice(h * HD, (h + 1) * HD) for h in range(2)]
            res = _heads_bwd([dict(q=q_ref[:, sl], k=k_ref[keys, sl], v=v_ref[keys, sl], o=o_ref[:, sl],
                                   do=do_ref[:, sl], lse=lse_ref[:, sl.start:sl.start + 1], bias=b) for sl in sls])
            dq_ref[...] = jnp.concatenate([r[0] for r in res], axis=1)
            dk_ref[keys, :] += jnp.concatenate([r[1] for r in res], axis=1)
            dv_ref[keys, :] += jnp.concatenate([r[2] for r in res], axis=1)

        _per_window(pl.program_id(1), tile)

    qs = pl.BlockSpec((BQ_A, LANE), lambda p, i: (i, p))
    ks = pl.BlockSpec((T, LANE), lambda p, i: (0, p))
    return pl.pallas_call(
        body, name=name, grid=(HA // 2, T // BQ_A),
        in_specs=[_qkv_rows(BQ_A, "qa"), _qkv_all("ka"), _qkv_all("va"), qs, qs, qs,
                  pl.BlockSpec((BQ_A, T), lambda p, i: (i, 0))], out_specs=[qs, ks, ks],
        out_shape=[jax.ShapeDtypeStruct((T, WA), F32)] * 3, compiler_params=_params(("parallel", "arbitrary")),
    )(qkv, qkv, qkv, oa, lse, doa, bias)


BQ_B = 128
SPAN_B = BQ_B + 2 * WINDOW_B


def _window_b(i):
    start = pl.multiple_of(jnp.clip(i * BQ_B - WINDOW_B, 0, T - SPAN_B), BQ_B)
    qpos = i * BQ_B + lax.broadcasted_iota(jnp.int32, (BQ_B, SPAN_B), 0)
    kpos = start + lax.broadcasted_iota(jnp.int32, (BQ_B, SPAN_B), 1)
    return start, jnp.abs(qpos - kpos) <= WINDOW_B


GROUP_B = HB // HKV


def _stack_group(ref, g):
    return jnp.concatenate([ref[:, h * HD:(h + 1) * HD] for h in range(g * GROUP_B, (g + 1) * GROUP_B)], axis=0)


def _sink_column(sink_ref, g):
    return jnp.concatenate([jnp.full((BQ_B, 1), sink_ref[h], F32) for h in range(g * GROUP_B, (g + 1) * GROUP_B)],
                           axis=0)


def _unstack(stacked):
    return [s[j * BQ_B:(j + 1) * BQ_B] for s in stacked for j in range(GROUP_B)]


def _attn_b_fwd(name, qb, kb, vb, sink):
    def body(sink_ref, q_ref, k_ref, v_ref, o_ref, lse_ref):
        start, valid = _window_b(pl.program_id(0))
        valid = jnp.concatenate([valid] * GROUP_B, axis=0)
        kw, vw = k_ref[pl.ds(start, SPAN_B), :], v_ref[pl.ds(start, SPAN_B), :]
        outs = _heads_fwd([dict(q=_stack_group(q_ref, g), k=kw[:, g * HD:(g + 1) * HD], v=vw[:, g * HD:(g + 1) * HD],
                                valid=valid, sink=_sink_column(sink_ref, g)) for g in range(HKV)])
        o_ref[...] = jnp.concatenate(_unstack([o for o, _ in outs]), axis=1)
        lse_ref[...] = _per_head(_unstack([lse for _, lse in outs]))

    qs = pl.BlockSpec((BQ_B, WB), lambda i: (i, 0))
    return pl.pallas_call(
        body, name=name, grid=(T // BQ_B,),
        in_specs=[pl.BlockSpec(memory_space=pltpu.SMEM), qs, _whole((T, WKV)), _whole((T, WKV))],
        out_specs=[qs, qs],
        out_shape=[jax.ShapeDtypeStruct((T, WB), F32)] * 2, compiler_params=_params(("parallel",)),
    )(sink, qb, kb, vb)


def _attn_b_bwd(name, qb, kb, vb, ob, lse, dob, sink):
    def body(sink_ref, q_ref, k_ref, v_ref, o_ref, lse_ref, do_ref, dq_ref, dk_ref, dv_ref, dsink_ref):
        i = pl.program_id(0)
        start, valid = _window_b(i)
        valid = jnp.concatenate([valid] * GROUP_B, axis=0)
        kw, vw = k_ref[pl.ds(start, SPAN_B), :], v_ref[pl.ds(start, SPAN_B), :]
        res = _heads_bwd([dict(q=_stack_group(q_ref, g), k=kw[:, g * HD:(g + 1) * HD], v=vw[:, g * HD:(g + 1) * HD],
                               o=_stack_group(o_ref, g), do=_stack_group(do_ref, g),
                               lse=jnp.concatenate([lse_ref[:, h * HD:h * HD + 1]
                                                    for h in range(g * GROUP_B, (g + 1) * GROUP_B)], axis=0),
                               valid=valid, sink=_sink_column(sink_ref, g)) for g in range(HKV)])
        dks, dvs = [r[1] for r in res], [r[2] for r in res]
        lane = lax.broadcasted_iota(jnp.int32, (1, LANE), 1)
        dsink = jnp.zeros((1, LANE), F32)
        for h, rows in enumerate(_unstack([r[4] for r in res])):
            dsink += jnp.where(lane == h, jnp.sum(rows), 0.0)
        dq_ref[...] = jnp.concatenate(_unstack([r[0] for r in res]), axis=1)

        @pl.when(i == 0)
        def _():
            dk_ref[...] = jnp.zeros_like(dk_ref)
            dv_ref[...] = jnp.zeros_like(dv_ref)
            dsink_ref[...] = jnp.zeros_like(dsink_ref)

        dk_ref[pl.ds(start, SPAN_B), :] += jnp.concatenate(dks, axis=1)
        dv_ref[pl.ds(start, SPAN_B), :] += jnp.concatenate(dvs, axis=1)
        dsink_ref[...] += dsink

    qs = pl.BlockSpec((BQ_B, WB), lambda i: (i, 0))
    return pl.pallas_call(
        body, name=name, grid=(T // BQ_B,),
        in_specs=[pl.BlockSpec(memory_space=pltpu.SMEM), qs, _whole((T, WKV)), _whole((T, WKV)), qs, qs, qs],
        out_specs=[qs, _whole((T, WKV)), _whole((T, WKV)), _whole((1, LANE))],
        out_shape=[jax.ShapeDtypeStruct((T, WB), F32), jax.ShapeDtypeStruct((T, WKV), F32),
                   jax.ShapeDtypeStruct((T, WKV), F32), jax.ShapeDtypeStruct((1, LANE), F32)],
        compiler_params=_params(("arbitrary",)),
    )(sink, qb, kb, vb, ob, lse, dob)


SPAN_C = NA_ROWS * GRID_W


def _row_start(r):
    return jnp.clip(r - NA_ROWS // 2, 0, ROWS - NA_ROWS)


def _off_index(r):
    return _row_start(r) - r + (NA_ROWS - 1)


N_TAB = 16
RPS_FWD, RPS_BWD = 4, 8


def _rpb_tables(name, rpb):
    circ = jnp.concatenate([rpb[..., NA_COLS - 1:], jnp.zeros(rpb.shape[:2] + (LANE - (2 * NA_COLS - 1),), F32),
                            rpb[..., :NA_COLS - 1]], axis=-1)
    circ = jnp.pad(circ, ((0, 0), (0, N_TAB + 1 - circ.shape[1]), (0, 0)))

    def body(w_ref, o_ref):
        c = lax.broadcasted_iota(jnp.int32, (GRID_W, LANE), 0)
        lane = lax.broadcasted_iota(jnp.int32, (GRID_W, LANE), 1)
        cs = jnp.clip(c - NA_COLS // 2, 0, GRID_W - NA_COLS)
        valid = (lane % GRID_W >= cs) & (lane % GRID_W < cs + NA_COLS)
        toep = [pltpu.roll(jnp.broadcast_to(w_ref[a:a + 1, :], (GRID_W, LANE)), 0, 1, stride=1, stride_axis=0)
                for a in range(N_TAB + 1)]
        for a in range(N_TAB):
            pair = jnp.where(lane < GRID_W, toep[a], pltpu.roll(toep[a + 1], GRID_W, 1))
            o_ref[a] = jnp.where(valid, pair, NEG)

    return pl.pallas_call(
        body, name=name, grid=(HC,),
        in_specs=[pl.BlockSpec((None, N_TAB + 1, LANE), lambda h: (h, 0, 0))],
        out_specs=pl.BlockSpec((None, N_TAB, GRID_W, LANE), lambda h: (h, 0, 0, 0)),
        out_shape=jax.ShapeDtypeStruct((HC, N_TAB, GRID_W, LANE), F32), compiler_params=_params(("parallel",)),
    )(circ)


def _bias_c(t_ref, h, d):
    return jnp.concatenate([t_ref[h, d + k] for k in range(0, NA_ROWS, 2)], axis=1)


def _attn_c_fwd(name, qkv, tables):
    RPS = RPS_FWD

    def body(q_ref, k_ref, v_ref, t_ref, o_ref, lse_ref):
        heads = []
        for rr in range(RPS):
            r = pl.program_id(1) * RPS + rr
            rows = slice(rr * GRID_W, (rr + 1) * GRID_W)
            start = pl.multiple_of(_row_start(r) * GRID_W, GRID_W)
            kw, vw = k_ref[pl.ds(start, SPAN_C), :], v_ref[pl.ds(start, SPAN_C), :]
            heads += [dict(q=q_ref[rows, h * HD:(h + 1) * HD], k=kw[:, h * HD:(h + 1) * HD], v=vw[:, h * HD:(h + 1) * HD],
                           bias=_bias_c(t_ref, h, _off_index(r))) for h in range(2)]
        outs = _heads_fwd(heads)
        for rr in range(RPS):
            rows = slice(rr * GRID_W, (rr + 1) * GRID_W)
            o_ref[rows, :] = jnp.concatenate([o for o, _ in outs[2 * rr:2 * rr + 2]], axis=1)
            lse_ref[rows, :] = _per_head([lse for _, lse in outs[2 * rr:2 * rr + 2]])

    qs = pl.BlockSpec((RPS * GRID_W, LANE), lambda p, r: (r, p))
    ks = pl.BlockSpec((T, LANE), lambda p, r: (0, p))
    ts = pl.BlockSpec((2, N_TAB, GRID_W, LANE), lambda p, r: (p, 0, 0, 0))
    return pl.pallas_call(
        body, name=name, grid=(HC // 2, ROWS // RPS),
        in_specs=[_qkv_rows(RPS * GRID_W, "qc"), _qkv_all("kc"), _qkv_all("vc"), ts], out_specs=[qs, qs],
        out_shape=[jax.ShapeDtypeStruct((T, WC), F32)] * 2, compiler_params=_params(("parallel", "parallel")),
    )(qkv, qkv, qkv, tables)


def _attn_c_bwd(name, qkv, oc, lse, doc, tables):
    RPS = RPS_BWD

    def body(q_ref, k_ref, v_ref, o_ref, lse_ref, do_ref, t_ref, dq_ref, dk_ref, dv_ref, dt_ref):
        @pl.when(pl.program_id(1) == 0)
        def _():
            dk_ref[...] = jnp.zeros_like(dk_ref)
            dv_ref[...] = jnp.zeros_like(dv_ref)
            dt_ref[...] = jnp.zeros_like(dt_ref)

        heads, where = [], []
        for rr in range(RPS):
            r = pl.program_id(1) * RPS + rr
            rows = slice(rr * GRID_W, (rr + 1) * GRID_W)
            d = _off_index(r)
            start = pl.multiple_of(_row_start(r) * GRID_W, GRID_W)
            kw, vw = k_ref[pl.ds(start, SPAN_C), :], v_ref[pl.ds(start, SPAN_C), :]
            where.append((rows, d, start))
            for h in range(2):
                sl = slice(h * HD, (h + 1) * HD)
                heads.append(dict(q=q_ref[rows, sl], k=kw[:, sl], v=vw[:, sl], o=o_ref[rows, sl], do=do_ref[rows, sl],
                                  lse=lse_ref[rows, h * HD:h * HD + 1], bias=_bias_c(t_ref, h, d)))
        res = _heads_bwd(heads)
        for rr, (rows, d, start) in enumerate(where):
            pair = res[2 * rr:2 * rr + 2]
            for h in range(2):
                for k in range(0, NA_ROWS, 2):
                    dt_ref[h, d + k] += pair[h][3][:, k * GRID_W:(k + 2) * GRID_W]
            dq_ref[rows, :] = jnp.concatenate([p[0] for p in pair], axis=1)
            dk_ref[pl.ds(start, SPAN_C), :] += jnp.concatenate([p[1] for p in pair], axis=1)
            dv_ref[pl.ds(start, SPAN_C), :] += jnp.concatenate([p[2] for p in pair], axis=1)

    qs = pl.BlockSpec((RPS * GRID_W, LANE), lambda p, r: (r, p))
    ks = pl.BlockSpec((T, LANE), lambda p, r: (0, p))
    ts = pl.BlockSpec((2, N_TAB, GRID_W, LANE), lambda p, r: (p, 0, 0, 0))
    return pl.pallas_call(
        body, name=name, grid=(HC // 2, ROWS // RPS),
        in_specs=[_qkv_rows(RPS * GRID_W, "qc"), _qkv_all("kc"), _qkv_all("vc"), qs, qs, qs, ts],
        out_specs=[qs, ks, ks, ts],
        out_shape=[jax.ShapeDtypeStruct((T, WC), F32)] * 3 + [jax.ShapeDtypeStruct((HC, N_TAB, GRID_W, LANE), F32)],
        compiler_params=_params(("parallel", "arbitrary")),
    )(qkv, qkv, qkv, oc, lse, doc, tables)


def _split3(v):
    hi = v.astype(BF16)
    r1 = v - hi.astype(F32)
    mid = r1.astype(BF16)
    lo = (r1 - mid.astype(F32)).astype(BF16)
    return hi, mid, lo


def _rpb_reduce(name, dtables):
    x = dtables.reshape(HC, N_TAB, GRID_W * LANE)
    c = jnp.arange(GRID_W)[:, None]
    lane = jnp.arange(LANE)[None, :]
    col = (lane // GRID_W) * LANE + jnp.clip(lane % GRID_W - c + (NA_COLS - 1), 0, 2 * NA_COLS - 2)
    col_onehot = (col.reshape(-1)[:, None] == jnp.arange(2 * LANE)[None, :]).astype(BF16)
    a2 = jnp.arange(N_TAB)[None, :]
    row_onehot = jnp.concatenate([(jnp.arange(16)[:, None] == a2 + u) & (a2 < 2 * NA_ROWS - 2) for u in range(2)],
                                 axis=1).astype(BF16)

    def body(x_ref, e_ref, f_ref, o_ref):
        y = sum(jnp.dot(part, e_ref[...], preferred_element_type=F32) for part in _split3(x_ref[...]))
        z = jnp.concatenate([y[:, :LANE], y[:, LANE:]], axis=0)
        o_ref[...] = sum(jnp.dot(f_ref[...], part, preferred_element_type=F32) for part in _split3(z))

    out = pl.pallas_call(
        body, name=name, grid=(HC,),
        in_specs=[pl.BlockSpec((None, N_TAB, GRID_W * LANE), lambda h: (h, 0, 0)),
                  _whole((GRID_W * LANE, 2 * LANE)), _whole((16, 2 * N_TAB))],
        out_specs=pl.BlockSpec((None, 16, LANE), lambda h: (h, 0, 0)),
        out_shape=jax.ShapeDtypeStruct((HC, 16, LANE), F32), compiler_params=_params(("parallel",)),
    )(x, col_onehot, row_onehot)
    return out[:, :2 * NA_ROWS - 1, :2 * NA_COLS - 1]


TC = 128
CHUNK = 128
MARGIN = 8


def _shift_down(v, rows):
    return jnp.where(rows == 0, 0.0, pltpu.roll(v, 1, 0))


def _shift_up(v, rows):
    return jnp.where(rows == T - 1, 0.0, pltpu.roll(v, T - 1, 0))


def _conv(v, w, b, rows):
    return _shift_down(v, rows) * w[0:1] + v * w[1:2] + _shift_up(v, rows) * w[2:3] + b


FWD_BLOCKS = 4
BWD_BLOCKS = 1


def _ffn_mid_fwd(name, up, conv_w, conv_b):
    wide = FWD_BLOCKS * TC

    def body(xg_ref, xv_ref, wg_ref, wv_ref, bg_ref, bv_ref, o_ref):
        rows = lax.broadcasted_iota(jnp.int32, (T, TC), 0)
        for b in range(FWD_BLOCKS):
            lanes = slice(b * TC, (b + 1) * TC)
            ug = _conv(xg_ref[b], wg_ref[:, lanes], bg_ref[:, lanes], rows)
            uv = _conv(xv_ref[b], wv_ref[:, lanes], bv_ref[:, lanes], rows)
            o_ref[:, lanes] = (ug * jax.nn.sigmoid(ug) * uv).astype(BF16)

    gate = lambda shape: pl.BlockSpec(shape, lambda j: (0, j))
    val = lambda shape: pl.BlockSpec(shape, lambda j: (0, j + DFF // wide))
    return pl.pallas_call(
        body, name=name, grid=(DFF // wide,),
        in_specs=[pl.BlockSpec((FWD_BLOCKS, T, TC), lambda j: (j, 0, 0)),
                  pl.BlockSpec((FWD_BLOCKS, T, TC), lambda j: (j + DFF // wide, 0, 0)),
                  gate((3, wide)), val((3, wide)), gate((1, wide)), val((1, wide))],
        out_specs=pl.BlockSpec((T, wide), lambda j: (0, j)),
        out_shape=jax.ShapeDtypeStruct((T, DFF), BF16), compiler_params=_params(("parallel",)),
    )(up, up, conv_w, conv_w, conv_b, conv_b)


def _ffn_mid_bwd(name, dact, up, conv_w, conv_b):
    window = CHUNK + 2 * MARGIN
    centre = slice(MARGIN, MARGIN + CHUNK)

    def shifted(v):
        return pltpu.roll(v, 1, 0), pltpu.roll(v, window - 1, 0)

    def fold(v):
        return jnp.sum(v[centre].reshape(CHUNK // 8, 8, TC), axis=0)

    wide = BWD_BLOCKS * TC

    def body(da_ref, xg_ref, xv_ref, wg_ref, wv_ref, bg_ref, bv_ref, dx_ref, dw_ref, db_ref, dap, xgp, xvp):
        for b in range(BWD_BLOCKS):
            block(b, da_ref, xg_ref, xv_ref, wg_ref, wv_ref, bg_ref, bv_ref, dx_ref, dw_ref, db_ref, dap, xgp, xvp)

    def block(b, da_ref, xg_ref, xv_ref, wg_ref, wv_ref, bg_ref, bv_ref, dx_ref, dw_ref, db_ref, dap, xgp, xvp):
        lanes = slice(b * TC, (b + 1) * TC)
        for src, pad in ((da_ref[:, lanes], dap), (xg_ref[b], xgp), (xv_ref[b], xvp)):
            pad[0:MARGIN, :] = jnp.zeros((MARGIN, TC), F32)
            pad[MARGIN:MARGIN + T, :] = src
            pad[MARGIN + T:, :] = jnp.zeros((MARGIN, TC), F32)
        wg, wv, bg, bv = wg_ref[:, lanes], wv_ref[:, lanes], bg_ref[:, lanes], bv_ref[:, lanes]

        def chunk(c, sums):
            r0 = pl.multiple_of(c * CHUNK, CHUNK)
            da, xg, xv = dap[pl.ds(r0, window), :], xgp[pl.ds(r0, window), :], xvp[pl.ds(r0, window), :]
            xg_prev, xg_next = shifted(xg)
            xv_prev, xv_next = shifted(xv)
            ug = xg_prev * wg[0:1] + xg * wg[1:2] + xg_next * wg[2:3] + bg
            uv = xv_prev * wv[0:1] + xv * wv[1:2] + xv_next * wv[2:3] + bv
            sg = jax.nn.sigmoid(ug)
            dug = da * uv * (sg * (1.0 + ug * (1.0 - sg)))
            duv = da * (ug * sg)
            out = []
            for half, (x_prev, x, x_next, w, du) in enumerate(((xg_prev, xg, xg_next, wg, dug),
                                                               (xv_prev, xv, xv_next, wv, duv))):
                du_prev, du_next = shifted(du)
                dx = du_next * w[0:1] + du * w[1:2] + du_prev * w[2:3]
                dx_ref[half, pl.ds(r0, CHUNK), lanes] = dx[centre].astype(BF16)
                out += [fold(x_prev * du), fold(x * du), fold(x_next * du), fold(du)]
            return tuple(s + o for s, o in zip(sums, out))

        sums = lax.fori_loop(0, T // CHUNK, chunk, tuple(jnp.zeros((8, TC), F32) for _ in range(8)))
        rows = [jnp.sum(s, axis=0, keepdims=True) for s in sums]
        for half in range(2):
            dw_ref[half, :, lanes] = jnp.concatenate(rows[4 * half:4 * half + 3], axis=0)
            db_ref[half, :, lanes] = rows[4 * half + 3]

    gate = lambda shape: pl.BlockSpec(shape, lambda j: (0, j))
    val = lambda shape: pl.BlockSpec(shape, lambda j: (0, j + DFF // wide))
    return pl.pallas_call(
        body, name=name, grid=(DFF // wide,),
        in_specs=[gate((T, wide)), pl.BlockSpec((BWD_BLOCKS, T, TC), lambda j: (j, 0, 0)),
                  pl.BlockSpec((BWD_BLOCKS, T, TC), lambda j: (j + DFF // wide, 0, 0)),
                  gate((3, wide)), val((3, wide)), gate((1, wide)), val((1, wide))],
        out_specs=[pl.BlockSpec((2, T, wide), lambda j: (0, 0, j)), pl.BlockSpec((2, 3, wide), lambda j: (0, 0, j)),
                   pl.BlockSpec((2, 1, wide), lambda j: (0, 0, j))],
        out_shape=[jax.ShapeDtypeStruct((2, T, DFF), BF16), jax.ShapeDtypeStruct((2, 3, DFF), F32),
                   jax.ShapeDtypeStruct((2, 1, DFF), F32)],
        scratch_shapes=[pltpu.VMEM((T + 2 * MARGIN, TC), F32)] * 3,
        compiler_params=_params(("parallel",)),
    )(dact, up, up, conv_w, conv_w, conv_b, conv_b)


def _dup_spec(tm, nj):
    per = DFF // nj
    return pl.BlockSpec((None, tm, nj), lambda a, b, j: (j // per, 0 if tm == T else b, j % per))


def _dup_spec_tn(tm, nj):
    per = DFF // nj
    return pl.BlockSpec((None, tm, nj), lambda j, kt, r: (j // per, 0, j % per))


def _adamw_math(w, g, m, v):
    m = ADAM_B1 * m + (1.0 - ADAM_B1) * g
    v = ADAM_B2 * v + (1.0 - ADAM_B2) * (g * g)
    m_hat = m / (1.0 - ADAM_B1 ** ADAM_STEP)
    v_hat = v / (1.0 - ADAM_B2 ** ADAM_STEP)
    delta = -ADAM_LR * (m_hat / (jnp.sqrt(v_hat) + ADAM_EPS) + ADAM_WD * w)
    return delta, m, v


ADAM_BLOCK = 256 * 1408


def _adamw_sharded(name, w, m, v, parts):
    _, r, c = w.shape
    tr = max(t for t in range(16, r + 1, 16) if r % t == 0 and t * c <= ADAM_BLOCK)

    def body(w_ref, m_ref, v_ref, p0_ref, p1_ref, g_ref, d_ref, nm_ref, nv_ref):
        def run(p_ref):
            g = p_ref[0].astype(F32)
            for k in range(1, N_DEV):
                g = g + p_ref[k].astype(F32)
            d, nm, nv = _adamw_math(w_ref[...], g, m_ref[...], v_ref[...])
            g_ref[...] = g
            d_ref[...] = d
            nm_ref[...] = nm
            nv_ref[...] = nv

        @pl.when(pl.program_id(0) == 0)
        def _():
            run(p0_ref)

        @pl.when(pl.program_id(0) == 1)
        def _():
            run(p1_ref)

    ws = pl.BlockSpec((None, tr, c), lambda l, i: (l, i, 0))
    p0 = pl.BlockSpec((N_DEV, tr, c), lambda l, i: (0, jnp.where(l == 0, i, r // tr - 1), 0))
    p1 = pl.BlockSpec((N_DEV, tr, c), lambda l, i: (0, jnp.where(l == 1, i, 0), 0))
    return pl.pallas_call(
        body, name=name, grid=(DEPTH, r // tr), in_specs=[ws, ws, ws, p0, p1], out_specs=[ws] * 4,
        out_shape=[jax.ShapeDtypeStruct(w.shape, F32)] * 4, compiler_params=_params(("arbitrary", "arbitrary")),
    )(w, m, v, *parts)


def _sum_devices(name, parts):
    r = parts.shape[1]

    def body(p_ref, o_ref):
        g = p_ref[0]
        for k in range(1, N_DEV):
            g = g + p_ref[k]
        o_ref[...] = g

    return pl.pallas_call(
        body, name=name, in_specs=[pl.BlockSpec((N_DEV, r, LANE), lambda: (0, 0, 0))],
        out_specs=pl.BlockSpec((r, LANE), lambda: (0, 0)), out_shape=jax.ShapeDtypeStruct((r, LANE), F32),
        compiler_params=_params(),
    )(parts)


def _adamw_small(name, ws, gs, ms, vs):
    n = len(ws)
    shapes = [w.shape for w in ws]
    ws, gs, ms, vs = ([a.reshape(1, -1) if a.ndim == 1 else a for a in arrs] for arrs in (ws, gs, ms, vs))
    specs = [pl.BlockSpec(memory_space=pltpu.VMEM)] * n

    def body(*refs):
        for i in range(n):
            w_ref, g_ref, m_ref, v_ref = (refs[k * n + i] for k in range(4))
            d, nm, nv = _adamw_math(w_ref[...], g_ref[...], m_ref[...], v_ref[...])
            refs[4 * n + i][...] = d
            refs[5 * n + i][...] = nm
            refs[6 * n + i][...] = nv

    outs = pl.pallas_call(
        body, name=name, in_specs=specs * 4, out_specs=specs * 3,
        out_shape=[jax.ShapeDtypeStruct(w.shape, F32) for w in ws] * 3, compiler_params=_params(),
    )(*ws, *gs, *ms, *vs)
    outs = [o.reshape(shapes[i % n]) for i, o in enumerate(outs)]
    return outs[:n], outs[n:2 * n], outs[2 * n:]


def _pack(arrays):
    flat = jnp.concatenate([a.reshape(-1) for a in arrays])
    pad = (-flat.shape[0]) % (8 * LANE)
    return jnp.pad(flat, (0, pad)).reshape(-1, LANE)


def _unpack(buf, shapes):
    flat, out, off = buf.reshape(-1), [], 0
    for s in shapes:
        n = 1
        for d in s:
            n *= d
        out.append(flat[off:off + n].reshape(s))
        off += n
    return out


def _local_step(x, target, small, weights, conv_w_full, hand_over, used):
    cos2, sin2 = _rope_tables()
    bias_a = _dilation_bias()
    tables = [_rpb_tables(f"rpb_tables_{l}", small["rpb_c"][l]) for l in range(DEPTH)]
    saved, carry = [], 0.0
    for l in range(DEPTH):
        g1, g2 = small["ln_attn"][l][None] + carry, small["ln_ffn"][l][None]
        gain, sink, cb = small["mix_gain"][l][None], small["sink_b"][l], small["conv_b"][l][None]
        cw = conv_w_full[l]
        bias = tables[l]
        h1, qkv = _prologue_matmul(f"proj_in_{l}", _rmsnorm_rows, [x, g1], [D, None],
                                   weights("w_in", l, [cos2, sin2, bias_a] + tables if l == 0 else x),
                                   (D, 1024), lambda j: (0, j), 1024, epilogue=_rope_epilogue, extras=(cos2, sin2),
                                   out_dtype=BF16)
        zero = used(f"proj_in_{l}", qkv)
        qb, kb, vb = (qkv[:, BLOCK_OF[n] * LANE:BLOCK_OF[n] * LANE + w] for n, w in (("qb", WB), ("kb", WKV), ("vb", WKV)))
        oa, lse_a = _attn_a_fwd(f"attn_a_{l}", qkv, bias_a)
        ob, lse_b = _attn_b_fwd(f"attn_b_{l}", qb, kb, vb, sink + zero)
        oc, lse_c = _attn_c_fwd(f"attn_c_{l}", qkv, bias)
        mixed, x_mid = _prologue_matmul(f"proj_out_{l}", _mix_rows, [oa, ob, oc, gain + used(f"attn_{l}", oc)],
                                        [WA, WB, WC, None],
                                        weights("w_out", l, oc), (N_DEV, D // N_DEV, 512), lambda j: (0, 0, j), 512,
                                        res=x)
        h2, up = _prologue_matmul(f"ffn_up_{l}", _rmsnorm_rows, [x_mid, g2 + used(f"proj_out_{l}", x_mid)], [D, None],
                                  weights("w_up", l, x_mid), (D, 1024), lambda j: (0, j), 1024, blocked_out=True)
        act = _ffn_mid_fwd(f"ffn_mid_{l}", up, cw, cb + used(f"ffn_up_{l}", up))
        x_out = _nn_rows(f"ffn_down_{l}", act, weights("w_down", l, act), x_mid, 4, 1024, 1024)
        carry = used(f"ffn_down_{l}", x_out)
        saved.append(dict(x=x, h1=h1, qkv=(qkv, qb, kb, vb), o=(oa, ob, oc), lse=(lse_a, lse_b, lse_c), mixed=mixed,
                          x_mid=x_mid, h2=h2, up=up, act=act, g1=g1, g2=g2, gain=gain, sink=sink, cb=cb, cw=cw, bias=bias))
        x = x_out

    loss8, dx, dxb, d_ln_final = _loss_head(x, small["ln_final"][None], target)
    sgrads = [None] * DEPTH
    for l in reversed(range(DEPTH)):
        s = saved[l]
        qkv, qb, kb, vb = s["qkv"]
        oa, ob, oc = s["o"]
        wg_in, wg_out = weights("w_in", l, None), weights("w_out", l, None)
        wg_up, wg_down = weights("w_up", l, None), weights("w_down", l, None)
        g_down = _tn_rows(f"wgrad_down_{l}", s["act"], dxb, wg_down.shape[1], 2, 512)
        zero = hand_over("w_down", l, g_down)
        dact = _nt_rows(f"dgrad_down_{l}", dxb, wg_down, 4, 512)
        dup, d_cw, d_cb = _ffn_mid_bwd(f"ffn_mid_bwd_{l}", dact, s["up"], s["cw"], s["cb"] + zero)
        g_up = _tn_cols(f"wgrad_up_{l}", s["h2"], dup, _dup_spec_tn, 2 * DFF, DFF // 2)
        zero = hand_over("w_up", l, g_up)
        dh2 = _nt_cols(f"dgrad_up_{l}", dup, _dup_spec, wg_up, DFF // 2)
        dx, dxb, d_g2 = _rmsnorm_bwd(f"norm_ffn_bwd_{l}", dh2, s["x_mid"], s["g2"] + zero, dx)
        g_out = _tn_rows(f"wgrad_out_{l}", s["mixed"], dxb, wg_out.shape[1], 2, D)
        zero = hand_over("w_out", l, g_out)
        dmixed = _nt_rows(f"dgrad_out_{l}", dxb, wg_out, 2, T)
        doa, dob, doc, d_gain = _mix_bwd(f"mix_bwd_{l}", dmixed, oa, ob, oc, s["gain"] + zero)
        lse_a, lse_b, lse_c = s["lse"]
        dqa, dka, dva = _attn_a_bwd(f"attn_a_bwd_{l}", qkv, oa, lse_a, doa, bias_a)
        dqb, dkb, dvb, d_sink = _attn_b_bwd(f"attn_b_bwd_{l}", qb, kb, vb, ob, lse_b, dob, s["sink"])
        dqc, dkc, dvc, d_bias = _attn_c_bwd(f"attn_c_bwd_{l}", qkv, oc, lse_c, doc, s["bias"])
        d_rpb = _rpb_reduce(f"rpb_reduce_{l}", d_bias)
        dproj = _rope_bwd(f"rope_bwd_{l}", (dqa, dka, dva, dqb, dkb, dvb, dqc, dkc, dvc), cos2, sin2)
        g_in = _tn_cols(f"wgrad_in_{l}", s["h1"], dproj,
                        lambda tm, tn: pl.BlockSpec((tm, tn), lambda j, kt, r: (0, j)), IN_COLS, 1024)
        zero = hand_over("w_in", l, g_in)
        dh1 = _nt_cols(f"dgrad_in_{l}", dproj, lambda tm, nc: pl.BlockSpec((tm, nc), lambda kt, i, j: (i, j)), wg_in,
                       IN_COLS // 2)
        dx, dxb, d_g1 = _rmsnorm_bwd(f"norm_attn_bwd_{l}", dh1, s["x"], s["g1"] + zero, dx)
        sgrads[l] = dict(ln_attn=d_g1[0], sink_b=d_sink[0, :HB], rpb_c=d_rpb, mix_gain=d_gain[0], ln_ffn=d_g2[0],
                         conv_w=d_cw.transpose(1, 0, 2).reshape(3, 2 * DFF), conv_b=d_cb.reshape(2 * DFF))
    return loss8[0, 0], dx, d_ln_final[0], sgrads


SMALL_NAMES = ("ln_attn", "sink_b", "rpb_c", "mix_gain", "ln_ffn", "conv_b")


def kernel(x, ln_attn, w_in, sink_b, rpb_c, mix_gain, w_out, ln_ffn, w_up, conv_w, conv_b, w_down, ln_final, loss_target, m_ln_attn, m_w_in, m_sink_b, m_rpb_c, m_mix_gain, m_w_out, m_ln_ffn, m_w_up, m_conv_w, m_conv_b, m_w_down, m_ln_final, v_ln_attn, v_w_in, v_sink_b, v_rpb_c, v_mix_gain, v_w_out, v_ln_ffn, v_w_up, v_conv_w, v_conv_b, v_w_down, v_ln_final):
    me = 4 * lax.axis_index("x") + 2 * lax.axis_index("y") + lax.axis_index("c")
    small = dict(ln_attn=ln_attn, sink_b=sink_b, rpb_c=rpb_c, mix_gain=mix_gain, ln_ffn=ln_ffn, conv_b=conv_b,
                 ln_final=ln_final)

    names = ("w_in", "w_out", "w_up", "w_down")
    shards = dict(w_in=w_in, w_out=w_out, w_up=w_up, w_down=w_down)
    order = [(n, l) for l in range(DEPTH) for n in names]
    conv_key = ("conv_w", 0)
    started, arrived, forwarded, gathered = {}, {}, {}, {}

    def side_by_side(k):
        return k[0] in ("w_in", "w_up")

    def slot_of(k):
        return _col_slot(shards[k[0]].shape[2]) if side_by_side(k) else _lead_slot

    def begin(name, ks, zero):
        srcs = [_pack([conv_w]) + zero if k == conv_key else (shards[k[0]][k[1]] + zero).astype(BF16) for k in ks]
        lands = [lax.empty((s.shape[0], N_DEV * s.shape[1]) if side_by_side(k) else (N_DEV,) + s.shape, s.dtype)
                 for k, s in zip(ks, srcs)]
        peers = [ALL_PEERS if k == conv_key else NEAR_PEERS for k in ks]
        send, recv, bufs, tok = _copy_start(name, srcs + lands, _gather_plan(peers, [slot_of(k) for k in ks]),
                                            [len(p) + 1 for p in peers])
        for i, k in enumerate(ks):
            started[k] = (send[i], recv[i], bufs[i], bufs[len(ks) + i], peers[i])
        return tok

    token = begin("gather_start_first", order[:1], 0.0)
    token = begin("gather_start_rest", [conv_key] + order[1:], token[0, 0])

    def arrive(k, after):
        send, recv, src, land, peers = started[k]
        arrived[k] = _copy_wait(f"gather_{k[0]}_{k[1]}_arrived", [src, land], [send], [recv],
                                _gather_plan([peers], [slot_of(k)]), after)

    queue = list(order)

    def advance(after):
        if not queue:
            return 0.0
        k = queue.pop(0)
        arrive(k, after)
        forwarded[k] = _copy_start(f"gather_{k[0]}_{k[1]}_forward", [arrived[k][1]], _forward_plan(slot_of(k)),
                                   [len(OTHER_CHIPS)])
        return forwarded[k][3][0, 0]

    pass_on_behind = ("proj_in_0", "attn_0", "ffn_up_0", "ffn_down_0", "proj_in_1", "attn_1", "ffn_up_1")

    def used(point, result):
        return advance(result) if point in pass_on_behind else 0.0

    def weights(n, l, after):
        k = (n, l)
        if k not in gathered:
            if k not in forwarded:
                advance(after)
            send_b, recv_b, (land,), _ = forwarded[k]
            (gathered[k],) = _copy_wait(f"gather_{n}_{l}_done", [land], send_b, recv_b, _forward_plan(slot_of(k)),
                                        after)
        return gathered[k]

    pending = {}

    def hand_over(n, l, g):
        shard = shards[n].shape[1:]
        send, recv, bufs, tok = _copy_start(f"send_grad_{n}_{l}", [g, lax.empty((N_DEV,) + shard, g.dtype)],
                                            _scatter_plan(slot_of((n, l))), [len(ALL_PEERS) + 1])
        pending[(n, l)] = (send, recv, bufs)
        return tok[0, 0]

    def received(k, after):
        send, recv, bufs = pending[k]
        return _copy_wait(f"recv_grad_{k[0]}_{k[1]}", bufs, send, recv, _scatter_plan(slot_of(k)), after)[1]

    arrive(conv_key, token)
    cw_all = arrived[conv_key][1]
    nup = w_up.shape[2]
    cw_shards = cw_all.reshape(N_DEV, -1)[:, :DEPTH * 3 * nup].reshape(N_DEV, DEPTH, 3, nup)
    conv_w_full = cw_shards.transpose(1, 2, 0, 3).reshape(DEPTH, 3, N_DEV * nup)

    loss_local, dx, d_ln_final, sgrads = _local_step(
        x[0], loss_target[0], dict(small, ln_attn=ln_attn + token[0, 0]), weights, conv_w_full, hand_over, used)

    stacked = [jnp.stack([sgrads[l][n] for l in range(DEPTH)]) for n in SMALL_NAMES + ("conv_w",)] + [d_ln_final]
    shapes = [a.shape for a in stacked]
    mine = _pack(stacked)
    send_s, recv_s, bufs_s, _ = _copy_start("gather_small_grads_start", [mine, lax.empty((N_DEV,) + mine.shape, F32)],
                                            _gather_plan([ALL_PEERS], [_lead_slot]), [len(ALL_PEERS) + 1])

    big, after = {}, dx
    moments = dict(w_in=(m_w_in, v_w_in), w_out=(m_w_out, v_w_out), w_up=(m_w_up, v_w_up), w_down=(m_w_down, v_w_down))
    for n in reversed(names):
        parts = (received((n, 0), after), received((n, 1), after))
        big[n] = _adamw_sharded(f"adamw_{n}", shards[n], *moments[n], parts)
        after = big[n][1]

    _, everyone = _copy_wait("gather_small_grads_done", bufs_s, send_s, recv_s,
                             _gather_plan([ALL_PEERS], [_lead_slot]), after)
    g_small = _unpack(_sum_devices("sum_small_grads", everyone), shapes)
    g = dict(zip(SMALL_NAMES + ("conv_w", "ln_final"), g_small))
    g["conv_w"] = lax.dynamic_slice_in_dim(g["conv_w"], me * nup, nup, axis=2)

    snames = SMALL_NAMES + ("conv_w", "ln_final")
    sw = dict(small, conv_w=conv_w)
    sm = dict(ln_attn=m_ln_attn, sink_b=m_sink_b, rpb_c=m_rpb_c, mix_gain=m_mix_gain, ln_ffn=m_ln_ffn,
              conv_b=m_conv_b, conv_w=m_conv_w, ln_final=m_ln_final)
    sv = dict(ln_attn=v_ln_attn, sink_b=v_sink_b, rpb_c=v_rpb_c, mix_gain=v_mix_gain, ln_ffn=v_ln_ffn,
              conv_b=v_conv_b, conv_w=v_conv_w, ln_final=v_ln_final)
    s_delta, s_m, s_v = (dict(zip(snames, out)) for out in _adamw_small(
        "adamw_small", [sw[n] for n in snames], [g[n] for n in snames], [sm[n] for n in snames],
        [sv[n] for n in snames]))

    loss = lax.psum(loss_local, ("x", "y", "c"))
    outputs = ("ln_attn", "w_in", "sink_b", "rpb_c", "mix_gain", "w_out", "ln_ffn", "w_up", "conv_w", "conv_b",
               "w_down", "ln_final")
    grads = [big[n][0] if n in big else g[n] for n in outputs]
    deltas = [big[n][1] if n in big else s_delta[n] for n in outputs]
    new_m = [big[n][2] if n in big else s_m[n] for n in outputs]
    new_v = [big[n][3] if n in big else s_v[n] for n in outputs]
    return (loss, dx[None], *grads, *deltas, *new_m, *new_v)
```

```python
import functools

import jax
import jax.numpy as jnp
from jax import lax
from jax.experimental import pallas as pl
from jax.experimental.pallas import tpu as pltpu

F32 = jnp.float32
BF16 = jnp.bfloat16

N_DEV = 8
T = 2048
D = 2048
DEPTH = 2
HD = 64
HA, HB, HKV, HC = 12, 10, 2, 10
WA, WB, WKV, WC = HA * HD, HB * HD, HKV * HD, HC * HD
IN_COLS = 3 * WA + WB + 2 * WKV + 3 * WC
DFF = 5632
GRID_W = 64
ROWS = T // GRID_W
NA_ROWS, NA_COLS = 8, 16
WINDOW_B = 128
EPS = 1e-6
NEG = -1e30
ROPE_THETA = 10000.0
LANE = 128
VMEM_LIMIT = 56 * 1024 * 1024

ADAM_LR, ADAM_B1, ADAM_B2, ADAM_EPS, ADAM_WD, ADAM_STEP = 0.001, 0.9, 0.999, 1e-08, 0.01, 10

GROUPS = (("qa", WA, True, True), ("ka", WA, True, False), ("va", WA, False, False),
          ("qb", WB, True, True), ("kb", WKV, True, False), ("vb", WKV, False, False),
          ("qc", WC, False, True), ("kc", WC, False, False), ("vc", WC, False, False))


def _params(sem=None):
    return pltpu.CompilerParams(dimension_semantics=sem, vmem_limit_bytes=VMEM_LIMIT)


HBM_SPEC = pl.BlockSpec(memory_space=pltpu.HBM)
SEM_SPEC = pl.BlockSpec(memory_space=pltpu.SEMAPHORE)
DATAFLOW = pltpu.SideEffectType.DATAFLOW_SIDE_EFFECTING


ALL_PEERS = tuple((p >> 2 & 1, p >> 1 & 1, p & 1) for p in range(1, N_DEV))
OTHER_CHIPS = ((1, 0, 0), (0, 1, 0), (1, 1, 0))
NEAR_PEERS = ((0, 0, 1),) + OTHER_CHIPS


def _flip(x, y, c, f):
    return (1 - x if f[0] else x, 1 - y if f[1] else y, 1 - c if f[2] else c)


def _index(pos):
    return 4 * pos[0] + 2 * pos[1] + pos[2]


class _LocalCopy:
    def __init__(self, src, dst, sem):
        self.copy = pltpu.make_async_copy(src, dst, sem)

    def start(self):
        self.copy.start()

    def wait_send(self):
        self.copy.wait()

    def wait_recv(self):
        pass


def _descriptors(plan, bufs, send_sems, recv_sems):
    x, y, c = lax.axis_index("x"), lax.axis_index("y"), lax.axis_index("c")
    return [_LocalCopy(src, dst, send_sems[g].at[i]) if partner is None else
            pltpu.make_async_remote_copy(src_ref=src, dst_ref=dst, send_sem=send_sems[g].at[i],
                                         recv_sem=recv_sems[g].at[i], device_id=partner,
                                         device_id_type=pl.DeviceIdType.MESH)
            for g, copies in enumerate(plan(bufs, x, y, c)) for i, (src, dst, partner) in enumerate(copies)]


def _copy_start(name, bufs, plan, sizes):
    nb, ng = len(bufs), len(sizes)

    def body(*refs):
        for d in _descriptors(plan, refs[:nb], refs[nb:nb + ng], refs[nb + ng:nb + 2 * ng]):
            d.start()
        refs[2 * nb + 2 * ng][...] = jnp.zeros((8, LANE), F32)

    outs = pl.pallas_call(
        body, name=name,
        out_shape=[pltpu.SemaphoreType.DMA((s,)) for s in sizes] * 2 + [pltpu.HBM(b.shape, b.dtype) for b in bufs]
        + [jax.ShapeDtypeStruct((8, LANE), F32)],
        in_specs=[HBM_SPEC] * nb,
        out_specs=[SEM_SPEC] * (2 * ng) + [HBM_SPEC] * nb + [pl.BlockSpec(memory_space=pltpu.VMEM)],
        input_output_aliases={i: 2 * ng + i for i in range(nb)},
        compiler_params=pltpu.CompilerParams(has_side_effects=DATAFLOW),
    )(*[pltpu.with_memory_space_constraint(b, pltpu.HBM) for b in bufs])
    return outs[:ng], outs[ng:2 * ng], outs[2 * ng:2 * ng + nb], outs[2 * ng + nb]


def _copy_wait(name, bufs, send_sems, recv_sems, plan, after):
    nb, ng = len(bufs), len(send_sems)
    after = list(after) if isinstance(after, (list, tuple)) else [after]

    def body(*refs):
        for d in _descriptors(plan, refs[:nb], refs[nb:nb + ng], refs[nb + ng:nb + 2 * ng]):
            d.wait_send()
            d.wait_recv()

    return pl.pallas_call(
        body, name=name, out_shape=[pltpu.HBM(b.shape, b.dtype) for b in bufs],
        in_specs=[HBM_SPEC] * nb + [SEM_SPEC] * (2 * ng) + [pl.BlockSpec(memory_space=pl.ANY)] * len(after),
        out_specs=[HBM_SPEC] * nb, input_output_aliases={i: i for i in range(nb)},
        compiler_params=pltpu.CompilerParams(has_side_effects=DATAFLOW),
    )(*bufs, *send_sems, *recv_sems, *after)


def _lead_slot(ref, k):
    return ref.at[k]


def _col_slot(width):
    return lambda ref, k: ref.at[:, pl.ds(pl.multiple_of(k * width, LANE), width)]


def _gather_plan(peer_sets, slots):
    def plan(bufs, x, y, c):
        n = len(peer_sets)
        return [[(bufs[i], slots[i](bufs[n + i], _index((x, y, c))), _flip(x, y, c, f)) for f in peers]
                + [(bufs[i], slots[i](bufs[n + i], _index((x, y, c))), None)] for i, peers in enumerate(peer_sets)]
    return plan


def _forward_plan(slot):
    def plan(bufs, x, y, c):
        pieces = [slot(bufs[0], _index(_flip(x, y, c, f))) for f in OTHER_CHIPS]
        return [[(p, p, _flip(x, y, c, (0, 0, 1))) for p in pieces]]
    return plan


def _scatter_plan(slot):
    def plan(bufs, x, y, c):
        me = _index((x, y, c))
        peers = [_flip(x, y, c, f) for f in ALL_PEERS]
        return [[(slot(bufs[0], _index(p)), bufs[1].at[me], p) for p in peers]
                + [(slot(bufs[0], me), bufs[1].at[me], None)]]
    return plan


def _flat2(v):
    return v.reshape(-1, v.shape[-1])


def _matmul(name, kind, a, a_spec, b, b_spec, out_shape, out_spec, grid, res=None, res_spec=None, acc_shape=None):
    dims = {"nn": (((1,), (0,)), ((), ())), "nt": NT_DIMS, "tn": TN_DIMS}[kind]
    nred = grid[-1]

    def body(*refs):
        if res is None:
            a_ref, b_ref, o_ref = refs[:3]
            r_ref = None
        else:
            a_ref, b_ref, r_ref, o_ref = refs[:4]
        part = lax.dot_general(_flat2(a_ref[...]), _flat2(b_ref[...]), dims, preferred_element_type=F32)

        def finish(total):
            if r_ref is not None:
                total = total + r_ref[...]
            o_ref[...] = total.reshape(o_ref.shape).astype(o_ref.dtype)

        if nred == 1:
            finish(part)
        else:
            acc_ref = refs[-1]
            k = pl.program_id(len(grid) - 1)

            @pl.when(k == 0)
            def _():
                acc_ref[...] = part

            @pl.when(jnp.logical_and(k > 0, k < nred - 1))
            def _():
                acc_ref[...] += part

            @pl.when(k == nred - 1)
            def _():
                finish(acc_ref[...] + part)

    ins, specs = [a, b], [a_spec, b_spec]
    if res is not None:
        ins.append(res)
        specs.append(res_spec)
    scratch = [] if nred == 1 else [pltpu.VMEM(acc_shape, F32)]
    return pl.pallas_call(
        body, name=name, grid=grid, in_specs=specs, out_specs=out_spec, out_shape=out_shape, scratch_shapes=scratch,
        compiler_params=_params(("parallel",) * (len(grid) - 1) + ("arbitrary",)),
    )(*ins)


def _nn_rows(name, a, wg, res, s, tn, tm):
    _, kj, n = wg.shape
    return _matmul(
        name, "nn", a, pl.BlockSpec((tm, s * kj), lambda j, i, r: (i, r)),
        wg, pl.BlockSpec((s, kj, tn), lambda j, i, r: (r, 0, j)),
        jax.ShapeDtypeStruct((T, n), F32), pl.BlockSpec((tm, tn), lambda j, i, r: (i, j)),
        (n // tn, T // tm, N_DEV // s), res=res, res_spec=pl.BlockSpec((tm, tn), lambda j, i, r: (i, j)),
        acc_shape=(tm, tn))


def _nt_cols(name, dc, dc_spec_of, w, nc):
    k, n = w.shape
    tm = tk = 1024
    return _matmul(
        name, "nt", dc, dc_spec_of(tm, nc),
        w, pl.BlockSpec((tk, nc), lambda kt, i, j: (kt, j)),
        jax.ShapeDtypeStruct((T, k), F32), pl.BlockSpec((tm, tk), lambda kt, i, j: (i, kt)),
        (k // tk, T // tm, n // nc), acc_shape=(tm, tk))


def _nt_rows(name, dc, wg, s, tm):
    _, kj, n = wg.shape
    return _matmul(
        name, "nt", dc, pl.BlockSpec((tm, n), lambda kt, i, r: (i, 0)),
        wg, pl.BlockSpec((s, kj, n), lambda kt, i, r: (kt, 0, 0)),
        jax.ShapeDtypeStruct((T, N_DEV * kj), F32), pl.BlockSpec((tm, s * kj), lambda kt, i, r: (i, kt)),
        (N_DEV // s, T // tm, 1))


def _tn_cols(name, a, dc, dc_spec_of, n, tn):
    k = a.shape[1]
    tk = 512
    return _matmul(
        name, "tn", a, pl.BlockSpec((T, tk), lambda j, kt, r: (0, kt)),
        dc, dc_spec_of(T, tn),
        jax.ShapeDtypeStruct((k, n), BF16), pl.BlockSpec((tk, tn), lambda j, kt, r: (kt, j)),
        (n // tn, k // tk, 1))


def _tn_rows(name, a, dc, kj, s, tn):
    n = dc.shape[1]
    return _matmul(
        name, "tn", a, pl.BlockSpec((T, s * kj), lambda kt, j, r: (0, kt)),
        dc, pl.BlockSpec((T, tn), lambda kt, j, r: (0, j)),
        jax.ShapeDtypeStruct((N_DEV, kj, n), BF16), pl.BlockSpec((s, kj, tn), lambda kt, j, r: (kt, 0, j)),
        (N_DEV // s, n // tn, 1))


TR = 512


def _rows(width):
    return pl.BlockSpec((TR, width), lambda i: (i, 0))


def _whole(shape):
    return pl.BlockSpec(shape, lambda i: (0,) * len(shape))


def _rmsnorm_rows(x_ref, g_ref):
    xv = x_ref[...]
    r = lax.rsqrt(jnp.mean(xv * xv, axis=-1, keepdims=True) + EPS)
    return ((xv * r) * g_ref[...]).astype(BF16)


SUB = 256


def _prologue_matmul(name, prologue, ins, widths, w, w_block, w_index, tn, res=None, epilogue=None, extras=(),
                     out_dtype=F32, blocked_out=False):
    tm = 1024
    n = w.shape[-1]
    ni = len(ins)

    def body(*refs):
        w_ref = refs[ni]
        r_ref = refs[ni + 1] if res is not None else None
        x_refs = refs[ni + 1 + (res is not None):len(refs) - 3]
        h_ref, o_ref, h_scr = refs[-3:]

        @pl.when(pl.program_id(1) == 0)
        def _():
            h = prologue(*refs[:ni])
            h_scr[...] = h
            h_ref[...] = h

        for sub in range(tn // SUB):
            cols = slice(sub * SUB, (sub + 1) * SUB)
            w_cols = w_ref[(slice(None),) * (len(w_ref.shape) - 1) + (cols,)]
            part = jnp.dot(h_scr[...], _flat2(w_cols), preferred_element_type=F32)
            if r_ref is not None:
                part = part + r_ref[:, cols]
            if epilogue is not None:
                part = epilogue(pl.program_id(1) * (tn // SUB) + sub, part, *x_refs)
            if blocked_out:
                for b in range(SUB // LANE):
                    o_ref[sub * (SUB // LANE) + b] = part[:, b * LANE:(b + 1) * LANE].astype(out_dtype)
            else:
                o_ref[:, cols] = part.astype(out_dtype)

    tile = pl.BlockSpec((tm, tn), lambda i, j: (i, j))
    out_tile = pl.BlockSpec((tn // LANE, tm, LANE), lambda i, j: (j, i, 0)) if blocked_out else tile
    out_full = (n // LANE, T, LANE) if blocked_out else (T, n)
    specs = [pl.BlockSpec((1, D), lambda i, j: (0, 0)) if wd is None else pl.BlockSpec((tm, wd), lambda i, j: (i, 0))
             for wd in widths]
    specs.append(pl.BlockSpec(w_block, lambda i, j: w_index(j)))
    operands = list(ins) + [w]
    if res is not None:
        specs.append(tile)
        operands.append(res)
    specs += [pl.BlockSpec((tm, LANE), lambda i, j: (i, 0))] * len(extras)
    operands += list(extras)
    return pl.pallas_call(
        body, name=name, grid=(T // tm, n // tn), in_specs=specs,
        out_specs=[pl.BlockSpec((tm, D), lambda i, j: (i, 0)), out_tile],
        out_shape=[jax.ShapeDtypeStruct((T, D), BF16), jax.ShapeDtypeStruct(out_full, out_dtype)],
        scratch_shapes=[pltpu.VMEM((tm, D), BF16)], compiler_params=_params(("parallel", "arbitrary")),
    )(*operands)


def _rms_bwd_math(dy, xv, g):
    r = lax.rsqrt(jnp.mean(xv * xv, axis=-1, keepdims=True) + EPS)
    xhat = xv * r
    dxhat = dy * g
    dx = r * (dxhat - xhat * jnp.mean(dxhat * xhat, axis=-1, keepdims=True))
    return dx, dy * xhat


def _accumulate(ref, val):
    @pl.when(pl.program_id(0) == 0)
    def _():
        ref[...] = val

    @pl.when(pl.program_id(0) > 0)
    def _():
        ref[...] += val


def _rmsnorm_bwd(name, dy, x, g, res):
    def body(dy_ref, x_ref, g_ref, res_ref, dx_ref, dxb_ref, dg_ref):
        dx, dgr = _rms_bwd_math(dy_ref[...], x_ref[...], g_ref[...])
        tot = res_ref[...] + dx
        dx_ref[...] = tot
        dxb_ref[...] = tot.astype(BF16)
        _accumulate(dg_ref, jnp.sum(dgr, axis=0, keepdims=True))

    return pl.pallas_call(
        body, name=name, grid=(T // TR,), in_specs=[_rows(D), _rows(D), _whole((1, D)), _rows(D)],
        out_specs=[_rows(D), _rows(D), _whole((1, D))],
        out_shape=[jax.ShapeDtypeStruct((T, D), F32), jax.ShapeDtypeStruct((T, D), BF16),
                   jax.ShapeDtypeStruct((1, D), F32)],
        compiler_params=_params(("arbitrary",)),
    )(dy, x, g, res)


def _loss_head(x, g, target):
    def body(x_ref, g_ref, t_ref, loss_ref, dx_ref, dxb_ref, dg_ref):
        xv, gv = x_ref[...], g_ref[...]
        r = lax.rsqrt(jnp.mean(xv * xv, axis=-1, keepdims=True) + EPS)
        err = (xv * r) * gv - t_ref[...]
        part = 0.5 * jnp.sum(jnp.mean(err * err, axis=-1, keepdims=True))
        dx, dgr = _rms_bwd_math(err * (1.0 / D), xv, gv)
        dx_ref[...] = dx
        dxb_ref[...] = dx.astype(BF16)
        _accumulate(dg_ref, jnp.sum(dgr, axis=0, keepdims=True))
        _accumulate(loss_ref, jnp.full((8, LANE), part, F32))

    return pl.pallas_call(
        body, name="loss_head", grid=(T // TR,), in_specs=[_rows(D), _whole((1, D)), _rows(D)],
        out_specs=[_whole((8, LANE)), _rows(D), _rows(D), _whole((1, D))],
        out_shape=[jax.ShapeDtypeStruct((8, LANE), F32), jax.ShapeDtypeStruct((T, D), F32),
                   jax.ShapeDtypeStruct((T, D), BF16), jax.ShapeDtypeStruct((1, D), F32)],
        compiler_params=_params(("arbitrary",)),
    )(x, g, target)


MIX_OFFS = ((0, WA), (WA, WB), (WA + WB, WC))


def _mix_rows(oa_ref, ob_ref, oc_ref, g_ref):
    parts = []
    for ref, (off, w) in zip((oa_ref, ob_ref, oc_ref), MIX_OFFS):
        o = ref[...]
        r = lax.rsqrt(jnp.mean(o * o, axis=-1, keepdims=True) + EPS)
        parts.append(((o * r) * g_ref[:, off:off + w]).astype(BF16))
    return jnp.concatenate(parts, axis=1)


def _mix_bwd(name, dmixed, oa, ob, oc, gain):
    def body(dm_ref, oa_ref, ob_ref, oc_ref, g_ref, doa_ref, dob_ref, doc_ref, dg_ref):
        dgs = []
        for ref, dref, (off, w) in zip((oa_ref, ob_ref, oc_ref), (doa_ref, dob_ref, doc_ref), MIX_OFFS):
            dx, dgr = _rms_bwd_math(dm_ref[:, off:off + w], ref[...], g_ref[:, off:off + w])
            dref[...] = dx
            dgs.append(jnp.sum(dgr, axis=0, keepdims=True))
        _accumulate(dg_ref, jnp.concatenate(dgs, axis=1))

    return pl.pallas_call(
        body, name=name, grid=(T // TR,),
        in_specs=[_rows(D), _rows(WA), _rows(WB), _rows(WC), _whole((1, D))],
        out_specs=[_rows(WA), _rows(WB), _rows(WC), _whole((1, D))],
        out_shape=[jax.ShapeDtypeStruct((T, WA), F32), jax.ShapeDtypeStruct((T, WB), F32),
                   jax.ShapeDtypeStruct((T, WC), F32), jax.ShapeDtypeStruct((1, D), F32)],
        compiler_params=_params(("arbitrary",)),
    )(dmixed, oa, ob, oc, gain)


def _rope_tables():
    inv_freq = ROPE_THETA ** (-jnp.arange(0, HD, 2, dtype=F32) / HD)
    ang = jnp.arange(T, dtype=F32)[:, None] * inv_freq[None, :]
    cos, sin = jnp.cos(ang), jnp.sin(ang)
    cos2 = jnp.tile(jnp.concatenate([cos, cos], axis=1), (1, LANE // HD))
    sin2 = jnp.tile(jnp.concatenate([-sin, sin], axis=1), (1, LANE // HD))
    return cos2, sin2


def _rot_half(v):
    lane = lax.broadcasted_iota(jnp.int32, v.shape, 1)
    return jnp.where(lane % HD < HD // 2, pltpu.roll(v, LANE - HD // 2, 1), pltpu.roll(v, HD // 2, 1))


BLOCK_KINDS = tuple((rot, is_q) for _, w, rot, is_q in GROUPS for _ in range(w // LANE))
BLOCK_OF = {name: sum(w for _, w, _, _ in GROUPS[:g]) // LANE for g, (name, _, _, _) in enumerate(GROUPS)}


def _any_tile(j, tiles):
    return functools.reduce(jnp.logical_or, [j == t for t in tiles]) if tiles else False


def _rope_epilogue(j, tile, c_ref, s_ref):
    cv, sv = c_ref[...], s_ref[...]
    per, n_tiles = tile.shape[1] // LANE, IN_COLS // tile.shape[1]
    out = []
    for b in range(per):
        v = tile[:, b * LANE:(b + 1) * LANE]
        rot = _any_tile(j, [t for t in range(n_tiles) if BLOCK_KINDS[t * per + b][0]])
        is_q = _any_tile(j, [t for t in range(n_tiles) if BLOCK_KINDS[t * per + b][1]])
        if rot is not False:
            v = jnp.where(rot, v * cv + _rot_half(v) * sv, v)
        if is_q is not False:
            v = v * jnp.where(is_q, HD ** -0.5, 1.0)
        out.append(v)
    return jnp.concatenate(out, axis=1)


def _rope_bwd(name, grads, cos2, sin2):
    def body(*refs):
        ins, (c_ref, s_ref, o_ref) = refs[:9], refs[9:]
        cv, sv = c_ref[...], s_ref[...]
        off = 0
        for d_ref, (_, w, rot, is_q) in zip(ins, GROUPS):
            for b in range(w // LANE):
                v = d_ref[:, b * LANE:(b + 1) * LANE]
                if is_q:
                    v = v * (HD ** -0.5)
                if rot:
                    v = v * cv + _rot_half(v * sv)
                o_ref[:, off + b * LANE:off + (b + 1) * LANE] = v.astype(BF16)
            off += w

    return pl.pallas_call(
        body, name=name, grid=(T // TR,), in_specs=[_rows(w) for _, w, _, _ in GROUPS] + [_rows(LANE), _rows(LANE)],
        out_specs=_rows(IN_COLS), out_shape=jax.ShapeDtypeStruct((T, IN_COLS), BF16),
        compiler_params=_params(("parallel",)),
    )(*grads, cos2, sin2)


NT_DIMS = (((1,), (1,)), ((), ()))
TN_DIMS = (((0,), (0,)), ((), ()))


def _scores(q, k, bias, valid):
    s = lax.dot_general(q, k, NT_DIMS, preferred_element_type=F32)
    if bias is not None:
        s = s + bias
    if valid is not None:
        s = jnp.where(valid, s, NEG)
    return s


def _heads_fwd(heads):
    scores = [_scores(h["q"], h["k"], h.get("bias"), h.get("valid")) for h in heads]
    soft = []
    for s, h in zip(scores, heads):
        m = jnp.max(s, axis=1, keepdims=True)
        e = jnp.exp(s - m)
        l = jnp.sum(e, axis=1, keepdims=True)
        if h.get("sink") is not None:
            l = l + jnp.exp(h["sink"] - m)
        soft.append((e.astype(BF16), l, m + jnp.log(l)))
    return [(jnp.dot(e, h["v"], preferred_element_type=F32) / l, lse) for (e, l, lse), h in zip(soft, heads)]


def _heads_bwd(heads):
    dobs = [h["do"].astype(BF16) for h in heads]
    scores = [_scores(h["q"], h["k"], h.get("bias"), h.get("valid")) for h in heads]
    dps = [lax.dot_general(dob, h["v"], NT_DIMS, preferred_element_type=F32) for dob, h in zip(dobs, heads)]
    mid = []
    for s, dp, h in zip(scores, dps, heads):
        p = jnp.exp(s - h["lse"])
        delta = jnp.sum(h["do"] * h["o"], axis=1, keepdims=True)
        ds = p * (dp - delta)
        dsink = None if h.get("sink") is None else -jnp.exp(h["sink"] - h["lse"]) * delta
        mid.append((p.astype(BF16), ds, dsink))
    out = []
    for (pb, ds, dsink), dob, h in zip(mid, dobs, heads):
        dsb = ds.astype(BF16)
        out.append((jnp.dot(dsb, h["k"], preferred_element_type=F32),
                    lax.dot_general(dsb, h["q"], TN_DIMS, preferred_element_type=F32),
                    lax.dot_general(pb, dob, TN_DIMS, preferred_element_type=F32), ds, dsink))
    return out


def _per_head(cols):
    return jnp.concatenate([jnp.broadcast_to(c, (c.shape[0], HD)) for c in cols], axis=1)


DILATIONS = ((128, 1), (512, 4), (2048, 16))


BQ_A = 256
REACH_A = max(window // 2 for window, _ in DILATIONS)


def _first_key(i):
    return jnp.maximum(i * BQ_A - REACH_A, 0)


def _key_window_groups():
    groups = {}
    for i in range(T // BQ_A):
        width = min(T, (i + 1) * BQ_A + REACH_A) - max(i * BQ_A - REACH_A, 0)
        groups.setdefault(width, []).append(i)
    return groups


def _per_window(i, fn):
    for width, tiles in _key_window_groups().items():
        hit = functools.reduce(jnp.logical_or, [i == t for t in tiles])
        pl.when(hit)(functools.partial(fn, pl.multiple_of(_first_key(i), BQ_A), width))


def _dilation_bias():
    def body(o_ref):
        i = pl.program_id(0)
        t = i * BQ_A + lax.broadcasted_iota(jnp.int32, (BQ_A, T), 0)
        ad = jnp.abs(t - (_first_key(i) + lax.broadcasted_iota(jnp.int32, (BQ_A, T), 1)))
        count = jnp.zeros((BQ_A, T), jnp.int32)
        for window, r in DILATIONS:
            count += jnp.where(((ad & (r - 1)) == 0) & (ad <= window // 2), 1, 0)
        logs = jnp.where(count == 2, jnp.log(2.0), jnp.where(count == 3, jnp.log(3.0), 0.0)).astype(F32)
        o_ref[...] = jnp.where(count == 0, NEG, logs)

    return pl.pallas_call(
        body, name="dilation_bias", grid=(T // BQ_A,), out_specs=pl.BlockSpec((BQ_A, T), lambda i: (i, 0)),
        out_shape=jax.ShapeDtypeStruct((T, T), F32), compiler_params=_params(("parallel",)),
    )()


def _qkv_rows(rows, group):
    return pl.BlockSpec((rows, LANE), lambda p, i: (i, BLOCK_OF[group] + p))


def _qkv_all(group):
    return pl.BlockSpec((T, LANE), lambda p, i: (0, BLOCK_OF[group] + p))


def _attn_a_fwd(name, qkv, bias):
    def body(q_ref, k_ref, v_ref, b_ref, o_ref, lse_ref):
        def tile(first, width):
            b = b_ref[:, :width]
            outs = _heads_fwd([dict(q=q_ref[:, h * HD:(h + 1) * HD], k=k_ref[pl.ds(first, width), h * HD:(h + 1) * HD],
                                    v=v_ref[pl.ds(first, width), h * HD:(h + 1) * HD], bias=b) for h in range(2)])
            o_ref[...] = jnp.concatenate([o for o, _ in outs], axis=1)
            lse_ref[...] = _per_head([lse for _, lse in outs])

        _per_window(pl.program_id(1), tile)

    qs = pl.BlockSpec((BQ_A, LANE), lambda p, i: (i, p))
    ks = pl.BlockSpec((T, LANE), lambda p, i: (0, p))
    return pl.pallas_call(
        body, name=name, grid=(HA // 2, T // BQ_A),
        in_specs=[_qkv_rows(BQ_A, "qa"), _qkv_all("ka"), _qkv_all("va"), pl.BlockSpec((BQ_A, T), lambda p, i: (i, 0))],
        out_specs=[qs, qs],
        out_shape=[jax.ShapeDtypeStruct((T, WA), F32)] * 2, compiler_params=_params(("parallel", "parallel")),
    )(qkv, qkv, qkv, bias)


def _attn_a_bwd(name, qkv, oa, lse, doa, bias):
    def body(q_ref, k_ref, v_ref, o_ref, lse_ref, do_ref, b_ref, dq_ref, dk_ref, dv_ref):
        @pl.when(pl.program_id(1) == 0)
        def _():
            dk_ref[...] = jnp.zeros_like(dk_ref)
            dv_ref[...] = jnp.zeros_like(dv_ref)

        def tile(first, width):
            b = b_ref[:, :width]
            keys = pl.ds(first, width)
            sls = [slice(h * HD, (h + 1) * HD) for h in range(2)]
            res = _heads_bwd([dict(q=q_ref[:, sl], k=k_ref[keys, sl], v=v_ref[keys, sl], o=o_ref[:, sl],
                                   do=do_ref[:, sl], lse=lse_ref[:, sl.start:sl.start + 1], bias=b) for sl in sls])
            dq_ref[...] = jnp.concatenate([r[0] for r in res], axis=1)
            dk_ref[keys, :] += jnp.concatenate([r[1] for r in res], axis=1)
            dv_ref[keys, :] += jnp.concatenate([r[2] for r in res], axis=1)

        _per_window(pl.program_id(1), tile)

    qs = pl.BlockSpec((BQ_A, LANE), lambda p, i: (i, p))
    ks = pl.BlockSpec((T, LANE), lambda p, i: (0, p))
    return pl.pallas_call(
        body, name=name, grid=(HA // 2, T // BQ_A),
        in_specs=[_qkv_rows(BQ_A, "qa"), _qkv_all("ka"), _qkv_all("va"), qs, qs, qs,
                  pl.BlockSpec((BQ_A, T), lambda p, i: (i, 0))], out_specs=[qs, ks, ks],
        out_shape=[jax.ShapeDtypeStruct((T, WA), F32)] * 3, compiler_params=_params(("parallel", "arbitrary")),
    )(qkv, qkv, qkv, oa, lse, doa, bias)


BQ_B = 128
SPAN_B = BQ_B + 2 * WINDOW_B


def _window_b(i):
    start = pl.multiple_of(jnp.clip(i * BQ_B - WINDOW_B, 0, T - SPAN_B), BQ_B)
    qpos = i * BQ_B + lax.broadcasted_iota(jnp.int32, (BQ_B, SPAN_B), 0)
    kpos = start + lax.broadcasted_iota(jnp.int32, (BQ_B, SPAN_B), 1)
    return start, jnp.abs(qpos - kpos) <= WINDOW_B


GROUP_B = HB // HKV


def _stack_group(ref, g):
    return jnp.concatenate([ref[:, h * HD:(h + 1) * HD] for h in range(g * GROUP_B, (g + 1) * GROUP_B)], axis=0)


def _sink_column(sink_ref, g):
    return jnp.concatenate([jnp.full((BQ_B, 1), sink_ref[h], F32) for h in range(g * GROUP_B, (g + 1) * GROUP_B)],
                           axis=0)


def _unstack(stacked):
    return [s[j * BQ_B:(j + 1) * BQ_B] for s in stacked for j in range(GROUP_B)]


def _attn_b_fwd(name, qb, kb, vb, sink):
    def body(sink_ref, q_ref, k_ref, v_ref, o_ref, lse_ref):
        start, valid = _window_b(pl.program_id(0))
        valid = jnp.concatenate([valid] * GROUP_B, axis=0)
        kw, vw = k_ref[pl.ds(start, SPAN_B), :], v_ref[pl.ds(start, SPAN_B), :]
        outs = _heads_fwd([dict(q=_stack_group(q_ref, g), k=kw[:, g * HD:(g + 1) * HD], v=vw[:, g * HD:(g + 1) * HD],
                                valid=valid, sink=_sink_column(sink_ref, g)) for g in range(HKV)])
        o_ref[...] = jnp.concatenate(_unstack([o for o, _ in outs]), axis=1)
        lse_ref[...] = _per_head(_unstack([lse for _, lse in outs]))

    qs = pl.BlockSpec((BQ_B, WB), lambda i: (i, 0))
    return pl.pallas_call(
        body, name=name, grid=(T // BQ_B,),
        in_specs=[pl.BlockSpec(memory_space=pltpu.SMEM), qs, _whole((T, WKV)), _whole((T, WKV))],
        out_specs=[qs, qs],
        out_shape=[jax.ShapeDtypeStruct((T, WB), F32)] * 2, compiler_params=_params(("parallel",)),
    )(sink, qb, kb, vb)


def _attn_b_bwd(name, qb, kb, vb, ob, lse, dob, sink):
    def body(sink_ref, q_ref, k_ref, v_ref, o_ref, lse_ref, do_ref, dq_ref, dk_ref, dv_ref, dsink_ref):
        i = pl.program_id(0)
        start, valid = _window_b(i)
        valid = jnp.concatenate([valid] * GROUP_B, axis=0)
        kw, vw = k_ref[pl.ds(start, SPAN_B), :], v_ref[pl.ds(start, SPAN_B), :]
        res = _heads_bwd([dict(q=_stack_group(q_ref, g), k=kw[:, g * HD:(g + 1) * HD], v=vw[:, g * HD:(g + 1) * HD],
                               o=_stack_group(o_ref, g), do=_stack_group(do_ref, g),
                               lse=jnp.concatenate([lse_ref[:, h * HD:h * HD + 1]
                                                    for h in range(g * GROUP_B, (g + 1) * GROUP_B)], axis=0),
                               valid=valid, sink=_sink_column(sink_ref, g)) for g in range(HKV)])
        dks, dvs = [r[1] for r in res], [r[2] for r in res]
        lane = lax.broadcasted_iota(jnp.int32, (1, LANE), 1)
        dsink = jnp.zeros((1, LANE), F32)
        for h, rows in enumerate(_unstack([r[4] for r in res])):
            dsink += jnp.where(lane == h, jnp.sum(rows), 0.0)
        dq_ref[...] = jnp.concatenate(_unstack([r[0] for r in res]), axis=1)

        @pl.when(i == 0)
        def _():
            dk_ref[...] = jnp.zeros_like(dk_ref)
            dv_ref[...] = jnp.zeros_like(dv_ref)
            dsink_ref[...] = jnp.zeros_like(dsink_ref)

        dk_ref[pl.ds(start, SPAN_B), :] += jnp.concatenate(dks, axis=1)
        dv_ref[pl.ds(start, SPAN_B), :] += jnp.concatenate(dvs, axis=1)
        dsink_ref[...] += dsink

    qs = pl.BlockSpec((BQ_B, WB), lambda i: (i, 0))
    return pl.pallas_call(
        body, name=name, grid=(T // BQ_B,),
        in_specs=[pl.BlockSpec(memory_space=pltpu.SMEM), qs, _whole((T, WKV)), _whole((T, WKV)), qs, qs, qs],
        out_specs=[qs, _whole((T, WKV)), _whole((T, WKV)), _whole((1, LANE))],
        out_shape=[jax.ShapeDtypeStruct((T, WB), F32), jax.ShapeDtypeStruct((T, WKV), F32),
                   jax.ShapeDtypeStruct((T, WKV), F32), jax.ShapeDtypeStruct((1, LANE), F32)],
        compiler_params=_params(("arbitrary",)),
    )(sink, qb, kb, vb, ob, lse, dob)


SPAN_C = NA_ROWS * GRID_W


def _row_start(r):
    return jnp.clip(r - NA_ROWS // 2, 0, ROWS - NA_ROWS)


def _off_index(r):
    return _row_start(r) - r + (NA_ROWS - 1)


N_TAB = 16
RPS_FWD, RPS_BWD = 4, 8


def _rpb_tables(name, rpb):
    circ = jnp.concatenate([rpb[..., NA_COLS - 1:], jnp.zeros(rpb.shape[:2] + (LANE - (2 * NA_COLS - 1),), F32),
                            rpb[..., :NA_COLS - 1]], axis=-1)
    circ = jnp.pad(circ, ((0, 0), (0, N_TAB + 1 - circ.shape[1]), (0, 0)))

    def body(w_ref, o_ref):
        c = lax.broadcasted_iota(jnp.int32, (GRID_W, LANE), 0)
        lane = lax.broadcasted_iota(jnp.int32, (GRID_W, LANE), 1)
        cs = jnp.clip(c - NA_COLS // 2, 0, GRID_W - NA_COLS)
        valid = (lane % GRID_W >= cs) & (lane % GRID_W < cs + NA_COLS)
        toep = [pltpu.roll(jnp.broadcast_to(w_ref[a:a + 1, :], (GRID_W, LANE)), 0, 1, stride=1, stride_axis=0)
                for a in range(N_TAB + 1)]
        for a in range(N_TAB):
            pair = jnp.where(lane < GRID_W, toep[a], pltpu.roll(toep[a + 1], GRID_W, 1))
            o_ref[a] = jnp.where(valid, pair, NEG)

    return pl.pallas_call(
        body, name=name, grid=(HC,),
        in_specs=[pl.BlockSpec((None, N_TAB + 1, LANE), lambda h: (h, 0, 0))],
        out_specs=pl.BlockSpec((None, N_TAB, GRID_W, LANE), lambda h: (h, 0, 0, 0)),
        out_shape=jax.ShapeDtypeStruct((HC, N_TAB, GRID_W, LANE), F32), compiler_params=_params(("parallel",)),
    )(circ)


def _bias_c(t_ref, h, d):
    return jnp.concatenate([t_ref[h, d + k] for k in range(0, NA_ROWS, 2)], axis=1)


def _attn_c_fwd(name, qkv, tables):
    RPS = RPS_FWD

    def body(q_ref, k_ref, v_ref, t_ref, o_ref, lse_ref):
        heads = []
        for rr in range(RPS):
            r = pl.program_id(1) * RPS + rr
            rows = slice(rr * GRID_W, (rr + 1) * GRID_W)
            start = pl.multiple_of(_row_start(r) * GRID_W, GRID_W)
            kw, vw = k_ref[pl.ds(start, SPAN_C), :], v_ref[pl.ds(start, SPAN_C), :]
            heads += [dict(q=q_ref[rows, h * HD:(h + 1) * HD], k=kw[:, h * HD:(h + 1) * HD], v=vw[:, h * HD:(h + 1) * HD],
                           bias=_bias_c(t_ref, h, _off_index(r))) for h in range(2)]
        outs = _heads_fwd(heads)
        for rr in range(RPS):
            rows = slice(rr * GRID_W, (rr + 1) * GRID_W)
            o_ref[rows, :] = jnp.concatenate([o for o, _ in outs[2 * rr:2 * rr + 2]], axis=1)
            lse_ref[rows, :] = _per_head([lse for _, lse in outs[2 * rr:2 * rr + 2]])

    qs = pl.BlockSpec((RPS * GRID_W, LANE), lambda p, r: (r, p))
    ks = pl.BlockSpec((T, LANE), lambda p, r: (0, p))
    ts = pl.BlockSpec((2, N_TAB, GRID_W, LANE), lambda p, r: (p, 0, 0, 0))
    return pl.pallas_call(
        body, name=name, grid=(HC // 2, ROWS // RPS),
        in_specs=[_qkv_rows(RPS * GRID_W, "qc"), _qkv_all("kc"), _qkv_all("vc"), ts], out_specs=[qs, qs],
        out_shape=[jax.ShapeDtypeStruct((T, WC), F32)] * 2, compiler_params=_params(("parallel", "parallel")),
    )(qkv, qkv, qkv, tables)


def _attn_c_bwd(name, qkv, oc, lse, doc, tables):
    RPS = RPS_BWD

    def body(q_ref, k_ref, v_ref, o_ref, lse_ref, do_ref, t_ref, dq_ref, dk_ref, dv_ref, dt_ref):
        @pl.when(pl.program_id(1) == 0)
        def _():
            dk_ref[...] = jnp.zeros_like(dk_ref)
            dv_ref[...] = jnp.zeros_like(dv_ref)
            dt_ref[...] = jnp.zeros_like(dt_ref)

        heads, where = [], []
        for rr in range(RPS):
            r = pl.program_id(1) * RPS + rr
            rows = slice(rr * GRID_W, (rr + 1) * GRID_W)
            d = _off_index(r)
            start = pl.multiple_of(_row_start(r) * GRID_W, GRID_W)
            kw, vw = k_ref[pl.ds(start, SPAN_C), :], v_ref[pl.ds(start, SPAN_C), :]
            where.append((rows, d, start))
            for h in range(2):
                sl = slice(h * HD, (h + 1) * HD)
                heads.append(dict(q=q_ref[rows, sl], k=kw[:, sl], v=vw[:, sl], o=o_ref[rows, sl], do=do_ref[rows, sl],
                                  lse=lse_ref[rows, h * HD:h * HD + 1], bias=_bias_c(t_ref, h, d)))
        res = _heads_bwd(heads)
        for rr, (rows, d, start) in enumerate(where):
            pair = res[2 * rr:2 * rr + 2]
            for h in range(2):
                for k in range(0, NA_ROWS, 2):
                    dt_ref[h, d + k] += pair[h][3][:, k * GRID_W:(k + 2) * GRID_W]
            dq_ref[rows, :] = jnp.concatenate([p[0] for p in pair], axis=1)
            dk_ref[pl.ds(start, SPAN_C), :] += jnp.concatenate([p[1] for p in pair], axis=1)
            dv_ref[pl.ds(start, SPAN_C), :] += jnp.concatenate([p[2] for p in pair], axis=1)

    qs = pl.BlockSpec((RPS * GRID_W, LANE), lambda p, r: (r, p))
    ks = pl.BlockSpec((T, LANE), lambda p, r: (0, p))
    ts = pl.BlockSpec((2, N_TAB, GRID_W, LANE), lambda p, r: (p, 0, 0, 0))
    return pl.pallas_call(
        body, name=name, grid=(HC // 2, ROWS // RPS),
        in_specs=[_qkv_rows(RPS * GRID_W, "qc"), _qkv_all("kc"), _qkv_all("vc"), qs, qs, qs, ts],
        out_specs=[qs, ks, ks, ts],
        out_shape=[jax.ShapeDtypeStruct((T, WC), F32)] * 3 + [jax.ShapeDtypeStruct((HC, N_TAB, GRID_W, LANE), F32)],
        compiler_params=_params(("parallel", "arbitrary")),
    )(qkv, qkv, qkv, oc, lse, doc, tables)


def _split3(v):
    hi = v.astype(BF16)
    r1 = v - hi.astype(F32)
    mid = r1.astype(BF16)
    lo = (r1 - mid.astype(F32)).astype(BF16)
    return hi, mid, lo


def _rpb_reduce(name, dtables):
    x = dtables.reshape(HC, N_TAB, GRID_W * LANE)
    c = jnp.arange(GRID_W)[:, None]
    lane = jnp.arange(LANE)[None, :]
    col = (lane // GRID_W) * LANE + jnp.clip(lane % GRID_W - c + (NA_COLS - 1), 0, 2 * NA_COLS - 2)
    col_onehot = (col.reshape(-1)[:, None] == jnp.arange(2 * LANE)[None, :]).astype(BF16)
    a2 = jnp.arange(N_TAB)[None, :]
    row_onehot = jnp.concatenate([(jnp.arange(16)[:, None] == a2 + u) & (a2 < 2 * NA_ROWS - 2) for u in range(2)],
                                 axis=1).astype(BF16)

    def body(x_ref, e_ref, f_ref, o_ref):
        y = sum(jnp.dot(part, e_ref[...], preferred_element_type=F32) for part in _split3(x_ref[...]))
        z = jnp.concatenate([y[:, :LANE], y[:, LANE:]], axis=0)
        o_ref[...] = sum(jnp.dot(f_ref[...], part, preferred_element_type=F32) for part in _split3(z))

    out = pl.pallas_call(
        body, name=name, grid=(HC,),
        in_specs=[pl.BlockSpec((None, N_TAB, GRID_W * LANE), lambda h: (h, 0, 0)),
                  _whole((GRID_W * LANE, 2 * LANE)), _whole((16, 2 * N_TAB))],
        out_specs=pl.BlockSpec((None, 16, LANE), lambda h: (h, 0, 0)),
        out_shape=jax.ShapeDtypeStruct((HC, 16, LANE), F32), compiler_params=_params(("parallel",)),
    )(x, col_onehot, row_onehot)
    return out[:, :2 * NA_ROWS - 1, :2 * NA_COLS - 1]


TC = 128
CHUNK = 128
MARGIN = 8


def _shift_down(v, rows):
    return jnp.where(rows == 0, 0.0, pltpu.roll(v, 1, 0))


def _shift_up(v, rows):
    return jnp.where(rows == T - 1, 0.0, pltpu.roll(v, T - 1, 0))


def _conv(v, w, b, rows):
    return _shift_down(v, rows) * w[0:1] + v * w[1:2] + _shift_up(v, rows) * w[2:3] + b


FWD_BLOCKS = 4
BWD_BLOCKS = 1


def _ffn_mid_fwd(name, up, conv_w, conv_b):
    wide = FWD_BLOCKS * TC

    def body(xg_ref, xv_ref, wg_ref, wv_ref, bg_ref, bv_ref, o_ref):
        rows = lax.broadcasted_iota(jnp.int32, (T, TC), 0)
        for b in range(FWD_BLOCKS):
            lanes = slice(b * TC, (b + 1) * TC)
            ug = _conv(xg_ref[b], wg_ref[:, lanes], bg_ref[:, lanes], rows)
            uv = _conv(xv_ref[b], wv_ref[:, lanes], bv_ref[:, lanes], rows)
            o_ref[:, lanes] = (ug * jax.nn.sigmoid(ug) * uv).astype(BF16)

    gate = lambda shape: pl.BlockSpec(shape, lambda j: (0, j))
    val = lambda shape: pl.BlockSpec(shape, lambda j: (0, j + DFF // wide))
    return pl.pallas_call(
        body, name=name, grid=(DFF // wide,),
        in_specs=[pl.BlockSpec((FWD_BLOCKS, T, TC), lambda j: (j, 0, 0)),
                  pl.BlockSpec((FWD_BLOCKS, T, TC), lambda j: (j + DFF // wide, 0, 0)),
                  gate((3, wide)), val((3, wide)), gate((1, wide)), val((1, wide))],
        out_specs=pl.BlockSpec((T, wide), lambda j: (0, j)),
        out_shape=jax.ShapeDtypeStruct((T, DFF), BF16), compiler_params=_params(("parallel",)),
    )(up, up, conv_w, conv_w, conv_b, conv_b)


def _ffn_mid_bwd(name, dact, up, conv_w, conv_b):
    window = CHUNK + 2 * MARGIN
    centre = slice(MARGIN, MARGIN + CHUNK)

    def shifted(v):
        return pltpu.roll(v, 1, 0), pltpu.roll(v, window - 1, 0)

    def fold(v):
        return jnp.sum(v[centre].reshape(CHUNK // 8, 8, TC), axis=0)

    wide = BWD_BLOCKS * TC

    def body(da_ref, xg_ref, xv_ref, wg_ref, wv_ref, bg_ref, bv_ref, dx_ref, dw_ref, db_ref):
        for b in range(BWD_BLOCKS):
            block(b, da_ref, xg_ref, xv_ref, wg_ref, wv_ref, bg_ref, bv_ref, dx_ref, dw_ref, db_ref)

    def block(b, da_ref, xg_ref, xv_ref, wg_ref, wv_ref, bg_ref, bv_ref, dx_ref, dw_ref, db_ref):
        lanes = slice(b * TC, (b + 1) * TC)
        wg, wv, bg, bv = wg_ref[:, lanes], wv_ref[:, lanes], bg_ref[:, lanes], bv_ref[:, lanes]
        margin = jnp.zeros((MARGIN, TC), F32)
        last = T // CHUNK - 1

        def windows(c):
            if isinstance(c, int) and c == 0:
                rows = slice(0, CHUNK + MARGIN)
                return [jnp.concatenate([margin, v], axis=0)
                        for v in (da_ref[rows, lanes], xg_ref[b, rows, :], xv_ref[b, rows, :])]
            if isinstance(c, int) and c == last:
                rows = slice(T - CHUNK - MARGIN, T)
                return [jnp.concatenate([v, margin], axis=0)
                        for v in (da_ref[rows, lanes], xg_ref[b, rows, :], xv_ref[b, rows, :])]
            rows = pl.ds(pl.multiple_of(c * CHUNK - MARGIN, MARGIN), window)
            return [da_ref[rows, lanes], xg_ref[b, rows, :], xv_ref[b, rows, :]]

        def chunk(c, sums):
            r0 = c * CHUNK if isinstance(c, int) else pl.multiple_of(c * CHUNK, CHUNK)
            da, xg, xv = windows(c)
            xg_prev, xg_next = shifted(xg)
            xv_prev, xv_next = shifted(xv)
            ug = xg_prev * wg[0:1] + xg * wg[1:2] + xg_next * wg[2:3] + bg
            uv = xv_prev * wv[0:1] + xv * wv[1:2] + xv_next * wv[2:3] + bv
            sg = jax.nn.sigmoid(ug)
            dug = da * uv * (sg * (1.0 + ug * (1.0 - sg)))
            duv = da * (ug * sg)
            out = []
            for half, (x_prev, x, x_next, w, du) in enumerate(((xg_prev, xg, xg_next, wg, dug),
                                                               (xv_prev, xv, xv_next, wv, duv))):
                du_prev, du_next = shifted(du)
                dx = du_next * w[0:1] + du * w[1:2] + du_prev * w[2:3]
                dx_ref[half, pl.ds(r0, CHUNK), lanes] = dx[centre].astype(BF16)
                out += [fold(x_prev * du), fold(x * du), fold(x_next * du), fold(du)]
            return tuple(s + o for s, o in zip(sums, out))

        sums = chunk(0, tuple(jnp.zeros((8, TC), F32) for _ in range(8)))
        sums = lax.fori_loop(1, last, chunk, sums)
        sums = chunk(last, sums)
        rows = [jnp.sum(s, axis=0, keepdims=True) for s in sums]
        for half in range(2):
            dw_ref[half, :, lanes] = jnp.concatenate(rows[4 * half:4 * half + 3], axis=0)
            db_ref[half, :, lanes] = rows[4 * half + 3]

    gate = lambda shape: pl.BlockSpec(shape, lambda j: (0, j))
    val = lambda shape: pl.BlockSpec(shape, lambda j: (0, j + DFF // wide))
    return pl.pallas_call(
        body, name=name, grid=(DFF // wide,),
        in_specs=[gate((T, wide)), pl.BlockSpec((BWD_BLOCKS, T, TC), lambda j: (j, 0, 0)),
                  pl.BlockSpec((BWD_BLOCKS, T, TC), lambda j: (j + DFF // wide, 0, 0)),
                  gate((3, wide)), val((3, wide)), gate((1, wide)), val((1, wide))],
        out_specs=[pl.BlockSpec((2, T, wide), lambda j: (0, 0, j)), pl.BlockSpec((2, 3, wide), lambda j: (0, 0, j)),
                   pl.BlockSpec((2, 1, wide), lambda j: (0, 0, j))],
        out_shape=[jax.ShapeDtypeStruct((2, T, DFF), BF16), jax.ShapeDtypeStruct((2, 3, DFF), F32),
                   jax.ShapeDtypeStruct((2, 1, DFF), F32)],
        compiler_params=_params(("parallel",)),
    )(dact, up, up, conv_w, conv_w, conv_b, conv_b)


def _dup_spec(tm, nj):
    per = DFF // nj
    return pl.BlockSpec((None, tm, nj), lambda a, b, j: (j // per, 0 if tm == T else b, j % per))


def _dup_spec_tn(tm, nj):
    per = DFF // nj
    return pl.BlockSpec((None, tm, nj), lambda j, kt, r: (j // per, 0, j % per))


def _adamw_math(w, g, m, v):
    m = ADAM_B1 * m + (1.0 - ADAM_B1) * g
    v = ADAM_B2 * v + (1.0 - ADAM_B2) * (g * g)
    m_hat = m / (1.0 - ADAM_B1 ** ADAM_STEP)
    v_hat = v / (1.0 - ADAM_B2 ** ADAM_STEP)
    delta = -ADAM_LR * (m_hat / (jnp.sqrt(v_hat) + ADAM_EPS) + ADAM_WD * w)
    return delta, m, v


ADAM_BLOCK = 256 * 1408


def _adamw_sharded(name, w, m, v, parts):
    _, r, c = w.shape
    tr = max(t for t in range(16, r + 1, 16) if r % t == 0 and t * c <= ADAM_BLOCK)

    def body(w_ref, m_ref, v_ref, p0_ref, p1_ref, g_ref, d_ref, nm_ref, nv_ref):
        def run(p_ref):
            g = p_ref[0].astype(F32)
            for k in range(1, N_DEV):
                g = g + p_ref[k].astype(F32)
            d, nm, nv = _adamw_math(w_ref[...], g, m_ref[...], v_ref[...])
            g_ref[...] = g
            d_ref[...] = d
            nm_ref[...] = nm
            nv_ref[...] = nv

        @pl.when(pl.program_id(0) == 0)
        def _():
            run(p0_ref)

        @pl.when(pl.program_id(0) == 1)
        def _():
            run(p1_ref)

    ws = pl.BlockSpec((None, tr, c), lambda l, i: (l, i, 0))
    p0 = pl.BlockSpec((N_DEV, tr, c), lambda l, i: (0, jnp.where(l == 0, i, r // tr - 1), 0))
    p1 = pl.BlockSpec((N_DEV, tr, c), lambda l, i: (0, jnp.where(l == 1, i, 0), 0))
    return pl.pallas_call(
        body, name=name, grid=(DEPTH, r // tr), in_specs=[ws, ws, ws, p0, p1], out_specs=[ws] * 4,
        out_shape=[jax.ShapeDtypeStruct(w.shape, F32)] * 4, compiler_params=_params(("arbitrary", "arbitrary")),
    )(w, m, v, *parts)


def _sum_devices(name, parts):
    r = parts.shape[1]

    def body(p_ref, o_ref):
        g = p_ref[0]
        for k in range(1, N_DEV):
            g = g + p_ref[k]
        o_ref[...] = g

    return pl.pallas_call(
        body, name=name, in_specs=[pl.BlockSpec((N_DEV, r, LANE), lambda: (0, 0, 0))],
        out_specs=pl.BlockSpec((r, LANE), lambda: (0, 0)), out_shape=jax.ShapeDtypeStruct((r, LANE), F32),
        compiler_params=_params(),
    )(parts)


def _adamw_small(name, ws, gs, ms, vs):
    n = len(ws)
    shapes = [w.shape for w in ws]
    ws, gs, ms, vs = ([a.reshape(1, -1) if a.ndim == 1 else a for a in arrs] for arrs in (ws, gs, ms, vs))
    specs = [pl.BlockSpec(memory_space=pltpu.VMEM)] * n

    def body(*refs):
        for i in range(n):
            w_ref, g_ref, m_ref, v_ref = (refs[k * n + i] for k in range(4))
            d, nm, nv = _adamw_math(w_ref[...], g_ref[...], m_ref[...], v_ref[...])
            refs[4 * n + i][...] = d
            refs[5 * n + i][...] = nm
            refs[6 * n + i][...] = nv

    outs = pl.pallas_call(
        body, name=name, in_specs=specs * 4, out_specs=specs * 3,
        out_shape=[jax.ShapeDtypeStruct(w.shape, F32) for w in ws] * 3, compiler_params=_params(),
    )(*ws, *gs, *ms, *vs)
    outs = [o.reshape(shapes[i % n]) for i, o in enumerate(outs)]
    return outs[:n], outs[n:2 * n], outs[2 * n:]


def _pack(arrays):
    flat = jnp.concatenate([a.reshape(-1) for a in arrays])
    pad = (-flat.shape[0]) % (8 * LANE)
    return jnp.pad(flat, (0, pad)).reshape(-1, LANE)


def _unpack(buf, shapes):
    flat, out, off = buf.reshape(-1), [], 0
    for s in shapes:
        n = 1
        for d in s:
            n *= d
        out.append(flat[off:off + n].reshape(s))
        off += n
    return out


def _local_step(x, target, small, weights, conv_w_full, hand_over, used):
    cos2, sin2 = _rope_tables()
    bias_a = _dilation_bias()
    tables = [_rpb_tables(f"rpb_tables_{l}", small["rpb_c"][l]) for l in range(DEPTH)]
    saved, carry = [], 0.0
    for l in range(DEPTH):
        g1, g2 = small["ln_attn"][l][None] + carry, small["ln_ffn"][l][None]
        gain, sink, cb = small["mix_gain"][l][None], small["sink_b"][l], small["conv_b"][l][None]
        cw = conv_w_full[l]
        bias = tables[l]
        h1, qkv = _prologue_matmul(f"proj_in_{l}", _rmsnorm_rows, [x, g1], [D, None],
                                   weights("w_in", l, [cos2, sin2, bias_a] + tables if l == 0 else x),
                                   (D, 1024), lambda j: (0, j), 1024, epilogue=_rope_epilogue, extras=(cos2, sin2),
                                   out_dtype=BF16)
        zero = used(f"proj_in_{l}", qkv)
        qb, kb, vb = (qkv[:, BLOCK_OF[n] * LANE:BLOCK_OF[n] * LANE + w] for n, w in (("qb", WB), ("kb", WKV), ("vb", WKV)))
        oa, lse_a = _attn_a_fwd(f"attn_a_{l}", qkv, bias_a)
        ob, lse_b = _attn_b_fwd(f"attn_b_{l}", qb, kb, vb, sink + zero)
        oc, lse_c = _attn_c_fwd(f"attn_c_{l}", qkv, bias)
        mixed, x_mid = _prologue_matmul(f"proj_out_{l}", _mix_rows, [oa, ob, oc, gain + used(f"attn_{l}", oc)],
                                        [WA, WB, WC, None],
                                        weights("w_out", l, oc), (N_DEV, D // N_DEV, 512), lambda j: (0, 0, j), 512,
                                        res=x)
        h2, up = _prologue_matmul(f"ffn_up_{l}", _rmsnorm_rows, [x_mid, g2 + used(f"proj_out_{l}", x_mid)], [D, None],
                                  weights("w_up", l, x_mid), (D, 1024), lambda j: (0, j), 1024, blocked_out=True)
        act = _ffn_mid_fwd(f"ffn_mid_{l}", up, cw, cb + used(f"ffn_up_{l}", up))
        x_out = _nn_rows(f"ffn_down_{l}", act, weights("w_down", l, act), x_mid, 4, 1024, 1024)
        carry = used(f"ffn_down_{l}", x_out)
        saved.append(dict(x=x, h1=h1, qkv=(qkv, qb, kb, vb), o=(oa, ob, oc), lse=(lse_a, lse_b, lse_c), mixed=mixed,
                          x_mid=x_mid, h2=h2, up=up, act=act, g1=g1, g2=g2, gain=gain, sink=sink, cb=cb, cw=cw, bias=bias))
        x = x_out

    loss8, dx, dxb, d_ln_final = _loss_head(x, small["ln_final"][None], target)
    sgrads = [None] * DEPTH
    for l in reversed(range(DEPTH)):
        s = saved[l]
        qkv, qb, kb, vb = s["qkv"]
        oa, ob, oc = s["o"]
        wg_in, wg_out = weights("w_in", l, None), weights("w_out", l, None)
        wg_up, wg_down = weights("w_up", l, None), weights("w_down", l, None)
        g_down = _tn_rows(f"wgrad_down_{l}", s["act"], dxb, wg_down.shape[1], 2, 512)
        zero = hand_over("w_down", l, g_down)
        dact = _nt_rows(f"dgrad_down_{l}", dxb, wg_down, 4, 512)
        dup, d_cw, d_cb = _ffn_mid_bwd(f"ffn_mid_bwd_{l}", dact, s["up"], s["cw"], s["cb"] + zero)
        g_up = _tn_cols(f"wgrad_up_{l}", s["h2"], dup, _dup_spec_tn, 2 * DFF, DFF // 2)
        zero = hand_over("w_up", l, g_up)
        dh2 = _nt_cols(f"dgrad_up_{l}", dup, _dup_spec, wg_up, DFF // 2)
        dx, dxb, d_g2 = _rmsnorm_bwd(f"norm_ffn_bwd_{l}", dh2, s["x_mid"], s["g2"] + zero, dx)
        g_out = _tn_rows(f"wgrad_out_{l}", s["mixed"], dxb, wg_out.shape[1], 2, D)
        zero = hand_over("w_out", l, g_out)
        dmixed = _nt_rows(f"dgrad_out_{l}", dxb, wg_out, 2, T)
        doa, dob, doc, d_gain = _mix_bwd(f"mix_bwd_{l}", dmixed, oa, ob, oc, s["gain"] + zero)
        lse_a, lse_b, lse_c = s["lse"]
        dqa, dka, dva = _attn_a_bwd(f"attn_a_bwd_{l}", qkv, oa, lse_a, doa, bias_a)
        dqb, dkb, dvb, d_sink = _attn_b_bwd(f"attn_b_bwd_{l}", qb, kb, vb, ob, lse_b, dob, s["sink"])
        dqc, dkc, dvc, d_bias = _attn_c_bwd(f"attn_c_bwd_{l}", qkv, oc, lse_c, doc, s["bias"])
        d_rpb = _rpb_reduce(f"rpb_reduce_{l}", d_bias)
        dproj = _rope_bwd(f"rope_bwd_{l}", (dqa, dka, dva, dqb, dkb, dvb, dqc, dkc, dvc), cos2, sin2)
        g_in = _tn_cols(f"wgrad_in_{l}", s["h1"], dproj,
                        lambda tm, tn: pl.BlockSpec((tm, tn), lambda j, kt, r: (0, j)), IN_COLS, 1024)
        zero = hand_over("w_in", l, g_in)
        dh1 = _nt_cols(f"dgrad_in_{l}", dproj, lambda tm, nc: pl.BlockSpec((tm, nc), lambda kt, i, j: (i, j)), wg_in,
                       IN_COLS // 2)
        dx, dxb, d_g1 = _rmsnorm_bwd(f"norm_attn_bwd_{l}", dh1, s["x"], s["g1"] + zero, dx)
        sgrads[l] = dict(ln_attn=d_g1[0], sink_b=d_sink[0, :HB], rpb_c=d_rpb, mix_gain=d_gain[0], ln_ffn=d_g2[0],
                         conv_w=d_cw.transpose(1, 0, 2).reshape(3, 2 * DFF), conv_b=d_cb.reshape(2 * DFF))
    return loss8[0, 0], dx, d_ln_final[0], sgrads


SMALL_NAMES = ("ln_attn", "sink_b", "rpb_c", "mix_gain", "ln_ffn", "conv_b")


def kernel(x, ln_attn, w_in, sink_b, rpb_c, mix_gain, w_out, ln_ffn, w_up, conv_w, conv_b, w_down, ln_final, loss_target, m_ln_attn, m_w_in, m_sink_b, m_rpb_c, m_mix_gain, m_w_out, m_ln_ffn, m_w_up, m_conv_w, m_conv_b, m_w_down, m_ln_final, v_ln_attn, v_w_in, v_sink_b, v_rpb_c, v_mix_gain, v_w_out, v_ln_ffn, v_w_up, v_conv_w, v_conv_b, v_w_down, v_ln_final):
    me = 4 * lax.axis_index("x") + 2 * lax.axis_index("y") + lax.axis_index("c")
    small = dict(ln_attn=ln_attn, sink_b=sink_b, rpb_c=rpb_c, mix_gain=mix_gain, ln_ffn=ln_ffn, conv_b=conv_b,
                 ln_final=ln_final)

    names = ("w_in", "w_out", "w_up", "w_down")
    shards = dict(w_in=w_in, w_out=w_out, w_up=w_up, w_down=w_down)
    order = [(n, l) for l in range(DEPTH) for n in names]
    conv_key = ("conv_w", 0)
    started, arrived, forwarded, gathered = {}, {}, {}, {}

    def side_by_side(k):
        return k[0] in ("w_in", "w_up")

    def slot_of(k):
        return _col_slot(shards[k[0]].shape[2]) if side_by_side(k) else _lead_slot

    def begin(name, ks, zero):
        srcs = [_pack([conv_w]) + zero if k == conv_key else (shards[k[0]][k[1]] + zero).astype(BF16) for k in ks]
        lands = [lax.empty((s.shape[0], N_DEV * s.shape[1]) if side_by_side(k) else (N_DEV,) + s.shape, s.dtype)
                 for k, s in zip(ks, srcs)]
        peers = [ALL_PEERS if k == conv_key else NEAR_PEERS for k in ks]
        send, recv, bufs, tok = _copy_start(name, srcs + lands, _gather_plan(peers, [slot_of(k) for k in ks]),
                                            [len(p) + 1 for p in peers])
        for i, k in enumerate(ks):
            started[k] = (send[i], recv[i], bufs[i], bufs[len(ks) + i], peers[i])
        return tok

    token = begin("gather_start_first", order[:1], 0.0)
    token = begin("gather_start_rest", [conv_key] + order[1:], token[0, 0])

    def arrive(k, after):
        send, recv, src, land, peers = started[k]
        arrived[k] = _copy_wait(f"gather_{k[0]}_{k[1]}_arrived", [src, land], [send], [recv],
                                _gather_plan([peers], [slot_of(k)]), after)

    queue = list(order)

    def advance(after):
        if not queue:
            return 0.0
        k = queue.pop(0)
        arrive(k, after)
        forwarded[k] = _copy_start(f"gather_{k[0]}_{k[1]}_forward", [arrived[k][1]], _forward_plan(slot_of(k)),
                                   [len(OTHER_CHIPS)])
        return forwarded[k][3][0, 0]

    pass_on_behind = ("proj_in_0", "attn_0", "ffn_up_0", "ffn_down_0", "proj_in_1", "attn_1", "ffn_up_1")

    def used(point, result):
        return advance(result) if point in pass_on_behind else 0.0

    def weights(n, l, after):
        k = (n, l)
        if k not in gathered:
            if k not in forwarded:
                advance(after)
            send_b, recv_b, (land,), _ = forwarded[k]
            (gathered[k],) = _copy_wait(f"gather_{n}_{l}_done", [land], send_b, recv_b, _forward_plan(slot_of(k)),
                                        after)
        return gathered[k]

    pending = {}

    def hand_over(n, l, g):
        shard = shards[n].shape[1:]
        send, recv, bufs, tok = _copy_start(f"send_grad_{n}_{l}", [g, lax.empty((N_DEV,) + shard, g.dtype)],
                                            _scatter_plan(slot_of((n, l))), [len(ALL_PEERS) + 1])
        pending[(n, l)] = (send, recv, bufs)
        return tok[0, 0]

    def received(k, after):
        send, recv, bufs = pending[k]
        return _copy_wait(f"recv_grad_{k[0]}_{k[1]}", bufs, send, recv, _scatter_plan(slot_of(k)), after)[1]

    arrive(conv_key, token)
    cw_all = arrived[conv_key][1]
    nup = w_up.shape[2]
    cw_shards = cw_all.reshape(N_DEV, -1)[:, :DEPTH * 3 * nup].reshape(N_DEV, DEPTH, 3, nup)
    conv_w_full = cw_shards.transpose(1, 2, 0, 3).reshape(DEPTH, 3, N_DEV * nup)

    loss_local, dx, d_ln_final, sgrads = _local_step(
        x[0], loss_target[0], dict(small, ln_attn=ln_attn + token[0, 0]), weights, conv_w_full, hand_over, used)

    stacked = [jnp.stack([sgrads[l][n] for l in range(DEPTH)]) for n in SMALL_NAMES + ("conv_w",)] + [d_ln_final]
    shapes = [a.shape for a in stacked]
    mine = _pack(stacked)
    send_s, recv_s, bufs_s, _ = _copy_start("gather_small_grads_start", [mine, lax.empty((N_DEV,) + mine.shape, F32)],
                                            _gather_plan([ALL_PEERS], [_lead_slot]), [len(ALL_PEERS) + 1])

    big, after = {}, dx
    moments = dict(w_in=(m_w_in, v_w_in), w_out=(m_w_out, v_w_out), w_up=(m_w_up, v_w_up), w_down=(m_w_down, v_w_down))
    for n in reversed(names):
        parts = (received((n, 0), after), received((n, 1), after))
        big[n] = _adamw_sharded(f"adamw_{n}", shards[n], *moments[n], parts)
        after = big[n][1]

    _, everyone = _copy_wait("gather_small_grads_done", bufs_s, send_s, recv_s,
                             _gather_plan([ALL_PEERS], [_lead_slot]), after)
    g_small = _unpack(_sum_devices("sum_small_grads", everyone), shapes)
    g = dict(zip(SMALL_NAMES + ("conv_w", "ln_final"), g_small))
    g["conv_w"] = lax.dynamic_slice_in_dim(g["conv_w"], me * nup, nup, axis=2)

    snames = SMALL_NAMES + ("conv_w", "ln_final")
    sw = dict(small, conv_w=conv_w)
    sm = dict(ln_attn=m_ln_attn, sink_b=m_sink_b, rpb_c=m_rpb_c, mix_gain=m_mix_gain, ln_ffn=m_ln_ffn,
              conv_b=m_conv_b, conv_w=m_conv_w, ln_final=m_ln_final)
    sv = dict(ln_attn=v_ln_attn, sink_b=v_sink_b, rpb_c=v_rpb_c, mix_gain=v_mix_gain, ln_ffn=v_ln_ffn,
              conv_b=v_conv_b, conv_w=v_conv_w, ln_final=v_ln_final)
    s_delta, s_m, s_v = (dict(zip(snames, out)) for out in _adamw_small(
        "adamw_small", [sw[n] for n in snames], [g[n] for n in snames], [sm[n] for n in snames],
        [sv[n] for n in snames]))

    loss = lax.psum(loss_local, ("x", "y", "c"))
    outputs = ("ln_attn", "w_in", "sink_b", "rpb_c", "mix_gain", "w_out", "ln_ffn", "w_up", "conv_w", "conv_b",
               "w_down", "ln_final")
    grads = [big[n][0] if n in big else g[n] for n in outputs]
    deltas = [big[n][1] if n in big else s_delta[n] for n in outputs]
    new_m = [big[n][2] if n in big else s_m[n] for n in outputs]
    new_v = [big[n][3] if n in big else s_v[n] for n in outputs]
    return (loss, dx[None], *grads, *deltas, *new_m, *new_v)
```

```python
import functools

import jax
import jax.numpy as jnp
from jax import lax
from jax.experimental import pallas as pl
from jax.experimental.pallas import tpu as pltpu

F32 = jnp.float32
BF16 = jnp.bfloat16

N_DEV = 8
T = 2048
D = 2048
DEPTH = 2
HD = 64
HA, HB, HKV, HC = 12, 10, 2, 10
WA, WB, WKV, WC = HA * HD, HB * HD, HKV * HD, HC * HD
IN_COLS = 3 * WA + WB + 2 * WKV + 3 * WC
DFF = 5632
GRID_W = 64
ROWS = T // GRID_W
NA_ROWS, NA_COLS = 8, 16
WINDOW_B = 128
EPS = 1e-6
NEG = -1e30
ROPE_THETA = 10000.0
LANE = 128
VMEM_LIMIT = 56 * 1024 * 1024

ADAM_LR, ADAM_B1, ADAM_B2, ADAM_EPS, ADAM_WD, ADAM_STEP = 0.001, 0.9, 0.999, 1e-08, 0.01, 10

GROUPS = (("qa", WA, True, True), ("ka", WA, True, False), ("va", WA, False, False),
          ("qb", WB, True, True), ("kb", WKV, True, False), ("vb", WKV, False, False),
          ("qc", WC, False, True), ("kc", WC, False, False), ("vc", WC, False, False))


def _params(sem=None):
    return pltpu.CompilerParams(dimension_semantics=sem, vmem_limit_bytes=VMEM_LIMIT)


HBM_SPEC = pl.BlockSpec(memory_space=pltpu.HBM)
SEM_SPEC = pl.BlockSpec(memory_space=pltpu.SEMAPHORE)
DATAFLOW = pltpu.SideEffectType.DATAFLOW_SIDE_EFFECTING


ALL_PEERS = tuple((p >> 2 & 1, p >> 1 & 1, p & 1) for p in range(1, N_DEV))
OTHER_CHIPS = ((1, 0, 0), (0, 1, 0), (1, 1, 0))
NEAR_PEERS = ((0, 0, 1),) + OTHER_CHIPS


def _flip(x, y, c, f):
    return (1 - x if f[0] else x, 1 - y if f[1] else y, 1 - c if f[2] else c)


def _index(pos):
    return 4 * pos[0] + 2 * pos[1] + pos[2]


class _LocalCopy:
    def __init__(self, src, dst, sem):
        self.copy = pltpu.make_async_copy(src, dst, sem)

    def start(self):
        self.copy.start()

    def wait_send(self):
        self.copy.wait()

    def wait_recv(self):
        pass


def _descriptors(plan, bufs, send_sems, recv_sems):
    x, y, c = lax.axis_index("x"), lax.axis_index("y"), lax.axis_index("c")
    return [_LocalCopy(src, dst, send_sems[g].at[i]) if partner is None else
            pltpu.make_async_remote_copy(src_ref=src, dst_ref=dst, send_sem=send_sems[g].at[i],
                                         recv_sem=recv_sems[g].at[i], device_id=partner,
                                         device_id_type=pl.DeviceIdType.MESH)
            for g, copies in enumerate(plan(bufs, x, y, c)) for i, (src, dst, partner) in enumerate(copies)]


def _copy_start(name, bufs, plan, sizes):
    nb, ng = len(bufs), len(sizes)

    def body(*refs):
        for d in _descriptors(plan, refs[:nb], refs[nb:nb + ng], refs[nb + ng:nb + 2 * ng]):
            d.start()
        refs[2 * nb + 2 * ng][...] = jnp.zeros((8, LANE), F32)

    outs = pl.pallas_call(
        body, name=name,
        out_shape=[pltpu.SemaphoreType.DMA((s,)) for s in sizes] * 2 + [pltpu.HBM(b.shape, b.dtype) for b in bufs]
        + [jax.ShapeDtypeStruct((8, LANE), F32)],
        in_specs=[HBM_SPEC] * nb,
        out_specs=[SEM_SPEC] * (2 * ng) + [HBM_SPEC] * nb + [pl.BlockSpec(memory_space=pltpu.VMEM)],
        input_output_aliases={i: 2 * ng + i for i in range(nb)},
        compiler_params=pltpu.CompilerParams(has_side_effects=DATAFLOW),
    )(*[pltpu.with_memory_space_constraint(b, pltpu.HBM) for b in bufs])
    return outs[:ng], outs[ng:2 * ng], outs[2 * ng:2 * ng + nb], outs[2 * ng + nb]


def _copy_wait(name, bufs, send_sems, recv_sems, plan, after):
    nb, ng = len(bufs), len(send_sems)
    after = list(after) if isinstance(after, (list, tuple)) else [after]

    def body(*refs):
        for d in _descriptors(plan, refs[:nb], refs[nb:nb + ng], refs[nb + ng:nb + 2 * ng]):
            d.wait_send()
            d.wait_recv()

    return pl.pallas_call(
        body, name=name, out_shape=[pltpu.HBM(b.shape, b.dtype) for b in bufs],
        in_specs=[HBM_SPEC] * nb + [SEM_SPEC] * (2 * ng) + [pl.BlockSpec(memory_space=pl.ANY)] * len(after),
        out_specs=[HBM_SPEC] * nb, input_output_aliases={i: i for i in range(nb)},
        compiler_params=pltpu.CompilerParams(has_side_effects=DATAFLOW),
    )(*bufs, *send_sems, *recv_sems, *after)


def _lead_slot(ref, k):
    return ref.at[k]


def _col_slot(width):
    return lambda ref, k: ref.at[:, pl.ds(pl.multiple_of(k * width, LANE), width)]


def _gather_plan(peer_sets, slots):
    def plan(bufs, x, y, c):
        n = len(peer_sets)
        return [[(bufs[i], slots[i](bufs[n + i], _index((x, y, c))), _flip(x, y, c, f)) for f in peers]
                + [(bufs[i], slots[i](bufs[n + i], _index((x, y, c))), None)] for i, peers in enumerate(peer_sets)]
    return plan


def _forward_plan(slot):
    def plan(bufs, x, y, c):
        pieces = [slot(bufs[0], _index(_flip(x, y, c, f))) for f in OTHER_CHIPS]
        return [[(p, p, _flip(x, y, c, (0, 0, 1))) for p in pieces]]
    return plan


def _scatter_plan(slot):
    def plan(bufs, x, y, c):
        me = _index((x, y, c))
        peers = [_flip(x, y, c, f) for f in ALL_PEERS]
        return [[(slot(bufs[0], _index(p)), bufs[1].at[me], p) for p in peers]
                + [(slot(bufs[0], me), bufs[1].at[me], None)]]
    return plan


def _flat2(v):
    return v.reshape(-1, v.shape[-1])


def _matmul(name, kind, a, a_spec, b, b_spec, out_shape, out_spec, grid, res=None, res_spec=None, acc_shape=None):
    dims = {"nn": (((1,), (0,)), ((), ())), "nt": NT_DIMS, "tn": TN_DIMS}[kind]
    nred = grid[-1]

    def body(*refs):
        if res is None:
            a_ref, b_ref, o_ref = refs[:3]
            r_ref = None
        else:
            a_ref, b_ref, r_ref, o_ref = refs[:4]
        part = lax.dot_general(_flat2(a_ref[...]), _flat2(b_ref[...]), dims, preferred_element_type=F32)

        def finish(total):
            if r_ref is not None:
                total = total + r_ref[...]
            o_ref[...] = total.reshape(o_ref.shape).astype(o_ref.dtype)

        if nred == 1:
            finish(part)
        else:
            acc_ref = refs[-1]
            k = pl.program_id(len(grid) - 1)

            @pl.when(k == 0)
            def _():
                acc_ref[...] = part

            @pl.when(jnp.logical_and(k > 0, k < nred - 1))
            def _():
                acc_ref[...] += part

            @pl.when(k == nred - 1)
            def _():
                finish(acc_ref[...] + part)

    ins, specs = [a, b], [a_spec, b_spec]
    if res is not None:
        ins.append(res)
        specs.append(res_spec)
    scratch = [] if nred == 1 else [pltpu.VMEM(acc_shape, F32)]
    return pl.pallas_call(
        body, name=name, grid=grid, in_specs=specs, out_specs=out_spec, out_shape=out_shape, scratch_shapes=scratch,
        compiler_params=_params(("parallel",) * (len(grid) - 1) + ("arbitrary",)),
    )(*ins)


def _nn_rows(name, a, wg, res, s, tn, tm):
    _, kj, n = wg.shape
    return _matmul(
        name, "nn", a, pl.BlockSpec((tm, s * kj), lambda j, i, r: (i, r)),
        wg, pl.BlockSpec((s, kj, tn), lambda j, i, r: (r, 0, j)),
        jax.ShapeDtypeStruct((T, n), F32), pl.BlockSpec((tm, tn), lambda j, i, r: (i, j)),
        (n // tn, T // tm, N_DEV // s), res=res, res_spec=pl.BlockSpec((tm, tn), lambda j, i, r: (i, j)),
        acc_shape=(tm, tn))


def _nt_cols(name, dc, dc_spec_of, w, nc):
    k, n = w.shape
    tm = tk = 1024
    return _matmul(
        name, "nt", dc, dc_spec_of(tm, nc),
        w, pl.BlockSpec((tk, nc), lambda kt, i, j: (kt, j)),
        jax.ShapeDtypeStruct((T, k), F32), pl.BlockSpec((tm, tk), lambda kt, i, j: (i, kt)),
        (k // tk, T // tm, n // nc), acc_shape=(tm, tk))


def _nt_rows(name, dc, wg, s, tm):
    _, kj, n = wg.shape
    return _matmul(
        name, "nt", dc, pl.BlockSpec((tm, n), lambda kt, i, r: (i, 0)),
        wg, pl.BlockSpec((s, kj, n), lambda kt, i, r: (kt, 0, 0)),
        jax.ShapeDtypeStruct((T, N_DEV * kj), F32), pl.BlockSpec((tm, s * kj), lambda kt, i, r: (i, kt)),
        (N_DEV // s, T // tm, 1))


def _tn_cols(name, a, dc, dc_spec_of, n, tn):
    k = a.shape[1]
    tk = 512
    return _matmul(
        name, "tn", a, pl.BlockSpec((T, tk), lambda j, kt, r: (0, kt)),
        dc, dc_spec_of(T, tn),
        jax.ShapeDtypeStruct((k, n), BF16), pl.BlockSpec((tk, tn), lambda j, kt, r: (kt, j)),
        (n // tn, k // tk, 1))


def _tn_rows(name, a, dc, kj, s, tn):
    n = dc.shape[1]
    return _matmul(
        name, "tn", a, pl.BlockSpec((T, s * kj), lambda kt, j, r: (0, kt)),
        dc, pl.BlockSpec((T, tn), lambda kt, j, r: (0, j)),
        jax.ShapeDtypeStruct((N_DEV, kj, n), BF16), pl.BlockSpec((s, kj, tn), lambda kt, j, r: (kt, 0, j)),
        (N_DEV // s, n // tn, 1))


TR = 512


def _rows(width):
    return pl.BlockSpec((TR, width), lambda i: (i, 0))


def _whole(shape):
    return pl.BlockSpec(shape, lambda i: (0,) * len(shape))


def _rmsnorm_rows(x_ref, g_ref):
    xv = x_ref[...]
    r = lax.rsqrt(jnp.mean(xv * xv, axis=-1, keepdims=True) + EPS)
    return ((xv * r) * g_ref[...]).astype(BF16)


SUB = 256


def _prologue_matmul(name, prologue, ins, widths, w, w_block, w_index, tn, res=None, epilogue=None, extras=(),
                     out_dtype=F32, blocked_out=False):
    tm = 1024
    n = w.shape[-1]
    ni = len(ins)

    def body(*refs):
        w_ref = refs[ni]
        r_ref = refs[ni + 1] if res is not None else None
        x_refs = refs[ni + 1 + (res is not None):len(refs) - 3]
        h_ref, o_ref, h_scr = refs[-3:]

        @pl.when(pl.program_id(1) == 0)
        def _():
            h = prologue(*refs[:ni])
            h_scr[...] = h
            h_ref[...] = h

        for sub in range(tn // SUB):
            cols = slice(sub * SUB, (sub + 1) * SUB)
            w_cols = w_ref[(slice(None),) * (len(w_ref.shape) - 1) + (cols,)]
            part = jnp.dot(h_scr[...], _flat2(w_cols), preferred_element_type=F32)
            if r_ref is not None:
                part = part + r_ref[:, cols]
            if epilogue is not None:
                part = epilogue(pl.program_id(1) * (tn // SUB) + sub, part, *x_refs)
            if blocked_out:
                for b in range(SUB // LANE):
                    o_ref[sub * (SUB // LANE) + b] = part[:, b * LANE:(b + 1) * LANE].astype(out_dtype)
            else:
                o_ref[:, cols] = part.astype(out_dtype)

    tile = pl.BlockSpec((tm, tn), lambda i, j: (i, j))
    out_tile = pl.BlockSpec((tn // LANE, tm, LANE), lambda i, j: (j, i, 0)) if blocked_out else tile
    out_full = (n // LANE, T, LANE) if blocked_out else (T, n)
    specs = [pl.BlockSpec((1, D), lambda i, j: (0, 0)) if wd is None else pl.BlockSpec((tm, wd), lambda i, j: (i, 0))
             for wd in widths]
    specs.append(pl.BlockSpec(w_block, lambda i, j: w_index(j)))
    operands = list(ins) + [w]
    if res is not None:
        specs.append(tile)
        operands.append(res)
    specs += [pl.BlockSpec((tm, LANE), lambda i, j: (i, 0))] * len(extras)
    operands += list(extras)
    return pl.pallas_call(
        body, name=name, grid=(T // tm, n // tn), in_specs=specs,
        out_specs=[pl.BlockSpec((tm, D), lambda i, j: (i, 0)), out_tile],
        out_shape=[jax.ShapeDtypeStruct((T, D), BF16), jax.ShapeDtypeStruct(out_full, out_dtype)],
        scratch_shapes=[pltpu.VMEM((tm, D), BF16)], compiler_params=_params(("parallel", "arbitrary")),
    )(*operands)


def _rms_bwd_math(dy, xv, g):
    r = lax.rsqrt(jnp.mean(xv * xv, axis=-1, keepdims=True) + EPS)
    xhat = xv * r
    dxhat = dy * g
    dx = r * (dxhat - xhat * jnp.mean(dxhat * xhat, axis=-1, keepdims=True))
    return dx, dy * xhat


def _accumulate(ref, val):
    @pl.when(pl.program_id(0) == 0)
    def _():
        ref[...] = val

    @pl.when(pl.program_id(0) > 0)
    def _():
        ref[...] += val


def _rmsnorm_bwd(name, dy, x, g, res):
    def body(dy_ref, x_ref, g_ref, res_ref, dx_ref, dxb_ref, dg_ref):
        dx, dgr = _rms_bwd_math(dy_ref[...], x_ref[...], g_ref[...])
        tot = res_ref[...] + dx
        dx_ref[...] = tot
        dxb_ref[...] = tot.astype(BF16)
        _accumulate(dg_ref, jnp.sum(dgr, axis=0, keepdims=True))

    return pl.pallas_call(
        body, name=name, grid=(T // TR,), in_specs=[_rows(D), _rows(D), _whole((1, D)), _rows(D)],
        out_specs=[_rows(D), _rows(D), _whole((1, D))],
        out_shape=[jax.ShapeDtypeStruct((T, D), F32), jax.ShapeDtypeStruct((T, D), BF16),
                   jax.ShapeDtypeStruct((1, D), F32)],
        compiler_params=_params(("arbitrary",)),
    )(dy, x, g, res)


def _loss_head(x, g, target):
    def body(x_ref, g_ref, t_ref, loss_ref, dx_ref, dxb_ref, dg_ref):
        xv, gv = x_ref[...], g_ref[...]
        r = lax.rsqrt(jnp.mean(xv * xv, axis=-1, keepdims=True) + EPS)
        err = (xv * r) * gv - t_ref[...]
        part = 0.5 * jnp.sum(jnp.mean(err * err, axis=-1, keepdims=True))
        dx, dgr = _rms_bwd_math(err * (1.0 / D), xv, gv)
        dx_ref[...] = dx
        dxb_ref[...] = dx.astype(BF16)
        _accumulate(dg_ref, jnp.sum(dgr, axis=0, keepdims=True))
        _accumulate(loss_ref, jnp.full((8, LANE), part, F32))

    return pl.pallas_call(
        body, name="loss_head", grid=(T // TR,), in_specs=[_rows(D), _whole((1, D)), _rows(D)],
        out_specs=[_whole((8, LANE)), _rows(D), _rows(D), _whole((1, D))],
        out_shape=[jax.ShapeDtypeStruct((8, LANE), F32), jax.ShapeDtypeStruct((T, D), F32),
                   jax.ShapeDtypeStruct((T, D), BF16), jax.ShapeDtypeStruct((1, D), F32)],
        compiler_params=_params(("arbitrary",)),
    )(x, g, target)


MIX_OFFS = ((0, WA), (WA, WB), (WA + WB, WC))


def _mix_rows(oa_ref, ob_ref, oc_ref, g_ref):
    parts = []
    for ref, (off, w) in zip((oa_ref, ob_ref, oc_ref), MIX_OFFS):
        o = ref[...]
        r = lax.rsqrt(jnp.mean(o * o, axis=-1, keepdims=True) + EPS)
        parts.append(((o * r) * g_ref[:, off:off + w]).astype(BF16))
    return jnp.concatenate(parts, axis=1)


def _mix_bwd(name, dmixed, oa, ob, oc, gain):
    def body(dm_ref, oa_ref, ob_ref, oc_ref, g_ref, doa_ref, dob_ref, doc_ref, dg_ref):
        dgs = []
        for ref, dref, (off, w) in zip((oa_ref, ob_ref, oc_ref), (doa_ref, dob_ref, doc_ref), MIX_OFFS):
            dx, dgr = _rms_bwd_math(dm_ref[:, off:off + w], ref[...], g_ref[:, off:off + w])
            dref[...] = dx
            dgs.append(jnp.sum(dgr, axis=0, keepdims=True))
        _accumulate(dg_ref, jnp.concatenate(dgs, axis=1))

    return pl.pallas_call(
        body, name=name, grid=(T // TR,),
        in_specs=[_rows(D), _rows(WA), _rows(WB), _rows(WC), _whole((1, D))],
        out_specs=[_rows(WA), _rows(WB), _rows(WC), _whole((1, D))],
        out_shape=[jax.ShapeDtypeStruct((T, WA), F32), jax.ShapeDtypeStruct((T, WB), F32),
                   jax.ShapeDtypeStruct((T, WC), F32), jax.ShapeDtypeStruct((1, D), F32)],
        compiler_params=_params(("arbitrary",)),
    )(dmixed, oa, ob, oc, gain)


def _rope_tables():
    inv_freq = ROPE_THETA ** (-jnp.arange(0, HD, 2, dtype=F32) / HD)
    ang = jnp.arange(T, dtype=F32)[:, None] * inv_freq[None, :]
    cos, sin = jnp.cos(ang), jnp.sin(ang)
    cos2 = jnp.tile(jnp.concatenate([cos, cos], axis=1), (1, LANE // HD))
    sin2 = jnp.tile(jnp.concatenate([-sin, sin], axis=1), (1, LANE // HD))
    return cos2, sin2


def _rot_half(v):
    lane = lax.broadcasted_iota(jnp.int32, v.shape, 1)
    return jnp.where(lane % HD < HD // 2, pltpu.roll(v, LANE - HD // 2, 1), pltpu.roll(v, HD // 2, 1))


BLOCK_KINDS = tuple((rot, is_q) for _, w, rot, is_q in GROUPS for _ in range(w // LANE))
BLOCK_OF = {name: sum(w for _, w, _, _ in GROUPS[:g]) // LANE for g, (name, _, _, _) in enumerate(GROUPS)}


def _any_tile(j, tiles):
    return functools.reduce(jnp.logical_or, [j == t for t in tiles]) if tiles else False


def _rope_epilogue(j, tile, c_ref, s_ref):
    cv, sv = c_ref[...], s_ref[...]
    per, n_tiles = tile.shape[1] // LANE, IN_COLS // tile.shape[1]
    out = []
    for b in range(per):
        v = tile[:, b * LANE:(b + 1) * LANE]
        rot = _any_tile(j, [t for t in range(n_tiles) if BLOCK_KINDS[t * per + b][0]])
        is_q = _any_tile(j, [t for t in range(n_tiles) if BLOCK_KINDS[t * per + b][1]])
        if rot is not False:
            v = jnp.where(rot, v * cv + _rot_half(v) * sv, v)
        if is_q is not False:
            v = v * jnp.where(is_q, HD ** -0.5, 1.0)
        out.append(v)
    return jnp.concatenate(out, axis=1)


def _rope_bwd(name, grads, cos2, sin2):
    def body(*refs):
        ins, (c_ref, s_ref, o_ref) = refs[:9], refs[9:]
        cv, sv = c_ref[...], s_ref[...]
        off = 0
        for d_ref, (_, w, rot, is_q) in zip(ins, GROUPS):
            for b in range(w // LANE):
                v = d_ref[:, b * LANE:(b + 1) * LANE]
                if is_q:
                    v = v * (HD ** -0.5)
                if rot:
                    v = v * cv + _rot_half(v * sv)
                o_ref[:, off + b * LANE:off + (b + 1) * LANE] = v.astype(BF16)
            off += w

    return pl.pallas_call(
        body, name=name, grid=(T // TR,), in_specs=[_rows(w) for _, w, _, _ in GROUPS] + [_rows(LANE), _rows(LANE)],
        out_specs=_rows(IN_COLS), out_shape=jax.ShapeDtypeStruct((T, IN_COLS), BF16),
        compiler_params=_params(("parallel",)),
    )(*grads, cos2, sin2)


NT_DIMS = (((1,), (1,)), ((), ()))
TN_DIMS = (((0,), (0,)), ((), ()))


def _scores(q, k, bias, valid):
    s = lax.dot_general(q, k, NT_DIMS, preferred_element_type=F32)
    if bias is not None:
        s = s + bias
    if valid is not None:
        s = jnp.where(valid, s, NEG)
    return s


def _heads_fwd(heads):
    scores = [_scores(h["q"], h["k"], h.get("bias"), h.get("valid")) for h in heads]
    soft = []
    for s, h in zip(scores, heads):
        m = jnp.max(s, axis=1, keepdims=True)
        e = jnp.exp(s - m)
        l = jnp.sum(e, axis=1, keepdims=True)
        if h.get("sink") is not None:
            l = l + jnp.exp(h["sink"] - m)
        soft.append((e.astype(BF16), l, m + jnp.log(l)))
    return [(jnp.dot(e, h["v"], preferred_element_type=F32) / l, lse) for (e, l, lse), h in zip(soft, heads)]


def _heads_bwd(heads):
    dobs = [h["do"].astype(BF16) for h in heads]
    scores = [_scores(h["q"], h["k"], h.get("bias"), h.get("valid")) for h in heads]
    dps = [lax.dot_general(dob, h["v"], NT_DIMS, preferred_element_type=F32) for dob, h in zip(dobs, heads)]
    mid = []
    for s, dp, h in zip(scores, dps, heads):
        p = jnp.exp(s - h["lse"])
        delta = jnp.sum(h["do"] * h["o"], axis=1, keepdims=True)
        ds = p * (dp - delta)
        dsink = None if h.get("sink") is None else -jnp.exp(h["sink"] - h["lse"]) * delta
        mid.append((p.astype(BF16), ds, dsink))
    out = []
    for (pb, ds, dsink), dob, h in zip(mid, dobs, heads):
        dsb = ds.astype(BF16)
        out.append((jnp.dot(dsb, h["k"], preferred_element_type=F32),
                    lax.dot_general(dsb, h["q"], TN_DIMS, preferred_element_type=F32),
                    lax.dot_general(pb, dob, TN_DIMS, preferred_element_type=F32), ds, dsink))
    return out


def _per_head(cols):
    return jnp.concatenate([jnp.broadcast_to(c, (c.shape[0], HD)) for c in cols], axis=1)


DILATIONS = ((128, 1), (512, 4), (2048, 16))


BQ_A = 256
REACH_A = max(window // 2 for window, _ in DILATIONS)


def _first_key(i):
    return jnp.maximum(i * BQ_A - REACH_A, 0)


def _key_window_groups():
    groups = {}
    for i in range(T // BQ_A):
        width = min(T, (i + 1) * BQ_A + REACH_A) - max(i * BQ_A - REACH_A, 0)
        groups.setdefault(width, []).append(i)
    return groups


def _per_window(i, fn):
    for width, tiles in _key_window_groups().items():
        hit = functools.reduce(jnp.logical_or, [i == t for t in tiles])
        pl.when(hit)(functools.partial(fn, pl.multiple_of(_first_key(i), BQ_A), width))


def _dilation_bias():
    def body(o_ref):
        i = pl.program_id(0)
        t = i * BQ_A + lax.broadcasted_iota(jnp.int32, (BQ_A, T), 0)
        ad = jnp.abs(t - (_first_key(i) + lax.broadcasted_iota(jnp.int32, (BQ_A, T), 1)))
        count = jnp.zeros((BQ_A, T), jnp.int32)
        for window, r in DILATIONS:
            count += jnp.where(((ad & (r - 1)) == 0) & (ad <= window // 2), 1, 0)
        logs = jnp.where(count == 2, jnp.log(2.0), jnp.where(count == 3, jnp.log(3.0), 0.0)).astype(F32)
        o_ref[...] = jnp.where(count == 0, NEG, logs)

    return pl.pallas_call(
        body, name="dilation_bias", grid=(T // BQ_A,), out_specs=pl.BlockSpec((BQ_A, T), lambda i: (i, 0)),
        out_shape=jax.ShapeDtypeStruct((T, T), F32), compiler_params=_params(("parallel",)),
    )()


def _qkv_rows(rows, group):
    return pl.BlockSpec((rows, LANE), lambda p, i: (i, BLOCK_OF[group] + p))


def _qkv_all(group):
    return pl.BlockSpec((T, LANE), lambda p, i: (0, BLOCK_OF[group] + p))


def _attn_a_fwd(name, qkv, bias):
    def body(q_ref, k_ref, v_ref, b_ref, o_ref, lse_ref):
        def tile(first, width):
            b = b_ref[:, :width]
            outs = _heads_fwd([dict(q=q_ref[:, h * HD:(h + 1) * HD], k=k_ref[pl.ds(first, width), h * HD:(h + 1) * HD],
                                    v=v_ref[pl.ds(first, width), h * HD:(h + 1) * HD], bias=b) for h in range(2)])
            o_ref[...] = jnp.concatenate([o for o, _ in outs], axis=1)
            lse_ref[...] = _per_head([lse for _, lse in outs])

        _per_window(pl.program_id(1), tile)

    qs = pl.BlockSpec((BQ_A, LANE), lambda p, i: (i, p))
    ks = pl.BlockSpec((T, LANE), lambda p, i: (0, p))
    return pl.pallas_call(
        body, name=name, grid=(HA // 2, T // BQ_A),
        in_specs=[_qkv_rows(BQ_A, "qa"), _qkv_all("ka"), _qkv_all("va"), pl.BlockSpec((BQ_A, T), lambda p, i: (i, 0))],
        out_specs=[qs, qs],
        out_shape=[jax.ShapeDtypeStruct((T, WA), F32)] * 2, compiler_params=_params(("parallel", "parallel")),
    )(qkv, qkv, qkv, bias)


def _attn_a_bwd(name, qkv, oa, lse, doa, bias):
    def body(q_ref, k_ref, v_ref, o_ref, lse_ref, do_ref, b_ref, dq_ref, dk_ref, dv_ref):
        @pl.when(pl.program_id(1) == 0)
        def _():
            dk_ref[...] = jnp.zeros_like(dk_ref)
            dv_ref[...] = jnp.zeros_like(dv_ref)

        def tile(first, width):
            b = b_ref[:, :width]
            keys = pl.ds(first, width)
            sls = [slice(h * HD, (h + 1) * HD) for h in range(2)]
            res = _heads_bwd([dict(q=q_ref[:, sl], k=k_ref[keys, sl], v=v_ref[keys, sl], o=o_ref[:, sl],
                                   do=do_ref[:, sl], lse=lse_ref[:, sl.start:sl.start + 1], bias=b) for sl in sls])
            dq_ref[...] = jnp.concatenate([r[0] for r in res], axis=1)
            dk_ref[keys, :] += jnp.concatenate([r[1] for r in res], axis=1)
            dv_ref[keys, :] += jnp.concatenate([r[2] for r in res], axis=1)

        _per_window(pl.program_id(1), tile)

    qs = pl.BlockSpec((BQ_A, LANE), lambda p, i: (i, p))
    ks = pl.BlockSpec((T, LANE), lambda p, i: (0, p))
    return pl.pallas_call(
        body, name=name, grid=(HA // 2, T // BQ_A),
        in_specs=[_qkv_rows(BQ_A, "qa"), _qkv_all("ka"), _qkv_all("va"), qs, qs, qs,
                  pl.BlockSpec((BQ_A, T), lambda p, i: (i, 0))], out_specs=[qs, ks, ks],
        out_shape=[jax.ShapeDtypeStruct((T, WA), F32)] * 3, compiler_params=_params(("parallel", "arbitrary")),
    )(qkv, qkv, qkv, oa, lse, doa, bias)


BQ_B = 128
SPAN_B = BQ_B + 2 * WINDOW_B


def _window_b(i):
    start = pl.multiple_of(jnp.clip(i * BQ_B - WINDOW_B, 0, T - SPAN_B), BQ_B)
    qpos = i * BQ_B + lax.broadcasted_iota(jnp.int32, (BQ_B, SPAN_B), 0)
    kpos = start + lax.broadcasted_iota(jnp.int32, (BQ_B, SPAN_B), 1)
    return start, jnp.abs(qpos - kpos) <= WINDOW_B


GROUP_B = HB // HKV


def _stack_group(ref, g):
    return jnp.concatenate([ref[:, h * HD:(h + 1) * HD] for h in range(g * GROUP_B, (g + 1) * GROUP_B)], axis=0)


def _sink_column(sink_ref, g):
    return jnp.concatenate([jnp.full((BQ_B, 1), sink_ref[h], F32) for h in range(g * GROUP_B, (g + 1) * GROUP_B)],
                           axis=0)


def _unstack(stacked):
    return [s[j * BQ_B:(j + 1) * BQ_B] for s in stacked for j in range(GROUP_B)]


def _attn_b_fwd(name, qb, kb, vb, sink):
    def body(sink_ref, q_ref, k_ref, v_ref, o_ref, lse_ref):
        start, valid = _window_b(pl.program_id(0))
        valid = jnp.concatenate([valid] * GROUP_B, axis=0)
        kw, vw = k_ref[pl.ds(start, SPAN_B), :], v_ref[pl.ds(start, SPAN_B), :]
        outs = _heads_fwd([dict(q=_stack_group(q_ref, g), k=kw[:, g * HD:(g + 1) * HD], v=vw[:, g * HD:(g + 1) * HD],
                                valid=valid, sink=_sink_column(sink_ref, g)) for g in range(HKV)])
        o_ref[...] = jnp.concatenate(_unstack([o for o, _ in outs]), axis=1)
        lse_ref[...] = _per_head(_unstack([lse for _, lse in outs]))

    qs = pl.BlockSpec((BQ_B, WB), lambda i: (i, 0))
    return pl.pallas_call(
        body, name=name, grid=(T // BQ_B,),
        in_specs=[pl.BlockSpec(memory_space=pltpu.SMEM), qs, _whole((T, WKV)), _whole((T, WKV))],
        out_specs=[qs, qs],
        out_shape=[jax.ShapeDtypeStruct((T, WB), F32)] * 2, compiler_params=_params(("parallel",)),
    )(sink, qb, kb, vb)


def _attn_b_bwd(name, qb, kb, vb, ob, lse, dob, sink):
    def body(sink_ref, q_ref, k_ref, v_ref, o_ref, lse_ref, do_ref, dq_ref, dk_ref, dv_ref, dsink_ref):
        i = pl.program_id(0)
        start, valid = _window_b(i)
        valid = jnp.concatenate([valid] * GROUP_B, axis=0)
        kw, vw = k_ref[pl.ds(start, SPAN_B), :], v_ref[pl.ds(start, SPAN_B), :]
        res = _heads_bwd([dict(q=_stack_group(q_ref, g), k=kw[:, g * HD:(g + 1) * HD], v=vw[:, g * HD:(g + 1) * HD],
                               o=_stack_group(o_ref, g), do=_stack_group(do_ref, g),
                               lse=jnp.concatenate([lse_ref[:, h * HD:h * HD + 1]
                                                    for h in range(g * GROUP_B, (g + 1) * GROUP_B)], axis=0),
                               valid=valid, sink=_sink_column(sink_ref, g)) for g in range(HKV)])
        dks, dvs = [r[1] for r in res], [r[2] for r in res]
        lane = lax.broadcasted_iota(jnp.int32, (1, LANE), 1)
        dsink = jnp.zeros((1, LANE), F32)
        for h, rows in enumerate(_unstack([r[4] for r in res])):
            dsink += jnp.where(lane == h, jnp.sum(rows), 0.0)
        dq_ref[...] = jnp.concatenate(_unstack([r[0] for r in res]), axis=1)

        @pl.when(i == 0)
        def _():
            dk_ref[...] = jnp.zeros_like(dk_ref)
            dv_ref[...] = jnp.zeros_like(dv_ref)
            dsink_ref[...] = jnp.zeros_like(dsink_ref)

        dk_ref[pl.ds(start, SPAN_B), :] += jnp.concatenate(dks, axis=1)
        dv_ref[pl.ds(start, SPAN_B), :] += jnp.concatenate(dvs, axis=1)
        dsink_ref[...] += dsink

    qs = pl.BlockSpec((BQ_B, WB), lambda i: (i, 0))
    return pl.pallas_call(
        body, name=name, grid=(T // BQ_B,),
        in_specs=[pl.BlockSpec(memory_space=pltpu.SMEM), qs, _whole((T, WKV)), _whole((T, WKV)), qs, qs, qs],
        out_specs=[qs, _whole((T, WKV)), _whole((T, WKV)), _whole((1, LANE))],
        out_shape=[jax.ShapeDtypeStruct((T, WB), F32), jax.ShapeDtypeStruct((T, WKV), F32),
                   jax.ShapeDtypeStruct((T, WKV), F32), jax.ShapeDtypeStruct((1, LANE), F32)],
        compiler_params=_params(("arbitrary",)),
    )(sink, qb, kb, vb, ob, lse, dob)


SPAN_C = NA_ROWS * GRID_W


def _row_start(r):
    return jnp.clip(r - NA_ROWS // 2, 0, ROWS - NA_ROWS)


def _off_index(r):
    return _row_start(r) - r + (NA_ROWS - 1)


N_TAB = 16
RPS_FWD, RPS_BWD = 4, 8


def _rpb_tables(name, rpb):
    circ = jnp.concatenate([rpb[..., NA_COLS - 1:], jnp.zeros(rpb.shape[:2] + (LANE - (2 * NA_COLS - 1),), F32),
                            rpb[..., :NA_COLS - 1]], axis=-1)
    circ = jnp.pad(circ, ((0, 0), (0, N_TAB + 1 - circ.shape[1]), (0, 0)))

    def body(w_ref, o_ref):
        c = lax.broadcasted_iota(jnp.int32, (GRID_W, LANE), 0)
        lane = lax.broadcasted_iota(jnp.int32, (GRID_W, LANE), 1)
        cs = jnp.clip(c - NA_COLS // 2, 0, GRID_W - NA_COLS)
        valid = (lane % GRID_W >= cs) & (lane % GRID_W < cs + NA_COLS)
        toep = [pltpu.roll(jnp.broadcast_to(w_ref[a:a + 1, :], (GRID_W, LANE)), 0, 1, stride=1, stride_axis=0)
                for a in range(N_TAB + 1)]
        for a in range(N_TAB):
            pair = jnp.where(lane < GRID_W, toep[a], pltpu.roll(toep[a + 1], GRID_W, 1))
            o_ref[a] = jnp.where(valid, pair, NEG)

    return pl.pallas_call(
        body, name=name, grid=(HC,),
        in_specs=[pl.BlockSpec((None, N_TAB + 1, LANE), lambda h: (h, 0, 0))],
        out_specs=pl.BlockSpec((None, N_TAB, GRID_W, LANE), lambda h: (h, 0, 0, 0)),
        out_shape=jax.ShapeDtypeStruct((HC, N_TAB, GRID_W, LANE), F32), compiler_params=_params(("parallel",)),
    )(circ)


def _bias_c(t_ref, h, d):
    return jnp.concatenate([t_ref[h, d + k] for k in range(0, NA_ROWS, 2)], axis=1)


def _attn_c_fwd(name, qkv, tables):
    RPS = RPS_FWD

    def body(q_ref, k_ref, v_ref, t_ref, o_ref, lse_ref):
        heads = []
        for rr in range(RPS):
            r = pl.program_id(1) * RPS + rr
            rows = slice(rr * GRID_W, (rr + 1) * GRID_W)
            start = pl.multiple_of(_row_start(r) * GRID_W, GRID_W)
            kw, vw = k_ref[pl.ds(start, SPAN_C), :], v_ref[pl.ds(start, SPAN_C), :]
            heads += [dict(q=q_ref[rows, h * HD:(h + 1) * HD], k=kw[:, h * HD:(h + 1) * HD], v=vw[:, h * HD:(h + 1) * HD],
                           bias=_bias_c(t_ref, h, _off_index(r))) for h in range(2)]
        outs = _heads_fwd(heads)
        for rr in range(RPS):
            rows = slice(rr * GRID_W, (rr + 1) * GRID_W)
            o_ref[rows, :] = jnp.concatenate([o for o, _ in outs[2 * rr:2 * rr + 2]], axis=1)
            lse_ref[rows, :] = _per_head([lse for _, lse in outs[2 * rr:2 * rr + 2]])

    qs = pl.BlockSpec((RPS * GRID_W, LANE), lambda p, r: (r, p))
    ks = pl.BlockSpec((T, LANE), lambda p, r: (0, p))
    ts = pl.BlockSpec((2, N_TAB, GRID_W, LANE), lambda p, r: (p, 0, 0, 0))
    return pl.pallas_call(
        body, name=name, grid=(HC // 2, ROWS // RPS),
        in_specs=[_qkv_rows(RPS * GRID_W, "qc"), _qkv_all("kc"), _qkv_all("vc"), ts], out_specs=[qs, qs],
        out_shape=[jax.ShapeDtypeStruct((T, WC), F32)] * 2, compiler_params=_params(("parallel", "parallel")),
    )(qkv, qkv, qkv, tables)


def _attn_c_bwd(name, qkv, oc, lse, doc, tables):
    RPS = RPS_BWD

    def body(q_ref, k_ref, v_ref, o_ref, lse_ref, do_ref, t_ref, dq_ref, dk_ref, dv_ref, dt_ref):
        @pl.when(pl.program_id(1) == 0)
        def _():
            dk_ref[...] = jnp.zeros_like(dk_ref)
            dv_ref[...] = jnp.zeros_like(dv_ref)
            dt_ref[...] = jnp.zeros_like(dt_ref)

        heads, where = [], []
        for rr in range(RPS):
            r = pl.program_id(1) * RPS + rr
            rows = slice(rr * GRID_W, (rr + 1) * GRID_W)
            d = _off_index(r)
            start = pl.multiple_of(_row_start(r) * GRID_W, GRID_W)
            kw, vw = k_ref[pl.ds(start, SPAN_C), :], v_ref[pl.ds(start, SPAN_C), :]
            where.append((rows, d, start))
            for h in range(2):
                sl = slice(h * HD, (h + 1) * HD)
                heads.append(dict(q=q_ref[rows, sl], k=kw[:, sl], v=vw[:, sl], o=o_ref[rows, sl], do=do_ref[rows, sl],
                                  lse=lse_ref[rows, h * HD:h * HD + 1], bias=_bias_c(t_ref, h, d)))
        res = _heads_bwd(heads)
        for rr, (rows, d, start) in enumerate(where):
            pair = res[2 * rr:2 * rr + 2]
            for h in range(2):
                for k in range(0, NA_ROWS, 2):
                    dt_ref[h, d + k] += pair[h][3][:, k * GRID_W:(k + 2) * GRID_W]
            dq_ref[rows, :] = jnp.concatenate([p[0] for p in pair], axis=1)
            dk_ref[pl.ds(start, SPAN_C), :] += jnp.concatenate([p[1] for p in pair], axis=1)
            dv_ref[pl.ds(start, SPAN_C), :] += jnp.concatenate([p[2] for p in pair], axis=1)

    qs = pl.BlockSpec((RPS * GRID_W, LANE), lambda p, r: (r, p))
    ks = pl.BlockSpec((T, LANE), lambda p, r: (0, p))
    ts = pl.BlockSpec((2, N_TAB, GRID_W, LANE), lambda p, r: (p, 0, 0, 0))
    return pl.pallas_call(
        body, name=name, grid=(HC // 2, ROWS // RPS),
        in_specs=[_qkv_rows(RPS * GRID_W, "qc"), _qkv_all("kc"), _qkv_all("vc"), qs, qs, qs, ts],
        out_specs=[qs, ks, ks, ts],
        out_shape=[jax.ShapeDtypeStruct((T, WC), F32)] * 3 + [jax.ShapeDtypeStruct((HC, N_TAB, GRID_W, LANE), F32)],
        compiler_params=_params(("parallel", "arbitrary")),
    )(qkv, qkv, qkv, oc, lse, doc, tables)


def _split3(v):
    hi = v.astype(BF16)
    r1 = v - hi.astype(F32)
    mid = r1.astype(BF16)
    lo = (r1 - mid.astype(F32)).astype(BF16)
    return hi, mid, lo


def _rpb_reduce(name, dtables):
    x = dtables.reshape(HC, N_TAB, GRID_W * LANE)
    c = jnp.arange(GRID_W)[:, None]
    lane = jnp.arange(LANE)[None, :]
    col = (lane // GRID_W) * LANE + jnp.clip(lane % GRID_W - c + (NA_COLS - 1), 0, 2 * NA_COLS - 2)
    col_onehot = (col.reshape(-1)[:, None] == jnp.arange(2 * LANE)[None, :]).astype(BF16)
    a2 = jnp.arange(N_TAB)[None, :]
    row_onehot = jnp.concatenate([(jnp.arange(16)[:, None] == a2 + u) & (a2 < 2 * NA_ROWS - 2) for u in range(2)],
                                 axis=1).astype(BF16)

    def body(x_ref, e_ref, f_ref, o_ref):
        y = sum(jnp.dot(part, e_ref[...], preferred_element_type=F32) for part in _split3(x_ref[...]))
        z = jnp.concatenate([y[:, :LANE], y[:, LANE:]], axis=0)
        o_ref[...] = sum(jnp.dot(f_ref[...], part, preferred_element_type=F32) for part in _split3(z))

    out = pl.pallas_call(
        body, name=name, grid=(HC,),
        in_specs=[pl.BlockSpec((None, N_TAB, GRID_W * LANE), lambda h: (h, 0, 0)),
                  _whole((GRID_W * LANE, 2 * LANE)), _whole((16, 2 * N_TAB))],
        out_specs=pl.BlockSpec((None, 16, LANE), lambda h: (h, 0, 0)),
        out_shape=jax.ShapeDtypeStruct((HC, 16, LANE), F32), compiler_params=_params(("parallel",)),
    )(x, col_onehot, row_onehot)
    return out[:, :2 * NA_ROWS - 1, :2 * NA_COLS - 1]


TC = 128
CHUNK = 128
MARGIN = 8


def _conv(v, w, b):
    edge = 2 * MARGIN
    taps = lambda prev, cur, nxt: prev * w[0:1] + cur * w[1:2] + nxt * w[2:3] + b
    body = taps(pltpu.roll(v, 1, 0), v, pltpu.roll(v, T - 1, 0))
    row = lax.broadcasted_iota(jnp.int32, (edge, TC), 0)
    top, bot = v[0:edge], v[T - edge:T]
    top_u = taps(jnp.where(row == 0, 0.0, pltpu.roll(top, 1, 0)), top, pltpu.roll(top, edge - 1, 0))
    bot_u = taps(pltpu.roll(bot, 1, 0), bot, jnp.where(row == edge - 1, 0.0, pltpu.roll(bot, edge - 1, 0)))
    return jnp.concatenate([top_u[0:MARGIN], body[MARGIN:T - MARGIN], bot_u[MARGIN:edge]], axis=0)


FWD_BLOCKS = 4
BWD_BLOCKS = 1


def _ffn_mid_fwd(name, up, conv_w, conv_b):
    wide = FWD_BLOCKS * TC

    def body(xg_ref, xv_ref, wg_ref, wv_ref, bg_ref, bv_ref, o_ref):
        for b in range(FWD_BLOCKS):
            lanes = slice(b * TC, (b + 1) * TC)
            ug = _conv(xg_ref[b], wg_ref[:, lanes], bg_ref[:, lanes])
            uv = _conv(xv_ref[b], wv_ref[:, lanes], bv_ref[:, lanes])
            o_ref[:, lanes] = (ug * jax.nn.sigmoid(ug) * uv).astype(BF16)

    gate = lambda shape: pl.BlockSpec(shape, lambda j: (0, j))
    val = lambda shape: pl.BlockSpec(shape, lambda j: (0, j + DFF // wide))
    return pl.pallas_call(
        body, name=name, grid=(DFF // wide,),
        in_specs=[pl.BlockSpec((FWD_BLOCKS, T, TC), lambda j: (j, 0, 0)),
                  pl.BlockSpec((FWD_BLOCKS, T, TC), lambda j: (j + DFF // wide, 0, 0)),
                  gate((3, wide)), val((3, wide)), gate((1, wide)), val((1, wide))],
        out_specs=pl.BlockSpec((T, wide), lambda j: (0, j)),
        out_shape=jax.ShapeDtypeStruct((T, DFF), BF16), compiler_params=_params(("parallel",)),
    )(up, up, conv_w, conv_w, conv_b, conv_b)


def _ffn_mid_bwd(name, dact, up, conv_w, conv_b):
    window = CHUNK + 2 * MARGIN
    centre = slice(MARGIN, MARGIN + CHUNK)

    def shifted(v):
        return pltpu.roll(v, 1, 0), pltpu.roll(v, window - 1, 0)

    def fold(v):
        return jnp.sum(v[centre].reshape(CHUNK // 8, 8, TC), axis=0)

    wide = BWD_BLOCKS * TC

    def body(da_ref, xg_ref, xv_ref, wg_ref, wv_ref, bg_ref, bv_ref, dx_ref, dw_ref, db_ref):
        for b in range(BWD_BLOCKS):
            block(b, da_ref, xg_ref, xv_ref, wg_ref, wv_ref, bg_ref, bv_ref, dx_ref, dw_ref, db_ref)

    def block(b, da_ref, xg_ref, xv_ref, wg_ref, wv_ref, bg_ref, bv_ref, dx_ref, dw_ref, db_ref):
        lanes = slice(b * TC, (b + 1) * TC)
        wg, wv, bg, bv = wg_ref[:, lanes], wv_ref[:, lanes], bg_ref[:, lanes], bv_ref[:, lanes]
        margin = jnp.zeros((MARGIN, TC), F32)
        last = T // CHUNK - 1

        def windows(c):
            if isinstance(c, int) and c == 0:
                rows = slice(0, CHUNK + MARGIN)
                return [jnp.concatenate([margin, v], axis=0)
                        for v in (da_ref[rows, lanes], xg_ref[b, rows, :], xv_ref[b, rows, :])]
            if isinstance(c, int) and c == last:
                rows = slice(T - CHUNK - MARGIN, T)
                return [jnp.concatenate([v, margin], axis=0)
                        for v in (da_ref[rows, lanes], xg_ref[b, rows, :], xv_ref[b, rows, :])]
            rows = pl.ds(pl.multiple_of(c * CHUNK - MARGIN, MARGIN), window)
            return [da_ref[rows, lanes], xg_ref[b, rows, :], xv_ref[b, rows, :]]

        def chunk(c, sums):
            r0 = c * CHUNK if isinstance(c, int) else pl.multiple_of(c * CHUNK, CHUNK)
            da, xg, xv = windows(c)
            xg_prev, xg_next = shifted(xg)
            xv_prev, xv_next = shifted(xv)
            ug = xg_prev * wg[0:1] + xg * wg[1:2] + xg_next * wg[2:3] + bg
            uv = xv_prev * wv[0:1] + xv * wv[1:2] + xv_next * wv[2:3] + bv
            sg = jax.nn.sigmoid(ug)
            dug = da * uv * (sg * (1.0 + ug * (1.0 - sg)))
            duv = da * (ug * sg)
            out = []
            for half, (x_prev, x, x_next, w, du) in enumerate(((xg_prev, xg, xg_next, wg, dug),
                                                               (xv_prev, xv, xv_next, wv, duv))):
                du_prev, du_next = shifted(du)
                dx = du_next * w[0:1] + du * w[1:2] + du_prev * w[2:3]
                dx_ref[half, pl.ds(r0, CHUNK), lanes] = dx[centre].astype(BF16)
                out += [fold(x_prev * du), fold(x * du), fold(x_next * du), fold(du)]
            return tuple(s + o for s, o in zip(sums, out))

        sums = chunk(0, tuple(jnp.zeros((8, TC), F32) for _ in range(8)))
        sums = lax.fori_loop(1, last, chunk, sums)
        sums = chunk(last, sums)
        rows = [jnp.sum(s, axis=0, keepdims=True) for s in sums]
        for half in range(2):
            dw_ref[half, :, lanes] = jnp.concatenate(rows[4 * half:4 * half + 3], axis=0)
            db_ref[half, :, lanes] = rows[4 * half + 3]

    gate = lambda shape: pl.BlockSpec(shape, lambda j: (0, j))
    val = lambda shape: pl.BlockSpec(shape, lambda j: (0, j + DFF // wide))
    return pl.pallas_call(
        body, name=name, grid=(DFF // wide,),
        in_specs=[gate((T, wide)), pl.BlockSpec((BWD_BLOCKS, T, TC), lambda j: (j, 0, 0)),
                  pl.BlockSpec((BWD_BLOCKS, T, TC), lambda j: (j + DFF // wide, 0, 0)),
                  gate((3, wide)), val((3, wide)), gate((1, wide)), val((1, wide))],
        out_specs=[pl.BlockSpec((2, T, wide), lambda j: (0, 0, j)), pl.BlockSpec((2, 3, wide), lambda j: (0, 0, j)),
                   pl.BlockSpec((2, 1, wide), lambda j: (0, 0, j))],
        out_shape=[jax.ShapeDtypeStruct((2, T, DFF), BF16), jax.ShapeDtypeStruct((2, 3, DFF), F32),
                   jax.ShapeDtypeStruct((2, 1, DFF), F32)],
        compiler_params=_params(("parallel",)),
    )(dact, up, up, conv_w, conv_w, conv_b, conv_b)


def _dup_spec(tm, nj):
    per = DFF // nj
    return pl.BlockSpec((None, tm, nj), lambda a, b, j: (j // per, 0 if tm == T else b, j % per))


def _dup_spec_tn(tm, nj):
    per = DFF // nj
    return pl.BlockSpec((None, tm, nj), lambda j, kt, r: (j // per, 0, j % per))


def _adamw_math(w, g, m, v):
    m = ADAM_B1 * m + (1.0 - ADAM_B1) * g
    v = ADAM_B2 * v + (1.0 - ADAM_B2) * (g * g)
    m_hat = m / (1.0 - ADAM_B1 ** ADAM_STEP)
    v_hat = v / (1.0 - ADAM_B2 ** ADAM_STEP)
    delta = -ADAM_LR * (m_hat / (jnp.sqrt(v_hat) + ADAM_EPS) + ADAM_WD * w)
    return delta, m, v


ADAM_BLOCK = 256 * 1408


def _adamw_sharded(name, w, m, v, parts):
    _, r, c = w.shape
    tr = max(t for t in range(16, r + 1, 16) if r % t == 0 and t * c <= ADAM_BLOCK)

    def body(w_ref, m_ref, v_ref, p0_ref, p1_ref, g_ref, d_ref, nm_ref, nv_ref):
        def run(p_ref):
            g = p_ref[0].astype(F32)
            for k in range(1, N_DEV):
                g = g + p_ref[k].astype(F32)
            d, nm, nv = _adamw_math(w_ref[...], g, m_ref[...], v_ref[...])
            g_ref[...] = g
            d_ref[...] = d
            nm_ref[...] = nm
            nv_ref[...] = nv

        @pl.when(pl.program_id(0) == 0)
        def _():
            run(p0_ref)

        @pl.when(pl.program_id(0) == 1)
        def _():
            run(p1_ref)

    ws = pl.BlockSpec((None, tr, c), lambda l, i: (l, i, 0))
    p0 = pl.BlockSpec((N_DEV, tr, c), lambda l, i: (0, jnp.where(l == 0, i, r // tr - 1), 0))
    p1 = pl.BlockSpec((N_DEV, tr, c), lambda l, i: (0, jnp.where(l == 1, i, 0), 0))
    return pl.pallas_call(
        body, name=name, grid=(DEPTH, r // tr), in_specs=[ws, ws, ws, p0, p1], out_specs=[ws] * 4,
        out_shape=[jax.ShapeDtypeStruct(w.shape, F32)] * 4, compiler_params=_params(("arbitrary", "arbitrary")),
    )(w, m, v, *parts)


def _sum_devices(name, parts):
    r = parts.shape[1]

    def body(p_ref, o_ref):
        g = p_ref[0]
        for k in range(1, N_DEV):
            g = g + p_ref[k]
        o_ref[...] = g

    return pl.pallas_call(
        body, name=name, in_specs=[pl.BlockSpec((N_DEV, r, LANE), lambda: (0, 0, 0))],
        out_specs=pl.BlockSpec((r, LANE), lambda: (0, 0)), out_shape=jax.ShapeDtypeStruct((r, LANE), F32),
        compiler_params=_params(),
    )(parts)


def _adamw_small(name, ws, gs, ms, vs):
    n = len(ws)
    shapes = [w.shape for w in ws]
    ws, gs, ms, vs = ([a.reshape(1, -1) if a.ndim == 1 else a for a in arrs] for arrs in (ws, gs, ms, vs))
    specs = [pl.BlockSpec(memory_space=pltpu.VMEM)] * n

    def body(*refs):
        for i in range(n):
            w_ref, g_ref, m_ref, v_ref = (refs[k * n + i] for k in range(4))
            d, nm, nv = _adamw_math(w_ref[...], g_ref[...], m_ref[...], v_ref[...])
            refs[4 * n + i][...] = d
            refs[5 * n + i][...] = nm
            refs[6 * n + i][...] = nv

    outs = pl.pallas_call(
        body, name=name, in_specs=specs * 4, out_specs=specs * 3,
        out_shape=[jax.ShapeDtypeStruct(w.shape, F32) for w in ws] * 3, compiler_params=_params(),
    )(*ws, *gs, *ms, *vs)
    outs = [o.reshape(shapes[i % n]) for i, o in enumerate(outs)]
    return outs[:n], outs[n:2 * n], outs[2 * n:]


def _pack(arrays):
    flat = jnp.concatenate([a.reshape(-1) for a in arrays])
    pad = (-flat.shape[0]) % (8 * LANE)
    return jnp.pad(flat, (0, pad)).reshape(-1, LANE)


def _unpack(buf, shapes):
    flat, out, off = buf.reshape(-1), [], 0
    for s in shapes:
        n = 1
        for d in s:
            n *= d
        out.append(flat[off:off + n].reshape(s))
        off += n
    return out


def _local_step(x, target, small, weights, conv_w_full, hand_over, used):
    cos2, sin2 = _rope_tables()
    bias_a = _dilation_bias()
    tables = [_rpb_tables(f"rpb_tables_{l}", small["rpb_c"][l]) for l in range(DEPTH)]
    saved, carry = [], 0.0
    for l in range(DEPTH):
        g1, g2 = small["ln_attn"][l][None] + carry, small["ln_ffn"][l][None]
        gain, sink, cb = small["mix_gain"][l][None], small["sink_b"][l], small["conv_b"][l][None]
        cw = conv_w_full[l]
        bias = tables[l]
        h1, qkv = _prologue_matmul(f"proj_in_{l}", _rmsnorm_rows, [x, g1], [D, None],
                                   weights("w_in", l, [cos2, sin2, bias_a] + tables if l == 0 else x),
                                   (D, 1024), lambda j: (0, j), 1024, epilogue=_rope_epilogue, extras=(cos2, sin2),
                                   out_dtype=BF16)
        zero = used(f"proj_in_{l}", qkv)
        qb, kb, vb = (qkv[:, BLOCK_OF[n] * LANE:BLOCK_OF[n] * LANE + w] for n, w in (("qb", WB), ("kb", WKV), ("vb", WKV)))
        oa, lse_a = _attn_a_fwd(f"attn_a_{l}", qkv, bias_a)
        ob, lse_b = _attn_b_fwd(f"attn_b_{l}", qb, kb, vb, sink + zero)
        oc, lse_c = _attn_c_fwd(f"attn_c_{l}", qkv, bias)
        mixed, x_mid = _prologue_matmul(f"proj_out_{l}", _mix_rows, [oa, ob, oc, gain + used(f"attn_{l}", oc)],
                                        [WA, WB, WC, None],
                                        weights("w_out", l, oc), (N_DEV, D // N_DEV, 512), lambda j: (0, 0, j), 512,
                                        res=x)
        h2, up = _prologue_matmul(f"ffn_up_{l}", _rmsnorm_rows, [x_mid, g2 + used(f"proj_out_{l}", x_mid)], [D, None],
                                  weights("w_up", l, x_mid), (D, 1024), lambda j: (0, j), 1024, blocked_out=True)
        act = _ffn_mid_fwd(f"ffn_mid_{l}", up, cw, cb + used(f"ffn_up_{l}", up))
        x_out = _nn_rows(f"ffn_down_{l}", act, weights("w_down", l, act), x_mid, 4, 1024, 1024)
        carry = used(f"ffn_down_{l}", x_out)
        saved.append(dict(x=x, h1=h1, qkv=(qkv, qb, kb, vb), o=(oa, ob, oc), lse=(lse_a, lse_b, lse_c), mixed=mixed,
                          x_mid=x_mid, h2=h2, up=up, act=act, g1=g1, g2=g2, gain=gain, sink=sink, cb=cb, cw=cw, bias=bias))
        x = x_out

    loss8, dx, dxb, d_ln_final = _loss_head(x, small["ln_final"][None], target)
    sgrads = [None] * DEPTH
    for l in reversed(range(DEPTH)):
        s = saved[l]
        qkv, qb, kb, vb = s["qkv"]
        oa, ob, oc = s["o"]
        wg_in, wg_out = weights("w_in", l, None), weights("w_out", l, None)
        wg_up, wg_down = weights("w_up", l, None), weights("w_down", l, None)
        g_down = _tn_rows(f"wgrad_down_{l}", s["act"], dxb, wg_down.shape[1], 2, 512)
        zero = hand_over("w_down", l, g_down)
        dact = _nt_rows(f"dgrad_down_{l}", dxb, wg_down, 4, 512)
        dup, d_cw, d_cb = _ffn_mid_bwd(f"ffn_mid_bwd_{l}", dact, s["up"], s["cw"], s["cb"] + zero)
        g_up = _tn_cols(f"wgrad_up_{l}", s["h2"], dup, _dup_spec_tn, 2 * DFF, DFF // 2)
        zero = hand_over("w_up", l, g_up)
        dh2 = _nt_cols(f"dgrad_up_{l}", dup, _dup_spec, wg_up, DFF // 2)
        dx, dxb, d_g2 = _rmsnorm_bwd(f"norm_ffn_bwd_{l}", dh2, s["x_mid"], s["g2"] + zero, dx)
        g_out = _tn_rows(f"wgrad_out_{l}", s["mixed"], dxb, wg_out.shape[1], 2, D)
        zero = hand_over("w_out", l, g_out)
        dmixed = _nt_rows(f"dgrad_out_{l}", dxb, wg_out, 2, T)
        doa, dob, doc, d_gain = _mix_bwd(f"mix_bwd_{l}", dmixed, oa, ob, oc, s["gain"] + zero)
        lse_a, lse_b, lse_c = s["lse"]
        dqa, dka, dva = _attn_a_bwd(f"attn_a_bwd_{l}", qkv, oa, lse_a, doa, bias_a)
        dqb, dkb, dvb, d_sink = _attn_b_bwd(f"attn_b_bwd_{l}", qb, kb, vb, ob, lse_b, dob, s["sink"])
        dqc, dkc, dvc, d_bias = _attn_c_bwd(f"attn_c_bwd_{l}", qkv, oc, lse_c, doc, s["bias"])
        d_rpb = _rpb_reduce(f"rpb_reduce_{l}", d_bias)
        dproj = _rope_bwd(f"rope_bwd_{l}", (dqa, dka, dva, dqb, dkb, dvb, dqc, dkc, dvc), cos2, sin2)
        g_in = _tn_cols(f"wgrad_in_{l}", s["h1"], dproj,
                        lambda tm, tn: pl.BlockSpec((tm, tn), lambda j, kt, r: (0, j)), IN_COLS, 1024)
        zero = hand_over("w_in", l, g_in)
        dh1 = _nt_cols(f"dgrad_in_{l}", dproj, lambda tm, nc: pl.BlockSpec((tm, nc), lambda kt, i, j: (i, j)), wg_in,
                       IN_COLS // 2)
        dx, dxb, d_g1 = _rmsnorm_bwd(f"norm_attn_bwd_{l}", dh1, s["x"], s["g1"] + zero, dx)
        sgrads[l] = dict(ln_attn=d_g1[0], sink_b=d_sink[0, :HB], rpb_c=d_rpb, mix_gain=d_gain[0], ln_ffn=d_g2[0],
                         conv_w=d_cw.transpose(1, 0, 2).reshape(3, 2 * DFF), conv_b=d_cb.reshape(2 * DFF))
    return loss8[0, 0], dx, d_ln_final[0], sgrads


SMALL_NAMES = ("ln_attn", "sink_b", "rpb_c", "mix_gain", "ln_ffn", "conv_b")


def kernel(x, ln_attn, w_in, sink_b, rpb_c, mix_gain, w_out, ln_ffn, w_up, conv_w, conv_b, w_down, ln_final, loss_target, m_ln_attn, m_w_in, m_sink_b, m_rpb_c, m_mix_gain, m_w_out, m_ln_ffn, m_w_up, m_conv_w, m_conv_b, m_w_down, m_ln_final, v_ln_attn, v_w_in, v_sink_b, v_rpb_c, v_mix_gain, v_w_out, v_ln_ffn, v_w_up, v_conv_w, v_conv_b, v_w_down, v_ln_final):
    me = 4 * lax.axis_index("x") + 2 * lax.axis_index("y") + lax.axis_index("c")
    small = dict(ln_attn=ln_attn, sink_b=sink_b, rpb_c=rpb_c, mix_gain=mix_gain, ln_ffn=ln_ffn, conv_b=conv_b,
                 ln_final=ln_final)

    names = ("w_in", "w_out", "w_up", "w_down")
    shards = dict(w_in=w_in, w_out=w_out, w_up=w_up, w_down=w_down)
    order = [(n, l) for l in range(DEPTH) for n in names]
    conv_key = ("conv_w", 0)
    started, arrived, forwarded, gathered = {}, {}, {}, {}

    def side_by_side(k):
        return k[0] in ("w_in", "w_up")

    def slot_of(k):
        return _col_slot(shards[k[0]].shape[2]) if side_by_side(k) else _lead_slot

    def begin(name, ks, zero):
        srcs = [_pack([conv_w]) + zero if k == conv_key else (shards[k[0]][k[1]] + zero).astype(BF16) for k in ks]
        lands = [lax.empty((s.shape[0], N_DEV * s.shape[1]) if side_by_side(k) else (N_DEV,) + s.shape, s.dtype)
                 for k, s in zip(ks, srcs)]
        peers = [ALL_PEERS if k == conv_key else NEAR_PEERS for k in ks]
        send, recv, bufs, tok = _copy_start(name, srcs + lands, _gather_plan(peers, [slot_of(k) for k in ks]),
                                            [len(p) + 1 for p in peers])
        for i, k in enumerate(ks):
            started[k] = (send[i], recv[i], bufs[i], bufs[len(ks) + i], peers[i])
        return tok

    token = begin("gather_start_first", order[:1], 0.0)
    token = begin("gather_start_rest", [conv_key] + order[1:], token[0, 0])

    def arrive(k, after):
        send, recv, src, land, peers = started[k]
        arrived[k] = _copy_wait(f"gather_{k[0]}_{k[1]}_arrived", [src, land], [send], [recv],
                                _gather_plan([peers], [slot_of(k)]), after)

    queue = list(order)

    def advance(after):
        if not queue:
            return 0.0
        k = queue.pop(0)
        arrive(k, after)
        forwarded[k] = _copy_start(f"gather_{k[0]}_{k[1]}_forward", [arrived[k][1]], _forward_plan(slot_of(k)),
                                   [len(OTHER_CHIPS)])
        return forwarded[k][3][0, 0]

    pass_on_behind = ("proj_in_0", "attn_0", "ffn_up_0", "ffn_down_0", "proj_in_1", "attn_1", "ffn_up_1")

    def used(point, result):
        return advance(result) if point in pass_on_behind else 0.0

    def weights(n, l, after):
        k = (n, l)
        if k not in gathered:
            if k not in forwarded:
                advance(after)
            send_b, recv_b, (land,), _ = forwarded[k]
            (gathered[k],) = _copy_wait(f"gather_{n}_{l}_done", [land], send_b, recv_b, _forward_plan(slot_of(k)),
                                        after)
        return gathered[k]

    pending = {}

    def hand_over(n, l, g):
        shard = shards[n].shape[1:]
        send, recv, bufs, tok = _copy_start(f"send_grad_{n}_{l}", [g, lax.empty((N_DEV,) + shard, g.dtype)],
                                            _scatter_plan(slot_of((n, l))), [len(ALL_PEERS) + 1])
        pending[(n, l)] = (send, recv, bufs)
        return tok[0, 0]

    def received(k, after):
        send, recv, bufs = pending[k]
        return _copy_wait(f"recv_grad_{k[0]}_{k[1]}", bufs, send, recv, _scatter_plan(slot_of(k)), after)[1]

    arrive(conv_key, token)
    cw_all = arrived[conv_key][1]
    nup = w_up.shape[2]
    cw_shards = cw_all.reshape(N_DEV, -1)[:, :DEPTH * 3 * nup].reshape(N_DEV, DEPTH, 3, nup)
    conv_w_full = cw_shards.transpose(1, 2, 0, 3).reshape(DEPTH, 3, N_DEV * nup)

    loss_local, dx, d_ln_final, sgrads = _local_step(
        x[0], loss_target[0], dict(small, ln_attn=ln_attn + token[0, 0]), weights, conv_w_full, hand_over, used)

    stacked = [jnp.stack([sgrads[l][n] for l in range(DEPTH)]) for n in SMALL_NAMES + ("conv_w",)] + [d_ln_final]
    shapes = [a.shape for a in stacked]
    mine = _pack(stacked)
    send_s, recv_s, bufs_s, _ = _copy_start("gather_small_grads_start", [mine, lax.empty((N_DEV,) + mine.shape, F32)],
                                            _gather_plan([ALL_PEERS], [_lead_slot]), [len(ALL_PEERS) + 1])

    big, after = {}, dx
    moments = dict(w_in=(m_w_in, v_w_in), w_out=(m_w_out, v_w_out), w_up=(m_w_up, v_w_up), w_down=(m_w_down, v_w_down))
    for n in reversed(names):
        parts = (received((n, 0), after), received((n, 1), after))
        big[n] = _adamw_sharded(f"adamw_{n}", shards[n], *moments[n], parts)
        after = big[n][1]

    _, everyone = _copy_wait("gather_small_grads_done", bufs_s, send_s, recv_s,
                             _gather_plan([ALL_PEERS], [_lead_slot]), after)
    g_small = _unpack(_sum_devices("sum_small_grads", everyone), shapes)
    g = dict(zip(SMALL_NAMES + ("conv_w", "ln_final"), g_small))
    g["conv_w"] = lax.dynamic_slice_in_dim(g["conv_w"], me * nup, nup, axis=2)

    snames = SMALL_NAMES + ("conv_w", "ln_final")
    sw = dict(small, conv_w=conv_w)
    sm = dict(ln_attn=m_ln_attn, sink_b=m_sink_b, rpb_c=m_rpb_c, mix_gain=m_mix_gain, ln_ffn=m_ln_ffn,
              conv_b=m_conv_b, conv_w=m_conv_w, ln_final=m_ln_final)
    sv = dict(ln_attn=v_ln_attn, sink_b=v_sink_b, rpb_c=v_rpb_c, mix_gain=v_mix_gain, ln_ffn=v_ln_ffn,
              conv_b=v_conv_b, conv_w=v_conv_w, ln_final=v_ln_final)
    s_delta, s_m, s_v = (dict(zip(snames, out)) for out in _adamw_small(
        "adamw_small", [sw[n] for n in snames], [g[n] for n in snames], [sm[n] for n in snames],
        [sv[n] for n in snames]))

    loss = lax.psum(loss_local, ("x", "y", "c"))
    outputs = ("ln_attn", "w_in", "sink_b", "rpb_c", "mix_gain", "w_out", "ln_ffn", "w_up", "conv_w", "conv_b",
               "w_down", "ln_final")
    grads = [big[n][0] if n in big else g[n] for n in outputs]
    deltas = [big[n][1] if n in big else s_delta[n] for n in outputs]
    new_m = [big[n][2] if n in big else s_m[n] for n in outputs]
    new_v = [big[n][3] if n in big else s_v[n] for n in outputs]
    return (loss, dx[None], *grads, *deltas, *new_m, *new_v)
```

```python
import functools

import jax
import jax.numpy as jnp
from jax import lax
from jax.experimental import pallas as pl
from jax.experimental.pallas import tpu as pltpu

F32 = jnp.float32
BF16 = jnp.bfloat16

N_DEV = 8
T = 2048
D = 2048
DEPTH = 2
HD = 64
HA, HB, HKV, HC = 12, 10, 2, 10
WA, WB, WKV, WC = HA * HD, HB * HD, HKV * HD, HC * HD
IN_COLS = 3 * WA + WB + 2 * WKV + 3 * WC
DFF = 5632
GRID_W = 64
ROWS = T // GRID_W
NA_ROWS, NA_COLS = 8, 16
WINDOW_B = 128
EPS = 1e-6
NEG = -1e30
ROPE_THETA = 10000.0
LANE = 128
VMEM_LIMIT = 56 * 1024 * 1024

ADAM_LR, ADAM_B1, ADAM_B2, ADAM_EPS, ADAM_WD, ADAM_STEP = 0.001, 0.9, 0.999, 1e-08, 0.01, 10

GROUPS = (("qa", WA, True, True), ("ka", WA, True, False), ("va", WA, False, False),
          ("qb", WB, True, True), ("kb", WKV, True, False), ("vb", WKV, False, False),
          ("qc", WC, False, True), ("kc", WC, False, False), ("vc", WC, False, False))


def _params(sem=None):
    return pltpu.CompilerParams(dimension_semantics=sem, vmem_limit_bytes=VMEM_LIMIT)


HBM_SPEC = pl.BlockSpec(memory_space=pltpu.HBM)
SEM_SPEC = pl.BlockSpec(memory_space=pltpu.SEMAPHORE)
DATAFLOW = pltpu.SideEffectType.DATAFLOW_SIDE_EFFECTING


ALL_PEERS = tuple((p >> 2 & 1, p >> 1 & 1, p & 1) for p in range(1, N_DEV))
OTHER_CHIPS = ((1, 0, 0), (0, 1, 0), (1, 1, 0))
NEAR_PEERS = ((0, 0, 1),) + OTHER_CHIPS


def _flip(x, y, c, f):
    return (1 - x if f[0] else x, 1 - y if f[1] else y, 1 - c if f[2] else c)


def _index(pos):
    return 4 * pos[0] + 2 * pos[1] + pos[2]


class _LocalCopy:
    def __init__(self, src, dst, sem):
        self.copy = pltpu.make_async_copy(src, dst, sem)

    def start(self):
        self.copy.start()

    def wait_send(self):
        self.copy.wait()

    def wait_recv(self):
        pass


def _descriptors(plan, bufs, send_sems, recv_sems):
    x, y, c = lax.axis_index("x"), lax.axis_index("y"), lax.axis_index("c")
    return [_LocalCopy(src, dst, send_sems[g].at[i]) if partner is None else
            pltpu.make_async_remote_copy(src_ref=src, dst_ref=dst, send_sem=send_sems[g].at[i],
                                         recv_sem=recv_sems[g].at[i], device_id=partner,
                                         device_id_type=pl.DeviceIdType.MESH)
            for g, copies in enumerate(plan(bufs, x, y, c)) for i, (src, dst, partner) in enumerate(copies)]


def _copy_start(name, bufs, plan, sizes):
    nb, ng = len(bufs), len(sizes)

    def body(*refs):
        for d in _descriptors(plan, refs[:nb], refs[nb:nb + ng], refs[nb + ng:nb + 2 * ng]):
            d.start()
        refs[2 * nb + 2 * ng][...] = jnp.zeros((8, LANE), F32)

    outs = pl.pallas_call(
        body, name=name,
        out_shape=[pltpu.SemaphoreType.DMA((s,)) for s in sizes] * 2 + [pltpu.HBM(b.shape, b.dtype) for b in bufs]
        + [jax.ShapeDtypeStruct((8, LANE), F32)],
        in_specs=[HBM_SPEC] * nb,
        out_specs=[SEM_SPEC] * (2 * ng) + [HBM_SPEC] * nb + [pl.BlockSpec(memory_space=pltpu.VMEM)],
        input_output_aliases={i: 2 * ng + i for i in range(nb)},
        compiler_params=pltpu.CompilerParams(has_side_effects=DATAFLOW),
    )(*[pltpu.with_memory_space_constraint(b, pltpu.HBM) for b in bufs])
    return outs[:ng], outs[ng:2 * ng], outs[2 * ng:2 * ng + nb], outs[2 * ng + nb]


def _copy_wait(name, bufs, send_sems, recv_sems, plan, after):
    nb, ng = len(bufs), len(send_sems)
    after = list(after) if isinstance(after, (list, tuple)) else [after]

    def body(*refs):
        for d in _descriptors(plan, refs[:nb], refs[nb:nb + ng], refs[nb + ng:nb + 2 * ng]):
            d.wait_send()
            d.wait_recv()

    return pl.pallas_call(
        body, name=name, out_shape=[pltpu.HBM(b.shape, b.dtype) for b in bufs],
        in_specs=[HBM_SPEC] * nb + [SEM_SPEC] * (2 * ng) + [pl.BlockSpec(memory_space=pl.ANY)] * len(after),
        out_specs=[HBM_SPEC] * nb, input_output_aliases={i: i for i in range(nb)},
        compiler_params=pltpu.CompilerParams(has_side_effects=DATAFLOW),
    )(*bufs, *send_sems, *recv_sems, *after)


def _lead_slot(ref, k):
    return ref.at[k]


def _col_slot(width):
    return lambda ref, k: ref.at[:, pl.ds(pl.multiple_of(k * width, LANE), width)]


def _gather_plan(peer_sets, slots):
    def plan(bufs, x, y, c):
        n = len(peer_sets)
        return [[(bufs[i], slots[i](bufs[n + i], _index((x, y, c))), _flip(x, y, c, f)) for f in peers]
                + [(bufs[i], slots[i](bufs[n + i], _index((x, y, c))), None)] for i, peers in enumerate(peer_sets)]
    return plan


def _forward_plan(slot):
    def plan(bufs, x, y, c):
        pieces = [slot(bufs[0], _index(_flip(x, y, c, f))) for f in OTHER_CHIPS]
        return [[(p, p, _flip(x, y, c, (0, 0, 1))) for p in pieces]]
    return plan


def _scatter_plan(slot):
    def plan(bufs, x, y, c):
        me = _index((x, y, c))
        peers = [_flip(x, y, c, f) for f in ALL_PEERS]
        return [[(slot(bufs[0], _index(p)), bufs[1].at[me], p) for p in peers]
                + [(slot(bufs[0], me), bufs[1].at[me], None)]]
    return plan


def _flat2(v):
    return v.reshape(-1, v.shape[-1])


def _matmul(name, kind, a, a_spec, b, b_spec, out_shape, out_spec, grid, res=None, res_spec=None, acc_shape=None):
    dims = {"nn": (((1,), (0,)), ((), ())), "nt": NT_DIMS, "tn": TN_DIMS}[kind]
    nred = grid[-1]

    def body(*refs):
        if res is None:
            a_ref, b_ref, o_ref = refs[:3]
            r_ref = None
        else:
            a_ref, b_ref, r_ref, o_ref = refs[:4]
        part = lax.dot_general(_flat2(a_ref[...]), _flat2(b_ref[...]), dims, preferred_element_type=F32)

        def finish(total):
            if r_ref is not None:
                total = total + r_ref[...]
            o_ref[...] = total.reshape(o_ref.shape).astype(o_ref.dtype)

        if nred == 1:
            finish(part)
        else:
            acc_ref = refs[-1]
            k = pl.program_id(len(grid) - 1)

            @pl.when(k == 0)
            def _():
                acc_ref[...] = part

            @pl.when(jnp.logical_and(k > 0, k < nred - 1))
            def _():
                acc_ref[...] += part

            @pl.when(k == nred - 1)
            def _():
                finish(acc_ref[...] + part)

    ins, specs = [a, b], [a_spec, b_spec]
    if res is not None:
        ins.append(res)
        specs.append(res_spec)
    scratch = [] if nred == 1 else [pltpu.VMEM(acc_shape, F32)]
    return pl.pallas_call(
        body, name=name, grid=grid, in_specs=specs, out_specs=out_spec, out_shape=out_shape, scratch_shapes=scratch,
        compiler_params=_params(("parallel",) * (len(grid) - 1) + ("arbitrary",)),
    )(*ins)


def _nn_rows(name, a, wg, res, s, tn, tm):
    _, kj, n = wg.shape
    return _matmul(
        name, "nn", a, pl.BlockSpec((tm, s * kj), lambda j, i, r: (i, r)),
        wg, pl.BlockSpec((s, kj, tn), lambda j, i, r: (r, 0, j)),
        jax.ShapeDtypeStruct((T, n), F32), pl.BlockSpec((tm, tn), lambda j, i, r: (i, j)),
        (n // tn, T // tm, N_DEV // s), res=res, res_spec=pl.BlockSpec((tm, tn), lambda j, i, r: (i, j)),
        acc_shape=(tm, tn))


def _nt_cols(name, dc, dc_spec_of, w, nc):
    k, n = w.shape
    tm = tk = 1024
    return _matmul(
        name, "nt", dc, dc_spec_of(tm, nc),
        w, pl.BlockSpec((tk, nc), lambda kt, i, j: (kt, j)),
        jax.ShapeDtypeStruct((T, k), F32), pl.BlockSpec((tm, tk), lambda kt, i, j: (i, kt)),
        (k // tk, T // tm, n // nc), acc_shape=(tm, tk))


def _nt_rows(name, dc, wg, s, tm):
    _, kj, n = wg.shape
    return _matmul(
        name, "nt", dc, pl.BlockSpec((tm, n), lambda kt, i, r: (i, 0)),
        wg, pl.BlockSpec((s, kj, n), lambda kt, i, r: (kt, 0, 0)),
        jax.ShapeDtypeStruct((T, N_DEV * kj), F32), pl.BlockSpec((tm, s * kj), lambda kt, i, r: (i, kt)),
        (N_DEV // s, T // tm, 1))


def _tn_cols(name, a, dc, dc_spec_of, n, tn):
    k = a.shape[1]
    tk = 1024
    return _matmul(
        name, "tn", a, pl.BlockSpec((T, tk), lambda j, kt, r: (0, kt)),
        dc, dc_spec_of(T, tn),
        jax.ShapeDtypeStruct((k, n), BF16), pl.BlockSpec((tk, tn), lambda j, kt, r: (kt, j)),
        (n // tn, k // tk, 1))


def _tn_rows(name, a, dc, kj, s, tn):
    n = dc.shape[1]
    return _matmul(
        name, "tn", a, pl.BlockSpec((T, s * kj), lambda kt, j, r: (0, kt)),
        dc, pl.BlockSpec((T, tn), lambda kt, j, r: (0, j)),
        jax.ShapeDtypeStruct((N_DEV, kj, n), BF16), pl.BlockSpec((s, kj, tn), lambda kt, j, r: (kt, 0, j)),
        (N_DEV // s, n // tn, 1))


TR = 512


def _rows(width):
    return pl.BlockSpec((TR, width), lambda i: (i, 0))


def _whole(shape):
    return pl.BlockSpec(shape, lambda i: (0,) * len(shape))


def _rmsnorm_rows(x_ref, g_ref):
    xv = x_ref[...]
    r = lax.rsqrt(jnp.mean(xv * xv, axis=-1, keepdims=True) + EPS)
    return ((xv * r) * g_ref[...]).astype(BF16)


SUB = 256


def _prologue_matmul(name, prologue, ins, widths, w, w_block, w_index, tn, res=None, epilogue=None, extras=(),
                     out_dtype=F32, blocked_out=False):
    tm = 1024
    n = w.shape[-1]
    ni = len(ins)

    def body(*refs):
        w_ref = refs[ni]
        r_ref = refs[ni + 1] if res is not None else None
        x_refs = refs[ni + 1 + (res is not None):len(refs) - 3]
        h_ref, o_ref, h_scr = refs[-3:]

        @pl.when(pl.program_id(1) == 0)
        def _():
            h = prologue(*refs[:ni])
            h_scr[...] = h
            h_ref[...] = h

        for sub in range(tn // SUB):
            cols = slice(sub * SUB, (sub + 1) * SUB)
            w_cols = w_ref[(slice(None),) * (len(w_ref.shape) - 1) + (cols,)]
            part = jnp.dot(h_scr[...], _flat2(w_cols), preferred_element_type=F32)
            if r_ref is not None:
                part = part + r_ref[:, cols]
            if epilogue is not None:
                part = epilogue(pl.program_id(1) * (tn // SUB) + sub, part, *x_refs)
            if blocked_out:
                for b in range(SUB // LANE):
                    o_ref[sub * (SUB // LANE) + b] = part[:, b * LANE:(b + 1) * LANE].astype(out_dtype)
            else:
                o_ref[:, cols] = part.astype(out_dtype)

    tile = pl.BlockSpec((tm, tn), lambda i, j: (i, j))
    out_tile = pl.BlockSpec((tn // LANE, tm, LANE), lambda i, j: (j, i, 0)) if blocked_out else tile
    out_full = (n // LANE, T, LANE) if blocked_out else (T, n)
    specs = [pl.BlockSpec((1, D), lambda i, j: (0, 0)) if wd is None else pl.BlockSpec((tm, wd), lambda i, j: (i, 0))
             for wd in widths]
    specs.append(pl.BlockSpec(w_block, lambda i, j: w_index(j)))
    operands = list(ins) + [w]
    if res is not None:
        specs.append(tile)
        operands.append(res)
    specs += [pl.BlockSpec((tm, LANE), lambda i, j: (i, 0))] * len(extras)
    operands += list(extras)
    return pl.pallas_call(
        body, name=name, grid=(T // tm, n // tn), in_specs=specs,
        out_specs=[pl.BlockSpec((tm, D), lambda i, j: (i, 0)), out_tile],
        out_shape=[jax.ShapeDtypeStruct((T, D), BF16), jax.ShapeDtypeStruct(out_full, out_dtype)],
        scratch_shapes=[pltpu.VMEM((tm, D), BF16)], compiler_params=_params(("parallel", "arbitrary")),
    )(*operands)


def _rms_bwd_math(dy, xv, g):
    r = lax.rsqrt(jnp.mean(xv * xv, axis=-1, keepdims=True) + EPS)
    xhat = xv * r
    dxhat = dy * g
    dx = r * (dxhat - xhat * jnp.mean(dxhat * xhat, axis=-1, keepdims=True))
    return dx, dy * xhat


def _accumulate(ref, val):
    @pl.when(pl.program_id(0) == 0)
    def _():
        ref[...] = val

    @pl.when(pl.program_id(0) > 0)
    def _():
        ref[...] += val


def _rmsnorm_bwd(name, dy, x, g, res):
    def body(dy_ref, x_ref, g_ref, res_ref, dx_ref, dxb_ref, dg_ref):
        dx, dgr = _rms_bwd_math(dy_ref[...], x_ref[...], g_ref[...])
        tot = res_ref[...] + dx
        dx_ref[...] = tot
        dxb_ref[...] = tot.astype(BF16)
        _accumulate(dg_ref, jnp.sum(dgr, axis=0, keepdims=True))

    return pl.pallas_call(
        body, name=name, grid=(T // TR,), in_specs=[_rows(D), _rows(D), _whole((1, D)), _rows(D)],
        out_specs=[_rows(D), _rows(D), _whole((1, D))],
        out_shape=[jax.ShapeDtypeStruct((T, D), F32), jax.ShapeDtypeStruct((T, D), BF16),
                   jax.ShapeDtypeStruct((1, D), F32)],
        compiler_params=_params(("arbitrary",)),
    )(dy, x, g, res)


def _loss_head(x, g, target):
    def body(x_ref, g_ref, t_ref, loss_ref, dx_ref, dxb_ref, dg_ref):
        xv, gv = x_ref[...], g_ref[...]
        r = lax.rsqrt(jnp.mean(xv * xv, axis=-1, keepdims=True) + EPS)
        err = (xv * r) * gv - t_ref[...]
        part = 0.5 * jnp.sum(jnp.mean(err * err, axis=-1, keepdims=True))
        dx, dgr = _rms_bwd_math(err * (1.0 / D), xv, gv)
        dx_ref[...] = dx
        dxb_ref[...] = dx.astype(BF16)
        _accumulate(dg_ref, jnp.sum(dgr, axis=0, keepdims=True))
        _accumulate(loss_ref, jnp.full((8, LANE), part, F32))

    return pl.pallas_call(
        body, name="loss_head", grid=(T // TR,), in_specs=[_rows(D), _whole((1, D)), _rows(D)],
        out_specs=[_whole((8, LANE)), _rows(D), _rows(D), _whole((1, D))],
        out_shape=[jax.ShapeDtypeStruct((8, LANE), F32), jax.ShapeDtypeStruct((T, D), F32),
                   jax.ShapeDtypeStruct((T, D), BF16), jax.ShapeDtypeStruct((1, D), F32)],
        compiler_params=_params(("arbitrary",)),
    )(x, g, target)


MIX_OFFS = ((0, WA), (WA, WB), (WA + WB, WC))


def _mix_rows(oa_ref, ob_ref, oc_ref, g_ref):
    parts = []
    for ref, (off, w) in zip((oa_ref, ob_ref, oc_ref), MIX_OFFS):
        o = ref[...]
        r = lax.rsqrt(jnp.mean(o * o, axis=-1, keepdims=True) + EPS)
        parts.append(((o * r) * g_ref[:, off:off + w]).astype(BF16))
    return jnp.concatenate(parts, axis=1)


def _mix_bwd(name, dmixed, oa, ob, oc, gain):
    def body(dm_ref, oa_ref, ob_ref, oc_ref, g_ref, doa_ref, dob_ref, doc_ref, dg_ref):
        dgs = []
        for ref, dref, (off, w) in zip((oa_ref, ob_ref, oc_ref), (doa_ref, dob_ref, doc_ref), MIX_OFFS):
            dx, dgr = _rms_bwd_math(dm_ref[:, off:off + w], ref[...], g_ref[:, off:off + w])
            dref[...] = dx
            dgs.append(jnp.sum(dgr, axis=0, keepdims=True))
        _accumulate(dg_ref, jnp.concatenate(dgs, axis=1))

    return pl.pallas_call(
        body, name=name, grid=(T // TR,),
        in_specs=[_rows(D), _rows(WA), _rows(WB), _rows(WC), _whole((1, D))],
        out_specs=[_rows(WA), _rows(WB), _rows(WC), _whole((1, D))],
        out_shape=[jax.ShapeDtypeStruct((T, WA), F32), jax.ShapeDtypeStruct((T, WB), F32),
                   jax.ShapeDtypeStruct((T, WC), F32), jax.ShapeDtypeStruct((1, D), F32)],
        compiler_params=_params(("arbitrary",)),
    )(dmixed, oa, ob, oc, gain)


def _rope_tables():
    inv_freq = ROPE_THETA ** (-jnp.arange(0, HD, 2, dtype=F32) / HD)
    ang = jnp.arange(T, dtype=F32)[:, None] * inv_freq[None, :]
    cos, sin = jnp.cos(ang), jnp.sin(ang)
    cos2 = jnp.tile(jnp.concatenate([cos, cos], axis=1), (1, LANE // HD))
    sin2 = jnp.tile(jnp.concatenate([-sin, sin], axis=1), (1, LANE // HD))
    return cos2, sin2


def _rot_half(v):
    lane = lax.broadcasted_iota(jnp.int32, v.shape, 1)
    return jnp.where(lane % HD < HD // 2, pltpu.roll(v, LANE - HD // 2, 1), pltpu.roll(v, HD // 2, 1))


BLOCK_KINDS = tuple((rot, is_q) for _, w, rot, is_q in GROUPS for _ in range(w // LANE))
BLOCK_OF = {name: sum(w for _, w, _, _ in GROUPS[:g]) // LANE for g, (name, _, _, _) in enumerate(GROUPS)}


def _any_tile(j, tiles):
    return functools.reduce(jnp.logical_or, [j == t for t in tiles]) if tiles else False


def _rope_epilogue(j, tile, c_ref, s_ref):
    cv, sv = c_ref[...], s_ref[...]
    per, n_tiles = tile.shape[1] // LANE, IN_COLS // tile.shape[1]
    out = []
    for b in range(per):
        v = tile[:, b * LANE:(b + 1) * LANE]
        rot = _any_tile(j, [t for t in range(n_tiles) if BLOCK_KINDS[t * per + b][0]])
        is_q = _any_tile(j, [t for t in range(n_tiles) if BLOCK_KINDS[t * per + b][1]])
        if rot is not False:
            v = jnp.where(rot, v * cv + _rot_half(v) * sv, v)
        if is_q is not False:
            v = v * jnp.where(is_q, HD ** -0.5, 1.0)
        out.append(v)
    return jnp.concatenate(out, axis=1)


def _rope_bwd(name, grads, cos2, sin2):
    def body(*refs):
        ins, (c_ref, s_ref, o_ref) = refs[:9], refs[9:]
        cv, sv = c_ref[...], s_ref[...]
        off = 0
        for d_ref, (_, w, rot, is_q) in zip(ins, GROUPS):
            for b in range(w // LANE):
                v = d_ref[:, b * LANE:(b + 1) * LANE]
                if is_q:
                    v = v * (HD ** -0.5)
                if rot:
                    v = v * cv + _rot_half(v * sv)
                o_ref[:, off + b * LANE:off + (b + 1) * LANE] = v.astype(BF16)
            off += w

    return pl.pallas_call(
        body, name=name, grid=(T // TR,), in_specs=[_rows(w) for _, w, _, _ in GROUPS] + [_rows(LANE), _rows(LANE)],
        out_specs=_rows(IN_COLS), out_shape=jax.ShapeDtypeStruct((T, IN_COLS), BF16),
        compiler_params=_params(("parallel",)),
    )(*grads, cos2, sin2)


NT_DIMS = (((1,), (1,)), ((), ()))
TN_DIMS = (((0,), (0,)), ((), ()))


def _scores(q, k, bias, valid):
    s = lax.dot_general(q, k, NT_DIMS, preferred_element_type=F32)
    if bias is not None:
        s = s + bias
    if valid is not None:
        s = jnp.where(valid, s, NEG)
    return s


def _heads_fwd(heads):
    scores = [_scores(h["q"], h["k"], h.get("bias"), h.get("valid")) for h in heads]
    soft = []
    for s, h in zip(scores, heads):
        m = jnp.max(s, axis=1, keepdims=True)
        e = jnp.exp(s - m)
        l = jnp.sum(e, axis=1, keepdims=True)
        if h.get("sink") is not None:
            l = l + jnp.exp(h["sink"] - m)
        soft.append((e.astype(BF16), l, m + jnp.log(l)))
    return [(jnp.dot(e, h["v"], preferred_element_type=F32) / l, lse) for (e, l, lse), h in zip(soft, heads)]


def _heads_bwd(heads):
    dobs = [h["do"].astype(BF16) for h in heads]
    scores = [_scores(h["q"], h["k"], h.get("bias"), h.get("valid")) for h in heads]
    dps = [lax.dot_general(dob, h["v"], NT_DIMS, preferred_element_type=F32) for dob, h in zip(dobs, heads)]
    mid = []
    for s, dp, h in zip(scores, dps, heads):
        p = jnp.exp(s - h["lse"])
        delta = jnp.sum(h["do"] * h["o"], axis=1, keepdims=True)
        ds = p * (dp - delta)
        dsink = None if h.get("sink") is None else -jnp.exp(h["sink"] - h["lse"]) * delta
        mid.append((p.astype(BF16), ds, dsink))
    out = []
    for (pb, ds, dsink), dob, h in zip(mid, dobs, heads):
        dsb = ds.astype(BF16)
        out.append((jnp.dot(dsb, h["k"], preferred_element_type=F32),
                    lax.dot_general(dsb, h["q"], TN_DIMS, preferred_element_type=F32),
                    lax.dot_general(pb, dob, TN_DIMS, preferred_element_type=F32), ds, dsink))
    return out


def _per_head(cols):
    return jnp.concatenate([jnp.broadcast_to(c, (c.shape[0], HD)) for c in cols], axis=1)


DILATIONS = ((128, 1), (512, 4), (2048, 16))


BQ_A = 256
REACH_A = max(window // 2 for window, _ in DILATIONS)


def _first_key(i):
    return jnp.maximum(i * BQ_A - REACH_A, 0)


def _key_window_groups():
    groups = {}
    for i in range(T // BQ_A):
        width = min(T, (i + 1) * BQ_A + REACH_A) - max(i * BQ_A - REACH_A, 0)
        groups.setdefault(width, []).append(i)
    return groups


def _per_window(i, fn):
    for width, tiles in _key_window_groups().items():
        hit = functools.reduce(jnp.logical_or, [i == t for t in tiles])
        pl.when(hit)(functools.partial(fn, pl.multiple_of(_first_key(i), BQ_A), width))


def _dilation_bias():
    def body(o_ref):
        i = pl.program_id(0)
        t = i * BQ_A + lax.broadcasted_iota(jnp.int32, (BQ_A, T), 0)
        ad = jnp.abs(t - (_first_key(i) + lax.broadcasted_iota(jnp.int32, (BQ_A, T), 1)))
        count = jnp.zeros((BQ_A, T), jnp.int32)
        for window, r in DILATIONS:
            count += jnp.where(((ad & (r - 1)) == 0) & (ad <= window // 2), 1, 0)
        logs = jnp.where(count == 2, jnp.log(2.0), jnp.where(count == 3, jnp.log(3.0), 0.0)).astype(F32)
        o_ref[...] = jnp.where(count == 0, NEG, logs)

    return pl.pallas_call(
        body, name="dilation_bias", grid=(T // BQ_A,), out_specs=pl.BlockSpec((BQ_A, T), lambda i: (i, 0)),
        out_shape=jax.ShapeDtypeStruct((T, T), F32), compiler_params=_params(("parallel",)),
    )()


def _qkv_rows(rows, group):
    return pl.BlockSpec((rows, LANE), lambda p, i: (i, BLOCK_OF[group] + p))


def _qkv_all(group):
    return pl.BlockSpec((T, LANE), lambda p, i: (0, BLOCK_OF[group] + p))


def _attn_a_fwd(name, qkv, bias):
    def body(q_ref, k_ref, v_ref, b_ref, o_ref, lse_ref):
        def tile(first, width):
            b = b_ref[:, :width]
            outs = _heads_fwd([dict(q=q_ref[:, h * HD:(h + 1) * HD], k=k_ref[pl.ds(first, width), h * HD:(h + 1) * HD],
                                    v=v_ref[pl.ds(first, width), h * HD:(h + 1) * HD], bias=b) for h in range(2)])
            o_ref[...] = jnp.concatenate([o for o, _ in outs], axis=1)
            lse_ref[...] = _per_head([lse for _, lse in outs])

        _per_window(pl.program_id(1), tile)

    qs = pl.BlockSpec((BQ_A, LANE), lambda p, i: (i, p))
    ks = pl.BlockSpec((T, LANE), lambda p, i: (0, p))
    return pl.pallas_call(
        body, name=name, grid=(HA // 2, T // BQ_A),
        in_specs=[_qkv_rows(BQ_A, "qa"), _qkv_all("ka"), _qkv_all("va"), pl.BlockSpec((BQ_A, T), lambda p, i: (i, 0))],
        out_specs=[qs, qs],
        out_shape=[jax.ShapeDtypeStruct((T, WA), F32)] * 2, compiler_params=_params(("parallel", "parallel")),
    )(qkv, qkv, qkv, bias)


def _attn_a_bwd(name, qkv, oa, lse, doa, bias):
    def body(q_ref, k_ref, v_ref, o_ref, lse_ref, do_ref, b_ref, dq_ref, dk_ref, dv_ref):
        @pl.when(pl.program_id(1) == 0)
        def _():
            dk_ref[...] = jnp.zeros_like(dk_ref)
            dv_ref[...] = jnp.zeros_like(dv_ref)

        def tile(first, width):
            b = b_ref[:, :width]
            keys = pl.ds(first, width)
            sls = [slice(h * HD, (h + 1) * HD) for h in range(2)]
            res = _heads_bwd([dict(q=q_ref[:, sl], k=k_ref[keys, sl], v=v_ref[keys, sl], o=o_ref[:, sl],
                                   do=do_ref[:, sl], lse=lse_ref[:, sl.start:sl.start + 1], bias=b) for sl in sls])
            dq_ref[...] = jnp.concatenate([r[0] for r in res], axis=1)
            dk_ref[keys, :] += jnp.concatenate([r[1] for r in res], axis=1)
            dv_ref[keys, :] += jnp.concatenate([r[2] for r in res], axis=1)

        _per_window(pl.program_id(1), tile)

    qs = pl.BlockSpec((BQ_A, LANE), lambda p, i: (i, p))
    ks = pl.BlockSpec((T, LANE), lambda p, i: (0, p))
    return pl.pallas_call(
        body, name=name, grid=(HA // 2, T // BQ_A),
        in_specs=[_qkv_rows(BQ_A, "qa"), _qkv_all("ka"), _qkv_all("va"), qs, qs, qs,
                  pl.BlockSpec((BQ_A, T), lambda p, i: (i, 0))], out_specs=[qs, ks, ks],
        out_shape=[jax.ShapeDtypeStruct((T, WA), F32)] * 3, compiler_params=_params(("parallel", "arbitrary")),
    )(qkv, qkv, qkv, oa, lse, doa, bias)


BQ_B = 128
SPAN_B = BQ_B + 2 * WINDOW_B


def _window_b(i):
    start = pl.multiple_of(jnp.clip(i * BQ_B - WINDOW_B, 0, T - SPAN_B), BQ_B)
    qpos = i * BQ_B + lax.broadcasted_iota(jnp.int32, (BQ_B, SPAN_B), 0)
    kpos = start + lax.broadcasted_iota(jnp.int32, (BQ_B, SPAN_B), 1)
    return start, jnp.abs(qpos - kpos) <= WINDOW_B


GROUP_B = HB // HKV


def _stack_group(ref, g):
    return jnp.concatenate([ref[:, h * HD:(h + 1) * HD] for h in range(g * GROUP_B, (g + 1) * GROUP_B)], axis=0)


def _sink_column(sink_ref, g):
    return jnp.concatenate([jnp.full((BQ_B, 1), sink_ref[h], F32) for h in range(g * GROUP_B, (g + 1) * GROUP_B)],
                           axis=0)


def _unstack(stacked):
    return [s[j * BQ_B:(j + 1) * BQ_B] for s in stacked for j in range(GROUP_B)]


def _attn_b_fwd(name, qb, kb, vb, sink):
    def body(sink_ref, q_ref, k_ref, v_ref, o_ref, lse_ref):
        start, valid = _window_b(pl.program_id(0))
        valid = jnp.concatenate([valid] * GROUP_B, axis=0)
        kw, vw = k_ref[pl.ds(start, SPAN_B), :], v_ref[pl.ds(start, SPAN_B), :]
        outs = _heads_fwd([dict(q=_stack_group(q_ref, g), k=kw[:, g * HD:(g + 1) * HD], v=vw[:, g * HD:(g + 1) * HD],
                                valid=valid, sink=_sink_column(sink_ref, g)) for g in range(HKV)])
        o_ref[...] = jnp.concatenate(_unstack([o for o, _ in outs]), axis=1)
        lse_ref[...] = _per_head(_unstack([lse for _, lse in outs]))

    qs = pl.BlockSpec((BQ_B, WB), lambda i: (i, 0))
    return pl.pallas_call(
        body, name=name, grid=(T // BQ_B,),
        in_specs=[pl.BlockSpec(memory_space=pltpu.SMEM), qs, _whole((T, WKV)), _whole((T, WKV))],
        out_specs=[qs, qs],
        out_shape=[jax.ShapeDtypeStruct((T, WB), F32)] * 2, compiler_params=_params(("parallel",)),
    )(sink, qb, kb, vb)


def _attn_b_bwd(name, qb, kb, vb, ob, lse, dob, sink):
    def body(sink_ref, q_ref, k_ref, v_ref, o_ref, lse_ref, do_ref, dq_ref, dk_ref, dv_ref, dsink_ref):
        i = pl.program_id(0)
        start, valid = _window_b(i)
        valid = jnp.concatenate([valid] * GROUP_B, axis=0)
        kw, vw = k_ref[pl.ds(start, SPAN_B), :], v_ref[pl.ds(start, SPAN_B), :]
        res = _heads_bwd([dict(q=_stack_group(q_ref, g), k=kw[:, g * HD:(g + 1) * HD], v=vw[:, g * HD:(g + 1) * HD],
                               o=_stack_group(o_ref, g), do=_stack_group(do_ref, g),
                               lse=jnp.concatenate([lse_ref[:, h * HD:h * HD + 1]
                                                    for h in range(g * GROUP_B, (g + 1) * GROUP_B)], axis=0),
                               valid=valid, sink=_sink_column(sink_ref, g)) for g in range(HKV)])
        dks, dvs = [r[1] for r in res], [r[2] for r in res]
        lane = lax.broadcasted_iota(jnp.int32, (1, LANE), 1)
        dsink = jnp.zeros((1, LANE), F32)
        for h, rows in enumerate(_unstack([r[4] for r in res])):
            dsink += jnp.where(lane == h, jnp.sum(rows), 0.0)
        dq_ref[...] = jnp.concatenate(_unstack([r[0] for r in res]), axis=1)

        @pl.when(i == 0)
        def _():
            dk_ref[...] = jnp.zeros_like(dk_ref)
            dv_ref[...] = jnp.zeros_like(dv_ref)
            dsink_ref[...] = jnp.zeros_like(dsink_ref)

        dk_ref[pl.ds(start, SPAN_B), :] += jnp.concatenate(dks, axis=1)
        dv_ref[pl.ds(start, SPAN_B), :] += jnp.concatenate(dvs, axis=1)
        dsink_ref[...] += dsink

    qs = pl.BlockSpec((BQ_B, WB), lambda i: (i, 0))
    return pl.pallas_call(
        body, name=name, grid=(T // BQ_B,),
        in_specs=[pl.BlockSpec(memory_space=pltpu.SMEM), qs, _whole((T, WKV)), _whole((T, WKV)), qs, qs, qs],
        out_specs=[qs, _whole((T, WKV)), _whole((T, WKV)), _whole((1, LANE))],
        out_shape=[jax.ShapeDtypeStruct((T, WB), F32), jax.ShapeDtypeStruct((T, WKV), F32),
                   jax.ShapeDtypeStruct((T, WKV), F32), jax.ShapeDtypeStruct((1, LANE), F32)],
        compiler_params=_params(("arbitrary",)),
    )(sink, qb, kb, vb, ob, lse, dob)


SPAN_C = NA_ROWS * GRID_W


def _row_start(r):
    return jnp.clip(r - NA_ROWS // 2, 0, ROWS - NA_ROWS)


def _off_index(r):
    return _row_start(r) - r + (NA_ROWS - 1)


N_TAB = 16
RPS_FWD, RPS_BWD = 4, 8


def _rpb_tables(name, rpb):
    circ = jnp.concatenate([rpb[..., NA_COLS - 1:], jnp.zeros(rpb.shape[:2] + (LANE - (2 * NA_COLS - 1),), F32),
                            rpb[..., :NA_COLS - 1]], axis=-1)
    circ = jnp.pad(circ, ((0, 0), (0, N_TAB + 1 - circ.shape[1]), (0, 0)))

    def body(w_ref, o_ref):
        c = lax.broadcasted_iota(jnp.int32, (GRID_W, LANE), 0)
        lane = lax.broadcasted_iota(jnp.int32, (GRID_W, LANE), 1)
        cs = jnp.clip(c - NA_COLS // 2, 0, GRID_W - NA_COLS)
        valid = (lane % GRID_W >= cs) & (lane % GRID_W < cs + NA_COLS)
        toep = [pltpu.roll(jnp.broadcast_to(w_ref[a:a + 1, :], (GRID_W, LANE)), 0, 1, stride=1, stride_axis=0)
                for a in range(N_TAB + 1)]
        for a in range(N_TAB):
            pair = jnp.where(lane < GRID_W, toep[a], pltpu.roll(toep[a + 1], GRID_W, 1))
            o_ref[a] = jnp.where(valid, pair, NEG)

    return pl.pallas_call(
        body, name=name, grid=(HC,),
        in_specs=[pl.BlockSpec((None, N_TAB + 1, LANE), lambda h: (h, 0, 0))],
        out_specs=pl.BlockSpec((None, N_TAB, GRID_W, LANE), lambda h: (h, 0, 0, 0)),
        out_shape=jax.ShapeDtypeStruct((HC, N_TAB, GRID_W, LANE), F32), compiler_params=_params(("parallel",)),
    )(circ)


def _bias_c(t_ref, h, d):
    return jnp.concatenate([t_ref[h, d + k] for k in range(0, NA_ROWS, 2)], axis=1)


def _attn_c_fwd(name, qkv, tables):
    RPS = RPS_FWD

    def body(q_ref, k_ref, v_ref, t_ref, o_ref, lse_ref):
        heads = []
        for rr in range(RPS):
            r = pl.program_id(1) * RPS + rr
            rows = slice(rr * GRID_W, (rr + 1) * GRID_W)
            start = pl.multiple_of(_row_start(r) * GRID_W, GRID_W)
            kw, vw = k_ref[pl.ds(start, SPAN_C), :], v_ref[pl.ds(start, SPAN_C), :]
            heads += [dict(q=q_ref[rows, h * HD:(h + 1) * HD], k=kw[:, h * HD:(h + 1) * HD], v=vw[:, h * HD:(h + 1) * HD],
                           bias=_bias_c(t_ref, h, _off_index(r))) for h in range(2)]
        outs = _heads_fwd(heads)
        for rr in range(RPS):
            rows = slice(rr * GRID_W, (rr + 1) * GRID_W)
            o_ref[rows, :] = jnp.concatenate([o for o, _ in outs[2 * rr:2 * rr + 2]], axis=1)
            lse_ref[rows, :] = _per_head([lse for _, lse in outs[2 * rr:2 * rr + 2]])

    qs = pl.BlockSpec((RPS * GRID_W, LANE), lambda p, r: (r, p))
    ks = pl.BlockSpec((T, LANE), lambda p, r: (0, p))
    ts = pl.BlockSpec((2, N_TAB, GRID_W, LANE), lambda p, r: (p, 0, 0, 0))
    return pl.pallas_call(
        body, name=name, grid=(HC // 2, ROWS // RPS),
        in_specs=[_qkv_rows(RPS * GRID_W, "qc"), _qkv_all("kc"), _qkv_all("vc"), ts], out_specs=[qs, qs],
        out_shape=[jax.ShapeDtypeStruct((T, WC), F32)] * 2, compiler_params=_params(("parallel", "parallel")),
    )(qkv, qkv, qkv, tables)


def _attn_c_bwd(name, qkv, oc, lse, doc, tables):
    RPS = RPS_BWD

    def body(q_ref, k_ref, v_ref, o_ref, lse_ref, do_ref, t_ref, dq_ref, dk_ref, dv_ref, dt_ref):
        @pl.when(pl.program_id(1) == 0)
        def _():
            dk_ref[...] = jnp.zeros_like(dk_ref)
            dv_ref[...] = jnp.zeros_like(dv_ref)
            dt_ref[...] = jnp.zeros_like(dt_ref)

        heads, where = [], []
        for rr in range(RPS):
            r = pl.program_id(1) * RPS + rr
            rows = slice(rr * GRID_W, (rr + 1) * GRID_W)
            d = _off_index(r)
            start = pl.multiple_of(_row_start(r) * GRID_W, GRID_W)
            kw, vw = k_ref[pl.ds(start, SPAN_C), :], v_ref[pl.ds(start, SPAN_C), :]
            where.append((rows, d, start))
            for h in range(2):
                sl = slice(h * HD, (h + 1) * HD)
                heads.append(dict(q=q_ref[rows, sl], k=kw[:, sl], v=vw[:, sl], o=o_ref[rows, sl], do=do_ref[rows, sl],
                                  lse=lse_ref[rows, h * HD:h * HD + 1], bias=_bias_c(t_ref, h, d)))
        res = _heads_bwd(heads)
        for rr, (rows, d, start) in enumerate(where):
            pair = res[2 * rr:2 * rr + 2]
            for h in range(2):
                for k in range(0, NA_ROWS, 2):
                    dt_ref[h, d + k] += pair[h][3][:, k * GRID_W:(k + 2) * GRID_W]
            dq_ref[rows, :] = jnp.concatenate([p[0] for p in pair], axis=1)
            dk_ref[pl.ds(start, SPAN_C), :] += jnp.concatenate([p[1] for p in pair], axis=1)
            dv_ref[pl.ds(start, SPAN_C), :] += jnp.concatenate([p[2] for p in pair], axis=1)

    qs = pl.BlockSpec((RPS * GRID_W, LANE), lambda p, r: (r, p))
    ks = pl.BlockSpec((T, LANE), lambda p, r: (0, p))
    ts = pl.BlockSpec((2, N_TAB, GRID_W, LANE), lambda p, r: (p, 0, 0, 0))
    return pl.pallas_call(
        body, name=name, grid=(HC // 2, ROWS // RPS),
        in_specs=[_qkv_rows(RPS * GRID_W, "qc"), _qkv_all("kc"), _qkv_all("vc"), qs, qs, qs, ts],
        out_specs=[qs, ks, ks, ts],
        out_shape=[jax.ShapeDtypeStruct((T, WC), F32)] * 3 + [jax.ShapeDtypeStruct((HC, N_TAB, GRID_W, LANE), F32)],
        compiler_params=_params(("parallel", "arbitrary")),
    )(qkv, qkv, qkv, oc, lse, doc, tables)


def _split3(v):
    hi = v.astype(BF16)
    r1 = v - hi.astype(F32)
    mid = r1.astype(BF16)
    lo = (r1 - mid.astype(F32)).astype(BF16)
    return hi, mid, lo


def _rpb_reduce(name, dtables):
    x = dtables.reshape(HC, N_TAB, GRID_W * LANE)
    c = jnp.arange(GRID_W)[:, None]
    lane = jnp.arange(LANE)[None, :]
    col = (lane // GRID_W) * LANE + jnp.clip(lane % GRID_W - c + (NA_COLS - 1), 0, 2 * NA_COLS - 2)
    col_onehot = (col.reshape(-1)[:, None] == jnp.arange(2 * LANE)[None, :]).astype(BF16)
    a2 = jnp.arange(N_TAB)[None, :]
    row_onehot = jnp.concatenate([(jnp.arange(16)[:, None] == a2 + u) & (a2 < 2 * NA_ROWS - 2) for u in range(2)],
                                 axis=1).astype(BF16)

    def body(x_ref, e_ref, f_ref, o_ref):
        y = sum(jnp.dot(part, e_ref[...], preferred_element_type=F32) for part in _split3(x_ref[...]))
        z = jnp.concatenate([y[:, :LANE], y[:, LANE:]], axis=0)
        o_ref[...] = sum(jnp.dot(f_ref[...], part, preferred_element_type=F32) for part in _split3(z))

    out = pl.pallas_call(
        body, name=name, grid=(HC,),
        in_specs=[pl.BlockSpec((None, N_TAB, GRID_W * LANE), lambda h: (h, 0, 0)),
                  _whole((GRID_W * LANE, 2 * LANE)), _whole((16, 2 * N_TAB))],
        out_specs=pl.BlockSpec((None, 16, LANE), lambda h: (h, 0, 0)),
        out_shape=jax.ShapeDtypeStruct((HC, 16, LANE), F32), compiler_params=_params(("parallel",)),
    )(x, col_onehot, row_onehot)
    return out[:, :2 * NA_ROWS - 1, :2 * NA_COLS - 1]


TC = 128
CHUNK = 128
MARGIN = 8


def _shift_down(v, rows):
    return jnp.where(rows == 0, 0.0, pltpu.roll(v, 1, 0))


def _shift_up(v, rows):
    return jnp.where(rows == T - 1, 0.0, pltpu.roll(v, T - 1, 0))


def _conv(v, w, b, rows):
    return _shift_down(v, rows) * w[0:1] + v * w[1:2] + _shift_up(v, rows) * w[2:3] + b


FWD_BLOCKS = 4
BWD_BLOCKS = 1


def _ffn_mid_fwd(name, up, conv_w, conv_b):
    wide = FWD_BLOCKS * TC

    def body(xg_ref, xv_ref, wg_ref, wv_ref, bg_ref, bv_ref, o_ref):
        rows = lax.broadcasted_iota(jnp.int32, (T, TC), 0)
        for b in range(FWD_BLOCKS):
            lanes = slice(b * TC, (b + 1) * TC)
            ug = _conv(xg_ref[b], wg_ref[:, lanes], bg_ref[:, lanes], rows)
            uv = _conv(xv_ref[b], wv_ref[:, lanes], bv_ref[:, lanes], rows)
            o_ref[:, lanes] = (ug * jax.nn.sigmoid(ug) * uv).astype(BF16)

    gate = lambda shape: pl.BlockSpec(shape, lambda j: (0, j))
    val = lambda shape: pl.BlockSpec(shape, lambda j: (0, j + DFF // wide))
    return pl.pallas_call(
        body, name=name, grid=(DFF // wide,),
        in_specs=[pl.BlockSpec((FWD_BLOCKS, T, TC), lambda j: (j, 0, 0)),
                  pl.BlockSpec((FWD_BLOCKS, T, TC), lambda j: (j + DFF // wide, 0, 0)),
                  gate((3, wide)), val((3, wide)), gate((1, wide)), val((1, wide))],
        out_specs=pl.BlockSpec((T, wide), lambda j: (0, j)),
        out_shape=jax.ShapeDtypeStruct((T, DFF), BF16), compiler_params=_params(("parallel",)),
    )(up, up, conv_w, conv_w, conv_b, conv_b)


def _ffn_mid_bwd(name, dact, up, conv_w, conv_b):
    window = CHUNK + 2 * MARGIN
    centre = slice(MARGIN, MARGIN + CHUNK)

    def shifted(v):
        return pltpu.roll(v, 1, 0), pltpu.roll(v, window - 1, 0)

    def fold(v):
        return jnp.sum(v[centre].reshape(CHUNK // 8, 8, TC), axis=0)

    wide = BWD_BLOCKS * TC

    def body(da_ref, xg_ref, xv_ref, wg_ref, wv_ref, bg_ref, bv_ref, dx_ref, dw_ref, db_ref):
        for b in range(BWD_BLOCKS):
            block(b, da_ref, xg_ref, xv_ref, wg_ref, wv_ref, bg_ref, bv_ref, dx_ref, dw_ref, db_ref)

    def block(b, da_ref, xg_ref, xv_ref, wg_ref, wv_ref, bg_ref, bv_ref, dx_ref, dw_ref, db_ref):
        lanes = slice(b * TC, (b + 1) * TC)
        wg, wv, bg, bv = wg_ref[:, lanes], wv_ref[:, lanes], bg_ref[:, lanes], bv_ref[:, lanes]
        margin = jnp.zeros((MARGIN, TC), F32)
        last = T // CHUNK - 1

        def windows(c):
            if isinstance(c, int) and c == 0:
                rows = slice(0, CHUNK + MARGIN)
                return [jnp.concatenate([margin, v], axis=0)
                        for v in (da_ref[rows, lanes], xg_ref[b, rows, :], xv_ref[b, rows, :])]
            if isinstance(c, int) and c == last:
                rows = slice(T - CHUNK - MARGIN, T)
                return [jnp.concatenate([v, margin], axis=0)
                        for v in (da_ref[rows, lanes], xg_ref[b, rows, :], xv_ref[b, rows, :])]
            rows = pl.ds(pl.multiple_of(c * CHUNK - MARGIN, MARGIN), window)
            return [da_ref[rows, lanes], xg_ref[b, rows, :], xv_ref[b, rows, :]]

        def chunk(c, sums):
            r0 = c * CHUNK if isinstance(c, int) else pl.multiple_of(c * CHUNK, CHUNK)
            da, xg, xv = windows(c)
            xg_prev, xg_next = shifted(xg)
            xv_prev, xv_next = shifted(xv)
            ug = xg_prev * wg[0:1] + xg * wg[1:2] + xg_next * wg[2:3] + bg
            uv = xv_prev * wv[0:1] + xv * wv[1:2] + xv_next * wv[2:3] + bv
            sg = jax.nn.sigmoid(ug)
            dug = da * uv * (sg * (1.0 + ug * (1.0 - sg)))
            duv = da * (ug * sg)
            out = []
            for half, (x_prev, x, x_next, w, du) in enumerate(((xg_prev, xg, xg_next, wg, dug),
                                                               (xv_prev, xv, xv_next, wv, duv))):
                du_prev, du_next = shifted(du)
                dx = du_next * w[0:1] + du * w[1:2] + du_prev * w[2:3]
                dx_ref[half, pl.ds(r0, CHUNK), lanes] = dx[centre].astype(BF16)
                out += [fold(x_prev * du), fold(x * du), fold(x_next * du), fold(du)]
            return tuple(s + o for s, o in zip(sums, out))

        sums = chunk(0, tuple(jnp.zeros((8, TC), F32) for _ in range(8)))
        sums = lax.fori_loop(1, last, chunk, sums)
        sums = chunk(last, sums)
        rows = [jnp.sum(s, axis=0, keepdims=True) for s in sums]
        for half in range(2):
            dw_ref[half, :, lanes] = jnp.concatenate(rows[4 * half:4 * half + 3], axis=0)
            db_ref[half, :, lanes] = rows[4 * half + 3]

    gate = lambda shape: pl.BlockSpec(shape, lambda j: (0, j))
    val = lambda shape: pl.BlockSpec(shape, lambda j: (0, j + DFF // wide))
    return pl.pallas_call(
        body, name=name, grid=(DFF // wide,),
        in_specs=[gate((T, wide)), pl.BlockSpec((BWD_BLOCKS, T, TC), lambda j: (j, 0, 0)),
                  pl.BlockSpec((BWD_BLOCKS, T, TC), lambda j: (j + DFF // wide, 0, 0)),
                  gate((3, wide)), val((3, wide)), gate((1, wide)), val((1, wide))],
        out_specs=[pl.BlockSpec((2, T, wide), lambda j: (0, 0, j)), pl.BlockSpec((2, 3, wide), lambda j: (0, 0, j)),
                   pl.BlockSpec((2, 1, wide), lambda j: (0, 0, j))],
        out_shape=[jax.ShapeDtypeStruct((2, T, DFF), BF16), jax.ShapeDtypeStruct((2, 3, DFF), F32),
                   jax.ShapeDtypeStruct((2, 1, DFF), F32)],
        compiler_params=_params(("parallel",)),
    )(dact, up, up, conv_w, conv_w, conv_b, conv_b)


def _dup_spec(tm, nj):
    per = DFF // nj
    return pl.BlockSpec((None, tm, nj), lambda a, b, j: (j // per, 0 if tm == T else b, j % per))


def _dup_spec_tn(tm, nj):
    per = DFF // nj
    return pl.BlockSpec((None, tm, nj), lambda j, kt, r: (j // per, 0, j % per))


def _adamw_math(w, g, m, v):
    m = ADAM_B1 * m + (1.0 - ADAM_B1) * g
    v = ADAM_B2 * v + (1.0 - ADAM_B2) * (g * g)
    m_hat = m / (1.0 - ADAM_B1 ** ADAM_STEP)
    v_hat = v / (1.0 - ADAM_B2 ** ADAM_STEP)
    delta = -ADAM_LR * (m_hat / (jnp.sqrt(v_hat) + ADAM_EPS) + ADAM_WD * w)
    return delta, m, v


ADAM_BLOCK = 256 * 1408


def _adamw_sharded(name, w, m, v, parts):
    _, r, c = w.shape
    tr = max(t for t in range(16, r + 1, 16) if r % t == 0 and t * c <= ADAM_BLOCK)

    def body(w_ref, m_ref, v_ref, p0_ref, p1_ref, g_ref, d_ref, nm_ref, nv_ref):
        def run(p_ref):
            g = p_ref[0].astype(F32)
            for k in range(1, N_DEV):
                g = g + p_ref[k].astype(F32)
            d, nm, nv = _adamw_math(w_ref[...], g, m_ref[...], v_ref[...])
            g_ref[...] = g
            d_ref[...] = d
            nm_ref[...] = nm
            nv_ref[...] = nv

        @pl.when(pl.program_id(0) == 0)
        def _():
            run(p0_ref)

        @pl.when(pl.program_id(0) == 1)
        def _():
            run(p1_ref)

    ws = pl.BlockSpec((None, tr, c), lambda l, i: (l, i, 0))
    p0 = pl.BlockSpec((N_DEV, tr, c), lambda l, i: (0, jnp.where(l == 0, i, r // tr - 1), 0))
    p1 = pl.BlockSpec((N_DEV, tr, c), lambda l, i: (0, jnp.where(l == 1, i, 0), 0))
    return pl.pallas_call(
        body, name=name, grid=(DEPTH, r // tr), in_specs=[ws, ws, ws, p0, p1], out_specs=[ws] * 4,
        out_shape=[jax.ShapeDtypeStruct(w.shape, F32)] * 4, compiler_params=_params(("arbitrary", "arbitrary")),
    )(w, m, v, *parts)


def _sum_devices(name, parts):
    r = parts.shape[1]

    def body(p_ref, o_ref):
        g = p_ref[0]
        for k in range(1, N_DEV):
            g = g + p_ref[k]
        o_ref[...] = g

    return pl.pallas_call(
        body, name=name, in_specs=[pl.BlockSpec((N_DEV, r, LANE), lambda: (0, 0, 0))],
        out_specs=pl.BlockSpec((r, LANE), lambda: (0, 0)), out_shape=jax.ShapeDtypeStruct((r, LANE), F32),
        compiler_params=_params(),
    )(parts)


def _adamw_small(name, ws, gs, ms, vs):
    n = len(ws)
    shapes = [w.shape for w in ws]
    ws, gs, ms, vs = ([a.reshape(1, -1) if a.ndim == 1 else a for a in arrs] for arrs in (ws, gs, ms, vs))
    specs = [pl.BlockSpec(memory_space=pltpu.VMEM)] * n

    def body(*refs):
        for i in range(n):
            w_ref, g_ref, m_ref, v_ref = (refs[k * n + i] for k in range(4))
            d, nm, nv = _adamw_math(w_ref[...], g_ref[...], m_ref[...], v_ref[...])
            refs[4 * n + i][...] = d
            refs[5 * n + i][...] = nm
            refs[6 * n + i][...] = nv

    outs = pl.pallas_call(
        body, name=name, in_specs=specs * 4, out_specs=specs * 3,
        out_shape=[jax.ShapeDtypeStruct(w.shape, F32) for w in ws] * 3, compiler_params=_params(),
    )(*ws, *gs, *ms, *vs)
    outs = [o.reshape(shapes[i % n]) for i, o in enumerate(outs)]
    return outs[:n], outs[n:2 * n], outs[2 * n:]


def _pack(arrays):
    flat = jnp.concatenate([a.reshape(-1) for a in arrays])
    pad = (-flat.shape[0]) % (8 * LANE)
    return jnp.pad(flat, (0, pad)).reshape(-1, LANE)


def _unpack(buf, shapes):
    flat, out, off = buf.reshape(-1), [], 0
    for s in shapes:
        n = 1
        for d in s:
            n *= d
        out.append(flat[off:off + n].reshape(s))
        off += n
    return out


def _local_step(x, target, small, weights, conv_w_full, hand_over, used):
    cos2, sin2 = _rope_tables()
    bias_a = _dilation_bias()
    tables = [_rpb_tables(f"rpb_tables_{l}", small["rpb_c"][l]) for l in range(DEPTH)]
    saved, carry = [], 0.0
    for l in range(DEPTH):
        g1, g2 = small["ln_attn"][l][None] + carry, small["ln_ffn"][l][None]
        gain, sink, cb = small["mix_gain"][l][None], small["sink_b"][l], small["conv_b"][l][None]
        cw = conv_w_full[l]
        bias = tables[l]
        h1, qkv = _prologue_matmul(f"proj_in_{l}", _rmsnorm_rows, [x, g1], [D, None],
                                   weights("w_in", l, [cos2, sin2, bias_a] + tables if l == 0 else x),
                                   (D, 1024), lambda j: (0, j), 1024, epilogue=_rope_epilogue, extras=(cos2, sin2),
                                   out_dtype=BF16)
        zero = used(f"proj_in_{l}", qkv)
        qb, kb, vb = (qkv[:, BLOCK_OF[n] * LANE:BLOCK_OF[n] * LANE + w] for n, w in (("qb", WB), ("kb", WKV), ("vb", WKV)))
        oa, lse_a = _attn_a_fwd(f"attn_a_{l}", qkv, bias_a)
        ob, lse_b = _attn_b_fwd(f"attn_b_{l}", qb, kb, vb, sink + zero)
        oc, lse_c = _attn_c_fwd(f"attn_c_{l}", qkv, bias)
        mixed, x_mid = _prologue_matmul(f"proj_out_{l}", _mix_rows, [oa, ob, oc, gain + used(f"attn_{l}", oc)],
                                        [WA, WB, WC, None],
                                        weights("w_out", l, oc), (N_DEV, D // N_DEV, 512), lambda j: (0, 0, j), 512,
                                        res=x)
        h2, up = _prologue_matmul(f"ffn_up_{l}", _rmsnorm_rows, [x_mid, g2 + used(f"proj_out_{l}", x_mid)], [D, None],
                                  weights("w_up", l, x_mid), (D, 1024), lambda j: (0, j), 1024, blocked_out=True)
        act = _ffn_mid_fwd(f"ffn_mid_{l}", up, cw, cb + used(f"ffn_up_{l}", up))
        x_out = _nn_rows(f"ffn_down_{l}", act, weights("w_down", l, act), x_mid, 4, 1024, 1024)
        carry = used(f"ffn_down_{l}", x_out)
        saved.append(dict(x=x, h1=h1, qkv=(qkv, qb, kb, vb), o=(oa, ob, oc), lse=(lse_a, lse_b, lse_c), mixed=mixed,
                          x_mid=x_mid, h2=h2, up=up, act=act, g1=g1, g2=g2, gain=gain, sink=sink, cb=cb, cw=cw, bias=bias))
        x = x_out

    loss8, dx, dxb, d_ln_final = _loss_head(x, small["ln_final"][None], target)
    sgrads = [None] * DEPTH
    for l in reversed(range(DEPTH)):
        s = saved[l]
        qkv, qb, kb, vb = s["qkv"]
        oa, ob, oc = s["o"]
        wg_in, wg_out = weights("w_in", l, None), weights("w_out", l, None)
        wg_up, wg_down = weights("w_up", l, None), weights("w_down", l, None)
        g_down = _tn_rows(f"wgrad_down_{l}", s["act"], dxb, wg_down.shape[1], 2, 512)
        zero = hand_over("w_down", l, g_down)
        dact = _nt_rows(f"dgrad_down_{l}", dxb, wg_down, 4, 512)
        dup, d_cw, d_cb = _ffn_mid_bwd(f"ffn_mid_bwd_{l}", dact, s["up"], s["cw"], s["cb"] + zero)
        g_up = _tn_cols(f"wgrad_up_{l}", s["h2"], dup, _dup_spec_tn, 2 * DFF, DFF // 2)
        zero = hand_over("w_up", l, g_up)
        dh2 = _nt_cols(f"dgrad_up_{l}", dup, _dup_spec, wg_up, DFF // 2)
        dx, dxb, d_g2 = _rmsnorm_bwd(f"norm_ffn_bwd_{l}", dh2, s["x_mid"], s["g2"] + zero, dx)
        g_out = _tn_rows(f"wgrad_out_{l}", s["mixed"], dxb, wg_out.shape[1], 2, D)
        zero = hand_over("w_out", l, g_out)
        dmixed = _nt_rows(f"dgrad_out_{l}", dxb, wg_out, 2, T)
        doa, dob, doc, d_gain = _mix_bwd(f"mix_bwd_{l}", dmixed, oa, ob, oc, s["gain"] + zero)
        lse_a, lse_b, lse_c = s["lse"]
        dqa, dka, dva = _attn_a_bwd(f"attn_a_bwd_{l}", qkv, oa, lse_a, doa, bias_a)
        dqb, dkb, dvb, d_sink = _attn_b_bwd(f"attn_b_bwd_{l}", qb, kb, vb, ob, lse_b, dob, s["sink"])
        dqc, dkc, dvc, d_bias = _attn_c_bwd(f"attn_c_bwd_{l}", qkv, oc, lse_c, doc, s["bias"])
        d_rpb = _rpb_reduce(f"rpb_reduce_{l}", d_bias)
        dproj = _rope_bwd(f"rope_bwd_{l}", (dqa, dka, dva, dqb, dkb, dvb, dqc, dkc, dvc), cos2, sin2)
        g_in = _tn_cols(f"wgrad_in_{l}", s["h1"], dproj,
                        lambda tm, tn: pl.BlockSpec((tm, tn), lambda j, kt, r: (0, j)), IN_COLS, 1024)
        zero = hand_over("w_in", l, g_in)
        dh1 = _nt_cols(f"dgrad_in_{l}", dproj, lambda tm, nc: pl.BlockSpec((tm, nc), lambda kt, i, j: (i, j)), wg_in,
                       IN_COLS // 2)
        dx, dxb, d_g1 = _rmsnorm_bwd(f"norm_attn_bwd_{l}", dh1, s["x"], s["g1"] + zero, dx)
        sgrads[l] = dict(ln_attn=d_g1[0], sink_b=d_sink[0, :HB], rpb_c=d_rpb, mix_gain=d_gain[0], ln_ffn=d_g2[0],
                         conv_w=d_cw.transpose(1, 0, 2).reshape(3, 2 * DFF), conv_b=d_cb.reshape(2 * DFF))
    return loss8[0, 0], dx, d_ln_final[0], sgrads


SMALL_NAMES = ("ln_attn", "sink_b", "rpb_c", "mix_gain", "ln_ffn", "conv_b")


def kernel(x, ln_attn, w_in, sink_b, rpb_c, mix_gain, w_out, ln_ffn, w_up, conv_w, conv_b, w_down, ln_final, loss_target, m_ln_attn, m_w_in, m_sink_b, m_rpb_c, m_mix_gain, m_w_out, m_ln_ffn, m_w_up, m_conv_w, m_conv_b, m_w_down, m_ln_final, v_ln_attn, v_w_in, v_sink_b, v_rpb_c, v_mix_gain, v_w_out, v_ln_ffn, v_w_up, v_conv_w, v_conv_b, v_w_down, v_ln_final):
    me = 4 * lax.axis_index("x") + 2 * lax.axis_index("y") + lax.axis_index("c")
    small = dict(ln_attn=ln_attn, sink_b=sink_b, rpb_c=rpb_c, mix_gain=mix_gain, ln_ffn=ln_ffn, conv_b=conv_b,
                 ln_final=ln_final)

    names = ("w_in", "w_out", "w_up", "w_down")
    shards = dict(w_in=w_in, w_out=w_out, w_up=w_up, w_down=w_down)
    order = [(n, l) for l in range(DEPTH) for n in names]
    conv_key = ("conv_w", 0)
    started, arrived, forwarded, gathered = {}, {}, {}, {}

    def side_by_side(k):
        return k[0] in ("w_in", "w_up")

    def slot_of(k):
        return _col_slot(shards[k[0]].shape[2]) if side_by_side(k) else _lead_slot

    def begin(name, ks, zero):
        srcs = [_pack([conv_w]) + zero if k == conv_key else (shards[k[0]][k[1]] + zero).astype(BF16) for k in ks]
        lands = [lax.empty((s.shape[0], N_DEV * s.shape[1]) if side_by_side(k) else (N_DEV,) + s.shape, s.dtype)
                 for k, s in zip(ks, srcs)]
        peers = [ALL_PEERS if k == conv_key else NEAR_PEERS for k in ks]
        send, recv, bufs, tok = _copy_start(name, srcs + lands, _gather_plan(peers, [slot_of(k) for k in ks]),
                                            [len(p) + 1 for p in peers])
        for i, k in enumerate(ks):
            started[k] = (send[i], recv[i], bufs[i], bufs[len(ks) + i], peers[i])
        return tok

    token = begin("gather_start_first", order[:1], 0.0)
    token = begin("gather_start_rest", [conv_key] + order[1:], token[0, 0])

    def arrive(k, after):
        send, recv, src, land, peers = started[k]
        arrived[k] = _copy_wait(f"gather_{k[0]}_{k[1]}_arrived", [src, land], [send], [recv],
                                _gather_plan([peers], [slot_of(k)]), after)

    queue = list(order)

    def advance(after):
        if not queue:
            return 0.0
        k = queue.pop(0)
        arrive(k, after)
        forwarded[k] = _copy_start(f"gather_{k[0]}_{k[1]}_forward", [arrived[k][1]], _forward_plan(slot_of(k)),
                                   [len(OTHER_CHIPS)])
        return forwarded[k][3][0, 0]

    pass_on_behind = ("proj_in_0", "attn_0", "ffn_up_0", "ffn_down_0", "proj_in_1", "attn_1", "ffn_up_1")

    def used(point, result):
        return advance(result) if point in pass_on_behind else 0.0

    def weights(n, l, after):
        k = (n, l)
        if k not in gathered:
            if k not in forwarded:
                advance(after)
            send_b, recv_b, (land,), _ = forwarded[k]
            (gathered[k],) = _copy_wait(f"gather_{n}_{l}_done", [land], send_b, recv_b, _forward_plan(slot_of(k)),
                                        after)
        return gathered[k]

    pending = {}

    def hand_over(n, l, g):
        shard = shards[n].shape[1:]
        send, recv, bufs, tok = _copy_start(f"send_grad_{n}_{l}", [g, lax.empty((N_DEV,) + shard, g.dtype)],
                                            _scatter_plan(slot_of((n, l))), [len(ALL_PEERS) + 1])
        pending[(n, l)] = (send, recv, bufs)
        return tok[0, 0]

    def received(k, after):
        send, recv, bufs = pending[k]
        return _copy_wait(f"recv_grad_{k[0]}_{k[1]}", bufs, send, recv, _scatter_plan(slot_of(k)), after)[1]

    arrive(conv_key, token)
    cw_all = arrived[conv_key][1]
    nup = w_up.shape[2]
    cw_shards = cw_all.reshape(N_DEV, -1)[:, :DEPTH * 3 * nup].reshape(N_DEV, DEPTH, 3, nup)
    conv_w_full = cw_shards.transpose(1, 2, 0, 3).reshape(DEPTH, 3, N_DEV * nup)

    loss_local, dx, d_ln_final, sgrads = _local_step(
        x[0], loss_target[0], dict(small, ln_attn=ln_attn + token[0, 0]), weights, conv_w_full, hand_over, used)

    stacked = [jnp.stack([sgrads[l][n] for l in range(DEPTH)]) for n in SMALL_NAMES + ("conv_w",)] + [d_ln_final]
    shapes = [a.shape for a in stacked]
    mine = _pack(stacked)
    send_s, recv_s, bufs_s, _ = _copy_start("gather_small_grads_start", [mine, lax.empty((N_DEV,) + mine.shape, F32)],
                                            _gather_plan([ALL_PEERS], [_lead_slot]), [len(ALL_PEERS) + 1])

    big, after = {}, dx
    moments = dict(w_in=(m_w_in, v_w_in), w_out=(m_w_out, v_w_out), w_up=(m_w_up, v_w_up), w_down=(m_w_down, v_w_down))
    for n in reversed(names):
        parts = (received((n, 0), after), received((n, 1), after))
        big[n] = _adamw_sharded(f"adamw_{n}", shards[n], *moments[n], parts)
        after = big[n][1]

    _, everyone = _copy_wait("gather_small_grads_done", bufs_s, send_s, recv_s,
                             _gather_plan([ALL_PEERS], [_lead_slot]), after)
    g_small = _unpack(_sum_devices("sum_small_grads", everyone), shapes)
    g = dict(zip(SMALL_NAMES + ("conv_w", "ln_final"), g_small))
    g["conv_w"] = lax.dynamic_slice_in_dim(g["conv_w"], me * nup, nup, axis=2)

    snames = SMALL_NAMES + ("conv_w", "ln_final")
    sw = dict(small, conv_w=conv_w)
    sm = dict(ln_attn=m_ln_attn, sink_b=m_sink_b, rpb_c=m_rpb_c, mix_gain=m_mix_gain, ln_ffn=m_ln_ffn,
              conv_b=m_conv_b, conv_w=m_conv_w, ln_final=m_ln_final)
    sv = dict(ln_attn=v_ln_attn, sink_b=v_sink_b, rpb_c=v_rpb_c, mix_gain=v_mix_gain, ln_ffn=v_ln_ffn,
              conv_b=v_conv_b, conv_w=v_conv_w, ln_final=v_ln_final)
    s_delta, s_m, s_v = (dict(zip(snames, out)) for out in _adamw_small(
        "adamw_small", [sw[n] for n in snames], [g[n] for n in snames], [sm[n] for n in snames],
        [sv[n] for n in snames]))

    loss = lax.psum(loss_local, ("x", "y", "c"))
    outputs = ("ln_attn", "w_in", "sink_b", "rpb_c", "mix_gain", "w_out", "ln_ffn", "w_up", "conv_w", "conv_b",
               "w_down", "ln_final")
    grads = [big[n][0] if n in big else g[n] for n in outputs]
    deltas = [big[n][1] if n in big else s_delta[n] for n in outputs]
    new_m = [big[n][2] if n in big else s_m[n] for n in outputs]
    new_v = [big[n][3] if n in big else s_v[n] for n in outputs]
    return (loss, dx[None], *grads, *deltas, *new_m, *new_v)
```

```python
import functools

import jax
import jax.numpy as jnp
from jax import lax
from jax.experimental import pallas as pl
from jax.experimental.pallas import tpu as pltpu

F32 = jnp.float32
BF16 = jnp.bfloat16

N_DEV = 8
T = 2048
D = 2048
DEPTH = 2
HD = 64
HA, HB, HKV, HC = 12, 10, 2, 10
WA, WB, WKV, WC = HA * HD, HB * HD, HKV * HD, HC * HD
IN_COLS = 3 * WA + WB + 2 * WKV + 3 * WC
DFF = 5632
GRID_W = 64
ROWS = T // GRID_W
NA_ROWS, NA_COLS = 8, 16
WINDOW_B = 128
EPS = 1e-6
NEG = -1e30
ROPE_THETA = 10000.0
LANE = 128
VMEM_LIMIT = 56 * 1024 * 1024

ADAM_LR, ADAM_B1, ADAM_B2, ADAM_EPS, ADAM_WD, ADAM_STEP = 0.001, 0.9, 0.999, 1e-08, 0.01, 10

GROUPS = (("qa", WA, True, True), ("ka", WA, True, False), ("va", WA, False, False),
          ("qb", WB, True, True), ("kb", WKV, True, False), ("vb", WKV, False, False),
          ("qc", WC, False, True), ("kc", WC, False, False), ("vc", WC, False, False))


def _params(sem=None):
    return pltpu.CompilerParams(dimension_semantics=sem, vmem_limit_bytes=VMEM_LIMIT)


HBM_SPEC = pl.BlockSpec(memory_space=pltpu.HBM)
SEM_SPEC = pl.BlockSpec(memory_space=pltpu.SEMAPHORE)
DATAFLOW = pltpu.SideEffectType.DATAFLOW_SIDE_EFFECTING


ALL_PEERS = tuple((p >> 2 & 1, p >> 1 & 1, p & 1) for p in range(1, N_DEV))
OTHER_CHIPS = ((1, 0, 0), (0, 1, 0), (1, 1, 0))
NEAR_PEERS = ((0, 0, 1),) + OTHER_CHIPS


def _flip(x, y, c, f):
    return (1 - x if f[0] else x, 1 - y if f[1] else y, 1 - c if f[2] else c)


def _index(pos):
    return 4 * pos[0] + 2 * pos[1] + pos[2]


class _LocalCopy:
    def __init__(self, src, dst, sem):
        self.copy = pltpu.make_async_copy(src, dst, sem)

    def start(self):
        self.copy.start()

    def wait_send(self):
        self.copy.wait()

    def wait_recv(self):
        pass


def _descriptors(plan, bufs, send_sems, recv_sems):
    x, y, c = lax.axis_index("x"), lax.axis_index("y"), lax.axis_index("c")
    return [_LocalCopy(src, dst, send_sems[g].at[i]) if partner is None else
            pltpu.make_async_remote_copy(src_ref=src, dst_ref=dst, send_sem=send_sems[g].at[i],
                                         recv_sem=recv_sems[g].at[i], device_id=partner,
                                         device_id_type=pl.DeviceIdType.MESH)
            for g, copies in enumerate(plan(bufs, x, y, c)) for i, (src, dst, partner) in enumerate(copies)]


def _copy_start(name, bufs, plan, sizes):
    nb, ng = len(bufs), len(sizes)

    def body(*refs):
        for d in _descriptors(plan, refs[:nb], refs[nb:nb + ng], refs[nb + ng:nb + 2 * ng]):
            d.start()
        refs[2 * nb + 2 * ng][...] = jnp.zeros((8, LANE), F32)

    outs = pl.pallas_call(
        body, name=name,
        out_shape=[pltpu.SemaphoreType.DMA((s,)) for s in sizes] * 2 + [pltpu.HBM(b.shape, b.dtype) for b in bufs]
        + [jax.ShapeDtypeStruct((8, LANE), F32)],
        in_specs=[HBM_SPEC] * nb,
        out_specs=[SEM_SPEC] * (2 * ng) + [HBM_SPEC] * nb + [pl.BlockSpec(memory_space=pltpu.VMEM)],
        input_output_aliases={i: 2 * ng + i for i in range(nb)},
        compiler_params=pltpu.CompilerParams(has_side_effects=DATAFLOW),
    )(*[pltpu.with_memory_space_constraint(b, pltpu.HBM) for b in bufs])
    return outs[:ng], outs[ng:2 * ng], outs[2 * ng:2 * ng + nb], outs[2 * ng + nb]


def _copy_wait(name, bufs, send_sems, recv_sems, plan, after):
    nb, ng = len(bufs), len(send_sems)
    after = list(after) if isinstance(after, (list, tuple)) else [after]

    def body(*refs):
        for d in _descriptors(plan, refs[:nb], refs[nb:nb + ng], refs[nb + ng:nb + 2 * ng]):
            d.wait_send()
            d.wait_recv()

    return pl.pallas_call(
        body, name=name, out_shape=[pltpu.HBM(b.shape, b.dtype) for b in bufs],
        in_specs=[HBM_SPEC] * nb + [SEM_SPEC] * (2 * ng) + [pl.BlockSpec(memory_space=pl.ANY)] * len(after),
        out_specs=[HBM_SPEC] * nb, input_output_aliases={i: i for i in range(nb)},
        compiler_params=pltpu.CompilerParams(has_side_effects=DATAFLOW),
    )(*bufs, *send_sems, *recv_sems, *after)


def _lead_slot(ref, k):
    return ref.at[k]


def _col_slot(width):
    return lambda ref, k: ref.at[:, pl.ds(pl.multiple_of(k * width, LANE), width)]


def _gather_plan(peer_sets, slots):
    def plan(bufs, x, y, c):
        n = len(peer_sets)
        return [[(bufs[i], slots[i](bufs[n + i], _index((x, y, c))), _flip(x, y, c, f)) for f in peers]
                + [(bufs[i], slots[i](bufs[n + i], _index((x, y, c))), None)] for i, peers in enumerate(peer_sets)]
    return plan


def _forward_plan(slot):
    def plan(bufs, x, y, c):
        pieces = [slot(bufs[0], _index(_flip(x, y, c, f))) for f in OTHER_CHIPS]
        return [[(p, p, _flip(x, y, c, (0, 0, 1))) for p in pieces]]
    return plan


def _scatter_plan(slot):
    def plan(bufs, x, y, c):
        me = _index((x, y, c))
        peers = [_flip(x, y, c, f) for f in ALL_PEERS]
        return [[(slot(bufs[0], _index(p)), bufs[1].at[me], p) for p in peers]
                + [(slot(bufs[0], me), bufs[1].at[me], None)]]
    return plan


def _flat2(v):
    return v.reshape(-1, v.shape[-1])


def _matmul(name, kind, a, a_spec, b, b_spec, out_shape, out_spec, grid, res=None, res_spec=None, acc_shape=None):
    dims = {"nn": (((1,), (0,)), ((), ())), "nt": NT_DIMS, "tn": TN_DIMS}[kind]
    nred = grid[-1]

    def body(*refs):
        if res is None:
            a_ref, b_ref, o_ref = refs[:3]
            r_ref = None
        else:
            a_ref, b_ref, r_ref, o_ref = refs[:4]
        part = lax.dot_general(_flat2(a_ref[...]), _flat2(b_ref[...]), dims, preferred_element_type=F32)

        def finish(total):
            if r_ref is not None:
                total = total + r_ref[...]
            o_ref[...] = total.reshape(o_ref.shape).astype(o_ref.dtype)

        if nred == 1:
            finish(part)
        else:
            acc_ref = refs[-1]
            k = pl.program_id(len(grid) - 1)

            @pl.when(k == 0)
            def _():
                acc_ref[...] = part

            @pl.when(jnp.logical_and(k > 0, k < nred - 1))
            def _():
                acc_ref[...] += part

            @pl.when(k == nred - 1)
            def _():
                finish(acc_ref[...] + part)

    ins, specs = [a, b], [a_spec, b_spec]
    if res is not None:
        ins.append(res)
        specs.append(res_spec)
    scratch = [] if nred == 1 else [pltpu.VMEM(acc_shape, F32)]
    return pl.pallas_call(
        body, name=name, grid=grid, in_specs=specs, out_specs=out_spec, out_shape=out_shape, scratch_shapes=scratch,
        compiler_params=_params(("parallel",) * (len(grid) - 1) + ("arbitrary",)),
    )(*ins)


def _nn_rows(name, a, wg, res, s, tn, tm):
    _, kj, n = wg.shape
    return _matmul(
        name, "nn", a, pl.BlockSpec((tm, s * kj), lambda j, i, r: (i, r)),
        wg, pl.BlockSpec((s, kj, tn), lambda j, i, r: (r, 0, j)),
        jax.ShapeDtypeStruct((T, n), F32), pl.BlockSpec((tm, tn), lambda j, i, r: (i, j)),
        (n // tn, T // tm, N_DEV // s), res=res, res_spec=pl.BlockSpec((tm, tn), lambda j, i, r: (i, j)),
        acc_shape=(tm, tn))


def _nt_cols(name, dc, dc_spec_of, w, nc):
    k, n = w.shape
    tm = tk = 1024
    return _matmul(
        name, "nt", dc, dc_spec_of(tm, nc),
        w, pl.BlockSpec((tk, nc), lambda kt, i, j: (kt, j)),
        jax.ShapeDtypeStruct((T, k), F32), pl.BlockSpec((tm, tk), lambda kt, i, j: (i, kt)),
        (k // tk, T // tm, n // nc), acc_shape=(tm, tk))


def _nt_rows(name, dc, wg, s, tm):
    _, kj, n = wg.shape
    return _matmul(
        name, "nt", dc, pl.BlockSpec((tm, n), lambda kt, i, r: (i, 0)),
        wg, pl.BlockSpec((s, kj, n), lambda kt, i, r: (kt, 0, 0)),
        jax.ShapeDtypeStruct((T, N_DEV * kj), F32), pl.BlockSpec((tm, s * kj), lambda kt, i, r: (i, kt)),
        (N_DEV // s, T // tm, 1))


def _tn_cols(name, a, dc, dc_spec_of, n, tn):
    k = a.shape[1]
    tk = 1024
    return _matmul(
        name, "tn", a, pl.BlockSpec((T, tk), lambda j, kt, r: (0, kt)),
        dc, dc_spec_of(T, tn),
        jax.ShapeDtypeStruct((k, n), BF16), pl.BlockSpec((tk, tn), lambda j, kt, r: (kt, j)),
        (n // tn, k // tk, 1))


def _tn_rows(name, a, dc, kj, s, tn):
    n = dc.shape[1]
    return _matmul(
        name, "tn", a, pl.BlockSpec((T, s * kj), lambda kt, j, r: (0, kt)),
        dc, pl.BlockSpec((T, tn), lambda kt, j, r: (0, j)),
        jax.ShapeDtypeStruct((N_DEV, kj, n), BF16), pl.BlockSpec((s, kj, tn), lambda kt, j, r: (kt, 0, j)),
        (N_DEV // s, n // tn, 1))


TR = 512


def _rows(width):
    return pl.BlockSpec((TR, width), lambda i: (i, 0))


def _whole(shape):
    return pl.BlockSpec(shape, lambda i: (0,) * len(shape))


def _rmsnorm_rows(x_ref, g_ref):
    xv = x_ref[...]
    r = lax.rsqrt(jnp.mean(xv * xv, axis=-1, keepdims=True) + EPS)
    return ((xv * r) * g_ref[...]).astype(BF16)


SUB = 256


def _prologue_matmul(name, prologue, ins, widths, w, w_block, w_index, tn, res=None, epilogue=None, extras=(),
                     out_dtype=F32, blocked_out=False):
    tm = 1024
    n = w.shape[-1]
    ni = len(ins)

    def body(*refs):
        w_ref = refs[ni]
        r_ref = refs[ni + 1] if res is not None else None
        x_refs = refs[ni + 1 + (res is not None):len(refs) - 3]
        h_ref, o_ref, h_scr = refs[-3:]

        @pl.when(pl.program_id(1) == 0)
        def _():
            h = prologue(*refs[:ni])
            h_scr[...] = h
            h_ref[...] = h

        for sub in range(tn // SUB):
            cols = slice(sub * SUB, (sub + 1) * SUB)
            w_cols = w_ref[(slice(None),) * (len(w_ref.shape) - 1) + (cols,)]
            part = jnp.dot(h_scr[...], _flat2(w_cols), preferred_element_type=F32)
            if r_ref is not None:
                part = part + r_ref[:, cols]
            if epilogue is not None:
                part = epilogue(pl.program_id(1) * (tn // SUB) + sub, part, *x_refs)
            if blocked_out:
                for b in range(SUB // LANE):
                    o_ref[sub * (SUB // LANE) + b] = part[:, b * LANE:(b + 1) * LANE].astype(out_dtype)
            else:
                o_ref[:, cols] = part.astype(out_dtype)

    tile = pl.BlockSpec((tm, tn), lambda i, j: (i, j))
    out_tile = pl.BlockSpec((tn // LANE, tm, LANE), lambda i, j: (j, i, 0)) if blocked_out else tile
    out_full = (n // LANE, T, LANE) if blocked_out else (T, n)
    specs = [pl.BlockSpec((1, D), lambda i, j: (0, 0)) if wd is None else pl.BlockSpec((tm, wd), lambda i, j: (i, 0))
             for wd in widths]
    specs.append(pl.BlockSpec(w_block, lambda i, j: w_index(j)))
    operands = list(ins) + [w]
    if res is not None:
        specs.append(tile)
        operands.append(res)
    specs += [pl.BlockSpec((tm, LANE), lambda i, j: (i, 0))] * len(extras)
    operands += list(extras)
    return pl.pallas_call(
        body, name=name, grid=(T // tm, n // tn), in_specs=specs,
        out_specs=[pl.BlockSpec((tm, D), lambda i, j: (i, 0)), out_tile],
        out_shape=[jax.ShapeDtypeStruct((T, D), BF16), jax.ShapeDtypeStruct(out_full, out_dtype)],
        scratch_shapes=[pltpu.VMEM((tm, D), BF16)], compiler_params=_params(("parallel", "arbitrary")),
    )(*operands)


def _rms_bwd_math(dy, xv, g):
    r = lax.rsqrt(jnp.mean(xv * xv, axis=-1, keepdims=True) + EPS)
    xhat = xv * r
    dxhat = dy * g
    dx = r * (dxhat - xhat * jnp.mean(dxhat * xhat, axis=-1, keepdims=True))
    return dx, dy * xhat


def _accumulate(ref, val):
    @pl.when(pl.program_id(0) == 0)
    def _():
        ref[...] = val

    @pl.when(pl.program_id(0) > 0)
    def _():
        ref[...] += val


def _rmsnorm_bwd(name, dy, x, g, res):
    def body(dy_ref, x_ref, g_ref, res_ref, dx_ref, dxb_ref, dg_ref):
        dx, dgr = _rms_bwd_math(dy_ref[...], x_ref[...], g_ref[...])
        tot = res_ref[...] + dx
        dx_ref[...] = tot
        dxb_ref[...] = tot.astype(BF16)
        _accumulate(dg_ref, jnp.sum(dgr, axis=0, keepdims=True))

    return pl.pallas_call(
        body, name=name, grid=(T // TR,), in_specs=[_rows(D), _rows(D), _whole((1, D)), _rows(D)],
        out_specs=[_rows(D), _rows(D), _whole((1, D))],
        out_shape=[jax.ShapeDtypeStruct((T, D), F32), jax.ShapeDtypeStruct((T, D), BF16),
                   jax.ShapeDtypeStruct((1, D), F32)],
        compiler_params=_params(("arbitrary",)),
    )(dy, x, g, res)


def _loss_head(x, g, target):
    def body(x_ref, g_ref, t_ref, loss_ref, dx_ref, dxb_ref, dg_ref):
        xv, gv = x_ref[...], g_ref[...]
        r = lax.rsqrt(jnp.mean(xv * xv, axis=-1, keepdims=True) + EPS)
        err = (xv * r) * gv - t_ref[...]
        part = 0.5 * jnp.sum(jnp.mean(err * err, axis=-1, keepdims=True))
        dx, dgr = _rms_bwd_math(err * (1.0 / D), xv, gv)
        dx_ref[...] = dx
        dxb_ref[...] = dx.astype(BF16)
        _accumulate(dg_ref, jnp.sum(dgr, axis=0, keepdims=True))
        _accumulate(loss_ref, jnp.full((8, LANE), part, F32))

    return pl.pallas_call(
        body, name="loss_head", grid=(T // TR,), in_specs=[_rows(D), _whole((1, D)), _rows(D)],
        out_specs=[_whole((8, LANE)), _rows(D), _rows(D), _whole((1, D))],
        out_shape=[jax.ShapeDtypeStruct((8, LANE), F32), jax.ShapeDtypeStruct((T, D), F32),
                   jax.ShapeDtypeStruct((T, D), BF16), jax.ShapeDtypeStruct((1, D), F32)],
        compiler_params=_params(("arbitrary",)),
    )(x, g, target)


MIX_OFFS = ((0, WA), (WA, WB), (WA + WB, WC))


def _mix_rows(oa_ref, ob_ref, oc_ref, g_ref):
    parts = []
    for ref, (off, w) in zip((oa_ref, ob_ref, oc_ref), MIX_OFFS):
        o = ref[...]
        r = lax.rsqrt(jnp.mean(o * o, axis=-1, keepdims=True) + EPS)
        parts.append(((o * r) * g_ref[:, off:off + w]).astype(BF16))
    return jnp.concatenate(parts, axis=1)


def _mix_bwd(name, dmixed, oa, ob, oc, gain):
    def body(dm_ref, oa_ref, ob_ref, oc_ref, g_ref, doa_ref, dob_ref, doc_ref, dg_ref):
        dgs = []
        for ref, dref, (off, w) in zip((oa_ref, ob_ref, oc_ref), (doa_ref, dob_ref, doc_ref), MIX_OFFS):
            dx, dgr = _rms_bwd_math(dm_ref[:, off:off + w], ref[...], g_ref[:, off:off + w])
            dref[...] = dx
            dgs.append(jnp.sum(dgr, axis=0, keepdims=True))
        _accumulate(dg_ref, jnp.concatenate(dgs, axis=1))

    return pl.pallas_call(
        body, name=name, grid=(T // TR,),
        in_specs=[_rows(D), _rows(WA), _rows(WB), _rows(WC), _whole((1, D))],
        out_specs=[_rows(WA), _rows(WB), _rows(WC), _whole((1, D))],
        out_shape=[jax.ShapeDtypeStruct((T, WA), F32), jax.ShapeDtypeStruct((T, WB), F32),
                   jax.ShapeDtypeStruct((T, WC), F32), jax.ShapeDtypeStruct((1, D), F32)],
        compiler_params=_params(("arbitrary",)),
    )(dmixed, oa, ob, oc, gain)


def _rope_tables():
    inv_freq = ROPE_THETA ** (-jnp.arange(0, HD, 2, dtype=F32) / HD)
    ang = jnp.arange(T, dtype=F32)[:, None] * inv_freq[None, :]
    cos, sin = jnp.cos(ang), jnp.sin(ang)
    cos2 = jnp.tile(jnp.concatenate([cos, cos], axis=1), (1, LANE // HD))
    sin2 = jnp.tile(jnp.concatenate([-sin, sin], axis=1), (1, LANE // HD))
    return cos2, sin2


def _rot_half(v):
    lane = lax.broadcasted_iota(jnp.int32, v.shape, 1)
    return jnp.where(lane % HD < HD // 2, pltpu.roll(v, LANE - HD // 2, 1), pltpu.roll(v, HD // 2, 1))


BLOCK_KINDS = tuple((rot, is_q) for _, w, rot, is_q in GROUPS for _ in range(w // LANE))
BLOCK_OF = {name: sum(w for _, w, _, _ in GROUPS[:g]) // LANE for g, (name, _, _, _) in enumerate(GROUPS)}


def _any_tile(j, tiles):
    return functools.reduce(jnp.logical_or, [j == t for t in tiles]) if tiles else False


def _rope_epilogue(j, tile, c_ref, s_ref):
    cv, sv = c_ref[...], s_ref[...]
    per, n_tiles = tile.shape[1] // LANE, IN_COLS // tile.shape[1]
    out = []
    for b in range(per):
        v = tile[:, b * LANE:(b + 1) * LANE]
        rot = _any_tile(j, [t for t in range(n_tiles) if BLOCK_KINDS[t * per + b][0]])
        is_q = _any_tile(j, [t for t in range(n_tiles) if BLOCK_KINDS[t * per + b][1]])
        if rot is not False:
            v = jnp.where(rot, v * cv + _rot_half(v) * sv, v)
        if is_q is not False:
            v = v * jnp.where(is_q, HD ** -0.5, 1.0)
        out.append(v)
    return jnp.concatenate(out, axis=1)


def _rope_bwd(name, grads, cos2, sin2):
    def body(*refs):
        ins, (c_ref, s_ref, o_ref) = refs[:9], refs[9:]
        cv, sv = c_ref[...], s_ref[...]
        off = 0
        for d_ref, (_, w, rot, is_q) in zip(ins, GROUPS):
            for b in range(w // LANE):
                v = d_ref[:, b * LANE:(b + 1) * LANE]
                if is_q:
                    v = v * (HD ** -0.5)
                if rot:
                    v = v * cv + _rot_half(v * sv)
                o_ref[:, off + b * LANE:off + (b + 1) * LANE] = v.astype(BF16)
            off += w

    return pl.pallas_call(
        body, name=name, grid=(T // TR,), in_specs=[_rows(w) for _, w, _, _ in GROUPS] + [_rows(LANE), _rows(LANE)],
        out_specs=_rows(IN_COLS), out_shape=jax.ShapeDtypeStruct((T, IN_COLS), BF16),
        compiler_params=_params(("parallel",)),
    )(*grads, cos2, sin2)


NT_DIMS = (((1,), (1,)), ((), ()))
TN_DIMS = (((0,), (0,)), ((), ()))


def _scores(q, k, bias, valid):
    s = lax.dot_general(q, k, NT_DIMS, preferred_element_type=F32)
    if bias is not None:
        s = s + bias
    if valid is not None:
        s = jnp.where(valid, s, NEG)
    return s


def _heads_fwd(heads):
    scores = [_scores(h["q"], h["k"], h.get("bias"), h.get("valid")) for h in heads]
    soft = []
    for s, h in zip(scores, heads):
        m = jnp.max(s, axis=1, keepdims=True)
        e = jnp.exp(s - m)
        l = jnp.sum(e, axis=1, keepdims=True)
        if h.get("sink") is not None:
            l = l + jnp.exp(h["sink"] - m)
        soft.append((e.astype(BF16), l, m + jnp.log(l)))
    return [(jnp.dot(e, h["v"], preferred_element_type=F32) / l, lse) for (e, l, lse), h in zip(soft, heads)]


def _heads_bwd(heads):
    dobs = [h["do"].astype(BF16) for h in heads]
    scores = [_scores(h["q"], h["k"], h.get("bias"), h.get("valid")) for h in heads]
    dps = [lax.dot_general(dob, h["v"], NT_DIMS, preferred_element_type=F32) for dob, h in zip(dobs, heads)]
    mid = []
    for s, dp, h in zip(scores, dps, heads):
        p = jnp.exp(s - h["lse"])
        delta = jnp.sum(h["do"] * h["o"], axis=1, keepdims=True)
        ds = p * (dp - delta)
        dsink = None if h.get("sink") is None else -jnp.exp(h["sink"] - h["lse"]) * delta
        mid.append((p.astype(BF16), ds, dsink))
    out = []
    for (pb, ds, dsink), dob, h in zip(mid, dobs, heads):
        dsb = ds.astype(BF16)
        out.append((jnp.dot(dsb, h["k"], preferred_element_type=F32),
                    lax.dot_general(dsb, h["q"], TN_DIMS, preferred_element_type=F32),
                    lax.dot_general(pb, dob, TN_DIMS, preferred_element_type=F32), ds, dsink))
    return out


def _per_head(cols):
    return jnp.concatenate([jnp.broadcast_to(c, (c.shape[0], HD)) for c in cols], axis=1)


DILATIONS = ((128, 1), (512, 4), (2048, 16))


BQ_A = 256
REACH_A = max(window // 2 for window, _ in DILATIONS)


def _first_key(i):
    return jnp.maximum(i * BQ_A - REACH_A, 0)


def _key_window_groups():
    groups = {}
    for i in range(T // BQ_A):
        width = min(T, (i + 1) * BQ_A + REACH_A) - max(i * BQ_A - REACH_A, 0)
        groups.setdefault(width, []).append(i)
    return groups


def _per_window(i, fn):
    for width, tiles in _key_window_groups().items():
        hit = functools.reduce(jnp.logical_or, [i == t for t in tiles])
        pl.when(hit)(functools.partial(fn, pl.multiple_of(_first_key(i), BQ_A), width))


def _dilation_bias():
    def body(o_ref):
        i = pl.program_id(0)
        t = i * BQ_A + lax.broadcasted_iota(jnp.int32, (BQ_A, T), 0)
        ad = jnp.abs(t - (_first_key(i) + lax.broadcasted_iota(jnp.int32, (BQ_A, T), 1)))
        count = jnp.zeros((BQ_A, T), jnp.int32)
        for window, r in DILATIONS:
            count += jnp.where(((ad & (r - 1)) == 0) & (ad <= window // 2), 1, 0)
        logs = jnp.where(count == 2, jnp.log(2.0), jnp.where(count == 3, jnp.log(3.0), 0.0)).astype(F32)
        o_ref[...] = jnp.where(count == 0, NEG, logs)

    return pl.pallas_call(
        body, name="dilation_bias", grid=(T // BQ_A,), out_specs=pl.BlockSpec((BQ_A, T), lambda i: (i, 0)),
        out_shape=jax.ShapeDtypeStruct((T, T), F32), compiler_params=_params(("parallel",)),
    )()


def _qkv_rows(rows, group):
    return pl.BlockSpec((rows, LANE), lambda p, i: (i, BLOCK_OF[group] + p))


def _qkv_all(group):
    return pl.BlockSpec((T, LANE), lambda p, i: (0, BLOCK_OF[group] + p))


def _attn_a_fwd(name, qkv, bias):
    def body(q_ref, k_ref, v_ref, b_ref, o_ref, lse_ref):
        def tile(first, width):
            b = b_ref[:, :width]
            outs = _heads_fwd([dict(q=q_ref[:, h * HD:(h + 1) * HD], k=k_ref[pl.ds(first, width), h * HD:(h + 1) * HD],
                                    v=v_ref[pl.ds(first, width), h * HD:(h + 1) * HD], bias=b) for h in range(2)])
            o_ref[...] = jnp.concatenate([o for o, _ in outs], axis=1)
            lse_ref[...] = _per_head([lse for _, lse in outs])

        _per_window(pl.program_id(1), tile)

    qs = pl.BlockSpec((BQ_A, LANE), lambda p, i: (i, p))
    ks = pl.BlockSpec((T, LANE), lambda p, i: (0, p))
    return pl.pallas_call(
        body, name=name, grid=(HA // 2, T // BQ_A),
        in_specs=[_qkv_rows(BQ_A, "qa"), _qkv_all("ka"), _qkv_all("va"), pl.BlockSpec((BQ_A, T), lambda p, i: (i, 0))],
        out_specs=[qs, qs],
        out_shape=[jax.ShapeDtypeStruct((T, WA), F32)] * 2, compiler_params=_params(("parallel", "parallel")),
    )(qkv, qkv, qkv, bias)


def _attn_a_bwd(name, qkv, oa, lse, doa, bias):
    def body(q_ref, k_ref, v_ref, o_ref, lse_ref, do_ref, b_ref, dq_ref, dk_ref, dv_ref):
        @pl.when(pl.program_id(1) == 0)
        def _():
            dk_ref[...] = jnp.zeros_like(dk_ref)
            dv_ref[...] = jnp.zeros_like(dv_ref)

        def tile(first, width):
            b = b_ref[:, :width]
            keys = pl.ds(first, width)
            sls = [slice(h * HD, (h + 1) * HD) for h in range(2)]
            res = _heads_bwd([dict(q=q_ref[:, sl], k=k_ref[keys, sl], v=v_ref[keys, sl], o=o_ref[:, sl],
                                   do=do_ref[:, sl], lse=lse_ref[:, sl.start:sl.start + 1], bias=b) for sl in sls])
            dq_ref[...] = jnp.concatenate([r[0] for r in res], axis=1)
            dk_ref[keys, :] += jnp.concatenate([r[1] for r in res], axis=1)
            dv_ref[keys, :] += jnp.concatenate([r[2] for r in res], axis=1)

        _per_window(pl.program_id(1), tile)

    qs = pl.BlockSpec((BQ_A, LANE), lambda p, i: (i, p))
    ks = pl.BlockSpec((T, LANE), lambda p, i: (0, p))
    return pl.pallas_call(
        body, name=name, grid=(HA // 2, T // BQ_A),
        in_specs=[_qkv_rows(BQ_A, "qa"), _qkv_all("ka"), _qkv_all("va"), qs, qs, qs,
                  pl.BlockSpec((BQ_A, T), lambda p, i: (i, 0))], out_specs=[qs, ks, ks],
        out_shape=[jax.ShapeDtypeStruct((T, WA), F32)] * 3, compiler_params=_params(("parallel", "arbitrary")),
    )(qkv, qkv, qkv, oa, lse, doa, bias)


BQ_B = 128
SPAN_B = BQ_B + 2 * WINDOW_B


def _window_b(i):
    start = pl.multiple_of(jnp.clip(i * BQ_B - WINDOW_B, 0, T - SPAN_B), BQ_B)
    qpos = i * BQ_B + lax.broadcasted_iota(jnp.int32, (BQ_B, SPAN_B), 0)
    kpos = start + lax.broadcasted_iota(jnp.int32, (BQ_B, SPAN_B), 1)
    return start, jnp.abs(qpos - kpos) <= WINDOW_B


GROUP_B = HB // HKV


def _stack_group(ref, g):
    return jnp.concatenate([ref[:, h * HD:(h + 1) * HD] for h in range(g * GROUP_B, (g + 1) * GROUP_B)], axis=0)


def _sink_column(sink_ref, g):
    return jnp.concatenate([jnp.full((BQ_B, 1), sink_ref[h], F32) for h in range(g * GROUP_B, (g + 1) * GROUP_B)],
                           axis=0)


def _unstack(stacked):
    return [s[j * BQ_B:(j + 1) * BQ_B] for s in stacked for j in range(GROUP_B)]


def _attn_b_fwd(name, qb, kb, vb, sink):
    def body(sink_ref, q_ref, k_ref, v_ref, o_ref, lse_ref):
        start, valid = _window_b(pl.program_id(0))
        valid = jnp.concatenate([valid] * GROUP_B, axis=0)
        kw, vw = k_ref[pl.ds(start, SPAN_B), :], v_ref[pl.ds(start, SPAN_B), :]
        outs = _heads_fwd([dict(q=_stack_group(q_ref, g), k=kw[:, g * HD:(g + 1) * HD], v=vw[:, g * HD:(g + 1) * HD],
                                valid=valid, sink=_sink_column(sink_ref, g)) for g in range(HKV)])
        o_ref[...] = jnp.concatenate(_unstack([o for o, _ in outs]), axis=1)
        lse_ref[...] = _per_head(_unstack([lse for _, lse in outs]))

    qs = pl.BlockSpec((BQ_B, WB), lambda i: (i, 0))
    return pl.pallas_call(
        body, name=name, grid=(T // BQ_B,),
        in_specs=[pl.BlockSpec(memory_space=pltpu.SMEM), qs, _whole((T, WKV)), _whole((T, WKV))],
        out_specs=[qs, qs],
        out_shape=[jax.ShapeDtypeStruct((T, WB), F32)] * 2, compiler_params=_params(("parallel",)),
    )(sink, qb, kb, vb)


def _attn_b_bwd(name, qb, kb, vb, ob, lse, dob, sink):
    def body(sink_ref, q_ref, k_ref, v_ref, o_ref, lse_ref, do_ref, dq_ref, dk_ref, dv_ref, dsink_ref):
        i = pl.program_id(0)
        start, valid = _window_b(i)
        valid = jnp.concatenate([valid] * GROUP_B, axis=0)
        kw, vw = k_ref[pl.ds(start, SPAN_B), :], v_ref[pl.ds(start, SPAN_B), :]
        res = _heads_bwd([dict(q=_stack_group(q_ref, g), k=kw[:, g * HD:(g + 1) * HD], v=vw[:, g * HD:(g + 1) * HD],
                               o=_stack_group(o_ref, g), do=_stack_group(do_ref, g),
                               lse=jnp.concatenate([lse_ref[:, h * HD:h * HD + 1]
                                                    for h in range(g * GROUP_B, (g + 1) * GROUP_B)], axis=0),
                               valid=valid, sink=_sink_column(sink_ref, g)) for g in range(HKV)])
        dks, dvs = [r[1] for r in res], [r[2] for r in res]
        lane = lax.broadcasted_iota(jnp.int32, (1, LANE), 1)
        dsink = jnp.zeros((1, LANE), F32)
        for h, rows in enumerate(_unstack([r[4] for r in res])):
            dsink += jnp.where(lane == h, jnp.sum(rows), 0.0)
        dq_ref[...] = jnp.concatenate(_unstack([r[0] for r in res]), axis=1)

        @pl.when(i == 0)
        def _():
            dk_ref[...] = jnp.zeros_like(dk_ref)
            dv_ref[...] = jnp.zeros_like(dv_ref)
            dsink_ref[...] = jnp.zeros_like(dsink_ref)

        dk_ref[pl.ds(start, SPAN_B), :] += jnp.concatenate(dks, axis=1)
        dv_ref[pl.ds(start, SPAN_B), :] += jnp.concatenate(dvs, axis=1)
        dsink_ref[...] += dsink

    qs = pl.BlockSpec((BQ_B, WB), lambda i: (i, 0))
    return pl.pallas_call(
        body, name=name, grid=(T // BQ_B,),
        in_specs=[pl.BlockSpec(memory_space=pltpu.SMEM), qs, _whole((T, WKV)), _whole((T, WKV)), qs, qs, qs],
        out_specs=[qs, _whole((T, WKV)), _whole((T, WKV)), _whole((1, LANE))],
        out_shape=[jax.ShapeDtypeStruct((T, WB), F32), jax.ShapeDtypeStruct((T, WKV), F32),
                   jax.ShapeDtypeStruct((T, WKV), F32), jax.ShapeDtypeStruct((1, LANE), F32)],
        compiler_params=_params(("arbitrary",)),
    )(sink, qb, kb, vb, ob, lse, dob)


SPAN_C = NA_ROWS * GRID_W


def _row_start(r):
    return jnp.clip(r - NA_ROWS // 2, 0, ROWS - NA_ROWS)


def _off_index(r):
    return _row_start(r) - r + (NA_ROWS - 1)


N_TAB = 16
RPS_FWD, RPS_BWD = 4, 8


def _rpb_tables(name, rpb):
    circ = jnp.concatenate([rpb[..., NA_COLS - 1:], jnp.zeros(rpb.shape[:2] + (LANE - (2 * NA_COLS - 1),), F32),
                            rpb[..., :NA_COLS - 1]], axis=-1)
    circ = jnp.pad(circ, ((0, 0), (0, N_TAB + 1 - circ.shape[1]), (0, 0)))

    def body(w_ref, o_ref):
        c = lax.broadcasted_iota(jnp.int32, (GRID_W, LANE), 0)
        lane = lax.broadcasted_iota(jnp.int32, (GRID_W, LANE), 1)
        cs = jnp.clip(c - NA_COLS // 2, 0, GRID_W - NA_COLS)
        valid = (lane % GRID_W >= cs) & (lane % GRID_W < cs + NA_COLS)
        toep = [pltpu.roll(jnp.broadcast_to(w_ref[a:a + 1, :], (GRID_W, LANE)), 0, 1, stride=1, stride_axis=0)
                for a in range(N_TAB + 1)]
        for a in range(N_TAB):
            pair = jnp.where(lane < GRID_W, toep[a], pltpu.roll(toep[a + 1], GRID_W, 1))
            o_ref[a] = jnp.where(valid, pair, NEG)

    return pl.pallas_call(
        body, name=name, grid=(HC,),
        in_specs=[pl.BlockSpec((None, N_TAB + 1, LANE), lambda h: (h, 0, 0))],
        out_specs=pl.BlockSpec((None, N_TAB, GRID_W, LANE), lambda h: (h, 0, 0, 0)),
        out_shape=jax.ShapeDtypeStruct((HC, N_TAB, GRID_W, LANE), F32), compiler_params=_params(("parallel",)),
    )(circ)


def _bias_c(t_ref, h, d):
    return jnp.concatenate([t_ref[h, d + k] for k in range(0, NA_ROWS, 2)], axis=1)


def _attn_c_fwd(name, qkv, tables):
    RPS = RPS_FWD

    def body(q_ref, k_ref, v_ref, t_ref, o_ref, lse_ref):
        heads = []
        for rr in range(RPS):
            r = pl.program_id(1) * RPS + rr
            rows = slice(rr * GRID_W, (rr + 1) * GRID_W)
            start = pl.multiple_of(_row_start(r) * GRID_W, GRID_W)
            kw, vw = k_ref[pl.ds(start, SPAN_C), :], v_ref[pl.ds(start, SPAN_C), :]
            heads += [dict(q=q_ref[rows, h * HD:(h + 1) * HD], k=kw[:, h * HD:(h + 1) * HD], v=vw[:, h * HD:(h + 1) * HD],
                           bias=_bias_c(t_ref, h, _off_index(r))) for h in range(2)]
        outs = _heads_fwd(heads)
        for rr in range(RPS):
            rows = slice(rr * GRID_W, (rr + 1) * GRID_W)
            o_ref[rows, :] = jnp.concatenate([o for o, _ in outs[2 * rr:2 * rr + 2]], axis=1)
            lse_ref[rows, :] = _per_head([lse for _, lse in outs[2 * rr:2 * rr + 2]])

    qs = pl.BlockSpec((RPS * GRID_W, LANE), lambda p, r: (r, p))
    ks = pl.BlockSpec((T, LANE), lambda p, r: (0, p))
    ts = pl.BlockSpec((2, N_TAB, GRID_W, LANE), lambda p, r: (p, 0, 0, 0))
    return pl.pallas_call(
        body, name=name, grid=(HC // 2, ROWS // RPS),
        in_specs=[_qkv_rows(RPS * GRID_W, "qc"), _qkv_all("kc"), _qkv_all("vc"), ts], out_specs=[qs, qs],
        out_shape=[jax.ShapeDtypeStruct((T, WC), F32)] * 2, compiler_params=_params(("parallel", "parallel")),
    )(qkv, qkv, qkv, tables)


def _attn_c_bwd(name, qkv, oc, lse, doc, tables):
    RPS = RPS_BWD

    def body(q_ref, k_ref, v_ref, o_ref, lse_ref, do_ref, t_ref, dq_ref, dk_ref, dv_ref, dt_ref):
        @pl.when(pl.program_id(1) == 0)
        def _():
            dk_ref[...] = jnp.zeros_like(dk_ref)
            dv_ref[...] = jnp.zeros_like(dv_ref)
            dt_ref[...] = jnp.zeros_like(dt_ref)

        heads, where = [], []
        for rr in range(RPS):
            r = pl.program_id(1) * RPS + rr
            rows = slice(rr * GRID_W, (rr + 1) * GRID_W)
            d = _off_index(r)
            start = pl.multiple_of(_row_start(r) * GRID_W, GRID_W)
            kw, vw = k_ref[pl.ds(start, SPAN_C), :], v_ref[pl.ds(start, SPAN_C), :]
            where.append((rows, d, start))
            for h in range(2):
                sl = slice(h * HD, (h + 1) * HD)
                heads.append(dict(q=q_ref[rows, sl], k=kw[:, sl], v=vw[:, sl], o=o_ref[rows, sl], do=do_ref[rows, sl],
                                  lse=lse_ref[rows, h * HD:h * HD + 1], bias=_bias_c(t_ref, h, d)))
        res = _heads_bwd(heads)
        for rr, (rows, d, start) in enumerate(where):
            pair = res[2 * rr:2 * rr + 2]
            for h in range(2):
                for k in range(0, NA_ROWS, 2):
                    dt_ref[h, d + k] += pair[h][3][:, k * GRID_W:(k + 2) * GRID_W]
            dq_ref[rows, :] = jnp.concatenate([p[0] for p in pair], axis=1)
            dk_ref[pl.ds(start, SPAN_C), :] += jnp.concatenate([p[1] for p in pair], axis=1)
            dv_ref[pl.ds(start, SPAN_C), :] += jnp.concatenate([p[2] for p in pair], axis=1)

    qs = pl.BlockSpec((RPS * GRID_W, LANE), lambda p, r: (r, p))
    ks = pl.BlockSpec((T, LANE), lambda p, r: (0, p))
    ts = pl.BlockSpec((2, N_TAB, GRID_W, LANE), lambda p, r: (p, 0, 0, 0))
    return pl.pallas_call(
        body, name=name, grid=(HC // 2, ROWS // RPS),
        in_specs=[_qkv_rows(RPS * GRID_W, "qc"), _qkv_all("kc"), _qkv_all("vc"), qs, qs, qs, ts],
        out_specs=[qs, ks, ks, ts],
        out_shape=[jax.ShapeDtypeStruct((T, WC), F32)] * 3 + [jax.ShapeDtypeStruct((HC, N_TAB, GRID_W, LANE), F32)],
        compiler_params=_params(("parallel", "arbitrary")),
    )(qkv, qkv, qkv, oc, lse, doc, tables)


def _split3(v):
    hi = v.astype(BF16)
    r1 = v - hi.astype(F32)
    mid = r1.astype(BF16)
    lo = (r1 - mid.astype(F32)).astype(BF16)
    return hi, mid, lo


def _rpb_reduce(name, dtables):
    x = dtables.reshape(HC, N_TAB, GRID_W * LANE)
    c = jnp.arange(GRID_W)[:, None]
    lane = jnp.arange(LANE)[None, :]
    col = (lane // GRID_W) * LANE + jnp.clip(lane % GRID_W - c + (NA_COLS - 1), 0, 2 * NA_COLS - 2)
    col_onehot = (col.reshape(-1)[:, None] == jnp.arange(2 * LANE)[None, :]).astype(BF16)
    a2 = jnp.arange(N_TAB)[None, :]
    row_onehot = jnp.concatenate([(jnp.arange(16)[:, None] == a2 + u) & (a2 < 2 * NA_ROWS - 2) for u in range(2)],
                                 axis=1).astype(BF16)

    def body(x_ref, e_ref, f_ref, o_ref):
        y = sum(jnp.dot(part, e_ref[...], preferred_element_type=F32) for part in _split3(x_ref[...]))
        z = jnp.concatenate([y[:, :LANE], y[:, LANE:]], axis=0)
        o_ref[...] = sum(jnp.dot(f_ref[...], part, preferred_element_type=F32) for part in _split3(z))

    out = pl.pallas_call(
        body, name=name, grid=(HC,),
        in_specs=[pl.BlockSpec((None, N_TAB, GRID_W * LANE), lambda h: (h, 0, 0)),
                  _whole((GRID_W * LANE, 2 * LANE)), _whole((16, 2 * N_TAB))],
        out_specs=pl.BlockSpec((None, 16, LANE), lambda h: (h, 0, 0)),
        out_shape=jax.ShapeDtypeStruct((HC, 16, LANE), F32), compiler_params=_params(("parallel",)),
    )(x, col_onehot, row_onehot)
    return out[:, :2 * NA_ROWS - 1, :2 * NA_COLS - 1]


TC = 128
CHUNK = 128
MARGIN = 8


def _shift_down(v, rows):
    return jnp.where(rows == 0, 0.0, pltpu.roll(v, 1, 0))


def _shift_up(v, rows):
    return jnp.where(rows == T - 1, 0.0, pltpu.roll(v, T - 1, 0))


def _conv(v, w, b, rows):
    return _shift_down(v, rows) * w[0:1] + v * w[1:2] + _shift_up(v, rows) * w[2:3] + b


FWD_BLOCKS = 4
BWD_BLOCKS = 1


def _ffn_mid_fwd(name, up, conv_w, conv_b):
    wide = FWD_BLOCKS * TC

    def body(xg_ref, xv_ref, wg_ref, wv_ref, bg_ref, bv_ref, o_ref):
        rows = lax.broadcasted_iota(jnp.int32, (T, TC), 0)
        for b in range(FWD_BLOCKS):
            lanes = slice(b * TC, (b + 1) * TC)
            ug = _conv(xg_ref[b], wg_ref[:, lanes], bg_ref[:, lanes], rows)
            uv = _conv(xv_ref[b], wv_ref[:, lanes], bv_ref[:, lanes], rows)
            o_ref[:, lanes] = (ug * jax.nn.sigmoid(ug) * uv).astype(BF16)

    gate = lambda shape: pl.BlockSpec(shape, lambda j: (0, j))
    val = lambda shape: pl.BlockSpec(shape, lambda j: (0, j + DFF // wide))
    return pl.pallas_call(
        body, name=name, grid=(DFF // wide,),
        in_specs=[pl.BlockSpec((FWD_BLOCKS, T, TC), lambda j: (j, 0, 0)),
                  pl.BlockSpec((FWD_BLOCKS, T, TC), lambda j: (j + DFF // wide, 0, 0)),
                  gate((3, wide)), val((3, wide)), gate((1, wide)), val((1, wide))],
        out_specs=pl.BlockSpec((T, wide), lambda j: (0, j)),
        out_shape=jax.ShapeDtypeStruct((T, DFF), BF16), compiler_params=_params(("parallel",)),
    )(up, up, conv_w, conv_w, conv_b, conv_b)


def _ffn_mid_bwd(name, dact, up, conv_w, conv_b):
    window = CHUNK + 2 * MARGIN
    centre = slice(MARGIN, MARGIN + CHUNK)

    def shifted(v):
        return pltpu.roll(v, 1, 0), pltpu.roll(v, window - 1, 0)

    def fold(v):
        return jnp.sum(v[centre].reshape(CHUNK // 8, 8, TC), axis=0)

    wide = BWD_BLOCKS * TC

    def body(da_ref, xg_ref, xv_ref, wg_ref, wv_ref, bg_ref, bv_ref, dx_ref, dw_ref, db_ref):
        for b in range(BWD_BLOCKS):
            block(b, da_ref, xg_ref, xv_ref, wg_ref, wv_ref, bg_ref, bv_ref, dx_ref, dw_ref, db_ref)

    def block(b, da_ref, xg_ref, xv_ref, wg_ref, wv_ref, bg_ref, bv_ref, dx_ref, dw_ref, db_ref):
        lanes = slice(b * TC, (b + 1) * TC)
        wg, wv, bg, bv = wg_ref[:, lanes], wv_ref[:, lanes], bg_ref[:, lanes], bv_ref[:, lanes]
        margin = jnp.zeros((MARGIN, TC), F32)
        last = T // CHUNK - 1

        def windows(c):
            if isinstance(c, int) and c == 0:
                rows = slice(0, CHUNK + MARGIN)
                return [jnp.concatenate([margin, v], axis=0)
                        for v in (da_ref[rows, lanes], xg_ref[b, rows, :], xv_ref[b, rows, :])]
            if isinstance(c, int) and c == last:
                rows = slice(T - CHUNK - MARGIN, T)
                return [jnp.concatenate([v, margin], axis=0)
                        for v in (da_ref[rows, lanes], xg_ref[b, rows, :], xv_ref[b, rows, :])]
            rows = pl.ds(pl.multiple_of(c * CHUNK - MARGIN, MARGIN), window)
            return [da_ref[rows, lanes], xg_ref[b, rows, :], xv_ref[b, rows, :]]

        def chunk(c, sums):
            r0 = c * CHUNK if isinstance(c, int) else pl.multiple_of(c * CHUNK, CHUNK)
            da, xg, xv = windows(c)
            xg_prev, xg_next = shifted(xg)
            xv_prev, xv_next = shifted(xv)
            ug = xg_prev * wg[0:1] + xg * wg[1:2] + xg_next * wg[2:3] + bg
            uv = xv_prev * wv[0:1] + xv * wv[1:2] + xv_next * wv[2:3] + bv
            sg = jax.nn.sigmoid(ug)
            dug = da * uv * (sg * (1.0 + ug * (1.0 - sg)))
            duv = da * (ug * sg)
            out = []
            for half, (x_prev, x, x_next, w, du) in enumerate(((xg_prev, xg, xg_next, wg, dug),
                                                               (xv_prev, xv, xv_next, wv, duv))):
                du_prev, du_next = shifted(du)
                dx = du_next * w[0:1] + du * w[1:2] + du_prev * w[2:3]
                dx_ref[half, pl.ds(r0, CHUNK), lanes] = dx[centre].astype(BF16)
                out += [fold(x_prev * du), fold(x * du), fold(x_next * du), fold(du)]
            return tuple(s + o for s, o in zip(sums, out))

        sums = chunk(0, tuple(jnp.zeros((8, TC), F32) for _ in range(8)))
        sums = lax.fori_loop(1, last, chunk, sums)
        sums = chunk(last, sums)
        rows = [jnp.sum(s, axis=0, keepdims=True) for s in sums]
        for half in range(2):
            dw_ref[half, :, lanes] = jnp.concatenate(rows[4 * half:4 * half + 3], axis=0)
            db_ref[half, :, lanes] = rows[4 * half + 3]

    gate = lambda shape: pl.BlockSpec(shape, lambda j: (0, j))
    val = lambda shape: pl.BlockSpec(shape, lambda j: (0, j + DFF // wide))
    return pl.pallas_call(
        body, name=name, grid=(DFF // wide,),
        in_specs=[gate((T, wide)), pl.BlockSpec((BWD_BLOCKS, T, TC), lambda j: (j, 0, 0)),
                  pl.BlockSpec((BWD_BLOCKS, T, TC), lambda j: (j + DFF // wide, 0, 0)),
                  gate((3, wide)), val((3, wide)), gate((1, wide)), val((1, wide))],
        out_specs=[pl.BlockSpec((2, T, wide), lambda j: (0, 0, j)), pl.BlockSpec((2, 3, wide), lambda j: (0, 0, j)),
                   pl.BlockSpec((2, 1, wide), lambda j: (0, 0, j))],
        out_shape=[jax.ShapeDtypeStruct((2, T, DFF), BF16), jax.ShapeDtypeStruct((2, 3, DFF), F32),
                   jax.ShapeDtypeStruct((2, 1, DFF), F32)],
        compiler_params=_params(("parallel",)),
    )(dact, up, up, conv_w, conv_w, conv_b, conv_b)


def _dup_spec(tm, nj):
    per = DFF // nj
    return pl.BlockSpec((None, tm, nj), lambda a, b, j: (j // per, 0 if tm == T else b, j % per))


def _dup_spec_tn(tm, nj):
    per = DFF // nj
    return pl.BlockSpec((None, tm, nj), lambda j, kt, r: (j // per, 0, j % per))


def _adamw_math(w, g, m, v):
    m = ADAM_B1 * m + (1.0 - ADAM_B1) * g
    v = ADAM_B2 * v + (1.0 - ADAM_B2) * (g * g)
    m_hat = m / (1.0 - ADAM_B1 ** ADAM_STEP)
    v_hat = v / (1.0 - ADAM_B2 ** ADAM_STEP)
    delta = -ADAM_LR * (m_hat / (jnp.sqrt(v_hat) + ADAM_EPS) + ADAM_WD * w)
    return delta, m, v


ADAM_BLOCK = 256 * 1408


def _adamw_sharded(name, w, m, v, parts):
    _, r, c = w.shape
    tr = max(t for t in range(16, r + 1, 16) if r % t == 0 and t * c <= ADAM_BLOCK)

    def body(w_ref, m_ref, v_ref, p0_ref, p1_ref, g_ref, d_ref, nm_ref, nv_ref):
        def run(p_ref):
            g = p_ref[0].astype(F32)
            for k in range(1, N_DEV):
                g = g + p_ref[k].astype(F32)
            d, nm, nv = _adamw_math(w_ref[...], g, m_ref[...], v_ref[...])
            g_ref[...] = g
            d_ref[...] = d
            nm_ref[...] = nm
            nv_ref[...] = nv

        @pl.when(pl.program_id(0) == 0)
        def _():
            run(p0_ref)

        @pl.when(pl.program_id(0) == 1)
        def _():
            run(p1_ref)

    ws = pl.BlockSpec((None, tr, c), lambda l, i: (l, i, 0))
    p0 = pl.BlockSpec((N_DEV, tr, c), lambda l, i: (0, jnp.where(l == 0, i, r // tr - 1), 0))
    p1 = pl.BlockSpec((N_DEV, tr, c), lambda l, i: (0, jnp.where(l == 1, i, 0), 0))
    return pl.pallas_call(
        body, name=name, grid=(DEPTH, r // tr), in_specs=[ws, ws, ws, p0, p1], out_specs=[ws] * 4,
        out_shape=[jax.ShapeDtypeStruct(w.shape, F32)] * 4, compiler_params=_params(("arbitrary", "arbitrary")),
    )(w, m, v, *parts)


def _sum_devices(name, parts):
    r = parts.shape[1]

    def body(p_ref, o_ref):
        g = p_ref[0]
        for k in range(1, N_DEV):
            g = g + p_ref[k]
        o_ref[...] = g

    return pl.pallas_call(
        body, name=name, in_specs=[pl.BlockSpec((N_DEV, r, LANE), lambda: (0, 0, 0))],
        out_specs=pl.BlockSpec((r, LANE), lambda: (0, 0)), out_shape=jax.ShapeDtypeStruct((r, LANE), F32),
        compiler_params=_params(),
    )(parts)


def _adamw_small(name, ws, gs, ms, vs):
    n = len(ws)
    shapes = [w.shape for w in ws]
    ws, gs, ms, vs = ([a.reshape(1, -1) if a.ndim == 1 else a for a in arrs] for arrs in (ws, gs, ms, vs))
    specs = [pl.BlockSpec(memory_space=pltpu.VMEM)] * n

    def body(*refs):
        for i in range(n):
            w_ref, g_ref, m_ref, v_ref = (refs[k * n + i] for k in range(4))
            d, nm, nv = _adamw_math(w_ref[...], g_ref[...], m_ref[...], v_ref[...])
            refs[4 * n + i][...] = d
            refs[5 * n + i][...] = nm
            refs[6 * n + i][...] = nv

    outs = pl.pallas_call(
        body, name=name, in_specs=specs * 4, out_specs=specs * 3,
        out_shape=[jax.ShapeDtypeStruct(w.shape, F32) for w in ws] * 3, compiler_params=_params(),
    )(*ws, *gs, *ms, *vs)
    outs = [o.reshape(shapes[i % n]) for i, o in enumerate(outs)]
    return outs[:n], outs[n:2 * n], outs[2 * n:]


def _pack(arrays):
    flat = jnp.concatenate([a.reshape(-1) for a in arrays])
    pad = (-flat.shape[0]) % (8 * LANE)
    return jnp.pad(flat, (0, pad)).reshape(-1, LANE)


def _unpack(buf, shapes):
    flat, out, off = buf.reshape(-1), [], 0
    for s in shapes:
        n = 1
        for d in s:
            n *= d
        out.append(flat[off:off + n].reshape(s))
        off += n
    return out


def _local_step(x, target, small, weights, conv_w_full, hand_over, used):
    cos2, sin2 = _rope_tables()
    bias_a = _dilation_bias()
    tables = [_rpb_tables(f"rpb_tables_{l}", small["rpb_c"][l]) for l in range(DEPTH)]
    saved, carry = [], 0.0
    for l in range(DEPTH):
        g1, g2 = small["ln_attn"][l][None] + carry, small["ln_ffn"][l][None]
        gain, sink, cb = small["mix_gain"][l][None], small["sink_b"][l], small["conv_b"][l][None]
        cw = conv_w_full[l]
        bias = tables[l]
        h1, qkv = _prologue_matmul(f"proj_in_{l}", _rmsnorm_rows, [x, g1], [D, None],
                                   weights("w_in", l, [cos2, sin2, bias_a] + tables if l == 0 else x),
                                   (D, 1024), lambda j: (0, j), 1024, epilogue=_rope_epilogue, extras=(cos2, sin2),
                                   out_dtype=BF16)
        zero = used(f"proj_in_{l}", qkv)
        qb, kb, vb = (qkv[:, BLOCK_OF[n] * LANE:BLOCK_OF[n] * LANE + w] for n, w in (("qb", WB), ("kb", WKV), ("vb", WKV)))
        oa, lse_a = _attn_a_fwd(f"attn_a_{l}", qkv, bias_a)
        ob, lse_b = _attn_b_fwd(f"attn_b_{l}", qb, kb, vb, sink + zero)
        oc, lse_c = _attn_c_fwd(f"attn_c_{l}", qkv, bias)
        mixed, x_mid = _prologue_matmul(f"proj_out_{l}", _mix_rows, [oa, ob, oc, gain + used(f"attn_{l}", oc)],
                                        [WA, WB, WC, None],
                                        weights("w_out", l, oc), (N_DEV, D // N_DEV, 512), lambda j: (0, 0, j), 512,
                                        res=x)
        h2, up = _prologue_matmul(f"ffn_up_{l}", _rmsnorm_rows, [x_mid, g2 + used(f"proj_out_{l}", x_mid)], [D, None],
                                  weights("w_up", l, x_mid), (D, 1024), lambda j: (0, j), 1024, blocked_out=True)
        act = _ffn_mid_fwd(f"ffn_mid_{l}", up, cw, cb + used(f"ffn_up_{l}", up))
        x_out = _nn_rows(f"ffn_down_{l}", act, weights("w_down", l, act), x_mid, 4, 1024, 1024)
        carry = used(f"ffn_down_{l}", x_out)
        saved.append(dict(x=x, h1=h1, qkv=(qkv, qb, kb, vb), o=(oa, ob, oc), lse=(lse_a, lse_b, lse_c), mixed=mixed,
                          x_mid=x_mid, h2=h2, up=up, act=act, g1=g1, g2=g2, gain=gain, sink=sink, cb=cb, cw=cw, bias=bias))
        x = x_out

    loss8, dx, dxb, d_ln_final = _loss_head(x, small["ln_final"][None], target)
    sgrads = [None] * DEPTH
    for l in reversed(range(DEPTH)):
        s = saved[l]
        qkv, qb, kb, vb = s["qkv"]
        oa, ob, oc = s["o"]
        wg_in, wg_out = weights("w_in", l, None), weights("w_out", l, None)
        wg_up, wg_down = weights("w_up", l, None), weights("w_down", l, None)
        g_down = _tn_rows(f"wgrad_down_{l}", s["act"], dxb, wg_down.shape[1], 2, 1024)
        zero = hand_over("w_down", l, g_down)
        dact = _nt_rows(f"dgrad_down_{l}", dxb, wg_down, 4, 512)
        dup, d_cw, d_cb = _ffn_mid_bwd(f"ffn_mid_bwd_{l}", dact, s["up"], s["cw"], s["cb"] + zero)
        g_up = _tn_cols(f"wgrad_up_{l}", s["h2"], dup, _dup_spec_tn, 2 * DFF, DFF // 2)
        zero = hand_over("w_up", l, g_up)
        dh2 = _nt_cols(f"dgrad_up_{l}", dup, _dup_spec, wg_up, DFF // 2)
        dx, dxb, d_g2 = _rmsnorm_bwd(f"norm_ffn_bwd_{l}", dh2, s["x_mid"], s["g2"] + zero, dx)
        g_out = _tn_rows(f"wgrad_out_{l}", s["mixed"], dxb, wg_out.shape[1], 2, D)
        zero = hand_over("w_out", l, g_out)
        dmixed = _nt_rows(f"dgrad_out_{l}", dxb, wg_out, 2, T)
        doa, dob, doc, d_gain = _mix_bwd(f"mix_bwd_{l}", dmixed, oa, ob, oc, s["gain"] + zero)
        lse_a, lse_b, lse_c = s["lse"]
        dqa, dka, dva = _attn_a_bwd(f"attn_a_bwd_{l}", qkv, oa, lse_a, doa, bias_a)
        dqb, dkb, dvb, d_sink = _attn_b_bwd(f"attn_b_bwd_{l}", qb, kb, vb, ob, lse_b, dob, s["sink"])
        dqc, dkc, dvc, d_bias = _attn_c_bwd(f"attn_c_bwd_{l}", qkv, oc, lse_c, doc, s["bias"])
        d_rpb = _rpb_reduce(f"rpb_reduce_{l}", d_bias)
        dproj = _rope_bwd(f"rope_bwd_{l}", (dqa, dka, dva, dqb, dkb, dvb, dqc, dkc, dvc), cos2, sin2)
        g_in = _tn_cols(f"wgrad_in_{l}", s["h1"], dproj,
                        lambda tm, tn: pl.BlockSpec((tm, tn), lambda j, kt, r: (0, j)), IN_COLS, 1024)
        zero = hand_over("w_in", l, g_in)
        dh1 = _nt_cols(f"dgrad_in_{l}", dproj, lambda tm, nc: pl.BlockSpec((tm, nc), lambda kt, i, j: (i, j)), wg_in,
                       IN_COLS // 2)
        dx, dxb, d_g1 = _rmsnorm_bwd(f"norm_attn_bwd_{l}", dh1, s["x"], s["g1"] + zero, dx)
        sgrads[l] = dict(ln_attn=d_g1[0], sink_b=d_sink[0, :HB], rpb_c=d_rpb, mix_gain=d_gain[0], ln_ffn=d_g2[0],
                         conv_w=d_cw.transpose(1, 0, 2).reshape(3, 2 * DFF), conv_b=d_cb.reshape(2 * DFF))
    return loss8[0, 0], dx, d_ln_final[0], sgrads


SMALL_NAMES = ("ln_attn", "sink_b", "rpb_c", "mix_gain", "ln_ffn", "conv_b")


def kernel(x, ln_attn, w_in, sink_b, rpb_c, mix_gain, w_out, ln_ffn, w_up, conv_w, conv_b, w_down, ln_final, loss_target, m_ln_attn, m_w_in, m_sink_b, m_rpb_c, m_mix_gain, m_w_out, m_ln_ffn, m_w_up, m_conv_w, m_conv_b, m_w_down, m_ln_final, v_ln_attn, v_w_in, v_sink_b, v_rpb_c, v_mix_gain, v_w_out, v_ln_ffn, v_w_up, v_conv_w, v_conv_b, v_w_down, v_ln_final):
    me = 4 * lax.axis_index("x") + 2 * lax.axis_index("y") + lax.axis_index("c")
    small = dict(ln_attn=ln_attn, sink_b=sink_b, rpb_c=rpb_c, mix_gain=mix_gain, ln_ffn=ln_ffn, conv_b=conv_b,
                 ln_final=ln_final)

    names = ("w_in", "w_out", "w_up", "w_down")
    shards = dict(w_in=w_in, w_out=w_out, w_up=w_up, w_down=w_down)
    order = [(n, l) for l in range(DEPTH) for n in names]
    conv_key = ("conv_w", 0)
    started, arrived, forwarded, gathered = {}, {}, {}, {}

    def side_by_side(k):
        return k[0] in ("w_in", "w_up")

    def slot_of(k):
        return _col_slot(shards[k[0]].shape[2]) if side_by_side(k) else _lead_slot

    def begin(name, ks, zero):
        srcs = [_pack([conv_w]) + zero if k == conv_key else (shards[k[0]][k[1]] + zero).astype(BF16) for k in ks]
        lands = [lax.empty((s.shape[0], N_DEV * s.shape[1]) if side_by_side(k) else (N_DEV,) + s.shape, s.dtype)
                 for k, s in zip(ks, srcs)]
        peers = [ALL_PEERS if k == conv_key else NEAR_PEERS for k in ks]
        send, recv, bufs, tok = _copy_start(name, srcs + lands, _gather_plan(peers, [slot_of(k) for k in ks]),
                                            [len(p) + 1 for p in peers])
        for i, k in enumerate(ks):
            started[k] = (send[i], recv[i], bufs[i], bufs[len(ks) + i], peers[i])
        return tok

    token = begin("gather_start_first", order[:1], 0.0)
    token = begin("gather_start_rest", [conv_key] + order[1:], token[0, 0])

    def arrive(k, after):
        send, recv, src, land, peers = started[k]
        arrived[k] = _copy_wait(f"gather_{k[0]}_{k[1]}_arrived", [src, land], [send], [recv],
                                _gather_plan([peers], [slot_of(k)]), after)

    queue = list(order)

    def advance(after):
        if not queue:
            return 0.0
        k = queue.pop(0)
        arrive(k, after)
        forwarded[k] = _copy_start(f"gather_{k[0]}_{k[1]}_forward", [arrived[k][1]], _forward_plan(slot_of(k)),
                                   [len(OTHER_CHIPS)])
        return forwarded[k][3][0, 0]

    pass_on_behind = ("proj_in_0", "attn_0", "ffn_up_0", "ffn_down_0", "proj_in_1", "attn_1", "ffn_up_1")

    def used(point, result):
        return advance(result) if point in pass_on_behind else 0.0

    def weights(n, l, after):
        k = (n, l)
        if k not in gathered:
            if k not in forwarded:
                advance(after)
            send_b, recv_b, (land,), _ = forwarded[k]
            (gathered[k],) = _copy_wait(f"gather_{n}_{l}_done", [land], send_b, recv_b, _forward_plan(slot_of(k)),
                                        after)
        return gathered[k]

    pending = {}

    def hand_over(n, l, g):
        shard = shards[n].shape[1:]
        send, recv, bufs, tok = _copy_start(f"send_grad_{n}_{l}", [g, lax.empty((N_DEV,) + shard, g.dtype)],
                                            _scatter_plan(slot_of((n, l))), [len(ALL_PEERS) + 1])
        pending[(n, l)] = (send, recv, bufs)
        return tok[0, 0]

    def received(k, after):
        send, recv, bufs = pending[k]
        return _copy_wait(f"recv_grad_{k[0]}_{k[1]}", bufs, send, recv, _scatter_plan(slot_of(k)), after)[1]

    arrive(conv_key, token)
    cw_all = arrived[conv_key][1]
    nup = w_up.shape[2]
    cw_shards = cw_all.reshape(N_DEV, -1)[:, :DEPTH * 3 * nup].reshape(N_DEV, DEPTH, 3, nup)
    conv_w_full = cw_shards.transpose(1, 2, 0, 3).reshape(DEPTH, 3, N_DEV * nup)

    loss_local, dx, d_ln_final, sgrads = _local_step(
        x[0], loss_target[0], dict(small, ln_attn=ln_attn + token[0, 0]), weights, conv_w_full, hand_over, used)

    stacked = [jnp.stack([sgrads[l][n] for l in range(DEPTH)]) for n in SMALL_NAMES + ("conv_w",)] + [d_ln_final]
    shapes = [a.shape for a in stacked]
    mine = _pack(stacked)
    send_s, recv_s, bufs_s, _ = _copy_start("gather_small_grads_start", [mine, lax.empty((N_DEV,) + mine.shape, F32)],
                                            _gather_plan([ALL_PEERS], [_lead_slot]), [len(ALL_PEERS) + 1])

    big, after = {}, dx
    moments = dict(w_in=(m_w_in, v_w_in), w_out=(m_w_out, v_w_out), w_up=(m_w_up, v_w_up), w_down=(m_w_down, v_w_down))
    for n in reversed(names):
        parts = (received((n, 0), after), received((n, 1), after))
        big[n] = _adamw_sharded(f"adamw_{n}", shards[n], *moments[n], parts)
        after = big[n][1]

    _, everyone = _copy_wait("gather_small_grads_done", bufs_s, send_s, recv_s,
                             _gather_plan([ALL_PEERS], [_lead_slot]), after)
    g_small = _unpack(_sum_devices("sum_small_grads", everyone), shapes)
    g = dict(zip(SMALL_NAMES + ("conv_w", "ln_final"), g_small))
    g["conv_w"] = lax.dynamic_slice_in_dim(g["conv_w"], me * nup, nup, axis=2)

    snames = SMALL_NAMES + ("conv_w", "ln_final")
    sw = dict(small, conv_w=conv_w)
    sm = dict(ln_attn=m_ln_attn, sink_b=m_sink_b, rpb_c=m_rpb_c, mix_gain=m_mix_gain, ln_ffn=m_ln_ffn,
              conv_b=m_conv_b, conv_w=m_conv_w, ln_final=m_ln_final)
    sv = dict(ln_attn=v_ln_attn, sink_b=v_sink_b, rpb_c=v_rpb_c, mix_gain=v_mix_gain, ln_ffn=v_ln_ffn,
              conv_b=v_conv_b, conv_w=v_conv_w, ln_final=v_ln_final)
    s_delta, s_m, s_v = (dict(zip(snames, out)) for out in _adamw_small(
        "adamw_small", [sw[n] for n in snames], [g[n] for n in snames], [sm[n] for n in snames],
        [sv[n] for n in snames]))

    loss = lax.psum(loss_local, ("x", "y", "c"))
    outputs = ("ln_attn", "w_in", "sink_b", "rpb_c", "mix_gain", "w_out", "ln_ffn", "w_up", "conv_w", "conv_b",
               "w_down", "ln_final")
    grads = [big[n][0] if n in big else g[n] for n in outputs]
    deltas = [big[n][1] if n in big else s_delta[n] for n in outputs]
    new_m = [big[n][2] if n in big else s_m[n] for n in outputs]
    new_v = [big[n][3] if n in big else s_v[n] for n in outputs]
    return (loss, dx[None], *grads, *deltas, *new_m, *new_v)
```

```python
import functools

import jax
import jax.numpy as jnp
from jax import lax
from jax.experimental import pallas as pl
from jax.experimental.pallas import tpu as pltpu

F32 = jnp.float32
BF16 = jnp.bfloat16

N_DEV = 8
T = 2048
D = 2048
DEPTH = 2
HD = 64
HA, HB, HKV, HC = 12, 10, 2, 10
WA, WB, WKV, WC = HA * HD, HB * HD, HKV * HD, HC * HD
IN_COLS = 3 * WA + WB + 2 * WKV + 3 * WC
DFF = 5632
GRID_W = 64
ROWS = T // GRID_W
NA_ROWS, NA_COLS = 8, 16
WINDOW_B = 128
EPS = 1e-6
NEG = -1e30
ROPE_THETA = 10000.0
LANE = 128
VMEM_LIMIT = 56 * 1024 * 1024

ADAM_LR, ADAM_B1, ADAM_B2, ADAM_EPS, ADAM_WD, ADAM_STEP = 0.001, 0.9, 0.999, 1e-08, 0.01, 10

GROUPS = (("qa", WA, True, True), ("ka", WA, True, False), ("va", WA, False, False),
          ("qb", WB, True, True), ("kb", WKV, True, False), ("vb", WKV, False, False),
          ("qc", WC, False, True), ("kc", WC, False, False), ("vc", WC, False, False))


def _params(sem=None):
    return pltpu.CompilerParams(dimension_semantics=sem, vmem_limit_bytes=VMEM_LIMIT)


HBM_SPEC = pl.BlockSpec(memory_space=pltpu.HBM)
SEM_SPEC = pl.BlockSpec(memory_space=pltpu.SEMAPHORE)
DATAFLOW = pltpu.SideEffectType.DATAFLOW_SIDE_EFFECTING


ALL_PEERS = tuple((p >> 2 & 1, p >> 1 & 1, p & 1) for p in range(1, N_DEV))
OTHER_CHIPS = ((1, 0, 0), (0, 1, 0), (1, 1, 0))
NEAR_PEERS = ((0, 0, 1),) + OTHER_CHIPS


def _flip(x, y, c, f):
    return (1 - x if f[0] else x, 1 - y if f[1] else y, 1 - c if f[2] else c)


def _index(pos):
    return 4 * pos[0] + 2 * pos[1] + pos[2]


class _LocalCopy:
    def __init__(self, src, dst, sem):
        self.copy = pltpu.make_async_copy(src, dst, sem)

    def start(self):
        self.copy.start()

    def wait_send(self):
        self.copy.wait()

    def wait_recv(self):
        pass


def _descriptors(plan, bufs, send_sems, recv_sems):
    x, y, c = lax.axis_index("x"), lax.axis_index("y"), lax.axis_index("c")
    return [_LocalCopy(src, dst, send_sems[g].at[i]) if partner is None else
            pltpu.make_async_remote_copy(src_ref=src, dst_ref=dst, send_sem=send_sems[g].at[i],
                                         recv_sem=recv_sems[g].at[i], device_id=partner,
                                         device_id_type=pl.DeviceIdType.MESH)
            for g, copies in enumerate(plan(bufs, x, y, c)) for i, (src, dst, partner) in enumerate(copies)]


def _copy_start(name, bufs, plan, sizes):
    nb, ng = len(bufs), len(sizes)

    def body(*refs):
        for d in _descriptors(plan, refs[:nb], refs[nb:nb + ng], refs[nb + ng:nb + 2 * ng]):
            d.start()
        refs[2 * nb + 2 * ng][...] = jnp.zeros((8, LANE), F32)

    outs = pl.pallas_call(
        body, name=name,
        out_shape=[pltpu.SemaphoreType.DMA((s,)) for s in sizes] * 2 + [pltpu.HBM(b.shape, b.dtype) for b in bufs]
        + [jax.ShapeDtypeStruct((8, LANE), F32)],
        in_specs=[HBM_SPEC] * nb,
        out_specs=[SEM_SPEC] * (2 * ng) + [HBM_SPEC] * nb + [pl.BlockSpec(memory_space=pltpu.VMEM)],
        input_output_aliases={i: 2 * ng + i for i in range(nb)},
        compiler_params=pltpu.CompilerParams(has_side_effects=DATAFLOW),
    )(*[pltpu.with_memory_space_constraint(b, pltpu.HBM) for b in bufs])
    return outs[:ng], outs[ng:2 * ng], outs[2 * ng:2 * ng + nb], outs[2 * ng + nb]


def _copy_wait(name, bufs, send_sems, recv_sems, plan, after):
    nb, ng = len(bufs), len(send_sems)
    after = list(after) if isinstance(after, (list, tuple)) else [after]

    def body(*refs):
        for d in _descriptors(plan, refs[:nb], refs[nb:nb + ng], refs[nb + ng:nb + 2 * ng]):
            d.wait_send()
            d.wait_recv()

    return pl.pallas_call(
        body, name=name, out_shape=[pltpu.HBM(b.shape, b.dtype) for b in bufs],
        in_specs=[HBM_SPEC] * nb + [SEM_SPEC] * (2 * ng) + [pl.BlockSpec(memory_space=pl.ANY)] * len(after),
        out_specs=[HBM_SPEC] * nb, input_output_aliases={i: i for i in range(nb)},
        compiler_params=pltpu.CompilerParams(has_side_effects=DATAFLOW),
    )(*bufs, *send_sems, *recv_sems, *after)


def _lead_slot(ref, k):
    return ref.at[k]


def _col_slot(width):
    return lambda ref, k: ref.at[:, pl.ds(pl.multiple_of(k * width, LANE), width)]


def _gather_plan(peer_sets, slots):
    def plan(bufs, x, y, c):
        n = len(peer_sets)
        return [[(bufs[i], slots[i](bufs[n + i], _index((x, y, c))), _flip(x, y, c, f)) for f in peers]
                + [(bufs[i], slots[i](bufs[n + i], _index((x, y, c))), None)] for i, peers in enumerate(peer_sets)]
    return plan


def _forward_plan(slot):
    def plan(bufs, x, y, c):
        pieces = [slot(bufs[0], _index(_flip(x, y, c, f))) for f in OTHER_CHIPS]
        return [[(p, p, _flip(x, y, c, (0, 0, 1))) for p in pieces]]
    return plan


def _scatter_plan(slot):
    def plan(bufs, x, y, c):
        me = _index((x, y, c))
        peers = [_flip(x, y, c, f) for f in ALL_PEERS]
        return [[(slot(bufs[0], _index(p)), bufs[1].at[me], p) for p in peers]
                + [(slot(bufs[0], me), bufs[1].at[me], None)]]
    return plan


def _flat2(v):
    return v.reshape(-1, v.shape[-1])


def _matmul(name, kind, a, a_spec, b, b_spec, out_shape, out_spec, grid, res=None, res_spec=None, acc_shape=None):
    dims = {"nn": (((1,), (0,)), ((), ())), "nt": NT_DIMS, "tn": TN_DIMS}[kind]
    nred = grid[-1]

    def body(*refs):
        if res is None:
            a_ref, b_ref, o_ref = refs[:3]
            r_ref = None
        else:
            a_ref, b_ref, r_ref, o_ref = refs[:4]
        part = lax.dot_general(_flat2(a_ref[...]), _flat2(b_ref[...]), dims, preferred_element_type=F32)

        def finish(total):
            if r_ref is not None:
                total = total + r_ref[...]
            o_ref[...] = total.reshape(o_ref.shape).astype(o_ref.dtype)

        if nred == 1:
            finish(part)
        else:
            acc_ref = refs[-1]
            k = pl.program_id(len(grid) - 1)

            @pl.when(k == 0)
            def _():
                acc_ref[...] = part

            @pl.when(jnp.logical_and(k > 0, k < nred - 1))
            def _():
                acc_ref[...] += part

            @pl.when(k == nred - 1)
            def _():
                finish(acc_ref[...] + part)

    ins, specs = [a, b], [a_spec, b_spec]
    if res is not None:
        ins.append(res)
        specs.append(res_spec)
    scratch = [] if nred == 1 else [pltpu.VMEM(acc_shape, F32)]
    return pl.pallas_call(
        body, name=name, grid=grid, in_specs=specs, out_specs=out_spec, out_shape=out_shape, scratch_shapes=scratch,
        compiler_params=_params(("parallel",) * (len(grid) - 1) + ("arbitrary",)),
    )(*ins)


def _nn_rows(name, a, wg, res, s, tn, tm):
    _, kj, n = wg.shape
    return _matmul(
        name, "nn", a, pl.BlockSpec((tm, s * kj), lambda j, i, r: (i, r)),
        wg, pl.BlockSpec((s, kj, tn), lambda j, i, r: (r, 0, j)),
        jax.ShapeDtypeStruct((T, n), F32), pl.BlockSpec((tm, tn), lambda j, i, r: (i, j)),
        (n // tn, T // tm, N_DEV // s), res=res, res_spec=pl.BlockSpec((tm, tn), lambda j, i, r: (i, j)),
        acc_shape=(tm, tn))


def _nt_cols(name, dc, dc_spec_of, w, nc):
    k, n = w.shape
    tm = tk = 1024
    return _matmul(
        name, "nt", dc, dc_spec_of(tm, nc),
        w, pl.BlockSpec((tk, nc), lambda kt, i, j: (kt, j)),
        jax.ShapeDtypeStruct((T, k), F32), pl.BlockSpec((tm, tk), lambda kt, i, j: (i, kt)),
        (k // tk, T // tm, n // nc), acc_shape=(tm, tk))


def _nt_rows(name, dc, wg, s, tm):
    _, kj, n = wg.shape
    return _matmul(
        name, "nt", dc, pl.BlockSpec((tm, n), lambda kt, i, r: (i, 0)),
        wg, pl.BlockSpec((s, kj, n), lambda kt, i, r: (kt, 0, 0)),
        jax.ShapeDtypeStruct((T, N_DEV * kj), F32), pl.BlockSpec((tm, s * kj), lambda kt, i, r: (i, kt)),
        (N_DEV // s, T // tm, 1))


def _tn_cols(name, a, dc, dc_spec_of, n, tn):
    k = a.shape[1]
    tk = 1024
    return _matmul(
        name, "tn", a, pl.BlockSpec((T, tk), lambda j, kt, r: (0, kt)),
        dc, dc_spec_of(T, tn),
        jax.ShapeDtypeStruct((k, n), BF16), pl.BlockSpec((tk, tn), lambda j, kt, r: (kt, j)),
        (n // tn, k // tk, 1))


def _tn_rows(name, a, dc, kj, s, tn):
    n = dc.shape[1]
    return _matmul(
        name, "tn", a, pl.BlockSpec((T, s * kj), lambda kt, j, r: (0, kt)),
        dc, pl.BlockSpec((T, tn), lambda kt, j, r: (0, j)),
        jax.ShapeDtypeStruct((N_DEV, kj, n), BF16), pl.BlockSpec((s, kj, tn), lambda kt, j, r: (kt, 0, j)),
        (N_DEV // s, n // tn, 1))


TR = 512


def _rows(width):
    return pl.BlockSpec((TR, width), lambda i: (i, 0))


def _whole(shape):
    return pl.BlockSpec(shape, lambda i: (0,) * len(shape))


def _rmsnorm_rows(x_ref, g_ref):
    xv = x_ref[...]
    r = lax.rsqrt(jnp.mean(xv * xv, axis=-1, keepdims=True) + EPS)
    return ((xv * r) * g_ref[...]).astype(BF16)


SUB = 256


def _prologue_matmul(name, prologue, ins, widths, w, w_block, w_index, tn, res=None, epilogue=None, extras=(),
                     out_dtype=F32, blocked_out=False):
    tm = 1024
    n = w.shape[-1]
    ni = len(ins)

    def body(*refs):
        w_ref = refs[ni]
        r_ref = refs[ni + 1] if res is not None else None
        x_refs = refs[ni + 1 + (res is not None):len(refs) - 3]
        h_ref, o_ref, h_scr = refs[-3:]

        @pl.when(pl.program_id(1) == 0)
        def _():
            h = prologue(*refs[:ni])
            h_scr[...] = h
            h_ref[...] = h

        for sub in range(tn // SUB):
            cols = slice(sub * SUB, (sub + 1) * SUB)
            w_cols = w_ref[(slice(None),) * (len(w_ref.shape) - 1) + (cols,)]
            part = jnp.dot(h_scr[...], _flat2(w_cols), preferred_element_type=F32)
            if r_ref is not None:
                part = part + r_ref[:, cols]
            if epilogue is not None:
                part = epilogue(pl.program_id(1) * (tn // SUB) + sub, part, *x_refs)
            if blocked_out:
                for b in range(SUB // LANE):
                    o_ref[sub * (SUB // LANE) + b] = part[:, b * LANE:(b + 1) * LANE].astype(out_dtype)
            else:
                o_ref[:, cols] = part.astype(out_dtype)

    tile = pl.BlockSpec((tm, tn), lambda i, j: (i, j))
    out_tile = pl.BlockSpec((tn // LANE, tm, LANE), lambda i, j: (j, i, 0)) if blocked_out else tile
    out_full = (n // LANE, T, LANE) if blocked_out else (T, n)
    specs = [pl.BlockSpec((1, D), lambda i, j: (0, 0)) if wd is None else pl.BlockSpec((tm, wd), lambda i, j: (i, 0))
             for wd in widths]
    specs.append(pl.BlockSpec(w_block, lambda i, j: w_index(j)))
    operands = list(ins) + [w]
    if res is not None:
        specs.append(tile)
        operands.append(res)
    specs += [pl.BlockSpec((tm, LANE), lambda i, j: (i, 0))] * len(extras)
    operands += list(extras)
    return pl.pallas_call(
        body, name=name, grid=(T // tm, n // tn), in_specs=specs,
        out_specs=[pl.BlockSpec((tm, D), lambda i, j: (i, 0)), out_tile],
        out_shape=[jax.ShapeDtypeStruct((T, D), BF16), jax.ShapeDtypeStruct(out_full, out_dtype)],
        scratch_shapes=[pltpu.VMEM((tm, D), BF16)], compiler_params=_params(("parallel", "arbitrary")),
    )(*operands)


def _rms_bwd_math(dy, xv, g):
    r = lax.rsqrt(jnp.mean(xv * xv, axis=-1, keepdims=True) + EPS)
    xhat = xv * r
    dxhat = dy * g
    dx = r * (dxhat - xhat * jnp.mean(dxhat * xhat, axis=-1, keepdims=True))
    return dx, dy * xhat


def _accumulate(ref, val):
    @pl.when(pl.program_id(0) == 0)
    def _():
        ref[...] = val

    @pl.when(pl.program_id(0) > 0)
    def _():
        ref[...] += val


def _rmsnorm_bwd(name, dy, x, g, res):
    def body(dy_ref, x_ref, g_ref, res_ref, dx_ref, dxb_ref, dg_ref):
        dx, dgr = _rms_bwd_math(dy_ref[...], x_ref[...], g_ref[...])
        tot = res_ref[...] + dx
        dx_ref[...] = tot
        dxb_ref[...] = tot.astype(BF16)
        _accumulate(dg_ref, jnp.sum(dgr, axis=0, keepdims=True))

    return pl.pallas_call(
        body, name=name, grid=(T // TR,), in_specs=[_rows(D), _rows(D), _whole((1, D)), _rows(D)],
        out_specs=[_rows(D), _rows(D), _whole((1, D))],
        out_shape=[jax.ShapeDtypeStruct((T, D), F32), jax.ShapeDtypeStruct((T, D), BF16),
                   jax.ShapeDtypeStruct((1, D), F32)],
        compiler_params=_params(("arbitrary",)),
    )(dy, x, g, res)


def _loss_head(x, g, target):
    def body(x_ref, g_ref, t_ref, loss_ref, dx_ref, dxb_ref, dg_ref):
        xv, gv = x_ref[...], g_ref[...]
        r = lax.rsqrt(jnp.mean(xv * xv, axis=-1, keepdims=True) + EPS)
        err = (xv * r) * gv - t_ref[...]
        part = 0.5 * jnp.sum(jnp.mean(err * err, axis=-1, keepdims=True))
        dx, dgr = _rms_bwd_math(err * (1.0 / D), xv, gv)
        dx_ref[...] = dx
        dxb_ref[...] = dx.astype(BF16)
        _accumulate(dg_ref, jnp.sum(dgr, axis=0, keepdims=True))
        _accumulate(loss_ref, jnp.full((8, LANE), part, F32))

    return pl.pallas_call(
        body, name="loss_head", grid=(T // TR,), in_specs=[_rows(D), _whole((1, D)), _rows(D)],
        out_specs=[_whole((8, LANE)), _rows(D), _rows(D), _whole((1, D))],
        out_shape=[jax.ShapeDtypeStruct((8, LANE), F32), jax.ShapeDtypeStruct((T, D), F32),
                   jax.ShapeDtypeStruct((T, D), BF16), jax.ShapeDtypeStruct((1, D), F32)],
        compiler_params=_params(("arbitrary",)),
    )(x, g, target)


MIX_OFFS = ((0, WA), (WA, WB), (WA + WB, WC))


def _mix_rows(oa_ref, ob_ref, oc_ref, g_ref):
    parts = []
    for ref, (off, w) in zip((oa_ref, ob_ref, oc_ref), MIX_OFFS):
        o = ref[...]
        r = lax.rsqrt(jnp.mean(o * o, axis=-1, keepdims=True) + EPS)
        parts.append(((o * r) * g_ref[:, off:off + w]).astype(BF16))
    return jnp.concatenate(parts, axis=1)


def _mix_bwd(name, dmixed, oa, ob, oc, gain):
    def body(dm_ref, oa_ref, ob_ref, oc_ref, g_ref, doa_ref, dob_ref, doc_ref, dg_ref):
        dgs = []
        for ref, dref, (off, w) in zip((oa_ref, ob_ref, oc_ref), (doa_ref, dob_ref, doc_ref), MIX_OFFS):
            dx, dgr = _rms_bwd_math(dm_ref[:, off:off + w], ref[...], g_ref[:, off:off + w])
            dref[...] = dx
            dgs.append(jnp.sum(dgr, axis=0, keepdims=True))
        _accumulate(dg_ref, jnp.concatenate(dgs, axis=1))

    return pl.pallas_call(
        body, name=name, grid=(T // TR,),
        in_specs=[_rows(D), _rows(WA), _rows(WB), _rows(WC), _whole((1, D))],
        out_specs=[_rows(WA), _rows(WB), _rows(WC), _whole((1, D))],
        out_shape=[jax.ShapeDtypeStruct((T, WA), F32), jax.ShapeDtypeStruct((T, WB), F32),
                   jax.ShapeDtypeStruct((T, WC), F32), jax.ShapeDtypeStruct((1, D), F32)],
        compiler_params=_params(("arbitrary",)),
    )(dmixed, oa, ob, oc, gain)


def _rope_tables():
    inv_freq = ROPE_THETA ** (-jnp.arange(0, HD, 2, dtype=F32) / HD)
    ang = jnp.arange(T, dtype=F32)[:, None] * inv_freq[None, :]
    cos, sin = jnp.cos(ang), jnp.sin(ang)
    cos2 = jnp.tile(jnp.concatenate([cos, cos], axis=1), (1, LANE // HD))
    sin2 = jnp.tile(jnp.concatenate([-sin, sin], axis=1), (1, LANE // HD))
    return cos2, sin2


def _rot_half(v):
    lane = lax.broadcasted_iota(jnp.int32, v.shape, 1)
    return jnp.where(lane % HD < HD // 2, pltpu.roll(v, LANE - HD // 2, 1), pltpu.roll(v, HD // 2, 1))


BLOCK_KINDS = tuple((rot, is_q) for _, w, rot, is_q in GROUPS for _ in range(w // LANE))
BLOCK_OF = {name: sum(w for _, w, _, _ in GROUPS[:g]) // LANE for g, (name, _, _, _) in enumerate(GROUPS)}


def _any_tile(j, tiles):
    return functools.reduce(jnp.logical_or, [j == t for t in tiles]) if tiles else False


def _rope_epilogue(j, tile, c_ref, s_ref):
    cv, sv = c_ref[...], s_ref[...]
    per, n_tiles = tile.shape[1] // LANE, IN_COLS // tile.shape[1]
    out = []
    for b in range(per):
        v = tile[:, b * LANE:(b + 1) * LANE]
        rot = _any_tile(j, [t for t in range(n_tiles) if BLOCK_KINDS[t * per + b][0]])
        is_q = _any_tile(j, [t for t in range(n_tiles) if BLOCK_KINDS[t * per + b][1]])
        if rot is not False:
            v = jnp.where(rot, v * cv + _rot_half(v) * sv, v)
        if is_q is not False:
            v = v * jnp.where(is_q, HD ** -0.5, 1.0)
        out.append(v)
    return jnp.concatenate(out, axis=1)


def _rope_bwd(name, grads, cos2, sin2):
    def body(*refs):
        ins, (c_ref, s_ref, o_ref) = refs[:9], refs[9:]
        cv, sv = c_ref[...], s_ref[...]
        off = 0
        for d_ref, (_, w, rot, is_q) in zip(ins, GROUPS):
            for b in range(w // LANE):
                v = d_ref[:, b * LANE:(b + 1) * LANE]
                if is_q:
                    v = v * (HD ** -0.5)
                if rot:
                    v = v * cv + _rot_half(v * sv)
                o_ref[:, off + b * LANE:off + (b + 1) * LANE] = v.astype(BF16)
            off += w

    return pl.pallas_call(
        body, name=name, grid=(T // TR,), in_specs=[_rows(w) for _, w, _, _ in GROUPS] + [_rows(LANE), _rows(LANE)],
        out_specs=_rows(IN_COLS), out_shape=jax.ShapeDtypeStruct((T, IN_COLS), BF16),
        compiler_params=_params(("parallel",)),
    )(*grads, cos2, sin2)


NT_DIMS = (((1,), (1,)), ((), ()))
TN_DIMS = (((0,), (0,)), ((), ()))


def _scores(q, k, bias, valid):
    s = lax.dot_general(q, k, NT_DIMS, preferred_element_type=F32)
    if bias is not None:
        s = s + bias
    if valid is not None:
        s = jnp.where(valid, s, NEG)
    return s


def _heads_fwd(heads):
    scores = [_scores(h["q"], h["k"], h.get("bias"), h.get("valid")) for h in heads]
    soft = []
    for s, h in zip(scores, heads):
        m = jnp.max(s, axis=1, keepdims=True)
        e = jnp.exp(s - m)
        l = jnp.sum(e, axis=1, keepdims=True)
        if h.get("sink") is not None:
            l = l + jnp.exp(h["sink"] - m)
        soft.append((e.astype(BF16), l, m + jnp.log(l)))
    return [(jnp.dot(e, h["v"], preferred_element_type=F32) / l, lse) for (e, l, lse), h in zip(soft, heads)]


def _heads_bwd(heads):
    dobs = [h["do"].astype(BF16) for h in heads]
    scores = [_scores(h["q"], h["k"], h.get("bias"), h.get("valid")) for h in heads]
    dps = [lax.dot_general(dob, h["v"], NT_DIMS, preferred_element_type=F32) for dob, h in zip(dobs, heads)]
    mid = []
    for s, dp, h in zip(scores, dps, heads):
        p = jnp.exp(s - h["lse"])
        delta = jnp.sum(h["do"] * h["o"], axis=1, keepdims=True)
        ds = p * (dp - delta)
        dsink = None if h.get("sink") is None else -jnp.exp(h["sink"] - h["lse"]) * delta
        mid.append((p.astype(BF16), ds, dsink))
    out = []
    for (pb, ds, dsink), dob, h in zip(mid, dobs, heads):
        dsb = ds.astype(BF16)
        out.append((jnp.dot(dsb, h["k"], preferred_element_type=F32),
                    lax.dot_general(dsb, h["q"], TN_DIMS, preferred_element_type=F32),
                    lax.dot_general(pb, dob, TN_DIMS, preferred_element_type=F32), ds, dsink))
    return out


def _per_head(cols):
    return jnp.concatenate([jnp.broadcast_to(c, (c.shape[0], HD)) for c in cols], axis=1)


DILATIONS = ((128, 1), (512, 4), (2048, 16))


BQ_A = 256
REACH_A = max(window // 2 for window, _ in DILATIONS)


def _first_key(i):
    return jnp.maximum(i * BQ_A - REACH_A, 0)


def _key_window_groups():
    groups = {}
    for i in range(T // BQ_A):
        width = min(T, (i + 1) * BQ_A + REACH_A) - max(i * BQ_A - REACH_A, 0)
        groups.setdefault(width, []).append(i)
    return groups


def _per_window(i, fn):
    for width, tiles in _key_window_groups().items():
        hit = functools.reduce(jnp.logical_or, [i == t for t in tiles])
        pl.when(hit)(functools.partial(fn, pl.multiple_of(_first_key(i), BQ_A), width))


def _dilation_bias():
    def body(o_ref):
        i = pl.program_id(0)
        t = i * BQ_A + lax.broadcasted_iota(jnp.int32, (BQ_A, T), 0)
        ad = jnp.abs(t - (_first_key(i) + lax.broadcasted_iota(jnp.int32, (BQ_A, T), 1)))
        count = jnp.zeros((BQ_A, T), jnp.int32)
        for window, r in DILATIONS:
            count += jnp.where(((ad & (r - 1)) == 0) & (ad <= window // 2), 1, 0)
        logs = jnp.where(count == 2, jnp.log(2.0), jnp.where(count == 3, jnp.log(3.0), 0.0)).astype(F32)
        o_ref[...] = jnp.where(count == 0, NEG, logs)

    return pl.pallas_call(
        body, name="dilation_bias", grid=(T // BQ_A,), out_specs=pl.BlockSpec((BQ_A, T), lambda i: (i, 0)),
        out_shape=jax.ShapeDtypeStruct((T, T), F32), compiler_params=_params(("parallel",)),
    )()


def _qkv_rows(rows, group):
    return pl.BlockSpec((rows, LANE), lambda p, i: (i, BLOCK_OF[group] + p))


def _qkv_all(group):
    return pl.BlockSpec((T, LANE), lambda p, i: (0, BLOCK_OF[group] + p))


def _attn_a_fwd(name, qkv, bias):
    def body(q_ref, k_ref, v_ref, b_ref, o_ref, lse_ref):
        def tile(first, width):
            b = b_ref[:, :width]
            outs = _heads_fwd([dict(q=q_ref[:, h * HD:(h + 1) * HD], k=k_ref[pl.ds(first, width), h * HD:(h + 1) * HD],
                                    v=v_ref[pl.ds(first, width), h * HD:(h + 1) * HD], bias=b) for h in range(2)])
            o_ref[...] = jnp.concatenate([o for o, _ in outs], axis=1)
            lse_ref[...] = _per_head([lse for _, lse in outs])

        _per_window(pl.program_id(1), tile)

    qs = pl.BlockSpec((BQ_A, LANE), lambda p, i: (i, p))
    ks = pl.BlockSpec((T, LANE), lambda p, i: (0, p))
    return pl.pallas_call(
        body, name=name, grid=(HA // 2, T // BQ_A),
        in_specs=[_qkv_rows(BQ_A, "qa"), _qkv_all("ka"), _qkv_all("va"), pl.BlockSpec((BQ_A, T), lambda p, i: (i, 0))],
        out_specs=[qs, qs],
        out_shape=[jax.ShapeDtypeStruct((T, WA), F32)] * 2, compiler_params=_params(("parallel", "parallel")),
    )(qkv, qkv, qkv, bias)


def _attn_a_bwd(name, qkv, oa, lse, doa, bias):
    def body(q_ref, k_ref, v_ref, o_ref, lse_ref, do_ref, b_ref, dq_ref, dk_ref, dv_ref):
        @pl.when(pl.program_id(1) == 0)
        def _():
            dk_ref[...] = jnp.zeros_like(dk_ref)
            dv_ref[...] = jnp.zeros_like(dv_ref)

        def tile(first, width):
            b = b_ref[:, :width]
            keys = pl.ds(first, width)
            sls = [slice(h * HD, (h + 1) * HD) for h in range(2)]
            res = _heads_bwd([dict(q=q_ref[:, sl], k=k_ref[keys, sl], v=v_ref[keys, sl], o=o_ref[:, sl],
                                   do=do_ref[:, sl], lse=lse_ref[:, sl.start:sl.start + 1], bias=b) for sl in sls])
            dq_ref[...] = jnp.concatenate([r[0] for r in res], axis=1)
            dk_ref[keys, :] += jnp.concatenate([r[1] for r in res], axis=1)
            dv_ref[keys, :] += jnp.concatenate([r[2] for r in res], axis=1)

        _per_window(pl.program_id(1), tile)

    qs = pl.BlockSpec((BQ_A, LANE), lambda p, i: (i, p))
    ks = pl.BlockSpec((T, LANE), lambda p, i: (0, p))
    return pl.pallas_call(
        body, name=name, grid=(HA // 2, T // BQ_A),
        in_specs=[_qkv_rows(BQ_A, "qa"), _qkv_all("ka"), _qkv_all("va"), qs, qs, qs,
                  pl.BlockSpec((BQ_A, T), lambda p, i: (i, 0))], out_specs=[qs, ks, ks],
        out_shape=[jax.ShapeDtypeStruct((T, WA), F32)] * 3, compiler_params=_params(("parallel", "arbitrary")),
    )(qkv, qkv, qkv, oa, lse, doa, bias)


BQ_B = 128
SPAN_B = BQ_B + 2 * WINDOW_B


def _window_b(i):
    start = pl.multiple_of(jnp.clip(i * BQ_B - WINDOW_B, 0, T - SPAN_B), BQ_B)
    qpos = i * BQ_B + lax.broadcasted_iota(jnp.int32, (BQ_B, SPAN_B), 0)
    kpos = start + lax.broadcasted_iota(jnp.int32, (BQ_B, SPAN_B), 1)
    return start, jnp.abs(qpos - kpos) <= WINDOW_B


GROUP_B = HB // HKV


def _stack_group(ref, g):
    return jnp.concatenate([ref[:, h * HD:(h + 1) * HD] for h in range(g * GROUP_B, (g + 1) * GROUP_B)], axis=0)


def _sink_column(sink_ref, g):
    return jnp.concatenate([jnp.full((BQ_B, 1), sink_ref[h], F32) for h in range(g * GROUP_B, (g + 1) * GROUP_B)],
                           axis=0)


def _unstack(stacked):
    return [s[j * BQ_B:(j + 1) * BQ_B] for s in stacked for j in range(GROUP_B)]


def _attn_b_fwd(name, qb, kb, vb, sink):
    def body(sink_ref, q_ref, k_ref, v_ref, o_ref, lse_ref):
        start, valid = _window_b(pl.program_id(0))
        valid = jnp.concatenate([valid] * GROUP_B, axis=0)
        kw, vw = k_ref[pl.ds(start, SPAN_B), :], v_ref[pl.ds(start, SPAN_B), :]
        outs = _heads_fwd([dict(q=_stack_group(q_ref, g), k=kw[:, g * HD:(g + 1) * HD], v=vw[:, g * HD:(g + 1) * HD],
                                valid=valid, sink=_sink_column(sink_ref, g)) for g in range(HKV)])
        o_ref[...] = jnp.concatenate(_unstack([o for o, _ in outs]), axis=1)
        lse_ref[...] = _per_head(_unstack([lse for _, lse in outs]))

    qs = pl.BlockSpec((BQ_B, WB), lambda i: (i, 0))
    return pl.pallas_call(
        body, name=name, grid=(T // BQ_B,),
        in_specs=[pl.BlockSpec(memory_space=pltpu.SMEM), qs, _whole((T, WKV)), _whole((T, WKV))],
        out_specs=[qs, qs],
        out_shape=[jax.ShapeDtypeStruct((T, WB), F32)] * 2, compiler_params=_params(("parallel",)),
    )(sink, qb, kb, vb)


def _attn_b_bwd(name, qb, kb, vb, ob, lse, dob, sink):
    def body(sink_ref, q_ref, k_ref, v_ref, o_ref, lse_ref, do_ref, dq_ref, dk_ref, dv_ref, dsink_ref):
        i = pl.program_id(0)
        start, valid = _window_b(i)
        valid = jnp.concatenate([valid] * GROUP_B, axis=0)
        kw, vw = k_ref[pl.ds(start, SPAN_B), :], v_ref[pl.ds(start, SPAN_B), :]
        res = _heads_bwd([dict(q=_stack_group(q_ref, g), k=kw[:, g * HD:(g + 1) * HD], v=vw[:, g * HD:(g + 1) * HD],
                               o=_stack_group(o_ref, g), do=_stack_group(do_ref, g),
                               lse=jnp.concatenate([lse_ref[:, h * HD:h * HD + 1]
                                                    for h in range(g * GROUP_B, (g + 1) * GROUP_B)], axis=0),
                               valid=valid, sink=_sink_column(sink_ref, g)) for g in range(HKV)])
        dks, dvs = [r[1] for r in res], [r[2] for r in res]
        lane = lax.broadcasted_iota(jnp.int32, (1, LANE), 1)
        dsink = jnp.zeros((1, LANE), F32)
        for h, rows in enumerate(_unstack([r[4] for r in res])):
            dsink += jnp.where(lane == h, jnp.sum(rows), 0.0)
        dq_ref[...] = jnp.concatenate(_unstack([r[0] for r in res]), axis=1)

        @pl.when(i == 0)
        def _():
            dk_ref[...] = jnp.zeros_like(dk_ref)
            dv_ref[...] = jnp.zeros_like(dv_ref)
            dsink_ref[...] = jnp.zeros_like(dsink_ref)

        dk_ref[pl.ds(start, SPAN_B), :] += jnp.concatenate(dks, axis=1)
        dv_ref[pl.ds(start, SPAN_B), :] += jnp.concatenate(dvs, axis=1)
        dsink_ref[...] += dsink

    qs = pl.BlockSpec((BQ_B, WB), lambda i: (i, 0))
    return pl.pallas_call(
        body, name=name, grid=(T // BQ_B,),
        in_specs=[pl.BlockSpec(memory_space=pltpu.SMEM), qs, _whole((T, WKV)), _whole((T, WKV)), qs, qs, qs],
        out_specs=[qs, _whole((T, WKV)), _whole((T, WKV)), _whole((1, LANE))],
        out_shape=[jax.ShapeDtypeStruct((T, WB), F32), jax.ShapeDtypeStruct((T, WKV), F32),
                   jax.ShapeDtypeStruct((T, WKV), F32), jax.ShapeDtypeStruct((1, LANE), F32)],
        compiler_params=_params(("arbitrary",)),
    )(sink, qb, kb, vb, ob, lse, dob)


SPAN_C = NA_ROWS * GRID_W


def _row_start(r):
    return jnp.clip(r - NA_ROWS // 2, 0, ROWS - NA_ROWS)


def _off_index(r):
    return _row_start(r) - r + (NA_ROWS - 1)


N_TAB = 16
RPS_FWD, RPS_BWD = 4, 8


def _rpb_tables(name, rpb):
    circ = jnp.concatenate([rpb[..., NA_COLS - 1:], jnp.zeros(rpb.shape[:2] + (LANE - (2 * NA_COLS - 1),), F32),
                            rpb[..., :NA_COLS - 1]], axis=-1)
    circ = jnp.pad(circ, ((0, 0), (0, N_TAB + 1 - circ.shape[1]), (0, 0)))

    def body(w_ref, o_ref):
        c = lax.broadcasted_iota(jnp.int32, (GRID_W, LANE), 0)
        lane = lax.broadcasted_iota(jnp.int32, (GRID_W, LANE), 1)
        cs = jnp.clip(c - NA_COLS // 2, 0, GRID_W - NA_COLS)
        valid = (lane % GRID_W >= cs) & (lane % GRID_W < cs + NA_COLS)
        toep = [pltpu.roll(jnp.broadcast_to(w_ref[a:a + 1, :], (GRID_W, LANE)), 0, 1, stride=1, stride_axis=0)
                for a in range(N_TAB + 1)]
        for a in range(N_TAB):
            pair = jnp.where(lane < GRID_W, toep[a], pltpu.roll(toep[a + 1], GRID_W, 1))
            o_ref[a] = jnp.where(valid, pair, NEG)

    return pl.pallas_call(
        body, name=name, grid=(HC,),
        in_specs=[pl.BlockSpec((None, N_TAB + 1, LANE), lambda h: (h, 0, 0))],
        out_specs=pl.BlockSpec((None, N_TAB, GRID_W, LANE), lambda h: (h, 0, 0, 0)),
        out_shape=jax.ShapeDtypeStruct((HC, N_TAB, GRID_W, LANE), F32), compiler_params=_params(("parallel",)),
    )(circ)


def _bias_c(t_ref, h, d):
    return jnp.concatenate([t_ref[h, d + k] for k in range(0, NA_ROWS, 2)], axis=1)


def _attn_c_fwd(name, qkv, tables):
    RPS = RPS_FWD

    def body(q_ref, k_ref, v_ref, t_ref, o_ref, lse_ref):
        heads = []
        for rr in range(RPS):
            r = pl.program_id(1) * RPS + rr
            rows = slice(rr * GRID_W, (rr + 1) * GRID_W)
            start = pl.multiple_of(_row_start(r) * GRID_W, GRID_W)
            kw, vw = k_ref[pl.ds(start, SPAN_C), :], v_ref[pl.ds(start, SPAN_C), :]
            heads += [dict(q=q_ref[rows, h * HD:(h + 1) * HD], k=kw[:, h * HD:(h + 1) * HD], v=vw[:, h * HD:(h + 1) * HD],
                           bias=_bias_c(t_ref, h, _off_index(r))) for h in range(2)]
        outs = _heads_fwd(heads)
        for rr in range(RPS):
            rows = slice(rr * GRID_W, (rr + 1) * GRID_W)
            o_ref[rows, :] = jnp.concatenate([o for o, _ in outs[2 * rr:2 * rr + 2]], axis=1)
            lse_ref[rows, :] = _per_head([lse for _, lse in outs[2 * rr:2 * rr + 2]])

    qs = pl.BlockSpec((RPS * GRID_W, LANE), lambda p, r: (r, p))
    ks = pl.BlockSpec((T, LANE), lambda p, r: (0, p))
    ts = pl.BlockSpec((2, N_TAB, GRID_W, LANE), lambda p, r: (p, 0, 0, 0))
    return pl.pallas_call(
        body, name=name, grid=(HC // 2, ROWS // RPS),
        in_specs=[_qkv_rows(RPS * GRID_W, "qc"), _qkv_all("kc"), _qkv_all("vc"), ts], out_specs=[qs, qs],
        out_shape=[jax.ShapeDtypeStruct((T, WC), F32)] * 2, compiler_params=_params(("parallel", "parallel")),
    )(qkv, qkv, qkv, tables)


def _attn_c_bwd(name, qkv, oc, lse, doc, tables):
    RPS = RPS_BWD

    def body(q_ref, k_ref, v_ref, o_ref, lse_ref, do_ref, t_ref, dq_ref, dk_ref, dv_ref, dt_ref):
        @pl.when(pl.program_id(1) == 0)
        def _():
            dk_ref[...] = jnp.zeros_like(dk_ref)
            dv_ref[...] = jnp.zeros_like(dv_ref)
            dt_ref[...] = jnp.zeros_like(dt_ref)

        heads, where = [], []
        for rr in range(RPS):
            r = pl.program_id(1) * RPS + rr
            rows = slice(rr * GRID_W, (rr + 1) * GRID_W)
            d = _off_index(r)
            start = pl.multiple_of(_row_start(r) * GRID_W, GRID_W)
            kw, vw = k_ref[pl.ds(start, SPAN_C), :], v_ref[pl.ds(start, SPAN_C), :]
            where.append((rows, d, start))
            for h in range(2):
                sl = slice(h * HD, (h + 1) * HD)
                heads.append(dict(q=q_ref[rows, sl], k=kw[:, sl], v=vw[:, sl], o=o_ref[rows, sl], do=do_ref[rows, sl],
                                  lse=lse_ref[rows, h * HD:h * HD + 1], bias=_bias_c(t_ref, h, d)))
        res = _heads_bwd(heads)
        for rr, (rows, d, start) in enumerate(where):
            pair = res[2 * rr:2 * rr + 2]
            for h in range(2):
                for k in range(0, NA_ROWS, 2):
                    dt_ref[h, d + k] += pair[h][3][:, k * GRID_W:(k + 2) * GRID_W]
            dq_ref[rows, :] = jnp.concatenate([p[0] for p in pair], axis=1)
            dk_ref[pl.ds(start, SPAN_C), :] += jnp.concatenate([p[1] for p in pair], axis=1)
            dv_ref[pl.ds(start, SPAN_C), :] += jnp.concatenate([p[2] for p in pair], axis=1)

    qs = pl.BlockSpec((RPS * GRID_W, LANE), lambda p, r: (r, p))
    ks = pl.BlockSpec((T, LANE), lambda p, r: (0, p))
    ts = pl.BlockSpec((2, N_TAB, GRID_W, LANE), lambda p, r: (p, 0, 0, 0))
    return pl.pallas_call(
        body, name=name, grid=(HC // 2, ROWS // RPS),
        in_specs=[_qkv_rows(RPS * GRID_W, "qc"), _qkv_all("kc"), _qkv_all("vc"), qs, qs, qs, ts],
        out_specs=[qs, ks, ks, ts],
        out_shape=[jax.ShapeDtypeStruct((T, WC), F32)] * 3 + [jax.ShapeDtypeStruct((HC, N_TAB, GRID_W, LANE), F32)],
        compiler_params=_params(("parallel", "arbitrary")),
    )(qkv, qkv, qkv, oc, lse, doc, tables)


def _split3(v):
    hi = v.astype(BF16)
    r1 = v - hi.astype(F32)
    mid = r1.astype(BF16)
    lo = (r1 - mid.astype(F32)).astype(BF16)
    return hi, mid, lo


def _rpb_reduce(name, dtables):
    x = dtables.reshape(HC, N_TAB, GRID_W * LANE)
    c = jnp.arange(GRID_W)[:, None]
    lane = jnp.arange(LANE)[None, :]
    col = (lane // GRID_W) * LANE + jnp.clip(lane % GRID_W - c + (NA_COLS - 1), 0, 2 * NA_COLS - 2)
    col_onehot = (col.reshape(-1)[:, None] == jnp.arange(2 * LANE)[None, :]).astype(BF16)
    a2 = jnp.arange(N_TAB)[None, :]
    row_onehot = jnp.concatenate([(jnp.arange(16)[:, None] == a2 + u) & (a2 < 2 * NA_ROWS - 2) for u in range(2)],
                                 axis=1).astype(BF16)

    def body(x_ref, e_ref, f_ref, o_ref):
        y = sum(jnp.dot(part, e_ref[...], preferred_element_type=F32) for part in _split3(x_ref[...]))
        z = jnp.concatenate([y[:, :LANE], y[:, LANE:]], axis=0)
        o_ref[...] = sum(jnp.dot(f_ref[...], part, preferred_element_type=F32) for part in _split3(z))

    out = pl.pallas_call(
        body, name=name, grid=(HC,),
        in_specs=[pl.BlockSpec((None, N_TAB, GRID_W * LANE), lambda h: (h, 0, 0)),
                  _whole((GRID_W * LANE, 2 * LANE)), _whole((16, 2 * N_TAB))],
        out_specs=pl.BlockSpec((None, 16, LANE), lambda h: (h, 0, 0)),
        out_shape=jax.ShapeDtypeStruct((HC, 16, LANE), F32), compiler_params=_params(("parallel",)),
    )(x, col_onehot, row_onehot)
    return out[:, :2 * NA_ROWS - 1, :2 * NA_COLS - 1]


TC = 128
CHUNK = 128
MARGIN = 8


FWD_BLOCKS = 4
BWD_BLOCKS = 1


def _ffn_mid_fwd(name, up, conv_w, conv_b):
    wide = FWD_BLOCKS * TC

    window = CHUNK + 2 * MARGIN
    centre = slice(MARGIN, MARGIN + CHUNK)
    last = T // CHUNK - 1

    def body(xg_ref, xv_ref, wg_ref, wv_ref, bg_ref, bv_ref, o_ref):
        margin = jnp.zeros((MARGIN, TC), F32)
        for b in range(FWD_BLOCKS):
            lanes = slice(b * TC, (b + 1) * TC)
            wg, wv, bg, bv = wg_ref[:, lanes], wv_ref[:, lanes], bg_ref[:, lanes], bv_ref[:, lanes]

            def windows(c, b=b):
                if isinstance(c, int) and c == 0:
                    rows = slice(0, CHUNK + MARGIN)
                    return [jnp.concatenate([margin, v], axis=0) for v in (xg_ref[b, rows, :], xv_ref[b, rows, :])]
                if isinstance(c, int) and c == last:
                    rows = slice(T - CHUNK - MARGIN, T)
                    return [jnp.concatenate([v, margin], axis=0) for v in (xg_ref[b, rows, :], xv_ref[b, rows, :])]
                rows = pl.ds(pl.multiple_of(c * CHUNK - MARGIN, MARGIN), window)
                return [xg_ref[b, rows, :], xv_ref[b, rows, :]]

            def chunk(c, carry, lanes=lanes, wg=wg, wv=wv, bg=bg, bv=bv, windows=windows):
                r0 = c * CHUNK if isinstance(c, int) else pl.multiple_of(c * CHUNK, CHUNK)
                xg, xv = windows(c)
                ug = pltpu.roll(xg, 1, 0) * wg[0:1] + xg * wg[1:2] + pltpu.roll(xg, window - 1, 0) * wg[2:3] + bg
                uv = pltpu.roll(xv, 1, 0) * wv[0:1] + xv * wv[1:2] + pltpu.roll(xv, window - 1, 0) * wv[2:3] + bv
                o_ref[pl.ds(r0, CHUNK), lanes] = (ug * jax.nn.sigmoid(ug) * uv)[centre].astype(BF16)
                return carry

            chunk(0, 0)
            lax.fori_loop(1, last, chunk, 0)
            chunk(last, 0)

    gate = lambda shape: pl.BlockSpec(shape, lambda j: (0, j))
    val = lambda shape: pl.BlockSpec(shape, lambda j: (0, j + DFF // wide))
    return pl.pallas_call(
        body, name=name, grid=(DFF // wide,),
        in_specs=[pl.BlockSpec((FWD_BLOCKS, T, TC), lambda j: (j, 0, 0)),
                  pl.BlockSpec((FWD_BLOCKS, T, TC), lambda j: (j + DFF // wide, 0, 0)),
                  gate((3, wide)), val((3, wide)), gate((1, wide)), val((1, wide))],
        out_specs=pl.BlockSpec((T, wide), lambda j: (0, j)),
        out_shape=jax.ShapeDtypeStruct((T, DFF), BF16), compiler_params=_params(("parallel",)),
    )(up, up, conv_w, conv_w, conv_b, conv_b)


def _ffn_mid_bwd(name, dact, up, conv_w, conv_b):
    window = CHUNK + 2 * MARGIN
    centre = slice(MARGIN, MARGIN + CHUNK)

    def shifted(v):
        return pltpu.roll(v, 1, 0), pltpu.roll(v, window - 1, 0)

    def fold(v):
        return jnp.sum(v[centre].reshape(CHUNK // 8, 8, TC), axis=0)

    wide = BWD_BLOCKS * TC

    def body(da_ref, xg_ref, xv_ref, wg_ref, wv_ref, bg_ref, bv_ref, dx_ref, dw_ref, db_ref):
        for b in range(BWD_BLOCKS):
            block(b, da_ref, xg_ref, xv_ref, wg_ref, wv_ref, bg_ref, bv_ref, dx_ref, dw_ref, db_ref)

    def block(b, da_ref, xg_ref, xv_ref, wg_ref, wv_ref, bg_ref, bv_ref, dx_ref, dw_ref, db_ref):
        lanes = slice(b * TC, (b + 1) * TC)
        wg, wv, bg, bv = wg_ref[:, lanes], wv_ref[:, lanes], bg_ref[:, lanes], bv_ref[:, lanes]
        margin = jnp.zeros((MARGIN, TC), F32)
        last = T // CHUNK - 1

        def windows(c):
            if isinstance(c, int) and c == 0:
                rows = slice(0, CHUNK + MARGIN)
                return [jnp.concatenate([margin, v], axis=0)
                        for v in (da_ref[rows, lanes], xg_ref[b, rows, :], xv_ref[b, rows, :])]
            if isinstance(c, int) and c == last:
                rows = slice(T - CHUNK - MARGIN, T)
                return [jnp.concatenate([v, margin], axis=0)
                        for v in (da_ref[rows, lanes], xg_ref[b, rows, :], xv_ref[b, rows, :])]
            rows = pl.ds(pl.multiple_of(c * CHUNK - MARGIN, MARGIN), window)
            return [da_ref[rows, lanes], xg_ref[b, rows, :], xv_ref[b, rows, :]]

        def chunk(c, sums):
            r0 = c * CHUNK if isinstance(c, int) else pl.multiple_of(c * CHUNK, CHUNK)
            da, xg, xv = windows(c)
            xg_prev, xg_next = shifted(xg)
            xv_prev, xv_next = shifted(xv)
            ug = xg_prev * wg[0:1] + xg * wg[1:2] + xg_next * wg[2:3] + bg
            uv = xv_prev * wv[0:1] + xv * wv[1:2] + xv_next * wv[2:3] + bv
            sg = jax.nn.sigmoid(ug)
            dug = da * uv * (sg * (1.0 + ug * (1.0 - sg)))
            duv = da * (ug * sg)
            out = []
            for half, (x_prev, x, x_next, w, du) in enumerate(((xg_prev, xg, xg_next, wg, dug),
                                                               (xv_prev, xv, xv_next, wv, duv))):
                du_prev, du_next = shifted(du)
                dx = du_next * w[0:1] + du * w[1:2] + du_prev * w[2:3]
                dx_ref[half, pl.ds(r0, CHUNK), lanes] = dx[centre].astype(BF16)
                out += [fold(x_prev * du), fold(x * du), fold(x_next * du), fold(du)]
            return tuple(s + o for s, o in zip(sums, out))

        sums = chunk(0, tuple(jnp.zeros((8, TC), F32) for _ in range(8)))
        sums = lax.fori_loop(1, last, chunk, sums)
        sums = chunk(last, sums)
        rows = [jnp.sum(s, axis=0, keepdims=True) for s in sums]
        for half in range(2):
            dw_ref[half, :, lanes] = jnp.concatenate(rows[4 * half:4 * half + 3], axis=0)
            db_ref[half, :, lanes] = rows[4 * half + 3]

    gate = lambda shape: pl.BlockSpec(shape, lambda j: (0, j))
    val = lambda shape: pl.BlockSpec(shape, lambda j: (0, j + DFF // wide))
    return pl.pallas_call(
        body, name=name, grid=(DFF // wide,),
        in_specs=[gate((T, wide)), pl.BlockSpec((BWD_BLOCKS, T, TC), lambda j: (j, 0, 0)),
                  pl.BlockSpec((BWD_BLOCKS, T, TC), lambda j: (j + DFF // wide, 0, 0)),
                  gate((3, wide)), val((3, wide)), gate((1, wide)), val((1, wide))],
        out_specs=[pl.BlockSpec((2, T, wide), lambda j: (0, 0, j)), pl.BlockSpec((2, 3, wide), lambda j: (0, 0, j)),
                   pl.BlockSpec((2, 1, wide), lambda j: (0, 0, j))],
        out_shape=[jax.ShapeDtypeStruct((2, T, DFF), BF16), jax.ShapeDtypeStruct((2, 3, DFF), F32),
                   jax.ShapeDtypeStruct((2, 1, DFF), F32)],
        compiler_params=_params(("parallel",)),
    )(dact, up, up, conv_w, conv_w, conv_b, conv_b)


def _dup_spec(tm, nj):
    per = DFF // nj
    return pl.BlockSpec((None, tm, nj), lambda a, b, j: (j // per, 0 if tm == T else b, j % per))


def _dup_spec_tn(tm, nj):
    per = DFF // nj
    return pl.BlockSpec((None, tm, nj), lambda j, kt, r: (j // per, 0, j % per))


def _adamw_math(w, g, m, v):
    m = ADAM_B1 * m + (1.0 - ADAM_B1) * g
    v = ADAM_B2 * v + (1.0 - ADAM_B2) * (g * g)
    m_hat = m / (1.0 - ADAM_B1 ** ADAM_STEP)
    v_hat = v / (1.0 - ADAM_B2 ** ADAM_STEP)
    delta = -ADAM_LR * (m_hat / (jnp.sqrt(v_hat) + ADAM_EPS) + ADAM_WD * w)
    return delta, m, v


ADAM_BLOCK = 256 * 1408


def _adamw_sharded(name, w, m, v, parts):
    _, r, c = w.shape
    tr = max(t for t in range(16, r + 1, 16) if r % t == 0 and t * c <= ADAM_BLOCK)

    def body(w_ref, m_ref, v_ref, p0_ref, p1_ref, g_ref, d_ref, nm_ref, nv_ref):
        def run(p_ref):
            g = p_ref[0].astype(F32)
            for k in range(1, N_DEV):
                g = g + p_ref[k].astype(F32)
            d, nm, nv = _adamw_math(w_ref[...], g, m_ref[...], v_ref[...])
            g_ref[...] = g
            d_ref[...] = d
            nm_ref[...] = nm
            nv_ref[...] = nv

        @pl.when(pl.program_id(0) == 0)
        def _():
            run(p0_ref)

        @pl.when(pl.program_id(0) == 1)
        def _():
            run(p1_ref)

    ws = pl.BlockSpec((None, tr, c), lambda l, i: (l, i, 0))
    p0 = pl.BlockSpec((N_DEV, tr, c), lambda l, i: (0, jnp.where(l == 0, i, r // tr - 1), 0))
    p1 = pl.BlockSpec((N_DEV, tr, c), lambda l, i: (0, jnp.where(l == 1, i, 0), 0))
    return pl.pallas_call(
        body, name=name, grid=(DEPTH, r // tr), in_specs=[ws, ws, ws, p0, p1], out_specs=[ws] * 4,
        out_shape=[jax.ShapeDtypeStruct(w.shape, F32)] * 4, compiler_params=_params(("arbitrary", "arbitrary")),
    )(w, m, v, *parts)


def _sum_devices(name, parts):
    r = parts.shape[1]

    def body(p_ref, o_ref):
        g = p_ref[0]
        for k in range(1, N_DEV):
            g = g + p_ref[k]
        o_ref[...] = g

    return pl.pallas_call(
        body, name=name, in_specs=[pl.BlockSpec((N_DEV, r, LANE), lambda: (0, 0, 0))],
        out_specs=pl.BlockSpec((r, LANE), lambda: (0, 0)), out_shape=jax.ShapeDtypeStruct((r, LANE), F32),
        compiler_params=_params(),
    )(parts)


def _adamw_small(name, ws, gs, ms, vs):
    n = len(ws)
    shapes = [w.shape for w in ws]
    ws, gs, ms, vs = ([a.reshape(1, -1) if a.ndim == 1 else a for a in arrs] for arrs in (ws, gs, ms, vs))
    specs = [pl.BlockSpec(memory_space=pltpu.VMEM)] * n

    def body(*refs):
        for i in range(n):
            w_ref, g_ref, m_ref, v_ref = (refs[k * n + i] for k in range(4))
            d, nm, nv = _adamw_math(w_ref[...], g_ref[...], m_ref[...], v_ref[...])
            refs[4 * n + i][...] = d
            refs[5 * n + i][...] = nm
            refs[6 * n + i][...] = nv

    outs = pl.pallas_call(
        body, name=name, in_specs=specs * 4, out_specs=specs * 3,
        out_shape=[jax.ShapeDtypeStruct(w.shape, F32) for w in ws] * 3, compiler_params=_params(),
    )(*ws, *gs, *ms, *vs)
    outs = [o.reshape(shapes[i % n]) for i, o in enumerate(outs)]
    return outs[:n], outs[n:2 * n], outs[2 * n:]


def _pack(arrays):
    flat = jnp.concatenate([a.reshape(-1) for a in arrays])
    pad = (-flat.shape[0]) % (8 * LANE)
    return jnp.pad(flat, (0, pad)).reshape(-1, LANE)


def _unpack(buf, shapes):
    flat, out, off = buf.reshape(-1), [], 0
    for s in shapes:
        n = 1
        for d in s:
            n *= d
        out.append(flat[off:off + n].reshape(s))
        off += n
    return out


def _local_step(x, target, small, weights, conv_w_full, hand_over, used):
    cos2, sin2 = _rope_tables()
    bias_a = _dilation_bias()
    tables = [_rpb_tables(f"rpb_tables_{l}", small["rpb_c"][l]) for l in range(DEPTH)]
    saved, carry = [], 0.0
    for l in range(DEPTH):
        g1, g2 = small["ln_attn"][l][None] + carry, small["ln_ffn"][l][None]
        gain, sink, cb = small["mix_gain"][l][None], small["sink_b"][l], small["conv_b"][l][None]
        cw = conv_w_full[l]
        bias = tables[l]
        h1, qkv = _prologue_matmul(f"proj_in_{l}", _rmsnorm_rows, [x, g1], [D, None],
                                   weights("w_in", l, [cos2, sin2, bias_a] + tables if l == 0 else x),
                                   (D, 1024), lambda j: (0, j), 1024, epilogue=_rope_epilogue, extras=(cos2, sin2),
                                   out_dtype=BF16)
        zero = used(f"proj_in_{l}", qkv)
        qb, kb, vb = (qkv[:, BLOCK_OF[n] * LANE:BLOCK_OF[n] * LANE + w] for n, w in (("qb", WB), ("kb", WKV), ("vb", WKV)))
        oa, lse_a = _attn_a_fwd(f"attn_a_{l}", qkv, bias_a)
        ob, lse_b = _attn_b_fwd(f"attn_b_{l}", qb, kb, vb, sink + zero)
        oc, lse_c = _attn_c_fwd(f"attn_c_{l}", qkv, bias)
        mixed, x_mid = _prologue_matmul(f"proj_out_{l}", _mix_rows, [oa, ob, oc, gain + used(f"attn_{l}", oc)],
                                        [WA, WB, WC, None],
                                        weights("w_out", l, oc), (N_DEV, D // N_DEV, 512), lambda j: (0, 0, j), 512,
                                        res=x)
        h2, up = _prologue_matmul(f"ffn_up_{l}", _rmsnorm_rows, [x_mid, g2 + used(f"proj_out_{l}", x_mid)], [D, None],
                                  weights("w_up", l, x_mid), (D, 1024), lambda j: (0, j), 1024, blocked_out=True)
        act = _ffn_mid_fwd(f"ffn_mid_{l}", up, cw, cb + used(f"ffn_up_{l}", up))
        x_out = _nn_rows(f"ffn_down_{l}", act, weights("w_down", l, act), x_mid, 4, 1024, 1024)
        carry = used(f"ffn_down_{l}", x_out)
        saved.append(dict(x=x, h1=h1, qkv=(qkv, qb, kb, vb), o=(oa, ob, oc), lse=(lse_a, lse_b, lse_c), mixed=mixed,
                          x_mid=x_mid, h2=h2, up=up, act=act, g1=g1, g2=g2, gain=gain, sink=sink, cb=cb, cw=cw, bias=bias))
        x = x_out

    loss8, dx, dxb, d_ln_final = _loss_head(x, small["ln_final"][None], target)
    sgrads = [None] * DEPTH
    for l in reversed(range(DEPTH)):
        s = saved[l]
        qkv, qb, kb, vb = s["qkv"]
        oa, ob, oc = s["o"]
        wg_in, wg_out = weights("w_in", l, None), weights("w_out", l, None)
        wg_up, wg_down = weights("w_up", l, None), weights("w_down", l, None)
        g_down = _tn_rows(f"wgrad_down_{l}", s["act"], dxb, wg_down.shape[1], 2, 512)
        zero = hand_over("w_down", l, g_down)
        dact = _nt_rows(f"dgrad_down_{l}", dxb, wg_down, 4, 512)
        dup, d_cw, d_cb = _ffn_mid_bwd(f"ffn_mid_bwd_{l}", dact, s["up"], s["cw"], s["cb"] + zero)
        g_up = _tn_cols(f"wgrad_up_{l}", s["h2"], dup, _dup_spec_tn, 2 * DFF, DFF // 2)
        zero = hand_over("w_up", l, g_up)
        dh2 = _nt_cols(f"dgrad_up_{l}", dup, _dup_spec, wg_up, DFF // 2)
        dx, dxb, d_g2 = _rmsnorm_bwd(f"norm_ffn_bwd_{l}", dh2, s["x_mid"], s["g2"] + zero, dx)
        g_out = _tn_rows(f"wgrad_out_{l}", s["mixed"], dxb, wg_out.shape[1], 2, D)
        zero = hand_over("w_out", l, g_out)
        dmixed = _nt_rows(f"dgrad_out_{l}", dxb, wg_out, 2, T)
        doa, dob, doc, d_gain = _mix_bwd(f"mix_bwd_{l}", dmixed, oa, ob, oc, s["gain"] + zero)
        lse_a, lse_b, lse_c = s["lse"]
        dqa, dka, dva = _attn_a_bwd(f"attn_a_bwd_{l}", qkv, oa, lse_a, doa, bias_a)
        dqb, dkb, dvb, d_sink = _attn_b_bwd(f"attn_b_bwd_{l}", qb, kb, vb, ob, lse_b, dob, s["sink"])
        dqc, dkc, dvc, d_bias = _attn_c_bwd(f"attn_c_bwd_{l}", qkv, oc, lse_c, doc, s["bias"])
        d_rpb = _rpb_reduce(f"rpb_reduce_{l}", d_bias)
        dproj = _rope_bwd(f"rope_bwd_{l}", (dqa, dka, dva, dqb, dkb, dvb, dqc, dkc, dvc), cos2, sin2)
        g_in = _tn_cols(f"wgrad_in_{l}", s["h1"], dproj,
                        lambda tm, tn: pl.BlockSpec((tm, tn), lambda j, kt, r: (0, j)), IN_COLS, 1024)
        zero = hand_over("w_in", l, g_in)
        dh1 = _nt_cols(f"dgrad_in_{l}", dproj, lambda tm, nc: pl.BlockSpec((tm, nc), lambda kt, i, j: (i, j)), wg_in,
                       IN_COLS // 2)
        dx, dxb, d_g1 = _rmsnorm_bwd(f"norm_attn_bwd_{l}", dh1, s["x"], s["g1"] + zero, dx)
        sgrads[l] = dict(ln_attn=d_g1[0], sink_b=d_sink[0, :HB], rpb_c=d_rpb, mix_gain=d_gain[0], ln_ffn=d_g2[0],
                         conv_w=d_cw.transpose(1, 0, 2).reshape(3, 2 * DFF), conv_b=d_cb.reshape(2 * DFF))
    return loss8[0, 0], dx, d_ln_final[0], sgrads


SMALL_NAMES = ("ln_attn", "sink_b", "rpb_c", "mix_gain", "ln_ffn", "conv_b")


def kernel(x, ln_attn, w_in, sink_b, rpb_c, mix_gain, w_out, ln_ffn, w_up, conv_w, conv_b, w_down, ln_final, loss_target, m_ln_attn, m_w_in, m_sink_b, m_rpb_c, m_mix_gain, m_w_out, m_ln_ffn, m_w_up, m_conv_w, m_conv_b, m_w_down, m_ln_final, v_ln_attn, v_w_in, v_sink_b, v_rpb_c, v_mix_gain, v_w_out, v_ln_ffn, v_w_up, v_conv_w, v_conv_b, v_w_down, v_ln_final):
    me = 4 * lax.axis_index("x") + 2 * lax.axis_index("y") + lax.axis_index("c")
    small = dict(ln_attn=ln_attn, sink_b=sink_b, rpb_c=rpb_c, mix_gain=mix_gain, ln_ffn=ln_ffn, conv_b=conv_b,
                 ln_final=ln_final)

    names = ("w_in", "w_out", "w_up", "w_down")
    shards = dict(w_in=w_in, w_out=w_out, w_up=w_up, w_down=w_down)
    order = [(n, l) for l in range(DEPTH) for n in names]
    conv_key = ("conv_w", 0)
    started, arrived, forwarded, gathered = {}, {}, {}, {}

    def side_by_side(k):
        return k[0] in ("w_in", "w_up")

    def slot_of(k):
        return _col_slot(shards[k[0]].shape[2]) if side_by_side(k) else _lead_slot

    def begin(name, ks, zero):
        srcs = [_pack([conv_w]) + zero if k == conv_key else (shards[k[0]][k[1]] + zero).astype(BF16) for k in ks]
        lands = [lax.empty((s.shape[0], N_DEV * s.shape[1]) if side_by_side(k) else (N_DEV,) + s.shape, s.dtype)
                 for k, s in zip(ks, srcs)]
        peers = [ALL_PEERS if k == conv_key else NEAR_PEERS for k in ks]
        send, recv, bufs, tok = _copy_start(name, srcs + lands, _gather_plan(peers, [slot_of(k) for k in ks]),
                                            [len(p) + 1 for p in peers])
        for i, k in enumerate(ks):
            started[k] = (send[i], recv[i], bufs[i], bufs[len(ks) + i], peers[i])
        return tok

    token = begin("gather_start_first", order[:1], 0.0)
    token = begin("gather_start_rest", [conv_key] + order[1:], token[0, 0])

    def arrive(k, after):
        send, recv, src, land, peers = started[k]
        arrived[k] = _copy_wait(f"gather_{k[0]}_{k[1]}_arrived", [src, land], [send], [recv],
                                _gather_plan([peers], [slot_of(k)]), after)

    queue = list(order)

    def advance(after):
        if not queue:
            return 0.0
        k = queue.pop(0)
        arrive(k, after)
        forwarded[k] = _copy_start(f"gather_{k[0]}_{k[1]}_forward", [arrived[k][1]], _forward_plan(slot_of(k)),
                                   [len(OTHER_CHIPS)])
        return forwarded[k][3][0, 0]

    pass_on_behind = ("proj_in_0", "attn_0", "ffn_up_0", "ffn_down_0", "proj_in_1", "attn_1", "ffn_up_1")

    def used(point, result):
        return advance(result) if point in pass_on_behind else 0.0

    def weights(n, l, after):
        k = (n, l)
        if k not in gathered:
            if k not in forwarded:
                advance(after)
            send_b, recv_b, (land,), _ = forwarded[k]
            (gathered[k],) = _copy_wait(f"gather_{n}_{l}_done", [land], send_b, recv_b, _forward_plan(slot_of(k)),
                                        after)
        return gathered[k]

    pending = {}

    def hand_over(n, l, g):
        shard = shards[n].shape[1:]
        send, recv, bufs, tok = _copy_start(f"send_grad_{n}_{l}", [g, lax.empty((N_DEV,) + shard, g.dtype)],
                                            _scatter_plan(slot_of((n, l))), [len(ALL_PEERS) + 1])
        pending[(n, l)] = (send, recv, bufs)
        return tok[0, 0]

    def received(k, after):
        send, recv, bufs = pending[k]
        return _copy_wait(f"recv_grad_{k[0]}_{k[1]}", bufs, send, recv, _scatter_plan(slot_of(k)), after)[1]

    arrive(conv_key, token)
    cw_all = arrived[conv_key][1]
    nup = w_up.shape[2]
    cw_shards = cw_all.reshape(N_DEV, -1)[:, :DEPTH * 3 * nup].reshape(N_DEV, DEPTH, 3, nup)
    conv_w_full = cw_shards.transpose(1, 2, 0, 3).reshape(DEPTH, 3, N_DEV * nup)

    loss_local, dx, d_ln_final, sgrads = _local_step(
        x[0], loss_target[0], dict(small, ln_attn=ln_attn + token[0, 0]), weights, conv_w_full, hand_over, used)

    stacked = [jnp.stack([sgrads[l][n] for l in range(DEPTH)]) for n in SMALL_NAMES + ("conv_w",)] + [d_ln_final]
    shapes = [a.shape for a in stacked]
    mine = _pack(stacked)
    send_s, recv_s, bufs_s, _ = _copy_start("gather_small_grads_start", [mine, lax.empty((N_DEV,) + mine.shape, F32)],
                                            _gather_plan([ALL_PEERS], [_lead_slot]), [len(ALL_PEERS) + 1])

    big, after = {}, dx
    moments = dict(w_in=(m_w_in, v_w_in), w_out=(m_w_out, v_w_out), w_up=(m_w_up, v_w_up), w_down=(m_w_down, v_w_down))
    for n in reversed(names):
        parts = (received((n, 0), after), received((n, 1), after))
        big[n] = _adamw_sharded(f"adamw_{n}", shards[n], *moments[n], parts)
        after = big[n][1]

    _, everyone = _copy_wait("gather_small_grads_done", bufs_s, send_s, recv_s,
                             _gather_plan([ALL_PEERS], [_lead_slot]), after)
    g_small = _unpack(_sum_devices("sum_small_grads", everyone), shapes)
    g = dict(zip(SMALL_NAMES + ("conv_w", "ln_final"), g_small))
    g["conv_w"] = lax.dynamic_slice_in_dim(g["conv_w"], me * nup, nup, axis=2)

    snames = SMALL_NAMES + ("conv_w", "ln_final")
    sw = dict(small, conv_w=conv_w)
    sm = dict(ln_attn=m_ln_attn, sink_b=m_sink_b, rpb_c=m_rpb_c, mix_gain=m_mix_gain, ln_ffn=m_ln_ffn,
              conv_b=m_conv_b, conv_w=m_conv_w, ln_final=m_ln_final)
    sv = dict(ln_attn=v_ln_attn, sink_b=v_sink_b, rpb_c=v_rpb_c, mix_gain=v_mix_gain, ln_ffn=v_ln_ffn,
              conv_b=v_conv_b, conv_w=v_conv_w, ln_final=v_ln_final)
    s_delta, s_m, s_v = (dict(zip(snames, out)) for out in _adamw_small(
        "adamw_small", [sw[n] for n in snames], [g[n] for n in snames], [sm[n] for n in snames],
        [sv[n] for n in snames]))

    loss = lax.psum(loss_local, ("x", "y", "c"))
    outputs = ("ln_attn", "w_in", "sink_b", "rpb_c", "mix_gain", "w_out", "ln_ffn", "w_up", "conv_w", "conv_b",
               "w_down", "ln_final")
    grads = [big[n][0] if n in big else g[n] for n in outputs]
    deltas = [big[n][1] if n in big else s_delta[n] for n in outputs]
    new_m = [big[n][2] if n in big else s_m[n] for n in outputs]
    new_v = [big[n][3] if n in big else s_v[n] for n in outputs]
    return (loss, dx[None], *grads, *deltas, *new_m, *new_v)
```
